```python
import jax
import jax.numpy as jnp
from jax import lax

D_MODEL = 1024
BATCH = 16
SEQ = 2048
DEPTH = 1

N_POOL_GROUPS = 4
POOL_WINDOWS = (2, 4, 8, 16)
POOL_WIDTH = D_MODEL
POOL_GROUP_DIM = POOL_WIDTH // N_POOL_GROUPS
CONV_WIDTH = D_MODEL
CONV_KSIZE = 3
N_BRANCHES = 2
D_IN = POOL_WIDTH + 3 * CONV_WIDTH + N_BRANCHES * D_MODEL
D_FF = ((8 * D_MODEL // 3 + 127) // 128) * 128
FFN_KSIZE = 3
RMS_EPS = 1e-6

kernel_name = 'hybrid_pool_shortconv_gated_block'


def rms_norm(x, g):
    xf = x.astype(jnp.float32)
    inv = lax.rsqrt(jnp.mean(xf * xf, axis=-1, keepdims=True) + RMS_EPS)
    return (xf * inv * g.astype(jnp.float32)).astype(x.dtype)


def causal_depthwise_conv(u, w):
    k, c = w.shape
    return lax.conv_general_dilated(
        u, w.astype(u.dtype)[:, None, :], window_strides=(1,), padding=[(k - 1, 0)],
        dimension_numbers=('NWC', 'WIO', 'NWC'), feature_group_count=c)


def causal_multiscale_pool(u):
    s = u.shape[1]
    pos = jnp.arange(1, s + 1, dtype=jnp.float32)
    outs = []
    for gi, win in enumerate(POOL_WINDOWS):
        ug = u[..., gi * POOL_GROUP_DIM:(gi + 1) * POOL_GROUP_DIM].astype(jnp.float32)
        csum = jnp.pad(jnp.cumsum(ug, axis=1), ((0, 0), (1, 0), (0, 0)))
        upper = csum[:, 1:]
        lower = jnp.pad(csum[:, :s + 1 - win], ((0, 0), (win - 1, 0), (0, 0)))
        count = jnp.minimum(pos, float(win))[None, :, None]
        outs.append(((upper - lower) / count - ug).astype(u.dtype))
    return jnp.stack(outs, axis=2)


def token_mixer(h, w_in, pool_w, pool_scale, w_pool_proj, conv_w, w_conv_out, w_o):
    b, s, _ = h.shape
    z = jnp.einsum('bsd,de->bse', h, w_in)
    splits = [POOL_WIDTH,
              POOL_WIDTH + CONV_WIDTH,
              POOL_WIDTH + 2 * CONV_WIDTH,
              POOL_WIDTH + 3 * CONV_WIDTH,
              POOL_WIDTH + 3 * CONV_WIDTH + D_MODEL]
    z_pool, z_b, z_c, z_v, z_gpool, z_gconv = jnp.split(z, splits, axis=-1)
    p = causal_multiscale_pool(z_pool)
    p = jnp.einsum('bsgc,gce->bsge', p, pool_w).reshape(b, s, POOL_WIDTH) * pool_scale
    y_pool = jnp.einsum('bsp,pd->bsd', p, w_pool_proj)
    y_conv = jnp.einsum('bsc,cd->bsd', z_b * causal_depthwise_conv(z_c * z_v, conv_w), w_conv_out)
    merged = jax.nn.sigmoid(z_gpool) * y_pool + jax.nn.sigmoid(z_gconv) * y_conv
    return jnp.einsum('bsd,de->bse', merged, w_o)


def channel_mixer(h, w_up, ffn_conv_w, ffn_conv_b, w_down):
    u = causal_depthwise_conv(jnp.einsum('bsd,df->bsf', h, w_up), ffn_conv_w) + ffn_conv_b
    gate, val = jnp.split(u, 2, axis=-1)
    return jnp.einsum('bsf,fd->bsd', jax.nn.silu(gate) * val, w_down)


def _normal(k, shape, scale):
    return jax.random.normal(k, shape, jnp.float32) * scale


def _fwd_setup_inputs(seed: int = 0) -> dict:
    key = jax.random.key(seed)
    ks = jax.random.split(key, 15)
    return {
        'x': _normal(ks[0], (BATCH, SEQ, D_MODEL), 1.0),
        'norm_mix': 1.0 + _normal(ks[1], (DEPTH, D_MODEL), 0.1),
        'w_in': _normal(ks[2], (DEPTH, D_MODEL, D_IN), D_MODEL ** -0.5),
        'pool_w': _normal(ks[3], (DEPTH, N_POOL_GROUPS, POOL_GROUP_DIM, POOL_GROUP_DIM), POOL_GROUP_DIM ** -0.5),
        'pool_scale': 1.0 + _normal(ks[4], (DEPTH, POOL_WIDTH), 0.1),
        'w_pool_proj': _normal(ks[5], (DEPTH, POOL_WIDTH, D_MODEL), POOL_WIDTH ** -0.5),
        'conv_w': _normal(ks[6], (DEPTH, CONV_KSIZE, CONV_WIDTH), CONV_KSIZE ** -0.5),
        'w_conv_out': _normal(ks[7], (DEPTH, CONV_WIDTH, D_MODEL), CONV_WIDTH ** -0.5),
        'w_o': _normal(ks[8], (DEPTH, D_MODEL, D_MODEL), D_MODEL ** -0.5),
        'norm_ffn': 1.0 + _normal(ks[9], (DEPTH, D_MODEL), 0.1),
        'w_up': _normal(ks[10], (DEPTH, D_MODEL, 2 * D_FF), D_MODEL ** -0.5),
        'ffn_conv_w': _normal(ks[11], (DEPTH, FFN_KSIZE, 2 * D_FF), FFN_KSIZE ** -0.5),
        'ffn_conv_b': _normal(ks[12], (DEPTH, 2 * D_FF), 0.02),
        'w_down': _normal(ks[13], (DEPTH, D_FF, D_MODEL), D_FF ** -0.5),
        'norm_final': 1.0 + _normal(ks[14], (D_MODEL,), 0.1),
    }


def _fwd_reference(x, norm_mix, w_in, pool_w, pool_scale, w_pool_proj, conv_w, w_conv_out, w_o,
              norm_ffn, w_up, ffn_conv_w, ffn_conv_b, w_down, norm_final):
    for layer in range(DEPTH):
        h = rms_norm(x, norm_mix[layer])
        x = x + token_mixer(h, w_in[layer], pool_w[layer], pool_scale[layer], w_pool_proj[layer],
                            conv_w[layer], w_conv_out[layer], w_o[layer])
        h = rms_norm(x, norm_ffn[layer])
        x = x + channel_mixer(h, w_up[layer], ffn_conv_w[layer], ffn_conv_b[layer], w_down[layer])
    return rms_norm(x, norm_final)


import jax as _jax
import jax.numpy as _jnp

TWIN_FORMAT = 'train_step'
FWD_PARAMS = ['x', 'norm_mix', 'w_in', 'pool_w', 'pool_scale', 'w_pool_proj', 'conv_w', 'w_conv_out', 'w_o', 'norm_ffn', 'w_up', 'ffn_conv_w', 'ffn_conv_b', 'w_down', 'norm_final']
TWIN_WEIGHTS = ['norm_mix', 'w_in', 'pool_w', 'pool_scale', 'w_pool_proj', 'conv_w', 'w_conv_out', 'w_o', 'norm_ffn', 'w_up', 'ffn_conv_w', 'ffn_conv_b', 'w_down', 'norm_final']
TWIN_DIFF_INPUT = 'x'
TWIN_INPUTS = ['x', 'norm_mix', 'w_in', 'pool_w', 'pool_scale', 'w_pool_proj', 'conv_w', 'w_conv_out', 'w_o', 'norm_ffn', 'w_up', 'ffn_conv_w', 'ffn_conv_b', 'w_down', 'norm_final', 'loss_target', 'm_norm_mix', 'm_w_in', 'm_pool_w', 'm_pool_scale', 'm_w_pool_proj', 'm_conv_w', 'm_w_conv_out', 'm_w_o', 'm_norm_ffn', 'm_w_up', 'm_ffn_conv_w', 'm_ffn_conv_b', 'm_w_down', 'm_norm_final', 'v_norm_mix', 'v_w_in', 'v_pool_w', 'v_pool_scale', 'v_w_pool_proj', 'v_conv_w', 'v_w_conv_out', 'v_w_o', 'v_norm_ffn', 'v_w_up', 'v_ffn_conv_w', 'v_ffn_conv_b', 'v_w_down', 'v_norm_final']
TWIN_OUTPUTS = ['loss', 'grad_x', 'grad_norm_mix', 'grad_w_in', 'grad_pool_w', 'grad_pool_scale', 'grad_w_pool_proj', 'grad_conv_w', 'grad_w_conv_out', 'grad_w_o', 'grad_norm_ffn', 'grad_w_up', 'grad_ffn_conv_w', 'grad_ffn_conv_b', 'grad_w_down', 'grad_norm_final', 'delta_norm_mix', 'delta_w_in', 'delta_pool_w', 'delta_pool_scale', 'delta_w_pool_proj', 'delta_conv_w', 'delta_w_conv_out', 'delta_w_o', 'delta_norm_ffn', 'delta_w_up', 'delta_ffn_conv_w', 'delta_ffn_conv_b', 'delta_w_down', 'delta_norm_final', 'new_m_norm_mix', 'new_m_w_in', 'new_m_pool_w', 'new_m_pool_scale', 'new_m_w_pool_proj', 'new_m_conv_w', 'new_m_w_conv_out', 'new_m_w_o', 'new_m_norm_ffn', 'new_m_w_up', 'new_m_ffn_conv_w', 'new_m_ffn_conv_b', 'new_m_w_down', 'new_m_norm_final', 'new_v_norm_mix', 'new_v_w_in', 'new_v_pool_w', 'new_v_pool_scale', 'new_v_w_pool_proj', 'new_v_conv_w', 'new_v_w_conv_out', 'new_v_w_o', 'new_v_norm_ffn', 'new_v_w_up', 'new_v_ffn_conv_w', 'new_v_ffn_conv_b', 'new_v_w_down', 'new_v_norm_final']
TWIN_LEAF_KINDS = {'loss': 'loss', 'grad_x': 'grad_x', 'grad_norm_mix': 'grad_w', 'grad_w_in': 'grad_w', 'grad_pool_w': 'grad_w', 'grad_pool_scale': 'grad_w', 'grad_w_pool_proj': 'grad_w', 'grad_conv_w': 'grad_w', 'grad_w_conv_out': 'grad_w', 'grad_w_o': 'grad_w', 'grad_norm_ffn': 'grad_w', 'grad_w_up': 'grad_w', 'grad_ffn_conv_w': 'grad_w', 'grad_ffn_conv_b': 'grad_w', 'grad_w_down': 'grad_w', 'grad_norm_final': 'grad_w', 'delta_norm_mix': 'delta_w', 'delta_w_in': 'delta_w', 'delta_pool_w': 'delta_w', 'delta_pool_scale': 'delta_w', 'delta_w_pool_proj': 'delta_w', 'delta_conv_w': 'delta_w', 'delta_w_conv_out': 'delta_w', 'delta_w_o': 'delta_w', 'delta_norm_ffn': 'delta_w', 'delta_w_up': 'delta_w', 'delta_ffn_conv_w': 'delta_w', 'delta_ffn_conv_b': 'delta_w', 'delta_w_down': 'delta_w', 'delta_norm_final': 'delta_w', 'new_m_norm_mix': 'new_m', 'new_m_w_in': 'new_m', 'new_m_pool_w': 'new_m', 'new_m_pool_scale': 'new_m', 'new_m_w_pool_proj': 'new_m', 'new_m_conv_w': 'new_m', 'new_m_w_conv_out': 'new_m', 'new_m_w_o': 'new_m', 'new_m_norm_ffn': 'new_m', 'new_m_w_up': 'new_m', 'new_m_ffn_conv_w': 'new_m', 'new_m_ffn_conv_b': 'new_m', 'new_m_w_down': 'new_m', 'new_m_norm_final': 'new_m', 'new_v_norm_mix': 'new_v', 'new_v_w_in': 'new_v', 'new_v_pool_w': 'new_v', 'new_v_pool_scale': 'new_v', 'new_v_w_pool_proj': 'new_v', 'new_v_conv_w': 'new_v', 'new_v_w_conv_out': 'new_v', 'new_v_w_o': 'new_v', 'new_v_norm_ffn': 'new_v', 'new_v_w_up': 'new_v', 'new_v_ffn_conv_w': 'new_v', 'new_v_ffn_conv_b': 'new_v', 'new_v_w_down': 'new_v', 'new_v_norm_final': 'new_v'}


def _forward(args):
    return _fwd_reference(*[args[k] for k in FWD_PARAMS])


def _output_shape():
    out = _jax.eval_shape(lambda: _forward(_fwd_setup_inputs(0)))
    return out.shape, out.dtype

N_MICROBATCH = 1
ADAM_LR = 0.001
ADAM_B1 = 0.9
ADAM_B2 = 0.999
ADAM_EPS = 1e-08
ADAM_WD = 0.01
ADAM_STEP = 10
PER_EXAMPLE_BATCH_AXIS = {'x': 0, 'loss_target': 0}
SHARED_INPUTS = []
_WEIGHT_DTYPES = {'norm_mix': _jnp.float32, 'w_in': _jnp.float32, 'pool_w': _jnp.float32, 'pool_scale': _jnp.float32, 'w_pool_proj': _jnp.float32, 'conv_w': _jnp.float32, 'w_conv_out': _jnp.float32, 'w_o': _jnp.float32, 'norm_ffn': _jnp.float32, 'w_up': _jnp.float32, 'ffn_conv_w': _jnp.float32, 'ffn_conv_b': _jnp.float32, 'w_down': _jnp.float32, 'norm_final': _jnp.float32}
MOMENT_SCALE = {'norm_mix': 1.735409e-01, 'w_in': 7.340047e-02, 'pool_w': 8.121563e-02, 'pool_scale': 7.739969e-02, 'w_pool_proj': 8.158074e-02, 'conv_w': 9.085462e-02, 'w_conv_out': 9.091132e-02, 'w_o': 1.250461e-01, 'norm_ffn': 1.239648e-01, 'w_up': 5.099027e-02, 'ffn_conv_w': 5.117180e-02, 'ffn_conv_b': 5.082009e-02, 'w_down': 8.681567e-02, 'norm_final': 3.214326e+01}


def _to_microbatches(a, axis):
    t = _jnp.moveaxis(a, axis, 0)
    t = t.reshape((N_MICROBATCH, t.shape[0] // N_MICROBATCH) + t.shape[1:])
    return _jnp.moveaxis(t, 1, axis + 1)


def setup_inputs(seed: int = 0) -> dict:
    inp = _fwd_setup_inputs(seed)
    key = _jax.random.fold_in(_jax.random.key(seed), 7919)
    shape, _ = _output_shape()
    out = dict(inp)
    out["loss_target"] = _jax.random.normal(_jax.random.fold_in(key, 0), shape, _jnp.float32)
    for i, name in enumerate(TWIN_WEIGHTS):
        w = inp[name].astype(_jnp.float32)
        if MOMENT_SCALE is None:
            s = _jnp.sqrt(_jnp.mean(_jnp.square(w)) + 1e-30)
        else:
            s = MOMENT_SCALE[name]
        km, kv = _jax.random.split(_jax.random.fold_in(key, i + 1))
        out[name] = w
        out["m_" + name] = s * _jax.random.normal(km, w.shape, _jnp.float32)
        out["v_" + name] = (s * s) * _jax.random.uniform(kv, w.shape, _jnp.float32, 0.5, 1.5)
    if N_MICROBATCH > 1:
        for name, axis in PER_EXAMPLE_BATCH_AXIS.items():
            out[name] = _to_microbatches(out[name], axis)
    return {'x': out['x'], 'norm_mix': out['norm_mix'], 'w_in': out['w_in'], 'pool_w': out['pool_w'], 'pool_scale': out['pool_scale'], 'w_pool_proj': out['w_pool_proj'], 'conv_w': out['conv_w'], 'w_conv_out': out['w_conv_out'], 'w_o': out['w_o'], 'norm_ffn': out['norm_ffn'], 'w_up': out['w_up'], 'ffn_conv_w': out['ffn_conv_w'], 'ffn_conv_b': out['ffn_conv_b'], 'w_down': out['w_down'], 'norm_final': out['norm_final'], 'loss_target': out['loss_target'], 'm_norm_mix': out['m_norm_mix'], 'm_w_in': out['m_w_in'], 'm_pool_w': out['m_pool_w'], 'm_pool_scale': out['m_pool_scale'], 'm_w_pool_proj': out['m_w_pool_proj'], 'm_conv_w': out['m_conv_w'], 'm_w_conv_out': out['m_w_conv_out'], 'm_w_o': out['m_w_o'], 'm_norm_ffn': out['m_norm_ffn'], 'm_w_up': out['m_w_up'], 'm_ffn_conv_w': out['m_ffn_conv_w'], 'm_ffn_conv_b': out['m_ffn_conv_b'], 'm_w_down': out['m_w_down'], 'm_norm_final': out['m_norm_final'], 'v_norm_mix': out['v_norm_mix'], 'v_w_in': out['v_w_in'], 'v_pool_w': out['v_pool_w'], 'v_pool_scale': out['v_pool_scale'], 'v_w_pool_proj': out['v_w_pool_proj'], 'v_conv_w': out['v_conv_w'], 'v_w_conv_out': out['v_w_conv_out'], 'v_w_o': out['v_w_o'], 'v_norm_ffn': out['v_norm_ffn'], 'v_w_up': out['v_w_up'], 'v_ffn_conv_w': out['v_ffn_conv_w'], 'v_ffn_conv_b': out['v_ffn_conv_b'], 'v_w_down': out['v_w_down'], 'v_norm_final': out['v_norm_final']}


def _loss(weights, diff, rest, loss_target):
    with _jax.named_scope("forward"):
        args = {**rest, TWIN_DIFF_INPUT: diff, **{k: w.astype(_WEIGHT_DTYPES[k]) for k, w in weights.items()}}
        y = _forward(args)
    with _jax.named_scope("loss_head"):
        err = _jnp.square(y.astype(_jnp.float32) - loss_target)
        return 0.5 * _jnp.sum(_jnp.mean(err, axis=-1)) if err.ndim else 0.5 * err


def _adamw(w, g, m, v):
    m = ADAM_B1 * m + (1.0 - ADAM_B1) * g
    v = ADAM_B2 * v + (1.0 - ADAM_B2) * _jnp.square(g)
    m_hat = m / (1.0 - ADAM_B1 ** ADAM_STEP)
    v_hat = v / (1.0 - ADAM_B2 ** ADAM_STEP)
    delta = -ADAM_LR * (m_hat / (_jnp.sqrt(v_hat) + ADAM_EPS) + ADAM_WD * w)
    return delta, m, v


def reference(x, norm_mix, w_in, pool_w, pool_scale, w_pool_proj, conv_w, w_conv_out, w_o, norm_ffn, w_up, ffn_conv_w, ffn_conv_b, w_down, norm_final, loss_target, m_norm_mix, m_w_in, m_pool_w, m_pool_scale, m_w_pool_proj, m_conv_w, m_w_conv_out, m_w_o, m_norm_ffn, m_w_up, m_ffn_conv_w, m_ffn_conv_b, m_w_down, m_norm_final, v_norm_mix, v_w_in, v_pool_w, v_pool_scale, v_w_pool_proj, v_conv_w, v_w_conv_out, v_w_o, v_norm_ffn, v_w_up, v_ffn_conv_w, v_ffn_conv_b, v_w_down, v_norm_final):
    given = dict(x=x, norm_mix=norm_mix, w_in=w_in, pool_w=pool_w, pool_scale=pool_scale, w_pool_proj=w_pool_proj, conv_w=conv_w, w_conv_out=w_conv_out, w_o=w_o, norm_ffn=norm_ffn, w_up=w_up, ffn_conv_w=ffn_conv_w, ffn_conv_b=ffn_conv_b, w_down=w_down, norm_final=norm_final, loss_target=loss_target, m_norm_mix=m_norm_mix, m_w_in=m_w_in, m_pool_w=m_pool_w, m_pool_scale=m_pool_scale, m_w_pool_proj=m_w_pool_proj, m_conv_w=m_conv_w, m_w_conv_out=m_w_conv_out, m_w_o=m_w_o, m_norm_ffn=m_norm_ffn, m_w_up=m_w_up, m_ffn_conv_w=m_ffn_conv_w, m_ffn_conv_b=m_ffn_conv_b, m_w_down=m_w_down, m_norm_final=m_norm_final, v_norm_mix=v_norm_mix, v_w_in=v_w_in, v_pool_w=v_pool_w, v_pool_scale=v_pool_scale, v_w_pool_proj=v_w_pool_proj, v_conv_w=v_conv_w, v_w_conv_out=v_w_conv_out, v_w_o=v_w_o, v_norm_ffn=v_norm_ffn, v_w_up=v_w_up, v_ffn_conv_w=v_ffn_conv_w, v_ffn_conv_b=v_ffn_conv_b, v_w_down=v_w_down, v_norm_final=v_norm_final)
    weights = {n: given[n] for n in TWIN_WEIGHTS}
    shared = {n: given[n] for n in SHARED_INPUTS}
    per_example = {n: given[n] for n in ['x']}
    grad_fn = _jax.value_and_grad(_loss, argnums=(0, 1))

    def one_microbatch(ex, loss_target):
        ex = dict(ex)
        diff = ex.pop(TWIN_DIFF_INPUT)
        return grad_fn(weights, diff, {**shared, **ex}, loss_target)

    if N_MICROBATCH == 1:
        loss, (grad_w, grad_x) = one_microbatch(per_example, given["loss_target"])
    else:
        def body(carry, xs):
            loss_sum, grad_sum = carry
            l_k, (gw_k, gx_k) = one_microbatch(xs[0], xs[1])
            with _jax.named_scope("update"):
                return (loss_sum + l_k, _jax.tree.map(_jnp.add, grad_sum, gw_k)), gx_k

        init = (_jnp.zeros((), _jnp.float32), _jax.tree.map(_jnp.zeros_like, weights))
        (loss, grad_w), grad_x = _jax.lax.scan(body, init, (per_example, given["loss_target"]))
    with _jax.named_scope("update"):
        delta_w, new_m, new_v = {}, {}, {}
        for n in TWIN_WEIGHTS:
            delta_w[n], new_m[n], new_v[n] = _adamw(weights[n], grad_w[n], given["m_" + n], given["v_" + n])
    return (loss, grad_x, *[grad_w[n] for n in TWIN_WEIGHTS], *[delta_w[n] for n in TWIN_WEIGHTS],
            *[new_m[n] for n in TWIN_WEIGHTS], *[new_v[n] for n in TWIN_WEIGHTS])
```

```python
import functools

import jax
import jax.numpy as jnp
from jax import lax
from jax.experimental import pallas as pl
from jax.experimental.pallas import tpu as pltpu

F32 = jnp.float32
BF16 = jnp.bfloat16

NDEV = 8
D = 1024
NG = 4
CG = 256
WINS = (2, 4, 8, 16)
DIN = 6 * D
SH_IN = DIN // NDEV
NZT = DIN // CG
FF2 = 5632
SH_UP = FF2 // NDEV
FF = FF2 // 2
NCH = 4
SH_DN = FF // NDEV
RMS_EPS = 1e-6
HALO = 16

ADAM_LR = 0.001
ADAM_B1 = 0.9
ADAM_B2 = 0.999
ADAM_EPS = 1e-08
ADAM_WD = 0.01
ADAM_STEP = 10

TM_IN = 1024
TM_MIX = 256
TM_FFN = 256
TK_WGRAD = 512
VMEM_BIG = 56 * 1024 * 1024
MESH = pl.DeviceIdType.MESH
ANY = pl.BlockSpec(memory_space=pl.ANY)


def _cparams(n_axes, vmem=None):
    return pltpu.CompilerParams(dimension_semantics=("arbitrary",) * n_axes, vmem_limit_bytes=vmem)


def _dot(a, b):
    return jnp.dot(a, b, preferred_element_type=F32)


def _dot_nt(a, b):
    return lax.dot_general(a, b, (((1,), (1,)), ((), ())), preferred_element_type=F32)


def _dot_tn(a, b):
    return lax.dot_general(a, b, (((0,), (0,)), ((), ())), preferred_element_type=F32)


def _shift_down(ext, s, lead):
    return pltpu.roll(ext, s, 0)[lead:]


def _shift_up(ext, s, tm):
    n = ext.shape[0]
    return pltpu.roll(ext, n - s, 0)[:tm]


def _rms_inv(x):
    return lax.rsqrt(jnp.mean(x * x, axis=-1, keepdims=True) + RMS_EPS)


def _rms_bwd(dh, xn, inv, g):
    dxn = dh * g
    return inv * (dxn - xn * jnp.mean(dxn * xn, axis=-1, keepdims=True))


def _pos():
    return lax.axis_index("x"), lax.axis_index("y"), lax.axis_index("c")


def all_gather_blocks(shards):
    n = len(shards)

    def body(*refs):
        ins, outs = refs[:n], refs[n:2 * n]
        send_sems, recv_sems, local_sems = refs[2 * n:]
        x, y, c = _pos()
        sibling = (x, y, 1 - c)
        chips = [(1 - x, y), (x, 1 - y), (1 - x, 1 - y)]

        def copy(w, k, block, to, src=None):
            slot = outs[w].at[4 * block[0] + 2 * block[1] + block[2]]
            return pltpu.make_async_remote_copy(
                src_ref=slot if src is None else src, dst_ref=slot,
                send_sem=send_sems.at[7 * w + k], recv_sem=recv_sems.at[7 * w + k],
                device_id=to, device_id_type=MESH)

        mine, first = [], []
        for w in range(n):
            m = pltpu.make_async_copy(ins[w], outs[w].at[4 * x + 2 * y + c], local_sems.at[w])
            m.start()
            mine.append(m)
            first.append(copy(w, 0, (x, y, c), sibling, src=ins[w]))
            first += [copy(w, 1 + j, (x, y, c), (*chip, c), src=ins[w]) for j, chip in enumerate(chips)]
        for cp in first:
            cp.start()
        passed = []
        for w in range(n):
            for j, chip in enumerate(chips):
                copy(w, 1 + j, (*chip, c), (x, y, c)).wait_recv()
                fw = copy(w, 4 + j, (*chip, c), sibling)
                fw.start()
                passed.append(fw)
        for w in range(n):
            copy(w, 0, (x, y, 1 - c), (x, y, c)).wait_recv()
            for j, chip in enumerate(chips):
                copy(w, 4 + j, (*chip, 1 - c), (x, y, c)).wait_recv()
        for cp in first + passed:
            cp.wait_send()
        for m in mine:
            m.wait()

    return pl.pallas_call(
        body, name="all_gather_weights",
        out_shape=[jax.ShapeDtypeStruct((NDEV,) + s.shape, s.dtype) for s in shards],
        in_specs=[ANY] * n, out_specs=[ANY] * n,
        scratch_shapes=[pltpu.SemaphoreType.DMA((7 * n,)), pltpu.SemaphoreType.DMA((7 * n,)),
                        pltpu.SemaphoreType.DMA((n,))],
    )(*shards)


def all_reduce_small(v):
    rows = v.shape[0]

    def body(v_ref, out_ref, slots, send_sems, recv_sems, local_sem):
        x, y, c = _pos()
        me = 4 * x + 2 * y + c
        mine = pltpu.make_async_copy(v_ref, slots.at[me], local_sem)
        mine.start()
        offs = [(dx, dy, dc) for dx in (0, 1) for dy in (0, 1) for dc in (0, 1)][1:]

        def copy(k, src_slot, to):
            return pltpu.make_async_remote_copy(
                src_ref=v_ref, dst_ref=slots.at[src_slot], send_sem=send_sems.at[k], recv_sem=recv_sems.at[k],
                device_id=to, device_id_type=MESH)

        sends = []
        for k, (dx, dy, dc) in enumerate(offs):
            cp = copy(k, me, (x ^ dx, y ^ dy, c ^ dc))
            cp.start()
            sends.append(cp)
        for k, (dx, dy, dc) in enumerate(offs):
            copy(k, 4 * (x ^ dx) + 2 * (y ^ dy) + (c ^ dc), (x, y, c)).wait_recv()
        for cp in sends:
            cp.wait_send()
        mine.wait()
        acc = slots[0]
        for d in range(1, NDEV):
            acc = acc + slots[d]
        out_ref[...] = acc

    return pl.pallas_call(
        body, name="all_reduce_small",
        out_shape=jax.ShapeDtypeStruct((rows, D), F32),
        in_specs=[pl.BlockSpec(memory_space=pltpu.VMEM)], out_specs=pl.BlockSpec(memory_space=pltpu.VMEM),
        scratch_shapes=[pltpu.VMEM((NDEV, rows, D), F32), pltpu.SemaphoreType.DMA((7,)),
                        pltpu.SemaphoreType.DMA((7,)), pltpu.SemaphoreType.DMA],
    )(v)


def reduce_scatter_d2d(grads):
    n = len(grads)

    def body(*refs):
        ins, outs = refs[:n], refs[n:2 * n]
        send_sems, recv_sems = refs[2 * n:]
        x, y, c = _pos()
        cps = []
        for w in range(n):
            cp = pltpu.make_async_remote_copy(
                src_ref=ins[w].at[1 - c], dst_ref=outs[w], send_sem=send_sems.at[w], recv_sem=recv_sems.at[w],
                device_id=(x, y, 1 - c), device_id_type=MESH)
            cp.start()
            cps.append(cp)
        for cp in cps:
            cp.wait_recv()
        for cp in cps:
            cp.wait_send()

    return pl.pallas_call(
        body, name="reduce_scatter_d2d",
        out_shape=[jax.ShapeDtypeStruct(g.shape[1:], F32) for g in grads],
        in_specs=[ANY] * n, out_specs=[ANY] * n,
        scratch_shapes=[pltpu.SemaphoreType.DMA((n,)), pltpu.SemaphoreType.DMA((n,))],
    )(*grads)


def reduce_scatter_ici(parts):
    n = len(parts)

    def body(*refs):
        ins, outs = refs[:n], refs[n:2 * n]
        send_sems, recv_sems = refs[2 * n:]
        x, y, c = _pos()
        offs = [(1, 0), (0, 1), (1, 1)]
        cps = []
        for w in range(n):
            for k, (dx, dy) in enumerate(offs):
                ox, oy = x ^ dx, y ^ dy
                cp = pltpu.make_async_remote_copy(
                    src_ref=ins[w].at[2 * ox + oy], dst_ref=outs[w].at[2 * x + y],
                    send_sem=send_sems.at[3 * w + k], recv_sem=recv_sems.at[3 * w + k],
                    device_id=(ox, oy, c), device_id_type=MESH)
                cp.start()
                cps.append((cp, w, k, ox, oy))
        for cp, w, k, ox, oy in cps:
            pltpu.make_async_remote_copy(
                src_ref=ins[w].at[2 * ox + oy], dst_ref=outs[w].at[2 * ox + oy],
                send_sem=send_sems.at[3 * w + k], recv_sem=recv_sems.at[3 * w + k],
                device_id=(ox, oy, c), device_id_type=MESH).wait_recv()
        for cp, *_ in cps:
            cp.wait_send()

    return pl.pallas_call(
        body, name="reduce_scatter_ici",
        out_shape=[jax.ShapeDtypeStruct(p.shape, BF16) for p in parts],
        in_specs=[ANY] * n, out_specs=[ANY] * n,
        scratch_shapes=[pltpu.SemaphoreType.DMA((3 * n,)), pltpu.SemaphoreType.DMA((3 * n,))],
    )(*parts)


def fwd_in(x, g1, w_in_g, tm):
    t = x.shape[0]

    def body(x_ref, g_ref, w_ref, z_ref, h_ref):
        @pl.when(pl.program_id(1) == 0)
        def _():
            xf = x_ref[...]
            h_ref[...] = (xf * _rms_inv(xf) * g_ref[...]).astype(BF16)

        r = _dot(h_ref[...], w_ref[...])
        for q in range(3):
            z_ref[q] = r[:, q * CG:(q + 1) * CG].astype(BF16)

    return pl.pallas_call(
        body, name="fwd_in", grid=(t // tm, NDEV),
        in_specs=[pl.BlockSpec((tm, D), lambda i, j: (i, 0)), pl.BlockSpec((1, D), lambda i, j: (0, 0)),
                  pl.BlockSpec((None, D, SH_IN), lambda i, j: (j, 0, 0))],
        out_specs=[pl.BlockSpec((3, tm, CG), lambda i, j: (j, i, 0)), pl.BlockSpec((tm, D), lambda i, j: (i, 0))],
        out_shape=[jax.ShapeDtypeStruct((NZT, t, CG), BF16), jax.ShapeDtypeStruct((t, D), BF16)],
        compiler_params=_cparams(2, VMEM_BIG),
    )(x, g1, w_in_g)


def _pool_tile(z_ref, zh_ref, g, win, keep_hist, cnt):
    zt = z_ref[g].astype(F32)
    ext = jnp.concatenate([zh_ref[g].astype(F32) * keep_hist, zt], axis=0)
    s, sh = ext, 1
    while sh < win:
        s = s + pltpu.roll(s, sh, 0)
        sh *= 2
    return s[HALO:] / cnt - zt


def _conv_taps(ext, cur, w_ref, lanes, lead):
    x1 = _shift_down(ext, 1, lead)
    x2 = _shift_down(ext, 2, lead)
    out = w_ref[2:3, lanes] * cur + w_ref[1:2, lanes] * x1 + w_ref[0:1, lanes] * x2
    return out, x1, x2


def fwd_mix(z, x, pool_w, pool_scale, w_pp, conv_w, w_co, w_o, tm, seq):
    t = x.shape[0]
    tps = seq // tm
    hb = tm // HALO

    def body(z_ref, zph_ref, zcvh_ref, x_ref, pw_ref, ps_ref, wpp_ref, cw_ref, wco_ref, wo_ref,
             x1_ref, yp_ref, yc_ref):
        it = pl.program_id(0) % tps
        keep_hist = jnp.where(it == 0, 0.0, 1.0)
        pos = it * tm + lax.broadcasted_iota(jnp.int32, (tm, 1), 0)
        p2 = []
        for g, win in enumerate(WINS):
            cnt = jnp.minimum(pos + 1, win).astype(F32)
            p = _pool_tile(z_ref, zph_ref, g, win, keep_hist, cnt)
            lanes = slice(g * CG, (g + 1) * CG)
            p2.append((_dot(p.astype(BF16), pw_ref[g]) * ps_ref[:, lanes]).astype(BF16))
        y_pool = _dot(jnp.concatenate(p2, axis=1), wpp_ref[...])
        u = []
        for q in range(NG):
            lanes = slice(q * CG, (q + 1) * CG)
            cv = z_ref[8 + q].astype(F32) * z_ref[12 + q].astype(F32)
            cvh = zcvh_ref[q].astype(F32) * zcvh_ref[4 + q].astype(F32) * keep_hist
            cc, _, _ = _conv_taps(jnp.concatenate([cvh, cv], axis=0), cv, cw_ref, lanes, HALO)
            u.append((z_ref[4 + q].astype(F32) * cc).astype(BF16))
        y_conv = _dot(jnp.concatenate(u, axis=1), wco_ref[...])
        ypb, ycb = y_pool.astype(BF16), y_conv.astype(BF16)
        yp_ref[...] = ypb
        yc_ref[...] = ycb
        merged = []
        for q in range(NG):
            lanes = slice(q * CG, (q + 1) * CG)
            sp = jax.nn.sigmoid(z_ref[16 + q].astype(F32))
            sc = jax.nn.sigmoid(z_ref[20 + q].astype(F32))
            merged.append((sp * ypb[:, lanes].astype(F32) + sc * ycb[:, lanes].astype(F32)).astype(BF16))
        x1_ref[...] = x_ref[...] + _dot(jnp.concatenate(merged, axis=1), wo_ref[...])

    def hist(i):
        return jnp.maximum(i * hb - 1, 0)

    const2 = lambda i: (0, 0)
    return pl.pallas_call(
        body, name="fwd_mix", grid=(t // tm,),
        in_specs=[pl.BlockSpec((NZT, tm, CG), lambda i: (0, i, 0)),
                  pl.BlockSpec((NG, HALO, CG), lambda i: (0, hist(i), 0)),
                  pl.BlockSpec((2 * NG, HALO, CG), lambda i: (1, hist(i), 0)),
                  pl.BlockSpec((tm, D), lambda i: (i, 0)),
                  pl.BlockSpec((NG, CG, CG), lambda i: (0, 0, 0)), pl.BlockSpec((1, D), const2),
                  pl.BlockSpec((D, D), const2), pl.BlockSpec((3, D), const2),
                  pl.BlockSpec((D, D), const2), pl.BlockSpec((D, D), const2)],
        out_specs=[pl.BlockSpec((tm, D), lambda i: (i, 0))] * 3,
        out_shape=[jax.ShapeDtypeStruct((t, D), F32), jax.ShapeDtypeStruct((t, D), BF16),
                   jax.ShapeDtypeStruct((t, D), BF16)],
        compiler_params=_cparams(1, VMEM_BIG),
    )(z, z, z, x, pool_w, pool_scale, w_pp, conv_w, w_co, w_o)


def fwd_ffn(x1, g2, w_up_g, fcw, fcb, w_dn, gf, tgt, tm, seq):
    t = x1.shape[0]
    tps = seq // tm

    def body(x1_ref, g2_ref, wup_ref, fcw_ref, fcb_ref, wdn_ref, gf_ref, tgt_ref,
             up_ref, act_ref, h2_ref, dx2_ref, loss_ref, gfin_ref, hist_ref):
        i = pl.program_id(0)
        keep_hist = jnp.where(i % tps == 0, 0.0, 1.0)

        @pl.when(i == 0)
        def _():
            loss_ref[...] = jnp.zeros_like(loss_ref)
            gfin_ref[...] = jnp.zeros_like(gfin_ref)
            hist_ref[...] = jnp.zeros_like(hist_ref)

        x1v = x1_ref[...]
        h2 = (x1v * _rms_inv(x1v) * g2_ref[...]).astype(BF16)
        h2_ref[...] = h2
        d = jnp.zeros((tm, D), F32)
        lanes = slice(0, SH_UP)
        for k in range(NCH):
            conv = []
            for kk in (k, NCH + k):
                ub = _dot(h2, wup_ref[kk]).astype(BF16)
                up_ref[kk] = ub
                uf = ub.astype(F32)
                ext = jnp.concatenate([hist_ref[kk] * keep_hist, uf], axis=0)
                hist_ref[kk] = uf[tm - 8:]
                cc, _, _ = _conv_taps(ext, uf, fcw_ref.at[kk], lanes, 8)
                conv.append(cc + fcb_ref[kk])
            a = (conv[0] * jax.nn.sigmoid(conv[0]) * conv[1]).astype(BF16)
            act_ref[k] = a
            d = d + _dot(a, wdn_ref[k])
        x2 = x1v + d
        inv3 = _rms_inv(x2)
        xn = x2 * inv3
        diff = xn * gf_ref[...] - tgt_ref[...]
        loss_ref[...] += 0.5 * jnp.sum(jnp.mean(diff * diff, axis=-1))
        dy = diff * (1.0 / D)
        gfin_ref[...] += jnp.sum(dy * xn, axis=0, keepdims=True)
        dx2_ref[...] = _rms_bwd(dy, xn, inv3, gf_ref[...])

    tile = lambda i: (i, 0)
    const2 = lambda i: (0, 0)
    const3 = lambda i: (0, 0, 0)
    return pl.pallas_call(
        body, name="fwd_ffn", grid=(t // tm,),
        in_specs=[pl.BlockSpec((tm, D), tile), pl.BlockSpec((1, D), const2),
                  pl.BlockSpec((NDEV, D, SH_UP), const3), pl.BlockSpec((NDEV, 3, SH_UP), const3),
                  pl.BlockSpec((NDEV, 1, SH_UP), const3), pl.BlockSpec((NCH, SH_UP, D), const3),
                  pl.BlockSpec((1, D), const2), pl.BlockSpec((tm, D), tile)],
        out_specs=[pl.BlockSpec((NDEV, tm, SH_UP), lambda i: (0, i, 0)),
                   pl.BlockSpec((NCH, tm, SH_UP), lambda i: (0, i, 0)),
                   pl.BlockSpec((tm, D), tile), pl.BlockSpec((tm, D), tile),
                   pl.BlockSpec((8, 128), const2), pl.BlockSpec((1, D), const2)],
        out_shape=[jax.ShapeDtypeStruct((NDEV, t, SH_UP), BF16), jax.ShapeDtypeStruct((NCH, t, SH_UP), BF16),
                   jax.ShapeDtypeStruct((t, D), BF16), jax.ShapeDtypeStruct((t, D), F32),
                   jax.ShapeDtypeStruct((8, 128), F32), jax.ShapeDtypeStruct((1, D), F32)],
        scratch_shapes=[pltpu.VMEM((NDEV, 8, SH_UP), F32)],
        compiler_params=_cparams(1, VMEM_BIG),
    )(x1, g2, w_up_g, fcw, fcb, w_dn, gf, tgt)


def bwd_ffn(dx2, x1, g2, up, w_up_g, fcw, fcb, w_dn, tm, seq):
    t = x1.shape[0]
    nt = t // tm
    tps = seq // tm
    hb = tm // HALO

    def body(dx2_ref, x1_ref, g2_ref, up_ref, uph_ref, wup_ref, fcw_ref, fcb_ref, wdn_ref,
             dup_ref, dx1_ref, gvec_ref, gn_ref, carry_ref):
        i = pl.program_id(0)
        it = (nt - 1 - i) % tps
        keep_hist = jnp.where(it == 0, 0.0, 1.0)
        keep_next = jnp.where(it == tps - 1, 0.0, 1.0)

        @pl.when(i == 0)
        def _():
            gvec_ref[...] = jnp.zeros_like(gvec_ref)
            gn_ref[...] = jnp.zeros_like(gn_ref)
            carry_ref[...] = jnp.zeros_like(carry_ref)

        dx2v = dx2_ref[...]
        dxb = dx2v.astype(BF16)
        lanes = slice(0, SH_UP)
        dh2 = jnp.zeros((tm, D), F32)
        for k in range(NCH):
            pre, taps = [], []
            for kk in (k, NCH + k):
                uf = up_ref[kk].astype(F32)
                ext = jnp.concatenate([uph_ref[kk].astype(F32)[8:] * keep_hist, uf], axis=0)
                cc, u1, u2 = _conv_taps(ext, uf, fcw_ref.at[kk], lanes, 8)
                pre.append(cc + fcb_ref[kk])
                taps.append((u2, u1, uf))
            sg = jax.nn.sigmoid(pre[0])
            dact = _dot_nt(dxb, wdn_ref[k])
            dpre = [dact * pre[1] * (sg * (1.0 + pre[0] * (1.0 - sg))), dact * (pre[0] * sg)]
            for s, kk in enumerate((k, NCH + k)):
                dc = dpre[s]
                gvec_ref[kk, 0:1, :] += jnp.sum(dc, axis=0, keepdims=True)
                for tap in range(3):
                    gvec_ref[kk, tap + 1:tap + 2, :] += jnp.sum(dc * taps[s][tap], axis=0, keepdims=True)
                ext = jnp.concatenate([dc, carry_ref[kk] * keep_next], axis=0)
                carry_ref[kk] = dc[:8]
                w = fcw_ref.at[kk]
                du = w[2:3, :] * dc + w[1:2, :] * _shift_up(ext, 1, tm) + w[0:1, :] * _shift_up(ext, 2, tm)
                dub = du.astype(BF16)
                dup_ref[kk] = dub
                dh2 = dh2 + _dot_nt(dub, wup_ref[kk])
        x1v = x1_ref[...]
        inv2 = _rms_inv(x1v)
        xn = x1v * inv2
        gn_ref[...] += jnp.sum(dh2 * xn, axis=0, keepdims=True)
        dx1_ref[...] = dx2v + _rms_bwd(dh2, xn, inv2, g2_ref[...])

    rev = lambda i: (nt - 1 - i, 0)
    const2 = lambda i: (0, 0)
    const3 = lambda i: (0, 0, 0)
    return pl.pallas_call(
        body, name="bwd_ffn", grid=(nt,),
        in_specs=[pl.BlockSpec((tm, D), rev), pl.BlockSpec((tm, D), rev), pl.BlockSpec((1, D), const2),
                  pl.BlockSpec((NDEV, tm, SH_UP), lambda i: (0, nt - 1 - i, 0)),
                  pl.BlockSpec((NDEV, HALO, SH_UP), lambda i: (0, jnp.maximum((nt - 1 - i) * hb - 1, 0), 0)),
                  pl.BlockSpec((NDEV, D, SH_UP), const3), pl.BlockSpec((NDEV, 3, SH_UP), const3),
                  pl.BlockSpec((NDEV, 1, SH_UP), const3), pl.BlockSpec((NCH, SH_UP, D), const3)],
        out_specs=[pl.BlockSpec((NDEV, tm, SH_UP), lambda i: (0, nt - 1 - i, 0)), pl.BlockSpec((tm, D), rev),
                   pl.BlockSpec((NDEV, 4, SH_UP), const3), pl.BlockSpec((1, D), const2)],
        out_shape=[jax.ShapeDtypeStruct((NDEV, t, SH_UP), BF16), jax.ShapeDtypeStruct((t, D), F32),
                   jax.ShapeDtypeStruct((NDEV, 4, SH_UP), F32), jax.ShapeDtypeStruct((1, D), F32)],
        scratch_shapes=[pltpu.VMEM((NDEV, 8, SH_UP), F32)],
        compiler_params=_cparams(1, VMEM_BIG),
    )(dx2, x1, g2, up, up, w_up_g, fcw, fcb, w_dn)


def bwd_mix(dx1, z, y_pool, y_conv, pool_w, pool_scale, w_pp, conv_w, w_co, w_o, tm, seq):
    t = dx1.shape[0]
    nt = t // tm
    tps = seq // tm
    hb = tm // HALO

    def body(da_ref, z_ref, zph_ref, zcvh_ref, yp_ref, yc_ref, pw_ref, ps_ref, wpp_ref, cw_ref, wco_ref, wo_ref,
             dz_ref, mg_ref, p2_ref, u_ref, dyp_ref, dyc_ref, p_ref, dpw_ref, gvec_ref, cp_ref, cc_ref):
        i = pl.program_id(0)
        it = (nt - 1 - i) % tps
        keep_hist = jnp.where(it == 0, 0.0, 1.0)
        keep_next = jnp.where(it == tps - 1, 0.0, 1.0)
        pos = it * tm + lax.broadcasted_iota(jnp.int32, (tm, 1), 0)

        @pl.when(i == 0)
        def _():
            gvec_ref[...] = jnp.zeros_like(gvec_ref)
            cp_ref[...] = jnp.zeros_like(cp_ref)
            cc_ref[...] = jnp.zeros_like(cc_ref)

        dm = _dot_nt(da_ref[...].astype(BF16), wo_ref[...])
        merged, dyp, dyc = [], [], []
        for q in range(NG):
            lanes = slice(q * CG, (q + 1) * CG)
            sp = jax.nn.sigmoid(z_ref[16 + q].astype(F32))
            sc = jax.nn.sigmoid(z_ref[20 + q].astype(F32))
            yp = yp_ref[:, lanes].astype(F32)
            yc = yc_ref[:, lanes].astype(F32)
            dmq = dm[:, lanes]
            merged.append((sp * yp + sc * yc).astype(BF16))
            dyp.append((dmq * sp).astype(BF16))
            dyc.append((dmq * sc).astype(BF16))
            dz_ref[16 + q] = (dmq * yp * (sp * (1.0 - sp))).astype(BF16)
            dz_ref[20 + q] = (dmq * yc * (sc * (1.0 - sc))).astype(BF16)
        mg_ref[...] = jnp.concatenate(merged, axis=1)
        dypb = jnp.concatenate(dyp, axis=1)
        dycb = jnp.concatenate(dyc, axis=1)
        dyp_ref[...] = dypb
        dyc_ref[...] = dycb

        dp2 = _dot_nt(dypb, wpp_ref[...])
        p2 = []
        for g, win in enumerate(WINS):
            lanes = slice(g * CG, (g + 1) * CG)
            cnt = jnp.minimum(pos + 1, win).astype(F32)
            p = _pool_tile(z_ref, zph_ref, g, win, keep_hist, cnt)
            pb = p.astype(BF16)
            p_ref[g] = pb
            pw = _dot(pb, pw_ref[g])
            p2.append((pw * ps_ref[:, lanes]).astype(BF16))
            dp2g = dp2[:, lanes]
            gvec_ref[0:1, lanes] += jnp.sum(dp2g * pw, axis=0, keepdims=True)
            dpwb = (dp2g * ps_ref[:, lanes]).astype(BF16)
            dpw_ref[g] = dpwb
            dp = _dot_nt(dpwb, pw_ref[g])
            qv = dp / cnt
            ext = jnp.concatenate([qv, cp_ref[g] * keep_next], axis=0)
            cp_ref[g] = qv[:HALO]
            n = tm + HALO
            s, sh = ext, 1
            while sh < win:
                s = s + pltpu.roll(s, n - sh, 0)
                sh *= 2
            dz_ref[g] = (s[:tm] - dp).astype(BF16)
        p2_ref[...] = jnp.concatenate(p2, axis=1)

        du = _dot_nt(dycb, wco_ref[...])
        u = []
        for q in range(NG):
            lanes = slice(q * CG, (q + 1) * CG)
            zb = z_ref[4 + q].astype(F32)
            zc = z_ref[8 + q].astype(F32)
            zv = z_ref[12 + q].astype(F32)
            cv = zc * zv
            cvh = zcvh_ref[q].astype(F32) * zcvh_ref[4 + q].astype(F32) * keep_hist
            cc, cv1, cv2 = _conv_taps(jnp.concatenate([cvh, cv], axis=0), cv, cw_ref, lanes, HALO)
            u.append((zb * cc).astype(BF16))
            duq = du[:, lanes]
            dz_ref[4 + q] = (duq * cc).astype(BF16)
            dcc = duq * zb
            for tap, src in enumerate((cv2, cv1, cv)):
                gvec_ref[tap + 1:tap + 2, lanes] += jnp.sum(dcc * src, axis=0, keepdims=True)
            ext = jnp.concatenate([dcc, cc_ref[:, lanes] * keep_next], axis=0)
            cc_ref[:, lanes] = dcc[:8]
            dcv = (cw_ref[2:3, lanes] * dcc + cw_ref[1:2, lanes] * _shift_up(ext, 1, tm)
                   + cw_ref[0:1, lanes] * _shift_up(ext, 2, tm))
            dz_ref[8 + q] = (dcv * zv).astype(BF16)
            dz_ref[12 + q] = (dcv * zc).astype(BF16)
        u_ref[...] = jnp.concatenate(u, axis=1)

    def hist(i):
        return jnp.maximum((nt - 1 - i) * hb - 1, 0)

    rev = lambda i: (nt - 1 - i, 0)
    rev3 = lambda i: (0, nt - 1 - i, 0)
    const2 = lambda i: (0, 0)
    tok = jax.ShapeDtypeStruct((t, D), BF16)
    grp = jax.ShapeDtypeStruct((NG, t, CG), BF16)
    return pl.pallas_call(
        body, name="bwd_mix", grid=(nt,),
        in_specs=[pl.BlockSpec((tm, D), rev), pl.BlockSpec((NZT, tm, CG), rev3),
                  pl.BlockSpec((NG, HALO, CG), lambda i: (0, hist(i), 0)),
                  pl.BlockSpec((2 * NG, HALO, CG), lambda i: (1, hist(i), 0)),
                  pl.BlockSpec((tm, D), rev), pl.BlockSpec((tm, D), rev),
                  pl.BlockSpec((NG, CG, CG), lambda i: (0, 0, 0)), pl.BlockSpec((1, D), const2),
                  pl.BlockSpec((D, D), const2), pl.BlockSpec((3, D), const2),
                  pl.BlockSpec((D, D), const2), pl.BlockSpec((D, D), const2)],
        out_specs=[pl.BlockSpec((NZT, tm, CG), rev3)] + [pl.BlockSpec((tm, D), rev)] * 5
                  + [pl.BlockSpec((NG, tm, CG), rev3)] * 2 + [pl.BlockSpec((8, D), const2)],
        out_shape=[jax.ShapeDtypeStruct((NZT, t, CG), BF16), tok, tok, tok, tok, tok, grp, grp,
                   jax.ShapeDtypeStruct((8, D), F32)],
        scratch_shapes=[pltpu.VMEM((NG, HALO, CG), F32), pltpu.VMEM((8, D), F32)],
        compiler_params=_cparams(1, VMEM_BIG),
    )(dx1, z, z, z, y_pool, y_conv, pool_w, pool_scale, w_pp, conv_w, w_co, w_o)


def bwd_in(dz, w_in_g, dx1, x, g1, tm):
    t = x.shape[0]

    def body(dz_ref, w_ref, dx1_ref, x_ref, g_ref, gx_ref, gn_ref, acc_ref):
        i, j = pl.program_id(0), pl.program_id(1)

        @pl.when((i == 0) & (j == 0))
        def _():
            gn_ref[...] = jnp.zeros_like(gn_ref)

        @pl.when(j == 0)
        def _():
            acc_ref[...] = jnp.zeros_like(acc_ref)

        dzc = jnp.concatenate([dz_ref[q] for q in range(3)], axis=1)
        acc_ref[...] += _dot_nt(dzc, w_ref[...])

        @pl.when(j == NDEV - 1)
        def _():
            xv = x_ref[...]
            inv = _rms_inv(xv)
            xn = xv * inv
            dh = acc_ref[...]
            gn_ref[...] += jnp.sum(dh * xn, axis=0, keepdims=True)
            gx_ref[...] = dx1_ref[...] + _rms_bwd(dh, xn, inv, g_ref[...])

    tile = lambda i, j: (i, 0)
    return pl.pallas_call(
        body, name="bwd_in", grid=(t // tm, NDEV),
        in_specs=[pl.BlockSpec((3, tm, CG), lambda i, j: (j, i, 0)),
                  pl.BlockSpec((None, D, SH_IN), lambda i, j: (j, 0, 0)),
                  pl.BlockSpec((tm, D), tile), pl.BlockSpec((tm, D), tile), pl.BlockSpec((1, D), lambda i, j: (0, 0))],
        out_specs=[pl.BlockSpec((tm, D), tile), pl.BlockSpec((1, D), lambda i, j: (0, 0))],
        out_shape=[jax.ShapeDtypeStruct((t, D), F32), jax.ShapeDtypeStruct((1, D), F32)],
        scratch_shapes=[pltpu.VMEM((tm, D), F32)],
        compiler_params=_cparams(2, VMEM_BIG),
    )(dz, w_in_g, dx1, x, g1)


def _slot(j):
    return j % 2, j // 2


def wgrad_cols(a, b, pieces, width, name, tk):
    t, m = a.shape

    def body(a_ref, b_ref, o_ref):
        @pl.when(pl.program_id(1) == 0)
        def _():
            o_ref[...] = jnp.zeros_like(o_ref)

        av = a_ref[...]
        w = width // pieces
        for q in range(pieces):
            o_ref[:, q * w:(q + 1) * w] += _dot_tn(av, b_ref[q])

    return pl.pallas_call(
        body, name=name, grid=(NDEV, t // tk),
        in_specs=[pl.BlockSpec((tk, m), lambda j, k: (k, 0)),
                  pl.BlockSpec((pieces, tk, width // pieces), lambda j, k: (j, k, 0))],
        out_specs=pl.BlockSpec((None, None, m, width), lambda j, k: (j % 2, j // 2, 0, 0)),
        out_shape=jax.ShapeDtypeStruct((2, 4, m, width), F32),
        compiler_params=_cparams(2, VMEM_BIG),
    )(a, b)


def wgrad_down(act, dx2, tk):
    t = dx2.shape[0]

    def body(a_ref, b_ref, o_ref, acc_ref):
        kt = pl.program_id(1)

        @pl.when(kt == 0)
        def _():
            acc_ref[...] = jnp.zeros_like(acc_ref)

        acc_ref[...] += _dot_tn(a_ref[...], b_ref[...].astype(BF16))

        @pl.when(kt == pl.num_programs(1) - 1)
        def _():
            o_ref[0] = acc_ref[:SH_DN]
            o_ref[1] = acc_ref[SH_DN:]

    return pl.pallas_call(
        body, name="wgrad_down", grid=(NCH, t // tk),
        in_specs=[pl.BlockSpec((None, tk, SH_UP), lambda k, kt: (k, kt, 0)), pl.BlockSpec((tk, D), lambda k, kt: (kt, 0))],
        out_specs=pl.BlockSpec((2, None, SH_DN, D), lambda k, kt: (0, k, 0, 0)),
        out_shape=jax.ShapeDtypeStruct((2, 4, SH_DN, D), F32),
        scratch_shapes=[pltpu.VMEM((SH_UP, D), F32)],
        compiler_params=_cparams(2, VMEM_BIG),
    )(act, dx2)


def wgrad_square(a, b, name, tk):
    t = a.shape[0]

    def body(a_ref, b_ref, o_ref, acc_ref):
        kt = pl.program_id(0)

        @pl.when(kt == 0)
        def _():
            acc_ref[...] = jnp.zeros_like(acc_ref)

        acc_ref[...] += _dot_tn(a_ref[...], b_ref[...].astype(BF16))

        @pl.when(kt == pl.num_programs(0) - 1)
        def _():
            for j in range(NDEV):
                cc, xy = _slot(j)
                o_ref[cc, xy] = acc_ref[j * 128:(j + 1) * 128]

    return pl.pallas_call(
        body, name=name, grid=(t // tk,),
        in_specs=[pl.BlockSpec((tk, D), lambda k: (k, 0)), pl.BlockSpec((tk, D), lambda k: (k, 0))],
        out_specs=pl.BlockSpec((2, 4, 128, D), lambda k: (0, 0, 0, 0)),
        out_shape=jax.ShapeDtypeStruct((2, 4, 128, D), F32),
        scratch_shapes=[pltpu.VMEM((D, D), F32)],
        compiler_params=_cparams(1, VMEM_BIG),
    )(a, b)


def wgrad_pool(p, dpw, tk):
    t = p.shape[1]

    def body(a_ref, b_ref, o_ref):
        @pl.when(pl.program_id(0) == 0)
        def _():
            o_ref[...] = jnp.zeros_like(o_ref)

        for g in range(NG):
            o_ref[g] += _dot_tn(a_ref[g], b_ref[g])

    return pl.pallas_call(
        body, name="wgrad_pool", grid=(t // tk,),
        in_specs=[pl.BlockSpec((NG, tk, CG), lambda k: (0, k, 0))] * 2,
        out_specs=pl.BlockSpec((NG, CG, CG), lambda k: (0, 0, 0)),
        out_shape=jax.ShapeDtypeStruct((NG, CG, CG), F32),
        compiler_params=_cparams(1),
    )(p, dpw)


def _adamw(w, g, m, v):
    m = ADAM_B1 * m + (1.0 - ADAM_B1) * g
    v = ADAM_B2 * v + (1.0 - ADAM_B2) * (g * g)
    m_hat = m / (1.0 - ADAM_B1 ** ADAM_STEP)
    v_hat = v / (1.0 - ADAM_B2 ** ADAM_STEP)
    delta = -ADAM_LR * (m_hat / (jnp.sqrt(v_hat) + ADAM_EPS) + ADAM_WD * w)
    return delta, m, v


def _row_block(r):
    return 256 if r % 256 == 0 else r


def chip_partial(place, g, from_sibling, name):
    _, _, r, c = g.shape
    br = _row_block(r)

    def body(place_ref, g_ref, s_ref, o_ref):
        o_ref[...] = (g_ref[...] + s_ref[...]).astype(BF16)

    return pl.pallas_call(
        body, name=name,
        grid_spec=pltpu.PrefetchScalarGridSpec(
            num_scalar_prefetch=1, grid=(4, r // br),
            in_specs=[pl.BlockSpec((None, None, br, c), lambda q, i, pr: (pr[0], q, i, 0)),
                      pl.BlockSpec((None, br, c), lambda q, i, pr: (q, i, 0))],
            out_specs=pl.BlockSpec((None, br, c), lambda q, i, pr: (q, i, 0))),
        out_shape=jax.ShapeDtypeStruct((4, r, c), BF16),
        compiler_params=_cparams(2),
    )(place, g, from_sibling)


def finish_adamw(place, g, from_sibling, from_chips, w, m, v, name):
    _, _, r, c = g.shape
    br = _row_block(r)

    def body(place_ref, g_ref, s_ref, c1_ref, c2_ref, c3_ref, w_ref, m_ref, v_ref, og_ref, od_ref, om_ref, ov_ref):
        grad = g_ref[...] + s_ref[...]
        for ref in (c1_ref, c2_ref, c3_ref):
            grad = grad + ref[...].astype(F32)
        og_ref[...] = grad
        od_ref[...], om_ref[...], ov_ref[...] = _adamw(w_ref[...], grad, m_ref[...], v_ref[...])

    def other(k):
        return pl.BlockSpec((None, br, c), lambda i, pr: (pr[1] ^ k, i, 0))

    row = pl.BlockSpec((br, c), lambda i, pr: (i, 0))
    out = jax.ShapeDtypeStruct((r, c), F32)
    return pl.pallas_call(
        body, name=name,
        grid_spec=pltpu.PrefetchScalarGridSpec(
            num_scalar_prefetch=1, grid=(r // br,),
            in_specs=[pl.BlockSpec((None, None, br, c), lambda i, pr: (pr[0], pr[1], i, 0)),
                      pl.BlockSpec((None, br, c), lambda i, pr: (pr[1], i, 0)),
                      other(1), other(2), other(3), row, row, row],
            out_specs=[row] * 4),
        out_shape=[out] * 4,
        compiler_params=_cparams(1),
    )(place, g, from_sibling, from_chips, from_chips, from_chips, w, m, v)


def adamw_small(w, g, m, v):
    def body(w_ref, g_ref, m_ref, v_ref, od_ref, om_ref, ov_ref):
        od_ref[...], om_ref[...], ov_ref[...] = _adamw(w_ref[...], g_ref[...], m_ref[...], v_ref[...])

    out = jax.ShapeDtypeStruct(w.shape, F32)
    return pl.pallas_call(body, name="adamw_small", out_shape=[out] * 3)(w, g, m, v)


def _pad_lanes(a):
    return jnp.pad(a, ((0, 0), (0, D - a.shape[1])))


def kernel(x, norm_mix, w_in, pool_w, pool_scale, w_pool_proj, conv_w, w_conv_out, w_o, norm_ffn, w_up, ffn_conv_w, ffn_conv_b, w_down, norm_final, loss_target, m_norm_mix, m_w_in, m_pool_w, m_pool_scale, m_w_pool_proj, m_conv_w, m_w_conv_out, m_w_o, m_norm_ffn, m_w_up, m_ffn_conv_w, m_ffn_conv_b, m_w_down, m_norm_final, v_norm_mix, v_w_in, v_pool_w, v_pool_scale, v_w_pool_proj, v_conv_w, v_w_conv_out, v_w_o, v_norm_ffn, v_w_up, v_ffn_conv_w, v_ffn_conv_b, v_w_down, v_norm_final):
    nb, seq, _ = x.shape
    t = nb * seq
    tm_in = min(TM_IN, t)
    tm_mix = min(TM_MIX, seq)
    tm_ffn = min(TM_FFN, seq)
    tk = min(TK_WGRAD, t)
    xt = x.reshape(t, D)
    tgt = loss_target.reshape(t, D)
    xi, yi, ci = _pos()
    me = 4 * xi + 2 * yi + ci
    place = jnp.stack([ci, 2 * xi + yi]).astype(jnp.int32)

    taps = jnp.concatenate([_pad_lanes(conv_w[0]), _pad_lanes(ffn_conv_w[0]), jnp.zeros((2, D), F32)], axis=0)
    w_in_g, pool_w_g, w_pp_g, w_co_g, w_o_g, w_up_g, w_dn_g, taps_g = all_gather_blocks([
        w_in[0].astype(BF16), pool_w[0].astype(BF16), w_pool_proj[0].astype(BF16), w_conv_out[0].astype(BF16),
        w_o[0].astype(BF16), w_up[0].astype(BF16), w_down[0].astype(BF16), taps])
    pool_w_f = pool_w_g.transpose(1, 0, 2, 3).reshape(NG, CG, CG)
    w_pp_f = w_pp_g.reshape(D, D)
    w_co_f = w_co_g.reshape(D, D)
    w_o_f = w_o_g.reshape(D, D)
    w_dn_f = w_dn_g.reshape(NCH, SH_UP, D)
    conv_w_f = taps_g[:, 0:3, :128].transpose(1, 0, 2).reshape(3, D)
    fcw_f = taps_g[:, 3:6, :SH_UP]
    fcb_f = ffn_conv_b.reshape(NDEV, 1, SH_UP)
    gfin = norm_final.reshape(1, D)

    z, h1 = fwd_in(xt, norm_mix, w_in_g, tm_in)
    x1, y_pool, y_conv = fwd_mix(z, xt, pool_w_f, pool_scale, w_pp_f, conv_w_f, w_co_f, w_o_f, tm_mix, seq)
    up, act, h2, dx2, loss_acc, g_fin = fwd_ffn(x1, norm_ffn, w_up_g, fcw_f, fcb_f, w_dn_f, gfin, tgt, tm_ffn, seq)

    d_up, dx1, g_ffn_vec, g_nffn = bwd_ffn(dx2, x1, norm_ffn, up, w_up_g, fcw_f, fcb_f, w_dn_f, tm_ffn, seq)
    dz, merged, p2, u, dyp, dyc, p, dpw, g_mix_vec = bwd_mix(
        dx1, z, y_pool, y_conv, pool_w_f, pool_scale, w_pp_f, conv_w_f, w_co_f, w_o_f, tm_mix, seq)
    grad_x, g_nmix = bwd_in(dz, w_in_g, dx1, xt, norm_mix, tm_in)

    gw_in = wgrad_cols(h1, dz, 3, SH_IN, "wgrad_in", tk)
    gw_up = wgrad_cols(h2, d_up, 1, SH_UP, "wgrad_up", tk)
    gw_dn = wgrad_down(act, dx2, tk)
    gw_o = wgrad_square(merged, dx1, "wgrad_o", tk)
    gw_pp = wgrad_square(p2, dyp, "wgrad_pool_proj", tk)
    gw_co = wgrad_square(u, dyc, "wgrad_conv_out", tk)
    gw_pool = wgrad_pool(p, dpw, tk).reshape(NG, 4, 2, 32, CG).transpose(2, 1, 0, 3, 4).reshape(2, 4, NG * 32, CG)

    packed = jnp.concatenate([
        g_nmix, g_mix_vec[0:1], g_nffn, g_fin, g_mix_vec[1:4], _pad_lanes(loss_acc[0:1, 0:1]),
        _pad_lanes(g_ffn_vec[:, 0, :]), _pad_lanes(g_ffn_vec[:, 1, :]), _pad_lanes(g_ffn_vec[:, 2, :]),
        _pad_lanes(g_ffn_vec[:, 3, :])], axis=0)
    red = all_reduce_small(packed)
    loss = red[7, 0]
    g_norm_mix, g_pool_scale, g_norm_ffn = red[0:1], red[1:2], red[2:3]
    g_norm_final = red[3]
    g_conv_w = lax.dynamic_slice(red, (4, me * 128), (3, 128))
    g_fcb = red[8:16, :SH_UP].reshape(1, FF2)
    g_fcw = jnp.stack([lax.dynamic_slice(red, (16 + 8 * k + me, 0), (1, SH_UP))[0] for k in range(3)])

    full = [gw_in, gw_pool, gw_pp, gw_co, gw_o, gw_up, gw_dn]
    names = ["w_in", "pool_w", "w_pool_proj", "w_conv_out", "w_o", "w_up", "w_down"]
    from_sib = reduce_scatter_d2d(full)
    parts = [chip_partial(place, g, s, "chip_partial_" + nm) for g, s, nm in zip(full, from_sib, names)]
    from_chips = reduce_scatter_ici(parts)
    shard_wmv = [(w_in, m_w_in, v_w_in), (pool_w, m_pool_w, v_pool_w), (w_pool_proj, m_w_pool_proj, v_w_pool_proj),
                 (w_conv_out, m_w_conv_out, v_w_conv_out), (w_o, m_w_o, v_w_o), (w_up, m_w_up, v_w_up),
                 (w_down, m_w_down, v_w_down)]
    big = {}
    for nm, g, s, fc, (w, m, v) in zip(names, full, from_sib, from_chips, shard_wmv):
        rc = g.shape[2:]
        outs = finish_adamw(place, g, s, fc, w.reshape(rc), m.reshape(rc), v.reshape(rc), "adamw_" + nm)
        big[nm] = [o.reshape(w.shape) for o in outs]

    def pack_small(nm_, ps_, nf_, nfin_, fcb_, cw_, fcw_):
        return jnp.concatenate([nm_, ps_, nf_, nfin_.reshape(1, D), _pad_lanes(fcb_.reshape(NDEV, SH_UP)),
                                _pad_lanes(cw_.reshape(3, 128)), _pad_lanes(fcw_.reshape(3, SH_UP)),
                                jnp.zeros((6, D), F32)], axis=0)

    sw = pack_small(norm_mix, pool_scale, norm_ffn, norm_final, ffn_conv_b, conv_w, ffn_conv_w)
    sm = pack_small(m_norm_mix, m_pool_scale, m_norm_ffn, m_norm_final, m_ffn_conv_b, m_conv_w, m_ffn_conv_w)
    sv = pack_small(v_norm_mix, v_pool_scale, v_norm_ffn, v_norm_final, v_ffn_conv_b, v_conv_w, v_ffn_conv_w)
    sg = pack_small(g_norm_mix, g_pool_scale, g_norm_ffn, g_norm_final, g_fcb, g_conv_w, g_fcw)
    small_out = adamw_small(sw, sg, sm, sv)

    def unpack_small(a):
        return {"norm_mix": a[0:1], "pool_scale": a[1:2], "norm_ffn": a[2:3], "norm_final": a[3],
                "ffn_conv_b": a[4:12, :SH_UP].reshape(1, FF2), "conv_w": a[12:15, :128].reshape(1, 3, 128),
                "ffn_conv_w": a[15:18, :SH_UP].reshape(1, 3, SH_UP)}

    small = [unpack_small(a) for a in small_out]
    grads = {"norm_mix": g_norm_mix, "pool_scale": g_pool_scale, "norm_ffn": g_norm_ffn, "norm_final": g_norm_final,
             "ffn_conv_b": g_fcb, "conv_w": g_conv_w.reshape(1, 3, 128), "ffn_conv_w": g_fcw.reshape(1, 3, SH_UP)}
    order = ["norm_mix", "w_in", "pool_w", "pool_scale", "w_pool_proj", "conv_w", "w_conv_out", "w_o", "norm_ffn",
             "w_up", "ffn_conv_w", "ffn_conv_b", "w_down", "norm_final"]
    out = [loss, grad_x.reshape(nb, seq, D)]
    out += [big[nm][0] if nm in big else grads[nm] for nm in order]
    for idx in range(3):
        out += [big[nm][idx + 1] if nm in big else small[idx][nm] for nm in order]
    return tuple(out)
```

```python
import functools

import jax
import jax.numpy as jnp
from jax import lax
from jax.experimental import pallas as pl
from jax.experimental.pallas import tpu as pltpu
from jax.experimental.pallas import tpu_sc as plsc

F32 = jnp.float32
BF16 = jnp.bfloat16

NDEV = 8
D = 1024
NG = 4
CG = 256
WINS = (2, 4, 8, 16)
DIN = 6 * D
SH_IN = DIN // NDEV
NZT = DIN // CG
FF2 = 5632
SH_UP = FF2 // NDEV
FF = FF2 // 2
NCH = 4
SH_DN = FF // NDEV
RMS_EPS = 1e-6
HALO = 16

ADAM_LR = 0.001
ADAM_B1 = 0.9
ADAM_B2 = 0.999
ADAM_EPS = 1e-08
ADAM_WD = 0.01
ADAM_STEP = 10

TM_IN = 1024
TM_MIX = 256
TM_FFN = 256
TK_WGRAD = 512
MIX_POOL_PROJ, MIX_CONV_OUT, MIX_O = 0, 1, 2
MIX_COLS = 3 * D + CG
VMEM_BIG = 56 * 1024 * 1024
MESH = pl.DeviceIdType.MESH
ANY = pl.BlockSpec(memory_space=pl.ANY)


def _cparams(n_axes, vmem=None):
    return pltpu.CompilerParams(dimension_semantics=("arbitrary",) * n_axes, vmem_limit_bytes=vmem)


def _dot(a, b):
    return jnp.dot(a, b, preferred_element_type=F32)


def _dot_nt(a, b):
    return lax.dot_general(a, b, (((1,), (1,)), ((), ())), preferred_element_type=F32)


def _dot_tn(a, b):
    return lax.dot_general(a, b, (((0,), (0,)), ((), ())), preferred_element_type=F32)


def _shift_down(ext, s, lead):
    return pltpu.roll(ext, s, 0)[lead:]


def _shift_up(ext, s, tm):
    n = ext.shape[0]
    return pltpu.roll(ext, n - s, 0)[:tm]


def _rms_inv(x):
    return lax.rsqrt(jnp.mean(x * x, axis=-1, keepdims=True) + RMS_EPS)


def _rms_bwd(dh, xn, inv, g):
    dxn = dh * g
    return inv * (dxn - xn * jnp.mean(dxn * xn, axis=-1, keepdims=True))


def _pos():
    return lax.axis_index("x"), lax.axis_index("y"), lax.axis_index("c")


def _handshake(peers):
    barrier = pltpu.get_barrier_semaphore()
    for peer in peers:
        pl.semaphore_signal(barrier, inc=1, device_id=peer, device_id_type=MESH)
    pl.semaphore_wait(barrier, len(peers))


def _sequencer(body, out_type, n_sems, name, collective_id):
    return pl.kernel(
        body, out_type=out_type, mesh=plsc.ScalarSubcoreMesh(axis_name="sequencer", num_cores=1), name=name,
        scratch_types=[pltpu.SemaphoreType.DMA((n_sems,)), pltpu.SemaphoreType.DMA((n_sems,))],
        compiler_params=pltpu.CompilerParams(collective_id=collective_id))


def all_gather_blocks(shard, name, collective_id):
    def body(in_ref, out_ref, send_sems, recv_sems):
        x, y, c = _pos()
        sibling = (x, y, 1 - c)
        chips = [(1 - x, y), (x, 1 - y), (1 - x, 1 - y)]
        _handshake([sibling] + [(*chip, c) for chip in chips])

        def copy(k, block, to, src=None):
            slot = out_ref.at[4 * block[0] + 2 * block[1] + block[2]]
            return pltpu.make_async_remote_copy(
                src_ref=slot if src is None else src, dst_ref=slot,
                send_sem=send_sems.at[k], recv_sem=recv_sems.at[k], device_id=to, device_id_type=MESH)

        mine = pltpu.make_async_copy(in_ref, out_ref.at[4 * x + 2 * y + c], send_sems.at[7])
        mine.start()
        first = [copy(0, (x, y, c), sibling, src=in_ref)]
        first += [copy(1 + j, (x, y, c), (*chip, c), src=in_ref) for j, chip in enumerate(chips)]
        for cp in first:
            cp.start()
        passed = []
        for j, chip in enumerate(chips):
            copy(1 + j, (*chip, c), (x, y, c)).wait_recv()
            fw = copy(4 + j, (*chip, c), sibling)
            fw.start()
            passed.append(fw)
        copy(0, (x, y, 1 - c), (x, y, c)).wait_recv()
        for j, chip in enumerate(chips):
            copy(4 + j, (*chip, 1 - c), (x, y, c)).wait_recv()
        for cp in first + passed:
            cp.wait_send()
        mine.wait()

    out = jax.ShapeDtypeStruct((NDEV,) + shard.shape, shard.dtype)
    return _sequencer(body, out, 8, name, collective_id)(shard)


def _exchange_small(v, reduce, name):
    rows = v.shape[0]

    def body(v_ref, out_ref, slots, send_sems, recv_sems, local_sem):
        x, y, c = _pos()
        me = 4 * x + 2 * y + c
        mine = pltpu.make_async_copy(v_ref, slots.at[me], local_sem)
        mine.start()
        offs = [(dx, dy, dc) for dx in (0, 1) for dy in (0, 1) for dc in (0, 1)][1:]

        def copy(k, src_slot, to):
            return pltpu.make_async_remote_copy(
                src_ref=v_ref, dst_ref=slots.at[src_slot], send_sem=send_sems.at[k], recv_sem=recv_sems.at[k],
                device_id=to, device_id_type=MESH)

        sends = []
        for k, (dx, dy, dc) in enumerate(offs):
            cp = copy(k, me, (x ^ dx, y ^ dy, c ^ dc))
            cp.start()
            sends.append(cp)
        for k, (dx, dy, dc) in enumerate(offs):
            copy(k, 4 * (x ^ dx) + 2 * (y ^ dy) + (c ^ dc), (x, y, c)).wait_recv()
        for cp in sends:
            cp.wait_send()
        mine.wait()
        if reduce:
            acc = slots[0]
            for d in range(1, NDEV):
                acc = acc + slots[d]
            out_ref[...] = acc
        else:
            out_ref[...] = slots[...]

    out = jax.ShapeDtypeStruct((rows, D) if reduce else (NDEV, rows, D), F32)
    return pl.pallas_call(
        body, name=name, out_shape=out,
        in_specs=[pl.BlockSpec(memory_space=pltpu.VMEM)], out_specs=pl.BlockSpec(memory_space=pltpu.VMEM),
        scratch_shapes=[pltpu.VMEM((NDEV, rows, D), F32), pltpu.SemaphoreType.DMA((7,)),
                        pltpu.SemaphoreType.DMA((7,)), pltpu.SemaphoreType.DMA],
    )(v)


def reduce_scatter_d2d(grads, name, collective_id):
    n = len(grads)

    def body(*refs):
        ins, outs = refs[:n], refs[n:2 * n]
        send_sems, recv_sems = refs[2 * n:]
        x, y, c = _pos()
        _handshake([(x, y, 1 - c)])
        cps = []
        for w in range(n):
            cp = pltpu.make_async_remote_copy(
                src_ref=ins[w].at[1 - c], dst_ref=outs[w], send_sem=send_sems.at[w], recv_sem=recv_sems.at[w],
                device_id=(x, y, 1 - c), device_id_type=MESH)
            cp.start()
            cps.append(cp)
        for cp in cps:
            cp.wait_recv()
        for cp in cps:
            cp.wait_send()

    out = [jax.ShapeDtypeStruct(g.shape[1:], F32) for g in grads]
    return _sequencer(body, out, n, name, collective_id)(*grads)


def reduce_scatter_ici(parts, name, collective_id):
    n = len(parts)

    def body(*refs):
        ins, outs = refs[:n], refs[n:2 * n]
        send_sems, recv_sems = refs[2 * n:]
        x, y, c = _pos()
        offs = [(1, 0), (0, 1), (1, 1)]
        _handshake([(x ^ dx, y ^ dy, c) for dx, dy in offs])
        cps = []
        for w in range(n):
            for k, (dx, dy) in enumerate(offs):
                ox, oy = x ^ dx, y ^ dy
                cp = pltpu.make_async_remote_copy(
                    src_ref=ins[w].at[2 * ox + oy], dst_ref=outs[w].at[2 * x + y],
                    send_sem=send_sems.at[3 * w + k], recv_sem=recv_sems.at[3 * w + k],
                    device_id=(ox, oy, c), device_id_type=MESH)
                cp.start()
                cps.append((cp, w, k, ox, oy))
        for cp, w, k, ox, oy in cps:
            pltpu.make_async_remote_copy(
                src_ref=ins[w].at[2 * ox + oy], dst_ref=outs[w].at[2 * ox + oy],
                send_sem=send_sems.at[3 * w + k], recv_sem=recv_sems.at[3 * w + k],
                device_id=(ox, oy, c), device_id_type=MESH).wait_recv()
        for cp, *_ in cps:
            cp.wait_send()

    out = [jax.ShapeDtypeStruct(p.shape, BF16) for p in parts]
    return _sequencer(body, out, 3 * n, name, collective_id)(*parts)


def fwd_in(x, g1, w_in_g, tm):
    t = x.shape[0]

    def body(x_ref, g_ref, w_ref, z_ref, h_ref):
        @pl.when(pl.program_id(1) == 0)
        def _():
            xf = x_ref[...]
            h_ref[...] = (xf * _rms_inv(xf) * g_ref[...]).astype(BF16)

        r = _dot(h_ref[...], w_ref[...])
        for q in range(3):
            z_ref[q] = r[:, q * CG:(q + 1) * CG].astype(BF16)

    return pl.pallas_call(
        body, name="fwd_in", grid=(t // tm, NDEV),
        in_specs=[pl.BlockSpec((tm, D), lambda i, j: (i, 0)), pl.BlockSpec((1, D), lambda i, j: (0, 0)),
                  pl.BlockSpec((None, D, SH_IN), lambda i, j: (j, 0, 0))],
        out_specs=[pl.BlockSpec((3, tm, CG), lambda i, j: (j, i, 0)), pl.BlockSpec((tm, D), lambda i, j: (i, 0))],
        out_shape=[jax.ShapeDtypeStruct((NZT, t, CG), BF16), jax.ShapeDtypeStruct((t, D), BF16)],
        compiler_params=_cparams(2, VMEM_BIG),
    )(x, g1, w_in_g)


def _pool_tile(z_ref, zh_ref, g, win, keep_hist, cnt):
    zt = z_ref[g].astype(F32)
    ext = jnp.concatenate([zh_ref[g].astype(F32) * keep_hist, zt], axis=0)
    s, sh = ext, 1
    while sh < win:
        s = s + pltpu.roll(s, sh, 0)
        sh *= 2
    return s[HALO:] / cnt - zt


def _conv_taps(ext, cur, w_ref, lanes, lead):
    x1 = _shift_down(ext, 1, lead)
    x2 = _shift_down(ext, 2, lead)
    out = w_ref[2:3, lanes] * cur + w_ref[1:2, lanes] * x1 + w_ref[0:1, lanes] * x2
    return out, x1, x2


def fwd_mix(z, x, pool_w, pool_scale, conv_w, wmix, tm, seq):
    t = x.shape[0]
    tps = seq // tm
    hb = tm // HALO

    def body(z_ref, zph_ref, zcvh_ref, x_ref, pw_ref, ps_ref, wpp_ref, cw_ref, wco_ref, wo_ref,
             x1_ref, yp_ref, yc_ref):
        it = pl.program_id(0) % tps
        keep_hist = jnp.where(it == 0, 0.0, 1.0)
        pos = it * tm + lax.broadcasted_iota(jnp.int32, (tm, 1), 0)
        p2 = []
        for g, win in enumerate(WINS):
            cnt = jnp.minimum(pos + 1, win).astype(F32)
            p = _pool_tile(z_ref, zph_ref, g, win, keep_hist, cnt)
            lanes = slice(g * CG, (g + 1) * CG)
            p2.append((_dot(p.astype(BF16), pw_ref[g]) * ps_ref[:, lanes]).astype(BF16))
        y_pool = _dot(jnp.concatenate(p2, axis=1), wpp_ref[...])
        u = []
        for q in range(NG):
            lanes = slice(q * CG, (q + 1) * CG)
            cv = z_ref[8 + q].astype(F32) * z_ref[12 + q].astype(F32)
            cvh = zcvh_ref[q].astype(F32) * zcvh_ref[4 + q].astype(F32) * keep_hist
            cc, _, _ = _conv_taps(jnp.concatenate([cvh, cv], axis=0), cv, cw_ref, lanes, HALO)
            u.append((z_ref[4 + q].astype(F32) * cc).astype(BF16))
        y_conv = _dot(jnp.concatenate(u, axis=1), wco_ref[...])
        ypb, ycb = y_pool.astype(BF16), y_conv.astype(BF16)
        yp_ref[...] = ypb
        yc_ref[...] = ycb
        merged = []
        for q in range(NG):
            lanes = slice(q * CG, (q + 1) * CG)
            sp = jax.nn.sigmoid(z_ref[16 + q].astype(F32))
            sc = jax.nn.sigmoid(z_ref[20 + q].astype(F32))
            merged.append((sp * ypb[:, lanes].astype(F32) + sc * ycb[:, lanes].astype(F32)).astype(BF16))
        x1_ref[...] = x_ref[...] + _dot(jnp.concatenate(merged, axis=1), wo_ref[...])

    def hist(i):
        return jnp.maximum(i * hb - 1, 0)

    const2 = lambda i: (0, 0)
    return pl.pallas_call(
        body, name="fwd_mix", grid=(t // tm,),
        in_specs=[pl.BlockSpec((NZT, tm, CG), lambda i: (0, i, 0)),
                  pl.BlockSpec((NG, HALO, CG), lambda i: (0, hist(i), 0)),
                  pl.BlockSpec((2 * NG, HALO, CG), lambda i: (1, hist(i), 0)),
                  pl.BlockSpec((tm, D), lambda i: (i, 0)),
                  pl.BlockSpec((NG, CG, CG), lambda i: (0, 0, 0)), pl.BlockSpec((1, D), const2),
                  pl.BlockSpec((D, D), lambda i: (0, MIX_POOL_PROJ)), pl.BlockSpec((3, D), const2),
                  pl.BlockSpec((D, D), lambda i: (0, MIX_CONV_OUT)), pl.BlockSpec((D, D), lambda i: (0, MIX_O))],
        out_specs=[pl.BlockSpec((tm, D), lambda i: (i, 0))] * 3,
        out_shape=[jax.ShapeDtypeStruct((t, D), F32), jax.ShapeDtypeStruct((t, D), BF16),
                   jax.ShapeDtypeStruct((t, D), BF16)],
        compiler_params=_cparams(1, VMEM_BIG),
    )(z, z, z, x, pool_w, pool_scale, wmix, conv_w, wmix, wmix)


def fwd_ffn(x1, g2, w_up_g, fcw, fcb, w_dn, gf, tgt, tm, seq):
    t = x1.shape[0]
    tps = seq // tm

    def body(x1_ref, g2_ref, wup_ref, fcw_ref, fcb_ref, wdn_ref, gf_ref, tgt_ref,
             up_ref, act_ref, h2_ref, dx2_ref, vec_ref, hist_ref):
        i = pl.program_id(0)
        keep_hist = jnp.where(i % tps == 0, 0.0, 1.0)

        @pl.when(i == 0)
        def _():
            vec_ref[...] = jnp.zeros_like(vec_ref)
            hist_ref[...] = jnp.zeros_like(hist_ref)

        x1v = x1_ref[...]
        h2 = (x1v * _rms_inv(x1v) * g2_ref[...]).astype(BF16)
        h2_ref[...] = h2
        d = jnp.zeros((tm, D), F32)
        lanes = slice(0, SH_UP)
        for k in range(NCH):
            conv = []
            for kk in (k, NCH + k):
                ub = _dot(h2, wup_ref[kk]).astype(BF16)
                up_ref[kk] = ub
                uf = ub.astype(F32)
                ext = jnp.concatenate([hist_ref[kk] * keep_hist, uf], axis=0)
                hist_ref[kk] = uf[tm - 8:]
                cc, _, _ = _conv_taps(ext, uf, fcw_ref.at[kk], lanes, 8)
                conv.append(cc + fcb_ref[kk])
            a = (conv[0] * jax.nn.sigmoid(conv[0]) * conv[1]).astype(BF16)
            act_ref[k] = a
            d = d + _dot(a, wdn_ref[k])
        x2 = x1v + d
        inv3 = _rms_inv(x2)
        xn = x2 * inv3
        diff = xn * gf_ref[...] - tgt_ref[...]
        dy = diff * (1.0 / D)
        vec_ref[0:1, :] += jnp.sum(dy * xn, axis=0, keepdims=True)
        vec_ref[1:2, :] += 0.5 * jnp.sum(jnp.mean(diff * diff, axis=-1))
        dx2_ref[...] = _rms_bwd(dy, xn, inv3, gf_ref[...])

    tile = lambda i: (i, 0)
    const2 = lambda i: (0, 0)
    const3 = lambda i: (0, 0, 0)
    return pl.pallas_call(
        body, name="fwd_ffn", grid=(t // tm,),
        in_specs=[pl.BlockSpec((tm, D), tile), pl.BlockSpec((1, D), const2),
                  pl.BlockSpec((NDEV, D, SH_UP), const3), pl.BlockSpec((NDEV, 3, SH_UP), const3),
                  pl.BlockSpec((NDEV, 1, SH_UP), const3), pl.BlockSpec((NCH, SH_UP, D), const3),
                  pl.BlockSpec((1, D), const2), pl.BlockSpec((tm, D), tile)],
        out_specs=[pl.BlockSpec((NDEV, tm, SH_UP), lambda i: (0, i, 0)),
                   pl.BlockSpec((NCH, tm, SH_UP), lambda i: (0, i, 0)),
                   pl.BlockSpec((tm, D), tile), pl.BlockSpec((tm, D), tile), pl.BlockSpec((8, D), const2)],
        out_shape=[jax.ShapeDtypeStruct((NDEV, t, SH_UP), BF16), jax.ShapeDtypeStruct((NCH, t, SH_UP), BF16),
                   jax.ShapeDtypeStruct((t, D), BF16), jax.ShapeDtypeStruct((t, D), F32),
                   jax.ShapeDtypeStruct((8, D), F32)],
        scratch_shapes=[pltpu.VMEM((NDEV, 8, SH_UP), F32)],
        compiler_params=_cparams(1, VMEM_BIG),
    )(x1, g2, w_up_g, fcw, fcb, w_dn, gf, tgt)


def bwd_ffn(dx2, x1, g2, up, w_up_g, fcw, fcb, w_dn, tm, seq):
    t = x1.shape[0]
    nt = t // tm
    tps = seq // tm
    hb = tm // HALO

    def body(dx2_ref, x1_ref, g2_ref, up_ref, uph_ref, wup_ref, fcw_ref, fcb_ref, wdn_ref,
             dup_ref, dx1_ref, gvec_ref, gn_ref, carry_ref):
        i = pl.program_id(0)
        it = (nt - 1 - i) % tps
        keep_hist = jnp.where(it == 0, 0.0, 1.0)
        keep_next = jnp.where(it == tps - 1, 0.0, 1.0)

        @pl.when(i == 0)
        def _():
            gvec_ref[...] = jnp.zeros_like(gvec_ref)
            gn_ref[...] = jnp.zeros_like(gn_ref)
            carry_ref[...] = jnp.zeros_like(carry_ref)

        dx2v = dx2_ref[...]
        dxb = dx2v.astype(BF16)
        lanes = slice(0, SH_UP)
        dh2 = jnp.zeros((tm, D), F32)
        for k in range(NCH):
            pre, taps = [], []
            for kk in (k, NCH + k):
                uf = up_ref[kk].astype(F32)
                ext = jnp.concatenate([uph_ref[kk].astype(F32)[8:] * keep_hist, uf], axis=0)
                cc, u1, u2 = _conv_taps(ext, uf, fcw_ref.at[kk], lanes, 8)
                pre.append(cc + fcb_ref[kk])
                taps.append((u2, u1, uf))
            sg = jax.nn.sigmoid(pre[0])
            dact = _dot_nt(dxb, wdn_ref[k])
            dpre = [dact * pre[1] * (sg * (1.0 + pre[0] * (1.0 - sg))), dact * (pre[0] * sg)]
            for s, kk in enumerate((k, NCH + k)):
                dc = dpre[s]
                gvec_ref[kk:kk + 1, lanes] += jnp.sum(dc, axis=0, keepdims=True)
                for tap in range(3):
                    row = NDEV * (tap + 1) + kk
                    gvec_ref[row:row + 1, lanes] += jnp.sum(dc * taps[s][tap], axis=0, keepdims=True)
                ext = jnp.concatenate([dc, carry_ref[kk] * keep_next], axis=0)
                carry_ref[kk] = dc[:8]
                w = fcw_ref.at[kk]
                du = w[2:3, :] * dc + w[1:2, :] * _shift_up(ext, 1, tm) + w[0:1, :] * _shift_up(ext, 2, tm)
                dub = du.astype(BF16)
                dup_ref[kk] = dub
                dh2 = dh2 + _dot_nt(dub, wup_ref[kk])
        x1v = x1_ref[...]
        inv2 = _rms_inv(x1v)
        xn = x1v * inv2
        gn_ref[0:1, :] += jnp.sum(dh2 * xn, axis=0, keepdims=True)
        dx1_ref[...] = dx2v + _rms_bwd(dh2, xn, inv2, g2_ref[...])

    rev = lambda i: (nt - 1 - i, 0)
    const2 = lambda i: (0, 0)
    const3 = lambda i: (0, 0, 0)
    return pl.pallas_call(
        body, name="bwd_ffn", grid=(nt,),
        in_specs=[pl.BlockSpec((tm, D), rev), pl.BlockSpec((tm, D), rev), pl.BlockSpec((1, D), const2),
                  pl.BlockSpec((NDEV, tm, SH_UP), lambda i: (0, nt - 1 - i, 0)),
                  pl.BlockSpec((NDEV, HALO, SH_UP), lambda i: (0, jnp.maximum((nt - 1 - i) * hb - 1, 0), 0)),
                  pl.BlockSpec((NDEV, D, SH_UP), const3), pl.BlockSpec((NDEV, 3, SH_UP), const3),
                  pl.BlockSpec((NDEV, 1, SH_UP), const3), pl.BlockSpec((NCH, SH_UP, D), const3)],
        out_specs=[pl.BlockSpec((NDEV, tm, SH_UP), lambda i: (0, nt - 1 - i, 0)), pl.BlockSpec((tm, D), rev),
                   pl.BlockSpec((4 * NDEV, D), const2), pl.BlockSpec((8, D), const2)],
        out_shape=[jax.ShapeDtypeStruct((NDEV, t, SH_UP), BF16), jax.ShapeDtypeStruct((t, D), F32),
                   jax.ShapeDtypeStruct((4 * NDEV, D), F32), jax.ShapeDtypeStruct((8, D), F32)],
        scratch_shapes=[pltpu.VMEM((NDEV, 8, SH_UP), F32)],
        compiler_params=_cparams(1, VMEM_BIG),
    )(dx2, x1, g2, up, up, w_up_g, fcw, fcb, w_dn)


def bwd_mix(dx1, z, y_pool, y_conv, pool_w, pool_scale, conv_w, wmix, tm, seq):
    t = dx1.shape[0]
    nt = t // tm
    tps = seq // tm
    hb = tm // HALO

    def body(da_ref, z_ref, zph_ref, zcvh_ref, yp_ref, yc_ref, pw_ref, ps_ref, wpp_ref, cw_ref, wco_ref, wo_ref,
             dz_ref, mg_ref, p2_ref, u_ref, dyp_ref, dyc_ref, p_ref, dpw_ref, gvec_ref, cp_ref, cc_ref):
        i = pl.program_id(0)
        it = (nt - 1 - i) % tps
        keep_hist = jnp.where(it == 0, 0.0, 1.0)
        keep_next = jnp.where(it == tps - 1, 0.0, 1.0)
        pos = it * tm + lax.broadcasted_iota(jnp.int32, (tm, 1), 0)

        @pl.when(i == 0)
        def _():
            gvec_ref[...] = jnp.zeros_like(gvec_ref)
            cp_ref[...] = jnp.zeros_like(cp_ref)
            cc_ref[...] = jnp.zeros_like(cc_ref)

        dm = _dot_nt(da_ref[...].astype(BF16), wo_ref[...])
        merged, dyp, dyc = [], [], []
        for q in range(NG):
            lanes = slice(q * CG, (q + 1) * CG)
            sp = jax.nn.sigmoid(z_ref[16 + q].astype(F32))
            sc = jax.nn.sigmoid(z_ref[20 + q].astype(F32))
            yp = yp_ref[:, lanes].astype(F32)
            yc = yc_ref[:, lanes].astype(F32)
            dmq = dm[:, lanes]
            merged.append((sp * yp + sc * yc).astype(BF16))
            dyp.append((dmq * sp).astype(BF16))
            dyc.append((dmq * sc).astype(BF16))
            dz_ref[16 + q] = (dmq * yp * (sp * (1.0 - sp))).astype(BF16)
            dz_ref[20 + q] = (dmq * yc * (sc * (1.0 - sc))).astype(BF16)
        mg_ref[...] = jnp.concatenate(merged, axis=1)
        dypb = jnp.concatenate(dyp, axis=1)
        dycb = jnp.concatenate(dyc, axis=1)
        dyp_ref[...] = dypb
        dyc_ref[...] = dycb

        dp2 = _dot_nt(dypb, wpp_ref[...])
        p2 = []
        for g, win in enumerate(WINS):
            lanes = slice(g * CG, (g + 1) * CG)
            cnt = jnp.minimum(pos + 1, win).astype(F32)
            p = _pool_tile(z_ref, zph_ref, g, win, keep_hist, cnt)
            pb = p.astype(BF16)
            p_ref[g] = pb
            pw = _dot(pb, pw_ref[g])
            p2.append((pw * ps_ref[:, lanes]).astype(BF16))
            dp2g = dp2[:, lanes]
            gvec_ref[0:1, lanes] += jnp.sum(dp2g * pw, axis=0, keepdims=True)
            dpwb = (dp2g * ps_ref[:, lanes]).astype(BF16)
            dpw_ref[g] = dpwb
            dp = _dot_nt(dpwb, pw_ref[g])
            qv = dp / cnt
            ext = jnp.concatenate([qv, cp_ref[g] * keep_next], axis=0)
            cp_ref[g] = qv[:HALO]
            n = tm + HALO
            s, sh = ext, 1
            while sh < win:
                s = s + pltpu.roll(s, n - sh, 0)
                sh *= 2
            dz_ref[g] = (s[:tm] - dp).astype(BF16)
        p2_ref[...] = jnp.concatenate(p2, axis=1)

        du = _dot_nt(dycb, wco_ref[...])
        u = []
        for q in range(NG):
            lanes = slice(q * CG, (q + 1) * CG)
            zb = z_ref[4 + q].astype(F32)
            zc = z_ref[8 + q].astype(F32)
            zv = z_ref[12 + q].astype(F32)
            cv = zc * zv
            cvh = zcvh_ref[q].astype(F32) * zcvh_ref[4 + q].astype(F32) * keep_hist
            cc, cv1, cv2 = _conv_taps(jnp.concatenate([cvh, cv], axis=0), cv, cw_ref, lanes, HALO)
            u.append((zb * cc).astype(BF16))
            duq = du[:, lanes]
            dz_ref[4 + q] = (duq * cc).astype(BF16)
            dcc = duq * zb
            for tap, src in enumerate((cv2, cv1, cv)):
                gvec_ref[tap + 1:tap + 2, lanes] += jnp.sum(dcc * src, axis=0, keepdims=True)
            ext = jnp.concatenate([dcc, cc_ref[:, lanes] * keep_next], axis=0)
            cc_ref[:, lanes] = dcc[:8]
            dcv = (cw_ref[2:3, lanes] * dcc + cw_ref[1:2, lanes] * _shift_up(ext, 1, tm)
                   + cw_ref[0:1, lanes] * _shift_up(ext, 2, tm))
            dz_ref[8 + q] = (dcv * zv).astype(BF16)
            dz_ref[12 + q] = (dcv * zc).astype(BF16)
        u_ref[...] = jnp.concatenate(u, axis=1)

    def hist(i):
        return jnp.maximum((nt - 1 - i) * hb - 1, 0)

    rev = lambda i: (nt - 1 - i, 0)
    rev3 = lambda i: (0, nt - 1 - i, 0)
    const2 = lambda i: (0, 0)
    tok = jax.ShapeDtypeStruct((t, D), BF16)
    grp = jax.ShapeDtypeStruct((NG, t, CG), BF16)
    return pl.pallas_call(
        body, name="bwd_mix", grid=(nt,),
        in_specs=[pl.BlockSpec((tm, D), rev), pl.BlockSpec((NZT, tm, CG), rev3),
                  pl.BlockSpec((NG, HALO, CG), lambda i: (0, hist(i), 0)),
                  pl.BlockSpec((2 * NG, HALO, CG), lambda i: (1, hist(i), 0)),
                  pl.BlockSpec((tm, D), rev), pl.BlockSpec((tm, D), rev),
                  pl.BlockSpec((NG, CG, CG), lambda i: (0, 0, 0)), pl.BlockSpec((1, D), const2),
                  pl.BlockSpec((D, D), lambda i: (0, MIX_POOL_PROJ)), pl.BlockSpec((3, D), const2),
                  pl.BlockSpec((D, D), lambda i: (0, MIX_CONV_OUT)), pl.BlockSpec((D, D), lambda i: (0, MIX_O))],
        out_specs=[pl.BlockSpec((NZT, tm, CG), rev3)] + [pl.BlockSpec((tm, D), rev)] * 5
                  + [pl.BlockSpec((NG, tm, CG), rev3)] * 2 + [pl.BlockSpec((8, D), const2)],
        out_shape=[jax.ShapeDtypeStruct((NZT, t, CG), BF16), tok, tok, tok, tok, tok, grp, grp,
                   jax.ShapeDtypeStruct((8, D), F32)],
        scratch_shapes=[pltpu.VMEM((NG, HALO, CG), F32), pltpu.VMEM((8, D), F32)],
        compiler_params=_cparams(1, VMEM_BIG),
    )(dx1, z, z, z, y_pool, y_conv, pool_w, pool_scale, wmix, conv_w, wmix, wmix)


def bwd_in(dz, w_in_g, dx1, x, g1, tm):
    t = x.shape[0]

    def body(dz_ref, w_ref, dx1_ref, x_ref, g_ref, gx_ref, gn_ref, acc_ref):
        i, j = pl.program_id(0), pl.program_id(1)

        @pl.when((i == 0) & (j == 0))
        def _():
            gn_ref[...] = jnp.zeros_like(gn_ref)

        @pl.when(j == 0)
        def _():
            acc_ref[...] = jnp.zeros_like(acc_ref)

        dzc = jnp.concatenate([dz_ref[q] for q in range(3)], axis=1)
        acc_ref[...] += _dot_nt(dzc, w_ref[...])

        @pl.when(j == NDEV - 1)
        def _():
            xv = x_ref[...]
            inv = _rms_inv(xv)
            xn = xv * inv
            dh = acc_ref[...]
            gn_ref[0:1, :] += jnp.sum(dh * xn, axis=0, keepdims=True)
            gx_ref[...] = dx1_ref[...] + _rms_bwd(dh, xn, inv, g_ref[...])

    tile = lambda i, j: (i, 0)
    return pl.pallas_call(
        body, name="bwd_in", grid=(t // tm, NDEV),
        in_specs=[pl.BlockSpec((3, tm, CG), lambda i, j: (j, i, 0)),
                  pl.BlockSpec((None, D, SH_IN), lambda i, j: (j, 0, 0)),
                  pl.BlockSpec((tm, D), tile), pl.BlockSpec((tm, D), tile), pl.BlockSpec((1, D), lambda i, j: (0, 0))],
        out_specs=[pl.BlockSpec((tm, D), tile), pl.BlockSpec((8, D), lambda i, j: (0, 0))],
        out_shape=[jax.ShapeDtypeStruct((t, D), F32), jax.ShapeDtypeStruct((8, D), F32)],
        scratch_shapes=[pltpu.VMEM((tm, D), F32)],
        compiler_params=_cparams(2, VMEM_BIG),
    )(dz, w_in_g, dx1, x, g1)


def _slot(j):
    return j % 2, j // 2


def wgrad_cols(a, b, pieces, width, name, tk):
    t, m = a.shape

    def body(a_ref, b_ref, o_ref):
        @pl.when(pl.program_id(1) == 0)
        def _():
            o_ref[...] = jnp.zeros_like(o_ref)

        av = a_ref[...]
        w = width // pieces
        for q in range(pieces):
            o_ref[:, q * w:(q + 1) * w] += _dot_tn(av, b_ref[q])

    return pl.pallas_call(
        body, name=name, grid=(NDEV, t // tk),
        in_specs=[pl.BlockSpec((tk, m), lambda j, k: (k, 0)),
                  pl.BlockSpec((pieces, tk, width // pieces), lambda j, k: (j, k, 0))],
        out_specs=pl.BlockSpec((None, None, m, width), lambda j, k: (j % 2, j // 2, 0, 0)),
        out_shape=jax.ShapeDtypeStruct((2, 4, m, width), F32),
        compiler_params=_cparams(2, VMEM_BIG),
    )(a, b)


def wgrad_down(act, dx2, tk):
    t = dx2.shape[0]

    def body(a_ref, b_ref, o_ref, acc_ref):
        kt = pl.program_id(1)

        @pl.when(kt == 0)
        def _():
            acc_ref[...] = jnp.zeros_like(acc_ref)

        acc_ref[...] += _dot_tn(a_ref[...], b_ref[...].astype(BF16))

        @pl.when(kt == pl.num_programs(1) - 1)
        def _():
            o_ref[0] = acc_ref[:SH_DN]
            o_ref[1] = acc_ref[SH_DN:]

    return pl.pallas_call(
        body, name="wgrad_down", grid=(NCH, t // tk),
        in_specs=[pl.BlockSpec((None, tk, SH_UP), lambda k, kt: (k, kt, 0)), pl.BlockSpec((tk, D), lambda k, kt: (kt, 0))],
        out_specs=pl.BlockSpec((2, None, SH_DN, D), lambda k, kt: (0, k, 0, 0)),
        out_shape=jax.ShapeDtypeStruct((2, 4, SH_DN, D), F32),
        scratch_shapes=[pltpu.VMEM((SH_UP, D), F32)],
        compiler_params=_cparams(2, VMEM_BIG),
    )(act, dx2)


def wgrad_square(a, b, name, tk):
    t = a.shape[0]

    def body(a_ref, b_ref, o_ref, acc_ref):
        kt = pl.program_id(0)

        @pl.when(kt == 0)
        def _():
            acc_ref[...] = jnp.zeros_like(acc_ref)

        acc_ref[...] += _dot_tn(a_ref[...], b_ref[...].astype(BF16))

        @pl.when(kt == pl.num_programs(0) - 1)
        def _():
            for j in range(NDEV):
                cc, xy = _slot(j)
                o_ref[cc, xy] = acc_ref[j * 128:(j + 1) * 128]

    return pl.pallas_call(
        body, name=name, grid=(t // tk,),
        in_specs=[pl.BlockSpec((tk, D), lambda k: (k, 0)), pl.BlockSpec((tk, D), lambda k: (k, 0))],
        out_specs=pl.BlockSpec((2, 4, 128, D), lambda k: (0, 0, 0, 0)),
        out_shape=jax.ShapeDtypeStruct((2, 4, 128, D), F32),
        scratch_shapes=[pltpu.VMEM((D, D), F32)],
        compiler_params=_cparams(1, VMEM_BIG),
    )(a, b)


def wgrad_pool(p, dpw, tk):
    t = p.shape[1]

    def body(a_ref, b_ref, o_ref):
        @pl.when(pl.program_id(0) == 0)
        def _():
            o_ref[...] = jnp.zeros_like(o_ref)

        for g in range(NG):
            o_ref[g] += _dot_tn(a_ref[g], b_ref[g])

    return pl.pallas_call(
        body, name="wgrad_pool", grid=(t // tk,),
        in_specs=[pl.BlockSpec((NG, tk, CG), lambda k: (0, k, 0))] * 2,
        out_specs=pl.BlockSpec((NG, CG, CG), lambda k: (0, 0, 0)),
        out_shape=jax.ShapeDtypeStruct((NG, CG, CG), F32),
        compiler_params=_cparams(1),
    )(p, dpw)


def _adamw(w, g, m, v):
    m = ADAM_B1 * m + (1.0 - ADAM_B1) * g
    v = ADAM_B2 * v + (1.0 - ADAM_B2) * (g * g)
    m_hat = m / (1.0 - ADAM_B1 ** ADAM_STEP)
    v_hat = v / (1.0 - ADAM_B2 ** ADAM_STEP)
    delta = -ADAM_LR * (m_hat / (jnp.sqrt(v_hat) + ADAM_EPS) + ADAM_WD * w)
    return delta, m, v


def _row_block(r):
    return 256 if r % 256 == 0 else r


def chip_partial(place, g, from_sibling, name):
    _, _, r, c = g.shape
    br = _row_block(r)

    def body(place_ref, g_ref, s_ref, o_ref):
        o_ref[...] = (g_ref[...] + s_ref[...]).astype(BF16)

    return pl.pallas_call(
        body, name=name,
        grid_spec=pltpu.PrefetchScalarGridSpec(
            num_scalar_prefetch=1, grid=(4, r // br),
            in_specs=[pl.BlockSpec((None, None, br, c), lambda q, i, pr: (pr[0], q, i, 0)),
                      pl.BlockSpec((None, br, c), lambda q, i, pr: (q, i, 0))],
            out_specs=pl.BlockSpec((None, br, c), lambda q, i, pr: (q, i, 0))),
        out_shape=jax.ShapeDtypeStruct((4, r, c), BF16),
        compiler_params=_cparams(2),
    )(place, g, from_sibling)


def finish_adamw(place, g, from_sibling, from_chips, w, m, v, name):
    _, _, r, c = g.shape
    br = _row_block(r)

    def body(place_ref, g_ref, s_ref, c1_ref, c2_ref, c3_ref, w_ref, m_ref, v_ref, og_ref, od_ref, om_ref, ov_ref):
        grad = g_ref[...] + s_ref[...]
        for ref in (c1_ref, c2_ref, c3_ref):
            grad = grad + ref[...].astype(F32)
        og_ref[...] = grad
        od_ref[...], om_ref[...], ov_ref[...] = _adamw(w_ref[...], grad, m_ref[...], v_ref[...])

    def other(k):
        return pl.BlockSpec((None, br, c), lambda i, pr: (pr[1] ^ k, i, 0))

    row = pl.BlockSpec((br, c), lambda i, pr: (i, 0))
    out = jax.ShapeDtypeStruct((r, c), F32)
    return pl.pallas_call(
        body, name=name,
        grid_spec=pltpu.PrefetchScalarGridSpec(
            num_scalar_prefetch=1, grid=(r // br,),
            in_specs=[pl.BlockSpec((None, None, br, c), lambda i, pr: (pr[0], pr[1], i, 0)),
                      pl.BlockSpec((None, br, c), lambda i, pr: (pr[1], i, 0)),
                      other(1), other(2), other(3), row, row, row],
            out_specs=[row] * 4),
        out_shape=[out] * 4,
        compiler_params=_cparams(1),
    )(place, g, from_sibling, from_chips, from_chips, from_chips, w, m, v)


def adamw_small(items):
    n = len(items)

    def body(*refs):
        ins, outs = refs[:4 * n], refs[4 * n:]
        for i in range(n):
            w, g, m, v = (r[...] for r in ins[4 * i:4 * i + 4])
            outs[3 * i][...], outs[3 * i + 1][...], outs[3 * i + 2][...] = _adamw(w, g, m, v)

    out = [jax.ShapeDtypeStruct(it[0].shape, F32) for it in items for _ in range(3)]
    res = pl.pallas_call(body, name="adamw_small", out_shape=out)(*[a for it in items for a in it])
    return [res[3 * i:3 * i + 3] for i in range(n)]


def _pad_lanes(a):
    return jnp.pad(a, ((0, 0), (0, D - a.shape[1])))


def kernel(x, norm_mix, w_in, pool_w, pool_scale, w_pool_proj, conv_w, w_conv_out, w_o, norm_ffn, w_up, ffn_conv_w, ffn_conv_b, w_down, norm_final, loss_target, m_norm_mix, m_w_in, m_pool_w, m_pool_scale, m_w_pool_proj, m_conv_w, m_w_conv_out, m_w_o, m_norm_ffn, m_w_up, m_ffn_conv_w, m_ffn_conv_b, m_w_down, m_norm_final, v_norm_mix, v_w_in, v_pool_w, v_pool_scale, v_w_pool_proj, v_conv_w, v_w_conv_out, v_w_o, v_norm_ffn, v_w_up, v_ffn_conv_w, v_ffn_conv_b, v_w_down, v_norm_final):
    nb, seq, _ = x.shape
    t = nb * seq
    tm_in = min(TM_IN, t)
    tm_mix = min(TM_MIX, seq)
    tm_ffn = min(TM_FFN, seq)
    tk = min(TK_WGRAD, t)
    xt = x.reshape(t, D)
    tgt = loss_target.reshape(t, D)
    xi, yi, ci = _pos()
    me = 4 * xi + 2 * yi + ci
    place = jnp.stack([ci, 2 * xi + yi]).astype(jnp.int32)

    w_in_g = all_gather_blocks(w_in[0].astype(BF16), "all_gather_w_in", 0)
    mix_shard = jnp.concatenate(
        [w_pool_proj[0], w_conv_out[0], w_o[0], pool_w[0].reshape(NG * 32, CG)], axis=1).astype(BF16)
    wmix_g = all_gather_blocks(mix_shard, "all_gather_w_mix", 1)
    w_up_g = all_gather_blocks(w_up[0].astype(BF16), "all_gather_w_up", 2)
    w_dn_g = all_gather_blocks(w_down[0].astype(BF16), "all_gather_w_down", 3)
    taps = (jnp.pad(conv_w[0], ((0, 5), (0, D - 128))) + jnp.pad(ffn_conv_w[0], ((3, 2), (0, D - SH_UP))))
    taps_g = _exchange_small(taps, False, "all_gather_taps")
    wmix = wmix_g.reshape(D, MIX_COLS)
    pool_w_f = wmix_g[:, :, 3 * D:].reshape(NDEV, NG, 32, CG).transpose(1, 0, 2, 3).reshape(NG, CG, CG)
    w_dn_f = w_dn_g.reshape(NCH, SH_UP, D)
    conv_w_f = taps_g[:, 0:3, :128].transpose(1, 0, 2).reshape(3, D)
    fcw_f = taps_g[:, 3:6, :SH_UP]
    fcb_f = ffn_conv_b.reshape(NDEV, 1, SH_UP)
    gfin = norm_final.reshape(1, D)

    z, h1 = fwd_in(xt, norm_mix, w_in_g, tm_in)
    x1, y_pool, y_conv = fwd_mix(z, xt, pool_w_f, pool_scale, conv_w_f, wmix, tm_mix, seq)
    up, act, h2, dx2, ffn_vec = fwd_ffn(x1, norm_ffn, w_up_g, fcw_f, fcb_f, w_dn_f, gfin, tgt, tm_ffn, seq)

    def reduce_group(full, names, shards, tag, collective_id):
        from_sib = reduce_scatter_d2d(full, "reduce_scatter_d2d_" + tag, collective_id)
        parts = [chip_partial(place, g, s, "chip_partial_" + nm) for g, s, nm in zip(full, from_sib, names)]
        from_chips = reduce_scatter_ici(parts, "reduce_scatter_ici_" + tag, collective_id + 1)
        res = {}
        for nm, g, s, fc, (w, m, v) in zip(names, full, from_sib, from_chips, shards):
            rc = g.shape[2:]
            outs = finish_adamw(place, g, s, fc, w.reshape(rc), m.reshape(rc), v.reshape(rc), "adamw_" + nm)
            res[nm] = [o.reshape(w.shape) for o in outs]
        return res

    big = {}
    d_up, dx1, g_ffn_vec, g_nffn = bwd_ffn(dx2, x1, norm_ffn, up, w_up_g, fcw_f, fcb_f, w_dn_f, tm_ffn, seq)
    gw_dn = wgrad_down(act, dx2, tk)
    gw_up = wgrad_cols(h2, d_up, 1, SH_UP, "wgrad_up", tk)
    big.update(reduce_group([gw_dn, gw_up], ["w_down", "w_up"],
                            [(w_down, m_w_down, v_w_down), (w_up, m_w_up, v_w_up)], "ffn", 4))
    dz, merged, p2, u, dyp, dyc, p, dpw, g_mix_vec = bwd_mix(
        dx1, z, y_pool, y_conv, pool_w_f, pool_scale, conv_w_f, wmix, tm_mix, seq)
    gw_o = wgrad_square(merged, dx1, "wgrad_o", tk)
    gw_pp = wgrad_square(p2, dyp, "wgrad_pool_proj", tk)
    big.update(reduce_group([gw_o, gw_pp], ["w_o", "w_pool_proj"],
                            [(w_o, m_w_o, v_w_o), (w_pool_proj, m_w_pool_proj, v_w_pool_proj)], "mix_a", 6))
    gw_co = wgrad_square(u, dyc, "wgrad_conv_out", tk)
    gw_pool = wgrad_pool(p, dpw, tk).reshape(NG, 4, 2, 32, CG).transpose(2, 1, 0, 3, 4).reshape(2, 4, NG * 32, CG)
    big.update(reduce_group([gw_co, gw_pool], ["w_conv_out", "pool_w"],
                            [(w_conv_out, m_w_conv_out, v_w_conv_out), (pool_w, m_pool_w, v_pool_w)], "mix_b", 8))
    gw_in = wgrad_cols(h1, dz, 3, SH_IN, "wgrad_in", tk)
    big.update(reduce_group([gw_in], ["w_in"], [(w_in, m_w_in, v_w_in)], "in", 10))
    grad_x, g_nmix = bwd_in(dz, w_in_g, dx1, xt, norm_mix, tm_in)

    red = _exchange_small(jnp.concatenate([g_nmix, g_mix_vec, g_nffn, ffn_vec, g_ffn_vec], axis=0), True,
                          "all_reduce_small")
    g_norm_mix, g_pool_scale, g_norm_ffn = red[0:1], red[8:9], red[16:17]
    g_conv_w = lax.dynamic_slice(red, (9, me * 128), (3, 128))
    g_norm_final = red[24]
    loss = red[25, 0]
    g_fcb = red[32:40, :SH_UP].reshape(1, FF2)
    g_fcw = jnp.stack([lax.dynamic_slice(red, (40 + 8 * k + me, 0), (1, SH_UP))[0] for k in range(3)])
    grads = {"norm_mix": g_norm_mix, "pool_scale": g_pool_scale, "norm_ffn": g_norm_ffn, "norm_final": g_norm_final,
             "ffn_conv_b": g_fcb, "conv_w": g_conv_w.reshape(1, 3, 128), "ffn_conv_w": g_fcw.reshape(1, 3, SH_UP)}
    small_wmv = {"norm_mix": (norm_mix, m_norm_mix, v_norm_mix), "pool_scale": (pool_scale, m_pool_scale, v_pool_scale),
                 "norm_ffn": (norm_ffn, m_norm_ffn, v_norm_ffn), "norm_final": (norm_final, m_norm_final, v_norm_final),
                 "ffn_conv_b": (ffn_conv_b, m_ffn_conv_b, v_ffn_conv_b), "conv_w": (conv_w, m_conv_w, v_conv_w),
                 "ffn_conv_w": (ffn_conv_w, m_ffn_conv_w, v_ffn_conv_w)}
    small_names = list(small_wmv)
    flat2 = lambda a: a.reshape(-1, a.shape[-1])
    small_out = adamw_small([(flat2(small_wmv[nm][0]), flat2(grads[nm]), flat2(small_wmv[nm][1]),
                              flat2(small_wmv[nm][2])) for nm in small_names])
    small = {nm: [o.reshape(small_wmv[nm][0].shape) for o in outs] for nm, outs in zip(small_names, small_out)}

    order = ["norm_mix", "w_in", "pool_w", "pool_scale", "w_pool_proj", "conv_w", "w_conv_out", "w_o", "norm_ffn",
             "w_up", "ffn_conv_w", "ffn_conv_b", "w_down", "norm_final"]
    out = [loss, grad_x.reshape(nb, seq, D)]
    out += [big[nm][0] if nm in big else grads[nm] for nm in order]
    for idx in range(3):
        out += [big[nm][idx + 1] if nm in big else small[nm][idx] for nm in order]
    return tuple(out)
```

```python
import functools

import jax
import jax.numpy as jnp
from jax import lax
from jax.experimental import pallas as pl
from jax.experimental.pallas import tpu as pltpu
from jax.experimental.pallas import tpu_sc as plsc

F32 = jnp.float32
BF16 = jnp.bfloat16

NDEV = 8
D = 1024
NG = 4
CG = 256
WINS = (2, 4, 8, 16)
DIN = 6 * D
SH_IN = DIN // NDEV
NZT = DIN // CG
FF2 = 5632
SH_UP = FF2 // NDEV
FF = FF2 // 2
NCH = 4
SH_DN = FF // NDEV
RMS_EPS = 1e-6
HALO = 16

ADAM_LR = 0.001
ADAM_B1 = 0.9
ADAM_B2 = 0.999
ADAM_EPS = 1e-08
ADAM_WD = 0.01
ADAM_STEP = 10

TM_IN = 512
TM_BWD_IN = 256
TM_MIX = 256
TM_FFN = 256
FFN_CHUNKS_PER_STEP = 4
TK_WGRAD = 2048
MIX_POOL_PROJ, MIX_CONV_OUT, MIX_O = 0, 1, 2
MIX_COLS = 3 * D + CG
VMEM_BIG = 56 * 1024 * 1024
MESH = pl.DeviceIdType.MESH
ANY = pl.BlockSpec(memory_space=pl.ANY)


def _cparams(n_axes, vmem=None):
    return pltpu.CompilerParams(dimension_semantics=("arbitrary",) * n_axes, vmem_limit_bytes=vmem)


def _dot(a, b):
    return jnp.dot(a, b, preferred_element_type=F32)


def _dot_nt(a, b):
    return lax.dot_general(a, b, (((1,), (1,)), ((), ())), preferred_element_type=F32)


def _dot_tn(a, b):
    return lax.dot_general(a, b, (((0,), (0,)), ((), ())), preferred_element_type=F32)


def _shift_down(ext, s, lead):
    return pltpu.roll(ext, s, 0)[lead:]


def _shift_up(ext, s, tm):
    n = ext.shape[0]
    return pltpu.roll(ext, n - s, 0)[:tm]


def _rms_inv(x):
    return lax.rsqrt(jnp.mean(x * x, axis=-1, keepdims=True) + RMS_EPS)


def _rms_bwd(dh, xn, inv, g):
    dxn = dh * g
    return inv * (dxn - xn * jnp.mean(dxn * xn, axis=-1, keepdims=True))


def _pos():
    return lax.axis_index("x"), lax.axis_index("y"), lax.axis_index("c")


def _handshake(peers):
    barrier = pltpu.get_barrier_semaphore()
    for peer in peers:
        pl.semaphore_signal(barrier, inc=1, device_id=peer, device_id_type=MESH)
    pl.semaphore_wait(barrier, len(peers))


def _sequencer(body, out_type, n_sems, name, collective_id):
    return pl.kernel(
        body, out_type=out_type, mesh=plsc.ScalarSubcoreMesh(axis_name="sequencer", num_cores=1), name=name,
        scratch_types=[pltpu.SemaphoreType.DMA((n_sems,)), pltpu.SemaphoreType.DMA((n_sems,))],
        compiler_params=pltpu.CompilerParams(collective_id=collective_id))


def all_gather_blocks(shard, name, collective_id):
    def body(in_ref, out_ref, send_sems, recv_sems):
        x, y, c = _pos()
        sibling = (x, y, 1 - c)
        chips = [(1 - x, y), (x, 1 - y), (1 - x, 1 - y)]
        _handshake([sibling] + [(*chip, c) for chip in chips])

        def copy(k, block, to, src=None):
            slot = out_ref.at[4 * block[0] + 2 * block[1] + block[2]]
            return pltpu.make_async_remote_copy(
                src_ref=slot if src is None else src, dst_ref=slot,
                send_sem=send_sems.at[k], recv_sem=recv_sems.at[k], device_id=to, device_id_type=MESH)

        mine = pltpu.make_async_copy(in_ref, out_ref.at[4 * x + 2 * y + c], send_sems.at[7])
        mine.start()
        first = [copy(0, (x, y, c), sibling, src=in_ref)]
        first += [copy(1 + j, (x, y, c), (*chip, c), src=in_ref) for j, chip in enumerate(chips)]
        for cp in first:
            cp.start()
        passed = []
        for j, chip in enumerate(chips):
            copy(1 + j, (*chip, c), (x, y, c)).wait_recv()
            fw = copy(4 + j, (*chip, c), sibling)
            fw.start()
            passed.append(fw)
        copy(0, (x, y, 1 - c), (x, y, c)).wait_recv()
        for j, chip in enumerate(chips):
            copy(4 + j, (*chip, 1 - c), (x, y, c)).wait_recv()
        for cp in first + passed:
            cp.wait_send()
        mine.wait()

    out = jax.ShapeDtypeStruct((NDEV,) + shard.shape, shard.dtype)
    return _sequencer(body, out, 8, name, collective_id)(shard)


def _exchange_small(v, reduce, name):
    rows = v.shape[0]

    def body(v_ref, out_ref, slots, send_sems, recv_sems, local_sem):
        x, y, c = _pos()
        me = 4 * x + 2 * y + c
        mine = pltpu.make_async_copy(v_ref, slots.at[me], local_sem)
        mine.start()
        offs = [(dx, dy, dc) for dx in (0, 1) for dy in (0, 1) for dc in (0, 1)][1:]

        def copy(k, src_slot, to):
            return pltpu.make_async_remote_copy(
                src_ref=v_ref, dst_ref=slots.at[src_slot], send_sem=send_sems.at[k], recv_sem=recv_sems.at[k],
                device_id=to, device_id_type=MESH)

        sends = []
        for k, (dx, dy, dc) in enumerate(offs):
            cp = copy(k, me, (x ^ dx, y ^ dy, c ^ dc))
            cp.start()
            sends.append(cp)
        for k, (dx, dy, dc) in enumerate(offs):
            copy(k, 4 * (x ^ dx) + 2 * (y ^ dy) + (c ^ dc), (x, y, c)).wait_recv()
        for cp in sends:
            cp.wait_send()
        mine.wait()
        if reduce:
            acc = slots[0]
            for d in range(1, NDEV):
                acc = acc + slots[d]
            out_ref[...] = acc
        else:
            out_ref[...] = slots[...]

    out = jax.ShapeDtypeStruct((rows, D) if reduce else (NDEV, rows, D), F32)
    return pl.pallas_call(
        body, name=name, out_shape=out,
        in_specs=[pl.BlockSpec(memory_space=pltpu.VMEM)], out_specs=pl.BlockSpec(memory_space=pltpu.VMEM),
        scratch_shapes=[pltpu.VMEM((NDEV, rows, D), F32), pltpu.SemaphoreType.DMA((7,)),
                        pltpu.SemaphoreType.DMA((7,)), pltpu.SemaphoreType.DMA],
    )(v)


def reduce_scatter_d2d(grads, name, collective_id):
    n = len(grads)

    def body(*refs):
        ins, outs = refs[:n], refs[n:2 * n]
        send_sems, recv_sems = refs[2 * n:]
        x, y, c = _pos()
        _handshake([(x, y, 1 - c)])
        cps = []
        for w in range(n):
            cp = pltpu.make_async_remote_copy(
                src_ref=ins[w].at[1 - c], dst_ref=outs[w], send_sem=send_sems.at[w], recv_sem=recv_sems.at[w],
                device_id=(x, y, 1 - c), device_id_type=MESH)
            cp.start()
            cps.append(cp)
        for cp in cps:
            cp.wait_recv()
        for cp in cps:
            cp.wait_send()

    out = [jax.ShapeDtypeStruct(g.shape[1:], F32) for g in grads]
    return _sequencer(body, out, n, name, collective_id)(*grads)


def reduce_scatter_ici(parts, name, collective_id):
    n = len(parts)

    def body(*refs):
        ins, outs = refs[:n], refs[n:2 * n]
        send_sems, recv_sems = refs[2 * n:]
        x, y, c = _pos()
        offs = [(1, 0), (0, 1), (1, 1)]
        _handshake([(x ^ dx, y ^ dy, c) for dx, dy in offs])
        cps = []
        for w in range(n):
            for k, (dx, dy) in enumerate(offs):
                ox, oy = x ^ dx, y ^ dy
                cp = pltpu.make_async_remote_copy(
                    src_ref=ins[w].at[2 * ox + oy], dst_ref=outs[w].at[2 * x + y],
                    send_sem=send_sems.at[3 * w + k], recv_sem=recv_sems.at[3 * w + k],
                    device_id=(ox, oy, c), device_id_type=MESH)
                cp.start()
                cps.append((cp, w, k, ox, oy))
        for cp, w, k, ox, oy in cps:
            pltpu.make_async_remote_copy(
                src_ref=ins[w].at[2 * ox + oy], dst_ref=outs[w].at[2 * ox + oy],
                send_sem=send_sems.at[3 * w + k], recv_sem=recv_sems.at[3 * w + k],
                device_id=(ox, oy, c), device_id_type=MESH).wait_recv()
        for cp, *_ in cps:
            cp.wait_send()

    out = [jax.ShapeDtypeStruct(p.shape, BF16) for p in parts]
    return _sequencer(body, out, 3 * n, name, collective_id)(*parts)


def fwd_in(x, g1, w_in_g, tm):
    t = x.shape[0]

    def body(x_ref, g_ref, w_ref, z_ref, h_ref):
        xf = x_ref[...]
        h = (xf * _rms_inv(xf) * g_ref[...]).astype(BF16)
        h_ref[...] = h
        for j in range(NDEV):
            r = _dot(h, w_ref[j])
            for q in range(3):
                z_ref[3 * j + q] = r[:, q * CG:(q + 1) * CG].astype(BF16)

    return pl.pallas_call(
        body, name="fwd_in", grid=(t // tm,),
        in_specs=[pl.BlockSpec((tm, D), lambda i: (i, 0)), pl.BlockSpec((1, D), lambda i: (0, 0)),
                  pl.BlockSpec((NDEV, D, SH_IN), lambda i: (0, 0, 0))],
        out_specs=[pl.BlockSpec((NZT, tm, CG), lambda i: (0, i, 0)), pl.BlockSpec((tm, D), lambda i: (i, 0))],
        out_shape=[jax.ShapeDtypeStruct((NZT, t, CG), BF16), jax.ShapeDtypeStruct((t, D), BF16)],
        compiler_params=_cparams(1, VMEM_BIG),
    )(x, g1, w_in_g)


def _pool_tile(z_ref, zh_ref, g, win, keep_hist, cnt):
    zt = z_ref[g].astype(F32)
    ext = jnp.concatenate([zh_ref[g].astype(F32) * keep_hist, zt], axis=0)
    s, sh = ext, 1
    while sh < win:
        s = s + pltpu.roll(s, sh, 0)
        sh *= 2
    return s[HALO:] / cnt - zt


def _conv_taps(ext, cur, w_ref, lanes, lead):
    x1 = _shift_down(ext, 1, lead)
    x2 = _shift_down(ext, 2, lead)
    out = w_ref[2:3, lanes] * cur + w_ref[1:2, lanes] * x1 + w_ref[0:1, lanes] * x2
    return out, x1, x2


def fwd_mix(z, x, pool_w, pool_scale, conv_w, wmix, tm, seq):
    t = x.shape[0]
    tps = seq // tm
    hb = tm // HALO

    def body(z_ref, zph_ref, zcvh_ref, x_ref, pw_ref, ps_ref, wpp_ref, cw_ref, wco_ref, wo_ref,
             x1_ref, yp_ref, yc_ref):
        it = pl.program_id(0) % tps
        keep_hist = jnp.where(it == 0, 0.0, 1.0)
        pos = it * tm + lax.broadcasted_iota(jnp.int32, (tm, 1), 0)
        p2 = []
        for g, win in enumerate(WINS):
            cnt = jnp.minimum(pos + 1, win).astype(F32)
            p = _pool_tile(z_ref, zph_ref, g, win, keep_hist, cnt)
            lanes = slice(g * CG, (g + 1) * CG)
            p2.append((_dot(p.astype(BF16), pw_ref[g]) * ps_ref[:, lanes]).astype(BF16))
        y_pool = _dot(jnp.concatenate(p2, axis=1), wpp_ref[...])
        u = []
        for q in range(NG):
            lanes = slice(q * CG, (q + 1) * CG)
            cv = z_ref[8 + q].astype(F32) * z_ref[12 + q].astype(F32)
            cvh = zcvh_ref[q].astype(F32) * zcvh_ref[4 + q].astype(F32) * keep_hist
            cc, _, _ = _conv_taps(jnp.concatenate([cvh, cv], axis=0), cv, cw_ref, lanes, HALO)
            u.append((z_ref[4 + q].astype(F32) * cc).astype(BF16))
        y_conv = _dot(jnp.concatenate(u, axis=1), wco_ref[...])
        ypb, ycb = y_pool.astype(BF16), y_conv.astype(BF16)
        yp_ref[...] = ypb
        yc_ref[...] = ycb
        merged = []
        for q in range(NG):
            lanes = slice(q * CG, (q + 1) * CG)
            sp = jax.nn.sigmoid(z_ref[16 + q].astype(F32))
            sc = jax.nn.sigmoid(z_ref[20 + q].astype(F32))
            merged.append((sp * ypb[:, lanes].astype(F32) + sc * ycb[:, lanes].astype(F32)).astype(BF16))
        x1_ref[...] = x_ref[...] + _dot(jnp.concatenate(merged, axis=1), wo_ref[...])

    def hist(i):
        return jnp.maximum(i * hb - 1, 0)

    const2 = lambda i: (0, 0)
    return pl.pallas_call(
        body, name="fwd_mix", grid=(t // tm,),
        in_specs=[pl.BlockSpec((NZT, tm, CG), lambda i: (0, i, 0)),
                  pl.BlockSpec((NG, HALO, CG), lambda i: (0, hist(i), 0)),
                  pl.BlockSpec((2 * NG, HALO, CG), lambda i: (1, hist(i), 0)),
                  pl.BlockSpec((tm, D), lambda i: (i, 0)),
                  pl.BlockSpec((NG, CG, CG), lambda i: (0, 0, 0)), pl.BlockSpec((1, D), const2),
                  pl.BlockSpec((D, D), lambda i: (0, MIX_POOL_PROJ)), pl.BlockSpec((3, D), const2),
                  pl.BlockSpec((D, D), lambda i: (0, MIX_CONV_OUT)), pl.BlockSpec((D, D), lambda i: (0, MIX_O))],
        out_specs=[pl.BlockSpec((tm, D), lambda i: (i, 0))] * 3,
        out_shape=[jax.ShapeDtypeStruct((t, D), F32), jax.ShapeDtypeStruct((t, D), BF16),
                   jax.ShapeDtypeStruct((t, D), BF16)],
        compiler_params=_cparams(1, VMEM_BIG),
    )(z, z, z, x, pool_w, pool_scale, wmix, conv_w, wmix, wmix)


def fwd_ffn(x1, g2, w_up_g, fcw, fcb, w_dn, gf, tgt, tm, seq, cps):
    t = x1.shape[0]
    tps = seq // tm

    def body(x1_ref, g2_ref, wup_ref, fcw_ref, fcb_ref, wdn_ref, gf_ref, tgt_ref,
             up_ref, act_ref, h2_ref, dx2_ref, vec_ref, hist_ref, d_ref):
        i, k = pl.program_id(0), pl.program_id(1)
        keep_hist = jnp.where(i % tps == 0, 0.0, 1.0)

        @pl.when((i == 0) & (k == 0))
        def _():
            vec_ref[...] = jnp.zeros_like(vec_ref)
            hist_ref[...] = jnp.zeros_like(hist_ref)

        @pl.when(k == 0)
        def _():
            x1v = x1_ref[...]
            h2_ref[...] = (x1v * _rms_inv(x1v) * g2_ref[...]).astype(BF16)
            d_ref[...] = jnp.zeros_like(d_ref)

        h2 = h2_ref[...]
        lanes = slice(0, SH_UP)
        d = d_ref[...]
        for c in range(cps):
            kc = k * cps + c
            conv = []
            for s in range(2):
                ub = _dot(h2, wup_ref[s, c]).astype(BF16)
                up_ref[s, c] = ub
                uf = ub.astype(F32)
                ext = jnp.concatenate([hist_ref[s, kc] * keep_hist, uf], axis=0)
                hist_ref[s, kc] = uf[tm - 8:]
                cc, _, _ = _conv_taps(ext, uf, fcw_ref.at[s, c], lanes, 8)
                conv.append(cc + fcb_ref[s, c])
            a = (conv[0] * jax.nn.sigmoid(conv[0]) * conv[1]).astype(BF16)
            act_ref[c] = a
            d = d + _dot(a, wdn_ref[c])
        d_ref[...] = d

        @pl.when(k == NCH // cps - 1)
        def _():
            x2 = x1_ref[...] + d_ref[...]
            inv3 = _rms_inv(x2)
            xn = x2 * inv3
            diff = xn * gf_ref[...] - tgt_ref[...]
            dy = diff * (1.0 / D)
            vec_ref[0:1, :] += jnp.sum(dy * xn, axis=0, keepdims=True)
            vec_ref[1:2, :] += 0.5 * jnp.sum(jnp.mean(diff * diff, axis=-1))
            dx2_ref[...] = _rms_bwd(dy, xn, inv3, gf_ref[...])

    tile = lambda i, k: (i, 0)
    const2 = lambda i, k: (0, 0)
    pair = lambda i, k: (0, k, 0, 0)
    return pl.pallas_call(
        body, name="fwd_ffn", grid=(t // tm, NCH // cps),
        in_specs=[pl.BlockSpec((tm, D), tile), pl.BlockSpec((1, D), const2),
                  pl.BlockSpec((2, cps, D, SH_UP), pair), pl.BlockSpec((2, cps, 3, SH_UP), pair),
                  pl.BlockSpec((2, cps, 1, SH_UP), pair), pl.BlockSpec((cps, SH_UP, D), lambda i, k: (k, 0, 0)),
                  pl.BlockSpec((1, D), const2), pl.BlockSpec((tm, D), tile)],
        out_specs=[pl.BlockSpec((2, cps, tm, SH_UP), lambda i, k: (0, k, i, 0)),
                   pl.BlockSpec((cps, tm, SH_UP), lambda i, k: (k, i, 0)),
                   pl.BlockSpec((tm, D), tile), pl.BlockSpec((tm, D), tile), pl.BlockSpec((8, D), const2)],
        out_shape=[jax.ShapeDtypeStruct((2, NCH, t, SH_UP), BF16), jax.ShapeDtypeStruct((NCH, t, SH_UP), BF16),
                   jax.ShapeDtypeStruct((t, D), BF16), jax.ShapeDtypeStruct((t, D), F32),
                   jax.ShapeDtypeStruct((8, D), F32)],
        scratch_shapes=[pltpu.VMEM((2, NCH, 8, SH_UP), F32), pltpu.VMEM((tm, D), F32)],
        compiler_params=_cparams(2, VMEM_BIG),
    )(x1, g2, w_up_g.reshape(2, NCH, D, SH_UP), fcw.reshape(2, NCH, 3, SH_UP), fcb.reshape(2, NCH, 1, SH_UP),
      w_dn, gf, tgt)


def bwd_ffn(dx2, x1, g2, up, w_up_g, fcw, fcb, w_dn, tm, seq, cps):
    t = x1.shape[0]
    nt = t // tm
    tps = seq // tm
    hb = tm // HALO

    def body(dx2_ref, x1_ref, g2_ref, up_ref, uph_ref, wup_ref, fcw_ref, fcb_ref, wdn_ref,
             dup_ref, dx1_ref, gvec_ref, gn_ref, carry_ref, dh2_ref, acc_ref):
        i, k = pl.program_id(0), pl.program_id(1)
        it = (nt - 1 - i) % tps
        keep_hist = jnp.where(it == 0, 0.0, 1.0)
        keep_next = jnp.where(it == tps - 1, 0.0, 1.0)

        @pl.when((i == 0) & (k == 0))
        def _():
            acc_ref[...] = jnp.zeros_like(acc_ref)
            gn_ref[...] = jnp.zeros_like(gn_ref)
            carry_ref[...] = jnp.zeros_like(carry_ref)

        @pl.when(k == 0)
        def _():
            dh2_ref[...] = jnp.zeros_like(dh2_ref)

        dxb = dx2_ref[...].astype(BF16)
        lanes = slice(0, SH_UP)
        dh2 = dh2_ref[...]
        for c in range(cps):
            kc = k * cps + c
            pre, taps = [], []
            for s in range(2):
                uf = up_ref[s, c].astype(F32)
                ext = jnp.concatenate([uph_ref[s, c].astype(F32)[8:] * keep_hist, uf], axis=0)
                cc, u1, u2 = _conv_taps(ext, uf, fcw_ref.at[s, c], lanes, 8)
                pre.append(cc + fcb_ref[s, c])
                taps.append((u2, u1, uf))
            sg = jax.nn.sigmoid(pre[0])
            dact = _dot_nt(dxb, wdn_ref[c])
            dpre = [dact * pre[1] * (sg * (1.0 + pre[0] * (1.0 - sg))), dact * (pre[0] * sg)]
            for s in range(2):
                dc = dpre[s]
                acc_ref[s, kc, 0:1, lanes] += jnp.sum(dc, axis=0, keepdims=True)
                for tap in range(3):
                    acc_ref[s, kc, tap + 1:tap + 2, lanes] += jnp.sum(dc * taps[s][tap], axis=0, keepdims=True)
                ext = jnp.concatenate([dc, carry_ref[s, kc] * keep_next], axis=0)
                carry_ref[s, kc] = dc[:8]
                w = fcw_ref.at[s, c]
                du = w[2:3, :] * dc + w[1:2, :] * _shift_up(ext, 1, tm) + w[0:1, :] * _shift_up(ext, 2, tm)
                dub = du.astype(BF16)
                dup_ref[s, c] = dub
                dh2 = dh2 + _dot_nt(dub, wup_ref[s, c])
        dh2_ref[...] = dh2

        @pl.when(k == NCH // cps - 1)
        def _():
            x1v = x1_ref[...]
            inv2 = _rms_inv(x1v)
            xn = x1v * inv2
            gn_ref[0:1, :] += jnp.sum(dh2 * xn, axis=0, keepdims=True)
            dx1_ref[...] = dx2_ref[...] + _rms_bwd(dh2, xn, inv2, g2_ref[...])

        @pl.when((i == nt - 1) & (k == NCH // cps - 1))
        def _():
            gvec_ref[...] = acc_ref[...]

    rev = lambda i, k: (nt - 1 - i, 0)
    const2 = lambda i, k: (0, 0)
    pair = lambda i, k: (0, k, 0, 0)
    return pl.pallas_call(
        body, name="bwd_ffn", grid=(nt, NCH // cps),
        in_specs=[pl.BlockSpec((tm, D), rev), pl.BlockSpec((tm, D), rev), pl.BlockSpec((1, D), const2),
                  pl.BlockSpec((2, cps, tm, SH_UP), lambda i, k: (0, k, nt - 1 - i, 0)),
                  pl.BlockSpec((2, cps, HALO, SH_UP), lambda i, k: (0, k, jnp.maximum((nt - 1 - i) * hb - 1, 0), 0)),
                  pl.BlockSpec((2, cps, D, SH_UP), pair), pl.BlockSpec((2, cps, 3, SH_UP), pair),
                  pl.BlockSpec((2, cps, 1, SH_UP), pair), pl.BlockSpec((cps, SH_UP, D), lambda i, k: (k, 0, 0))],
        out_specs=[pl.BlockSpec((2, cps, tm, SH_UP), lambda i, k: (0, k, nt - 1 - i, 0)), pl.BlockSpec((tm, D), rev),
                   pl.BlockSpec((2, NCH, 8, D), lambda i, k: (0, 0, 0, 0)), pl.BlockSpec((8, D), const2)],
        out_shape=[jax.ShapeDtypeStruct((2, NCH, t, SH_UP), BF16), jax.ShapeDtypeStruct((t, D), F32),
                   jax.ShapeDtypeStruct((2, NCH, 8, D), F32), jax.ShapeDtypeStruct((8, D), F32)],
        scratch_shapes=[pltpu.VMEM((2, NCH, 8, SH_UP), F32), pltpu.VMEM((tm, D), F32),
                        pltpu.VMEM((2, NCH, 8, D), F32)],
        compiler_params=_cparams(2, VMEM_BIG),
    )(dx2, x1, g2, up, up, w_up_g.reshape(2, NCH, D, SH_UP), fcw.reshape(2, NCH, 3, SH_UP),
      fcb.reshape(2, NCH, 1, SH_UP), w_dn)


def bwd_mix(dx1, z, y_pool, y_conv, pool_w, pool_scale, conv_w, wmix, tm, seq):
    t = dx1.shape[0]
    nt = t // tm
    tps = seq // tm
    hb = tm // HALO

    def body(da_ref, z_ref, zph_ref, zcvh_ref, yp_ref, yc_ref, pw_ref, ps_ref, wpp_ref, cw_ref, wco_ref, wo_ref,
             dz_ref, mg_ref, p2_ref, u_ref, dyp_ref, dyc_ref, p_ref, dpw_ref, gvec_ref, cp_ref, cc_ref):
        i = pl.program_id(0)
        it = (nt - 1 - i) % tps
        keep_hist = jnp.where(it == 0, 0.0, 1.0)
        keep_next = jnp.where(it == tps - 1, 0.0, 1.0)
        pos = it * tm + lax.broadcasted_iota(jnp.int32, (tm, 1), 0)

        @pl.when(i == 0)
        def _():
            gvec_ref[...] = jnp.zeros_like(gvec_ref)
            cp_ref[...] = jnp.zeros_like(cp_ref)
            cc_ref[...] = jnp.zeros_like(cc_ref)

        dm = _dot_nt(da_ref[...].astype(BF16), wo_ref[...])
        merged, dyp, dyc = [], [], []
        for q in range(NG):
            lanes = slice(q * CG, (q + 1) * CG)
            sp = jax.nn.sigmoid(z_ref[16 + q].astype(F32))
            sc = jax.nn.sigmoid(z_ref[20 + q].astype(F32))
            yp = yp_ref[:, lanes].astype(F32)
            yc = yc_ref[:, lanes].astype(F32)
            dmq = dm[:, lanes]
            merged.append((sp * yp + sc * yc).astype(BF16))
            dyp.append((dmq * sp).astype(BF16))
            dyc.append((dmq * sc).astype(BF16))
            dz_ref[16 + q] = (dmq * yp * (sp * (1.0 - sp))).astype(BF16)
            dz_ref[20 + q] = (dmq * yc * (sc * (1.0 - sc))).astype(BF16)
        mg_ref[...] = jnp.concatenate(merged, axis=1)
        dypb = jnp.concatenate(dyp, axis=1)
        dycb = jnp.concatenate(dyc, axis=1)
        dyp_ref[...] = dypb
        dyc_ref[...] = dycb

        dp2 = _dot_nt(dypb, wpp_ref[...])
        p2 = []
        for g, win in enumerate(WINS):
            lanes = slice(g * CG, (g + 1) * CG)
            cnt = jnp.minimum(pos + 1, win).astype(F32)
            p = _pool_tile(z_ref, zph_ref, g, win, keep_hist, cnt)
            pb = p.astype(BF16)
            p_ref[g] = pb
            pw = _dot(pb, pw_ref[g])
            p2.append((pw * ps_ref[:, lanes]).astype(BF16))
            dp2g = dp2[:, lanes]
            gvec_ref[0:1, lanes] += jnp.sum(dp2g * pw, axis=0, keepdims=True)
            dpwb = (dp2g * ps_ref[:, lanes]).astype(BF16)
            dpw_ref[g] = dpwb
            dp = _dot_nt(dpwb, pw_ref[g])
            qv = dp / cnt
            ext = jnp.concatenate([qv, cp_ref[g] * keep_next], axis=0)
            cp_ref[g] = qv[:HALO]
            n = tm + HALO
            s, sh = ext, 1
            while sh < win:
                s = s + pltpu.roll(s, n - sh, 0)
                sh *= 2
            dz_ref[g] = (s[:tm] - dp).astype(BF16)
        p2_ref[...] = jnp.concatenate(p2, axis=1)

        du = _dot_nt(dycb, wco_ref[...])
        u = []
        for q in range(NG):
            lanes = slice(q * CG, (q + 1) * CG)
            zb = z_ref[4 + q].astype(F32)
            zc = z_ref[8 + q].astype(F32)
            zv = z_ref[12 + q].astype(F32)
            cv = zc * zv
            cvh = zcvh_ref[q].astype(F32) * zcvh_ref[4 + q].astype(F32) * keep_hist
            cc, cv1, cv2 = _conv_taps(jnp.concatenate([cvh, cv], axis=0), cv, cw_ref, lanes, HALO)
            u.append((zb * cc).astype(BF16))
            duq = du[:, lanes]
            dz_ref[4 + q] = (duq * cc).astype(BF16)
            dcc = duq * zb
            for tap, src in enumerate((cv2, cv1, cv)):
                gvec_ref[tap + 1:tap + 2, lanes] += jnp.sum(dcc * src, axis=0, keepdims=True)
            ext = jnp.concatenate([dcc, cc_ref[:, lanes] * keep_next], axis=0)
            cc_ref[:, lanes] = dcc[:8]
            dcv = (cw_ref[2:3, lanes] * dcc + cw_ref[1:2, lanes] * _shift_up(ext, 1, tm)
                   + cw_ref[0:1, lanes] * _shift_up(ext, 2, tm))
            dz_ref[8 + q] = (dcv * zv).astype(BF16)
            dz_ref[12 + q] = (dcv * zc).astype(BF16)
        u_ref[...] = jnp.concatenate(u, axis=1)

    def hist(i):
        return jnp.maximum((nt - 1 - i) * hb - 1, 0)

    rev = lambda i: (nt - 1 - i, 0)
    rev3 = lambda i: (0, nt - 1 - i, 0)
    const2 = lambda i: (0, 0)
    tok = jax.ShapeDtypeStruct((t, D), BF16)
    grp = jax.ShapeDtypeStruct((NG, t, CG), BF16)
    return pl.pallas_call(
        body, name="bwd_mix", grid=(nt,),
        in_specs=[pl.BlockSpec((tm, D), rev), pl.BlockSpec((NZT, tm, CG), rev3),
                  pl.BlockSpec((NG, HALO, CG), lambda i: (0, hist(i), 0)),
                  pl.BlockSpec((2 * NG, HALO, CG), lambda i: (1, hist(i), 0)),
                  pl.BlockSpec((tm, D), rev), pl.BlockSpec((tm, D), rev),
                  pl.BlockSpec((NG, CG, CG), lambda i: (0, 0, 0)), pl.BlockSpec((1, D), const2),
                  pl.BlockSpec((D, D), lambda i: (0, MIX_POOL_PROJ)), pl.BlockSpec((3, D), const2),
                  pl.BlockSpec((D, D), lambda i: (0, MIX_CONV_OUT)), pl.BlockSpec((D, D), lambda i: (0, MIX_O))],
        out_specs=[pl.BlockSpec((NZT, tm, CG), rev3)] + [pl.BlockSpec((tm, D), rev)] * 5
                  + [pl.BlockSpec((NG, tm, CG), rev3)] * 2 + [pl.BlockSpec((8, D), const2)],
        out_shape=[jax.ShapeDtypeStruct((NZT, t, CG), BF16), tok, tok, tok, tok, tok, grp, grp,
                   jax.ShapeDtypeStruct((8, D), F32)],
        scratch_shapes=[pltpu.VMEM((NG, HALO, CG), F32), pltpu.VMEM((8, D), F32)],
        compiler_params=_cparams(1, VMEM_BIG),
    )(dx1, z, z, z, y_pool, y_conv, pool_w, pool_scale, wmix, conv_w, wmix, wmix)


def bwd_in(dz, w_in_g, dx1, x, g1, tm):
    t = x.shape[0]

    def body(dz_ref, w_ref, dx1_ref, x_ref, g_ref, gx_ref, gn_ref):
        @pl.when(pl.program_id(0) == 0)
        def _():
            gn_ref[...] = jnp.zeros_like(gn_ref)

        dh = None
        for j in range(NDEV):
            dzc = jnp.concatenate([dz_ref[3 * j + q] for q in range(3)], axis=1)
            part = _dot_nt(dzc, w_ref[j])
            dh = part if dh is None else dh + part
        xv = x_ref[...]
        inv = _rms_inv(xv)
        xn = xv * inv
        gn_ref[0:1, :] += jnp.sum(dh * xn, axis=0, keepdims=True)
        gx_ref[...] = dx1_ref[...] + _rms_bwd(dh, xn, inv, g_ref[...])

    tile = lambda i: (i, 0)
    return pl.pallas_call(
        body, name="bwd_in", grid=(t // tm,),
        in_specs=[pl.BlockSpec((NZT, tm, CG), lambda i: (0, i, 0)),
                  pl.BlockSpec((NDEV, D, SH_IN), lambda i: (0, 0, 0)),
                  pl.BlockSpec((tm, D), tile), pl.BlockSpec((tm, D), tile), pl.BlockSpec((1, D), lambda i: (0, 0))],
        out_specs=[pl.BlockSpec((tm, D), tile), pl.BlockSpec((8, D), lambda i: (0, 0))],
        out_shape=[jax.ShapeDtypeStruct((t, D), F32), jax.ShapeDtypeStruct((8, D), F32)],
        compiler_params=_cparams(1, VMEM_BIG),
    )(dz, w_in_g, dx1, x, g1)


def _slot(j):
    return j % 2, j // 2


def wgrad_cols(a, b, pieces, width, name, tk):
    t, m = a.shape

    def body(a_ref, b_ref, o_ref):
        @pl.when(pl.program_id(1) == 0)
        def _():
            o_ref[...] = jnp.zeros_like(o_ref)

        av = a_ref[...]
        w = width // pieces
        for q in range(pieces):
            o_ref[:, q * w:(q + 1) * w] += _dot_tn(av, b_ref[q])

    return pl.pallas_call(
        body, name=name, grid=(NDEV, t // tk),
        in_specs=[pl.BlockSpec((tk, m), lambda j, k: (k, 0)),
                  pl.BlockSpec((pieces, tk, width // pieces), lambda j, k: (j, k, 0))],
        out_specs=pl.BlockSpec((None, None, m, width), lambda j, k: (j % 2, j // 2, 0, 0)),
        out_shape=jax.ShapeDtypeStruct((2, 4, m, width), F32),
        compiler_params=_cparams(2, VMEM_BIG),
    )(a, b)


def wgrad_down(act, dx2, tk):
    t = dx2.shape[0]

    def body(a_ref, b_ref, o_ref, acc_ref):
        kt = pl.program_id(1)

        @pl.when(kt == 0)
        def _():
            acc_ref[...] = jnp.zeros_like(acc_ref)

        acc_ref[...] += _dot_tn(a_ref[...], b_ref[...].astype(BF16))

        @pl.when(kt == pl.num_programs(1) - 1)
        def _():
            o_ref[0] = acc_ref[:SH_DN]
            o_ref[1] = acc_ref[SH_DN:]

    return pl.pallas_call(
        body, name="wgrad_down", grid=(NCH, t // tk),
        in_specs=[pl.BlockSpec((None, tk, SH_UP), lambda k, kt: (k, kt, 0)), pl.BlockSpec((tk, D), lambda k, kt: (kt, 0))],
        out_specs=pl.BlockSpec((2, None, SH_DN, D), lambda k, kt: (0, k, 0, 0)),
        out_shape=jax.ShapeDtypeStruct((2, 4, SH_DN, D), F32),
        scratch_shapes=[pltpu.VMEM((SH_UP, D), F32)],
        compiler_params=_cparams(2, VMEM_BIG),
    )(act, dx2)


def wgrad_square(a, b, name, tk):
    t = a.shape[0]

    def body(a_ref, b_ref, o_ref, acc_ref):
        kt = pl.program_id(0)

        @pl.when(kt == 0)
        def _():
            acc_ref[...] = jnp.zeros_like(acc_ref)

        acc_ref[...] += _dot_tn(a_ref[...], b_ref[...].astype(BF16))

        @pl.when(kt == pl.num_programs(0) - 1)
        def _():
            for j in range(NDEV):
                cc, xy = _slot(j)
                o_ref[cc, xy] = acc_ref[j * 128:(j + 1) * 128]

    return pl.pallas_call(
        body, name=name, grid=(t // tk,),
        in_specs=[pl.BlockSpec((tk, D), lambda k: (k, 0)), pl.BlockSpec((tk, D), lambda k: (k, 0))],
        out_specs=pl.BlockSpec((2, 4, 128, D), lambda k: (0, 0, 0, 0)),
        out_shape=jax.ShapeDtypeStruct((2, 4, 128, D), F32),
        scratch_shapes=[pltpu.VMEM((D, D), F32)],
        compiler_params=_cparams(1, VMEM_BIG),
    )(a, b)


def wgrad_pool(p, dpw, tk):
    t = p.shape[1]

    def body(a_ref, b_ref, o_ref):
        @pl.when(pl.program_id(0) == 0)
        def _():
            o_ref[...] = jnp.zeros_like(o_ref)

        for g in range(NG):
            o_ref[g] += _dot_tn(a_ref[g], b_ref[g])

    return pl.pallas_call(
        body, name="wgrad_pool", grid=(t // tk,),
        in_specs=[pl.BlockSpec((NG, tk, CG), lambda k: (0, k, 0))] * 2,
        out_specs=pl.BlockSpec((NG, CG, CG), lambda k: (0, 0, 0)),
        out_shape=jax.ShapeDtypeStruct((NG, CG, CG), F32),
        compiler_params=_cparams(1, VMEM_BIG),
    )(p, dpw)


def _adamw(w, g, m, v):
    m = ADAM_B1 * m + (1.0 - ADAM_B1) * g
    v = ADAM_B2 * v + (1.0 - ADAM_B2) * (g * g)
    m_hat = m / (1.0 - ADAM_B1 ** ADAM_STEP)
    v_hat = v / (1.0 - ADAM_B2 ** ADAM_STEP)
    delta = -ADAM_LR * (m_hat / (jnp.sqrt(v_hat) + ADAM_EPS) + ADAM_WD * w)
    return delta, m, v


def _row_block(r):
    return 512 if r % 512 == 0 else r


def chip_partial(place, g, from_sibling, name):
    _, _, r, c = g.shape

    def body(place_ref, g_ref, s_ref, o_ref):
        o_ref[...] = (g_ref[...] + s_ref[...]).astype(BF16)

    return pl.pallas_call(
        body, name=name,
        grid_spec=pltpu.PrefetchScalarGridSpec(
            num_scalar_prefetch=1, grid=(3,),
            in_specs=[pl.BlockSpec((None, None, r, c), lambda k, pr: (pr[0], pr[1] ^ (k + 1), 0, 0)),
                      pl.BlockSpec((None, r, c), lambda k, pr: (pr[1] ^ (k + 1), 0, 0))],
            out_specs=pl.BlockSpec((None, r, c), lambda k, pr: (pr[1] ^ (k + 1), 0, 0))),
        out_shape=jax.ShapeDtypeStruct((4, r, c), BF16),
        compiler_params=_cparams(1, VMEM_BIG // 2),
    )(place, g, from_sibling)


def finish_adamw(place, g, from_sibling, from_chips, w, m, v, name):
    _, _, r, c = g.shape
    br = _row_block(r)

    def body(place_ref, g_ref, s_ref, c1_ref, c2_ref, c3_ref, w_ref, m_ref, v_ref, og_ref, od_ref, om_ref, ov_ref):
        grad = g_ref[...] + s_ref[...]
        for ref in (c1_ref, c2_ref, c3_ref):
            grad = grad + ref[...].astype(F32)
        og_ref[...] = grad
        od_ref[...], om_ref[...], ov_ref[...] = _adamw(w_ref[...], grad, m_ref[...], v_ref[...])

    def other(k):
        return pl.BlockSpec((None, br, c), lambda i, pr: (pr[1] ^ k, i, 0))

    row = pl.BlockSpec((br, c), lambda i, pr: (i, 0))
    out = jax.ShapeDtypeStruct((r, c), F32)
    return pl.pallas_call(
        body, name=name,
        grid_spec=pltpu.PrefetchScalarGridSpec(
            num_scalar_prefetch=1, grid=(r // br,),
            in_specs=[pl.BlockSpec((None, None, br, c), lambda i, pr: (pr[0], pr[1], i, 0)),
                      pl.BlockSpec((None, br, c), lambda i, pr: (pr[1], i, 0)),
                      other(1), other(2), other(3), row, row, row],
            out_specs=[row] * 4),
        out_shape=[out] * 4,
        compiler_params=_cparams(1, VMEM_BIG),
    )(place, g, from_sibling, from_chips, from_chips, from_chips, w, m, v)


def adamw_small(items):
    n = len(items)

    def body(*refs):
        ins, outs = refs[:4 * n], refs[4 * n:]
        for i in range(n):
            w, g, m, v = (r[...] for r in ins[4 * i:4 * i + 4])
            outs[3 * i][...], outs[3 * i + 1][...], outs[3 * i + 2][...] = _adamw(w, g, m, v)

    out = [jax.ShapeDtypeStruct(it[0].shape, F32) for it in items for _ in range(3)]
    res = pl.pallas_call(body, name="adamw_small", out_shape=out)(*[a for it in items for a in it])
    return [res[3 * i:3 * i + 3] for i in range(n)]


def kernel(x, norm_mix, w_in, pool_w, pool_scale, w_pool_proj, conv_w, w_conv_out, w_o, norm_ffn, w_up, ffn_conv_w, ffn_conv_b, w_down, norm_final, loss_target, m_norm_mix, m_w_in, m_pool_w, m_pool_scale, m_w_pool_proj, m_conv_w, m_w_conv_out, m_w_o, m_norm_ffn, m_w_up, m_ffn_conv_w, m_ffn_conv_b, m_w_down, m_norm_final, v_norm_mix, v_w_in, v_pool_w, v_pool_scale, v_w_pool_proj, v_conv_w, v_w_conv_out, v_w_o, v_norm_ffn, v_w_up, v_ffn_conv_w, v_ffn_conv_b, v_w_down, v_norm_final):
    nb, seq, _ = x.shape
    t = nb * seq
    tm_in = min(TM_IN, t)
    tm_mix = min(TM_MIX, seq)
    tm_ffn = min(TM_FFN, seq)
    tk = min(TK_WGRAD, t)
    xt = x.reshape(t, D)
    tgt = loss_target.reshape(t, D)
    xi, yi, ci = _pos()
    me = 4 * xi + 2 * yi + ci
    place = jnp.stack([ci, 2 * xi + yi]).astype(jnp.int32)

    w_in_g = all_gather_blocks(w_in[0].astype(BF16), "all_gather_w_in", 0)
    mix_shard = jnp.concatenate(
        [w_pool_proj[0], w_conv_out[0], w_o[0], pool_w[0].reshape(NG * 32, CG)], axis=1).astype(BF16)
    wmix_g = all_gather_blocks(mix_shard, "all_gather_w_mix", 1)
    w_up_g = all_gather_blocks(w_up[0].astype(BF16), "all_gather_w_up", 2)
    w_dn_g = all_gather_blocks(w_down[0].astype(BF16), "all_gather_w_down", 3)
    taps = (jnp.pad(conv_w[0], ((0, 5), (0, D - 128))) + jnp.pad(ffn_conv_w[0], ((3, 2), (0, D - SH_UP))))
    taps_g = _exchange_small(taps, False, "all_gather_taps")
    wmix = wmix_g.reshape(D, MIX_COLS)
    pool_w_f = wmix_g[:, :, 3 * D:].reshape(NDEV, NG, 32, CG).transpose(1, 0, 2, 3).reshape(NG, CG, CG)
    w_dn_f = w_dn_g.reshape(NCH, SH_UP, D)
    conv_w_f = taps_g[:, 0:3, :128].transpose(1, 0, 2).reshape(3, D)
    fcw_f = taps_g[:, 3:6, :SH_UP]
    fcb_f = ffn_conv_b.reshape(NDEV, 1, SH_UP)
    gfin = norm_final.reshape(1, D)

    z, h1 = fwd_in(xt, norm_mix, w_in_g, tm_in)
    x1, y_pool, y_conv = fwd_mix(z, xt, pool_w_f, pool_scale, conv_w_f, wmix, tm_mix, seq)
    up, act, h2, dx2, ffn_vec = fwd_ffn(x1, norm_ffn, w_up_g, fcw_f, fcb_f, w_dn_f, gfin, tgt, tm_ffn, seq,
                                        FFN_CHUNKS_PER_STEP)

    def reduce_group(full, names, shards, tag, collective_id):
        from_sib = reduce_scatter_d2d(full, "reduce_scatter_d2d_" + tag, collective_id)
        parts = [chip_partial(place, g, s, "chip_partial_" + nm) for g, s, nm in zip(full, from_sib, names)]
        from_chips = reduce_scatter_ici(parts, "reduce_scatter_ici_" + tag, collective_id + 1)
        res = {}
        for nm, g, s, fc, (w, m, v) in zip(names, full, from_sib, from_chips, shards):
            rc = g.shape[2:]
            outs = finish_adamw(place, g, s, fc, w.reshape(rc), m.reshape(rc), v.reshape(rc), "adamw_" + nm)
            res[nm] = [o.reshape(w.shape) for o in outs]
        return res

    big = {}
    d_up, dx1, g_ffn_vec, g_nffn = bwd_ffn(dx2, x1, norm_ffn, up, w_up_g, fcw_f, fcb_f, w_dn_f, tm_ffn, seq,
                                           FFN_CHUNKS_PER_STEP)
    gw_dn = wgrad_down(act, dx2, tk)
    gw_up = wgrad_cols(h2, d_up.reshape(NDEV, t, SH_UP), 1, SH_UP, "wgrad_up", tk)
    big.update(reduce_group([gw_dn, gw_up], ["w_down", "w_up"],
                            [(w_down, m_w_down, v_w_down), (w_up, m_w_up, v_w_up)], "ffn", 4))
    dz, merged, p2, u, dyp, dyc, p, dpw, g_mix_vec = bwd_mix(
        dx1, z, y_pool, y_conv, pool_w_f, pool_scale, conv_w_f, wmix, tm_mix, seq)
    gw_o = wgrad_square(merged, dx1, "wgrad_o", tk)
    gw_pp = wgrad_square(p2, dyp, "wgrad_pool_proj", tk)
    big.update(reduce_group([gw_o, gw_pp], ["w_o", "w_pool_proj"],
                            [(w_o, m_w_o, v_w_o), (w_pool_proj, m_w_pool_proj, v_w_pool_proj)], "mix_a", 6))
    gw_co = wgrad_square(u, dyc, "wgrad_conv_out", tk)
    gw_pool = wgrad_pool(p, dpw, tk).reshape(NG, 4, 2, 32, CG).transpose(2, 1, 0, 3, 4).reshape(2, 4, NG * 32, CG)
    big.update(reduce_group([gw_co, gw_pool], ["w_conv_out", "pool_w"],
                            [(w_conv_out, m_w_conv_out, v_w_conv_out), (pool_w, m_pool_w, v_pool_w)], "mix_b", 8))
    gw_in = wgrad_cols(h1, dz, 3, SH_IN, "wgrad_in", tk)
    big.update(reduce_group([gw_in], ["w_in"], [(w_in, m_w_in, v_w_in)], "in", 10))
    grad_x, g_nmix = bwd_in(dz, w_in_g, dx1, xt, norm_mix, min(TM_BWD_IN, t))

    red = _exchange_small(
        jnp.concatenate([g_nmix, g_mix_vec, g_nffn, ffn_vec, g_ffn_vec.reshape(8 * NDEV, D)], axis=0), True,
        "all_reduce_small")
    g_norm_mix, g_pool_scale, g_norm_ffn = red[0:1], red[8:9], red[16:17]
    g_conv_w = lax.dynamic_slice(red, (9, me * 128), (3, 128))
    g_norm_final = red[24]
    loss = red[25, 0]
    g_fcb = red[32:].reshape(NDEV, 8, D)[:, 0, :SH_UP].reshape(1, FF2)
    g_fcw = lax.dynamic_slice(red, (33 + 8 * me, 0), (3, SH_UP))
    grads = {"norm_mix": g_norm_mix, "pool_scale": g_pool_scale, "norm_ffn": g_norm_ffn, "norm_final": g_norm_final,
             "ffn_conv_b": g_fcb, "conv_w": g_conv_w.reshape(1, 3, 128), "ffn_conv_w": g_fcw.reshape(1, 3, SH_UP)}
    small_wmv = {"norm_mix": (norm_mix, m_norm_mix, v_norm_mix), "pool_scale": (pool_scale, m_pool_scale, v_pool_scale),
                 "norm_ffn": (norm_ffn, m_norm_ffn, v_norm_ffn), "norm_final": (norm_final, m_norm_final, v_norm_final),
                 "ffn_conv_b": (ffn_conv_b, m_ffn_conv_b, v_ffn_conv_b), "conv_w": (conv_w, m_conv_w, v_conv_w),
                 "ffn_conv_w": (ffn_conv_w, m_ffn_conv_w, v_ffn_conv_w)}
    small_names = list(small_wmv)
    flat2 = lambda a: a.reshape(-1, a.shape[-1])
    small_out = adamw_small([(flat2(small_wmv[nm][0]), flat2(grads[nm]), flat2(small_wmv[nm][1]),
                              flat2(small_wmv[nm][2])) for nm in small_names])
    small = {nm: [o.reshape(small_wmv[nm][0].shape) for o in outs] for nm, outs in zip(small_names, small_out)}

    order = ["norm_mix", "w_in", "pool_w", "pool_scale", "w_pool_proj", "conv_w", "w_conv_out", "w_o", "norm_ffn",
             "w_up", "ffn_conv_w", "ffn_conv_b", "w_down", "norm_final"]
    out = [loss, grad_x.reshape(nb, seq, D)]
    out += [big[nm][0] if nm in big else grads[nm] for nm in order]
    for idx in range(3):
        out += [big[nm][idx + 1] if nm in big else small[nm][idx] for nm in order]
    return tuple(out)
```

```python
import functools

import jax
import jax.numpy as jnp
from jax import lax
from jax.experimental import pallas as pl
from jax.experimental.pallas import tpu as pltpu
from jax.experimental.pallas import tpu_sc as plsc

F32 = jnp.float32
BF16 = jnp.bfloat16

NDEV = 8
D = 1024
NG = 4
CG = 256
WINS = (2, 4, 8, 16)
DIN = 6 * D
SH_IN = DIN // NDEV
NZT = DIN // CG
FF2 = 5632
SH_UP = FF2 // NDEV
FF = FF2 // 2
NCH = 4
SH_DN = FF // NDEV
RMS_EPS = 1e-6
HALO = 16

ADAM_LR = 0.001
ADAM_B1 = 0.9
ADAM_B2 = 0.999
ADAM_EPS = 1e-08
ADAM_WD = 0.01
ADAM_STEP = 10

TM_IN = 512
TM_BWD_IN = 256
TM_MIX = 256
TM_FFN = 256
FFN_CHUNKS_PER_STEP = 4
TK_WGRAD = 2048
MIX_POOL_PROJ, MIX_CONV_OUT, MIX_O = 0, 1, 2
MIX_COLS = 3 * D + CG
VMEM_BIG = 56 * 1024 * 1024
MESH = pl.DeviceIdType.MESH
ANY = pl.BlockSpec(memory_space=pl.ANY)


def _cparams(n_axes, vmem=None):
    return pltpu.CompilerParams(dimension_semantics=("arbitrary",) * n_axes, vmem_limit_bytes=vmem)


def _dot(a, b):
    return jnp.dot(a, b, preferred_element_type=F32)


def _dot_nt(a, b):
    return lax.dot_general(a, b, (((1,), (1,)), ((), ())), preferred_element_type=F32)


def _dot_tn(a, b):
    return lax.dot_general(a, b, (((0,), (0,)), ((), ())), preferred_element_type=F32)


def _shift_down(ext, s, lead):
    return pltpu.roll(ext, s, 0)[lead:]


def _shift_up(ext, s, tm):
    n = ext.shape[0]
    return pltpu.roll(ext, n - s, 0)[:tm]


def _rms_inv(x):
    return lax.rsqrt(jnp.mean(x * x, axis=-1, keepdims=True) + RMS_EPS)


def _rms_bwd(dh, xn, inv, g):
    dxn = dh * g
    return inv * (dxn - xn * jnp.mean(dxn * xn, axis=-1, keepdims=True))


def _pos():
    return lax.axis_index("x"), lax.axis_index("y"), lax.axis_index("c")


def _handshake(peers):
    barrier = pltpu.get_barrier_semaphore()
    for peer in peers:
        pl.semaphore_signal(barrier, inc=1, device_id=peer, device_id_type=MESH)
    pl.semaphore_wait(barrier, len(peers))


def _sequencer(body, out_type, n_sems, name, collective_id):
    return pl.kernel(
        body, out_type=out_type, mesh=plsc.ScalarSubcoreMesh(axis_name="sequencer", num_cores=1), name=name,
        scratch_types=[pltpu.SemaphoreType.DMA((n_sems,)), pltpu.SemaphoreType.DMA((n_sems,))],
        compiler_params=pltpu.CompilerParams(collective_id=collective_id))


def all_gather_blocks(shard, name, collective_id):
    def body(in_ref, out_ref, send_sems, recv_sems):
        x, y, c = _pos()
        sibling = (x, y, 1 - c)
        chips = [(1 - x, y), (x, 1 - y), (1 - x, 1 - y)]
        _handshake([sibling] + [(*chip, c) for chip in chips])

        def copy(k, block, to, src=None):
            slot = out_ref.at[4 * block[0] + 2 * block[1] + block[2]]
            return pltpu.make_async_remote_copy(
                src_ref=slot if src is None else src, dst_ref=slot,
                send_sem=send_sems.at[k], recv_sem=recv_sems.at[k], device_id=to, device_id_type=MESH)

        mine = pltpu.make_async_copy(in_ref, out_ref.at[4 * x + 2 * y + c], send_sems.at[7])
        mine.start()
        first = [copy(0, (x, y, c), sibling, src=in_ref)]
        first += [copy(1 + j, (x, y, c), (*chip, c), src=in_ref) for j, chip in enumerate(chips)]
        for cp in first:
            cp.start()
        passed = []
        for j, chip in enumerate(chips):
            copy(1 + j, (*chip, c), (x, y, c)).wait_recv()
            fw = copy(4 + j, (*chip, c), sibling)
            fw.start()
            passed.append(fw)
        copy(0, (x, y, 1 - c), (x, y, c)).wait_recv()
        for j, chip in enumerate(chips):
            copy(4 + j, (*chip, 1 - c), (x, y, c)).wait_recv()
        for cp in first + passed:
            cp.wait_send()
        mine.wait()

    out = jax.ShapeDtypeStruct((NDEV,) + shard.shape, shard.dtype)
    return _sequencer(body, out, 8, name, collective_id)(shard)


def _exchange_small(v, reduce, name):
    rows = v.shape[0]

    def body(v_ref, out_ref, slots, send_sems, recv_sems, local_sem):
        x, y, c = _pos()
        me = 4 * x + 2 * y + c
        mine = pltpu.make_async_copy(v_ref, slots.at[me], local_sem)
        mine.start()
        offs = [(dx, dy, dc) for dx in (0, 1) for dy in (0, 1) for dc in (0, 1)][1:]

        def copy(k, src_slot, to):
            return pltpu.make_async_remote_copy(
                src_ref=v_ref, dst_ref=slots.at[src_slot], send_sem=send_sems.at[k], recv_sem=recv_sems.at[k],
                device_id=to, device_id_type=MESH)

        sends = []
        for k, (dx, dy, dc) in enumerate(offs):
            cp = copy(k, me, (x ^ dx, y ^ dy, c ^ dc))
            cp.start()
            sends.append(cp)
        for k, (dx, dy, dc) in enumerate(offs):
            copy(k, 4 * (x ^ dx) + 2 * (y ^ dy) + (c ^ dc), (x, y, c)).wait_recv()
        for cp in sends:
            cp.wait_send()
        mine.wait()
        if reduce:
            acc = slots[0]
            for d in range(1, NDEV):
                acc = acc + slots[d]
            out_ref[...] = acc
        else:
            out_ref[...] = slots[...]

    out = jax.ShapeDtypeStruct((rows, D) if reduce else (NDEV, rows, D), F32)
    return pl.pallas_call(
        body, name=name, out_shape=out,
        in_specs=[pl.BlockSpec(memory_space=pltpu.VMEM)], out_specs=pl.BlockSpec(memory_space=pltpu.VMEM),
        scratch_shapes=[pltpu.VMEM((NDEV, rows, D), F32), pltpu.SemaphoreType.DMA((7,)),
                        pltpu.SemaphoreType.DMA((7,)), pltpu.SemaphoreType.DMA],
    )(v)


def reduce_scatter_d2d(grads, name, collective_id):
    n = len(grads)

    def body(*refs):
        ins, outs = refs[:n], refs[n:2 * n]
        send_sems, recv_sems = refs[2 * n:]
        x, y, c = _pos()
        _handshake([(x, y, 1 - c)])
        cps = []
        for w in range(n):
            cp = pltpu.make_async_remote_copy(
                src_ref=ins[w].at[1 - c], dst_ref=outs[w], send_sem=send_sems.at[w], recv_sem=recv_sems.at[w],
                device_id=(x, y, 1 - c), device_id_type=MESH)
            cp.start()
            cps.append(cp)
        for cp in cps:
            cp.wait_recv()
        for cp in cps:
            cp.wait_send()

    out = [jax.ShapeDtypeStruct(g.shape[1:], F32) for g in grads]
    return _sequencer(body, out, n, name, collective_id)(*grads)


def reduce_scatter_ici(parts, name, collective_id):
    n = len(parts)

    def body(*refs):
        ins, outs = refs[:n], refs[n:2 * n]
        send_sems, recv_sems = refs[2 * n:]
        x, y, c = _pos()
        offs = [(1, 0), (0, 1), (1, 1)]
        _handshake([(x ^ dx, y ^ dy, c) for dx, dy in offs])
        cps = []
        for w in range(n):
            for k, (dx, dy) in enumerate(offs):
                ox, oy = x ^ dx, y ^ dy
                cp = pltpu.make_async_remote_copy(
                    src_ref=ins[w].at[2 * ox + oy], dst_ref=outs[w].at[2 * x + y],
                    send_sem=send_sems.at[3 * w + k], recv_sem=recv_sems.at[3 * w + k],
                    device_id=(ox, oy, c), device_id_type=MESH)
                cp.start()
                cps.append((cp, w, k, ox, oy))
        for cp, w, k, ox, oy in cps:
            pltpu.make_async_remote_copy(
                src_ref=ins[w].at[2 * ox + oy], dst_ref=outs[w].at[2 * ox + oy],
                send_sem=send_sems.at[3 * w + k], recv_sem=recv_sems.at[3 * w + k],
                device_id=(ox, oy, c), device_id_type=MESH).wait_recv()
        for cp, *_ in cps:
            cp.wait_send()

    out = [jax.ShapeDtypeStruct(p.shape, BF16) for p in parts]
    return _sequencer(body, out, 3 * n, name, collective_id)(*parts)


def fwd_in(x, g1, w_in_g, tm):
    t = x.shape[0]

    def body(x_ref, g_ref, w_ref, z_ref, h_ref):
        xf = x_ref[...]
        h = (xf * _rms_inv(xf) * g_ref[...]).astype(BF16)
        h_ref[...] = h
        for j in range(NDEV):
            z_ref[j] = _dot(h, w_ref[j]).astype(BF16)

    return pl.pallas_call(
        body, name="fwd_in_0", grid=(t // tm,),
        in_specs=[pl.BlockSpec((tm, D), lambda i: (i, 0)), pl.BlockSpec((1, D), lambda i: (0, 0)),
                  pl.BlockSpec((NDEV, D, CG), lambda i: (0, 0, 0))],
        out_specs=[pl.BlockSpec((NDEV, None, tm, CG), lambda i: (0, 0, i, 0)), pl.BlockSpec((tm, D), lambda i: (i, 0))],
        out_shape=[jax.ShapeDtypeStruct((NDEV, 3, t, CG), BF16), jax.ShapeDtypeStruct((t, D), BF16)],
        compiler_params=_cparams(1, VMEM_BIG),
    )(x, g1, w_in_g)


def fwd_in_more(h1, w_part, q, z, tm):
    t = h1.shape[0]

    def body(h_ref, w_ref, zin_ref, z_ref):
        h = h_ref[...]
        for j in range(NDEV):
            z_ref[j] = _dot(h, w_ref[j]).astype(BF16)

    return pl.pallas_call(
        body, name="fwd_in_%d" % q, grid=(t // tm,),
        in_specs=[pl.BlockSpec((tm, D), lambda i: (i, 0)), pl.BlockSpec((NDEV, D, CG), lambda i: (0, 0, 0)), ANY],
        out_specs=pl.BlockSpec((NDEV, None, tm, CG), lambda i: (0, q, i, 0)),
        out_shape=jax.ShapeDtypeStruct(z.shape, BF16), input_output_aliases={2: 0},
        compiler_params=_cparams(1, VMEM_BIG),
    )(h1, w_part, z)


def _pool_tile(z_ref, zh_ref, g, win, keep_hist, cnt):
    zt = z_ref[g].astype(F32)
    ext = jnp.concatenate([zh_ref[g].astype(F32) * keep_hist, zt], axis=0)
    s, sh = ext, 1
    while sh < win:
        s = s + pltpu.roll(s, sh, 0)
        sh *= 2
    return s[HALO:] / cnt - zt


def _conv_taps(ext, cur, w_ref, lanes, lead):
    x1 = _shift_down(ext, 1, lead)
    x2 = _shift_down(ext, 2, lead)
    out = w_ref[2:3, lanes] * cur + w_ref[1:2, lanes] * x1 + w_ref[0:1, lanes] * x2
    return out, x1, x2


def fwd_mix(z, x, pool_w, pool_scale, conv_w, wmix, tm, seq):
    t = x.shape[0]
    tps = seq // tm
    hb = tm // HALO

    def body(z_ref, zph_ref, zcvh_ref, x_ref, pw_ref, ps_ref, wpp_ref, cw_ref, wco_ref, wo_ref,
             x1_ref, yp_ref, yc_ref):
        it = pl.program_id(0) % tps
        keep_hist = jnp.where(it == 0, 0.0, 1.0)
        pos = it * tm + lax.broadcasted_iota(jnp.int32, (tm, 1), 0)
        p2 = []
        for g, win in enumerate(WINS):
            cnt = jnp.minimum(pos + 1, win).astype(F32)
            p = _pool_tile(z_ref, zph_ref, g, win, keep_hist, cnt)
            lanes = slice(g * CG, (g + 1) * CG)
            p2.append((_dot(p.astype(BF16), pw_ref[g]) * ps_ref[:, lanes]).astype(BF16))
        y_pool = _dot(jnp.concatenate(p2, axis=1), wpp_ref[...])
        u = []
        for q in range(NG):
            lanes = slice(q * CG, (q + 1) * CG)
            cv = z_ref[8 + q].astype(F32) * z_ref[12 + q].astype(F32)
            cvh = zcvh_ref[q].astype(F32) * zcvh_ref[4 + q].astype(F32) * keep_hist
            cc, _, _ = _conv_taps(jnp.concatenate([cvh, cv], axis=0), cv, cw_ref, lanes, HALO)
            u.append((z_ref[4 + q].astype(F32) * cc).astype(BF16))
        y_conv = _dot(jnp.concatenate(u, axis=1), wco_ref[...])
        ypb, ycb = y_pool.astype(BF16), y_conv.astype(BF16)
        yp_ref[...] = ypb
        yc_ref[...] = ycb
        merged = []
        for q in range(NG):
            lanes = slice(q * CG, (q + 1) * CG)
            sp = jax.nn.sigmoid(z_ref[16 + q].astype(F32))
            sc = jax.nn.sigmoid(z_ref[20 + q].astype(F32))
            merged.append((sp * ypb[:, lanes].astype(F32) + sc * ycb[:, lanes].astype(F32)).astype(BF16))
        x1_ref[...] = x_ref[...] + _dot(jnp.concatenate(merged, axis=1), wo_ref[...])

    def hist(i):
        return jnp.maximum(i * hb - 1, 0)

    const2 = lambda i: (0, 0)
    return pl.pallas_call(
        body, name="fwd_mix", grid=(t // tm,),
        in_specs=[pl.BlockSpec((NZT, tm, CG), lambda i: (0, i, 0)),
                  pl.BlockSpec((NG, HALO, CG), lambda i: (0, hist(i), 0)),
                  pl.BlockSpec((2 * NG, HALO, CG), lambda i: (1, hist(i), 0)),
                  pl.BlockSpec((tm, D), lambda i: (i, 0)),
                  pl.BlockSpec((NG, CG, CG), lambda i: (0, 0, 0)), pl.BlockSpec((1, D), const2),
                  pl.BlockSpec((D, D), lambda i: (0, MIX_POOL_PROJ)), pl.BlockSpec((3, D), const2),
                  pl.BlockSpec((D, D), lambda i: (0, MIX_CONV_OUT)), pl.BlockSpec((D, D), lambda i: (0, MIX_O))],
        out_specs=[pl.BlockSpec((tm, D), lambda i: (i, 0))] * 3,
        out_shape=[jax.ShapeDtypeStruct((t, D), F32), jax.ShapeDtypeStruct((t, D), BF16),
                   jax.ShapeDtypeStruct((t, D), BF16)],
        compiler_params=_cparams(1, VMEM_BIG),
    )(z, z, z, x, pool_w, pool_scale, wmix, conv_w, wmix, wmix)


def fwd_ffn(x1, g2, w_up_g, fcw, fcb, w_dn, gf, tgt, tm, seq, cps):
    t = x1.shape[0]
    tps = seq // tm

    def body(x1_ref, g2_ref, wup_ref, fcw_ref, fcb_ref, wdn_ref, gf_ref, tgt_ref,
             up_ref, pre_ref, act_ref, h2_ref, dx2_ref, vec_ref, hist_ref, d_ref):
        i, k = pl.program_id(0), pl.program_id(1)
        keep_hist = jnp.where(i % tps == 0, 0.0, 1.0)

        @pl.when((i == 0) & (k == 0))
        def _():
            vec_ref[...] = jnp.zeros_like(vec_ref)
            hist_ref[...] = jnp.zeros_like(hist_ref)

        @pl.when(k == 0)
        def _():
            x1v = x1_ref[...]
            h2_ref[...] = (x1v * _rms_inv(x1v) * g2_ref[...]).astype(BF16)
            d_ref[...] = jnp.zeros_like(d_ref)

        h2 = h2_ref[...]
        lanes = slice(0, SH_UP)
        d = d_ref[...]
        for c in range(cps):
            kc = k * cps + c
            conv = []
            for s in range(2):
                ub = _dot(h2, wup_ref[s, c]).astype(BF16)
                up_ref[s, c] = ub
                uf = ub.astype(F32)
                ext = jnp.concatenate([hist_ref[s, kc] * keep_hist, uf], axis=0)
                hist_ref[s, kc] = uf[tm - 8:]
                cc, _, _ = _conv_taps(ext, uf, fcw_ref.at[s, c], lanes, 8)
                conv.append(cc + fcb_ref[s, c])
                pre_ref[s, c] = conv[s].astype(BF16)
            a = (conv[0] * jax.nn.sigmoid(conv[0]) * conv[1]).astype(BF16)
            act_ref[c] = a
            d = d + _dot(a, wdn_ref[c])
        d_ref[...] = d

        @pl.when(k == NCH // cps - 1)
        def _():
            x2 = x1_ref[...] + d_ref[...]
            inv3 = _rms_inv(x2)
            xn = x2 * inv3
            diff = xn * gf_ref[...] - tgt_ref[...]
            dy = diff * (1.0 / D)
            vec_ref[0:1, :] += jnp.sum(dy * xn, axis=0, keepdims=True)
            vec_ref[1:2, :] += 0.5 * jnp.sum(jnp.mean(diff * diff, axis=-1))
            dx2_ref[...] = _rms_bwd(dy, xn, inv3, gf_ref[...])

    tile = lambda i, k: (i, 0)
    const2 = lambda i, k: (0, 0)
    pair = lambda i, k: (0, k, 0, 0)
    return pl.pallas_call(
        body, name="fwd_ffn", grid=(t // tm, NCH // cps),
        in_specs=[pl.BlockSpec((tm, D), tile), pl.BlockSpec((1, D), const2),
                  pl.BlockSpec((2, cps, D, SH_UP), pair), pl.BlockSpec((2, cps, 3, SH_UP), pair),
                  pl.BlockSpec((2, cps, 1, SH_UP), pair), pl.BlockSpec((cps, SH_UP, D), lambda i, k: (k, 0, 0)),
                  pl.BlockSpec((1, D), const2), pl.BlockSpec((tm, D), tile)],
        out_specs=[pl.BlockSpec((2, cps, tm, SH_UP), lambda i, k: (0, k, i, 0)),
                   pl.BlockSpec((2, cps, tm, SH_UP), lambda i, k: (0, k, i, 0)),
                   pl.BlockSpec((cps, tm, SH_UP), lambda i, k: (k, i, 0)),
                   pl.BlockSpec((tm, D), tile), pl.BlockSpec((tm, D), tile), pl.BlockSpec((8, D), const2)],
        out_shape=[jax.ShapeDtypeStruct((2, NCH, t, SH_UP), BF16), jax.ShapeDtypeStruct((2, NCH, t, SH_UP), BF16),
                   jax.ShapeDtypeStruct((NCH, t, SH_UP), BF16),
                   jax.ShapeDtypeStruct((t, D), BF16), jax.ShapeDtypeStruct((t, D), F32),
                   jax.ShapeDtypeStruct((8, D), F32)],
        scratch_shapes=[pltpu.VMEM((2, NCH, 8, SH_UP), F32), pltpu.VMEM((tm, D), F32)],
        compiler_params=_cparams(2, VMEM_BIG),
    )(x1, g2, w_up_g.reshape(2, NCH, D, SH_UP), fcw.reshape(2, NCH, 3, SH_UP), fcb.reshape(2, NCH, 1, SH_UP),
      w_dn, gf, tgt)


def bwd_ffn(dx2, x1, g2, up, pre, w_up_g, fcw, w_dn, tm, seq, cps):
    t = x1.shape[0]
    nt = t // tm
    tps = seq // tm

    def body(dx2_ref, x1_ref, g2_ref, up_ref, pre_ref, wup_ref, fcw_ref, wdn_ref,
             dup_ref, dx1_ref, gvec_ref, gn_ref, carry_ref, dh2_ref, acc_ref):
        i, k = pl.program_id(0), pl.program_id(1)
        it = (nt - 1 - i) % tps
        keep_next = jnp.where(it == tps - 1, 0.0, 1.0)

        @pl.when((i == 0) & (k == 0))
        def _():
            acc_ref[...] = jnp.zeros_like(acc_ref)
            gn_ref[...] = jnp.zeros_like(gn_ref)
            carry_ref[...] = jnp.zeros_like(carry_ref)

        @pl.when(k == 0)
        def _():
            dh2_ref[...] = jnp.zeros_like(dh2_ref)

        dxb = dx2_ref[...].astype(BF16)
        lanes = slice(0, SH_UP)
        dh2 = dh2_ref[...]
        for c in range(cps):
            kc = k * cps + c
            pre = [pre_ref[s, c].astype(F32) for s in range(2)]
            sg = jax.nn.sigmoid(pre[0])
            dact = _dot_nt(dxb, wdn_ref[c])
            dpre = [dact * pre[1] * (sg * (1.0 + pre[0] * (1.0 - sg))), dact * (pre[0] * sg)]
            for s in range(2):
                dc = dpre[s]
                ext = jnp.concatenate([dc, carry_ref[s, kc] * keep_next], axis=0)
                carry_ref[s, kc] = dc[:8]
                shifted = (_shift_up(ext, 2, tm), _shift_up(ext, 1, tm), dc)
                uf = up_ref[s, c].astype(F32)
                acc_ref[s, kc, 0:1, lanes] += jnp.sum(dc, axis=0, keepdims=True)
                for tap in range(3):
                    acc_ref[s, kc, tap + 1:tap + 2, lanes] += jnp.sum(shifted[tap] * uf, axis=0, keepdims=True)
                w = fcw_ref.at[s, c]
                du = w[2:3, :] * dc + w[1:2, :] * shifted[1] + w[0:1, :] * shifted[0]
                dub = du.astype(BF16)
                dup_ref[s, c] = dub
                dh2 = dh2 + _dot_nt(dub, wup_ref[s, c])
        dh2_ref[...] = dh2

        @pl.when(k == NCH // cps - 1)
        def _():
            x1v = x1_ref[...]
            inv2 = _rms_inv(x1v)
            xn = x1v * inv2
            gn_ref[0:1, :] += jnp.sum(dh2 * xn, axis=0, keepdims=True)
            dx1_ref[...] = dx2_ref[...] + _rms_bwd(dh2, xn, inv2, g2_ref[...])

        @pl.when((i == nt - 1) & (k == NCH // cps - 1))
        def _():
            gvec_ref[...] = acc_ref[...]

    rev = lambda i, k: (nt - 1 - i, 0)
    const2 = lambda i, k: (0, 0)
    pair = lambda i, k: (0, k, 0, 0)
    return pl.pallas_call(
        body, name="bwd_ffn", grid=(nt, NCH // cps),
        in_specs=[pl.BlockSpec((tm, D), rev), pl.BlockSpec((tm, D), rev), pl.BlockSpec((1, D), const2),
                  pl.BlockSpec((2, cps, tm, SH_UP), lambda i, k: (0, k, nt - 1 - i, 0)),
                  pl.BlockSpec((2, cps, tm, SH_UP), lambda i, k: (0, k, nt - 1 - i, 0)),
                  pl.BlockSpec((2, cps, D, SH_UP), pair), pl.BlockSpec((2, cps, 3, SH_UP), pair),
                  pl.BlockSpec((cps, SH_UP, D), lambda i, k: (k, 0, 0))],
        out_specs=[pl.BlockSpec((2, cps, tm, SH_UP), lambda i, k: (0, k, nt - 1 - i, 0)), pl.BlockSpec((tm, D), rev),
                   pl.BlockSpec((2, NCH, 8, D), lambda i, k: (0, 0, 0, 0)), pl.BlockSpec((8, D), const2)],
        out_shape=[jax.ShapeDtypeStruct((2, NCH, t, SH_UP), BF16), jax.ShapeDtypeStruct((t, D), F32),
                   jax.ShapeDtypeStruct((2, NCH, 8, D), F32), jax.ShapeDtypeStruct((8, D), F32)],
        scratch_shapes=[pltpu.VMEM((2, NCH, 8, SH_UP), F32), pltpu.VMEM((tm, D), F32),
                        pltpu.VMEM((2, NCH, 8, D), F32)],
        compiler_params=_cparams(2, VMEM_BIG),
    )(dx2, x1, g2, up, pre, w_up_g.reshape(2, NCH, D, SH_UP), fcw.reshape(2, NCH, 3, SH_UP), w_dn)


def bwd_mix(dx1, z, y_pool, y_conv, pool_w, pool_scale, conv_w, wmix, tm, seq):
    t = dx1.shape[0]
    nt = t // tm
    tps = seq // tm
    hb = tm // HALO

    def body(da_ref, z_ref, zph_ref, zcvh_ref, yp_ref, yc_ref, pw_ref, ps_ref, wpp_ref, cw_ref, wco_ref, wo_ref,
             dz_ref, mg_ref, p2_ref, u_ref, dyp_ref, dyc_ref, p_ref, dpw_ref, gvec_ref, cp_ref, cc_ref):
        i = pl.program_id(0)
        it = (nt - 1 - i) % tps
        keep_hist = jnp.where(it == 0, 0.0, 1.0)
        keep_next = jnp.where(it == tps - 1, 0.0, 1.0)
        pos = it * tm + lax.broadcasted_iota(jnp.int32, (tm, 1), 0)

        @pl.when(i == 0)
        def _():
            gvec_ref[...] = jnp.zeros_like(gvec_ref)
            cp_ref[...] = jnp.zeros_like(cp_ref)
            cc_ref[...] = jnp.zeros_like(cc_ref)

        dm = _dot_nt(da_ref[...].astype(BF16), wo_ref[...])
        merged, dyp, dyc = [], [], []
        for q in range(NG):
            lanes = slice(q * CG, (q + 1) * CG)
            sp = jax.nn.sigmoid(z_ref[16 + q].astype(F32))
            sc = jax.nn.sigmoid(z_ref[20 + q].astype(F32))
            yp = yp_ref[:, lanes].astype(F32)
            yc = yc_ref[:, lanes].astype(F32)
            dmq = dm[:, lanes]
            merged.append((sp * yp + sc * yc).astype(BF16))
            dyp.append((dmq * sp).astype(BF16))
            dyc.append((dmq * sc).astype(BF16))
            dz_ref[16 + q] = (dmq * yp * (sp * (1.0 - sp))).astype(BF16)
            dz_ref[20 + q] = (dmq * yc * (sc * (1.0 - sc))).astype(BF16)
        mg_ref[...] = jnp.concatenate(merged, axis=1)
        dypb = jnp.concatenate(dyp, axis=1)
        dycb = jnp.concatenate(dyc, axis=1)
        dyp_ref[...] = dypb
        dyc_ref[...] = dycb

        dp2 = _dot_nt(dypb, wpp_ref[...])
        p2 = []
        for g, win in enumerate(WINS):
            lanes = slice(g * CG, (g + 1) * CG)
            cnt = jnp.minimum(pos + 1, win).astype(F32)
            p = _pool_tile(z_ref, zph_ref, g, win, keep_hist, cnt)
            pb = p.astype(BF16)
            p_ref[g] = pb
            pw = _dot(pb, pw_ref[g])
            p2.append((pw * ps_ref[:, lanes]).astype(BF16))
            dp2g = dp2[:, lanes]
            gvec_ref[0:1, lanes] += jnp.sum(dp2g * pw, axis=0, keepdims=True)
            dpwb = (dp2g * ps_ref[:, lanes]).astype(BF16)
            dpw_ref[g] = dpwb
            dp = _dot_nt(dpwb, pw_ref[g])
            qv = dp / cnt
            ext = jnp.concatenate([qv, cp_ref[g] * keep_next], axis=0)
            cp_ref[g] = qv[:HALO]
            n = tm + HALO
            s, sh = ext, 1
            while sh < win:
                s = s + pltpu.roll(s, n - sh, 0)
                sh *= 2
            dz_ref[g] = (s[:tm] - dp).astype(BF16)
        p2_ref[...] = jnp.concatenate(p2, axis=1)

        du = _dot_nt(dycb, wco_ref[...])
        u = []
        for q in range(NG):
            lanes = slice(q * CG, (q + 1) * CG)
            zb = z_ref[4 + q].astype(F32)
            zc = z_ref[8 + q].astype(F32)
            zv = z_ref[12 + q].astype(F32)
            cv = zc * zv
            cvh = zcvh_ref[q].astype(F32) * zcvh_ref[4 + q].astype(F32) * keep_hist
            cc, cv1, cv2 = _conv_taps(jnp.concatenate([cvh, cv], axis=0), cv, cw_ref, lanes, HALO)
            u.append((zb * cc).astype(BF16))
            duq = du[:, lanes]
            dz_ref[4 + q] = (duq * cc).astype(BF16)
            dcc = duq * zb
            for tap, src in enumerate((cv2, cv1, cv)):
                gvec_ref[tap + 1:tap + 2, lanes] += jnp.sum(dcc * src, axis=0, keepdims=True)
            ext = jnp.concatenate([dcc, cc_ref[:, lanes] * keep_next], axis=0)
            cc_ref[:, lanes] = dcc[:8]
            dcv = (cw_ref[2:3, lanes] * dcc + cw_ref[1:2, lanes] * _shift_up(ext, 1, tm)
                   + cw_ref[0:1, lanes] * _shift_up(ext, 2, tm))
            dz_ref[8 + q] = (dcv * zv).astype(BF16)
            dz_ref[12 + q] = (dcv * zc).astype(BF16)
        u_ref[...] = jnp.concatenate(u, axis=1)

    def hist(i):
        return jnp.maximum((nt - 1 - i) * hb - 1, 0)

    rev = lambda i: (nt - 1 - i, 0)
    rev3 = lambda i: (0, nt - 1 - i, 0)
    const2 = lambda i: (0, 0)
    tok = jax.ShapeDtypeStruct((t, D), BF16)
    grp = jax.ShapeDtypeStruct((NG, t, CG), BF16)
    return pl.pallas_call(
        body, name="bwd_mix", grid=(nt,),
        in_specs=[pl.BlockSpec((tm, D), rev), pl.BlockSpec((NZT, tm, CG), rev3),
                  pl.BlockSpec((NG, HALO, CG), lambda i: (0, hist(i), 0)),
                  pl.BlockSpec((2 * NG, HALO, CG), lambda i: (1, hist(i), 0)),
                  pl.BlockSpec((tm, D), rev), pl.BlockSpec((tm, D), rev),
                  pl.BlockSpec((NG, CG, CG), lambda i: (0, 0, 0)), pl.BlockSpec((1, D), const2),
                  pl.BlockSpec((D, D), lambda i: (0, MIX_POOL_PROJ)), pl.BlockSpec((3, D), const2),
                  pl.BlockSpec((D, D), lambda i: (0, MIX_CONV_OUT)), pl.BlockSpec((D, D), lambda i: (0, MIX_O))],
        out_specs=[pl.BlockSpec((NZT, tm, CG), rev3)] + [pl.BlockSpec((tm, D), rev)] * 5
                  + [pl.BlockSpec((NG, tm, CG), rev3)] * 2 + [pl.BlockSpec((8, D), const2)],
        out_shape=[jax.ShapeDtypeStruct((NZT, t, CG), BF16), tok, tok, tok, tok, tok, grp, grp,
                   jax.ShapeDtypeStruct((8, D), F32)],
        scratch_shapes=[pltpu.VMEM((NG, HALO, CG), F32), pltpu.VMEM((8, D), F32)],
        compiler_params=_cparams(1, VMEM_BIG),
    )(dx1, z, z, z, y_pool, y_conv, pool_w, pool_scale, wmix, conv_w, wmix, wmix)


def bwd_in(dz, w_in_parts, dx1, x, g1, tm):
    t = x.shape[0]

    def body(dz_ref, w0_ref, w1_ref, w2_ref, dx1_ref, x_ref, g_ref, gx_ref, gn_ref):
        @pl.when(pl.program_id(0) == 0)
        def _():
            gn_ref[...] = jnp.zeros_like(gn_ref)

        dh = None
        for j in range(NDEV):
            for q, w_ref in enumerate((w0_ref, w1_ref, w2_ref)):
                part = _dot_nt(dz_ref[3 * j + q], w_ref[j])
                dh = part if dh is None else dh + part
        xv = x_ref[...]
        inv = _rms_inv(xv)
        xn = xv * inv
        gn_ref[0:1, :] += jnp.sum(dh * xn, axis=0, keepdims=True)
        gx_ref[...] = dx1_ref[...] + _rms_bwd(dh, xn, inv, g_ref[...])

    tile = lambda i: (i, 0)
    return pl.pallas_call(
        body, name="bwd_in", grid=(t // tm,),
        in_specs=[pl.BlockSpec((NZT, tm, CG), lambda i: (0, i, 0))]
                 + [pl.BlockSpec((NDEV, D, CG), lambda i: (0, 0, 0))] * 3
                 + [pl.BlockSpec((tm, D), tile), pl.BlockSpec((tm, D), tile), pl.BlockSpec((1, D), lambda i: (0, 0))],
        out_specs=[pl.BlockSpec((tm, D), tile), pl.BlockSpec((8, D), lambda i: (0, 0))],
        out_shape=[jax.ShapeDtypeStruct((t, D), F32), jax.ShapeDtypeStruct((8, D), F32)],
        compiler_params=_cparams(1, VMEM_BIG),
    )(dz, *w_in_parts, dx1, x, g1)


def _slot(j):
    return j % 2, j // 2


def wgrad_cols(a, b, pieces, width, name, tk):
    t, m = a.shape

    def body(a_ref, b_ref, o_ref):
        @pl.when(pl.program_id(1) == 0)
        def _():
            o_ref[...] = jnp.zeros_like(o_ref)

        av = a_ref[...]
        w = width // pieces
        for q in range(pieces):
            o_ref[:, q * w:(q + 1) * w] += _dot_tn(av, b_ref[q])

    return pl.pallas_call(
        body, name=name, grid=(NDEV, t // tk),
        in_specs=[pl.BlockSpec((tk, m), lambda j, k: (k, 0)),
                  pl.BlockSpec((pieces, tk, width // pieces), lambda j, k: (j, k, 0))],
        out_specs=pl.BlockSpec((None, None, m, width), lambda j, k: (j % 2, j // 2, 0, 0)),
        out_shape=jax.ShapeDtypeStruct((2, 4, m, width), F32),
        compiler_params=_cparams(2, VMEM_BIG),
    )(a, b)


def wgrad_down(act, dx2, tk):
    t = dx2.shape[0]

    def body(a_ref, b_ref, o_ref, acc_ref):
        kt = pl.program_id(1)

        @pl.when(kt == 0)
        def _():
            acc_ref[...] = jnp.zeros_like(acc_ref)

        acc_ref[...] += _dot_tn(a_ref[...], b_ref[...].astype(BF16))

        @pl.when(kt == pl.num_programs(1) - 1)
        def _():
            o_ref[0] = acc_ref[:SH_DN]
            o_ref[1] = acc_ref[SH_DN:]

    return pl.pallas_call(
        body, name="wgrad_down", grid=(NCH, t // tk),
        in_specs=[pl.BlockSpec((None, tk, SH_UP), lambda k, kt: (k, kt, 0)), pl.BlockSpec((tk, D), lambda k, kt: (kt, 0))],
        out_specs=pl.BlockSpec((2, None, SH_DN, D), lambda k, kt: (0, k, 0, 0)),
        out_shape=jax.ShapeDtypeStruct((2, 4, SH_DN, D), F32),
        scratch_shapes=[pltpu.VMEM((SH_UP, D), F32)],
        compiler_params=_cparams(2, VMEM_BIG),
    )(act, dx2)


def wgrad_square(a, b, name, tk):
    t = a.shape[0]

    def body(a_ref, b_ref, o_ref, acc_ref):
        kt = pl.program_id(0)

        @pl.when(kt == 0)
        def _():
            acc_ref[...] = jnp.zeros_like(acc_ref)

        acc_ref[...] += _dot_tn(a_ref[...], b_ref[...].astype(BF16))

        @pl.when(kt == pl.num_programs(0) - 1)
        def _():
            for j in range(NDEV):
                cc, xy = _slot(j)
                o_ref[cc, xy] = acc_ref[j * 128:(j + 1) * 128]

    return pl.pallas_call(
        body, name=name, grid=(t // tk,),
        in_specs=[pl.BlockSpec((tk, D), lambda k: (k, 0)), pl.BlockSpec((tk, D), lambda k: (k, 0))],
        out_specs=pl.BlockSpec((2, 4, 128, D), lambda k: (0, 0, 0, 0)),
        out_shape=jax.ShapeDtypeStruct((2, 4, 128, D), F32),
        scratch_shapes=[pltpu.VMEM((D, D), F32)],
        compiler_params=_cparams(1, VMEM_BIG),
    )(a, b)


def wgrad_pool(p, dpw, tk):
    t = p.shape[1]

    def body(a_ref, b_ref, o_ref):
        @pl.when(pl.program_id(0) == 0)
        def _():
            o_ref[...] = jnp.zeros_like(o_ref)

        for g in range(NG):
            o_ref[g] += _dot_tn(a_ref[g], b_ref[g])

    return pl.pallas_call(
        body, name="wgrad_pool", grid=(t // tk,),
        in_specs=[pl.BlockSpec((NG, tk, CG), lambda k: (0, k, 0))] * 2,
        out_specs=pl.BlockSpec((NG, CG, CG), lambda k: (0, 0, 0)),
        out_shape=jax.ShapeDtypeStruct((NG, CG, CG), F32),
        compiler_params=_cparams(1, VMEM_BIG),
    )(p, dpw)


def _adamw(w, g, m, v):
    m = ADAM_B1 * m + (1.0 - ADAM_B1) * g
    v = ADAM_B2 * v + (1.0 - ADAM_B2) * (g * g)
    m_hat = m / (1.0 - ADAM_B1 ** ADAM_STEP)
    v_hat = v / (1.0 - ADAM_B2 ** ADAM_STEP)
    delta = -ADAM_LR * (m_hat / (jnp.sqrt(v_hat) + ADAM_EPS) + ADAM_WD * w)
    return delta, m, v


def _row_block(r):
    return 512 if r % 512 == 0 else r


def chip_partial(place, g, from_sibling, name):
    _, _, r, c = g.shape

    def body(place_ref, g_ref, s_ref, o_ref):
        o_ref[...] = (g_ref[...] + s_ref[...]).astype(BF16)

    return pl.pallas_call(
        body, name=name,
        grid_spec=pltpu.PrefetchScalarGridSpec(
            num_scalar_prefetch=1, grid=(3,),
            in_specs=[pl.BlockSpec((None, None, r, c), lambda k, pr: (pr[0], pr[1] ^ (k + 1), 0, 0)),
                      pl.BlockSpec((None, r, c), lambda k, pr: (pr[1] ^ (k + 1), 0, 0))],
            out_specs=pl.BlockSpec((None, r, c), lambda k, pr: (pr[1] ^ (k + 1), 0, 0))),
        out_shape=jax.ShapeDtypeStruct((4, r, c), BF16),
        compiler_params=_cparams(1, VMEM_BIG),
    )(place, g, from_sibling)


def finish_adamw(place, g, from_sibling, from_chips, w, m, v, name):
    _, _, r, c = g.shape
    br = _row_block(r)

    def body(place_ref, g_ref, s_ref, c1_ref, c2_ref, c3_ref, w_ref, m_ref, v_ref, og_ref, od_ref, om_ref, ov_ref):
        grad = g_ref[...] + s_ref[...]
        for ref in (c1_ref, c2_ref, c3_ref):
            grad = grad + ref[...].astype(F32)
        og_ref[...] = grad
        od_ref[...], om_ref[...], ov_ref[...] = _adamw(w_ref[...], grad, m_ref[...], v_ref[...])

    def other(k):
        return pl.BlockSpec((None, br, c), lambda i, pr: (pr[1] ^ k, i, 0))

    row = pl.BlockSpec((br, c), lambda i, pr: (i, 0))
    out = jax.ShapeDtypeStruct((r, c), F32)
    return pl.pallas_call(
        body, name=name,
        grid_spec=pltpu.PrefetchScalarGridSpec(
            num_scalar_prefetch=1, grid=(r // br,),
            in_specs=[pl.BlockSpec((None, None, br, c), lambda i, pr: (pr[0], pr[1], i, 0)),
                      pl.BlockSpec((None, br, c), lambda i, pr: (pr[1], i, 0)),
                      other(1), other(2), other(3), row, row, row],
            out_specs=[row] * 4),
        out_shape=[out] * 4,
        compiler_params=_cparams(1, VMEM_BIG),
    )(place, g, from_sibling, from_chips, from_chips, from_chips, w, m, v)


def adamw_small(items):
    n = len(items)

    def body(*refs):
        ins, outs = refs[:4 * n], refs[4 * n:]
        for i in range(n):
            w, g, m, v = (r[...] for r in ins[4 * i:4 * i + 4])
            outs[3 * i][...], outs[3 * i + 1][...], outs[3 * i + 2][...] = _adamw(w, g, m, v)

    out = [jax.ShapeDtypeStruct(it[0].shape, F32) for it in items for _ in range(3)]
    res = pl.pallas_call(body, name="adamw_small", out_shape=out)(*[a for it in items for a in it])
    return [res[3 * i:3 * i + 3] for i in range(n)]


def kernel(x, norm_mix, w_in, pool_w, pool_scale, w_pool_proj, conv_w, w_conv_out, w_o, norm_ffn, w_up, ffn_conv_w, ffn_conv_b, w_down, norm_final, loss_target, m_norm_mix, m_w_in, m_pool_w, m_pool_scale, m_w_pool_proj, m_conv_w, m_w_conv_out, m_w_o, m_norm_ffn, m_w_up, m_ffn_conv_w, m_ffn_conv_b, m_w_down, m_norm_final, v_norm_mix, v_w_in, v_pool_w, v_pool_scale, v_w_pool_proj, v_conv_w, v_w_conv_out, v_w_o, v_norm_ffn, v_w_up, v_ffn_conv_w, v_ffn_conv_b, v_w_down, v_norm_final):
    nb, seq, _ = x.shape
    t = nb * seq
    tm_in = min(TM_IN, t)
    tm_mix = min(TM_MIX, seq)
    tm_ffn = min(TM_FFN, seq)
    tk = min(TK_WGRAD, t)
    xt = x.reshape(t, D)
    tgt = loss_target.reshape(t, D)
    xi, yi, ci = _pos()
    me = 4 * xi + 2 * yi + ci
    place = jnp.stack([ci, 2 * xi + yi]).astype(jnp.int32)

    w_in_b = w_in[0].astype(BF16)
    w_in_parts = [all_gather_blocks(w_in_b[:, q * CG:(q + 1) * CG], "all_gather_w_in_%d" % q, cid)
                  for q, cid in enumerate((0, 12, 13))]
    mix_shard = jnp.concatenate(
        [w_pool_proj[0], w_conv_out[0], w_o[0], pool_w[0].reshape(NG * 32, CG)], axis=1).astype(BF16)
    wmix_g = all_gather_blocks(mix_shard, "all_gather_w_mix", 1)
    w_up_g = all_gather_blocks(w_up[0].astype(BF16), "all_gather_w_up", 2)
    w_dn_g = all_gather_blocks(w_down[0].astype(BF16), "all_gather_w_down", 3)
    taps = (jnp.pad(conv_w[0], ((0, 5), (0, D - 128))) + jnp.pad(ffn_conv_w[0], ((3, 2), (0, D - SH_UP))))
    taps_g = _exchange_small(taps, False, "all_gather_taps")
    wmix = wmix_g.reshape(D, MIX_COLS)
    pool_w_f = wmix_g[:, :, 3 * D:].reshape(NDEV, NG, 32, CG).transpose(1, 0, 2, 3).reshape(NG, CG, CG)
    w_dn_f = w_dn_g.reshape(NCH, SH_UP, D)
    conv_w_f = taps_g[:, 0:3, :128].transpose(1, 0, 2).reshape(3, D)
    fcw_f = taps_g[:, 3:6, :SH_UP]
    fcb_f = ffn_conv_b.reshape(NDEV, 1, SH_UP)
    gfin = norm_final.reshape(1, D)

    z, h1 = fwd_in(xt, norm_mix, w_in_parts[0], tm_in)
    for q in (1, 2):
        z = fwd_in_more(h1, w_in_parts[q], q, z, tm_in)
    z = z.reshape(NZT, t, CG)
    x1, y_pool, y_conv = fwd_mix(z, xt, pool_w_f, pool_scale, conv_w_f, wmix, tm_mix, seq)
    up, pre, act, h2, dx2, ffn_vec = fwd_ffn(x1, norm_ffn, w_up_g, fcw_f, fcb_f, w_dn_f, gfin, tgt, tm_ffn, seq,
                                             FFN_CHUNKS_PER_STEP)

    def reduce_group(full, names, shards, tag, collective_id):
        from_sib = reduce_scatter_d2d(full, "reduce_scatter_d2d_" + tag, collective_id)
        parts = [chip_partial(place, g, s, "chip_partial_" + nm) for g, s, nm in zip(full, from_sib, names)]
        from_chips = reduce_scatter_ici(parts, "reduce_scatter_ici_" + tag, collective_id + 1)
        res = {}
        for nm, g, s, fc, (w, m, v) in zip(names, full, from_sib, from_chips, shards):
            rc = g.shape[2:]
            outs = finish_adamw(place, g, s, fc, w.reshape(rc), m.reshape(rc), v.reshape(rc), "adamw_" + nm)
            res[nm] = [o.reshape(w.shape) for o in outs]
        return res

    big = {}
    d_up, dx1, g_ffn_vec, g_nffn = bwd_ffn(dx2, x1, norm_ffn, up, pre, w_up_g, fcw_f, w_dn_f, tm_ffn, seq,
                                           FFN_CHUNKS_PER_STEP)
    gw_dn = wgrad_down(act, dx2, tk)
    gw_up = wgrad_cols(h2, d_up.reshape(NDEV, t, SH_UP), 1, SH_UP, "wgrad_up", tk)
    big.update(reduce_group([gw_dn, gw_up], ["w_down", "w_up"],
                            [(w_down, m_w_down, v_w_down), (w_up, m_w_up, v_w_up)], "ffn", 4))
    dz, merged, p2, u, dyp, dyc, p, dpw, g_mix_vec = bwd_mix(
        dx1, z, y_pool, y_conv, pool_w_f, pool_scale, conv_w_f, wmix, tm_mix, seq)
    gw_in = wgrad_cols(h1, dz, 3, SH_IN, "wgrad_in", tk)
    big.update(reduce_group([gw_in], ["w_in"], [(w_in, m_w_in, v_w_in)], "in", 10))
    gw_o = wgrad_square(merged, dx1, "wgrad_o", tk)
    gw_pp = wgrad_square(p2, dyp, "wgrad_pool_proj", tk)
    big.update(reduce_group([gw_o, gw_pp], ["w_o", "w_pool_proj"],
                            [(w_o, m_w_o, v_w_o), (w_pool_proj, m_w_pool_proj, v_w_pool_proj)], "mix_a", 6))
    gw_co = wgrad_square(u, dyc, "wgrad_conv_out", tk)
    gw_pool = wgrad_pool(p, dpw, tk).reshape(NG, 4, 2, 32, CG).transpose(2, 1, 0, 3, 4).reshape(2, 4, NG * 32, CG)
    big.update(reduce_group([gw_co, gw_pool], ["w_conv_out", "pool_w"],
                            [(w_conv_out, m_w_conv_out, v_w_conv_out), (pool_w, m_pool_w, v_pool_w)], "mix_b", 8))
    grad_x, g_nmix = bwd_in(dz, w_in_parts, dx1, xt, norm_mix, min(TM_BWD_IN, t))

    red = _exchange_small(
        jnp.concatenate([g_nmix, g_mix_vec, g_nffn, ffn_vec, g_ffn_vec.reshape(8 * NDEV, D)], axis=0), True,
        "all_reduce_small")
    g_norm_mix, g_pool_scale, g_norm_ffn = red[0:1], red[8:9], red[16:17]
    g_conv_w = lax.dynamic_slice(red, (9, me * 128), (3, 128))
    g_norm_final = red[24]
    loss = red[25, 0]
    g_fcb = red[32:].reshape(NDEV, 8, D)[:, 0, :SH_UP].reshape(1, FF2)
    g_fcw = lax.dynamic_slice(red, (33 + 8 * me, 0), (3, SH_UP))
    grads = {"norm_mix": g_norm_mix, "pool_scale": g_pool_scale, "norm_ffn": g_norm_ffn, "norm_final": g_norm_final,
             "ffn_conv_b": g_fcb, "conv_w": g_conv_w.reshape(1, 3, 128), "ffn_conv_w": g_fcw.reshape(1, 3, SH_UP)}
    small_wmv = {"norm_mix": (norm_mix, m_norm_mix, v_norm_mix), "pool_scale": (pool_scale, m_pool_scale, v_pool_scale),
                 "norm_ffn": (norm_ffn, m_norm_ffn, v_norm_ffn), "norm_final": (norm_final, m_norm_final, v_norm_final),
                 "ffn_conv_b": (ffn_conv_b, m_ffn_conv_b, v_ffn_conv_b), "conv_w": (conv_w, m_conv_w, v_conv_w),
                 "ffn_conv_w": (ffn_conv_w, m_ffn_conv_w, v_ffn_conv_w)}
    small_names = list(small_wmv)
    flat2 = lambda a: a.reshape(-1, a.shape[-1])
    small_out = adamw_small([(flat2(small_wmv[nm][0]), flat2(grads[nm]), flat2(small_wmv[nm][1]),
                              flat2(small_wmv[nm][2])) for nm in small_names])
    small = {nm: [o.reshape(small_wmv[nm][0].shape) for o in outs] for nm, outs in zip(small_names, small_out)}

    order = ["norm_mix", "w_in", "pool_w", "pool_scale", "w_pool_proj", "conv_w", "w_conv_out", "w_o", "norm_ffn",
             "w_up", "ffn_conv_w", "ffn_conv_b", "w_down", "norm_final"]
    out = [loss, grad_x.reshape(nb, seq, D)]
    out += [big[nm][0] if nm in big else grads[nm] for nm in order]
    for idx in range(3):
        out += [big[nm][idx + 1] if nm in big else small[nm][idx] for nm in order]
    return tuple(out)
```

```python
import functools

import jax
import jax.numpy as jnp
from jax import lax
from jax.experimental import pallas as pl
from jax.experimental.pallas import tpu as pltpu
from jax.experimental.pallas import tpu_sc as plsc

F32 = jnp.float32
BF16 = jnp.bfloat16

NDEV = 8
D = 1024
NG = 4
CG = 256
WINS = (2, 4, 8, 16)
DIN = 6 * D
SH_IN = DIN // NDEV
NZT = DIN // CG
FF2 = 5632
SH_UP = FF2 // NDEV
FF = FF2 // 2
NCH = 4
SH_DN = FF // NDEV
RMS_EPS = 1e-6
HALO = 16

ADAM_LR = 0.001
ADAM_B1 = 0.9
ADAM_B2 = 0.999
ADAM_EPS = 1e-08
ADAM_WD = 0.01
ADAM_STEP = 10

TM_IN = 512
TM_BWD_IN = 256
TM_MIX = 256
TM_FFN = 256
FFN_CHUNKS_PER_STEP = 4
TK_WGRAD = 2048
MIX_POOL_PROJ, MIX_CONV_OUT, MIX_O = 0, 1, 2
MIX_COLS = 3 * D + CG
VMEM_BIG = 56 * 1024 * 1024
MESH = pl.DeviceIdType.MESH
ANY = pl.BlockSpec(memory_space=pl.ANY)


def _cparams(n_axes, vmem=None):
    return pltpu.CompilerParams(dimension_semantics=("arbitrary",) * n_axes, vmem_limit_bytes=vmem)


def _dot(a, b):
    return jnp.dot(a, b, preferred_element_type=F32)


def _dot_nt(a, b):
    return lax.dot_general(a, b, (((1,), (1,)), ((), ())), preferred_element_type=F32)


def _dot_tn(a, b):
    return lax.dot_general(a, b, (((0,), (0,)), ((), ())), preferred_element_type=F32)


def _shift_down(ext, s, lead):
    return pltpu.roll(ext, s, 0)[lead:]


def _shift_up(ext, s, tm):
    n = ext.shape[0]
    return pltpu.roll(ext, n - s, 0)[:tm]


def _rms_inv(x):
    return lax.rsqrt(jnp.mean(x * x, axis=-1, keepdims=True) + RMS_EPS)


def _rms_bwd(dh, xn, inv, g):
    dxn = dh * g
    return inv * (dxn - xn * jnp.mean(dxn * xn, axis=-1, keepdims=True))


def _pos():
    return lax.axis_index("x"), lax.axis_index("y"), lax.axis_index("c")


def _handshake(peers):
    barrier = pltpu.get_barrier_semaphore()
    for peer in peers:
        pl.semaphore_signal(barrier, inc=1, device_id=peer, device_id_type=MESH)
    pl.semaphore_wait(barrier, len(peers))


def _sequencer(body, out_type, n_sems, name, collective_id):
    return pl.kernel(
        body, out_type=out_type, mesh=plsc.ScalarSubcoreMesh(axis_name="sequencer", num_cores=1), name=name,
        scratch_types=[pltpu.SemaphoreType.DMA((n_sems,)), pltpu.SemaphoreType.DMA((n_sems,))],
        compiler_params=pltpu.CompilerParams(collective_id=collective_id))


def all_gather_blocks(shards, name, collective_id):
    n = len(shards)

    def body(*refs):
        ins, outs = refs[:n], refs[n:2 * n]
        send_sems, recv_sems = refs[2 * n:]
        x, y, c = _pos()
        sibling = (x, y, 1 - c)
        chips = [(1 - x, y), (x, 1 - y), (1 - x, 1 - y)]
        _handshake([sibling] + [(*chip, c) for chip in chips])

        def copy(w, k, block, to, src=None):
            slot = outs[w].at[4 * block[0] + 2 * block[1] + block[2]]
            return pltpu.make_async_remote_copy(
                src_ref=slot if src is None else src, dst_ref=slot,
                send_sem=send_sems.at[8 * w + k], recv_sem=recv_sems.at[8 * w + k], device_id=to, device_id_type=MESH)

        mine, first, passed = [], [], []
        for w in range(n):
            m = pltpu.make_async_copy(ins[w], outs[w].at[4 * x + 2 * y + c], send_sems.at[8 * w + 7])
            m.start()
            mine.append(m)
            first.append(copy(w, 0, (x, y, c), sibling, src=ins[w]))
            first += [copy(w, 1 + j, (x, y, c), (*chip, c), src=ins[w]) for j, chip in enumerate(chips)]
        for cp in first:
            cp.start()
        for w in range(n):
            for j, chip in enumerate(chips):
                copy(w, 1 + j, (*chip, c), (x, y, c)).wait_recv()
                fw = copy(w, 4 + j, (*chip, c), sibling)
                fw.start()
                passed.append(fw)
        for w in range(n):
            copy(w, 0, (x, y, 1 - c), (x, y, c)).wait_recv()
            for j, chip in enumerate(chips):
                copy(w, 4 + j, (*chip, 1 - c), (x, y, c)).wait_recv()
        for cp in first + passed:
            cp.wait_send()
        for m in mine:
            m.wait()

    out = [jax.ShapeDtypeStruct((NDEV,) + s.shape, s.dtype) for s in shards]
    return _sequencer(body, out, 8 * n, name, collective_id)(*shards)


def _exchange_small(v, reduce, name):
    rows = v.shape[0]

    def body(v_ref, out_ref, slots, send_sems, recv_sems, local_sem):
        x, y, c = _pos()
        me = 4 * x + 2 * y + c
        mine = pltpu.make_async_copy(v_ref, slots.at[me], local_sem)
        mine.start()
        offs = [(dx, dy, dc) for dx in (0, 1) for dy in (0, 1) for dc in (0, 1)][1:]

        def copy(k, src_slot, to):
            return pltpu.make_async_remote_copy(
                src_ref=v_ref, dst_ref=slots.at[src_slot], send_sem=send_sems.at[k], recv_sem=recv_sems.at[k],
                device_id=to, device_id_type=MESH)

        sends = []
        for k, (dx, dy, dc) in enumerate(offs):
            cp = copy(k, me, (x ^ dx, y ^ dy, c ^ dc))
            cp.start()
            sends.append(cp)
        for k, (dx, dy, dc) in enumerate(offs):
            copy(k, 4 * (x ^ dx) + 2 * (y ^ dy) + (c ^ dc), (x, y, c)).wait_recv()
        for cp in sends:
            cp.wait_send()
        mine.wait()
        if reduce:
            acc = slots[0]
            for d in range(1, NDEV):
                acc = acc + slots[d]
            out_ref[...] = acc
        else:
            out_ref[...] = slots[...]

    out = jax.ShapeDtypeStruct((rows, D) if reduce else (NDEV, rows, D), F32)
    return pl.pallas_call(
        body, name=name, out_shape=out,
        in_specs=[pl.BlockSpec(memory_space=pltpu.VMEM)], out_specs=pl.BlockSpec(memory_space=pltpu.VMEM),
        scratch_shapes=[pltpu.VMEM((NDEV, rows, D), F32), pltpu.SemaphoreType.DMA((7,)),
                        pltpu.SemaphoreType.DMA((7,)), pltpu.SemaphoreType.DMA],
    )(v)


def reduce_scatter_d2d(grads, name, collective_id):
    n = len(grads)

    def body(*refs):
        ins, outs = refs[:n], refs[n:2 * n]
        send_sems, recv_sems = refs[2 * n:]
        x, y, c = _pos()
        _handshake([(x, y, 1 - c)])
        cps = []
        for w in range(n):
            cp = pltpu.make_async_remote_copy(
                src_ref=ins[w].at[1 - c], dst_ref=outs[w], send_sem=send_sems.at[w], recv_sem=recv_sems.at[w],
                device_id=(x, y, 1 - c), device_id_type=MESH)
            cp.start()
            cps.append(cp)
        for cp in cps:
            cp.wait_recv()
        for cp in cps:
            cp.wait_send()

    out = [jax.ShapeDtypeStruct(g.shape[1:], F32) for g in grads]
    return _sequencer(body, out, n, name, collective_id)(*grads)


def reduce_scatter_ici(parts, name, collective_id):
    n = len(parts)

    def body(*refs):
        ins, outs = refs[:n], refs[n:2 * n]
        send_sems, recv_sems = refs[2 * n:]
        x, y, c = _pos()
        offs = [(1, 0), (0, 1), (1, 1)]
        _handshake([(x ^ dx, y ^ dy, c) for dx, dy in offs])
        cps = []
        for w in range(n):
            for k, (dx, dy) in enumerate(offs):
                ox, oy = x ^ dx, y ^ dy
                cp = pltpu.make_async_remote_copy(
                    src_ref=ins[w].at[2 * ox + oy], dst_ref=outs[w].at[2 * x + y],
                    send_sem=send_sems.at[3 * w + k], recv_sem=recv_sems.at[3 * w + k],
                    device_id=(ox, oy, c), device_id_type=MESH)
                cp.start()
                cps.append((cp, w, k, ox, oy))
        for cp, w, k, ox, oy in cps:
            pltpu.make_async_remote_copy(
                src_ref=ins[w].at[2 * ox + oy], dst_ref=outs[w].at[2 * ox + oy],
                send_sem=send_sems.at[3 * w + k], recv_sem=recv_sems.at[3 * w + k],
                device_id=(ox, oy, c), device_id_type=MESH).wait_recv()
        for cp, *_ in cps:
            cp.wait_send()

    out = [jax.ShapeDtypeStruct(p.shape, BF16) for p in parts]
    return _sequencer(body, out, 3 * n, name, collective_id)(*parts)


def fwd_in(x, g1, w_in_g, tm):
    t = x.shape[0]

    def body(x_ref, g_ref, w_ref, z_ref, h_ref):
        xf = x_ref[...]
        h = (xf * _rms_inv(xf) * g_ref[...]).astype(BF16)
        h_ref[...] = h
        for j in range(NDEV):
            r = _dot(h, w_ref[j])
            for q in range(3):
                z_ref[3 * j + q] = r[:, q * CG:(q + 1) * CG].astype(BF16)

    return pl.pallas_call(
        body, name="fwd_in", grid=(t // tm,),
        in_specs=[pl.BlockSpec((tm, D), lambda i: (i, 0)), pl.BlockSpec((1, D), lambda i: (0, 0)),
                  pl.BlockSpec((NDEV, D, SH_IN), lambda i: (0, 0, 0))],
        out_specs=[pl.BlockSpec((NZT, tm, CG), lambda i: (0, i, 0)), pl.BlockSpec((tm, D), lambda i: (i, 0))],
        out_shape=[jax.ShapeDtypeStruct((NZT, t, CG), BF16), jax.ShapeDtypeStruct((t, D), BF16)],
        compiler_params=_cparams(1, VMEM_BIG),
    )(x, g1, w_in_g)


def _pool_tile(z_ref, zh_ref, g, win, keep_hist, cnt):
    zt = z_ref[g].astype(F32)
    ext = jnp.concatenate([zh_ref[g].astype(F32) * keep_hist, zt], axis=0)
    s, sh = ext, 1
    while sh < win:
        s = s + pltpu.roll(s, sh, 0)
        sh *= 2
    return s[HALO:] / cnt - zt


def _conv_taps(ext, cur, w_ref, lanes, lead):
    x1 = _shift_down(ext, 1, lead)
    x2 = _shift_down(ext, 2, lead)
    out = w_ref[2:3, lanes] * cur + w_ref[1:2, lanes] * x1 + w_ref[0:1, lanes] * x2
    return out, x1, x2


def fwd_mix(z, x, pool_w, pool_scale, conv_w, wmix, tm, seq):
    t = x.shape[0]
    tps = seq // tm
    hb = tm // HALO

    def body(z_ref, zph_ref, zcvh_ref, x_ref, pw_ref, ps_ref, wpp_ref, cw_ref, wco_ref, wo_ref,
             x1_ref, yp_ref, yc_ref):
        it = pl.program_id(0) % tps
        keep_hist = jnp.where(it == 0, 0.0, 1.0)
        pos = it * tm + lax.broadcasted_iota(jnp.int32, (tm, 1), 0)
        p2 = []
        for g, win in enumerate(WINS):
            cnt = jnp.minimum(pos + 1, win).astype(F32)
            p = _pool_tile(z_ref, zph_ref, g, win, keep_hist, cnt)
            lanes = slice(g * CG, (g + 1) * CG)
            p2.append((_dot(p.astype(BF16), pw_ref[g]) * ps_ref[:, lanes]).astype(BF16))
        y_pool = _dot(jnp.concatenate(p2, axis=1), wpp_ref[...])
        u = []
        for q in range(NG):
            lanes = slice(q * CG, (q + 1) * CG)
            cv = z_ref[8 + q].astype(F32) * z_ref[12 + q].astype(F32)
            cvh = zcvh_ref[q].astype(F32) * zcvh_ref[4 + q].astype(F32) * keep_hist
            cc, _, _ = _conv_taps(jnp.concatenate([cvh, cv], axis=0), cv, cw_ref, lanes, HALO)
            u.append((z_ref[4 + q].astype(F32) * cc).astype(BF16))
        y_conv = _dot(jnp.concatenate(u, axis=1), wco_ref[...])
        ypb, ycb = y_pool.astype(BF16), y_conv.astype(BF16)
        yp_ref[...] = ypb
        yc_ref[...] = ycb
        merged = []
        for q in range(NG):
            lanes = slice(q * CG, (q + 1) * CG)
            sp = jax.nn.sigmoid(z_ref[16 + q].astype(F32))
            sc = jax.nn.sigmoid(z_ref[20 + q].astype(F32))
            merged.append((sp * ypb[:, lanes].astype(F32) + sc * ycb[:, lanes].astype(F32)).astype(BF16))
        x1_ref[...] = x_ref[...] + _dot(jnp.concatenate(merged, axis=1), wo_ref[...])

    def hist(i):
        return jnp.maximum(i * hb - 1, 0)

    const2 = lambda i: (0, 0)
    return pl.pallas_call(
        body, name="fwd_mix", grid=(t // tm,),
        in_specs=[pl.BlockSpec((NZT, tm, CG), lambda i: (0, i, 0)),
                  pl.BlockSpec((NG, HALO, CG), lambda i: (0, hist(i), 0)),
                  pl.BlockSpec((2 * NG, HALO, CG), lambda i: (1, hist(i), 0)),
                  pl.BlockSpec((tm, D), lambda i: (i, 0)),
                  pl.BlockSpec((NG, CG, CG), lambda i: (0, 0, 0)), pl.BlockSpec((1, D), const2),
                  pl.BlockSpec((D, D), lambda i: (0, MIX_POOL_PROJ)), pl.BlockSpec((3, D), const2),
                  pl.BlockSpec((D, D), lambda i: (0, MIX_CONV_OUT)), pl.BlockSpec((D, D), lambda i: (0, MIX_O))],
        out_specs=[pl.BlockSpec((tm, D), lambda i: (i, 0))] * 3,
        out_shape=[jax.ShapeDtypeStruct((t, D), F32), jax.ShapeDtypeStruct((t, D), BF16),
                   jax.ShapeDtypeStruct((t, D), BF16)],
        compiler_params=_cparams(1, VMEM_BIG),
    )(z, z, z, x, pool_w, pool_scale, wmix, conv_w, wmix, wmix)


def fwd_ffn(x1, g2, w_up_g, fcw, fcb, w_dn, gf, tgt, tm, seq, cps):
    t = x1.shape[0]
    tps = seq // tm

    def body(x1_ref, g2_ref, wup_ref, fcw_ref, fcb_ref, wdn_ref, gf_ref, tgt_ref,
             up_ref, pre_ref, act_ref, h2_ref, dx2_ref, vec_ref, hist_ref, d_ref):
        i, k = pl.program_id(0), pl.program_id(1)
        keep_hist = jnp.where(i % tps == 0, 0.0, 1.0)

        @pl.when((i == 0) & (k == 0))
        def _():
            vec_ref[...] = jnp.zeros_like(vec_ref)
            hist_ref[...] = jnp.zeros_like(hist_ref)

        @pl.when(k == 0)
        def _():
            x1v = x1_ref[...]
            h2_ref[...] = (x1v * _rms_inv(x1v) * g2_ref[...]).astype(BF16)
            d_ref[...] = jnp.zeros_like(d_ref)

        h2 = h2_ref[...]
        lanes = slice(0, SH_UP)
        d = d_ref[...]
        for c in range(cps):
            kc = k * cps + c
            conv = []
            for s in range(2):
                ub = _dot(h2, wup_ref[s, c]).astype(BF16)
                up_ref[s, c] = ub
                uf = ub.astype(F32)
                ext = jnp.concatenate([hist_ref[s, kc] * keep_hist, uf], axis=0)
                hist_ref[s, kc] = uf[tm - 8:]
                cc, _, _ = _conv_taps(ext, uf, fcw_ref.at[s, c], lanes, 8)
                conv.append(cc + fcb_ref[s, c])
                pre_ref[s, c] = conv[s].astype(BF16)
            a = (conv[0] * jax.nn.sigmoid(conv[0]) * conv[1]).astype(BF16)
            act_ref[c] = a
            d = d + _dot(a, wdn_ref[c])
        d_ref[...] = d

        @pl.when(k == NCH // cps - 1)
        def _():
            x2 = x1_ref[...] + d_ref[...]
            inv3 = _rms_inv(x2)
            xn = x2 * inv3
            diff = xn * gf_ref[...] - tgt_ref[...]
            dy = diff * (1.0 / D)
            vec_ref[0:1, :] += jnp.sum(dy * xn, axis=0, keepdims=True)
            vec_ref[1:2, :] += 0.5 * jnp.sum(jnp.mean(diff * diff, axis=-1))
            dx2_ref[...] = _rms_bwd(dy, xn, inv3, gf_ref[...])

    tile = lambda i, k: (i, 0)
    const2 = lambda i, k: (0, 0)
    pair = lambda i, k: (0, k, 0, 0)
    return pl.pallas_call(
        body, name="fwd_ffn", grid=(t // tm, NCH // cps),
        in_specs=[pl.BlockSpec((tm, D), tile), pl.BlockSpec((1, D), const2),
                  pl.BlockSpec((2, cps, D, SH_UP), pair), pl.BlockSpec((2, cps, 3, SH_UP), pair),
                  pl.BlockSpec((2, cps, 1, SH_UP), pair), pl.BlockSpec((cps, SH_UP, D), lambda i, k: (k, 0, 0)),
                  pl.BlockSpec((1, D), const2), pl.BlockSpec((tm, D), tile)],
        out_specs=[pl.BlockSpec((2, cps, tm, SH_UP), lambda i, k: (0, k, i, 0)),
                   pl.BlockSpec((2, cps, tm, SH_UP), lambda i, k: (0, k, i, 0)),
                   pl.BlockSpec((cps, tm, SH_UP), lambda i, k: (k, i, 0)),
                   pl.BlockSpec((tm, D), tile), pl.BlockSpec((tm, D), tile), pl.BlockSpec((8, D), const2)],
        out_shape=[jax.ShapeDtypeStruct((2, NCH, t, SH_UP), BF16), jax.ShapeDtypeStruct((2, NCH, t, SH_UP), BF16),
                   jax.ShapeDtypeStruct((NCH, t, SH_UP), BF16),
                   jax.ShapeDtypeStruct((t, D), BF16), jax.ShapeDtypeStruct((t, D), F32),
                   jax.ShapeDtypeStruct((8, D), F32)],
        scratch_shapes=[pltpu.VMEM((2, NCH, 8, SH_UP), F32), pltpu.VMEM((tm, D), F32)],
        compiler_params=_cparams(2, VMEM_BIG),
    )(x1, g2, w_up_g.reshape(2, NCH, D, SH_UP), fcw.reshape(2, NCH, 3, SH_UP), fcb.reshape(2, NCH, 1, SH_UP),
      w_dn, gf, tgt)


def bwd_ffn(dx2, x1, g2, up, pre, w_up_g, fcw, w_dn, tm, seq, cps):
    t = x1.shape[0]
    nt = t // tm
    tps = seq // tm

    def body(dx2_ref, x1_ref, g2_ref, up_ref, pre_ref, wup_ref, fcw_ref, wdn_ref,
             dup_ref, dx1_ref, gvec_ref, gn_ref, carry_ref, dh2_ref, acc_ref):
        i, k = pl.program_id(0), pl.program_id(1)
        it = (nt - 1 - i) % tps
        keep_next = jnp.where(it == tps - 1, 0.0, 1.0)

        @pl.when((i == 0) & (k == 0))
        def _():
            acc_ref[...] = jnp.zeros_like(acc_ref)
            gn_ref[...] = jnp.zeros_like(gn_ref)
            carry_ref[...] = jnp.zeros_like(carry_ref)

        @pl.when(k == 0)
        def _():
            dh2_ref[...] = jnp.zeros_like(dh2_ref)

        dxb = dx2_ref[...].astype(BF16)
        lanes = slice(0, SH_UP)
        dh2 = dh2_ref[...]
        for c in range(cps):
            kc = k * cps + c
            pre = [pre_ref[s, c].astype(F32) for s in range(2)]
            sg = jax.nn.sigmoid(pre[0])
            dact = _dot_nt(dxb, wdn_ref[c])
            dpre = [dact * pre[1] * (sg * (1.0 + pre[0] * (1.0 - sg))), dact * (pre[0] * sg)]
            for s in range(2):
                dc = dpre[s]
                ext = jnp.concatenate([dc, carry_ref[s, kc] * keep_next], axis=0)
                carry_ref[s, kc] = dc[:8]
                shifted = (_shift_up(ext, 2, tm), _shift_up(ext, 1, tm), dc)
                uf = up_ref[s, c].astype(F32)
                acc_ref[s, kc, 0:1, lanes] += jnp.sum(dc, axis=0, keepdims=True)
                for tap in range(3):
                    acc_ref[s, kc, tap + 1:tap + 2, lanes] += jnp.sum(shifted[tap] * uf, axis=0, keepdims=True)
                w = fcw_ref.at[s, c]
                du = w[2:3, :] * dc + w[1:2, :] * shifted[1] + w[0:1, :] * shifted[0]
                dub = du.astype(BF16)
                dup_ref[s, c] = dub
                dh2 = dh2 + _dot_nt(dub, wup_ref[s, c])
        dh2_ref[...] = dh2

        @pl.when(k == NCH // cps - 1)
        def _():
            x1v = x1_ref[...]
            inv2 = _rms_inv(x1v)
            xn = x1v * inv2
            gn_ref[0:1, :] += jnp.sum(dh2 * xn, axis=0, keepdims=True)
            dx1_ref[...] = dx2_ref[...] + _rms_bwd(dh2, xn, inv2, g2_ref[...])

        @pl.when((i == nt - 1) & (k == NCH // cps - 1))
        def _():
            gvec_ref[...] = acc_ref[...]

    rev = lambda i, k: (nt - 1 - i, 0)
    const2 = lambda i, k: (0, 0)
    pair = lambda i, k: (0, k, 0, 0)
    return pl.pallas_call(
        body, name="bwd_ffn", grid=(nt, NCH // cps),
        in_specs=[pl.BlockSpec((tm, D), rev), pl.BlockSpec((tm, D), rev), pl.BlockSpec((1, D), const2),
                  pl.BlockSpec((2, cps, tm, SH_UP), lambda i, k: (0, k, nt - 1 - i, 0)),
                  pl.BlockSpec((2, cps, tm, SH_UP), lambda i, k: (0, k, nt - 1 - i, 0)),
                  pl.BlockSpec((2, cps, D, SH_UP), pair), pl.BlockSpec((2, cps, 3, SH_UP), pair),
                  pl.BlockSpec((cps, SH_UP, D), lambda i, k: (k, 0, 0))],
        out_specs=[pl.BlockSpec((2, cps, tm, SH_UP), lambda i, k: (0, k, nt - 1 - i, 0)), pl.BlockSpec((tm, D), rev),
                   pl.BlockSpec((2, NCH, 8, D), lambda i, k: (0, 0, 0, 0)), pl.BlockSpec((8, D), const2)],
        out_shape=[jax.ShapeDtypeStruct((2, NCH, t, SH_UP), BF16), jax.ShapeDtypeStruct((t, D), F32),
                   jax.ShapeDtypeStruct((2, NCH, 8, D), F32), jax.ShapeDtypeStruct((8, D), F32)],
        scratch_shapes=[pltpu.VMEM((2, NCH, 8, SH_UP), F32), pltpu.VMEM((tm, D), F32),
                        pltpu.VMEM((2, NCH, 8, D), F32)],
        compiler_params=_cparams(2, VMEM_BIG),
    )(dx2, x1, g2, up, pre, w_up_g.reshape(2, NCH, D, SH_UP), fcw.reshape(2, NCH, 3, SH_UP), w_dn)


def bwd_mix(dx1, z, y_pool, y_conv, pool_w, pool_scale, conv_w, wmix, tm, seq):
    t = dx1.shape[0]
    nt = t // tm
    tps = seq // tm
    hb = tm // HALO

    def body(da_ref, z_ref, zph_ref, zcvh_ref, yp_ref, yc_ref, pw_ref, ps_ref, wpp_ref, cw_ref, wco_ref, wo_ref,
             dz_ref, mg_ref, p2_ref, u_ref, dyp_ref, dyc_ref, p_ref, dpw_ref, gvec_ref, cp_ref, cc_ref):
        i = pl.program_id(0)
        it = (nt - 1 - i) % tps
        keep_hist = jnp.where(it == 0, 0.0, 1.0)
        keep_next = jnp.where(it == tps - 1, 0.0, 1.0)
        pos = it * tm + lax.broadcasted_iota(jnp.int32, (tm, 1), 0)

        @pl.when(i == 0)
        def _():
            gvec_ref[...] = jnp.zeros_like(gvec_ref)
            cp_ref[...] = jnp.zeros_like(cp_ref)
            cc_ref[...] = jnp.zeros_like(cc_ref)

        dm = _dot_nt(da_ref[...].astype(BF16), wo_ref[...])
        merged, dyp, dyc = [], [], []
        for q in range(NG):
            lanes = slice(q * CG, (q + 1) * CG)
            sp = jax.nn.sigmoid(z_ref[16 + q].astype(F32))
            sc = jax.nn.sigmoid(z_ref[20 + q].astype(F32))
            yp = yp_ref[:, lanes].astype(F32)
            yc = yc_ref[:, lanes].astype(F32)
            dmq = dm[:, lanes]
            merged.append((sp * yp + sc * yc).astype(BF16))
            dyp.append((dmq * sp).astype(BF16))
            dyc.append((dmq * sc).astype(BF16))
            dz_ref[16 + q] = (dmq * yp * (sp * (1.0 - sp))).astype(BF16)
            dz_ref[20 + q] = (dmq * yc * (sc * (1.0 - sc))).astype(BF16)
        mg_ref[...] = jnp.concatenate(merged, axis=1)
        dypb = jnp.concatenate(dyp, axis=1)
        dycb = jnp.concatenate(dyc, axis=1)
        dyp_ref[...] = dypb
        dyc_ref[...] = dycb

        dp2 = _dot_nt(dypb, wpp_ref[...])
        p2 = []
        for g, win in enumerate(WINS):
            lanes = slice(g * CG, (g + 1) * CG)
            cnt = jnp.minimum(pos + 1, win).astype(F32)
            p = _pool_tile(z_ref, zph_ref, g, win, keep_hist, cnt)
            pb = p.astype(BF16)
            p_ref[g] = pb
            pw = _dot(pb, pw_ref[g])
            p2.append((pw * ps_ref[:, lanes]).astype(BF16))
            dp2g = dp2[:, lanes]
            gvec_ref[0:1, lanes] += jnp.sum(dp2g * pw, axis=0, keepdims=True)
            dpwb = (dp2g * ps_ref[:, lanes]).astype(BF16)
            dpw_ref[g] = dpwb
            dp = _dot_nt(dpwb, pw_ref[g])
            qv = dp / cnt
            ext = jnp.concatenate([qv, cp_ref[g] * keep_next], axis=0)
            cp_ref[g] = qv[:HALO]
            n = tm + HALO
            s, sh = ext, 1
            while sh < win:
                s = s + pltpu.roll(s, n - sh, 0)
                sh *= 2
            dz_ref[g] = (s[:tm] - dp).astype(BF16)
        p2_ref[...] = jnp.concatenate(p2, axis=1)

        du = _dot_nt(dycb, wco_ref[...])
        u = []
        for q in range(NG):
            lanes = slice(q * CG, (q + 1) * CG)
            zb = z_ref[4 + q].astype(F32)
            zc = z_ref[8 + q].astype(F32)
            zv = z_ref[12 + q].astype(F32)
            cv = zc * zv
            cvh = zcvh_ref[q].astype(F32) * zcvh_ref[4 + q].astype(F32) * keep_hist
            cc, cv1, cv2 = _conv_taps(jnp.concatenate([cvh, cv], axis=0), cv, cw_ref, lanes, HALO)
            u.append((zb * cc).astype(BF16))
            duq = du[:, lanes]
            dz_ref[4 + q] = (duq * cc).astype(BF16)
            dcc = duq * zb
            for tap, src in enumerate((cv2, cv1, cv)):
                gvec_ref[tap + 1:tap + 2, lanes] += jnp.sum(dcc * src, axis=0, keepdims=True)
            ext = jnp.concatenate([dcc, cc_ref[:, lanes] * keep_next], axis=0)
            cc_ref[:, lanes] = dcc[:8]
            dcv = (cw_ref[2:3, lanes] * dcc + cw_ref[1:2, lanes] * _shift_up(ext, 1, tm)
                   + cw_ref[0:1, lanes] * _shift_up(ext, 2, tm))
            dz_ref[8 + q] = (dcv * zv).astype(BF16)
            dz_ref[12 + q] = (dcv * zc).astype(BF16)
        u_ref[...] = jnp.concatenate(u, axis=1)

    def hist(i):
        return jnp.maximum((nt - 1 - i) * hb - 1, 0)

    rev = lambda i: (nt - 1 - i, 0)
    rev3 = lambda i: (0, nt - 1 - i, 0)
    const2 = lambda i: (0, 0)
    tok = jax.ShapeDtypeStruct((t, D), BF16)
    grp = jax.ShapeDtypeStruct((NG, t, CG), BF16)
    return pl.pallas_call(
        body, name="bwd_mix", grid=(nt,),
        in_specs=[pl.BlockSpec((tm, D), rev), pl.BlockSpec((NZT, tm, CG), rev3),
                  pl.BlockSpec((NG, HALO, CG), lambda i: (0, hist(i), 0)),
                  pl.BlockSpec((2 * NG, HALO, CG), lambda i: (1, hist(i), 0)),
                  pl.BlockSpec((tm, D), rev), pl.BlockSpec((tm, D), rev),
                  pl.BlockSpec((NG, CG, CG), lambda i: (0, 0, 0)), pl.BlockSpec((1, D), const2),
                  pl.BlockSpec((D, D), lambda i: (0, MIX_POOL_PROJ)), pl.BlockSpec((3, D), const2),
                  pl.BlockSpec((D, D), lambda i: (0, MIX_CONV_OUT)), pl.BlockSpec((D, D), lambda i: (0, MIX_O))],
        out_specs=[pl.BlockSpec((NZT, tm, CG), rev3)] + [pl.BlockSpec((tm, D), rev)] * 5
                  + [pl.BlockSpec((NG, tm, CG), rev3)] * 2 + [pl.BlockSpec((8, D), const2)],
        out_shape=[jax.ShapeDtypeStruct((NZT, t, CG), BF16), tok, tok, tok, tok, tok, grp, grp,
                   jax.ShapeDtypeStruct((8, D), F32)],
        scratch_shapes=[pltpu.VMEM((NG, HALO, CG), F32), pltpu.VMEM((8, D), F32)],
        compiler_params=_cparams(1, VMEM_BIG),
    )(dx1, z, z, z, y_pool, y_conv, pool_w, pool_scale, wmix, conv_w, wmix, wmix)


def bwd_in(dz, w_in_g, dx1, x, g1, tm):
    t = x.shape[0]

    def body(dz_ref, w_ref, dx1_ref, x_ref, g_ref, gx_ref, gn_ref):
        @pl.when(pl.program_id(0) == 0)
        def _():
            gn_ref[...] = jnp.zeros_like(gn_ref)

        dh = None
        for j in range(NDEV):
            dzc = jnp.concatenate([dz_ref[3 * j + q] for q in range(3)], axis=1)
            part = _dot_nt(dzc, w_ref[j])
            dh = part if dh is None else dh + part
        xv = x_ref[...]
        inv = _rms_inv(xv)
        xn = xv * inv
        gn_ref[0:1, :] += jnp.sum(dh * xn, axis=0, keepdims=True)
        gx_ref[...] = dx1_ref[...] + _rms_bwd(dh, xn, inv, g_ref[...])

    tile = lambda i: (i, 0)
    return pl.pallas_call(
        body, name="bwd_in", grid=(t // tm,),
        in_specs=[pl.BlockSpec((NZT, tm, CG), lambda i: (0, i, 0)),
                  pl.BlockSpec((NDEV, D, SH_IN), lambda i: (0, 0, 0)),
                  pl.BlockSpec((tm, D), tile), pl.BlockSpec((tm, D), tile), pl.BlockSpec((1, D), lambda i: (0, 0))],
        out_specs=[pl.BlockSpec((tm, D), tile), pl.BlockSpec((8, D), lambda i: (0, 0))],
        out_shape=[jax.ShapeDtypeStruct((t, D), F32), jax.ShapeDtypeStruct((8, D), F32)],
        compiler_params=_cparams(1, VMEM_BIG),
    )(dz, w_in_g, dx1, x, g1)


def _slot(j):
    return j % 2, j // 2


def wgrad_cols(a, b, q, name, tk):
    t, m = a.shape
    width = b.shape[3]

    def body(a_ref, b_ref, o_ref):
        @pl.when(pl.program_id(1) == 0)
        def _():
            o_ref[...] = jnp.zeros_like(o_ref)

        o_ref[...] += _dot_tn(a_ref[...], b_ref[...])

    return pl.pallas_call(
        body, name=name, grid=(NDEV, t // tk),
        in_specs=[pl.BlockSpec((tk, m), lambda j, k: (k, 0)),
                  pl.BlockSpec((None, None, tk, width), lambda j, k: (j, q, k, 0))],
        out_specs=pl.BlockSpec((None, None, m, width), lambda j, k: (j % 2, j // 2, 0, 0)),
        out_shape=jax.ShapeDtypeStruct((2, 4, m, width), F32),
        compiler_params=_cparams(2, VMEM_BIG),
    )(a, b)


def wgrad_down(act, dx2, tk):
    t = dx2.shape[0]

    def body(a_ref, b_ref, o_ref, acc_ref):
        kt = pl.program_id(1)

        @pl.when(kt == 0)
        def _():
            acc_ref[...] = jnp.zeros_like(acc_ref)

        acc_ref[...] += _dot_tn(a_ref[...], b_ref[...].astype(BF16))

        @pl.when(kt == pl.num_programs(1) - 1)
        def _():
            o_ref[0] = acc_ref[:SH_DN]
            o_ref[1] = acc_ref[SH_DN:]

    return pl.pallas_call(
        body, name="wgrad_down", grid=(NCH, t // tk),
        in_specs=[pl.BlockSpec((None, tk, SH_UP), lambda k, kt: (k, kt, 0)), pl.BlockSpec((tk, D), lambda k, kt: (kt, 0))],
        out_specs=pl.BlockSpec((2, None, SH_DN, D), lambda k, kt: (0, k, 0, 0)),
        out_shape=jax.ShapeDtypeStruct((2, 4, SH_DN, D), F32),
        scratch_shapes=[pltpu.VMEM((SH_UP, D), F32)],
        compiler_params=_cparams(2, VMEM_BIG),
    )(act, dx2)


def wgrad_square(a, b, name, tk):
    t = a.shape[0]

    def body(a_ref, b_ref, o_ref, acc_ref):
        kt = pl.program_id(0)

        @pl.when(kt == 0)
        def _():
            acc_ref[...] = jnp.zeros_like(acc_ref)

        acc_ref[...] += _dot_tn(a_ref[...], b_ref[...].astype(BF16))

        @pl.when(kt == pl.num_programs(0) - 1)
        def _():
            for j in range(NDEV):
                cc, xy = _slot(j)
                o_ref[cc, xy] = acc_ref[j * 128:(j + 1) * 128]

    return pl.pallas_call(
        body, name=name, grid=(t // tk,),
        in_specs=[pl.BlockSpec((tk, D), lambda k: (k, 0)), pl.BlockSpec((tk, D), lambda k: (k, 0))],
        out_specs=pl.BlockSpec((2, 4, 128, D), lambda k: (0, 0, 0, 0)),
        out_shape=jax.ShapeDtypeStruct((2, 4, 128, D), F32),
        scratch_shapes=[pltpu.VMEM((D, D), F32)],
        compiler_params=_cparams(1, VMEM_BIG),
    )(a, b)


def wgrad_pool(p, dpw, tk):
    t = p.shape[1]

    def body(a_ref, b_ref, o_ref):
        @pl.when(pl.program_id(0) == 0)
        def _():
            o_ref[...] = jnp.zeros_like(o_ref)

        for g in range(NG):
            o_ref[g] += _dot_tn(a_ref[g], b_ref[g])

    return pl.pallas_call(
        body, name="wgrad_pool", grid=(t // tk,),
        in_specs=[pl.BlockSpec((NG, tk, CG), lambda k: (0, k, 0))] * 2,
        out_specs=pl.BlockSpec((NG, CG, CG), lambda k: (0, 0, 0)),
        out_shape=jax.ShapeDtypeStruct((NG, CG, CG), F32),
        compiler_params=_cparams(1, VMEM_BIG),
    )(p, dpw)


def _adamw(w, g, m, v):
    m = ADAM_B1 * m + (1.0 - ADAM_B1) * g
    v = ADAM_B2 * v + (1.0 - ADAM_B2) * (g * g)
    m_hat = m / (1.0 - ADAM_B1 ** ADAM_STEP)
    v_hat = v / (1.0 - ADAM_B2 ** ADAM_STEP)
    delta = -ADAM_LR * (m_hat / (jnp.sqrt(v_hat) + ADAM_EPS) + ADAM_WD * w)
    return delta, m, v


def _row_block(r):
    return 512 if r % 512 == 0 else r


def chip_partial(place, g, from_sibling, name):
    _, _, r, c = g.shape

    def body(place_ref, g_ref, s_ref, o_ref):
        o_ref[...] = (g_ref[...] + s_ref[...]).astype(BF16)

    return pl.pallas_call(
        body, name=name,
        grid_spec=pltpu.PrefetchScalarGridSpec(
            num_scalar_prefetch=1, grid=(3,),
            in_specs=[pl.BlockSpec((None, None, r, c), lambda k, pr: (pr[0], pr[1] ^ (k + 1), 0, 0)),
                      pl.BlockSpec((None, r, c), lambda k, pr: (pr[1] ^ (k + 1), 0, 0))],
            out_specs=pl.BlockSpec((None, r, c), lambda k, pr: (pr[1] ^ (k + 1), 0, 0))),
        out_shape=jax.ShapeDtypeStruct((4, r, c), BF16),
        compiler_params=_cparams(1, VMEM_BIG),
    )(place, g, from_sibling)


def finish_adamw(place, gs, from_sibling, from_chips, w, m, v, name):
    n = len(gs)
    r = gs[0].shape[2]
    widths = [g.shape[3] for g in gs]
    c = sum(widths)
    br = _row_block(r)

    def body(place_ref, *refs):
        g_refs, s_refs, c_refs = refs[:n], refs[n:2 * n], refs[2 * n:5 * n]
        w_ref, m_ref, v_ref, og_ref, od_ref, om_ref, ov_ref = refs[5 * n:]
        cols = []
        for q in range(n):
            grad = g_refs[q][...] + s_refs[q][...]
            for k in range(3):
                grad = grad + c_refs[3 * q + k][...].astype(F32)
            cols.append(grad)
        grad = cols[0] if n == 1 else jnp.concatenate(cols, axis=1)
        og_ref[...] = grad
        od_ref[...], om_ref[...], ov_ref[...] = _adamw(w_ref[...], grad, m_ref[...], v_ref[...])

    def other(k, cq):
        return pl.BlockSpec((None, br, cq), lambda i, pr: (pr[1] ^ k, i, 0))

    row = pl.BlockSpec((br, c), lambda i, pr: (i, 0))
    out = jax.ShapeDtypeStruct((r, c), F32)
    in_specs = [pl.BlockSpec((None, None, br, cq), lambda i, pr: (pr[0], pr[1], i, 0)) for cq in widths]
    in_specs += [pl.BlockSpec((None, br, cq), lambda i, pr: (pr[1], i, 0)) for cq in widths]
    in_specs += [other(k, cq) for cq in widths for k in (1, 2, 3)]
    return pl.pallas_call(
        body, name=name,
        grid_spec=pltpu.PrefetchScalarGridSpec(
            num_scalar_prefetch=1, grid=(r // br,), in_specs=in_specs + [row, row, row], out_specs=[row] * 4),
        out_shape=[out] * 4,
        compiler_params=_cparams(1, VMEM_BIG),
    )(place, *gs, *from_sibling, *[fc for fc in from_chips for _ in range(3)], w, m, v)


def adamw_small(items):
    n = len(items)

    def body(*refs):
        ins, outs = refs[:4 * n], refs[4 * n:]
        for i in range(n):
            w, g, m, v = (r[...] for r in ins[4 * i:4 * i + 4])
            outs[3 * i][...], outs[3 * i + 1][...], outs[3 * i + 2][...] = _adamw(w, g, m, v)

    out = [jax.ShapeDtypeStruct(it[0].shape, F32) for it in items for _ in range(3)]
    res = pl.pallas_call(body, name="adamw_small", out_shape=out)(*[a for it in items for a in it])
    return [res[3 * i:3 * i + 3] for i in range(n)]


def kernel(x, norm_mix, w_in, pool_w, pool_scale, w_pool_proj, conv_w, w_conv_out, w_o, norm_ffn, w_up, ffn_conv_w, ffn_conv_b, w_down, norm_final, loss_target, m_norm_mix, m_w_in, m_pool_w, m_pool_scale, m_w_pool_proj, m_conv_w, m_w_conv_out, m_w_o, m_norm_ffn, m_w_up, m_ffn_conv_w, m_ffn_conv_b, m_w_down, m_norm_final, v_norm_mix, v_w_in, v_pool_w, v_pool_scale, v_w_pool_proj, v_conv_w, v_w_conv_out, v_w_o, v_norm_ffn, v_w_up, v_ffn_conv_w, v_ffn_conv_b, v_w_down, v_norm_final):
    nb, seq, _ = x.shape
    t = nb * seq
    tm_in = min(TM_IN, t)
    tm_mix = min(TM_MIX, seq)
    tm_ffn = min(TM_FFN, seq)
    tk = min(TK_WGRAD, t)
    xt = x.reshape(t, D)
    tgt = loss_target.reshape(t, D)
    xi, yi, ci = _pos()
    me = 4 * xi + 2 * yi + ci
    place = jnp.stack([ci, 2 * xi + yi]).astype(jnp.int32)

    tie = lax.optimization_barrier
    w_in_g, = all_gather_blocks([w_in[0].astype(BF16)], "all_gather_w_in", 0)
    taps = (jnp.pad(conv_w[0], ((0, 5), (0, D - 128))) + jnp.pad(ffn_conv_w[0], ((3, 2), (0, D - SH_UP))))
    taps_g = _exchange_small(taps, False, "all_gather_taps")
    mix_shard = jnp.concatenate(
        [w_pool_proj[0], w_conv_out[0], w_o[0], pool_w[0].reshape(NG * 32, CG)], axis=1).astype(BF16)
    mix_shard, taps_g = tie((mix_shard, taps_g))
    wmix_g, = all_gather_blocks([mix_shard], "all_gather_w_mix", 0)
    ffn_shards, w_in_g = tie(([w_up[0].astype(BF16), w_down[0].astype(BF16)], w_in_g))
    w_up_g, w_dn_g = all_gather_blocks(ffn_shards, "all_gather_w_ffn", 0)
    wmix = wmix_g.reshape(D, MIX_COLS)
    pool_w_f = wmix_g[:, :, 3 * D:].reshape(NDEV, NG, 32, CG).transpose(1, 0, 2, 3).reshape(NG, CG, CG)
    w_dn_f = w_dn_g.reshape(NCH, SH_UP, D)
    conv_w_f = taps_g[:, 0:3, :128].transpose(1, 0, 2).reshape(3, D)
    fcw_f = taps_g[:, 3:6, :SH_UP]
    fcb_f = ffn_conv_b.reshape(NDEV, 1, SH_UP)
    gfin = norm_final.reshape(1, D)

    z, h1 = fwd_in(xt, norm_mix, w_in_g, tm_in)
    x1, y_pool, y_conv = fwd_mix(z, xt, pool_w_f, pool_scale, conv_w_f, wmix, tm_mix, seq)
    up, pre, act, h2, dx2, ffn_vec = fwd_ffn(x1, norm_ffn, w_up_g, fcw_f, fcb_f, w_dn_f, gfin, tgt, tm_ffn, seq,
                                             FFN_CHUNKS_PER_STEP)

    def to_sibling(full, tag):
        return reduce_scatter_d2d(full, "reduce_scatter_d2d_" + tag, 1)

    def partials(full, from_sib, names):
        return [chip_partial(place, g, s, "chip_partial_" + nm) for g, s, nm in zip(full, from_sib, names)]

    def to_chips(parts, tag):
        return reduce_scatter_ici(parts, "reduce_scatter_ici_" + tag, 2)

    def finish(nm, gs, from_sib, from_chips, wmv):
        w, m, v = wmv
        rc = (gs[0].shape[2], sum(g.shape[3] for g in gs))
        outs = finish_adamw(place, gs, from_sib, from_chips, w.reshape(rc), m.reshape(rc), v.reshape(rc), "adamw_" + nm)
        return [o.reshape(w.shape) for o in outs]

    big = {}
    gw_dn = wgrad_down(act, dx2, tk)
    sib_dn = to_sibling([gw_dn], "w_down")
    d_up, dx1, g_ffn_vec, g_nffn = bwd_ffn(dx2, x1, norm_ffn, up, pre, w_up_g, fcw_f, w_dn_f, tm_ffn, seq,
                                           FFN_CHUNKS_PER_STEP)
    d_up, part_dn = tie((d_up, partials([gw_dn], sib_dn, ["w_down"])))
    chips_dn = to_chips(part_dn, "w_down")
    gw_up = wgrad_cols(h2, d_up.reshape(NDEV, 1, t, SH_UP), 0, "wgrad_up", tk)
    sib_up = to_sibling([gw_up], "w_up")
    dz, merged, p2, u, dyp, dyc, p, dpw, g_mix_vec = bwd_mix(
        dx1, z, y_pool, y_conv, pool_w_f, pool_scale, conv_w_f, wmix, tm_mix, seq)
    merged, part_up = tie((merged, partials([gw_up], sib_up, ["w_up"])))
    chips_up = to_chips(part_up, "w_up")
    gw_o = wgrad_square(merged, dx1, "wgrad_o", tk)
    gw_pp = wgrad_square(p2, dyp, "wgrad_pool_proj", tk)
    sib_a = to_sibling([gw_o, gw_pp], "mix_a")
    gw_co = wgrad_square(u, dyc, "wgrad_conv_out", tk)
    gw_pool = wgrad_pool(p, dpw, tk).reshape(NG, 4, 2, 32, CG).transpose(2, 1, 0, 3, 4).reshape(2, 4, NG * 32, CG)
    sib_b = to_sibling([gw_co, gw_pool], "mix_b")
    dz8 = dz.reshape(NDEV, 3, t, CG)
    gw_in, sib_in, chips_in = [None] * 3, [None] * 3, [None] * 3
    gw_in[0] = wgrad_cols(h1, dz8, 0, "wgrad_in_0", tk)
    sib_in[0] = to_sibling([gw_in[0]], "w_in_0")
    h1, part_a, part_b = tie((h1, partials([gw_o, gw_pp], sib_a, ["w_o", "w_pool_proj"]),
                              partials([gw_co, gw_pool], sib_b, ["w_conv_out", "pool_w"])))
    chips_a = to_chips(part_a, "mix_a")
    chips_b = to_chips(part_b, "mix_b")
    gw_in[1] = wgrad_cols(h1, dz8, 1, "wgrad_in_1", tk)
    sib_in[1] = to_sibling([gw_in[1]], "w_in_1")
    h1, part_in = tie((h1, partials([gw_in[0]], sib_in[0], ["w_in_0"])))
    chips_in[0] = to_chips(part_in, "w_in_0")
    gw_in[2] = wgrad_cols(h1, dz8, 2, "wgrad_in_2", tk)
    sib_in[2] = to_sibling([gw_in[2]], "w_in_2")
    dx1, part_in1, part_in2 = tie((dx1, partials([gw_in[1]], sib_in[1], ["w_in_1"]),
                                   partials([gw_in[2]], sib_in[2], ["w_in_2"])))
    chips_in[1] = to_chips(part_in1, "w_in_1")
    chips_in[2] = to_chips(part_in2, "w_in_2")
    grad_x, g_nmix = bwd_in(dz, w_in_g, dx1, xt, norm_mix, min(TM_BWD_IN, t))

    big["w_down"] = finish("w_down", [gw_dn], sib_dn, chips_dn, (w_down, m_w_down, v_w_down))
    big["w_up"] = finish("w_up", [gw_up], sib_up, chips_up, (w_up, m_w_up, v_w_up))
    big["w_o"] = finish("w_o", [gw_o], sib_a[:1], chips_a[:1], (w_o, m_w_o, v_w_o))
    big["w_pool_proj"] = finish("w_pool_proj", [gw_pp], sib_a[1:], chips_a[1:], (w_pool_proj, m_w_pool_proj, v_w_pool_proj))
    big["w_conv_out"] = finish("w_conv_out", [gw_co], sib_b[:1], chips_b[:1], (w_conv_out, m_w_conv_out, v_w_conv_out))
    big["pool_w"] = finish("pool_w", [gw_pool], sib_b[1:], chips_b[1:], (pool_w, m_pool_w, v_pool_w))
    big["w_in"] = finish("w_in", gw_in, [s[0] for s in sib_in], [c[0] for c in chips_in], (w_in, m_w_in, v_w_in))

    red = _exchange_small(
        jnp.concatenate([g_nmix, g_mix_vec, g_nffn, ffn_vec, g_ffn_vec.reshape(8 * NDEV, D)], axis=0), True,
        "all_reduce_small")
    g_norm_mix, g_pool_scale, g_norm_ffn = red[0:1], red[8:9], red[16:17]
    g_conv_w = lax.dynamic_slice(red, (9, me * 128), (3, 128))
    g_norm_final = red[24]
    loss = red[25, 0]
    g_fcb = red[32:].reshape(NDEV, 8, D)[:, 0, :SH_UP].reshape(1, FF2)
    g_fcw = lax.dynamic_slice(red, (33 + 8 * me, 0), (3, SH_UP))
    grads = {"norm_mix": g_norm_mix, "pool_scale": g_pool_scale, "norm_ffn": g_norm_ffn, "norm_final": g_norm_final,
             "ffn_conv_b": g_fcb, "conv_w": g_conv_w.reshape(1, 3, 128), "ffn_conv_w": g_fcw.reshape(1, 3, SH_UP)}
    small_wmv = {"norm_mix": (norm_mix, m_norm_mix, v_norm_mix), "pool_scale": (pool_scale, m_pool_scale, v_pool_scale),
                 "norm_ffn": (norm_ffn, m_norm_ffn, v_norm_ffn), "norm_final": (norm_final, m_norm_final, v_norm_final),
                 "ffn_conv_b": (ffn_conv_b, m_ffn_conv_b, v_ffn_conv_b), "conv_w": (conv_w, m_conv_w, v_conv_w),
                 "ffn_conv_w": (ffn_conv_w, m_ffn_conv_w, v_ffn_conv_w)}
    small_names = list(small_wmv)
    flat2 = lambda a: a.reshape(-1, a.shape[-1])
    small_out = adamw_small([(flat2(small_wmv[nm][0]), flat2(grads[nm]), flat2(small_wmv[nm][1]),
                              flat2(small_wmv[nm][2])) for nm in small_names])
    small = {nm: [o.reshape(small_wmv[nm][0].shape) for o in outs] for nm, outs in zip(small_names, small_out)}

    order = ["norm_mix", "w_in", "pool_w", "pool_scale", "w_pool_proj", "conv_w", "w_conv_out", "w_o", "norm_ffn",
             "w_up", "ffn_conv_w", "ffn_conv_b", "w_down", "norm_final"]
    out = [loss, grad_x.reshape(nb, seq, D)]
    out += [big[nm][0] if nm in big else grads[nm] for nm in order]
    for idx in range(3):
        out += [big[nm][idx + 1] if nm in big else small[nm][idx] for nm in order]
    return tuple(out)
```

```python
import functools

import jax
import jax.numpy as jnp
from jax import lax
from jax.experimental import pallas as pl
from jax.experimental.pallas import tpu as pltpu
from jax.experimental.pallas import tpu_sc as plsc

F32 = jnp.float32
BF16 = jnp.bfloat16

NDEV = 8
D = 1024
NG = 4
CG = 256
WINS = (2, 4, 8, 16)
DIN = 6 * D
SH_IN = DIN // NDEV
NZT = DIN // CG
FF2 = 5632
SH_UP = FF2 // NDEV
FF = FF2 // 2
NCH = 4
SH_DN = FF // NDEV
RMS_EPS = 1e-6
HALO = 16

ADAM_LR = 0.001
ADAM_B1 = 0.9
ADAM_B2 = 0.999
ADAM_EPS = 1e-08
ADAM_WD = 0.01
ADAM_STEP = 10

TM_IN = 512
TM_BWD_IN = 256
TM_MIX = 256
TM_FFN = 256
FFN_CHUNKS_PER_STEP = 4
TK_WGRAD = 2048
MIX_POOL_PROJ, MIX_CONV_OUT, MIX_O = 0, 1, 2
MIX_COLS = 3 * D + CG
VMEM_BIG = 56 * 1024 * 1024
MESH = pl.DeviceIdType.MESH
ANY = pl.BlockSpec(memory_space=pl.ANY)


def _cparams(n_axes, vmem=None):
    return pltpu.CompilerParams(dimension_semantics=("arbitrary",) * n_axes, vmem_limit_bytes=vmem)


def _dot(a, b):
    return jnp.dot(a, b, preferred_element_type=F32)


def _dot_nt(a, b):
    return lax.dot_general(a, b, (((1,), (1,)), ((), ())), preferred_element_type=F32)


def _dot_tn(a, b):
    return lax.dot_general(a, b, (((0,), (0,)), ((), ())), preferred_element_type=F32)


def _shift_down(ext, s, lead):
    return pltpu.roll(ext, s, 0)[lead:]


def _shift_up(ext, s, tm):
    n = ext.shape[0]
    return pltpu.roll(ext, n - s, 0)[:tm]


def _rms_inv(x):
    return lax.rsqrt(jnp.mean(x * x, axis=-1, keepdims=True) + RMS_EPS)


def _rms_bwd(dh, xn, inv, g):
    dxn = dh * g
    return inv * (dxn - xn * jnp.mean(dxn * xn, axis=-1, keepdims=True))


def _pos():
    return lax.axis_index("x"), lax.axis_index("y"), lax.axis_index("c")


def _handshake(peers):
    barrier = pltpu.get_barrier_semaphore()
    for peer in peers:
        pl.semaphore_signal(barrier, inc=1, device_id=peer, device_id_type=MESH)
    pl.semaphore_wait(barrier, len(peers))


def _sequencer(body, out_type, n_sems, name, collective_id):
    return pl.kernel(
        body, out_type=out_type, mesh=plsc.ScalarSubcoreMesh(axis_name="sequencer", num_cores=1), name=name,
        scratch_types=[pltpu.SemaphoreType.DMA((n_sems,)), pltpu.SemaphoreType.DMA((n_sems,))],
        compiler_params=pltpu.CompilerParams(collective_id=collective_id))


def all_gather_blocks(shards, name, collective_id):
    n = len(shards)

    def body(*refs):
        ins, outs = refs[:n], refs[n:2 * n]
        send_sems, recv_sems = refs[2 * n:]
        x, y, c = _pos()
        sibling = (x, y, 1 - c)
        chips = [(1 - x, y), (x, 1 - y), (1 - x, 1 - y)]
        _handshake([sibling] + [(*chip, c) for chip in chips])

        def copy(w, k, block, to, src=None):
            slot = outs[w].at[4 * block[0] + 2 * block[1] + block[2]]
            return pltpu.make_async_remote_copy(
                src_ref=slot if src is None else src, dst_ref=slot,
                send_sem=send_sems.at[8 * w + k], recv_sem=recv_sems.at[8 * w + k], device_id=to, device_id_type=MESH)

        mine, first, passed = [], [], []
        for w in range(n):
            m = pltpu.make_async_copy(ins[w], outs[w].at[4 * x + 2 * y + c], send_sems.at[8 * w + 7])
            m.start()
            mine.append(m)
            first.append(copy(w, 0, (x, y, c), sibling, src=ins[w]))
            first += [copy(w, 1 + j, (x, y, c), (*chip, c), src=ins[w]) for j, chip in enumerate(chips)]
        for cp in first:
            cp.start()
        for w in range(n):
            for j, chip in enumerate(chips):
                copy(w, 1 + j, (*chip, c), (x, y, c)).wait_recv()
                fw = copy(w, 4 + j, (*chip, c), sibling)
                fw.start()
                passed.append(fw)
        for w in range(n):
            copy(w, 0, (x, y, 1 - c), (x, y, c)).wait_recv()
            for j, chip in enumerate(chips):
                copy(w, 4 + j, (*chip, 1 - c), (x, y, c)).wait_recv()
        for cp in first + passed:
            cp.wait_send()
        for m in mine:
            m.wait()

    out = [jax.ShapeDtypeStruct((NDEV,) + s.shape, s.dtype) for s in shards]
    return _sequencer(body, out, 8 * n, name, collective_id)(*shards)


def _exchange_small(v, reduce, name):
    rows = v.shape[0]

    def body(v_ref, out_ref, slots, send_sems, recv_sems, local_sem):
        x, y, c = _pos()
        me = 4 * x + 2 * y + c
        mine = pltpu.make_async_copy(v_ref, slots.at[me], local_sem)
        mine.start()
        offs = [(dx, dy, dc) for dx in (0, 1) for dy in (0, 1) for dc in (0, 1)][1:]

        def copy(k, src_slot, to):
            return pltpu.make_async_remote_copy(
                src_ref=v_ref, dst_ref=slots.at[src_slot], send_sem=send_sems.at[k], recv_sem=recv_sems.at[k],
                device_id=to, device_id_type=MESH)

        sends = []
        for k, (dx, dy, dc) in enumerate(offs):
            cp = copy(k, me, (x ^ dx, y ^ dy, c ^ dc))
            cp.start()
            sends.append(cp)
        for k, (dx, dy, dc) in enumerate(offs):
            copy(k, 4 * (x ^ dx) + 2 * (y ^ dy) + (c ^ dc), (x, y, c)).wait_recv()
        for cp in sends:
            cp.wait_send()
        mine.wait()
        if reduce:
            acc = slots[0]
            for d in range(1, NDEV):
                acc = acc + slots[d]
            out_ref[...] = acc
        else:
            out_ref[...] = slots[...]

    out = jax.ShapeDtypeStruct((rows, D) if reduce else (NDEV, rows, D), F32)
    return pl.pallas_call(
        body, name=name, out_shape=out,
        in_specs=[pl.BlockSpec(memory_space=pltpu.VMEM)], out_specs=pl.BlockSpec(memory_space=pltpu.VMEM),
        scratch_shapes=[pltpu.VMEM((NDEV, rows, D), F32), pltpu.SemaphoreType.DMA((7,)),
                        pltpu.SemaphoreType.DMA((7,)), pltpu.SemaphoreType.DMA],
    )(v)


def reduce_scatter_d2d(grads, name, collective_id):
    n = len(grads)

    def body(*refs):
        ins, outs = refs[:n], refs[n:2 * n]
        send_sems, recv_sems = refs[2 * n:]
        x, y, c = _pos()
        _handshake([(x, y, 1 - c)])
        cps = []
        for w in range(n):
            cp = pltpu.make_async_remote_copy(
                src_ref=ins[w].at[1 - c], dst_ref=outs[w], send_sem=send_sems.at[w], recv_sem=recv_sems.at[w],
                device_id=(x, y, 1 - c), device_id_type=MESH)
            cp.start()
            cps.append(cp)
        for cp in cps:
            cp.wait_recv()
        for cp in cps:
            cp.wait_send()

    out = [jax.ShapeDtypeStruct(g.shape[1:], F32) for g in grads]
    return _sequencer(body, out, n, name, collective_id)(*grads)


def reduce_scatter_ici(parts, name, collective_id):
    n = len(parts)

    def body(*refs):
        ins, outs = refs[:n], refs[n:2 * n]
        send_sems, recv_sems = refs[2 * n:]
        x, y, c = _pos()
        offs = [(1, 0), (0, 1), (1, 1)]
        _handshake([(x ^ dx, y ^ dy, c) for dx, dy in offs])
        cps = []
        for w in range(n):
            for k, (dx, dy) in enumerate(offs):
                ox, oy = x ^ dx, y ^ dy
                cp = pltpu.make_async_remote_copy(
                    src_ref=ins[w].at[2 * ox + oy], dst_ref=outs[w].at[2 * x + y],
                    send_sem=send_sems.at[3 * w + k], recv_sem=recv_sems.at[3 * w + k],
                    device_id=(ox, oy, c), device_id_type=MESH)
                cp.start()
                cps.append((cp, w, k, ox, oy))
        for cp, w, k, ox, oy in cps:
            pltpu.make_async_remote_copy(
                src_ref=ins[w].at[2 * ox + oy], dst_ref=outs[w].at[2 * ox + oy],
                send_sem=send_sems.at[3 * w + k], recv_sem=recv_sems.at[3 * w + k],
                device_id=(ox, oy, c), device_id_type=MESH).wait_recv()
        for cp, *_ in cps:
            cp.wait_send()

    out = [jax.ShapeDtypeStruct(p.shape, BF16) for p in parts]
    return _sequencer(body, out, 3 * n, name, collective_id)(*parts)


def fwd_in(x, g1, w_in_g, tm):
    t = x.shape[0]

    def body(x_ref, g_ref, w_ref, z_ref, h_ref):
        xf = x_ref[...]
        h = (xf * _rms_inv(xf) * g_ref[...]).astype(BF16)
        h_ref[...] = h
        for j in range(NDEV):
            r = _dot(h, w_ref[j])
            for q in range(3):
                z_ref[3 * j + q] = r[:, q * CG:(q + 1) * CG].astype(BF16)

    return pl.pallas_call(
        body, name="fwd_in", grid=(t // tm,),
        in_specs=[pl.BlockSpec((tm, D), lambda i: (i, 0)), pl.BlockSpec((1, D), lambda i: (0, 0)),
                  pl.BlockSpec((NDEV, D, SH_IN), lambda i: (0, 0, 0))],
        out_specs=[pl.BlockSpec((NZT, tm, CG), lambda i: (0, i, 0)), pl.BlockSpec((tm, D), lambda i: (i, 0))],
        out_shape=[jax.ShapeDtypeStruct((NZT, t, CG), BF16), jax.ShapeDtypeStruct((t, D), BF16)],
        compiler_params=_cparams(1, VMEM_BIG),
    )(x, g1, w_in_g)


def _pool_tile(z_ref, zh_ref, g, win, keep_hist, cnt):
    zt = z_ref[g].astype(F32)
    ext = jnp.concatenate([zh_ref[g].astype(F32) * keep_hist, zt], axis=0)
    s, sh = ext, 1
    while sh < win:
        s = s + pltpu.roll(s, sh, 0)
        sh *= 2
    return s[HALO:] / cnt - zt


def _conv_taps(ext, cur, w_ref, lanes, lead):
    x1 = _shift_down(ext, 1, lead)
    x2 = _shift_down(ext, 2, lead)
    out = w_ref[2:3, lanes] * cur + w_ref[1:2, lanes] * x1 + w_ref[0:1, lanes] * x2
    return out, x1, x2


def fwd_mix(z, x, pool_w, pool_scale, conv_w, wmix, tm, seq):
    t = x.shape[0]
    tps = seq // tm
    hb = tm // HALO

    def body(z_ref, zph_ref, zcvh_ref, x_ref, pw_ref, ps_ref, wpp_ref, cw_ref, wco_ref, wo_ref,
             x1_ref, yp_ref, yc_ref):
        it = pl.program_id(0) % tps
        keep_hist = jnp.where(it == 0, 0.0, 1.0)
        pos = it * tm + lax.broadcasted_iota(jnp.int32, (tm, 1), 0)
        p2 = []
        for g, win in enumerate(WINS):
            cnt = jnp.minimum(pos + 1, win).astype(F32)
            p = _pool_tile(z_ref, zph_ref, g, win, keep_hist, cnt)
            lanes = slice(g * CG, (g + 1) * CG)
            p2.append((_dot(p.astype(BF16), pw_ref[g]) * ps_ref[:, lanes]).astype(BF16))
        y_pool = _dot(jnp.concatenate(p2, axis=1), wpp_ref[...])
        u = []
        for q in range(NG):
            lanes = slice(q * CG, (q + 1) * CG)
            cv = z_ref[8 + q].astype(F32) * z_ref[12 + q].astype(F32)
            cvh = zcvh_ref[q].astype(F32) * zcvh_ref[4 + q].astype(F32) * keep_hist
            cc, _, _ = _conv_taps(jnp.concatenate([cvh, cv], axis=0), cv, cw_ref, lanes, HALO)
            u.append((z_ref[4 + q].astype(F32) * cc).astype(BF16))
        y_conv = _dot(jnp.concatenate(u, axis=1), wco_ref[...])
        ypb, ycb = y_pool.astype(BF16), y_conv.astype(BF16)
        yp_ref[...] = ypb
        yc_ref[...] = ycb
        merged = []
        for q in range(NG):
            lanes = slice(q * CG, (q + 1) * CG)
            sp = jax.nn.sigmoid(z_ref[16 + q].astype(F32))
            sc = jax.nn.sigmoid(z_ref[20 + q].astype(F32))
            merged.append((sp * ypb[:, lanes].astype(F32) + sc * ycb[:, lanes].astype(F32)).astype(BF16))
        x1_ref[...] = x_ref[...] + _dot(jnp.concatenate(merged, axis=1), wo_ref[...])

    def hist(i):
        return jnp.maximum(i * hb - 1, 0)

    const2 = lambda i: (0, 0)
    return pl.pallas_call(
        body, name="fwd_mix", grid=(t // tm,),
        in_specs=[pl.BlockSpec((NZT, tm, CG), lambda i: (0, i, 0)),
                  pl.BlockSpec((NG, HALO, CG), lambda i: (0, hist(i), 0)),
                  pl.BlockSpec((2 * NG, HALO, CG), lambda i: (1, hist(i), 0)),
                  pl.BlockSpec((tm, D), lambda i: (i, 0)),
                  pl.BlockSpec((NG, CG, CG), lambda i: (0, 0, 0)), pl.BlockSpec((1, D), const2),
                  pl.BlockSpec((D, D), lambda i: (0, MIX_POOL_PROJ)), pl.BlockSpec((3, D), const2),
                  pl.BlockSpec((D, D), lambda i: (0, MIX_CONV_OUT)), pl.BlockSpec((D, D), lambda i: (0, MIX_O))],
        out_specs=[pl.BlockSpec((tm, D), lambda i: (i, 0))] * 3,
        out_shape=[jax.ShapeDtypeStruct((t, D), F32), jax.ShapeDtypeStruct((t, D), BF16),
                   jax.ShapeDtypeStruct((t, D), BF16)],
        compiler_params=_cparams(1, VMEM_BIG),
    )(z, z, z, x, pool_w, pool_scale, wmix, conv_w, wmix, wmix)


def fwd_ffn(x1, g2, w_up_g, fcw, fcb, w_dn, gf, tgt, tm, seq, cps):
    t = x1.shape[0]
    tps = seq // tm

    def body(x1_ref, g2_ref, wup_ref, fcw_ref, fcb_ref, wdn_ref, gf_ref, tgt_ref,
             up_ref, pre_ref, act_ref, h2_ref, dx2_ref, vec_ref, hist_ref, d_ref):
        i, k = pl.program_id(0), pl.program_id(1)
        keep_hist = jnp.where(i % tps == 0, 0.0, 1.0)

        @pl.when((i == 0) & (k == 0))
        def _():
            vec_ref[...] = jnp.zeros_like(vec_ref)
            hist_ref[...] = jnp.zeros_like(hist_ref)

        @pl.when(k == 0)
        def _():
            x1v = x1_ref[...]
            h2_ref[...] = (x1v * _rms_inv(x1v) * g2_ref[...]).astype(BF16)
            d_ref[...] = jnp.zeros_like(d_ref)

        h2 = h2_ref[...]
        lanes = slice(0, SH_UP)
        d = d_ref[...]
        for c in range(cps):
            kc = k * cps + c
            conv = []
            for s in range(2):
                ub = _dot(h2, wup_ref[s, c]).astype(BF16)
                up_ref[s, c] = ub
                uf = ub.astype(F32)
                ext = jnp.concatenate([hist_ref[s, kc] * keep_hist, uf], axis=0)
                hist_ref[s, kc] = uf[tm - 8:]
                cc, _, _ = _conv_taps(ext, uf, fcw_ref.at[s, c], lanes, 8)
                conv.append(cc + fcb_ref[s, c])
                pre_ref[s, c] = conv[s].astype(BF16)
            a = (conv[0] * jax.nn.sigmoid(conv[0]) * conv[1]).astype(BF16)
            act_ref[c] = a
            d = d + _dot(a, wdn_ref[c])
        d_ref[...] = d

        @pl.when(k == NCH // cps - 1)
        def _():
            x2 = x1_ref[...] + d_ref[...]
            inv3 = _rms_inv(x2)
            xn = x2 * inv3
            diff = xn * gf_ref[...] - tgt_ref[...]
            dy = diff * (1.0 / D)
            vec_ref[0:1, :] += jnp.sum(dy * xn, axis=0, keepdims=True)
            vec_ref[1:2, :] += 0.5 * jnp.sum(jnp.mean(diff * diff, axis=-1))
            dx2_ref[...] = _rms_bwd(dy, xn, inv3, gf_ref[...])

    tile = lambda i, k: (i, 0)
    const2 = lambda i, k: (0, 0)
    pair = lambda i, k: (0, k, 0, 0)
    return pl.pallas_call(
        body, name="fwd_ffn", grid=(t // tm, NCH // cps),
        in_specs=[pl.BlockSpec((tm, D), tile), pl.BlockSpec((1, D), const2),
                  pl.BlockSpec((2, cps, D, SH_UP), pair), pl.BlockSpec((2, cps, 3, SH_UP), pair),
                  pl.BlockSpec((2, cps, 1, SH_UP), pair), pl.BlockSpec((cps, SH_UP, D), lambda i, k: (k, 0, 0)),
                  pl.BlockSpec((1, D), const2), pl.BlockSpec((tm, D), tile)],
        out_specs=[pl.BlockSpec((2, cps, tm, SH_UP), lambda i, k: (0, k, i, 0)),
                   pl.BlockSpec((2, cps, tm, SH_UP), lambda i, k: (0, k, i, 0)),
                   pl.BlockSpec((cps, tm, SH_UP), lambda i, k: (k, i, 0)),
                   pl.BlockSpec((tm, D), tile), pl.BlockSpec((tm, D), tile), pl.BlockSpec((8, D), const2)],
        out_shape=[jax.ShapeDtypeStruct((2, NCH, t, SH_UP), BF16), jax.ShapeDtypeStruct((2, NCH, t, SH_UP), BF16),
                   jax.ShapeDtypeStruct((NCH, t, SH_UP), BF16),
                   jax.ShapeDtypeStruct((t, D), BF16), jax.ShapeDtypeStruct((t, D), F32),
                   jax.ShapeDtypeStruct((8, D), F32)],
        scratch_shapes=[pltpu.VMEM((2, NCH, 8, SH_UP), F32), pltpu.VMEM((tm, D), F32)],
        compiler_params=_cparams(2, VMEM_BIG),
    )(x1, g2, w_up_g.reshape(2, NCH, D, SH_UP), fcw.reshape(2, NCH, 3, SH_UP), fcb.reshape(2, NCH, 1, SH_UP),
      w_dn, gf, tgt)


def bwd_ffn(dx2, x1, g2, up, pre, w_up_g, fcw, w_dn, tm, seq, cps):
    t = x1.shape[0]
    nt = t // tm
    tps = seq // tm

    def body(dx2_ref, x1_ref, g2_ref, up_ref, pre_ref, wup_ref, fcw_ref, wdn_ref,
             dup_ref, dx1_ref, gvec_ref, gn_ref, carry_ref, dh2_ref, acc_ref):
        i, k = pl.program_id(0), pl.program_id(1)
        it = (nt - 1 - i) % tps
        keep_next = jnp.where(it == tps - 1, 0.0, 1.0)

        @pl.when((i == 0) & (k == 0))
        def _():
            acc_ref[...] = jnp.zeros_like(acc_ref)
            gn_ref[...] = jnp.zeros_like(gn_ref)
            carry_ref[...] = jnp.zeros_like(carry_ref)

        @pl.when(k == 0)
        def _():
            dh2_ref[...] = jnp.zeros_like(dh2_ref)

        dxb = dx2_ref[...].astype(BF16)
        lanes = slice(0, SH_UP)
        dh2 = dh2_ref[...]
        for c in range(cps):
            kc = k * cps + c
            pre = [pre_ref[s, c].astype(F32) for s in range(2)]
            sg = jax.nn.sigmoid(pre[0])
            dact = _dot_nt(dxb, wdn_ref[c])
            dpre = [dact * pre[1] * (sg * (1.0 + pre[0] * (1.0 - sg))), dact * (pre[0] * sg)]
            for s in range(2):
                dc = dpre[s]
                ext = jnp.concatenate([dc, carry_ref[s, kc] * keep_next], axis=0)
                carry_ref[s, kc] = dc[:8]
                shifted = (_shift_up(ext, 2, tm), _shift_up(ext, 1, tm), dc)
                uf = up_ref[s, c].astype(F32)
                acc_ref[s, kc, 0:1, lanes] += jnp.sum(dc, axis=0, keepdims=True)
                for tap in range(3):
                    acc_ref[s, kc, tap + 1:tap + 2, lanes] += jnp.sum(shifted[tap] * uf, axis=0, keepdims=True)
                w = fcw_ref.at[s, c]
                du = w[2:3, :] * dc + w[1:2, :] * shifted[1] + w[0:1, :] * shifted[0]
                dub = du.astype(BF16)
                dup_ref[s, c] = dub
                dh2 = dh2 + _dot_nt(dub, wup_ref[s, c])
        dh2_ref[...] = dh2

        @pl.when(k == NCH // cps - 1)
        def _():
            x1v = x1_ref[...]
            inv2 = _rms_inv(x1v)
            xn = x1v * inv2
            gn_ref[0:1, :] += jnp.sum(dh2 * xn, axis=0, keepdims=True)
            dx1_ref[...] = dx2_ref[...] + _rms_bwd(dh2, xn, inv2, g2_ref[...])

        @pl.when((i == nt - 1) & (k == NCH // cps - 1))
        def _():
            gvec_ref[...] = acc_ref[...]

    rev = lambda i, k: (nt - 1 - i, 0)
    const2 = lambda i, k: (0, 0)
    pair = lambda i, k: (0, k, 0, 0)
    return pl.pallas_call(
        body, name="bwd_ffn", grid=(nt, NCH // cps),
        in_specs=[pl.BlockSpec((tm, D), rev), pl.BlockSpec((tm, D), rev), pl.BlockSpec((1, D), const2),
                  pl.BlockSpec((2, cps, tm, SH_UP), lambda i, k: (0, k, nt - 1 - i, 0)),
                  pl.BlockSpec((2, cps, tm, SH_UP), lambda i, k: (0, k, nt - 1 - i, 0)),
                  pl.BlockSpec((2, cps, D, SH_UP), pair), pl.BlockSpec((2, cps, 3, SH_UP), pair),
                  pl.BlockSpec((cps, SH_UP, D), lambda i, k: (k, 0, 0))],
        out_specs=[pl.BlockSpec((2, cps, tm, SH_UP), lambda i, k: (0, k, nt - 1 - i, 0)), pl.BlockSpec((tm, D), rev),
                   pl.BlockSpec((2, NCH, 8, D), lambda i, k: (0, 0, 0, 0)), pl.BlockSpec((8, D), const2)],
        out_shape=[jax.ShapeDtypeStruct((2, NCH, t, SH_UP), BF16), jax.ShapeDtypeStruct((t, D), F32),
                   jax.ShapeDtypeStruct((2, NCH, 8, D), F32), jax.ShapeDtypeStruct((8, D), F32)],
        scratch_shapes=[pltpu.VMEM((2, NCH, 8, SH_UP), F32), pltpu.VMEM((tm, D), F32),
                        pltpu.VMEM((2, NCH, 8, D), F32)],
        compiler_params=_cparams(2, VMEM_BIG),
    )(dx2, x1, g2, up, pre, w_up_g.reshape(2, NCH, D, SH_UP), fcw.reshape(2, NCH, 3, SH_UP), w_dn)


def bwd_mix(dx1, z, y_pool, y_conv, pool_w, pool_scale, conv_w, wmix, tm, seq):
    t = dx1.shape[0]
    nt = t // tm
    tps = seq // tm
    hb = tm // HALO

    def body(da_ref, z_ref, zph_ref, zcvh_ref, yp_ref, yc_ref, pw_ref, ps_ref, wpp_ref, cw_ref, wco_ref, wo_ref,
             dz_ref, mg_ref, p2_ref, u_ref, dyp_ref, dyc_ref, p_ref, dpw_ref, gvec_ref, cp_ref, cc_ref):
        i = pl.program_id(0)
        it = (nt - 1 - i) % tps
        keep_hist = jnp.where(it == 0, 0.0, 1.0)
        keep_next = jnp.where(it == tps - 1, 0.0, 1.0)
        pos = it * tm + lax.broadcasted_iota(jnp.int32, (tm, 1), 0)

        @pl.when(i == 0)
        def _():
            gvec_ref[...] = jnp.zeros_like(gvec_ref)
            cp_ref[...] = jnp.zeros_like(cp_ref)
            cc_ref[...] = jnp.zeros_like(cc_ref)

        dm = _dot_nt(da_ref[...].astype(BF16), wo_ref[...])
        merged, dyp, dyc = [], [], []
        for q in range(NG):
            lanes = slice(q * CG, (q + 1) * CG)
            sp = jax.nn.sigmoid(z_ref[16 + q].astype(F32))
            sc = jax.nn.sigmoid(z_ref[20 + q].astype(F32))
            yp = yp_ref[:, lanes].astype(F32)
            yc = yc_ref[:, lanes].astype(F32)
            dmq = dm[:, lanes]
            merged.append((sp * yp + sc * yc).astype(BF16))
            dyp.append((dmq * sp).astype(BF16))
            dyc.append((dmq * sc).astype(BF16))
            dz_ref[16 + q] = (dmq * yp * (sp * (1.0 - sp))).astype(BF16)
            dz_ref[20 + q] = (dmq * yc * (sc * (1.0 - sc))).astype(BF16)
        mg_ref[...] = jnp.concatenate(merged, axis=1)
        dypb = jnp.concatenate(dyp, axis=1)
        dycb = jnp.concatenate(dyc, axis=1)
        dyp_ref[...] = dypb
        dyc_ref[...] = dycb

        dp2 = _dot_nt(dypb, wpp_ref[...])
        p2 = []
        for g, win in enumerate(WINS):
            lanes = slice(g * CG, (g + 1) * CG)
            cnt = jnp.minimum(pos + 1, win).astype(F32)
            p = _pool_tile(z_ref, zph_ref, g, win, keep_hist, cnt)
            pb = p.astype(BF16)
            p_ref[g] = pb
            pw = _dot(pb, pw_ref[g])
            p2.append((pw * ps_ref[:, lanes]).astype(BF16))
            dp2g = dp2[:, lanes]
            gvec_ref[0:1, lanes] += jnp.sum(dp2g * pw, axis=0, keepdims=True)
            dpwb = (dp2g * ps_ref[:, lanes]).astype(BF16)
            dpw_ref[g] = dpwb
            dp = _dot_nt(dpwb, pw_ref[g])
            qv = dp / cnt
            ext = jnp.concatenate([qv, cp_ref[g] * keep_next], axis=0)
            cp_ref[g] = qv[:HALO]
            n = tm + HALO
            s, sh = ext, 1
            while sh < win:
                s = s + pltpu.roll(s, n - sh, 0)
                sh *= 2
            dz_ref[g] = (s[:tm] - dp).astype(BF16)
        p2_ref[...] = jnp.concatenate(p2, axis=1)

        du = _dot_nt(dycb, wco_ref[...])
        u = []
        for q in range(NG):
            lanes = slice(q * CG, (q + 1) * CG)
            zb = z_ref[4 + q].astype(F32)
            zc = z_ref[8 + q].astype(F32)
            zv = z_ref[12 + q].astype(F32)
            cv = zc * zv
            cvh = zcvh_ref[q].astype(F32) * zcvh_ref[4 + q].astype(F32) * keep_hist
            cc, cv1, cv2 = _conv_taps(jnp.concatenate([cvh, cv], axis=0), cv, cw_ref, lanes, HALO)
            u.append((zb * cc).astype(BF16))
            duq = du[:, lanes]
            dz_ref[4 + q] = (duq * cc).astype(BF16)
            dcc = duq * zb
            for tap, src in enumerate((cv2, cv1, cv)):
                gvec_ref[tap + 1:tap + 2, lanes] += jnp.sum(dcc * src, axis=0, keepdims=True)
            ext = jnp.concatenate([dcc, cc_ref[:, lanes] * keep_next], axis=0)
            cc_ref[:, lanes] = dcc[:8]
            dcv = (cw_ref[2:3, lanes] * dcc + cw_ref[1:2, lanes] * _shift_up(ext, 1, tm)
                   + cw_ref[0:1, lanes] * _shift_up(ext, 2, tm))
            dz_ref[8 + q] = (dcv * zv).astype(BF16)
            dz_ref[12 + q] = (dcv * zc).astype(BF16)
        u_ref[...] = jnp.concatenate(u, axis=1)

    def hist(i):
        return jnp.maximum((nt - 1 - i) * hb - 1, 0)

    rev = lambda i: (nt - 1 - i, 0)
    rev3 = lambda i: (0, nt - 1 - i, 0)
    const2 = lambda i: (0, 0)
    tok = jax.ShapeDtypeStruct((t, D), BF16)
    grp = jax.ShapeDtypeStruct((NG, t, CG), BF16)
    return pl.pallas_call(
        body, name="bwd_mix", grid=(nt,),
        in_specs=[pl.BlockSpec((tm, D), rev), pl.BlockSpec((NZT, tm, CG), rev3),
                  pl.BlockSpec((NG, HALO, CG), lambda i: (0, hist(i), 0)),
                  pl.BlockSpec((2 * NG, HALO, CG), lambda i: (1, hist(i), 0)),
                  pl.BlockSpec((tm, D), rev), pl.BlockSpec((tm, D), rev),
                  pl.BlockSpec((NG, CG, CG), lambda i: (0, 0, 0)), pl.BlockSpec((1, D), const2),
                  pl.BlockSpec((D, D), lambda i: (0, MIX_POOL_PROJ)), pl.BlockSpec((3, D), const2),
                  pl.BlockSpec((D, D), lambda i: (0, MIX_CONV_OUT)), pl.BlockSpec((D, D), lambda i: (0, MIX_O))],
        out_specs=[pl.BlockSpec((NZT, tm, CG), rev3)] + [pl.BlockSpec((tm, D), rev)] * 5
                  + [pl.BlockSpec((NG, tm, CG), rev3)] * 2 + [pl.BlockSpec((8, D), const2)],
        out_shape=[jax.ShapeDtypeStruct((NZT, t, CG), BF16), tok, tok, tok, tok, tok, grp, grp,
                   jax.ShapeDtypeStruct((8, D), F32)],
        scratch_shapes=[pltpu.VMEM((NG, HALO, CG), F32), pltpu.VMEM((8, D), F32)],
        compiler_params=_cparams(1, VMEM_BIG),
    )(dx1, z, z, z, y_pool, y_conv, pool_w, pool_scale, wmix, conv_w, wmix, wmix)


def bwd_in(dz, w_in_g, dx1, x, g1, tm):
    t = x.shape[0]

    def body(dz_ref, w_ref, dx1_ref, x_ref, g_ref, gx_ref, gn_ref):
        @pl.when(pl.program_id(0) == 0)
        def _():
            gn_ref[...] = jnp.zeros_like(gn_ref)

        dh = None
        for j in range(NDEV):
            dzc = jnp.concatenate([dz_ref[3 * j + q] for q in range(3)], axis=1)
            part = _dot_nt(dzc, w_ref[j])
            dh = part if dh is None else dh + part
        xv = x_ref[...]
        inv = _rms_inv(xv)
        xn = xv * inv
        gn_ref[0:1, :] += jnp.sum(dh * xn, axis=0, keepdims=True)
        gx_ref[...] = dx1_ref[...] + _rms_bwd(dh, xn, inv, g_ref[...])

    tile = lambda i: (i, 0)
    return pl.pallas_call(
        body, name="bwd_in", grid=(t // tm,),
        in_specs=[pl.BlockSpec((NZT, tm, CG), lambda i: (0, i, 0)),
                  pl.BlockSpec((NDEV, D, SH_IN), lambda i: (0, 0, 0)),
                  pl.BlockSpec((tm, D), tile), pl.BlockSpec((tm, D), tile), pl.BlockSpec((1, D), lambda i: (0, 0))],
        out_specs=[pl.BlockSpec((tm, D), tile), pl.BlockSpec((8, D), lambda i: (0, 0))],
        out_shape=[jax.ShapeDtypeStruct((t, D), F32), jax.ShapeDtypeStruct((8, D), F32)],
        compiler_params=_cparams(1, VMEM_BIG),
    )(dz, w_in_g, dx1, x, g1)


def _slot(j):
    return j % 2, j // 2


def wgrad_cols(a, b, q, name, tk):
    t, m = a.shape
    width = b.shape[3]

    def body(a_ref, b_ref, o_ref):
        @pl.when(pl.program_id(1) == 0)
        def _():
            o_ref[...] = jnp.zeros_like(o_ref)

        o_ref[...] += _dot_tn(a_ref[...], b_ref[...])

    return pl.pallas_call(
        body, name=name, grid=(NDEV, t // tk),
        in_specs=[pl.BlockSpec((tk, m), lambda j, k: (k, 0)),
                  pl.BlockSpec((None, None, tk, width), lambda j, k: (j, q, k, 0))],
        out_specs=pl.BlockSpec((None, None, m, width), lambda j, k: (j % 2, j // 2, 0, 0)),
        out_shape=jax.ShapeDtypeStruct((2, 4, m, width), F32),
        compiler_params=_cparams(2, VMEM_BIG),
    )(a, b)


def wgrad_down(act, dx2, tk):
    t = dx2.shape[0]

    def body(a_ref, b_ref, o_ref, acc_ref):
        kt = pl.program_id(1)

        @pl.when(kt == 0)
        def _():
            acc_ref[...] = jnp.zeros_like(acc_ref)

        acc_ref[...] += _dot_tn(a_ref[...], b_ref[...].astype(BF16))

        @pl.when(kt == pl.num_programs(1) - 1)
        def _():
            o_ref[0] = acc_ref[:SH_DN]
            o_ref[1] = acc_ref[SH_DN:]

    return pl.pallas_call(
        body, name="wgrad_down", grid=(NCH, t // tk),
        in_specs=[pl.BlockSpec((None, tk, SH_UP), lambda k, kt: (k, kt, 0)), pl.BlockSpec((tk, D), lambda k, kt: (kt, 0))],
        out_specs=pl.BlockSpec((2, None, SH_DN, D), lambda k, kt: (0, k, 0, 0)),
        out_shape=jax.ShapeDtypeStruct((2, 4, SH_DN, D), F32),
        scratch_shapes=[pltpu.VMEM((SH_UP, D), F32)],
        compiler_params=_cparams(2, VMEM_BIG),
    )(act, dx2)


def wgrad_square(a, b, name, tk):
    t = a.shape[0]

    def body(a_ref, b_ref, o_ref, acc_ref):
        kt = pl.program_id(0)

        @pl.when(kt == 0)
        def _():
            acc_ref[...] = jnp.zeros_like(acc_ref)

        acc_ref[...] += _dot_tn(a_ref[...], b_ref[...].astype(BF16))

        @pl.when(kt == pl.num_programs(0) - 1)
        def _():
            for j in range(NDEV):
                cc, xy = _slot(j)
                o_ref[cc, xy] = acc_ref[j * 128:(j + 1) * 128]

    return pl.pallas_call(
        body, name=name, grid=(t // tk,),
        in_specs=[pl.BlockSpec((tk, D), lambda k: (k, 0)), pl.BlockSpec((tk, D), lambda k: (k, 0))],
        out_specs=pl.BlockSpec((2, 4, 128, D), lambda k: (0, 0, 0, 0)),
        out_shape=jax.ShapeDtypeStruct((2, 4, 128, D), F32),
        scratch_shapes=[pltpu.VMEM((D, D), F32)],
        compiler_params=_cparams(1, VMEM_BIG),
    )(a, b)


def wgrad_pool(p, dpw, tk):
    t = p.shape[1]

    def body(a_ref, b_ref, o_ref):
        @pl.when(pl.program_id(0) == 0)
        def _():
            o_ref[...] = jnp.zeros_like(o_ref)

        for g in range(NG):
            o_ref[g] += _dot_tn(a_ref[g], b_ref[g])

    return pl.pallas_call(
        body, name="wgrad_pool", grid=(t // tk,),
        in_specs=[pl.BlockSpec((NG, tk, CG), lambda k: (0, k, 0))] * 2,
        out_specs=pl.BlockSpec((NG, CG, CG), lambda k: (0, 0, 0)),
        out_shape=jax.ShapeDtypeStruct((NG, CG, CG), F32),
        compiler_params=_cparams(1, VMEM_BIG),
    )(p, dpw)


def _adamw(w, g, m, v):
    m = ADAM_B1 * m + (1.0 - ADAM_B1) * g
    v = ADAM_B2 * v + (1.0 - ADAM_B2) * (g * g)
    m_hat = m / (1.0 - ADAM_B1 ** ADAM_STEP)
    v_hat = v / (1.0 - ADAM_B2 ** ADAM_STEP)
    delta = -ADAM_LR * (m_hat / (jnp.sqrt(v_hat) + ADAM_EPS) + ADAM_WD * w)
    return delta, m, v


def _row_block(r):
    return 512 if r % 512 == 0 else r


def chip_partial(place, g, from_sibling, name):
    _, _, r, c = g.shape

    def body(place_ref, g_ref, s_ref, o_ref):
        o_ref[...] = (g_ref[...] + s_ref[...]).astype(BF16)

    return pl.pallas_call(
        body, name=name,
        grid_spec=pltpu.PrefetchScalarGridSpec(
            num_scalar_prefetch=1, grid=(3,),
            in_specs=[pl.BlockSpec((None, None, r, c), lambda k, pr: (pr[0], pr[1] ^ (k + 1), 0, 0)),
                      pl.BlockSpec((None, r, c), lambda k, pr: (pr[1] ^ (k + 1), 0, 0))],
            out_specs=pl.BlockSpec((None, r, c), lambda k, pr: (pr[1] ^ (k + 1), 0, 0))),
        out_shape=jax.ShapeDtypeStruct((4, r, c), BF16),
        compiler_params=_cparams(1, VMEM_BIG),
    )(place, g, from_sibling)


def finish_adamw(place, gs, from_sibling, from_chips, w, m, v, name):
    n = len(gs)
    r = gs[0].shape[2]
    widths = [g.shape[3] for g in gs]
    c = sum(widths)
    br = _row_block(r)

    def body(place_ref, *refs):
        g_refs, s_refs, c_refs = refs[:n], refs[n:2 * n], refs[2 * n:5 * n]
        w_ref, m_ref, v_ref, og_ref, od_ref, om_ref, ov_ref = refs[5 * n:]
        cols = []
        for q in range(n):
            grad = g_refs[q][...] + s_refs[q][...]
            for k in range(3):
                grad = grad + c_refs[3 * q + k][...].astype(F32)
            cols.append(grad)
        grad = cols[0] if n == 1 else jnp.concatenate(cols, axis=1)
        og_ref[...] = grad
        od_ref[...], om_ref[...], ov_ref[...] = _adamw(w_ref[...], grad, m_ref[...], v_ref[...])

    def other(k, cq):
        return pl.BlockSpec((None, br, cq), lambda i, pr: (pr[1] ^ k, i, 0))

    row = pl.BlockSpec((br, c), lambda i, pr: (i, 0))
    out = jax.ShapeDtypeStruct((r, c), F32)
    in_specs = [pl.BlockSpec((None, None, br, cq), lambda i, pr: (pr[0], pr[1], i, 0)) for cq in widths]
    in_specs += [pl.BlockSpec((None, br, cq), lambda i, pr: (pr[1], i, 0)) for cq in widths]
    in_specs += [other(k, cq) for cq in widths for k in (1, 2, 3)]
    return pl.pallas_call(
        body, name=name,
        grid_spec=pltpu.PrefetchScalarGridSpec(
            num_scalar_prefetch=1, grid=(r // br,), in_specs=in_specs + [row, row, row], out_specs=[row] * 4),
        out_shape=[out] * 4,
        compiler_params=_cparams(1, VMEM_BIG),
    )(place, *gs, *from_sibling, *[fc for fc in from_chips for _ in range(3)], w, m, v)


def adamw_small(items):
    n = len(items)

    def body(*refs):
        ins, outs = refs[:4 * n], refs[4 * n:]
        for i in range(n):
            w, g, m, v = (r[...] for r in ins[4 * i:4 * i + 4])
            outs[3 * i][...], outs[3 * i + 1][...], outs[3 * i + 2][...] = _adamw(w, g, m, v)

    out = [jax.ShapeDtypeStruct(it[0].shape, F32) for it in items for _ in range(3)]
    res = pl.pallas_call(body, name="adamw_small", out_shape=out)(*[a for it in items for a in it])
    return [res[3 * i:3 * i + 3] for i in range(n)]


def kernel(x, norm_mix, w_in, pool_w, pool_scale, w_pool_proj, conv_w, w_conv_out, w_o, norm_ffn, w_up, ffn_conv_w, ffn_conv_b, w_down, norm_final, loss_target, m_norm_mix, m_w_in, m_pool_w, m_pool_scale, m_w_pool_proj, m_conv_w, m_w_conv_out, m_w_o, m_norm_ffn, m_w_up, m_ffn_conv_w, m_ffn_conv_b, m_w_down, m_norm_final, v_norm_mix, v_w_in, v_pool_w, v_pool_scale, v_w_pool_proj, v_conv_w, v_w_conv_out, v_w_o, v_norm_ffn, v_w_up, v_ffn_conv_w, v_ffn_conv_b, v_w_down, v_norm_final):
    nb, seq, _ = x.shape
    t = nb * seq
    tm_in = min(TM_IN, t)
    tm_mix = min(TM_MIX, seq)
    tm_ffn = min(TM_FFN, seq)
    tk = min(TK_WGRAD, t)
    xt = x.reshape(t, D)
    tgt = loss_target.reshape(t, D)
    xi, yi, ci = _pos()
    me = 4 * xi + 2 * yi + ci
    place = jnp.stack([ci, 2 * xi + yi]).astype(jnp.int32)

    tie = lax.optimization_barrier
    w_in_g, = all_gather_blocks([w_in[0].astype(BF16)], "all_gather_w_in", 0)
    taps = (jnp.pad(conv_w[0], ((0, 5), (0, D - 128))) + jnp.pad(ffn_conv_w[0], ((3, 2), (0, D - SH_UP))))
    taps_g = _exchange_small(taps, False, "all_gather_taps")
    mix_shard = jnp.concatenate(
        [w_pool_proj[0], w_conv_out[0], w_o[0], pool_w[0].reshape(NG * 32, CG)], axis=1).astype(BF16)
    mix_shard, taps_g = tie((mix_shard, taps_g))
    wmix_g, = all_gather_blocks([mix_shard], "all_gather_w_mix", 0)
    ffn_shards, w_in_g = tie(([w_up[0].astype(BF16), w_down[0].astype(BF16)], w_in_g))
    w_up_g, w_dn_g = all_gather_blocks(ffn_shards, "all_gather_w_ffn", 0)
    wmix = wmix_g.reshape(D, MIX_COLS)
    pool_w_f = wmix_g[:, :, 3 * D:].reshape(NDEV, NG, 32, CG).transpose(1, 0, 2, 3).reshape(NG, CG, CG)
    w_dn_f = w_dn_g.reshape(NCH, SH_UP, D)
    conv_w_f = taps_g[:, 0:3, :128].transpose(1, 0, 2).reshape(3, D)
    fcw_f = taps_g[:, 3:6, :SH_UP]
    fcb_f = ffn_conv_b.reshape(NDEV, 1, SH_UP)
    gfin = norm_final.reshape(1, D)

    z, h1 = fwd_in(xt, norm_mix, w_in_g, tm_in)
    x1, y_pool, y_conv = fwd_mix(z, xt, pool_w_f, pool_scale, conv_w_f, wmix, tm_mix, seq)
    up, pre, act, h2, dx2, ffn_vec = fwd_ffn(x1, norm_ffn, w_up_g, fcw_f, fcb_f, w_dn_f, gfin, tgt, tm_ffn, seq,
                                             FFN_CHUNKS_PER_STEP)

    def to_sibling(full, tag):
        return reduce_scatter_d2d(full, "reduce_scatter_d2d_" + tag, 1)

    def partials(full, from_sib, names):
        return [chip_partial(place, g, s, "chip_partial_" + nm) for g, s, nm in zip(full, from_sib, names)]

    def to_chips(parts, tag):
        return reduce_scatter_ici(parts, "reduce_scatter_ici_" + tag, 2)

    def finish(nm, gs, from_sib, from_chips, wmv):
        w, m, v = wmv
        rc = (gs[0].shape[2], sum(g.shape[3] for g in gs))
        outs = finish_adamw(place, gs, from_sib, from_chips, w.reshape(rc), m.reshape(rc), v.reshape(rc), "adamw_" + nm)
        return [o.reshape(w.shape) for o in outs]

    def after(x, dep):
        return tie((x, dep))[0]

    big = {}
    gw_dn = wgrad_down(act, dx2, tk)
    sib_dn = to_sibling([gw_dn], "w_down")
    d_up, dx1, g_ffn_vec, g_nffn = bwd_ffn(dx2, x1, norm_ffn, up, pre, w_up_g, fcw_f, w_dn_f, tm_ffn, seq,
                                           FFN_CHUNKS_PER_STEP)
    d_up, part_dn = tie((d_up, partials([gw_dn], sib_dn, ["w_down"])))
    chips_dn = to_chips(part_dn, "w_down")
    gw_up = wgrad_cols(h2, d_up.reshape(NDEV, 1, t, SH_UP), 0, "wgrad_up", tk)
    sib_up = to_sibling([after(gw_up, chips_dn)], "w_up")
    dz, merged, p2, u, dyp, dyc, p, dpw, g_mix_vec = bwd_mix(
        dx1, z, y_pool, y_conv, pool_w_f, pool_scale, conv_w_f, wmix, tm_mix, seq)
    merged, part_up = tie((merged, partials([gw_up], sib_up, ["w_up"])))
    chips_up = to_chips(part_up, "w_up")
    gw_o = wgrad_square(merged, dx1, "wgrad_o", tk)
    gw_pp = wgrad_square(p2, dyp, "wgrad_pool_proj", tk)
    gw_co = wgrad_square(u, dyc, "wgrad_conv_out", tk)
    gw_pool = wgrad_pool(p, dpw, tk).reshape(NG, 4, 2, 32, CG).transpose(2, 1, 0, 3, 4).reshape(2, 4, NG * 32, CG)
    dz8 = dz.reshape(NDEV, 3, t, CG)
    gw_in, sib_in, chips_in = [None] * 3, [None] * 3, [None] * 3
    gw_in[0] = wgrad_cols(h1, dz8, 0, "wgrad_in_0", tk)
    sib_a = to_sibling(after([gw_o, gw_pp], (chips_up, gw_in[0])), "mix_a")
    sib_b = to_sibling(after([gw_co, gw_pool], sib_a), "mix_b")
    sib_in[0] = to_sibling(after([gw_in[0]], sib_b), "w_in_0")
    gw_in[1] = wgrad_cols(h1, dz8, 1, "wgrad_in_1", tk)
    sib_in[1] = to_sibling(after([gw_in[1]], sib_in[0]), "w_in_1")
    h1, part_a, part_b, part_in0 = tie((h1, partials([gw_o, gw_pp], sib_a, ["w_o", "w_pool_proj"]),
                                        partials([gw_co, gw_pool], sib_b, ["w_conv_out", "pool_w"]),
                                        partials([gw_in[0]], sib_in[0], ["w_in_0"])))
    chips_a = to_chips(after(part_a, sib_in[1]), "mix_a")
    chips_b = to_chips(after(part_b, chips_a), "mix_b")
    chips_in[0] = to_chips(after(part_in0, chips_b), "w_in_0")
    gw_in[2] = wgrad_cols(h1, dz8, 2, "wgrad_in_2", tk)
    sib_in[2] = to_sibling(after([gw_in[2]], chips_in[0]), "w_in_2")
    dx1, part_in1, part_in2 = tie((dx1, partials([gw_in[1]], sib_in[1], ["w_in_1"]),
                                   partials([gw_in[2]], sib_in[2], ["w_in_2"])))
    chips_in[1] = to_chips(after(part_in1, sib_in[2]), "w_in_1")
    chips_in[2] = to_chips(part_in2, "w_in_2")
    grad_x, g_nmix = bwd_in(dz, w_in_g, dx1, xt, norm_mix, min(TM_BWD_IN, t))
    grad_x, chips_dn, chips_up, chips_a, chips_b, chips_in = tie(
        (grad_x, chips_dn, chips_up, chips_a, chips_b, chips_in))

    big["w_down"] = finish("w_down", [gw_dn], sib_dn, chips_dn, (w_down, m_w_down, v_w_down))
    big["w_up"] = finish("w_up", [gw_up], sib_up, chips_up, (w_up, m_w_up, v_w_up))
    big["w_o"] = finish("w_o", [gw_o], sib_a[:1], chips_a[:1], (w_o, m_w_o, v_w_o))
    big["w_pool_proj"] = finish("w_pool_proj", [gw_pp], sib_a[1:], chips_a[1:], (w_pool_proj, m_w_pool_proj, v_w_pool_proj))
    big["w_conv_out"] = finish("w_conv_out", [gw_co], sib_b[:1], chips_b[:1], (w_conv_out, m_w_conv_out, v_w_conv_out))
    big["pool_w"] = finish("pool_w", [gw_pool], sib_b[1:], chips_b[1:], (pool_w, m_pool_w, v_pool_w))
    big["w_in"] = finish("w_in", gw_in, [s[0] for s in sib_in], [c[0] for c in chips_in], (w_in, m_w_in, v_w_in))

    red = _exchange_small(
        jnp.concatenate([g_nmix, g_mix_vec, g_nffn, ffn_vec, g_ffn_vec.reshape(8 * NDEV, D)], axis=0), True,
        "all_reduce_small")
    g_norm_mix, g_pool_scale, g_norm_ffn = red[0:1], red[8:9], red[16:17]
    g_conv_w = lax.dynamic_slice(red, (9, me * 128), (3, 128))
    g_norm_final = red[24]
    loss = red[25, 0]
    g_fcb = red[32:].reshape(NDEV, 8, D)[:, 0, :SH_UP].reshape(1, FF2)
    g_fcw = lax.dynamic_slice(red, (33 + 8 * me, 0), (3, SH_UP))
    grads = {"norm_mix": g_norm_mix, "pool_scale": g_pool_scale, "norm_ffn": g_norm_ffn, "norm_final": g_norm_final,
             "ffn_conv_b": g_fcb, "conv_w": g_conv_w.reshape(1, 3, 128), "ffn_conv_w": g_fcw.reshape(1, 3, SH_UP)}
    small_wmv = {"norm_mix": (norm_mix, m_norm_mix, v_norm_mix), "pool_scale": (pool_scale, m_pool_scale, v_pool_scale),
                 "norm_ffn": (norm_ffn, m_norm_ffn, v_norm_ffn), "norm_final": (norm_final, m_norm_final, v_norm_final),
                 "ffn_conv_b": (ffn_conv_b, m_ffn_conv_b, v_ffn_conv_b), "conv_w": (conv_w, m_conv_w, v_conv_w),
                 "ffn_conv_w": (ffn_conv_w, m_ffn_conv_w, v_ffn_conv_w)}
    small_names = list(small_wmv)
    flat2 = lambda a: a.reshape(-1, a.shape[-1])
    small_out = adamw_small([(flat2(small_wmv[nm][0]), flat2(grads[nm]), flat2(small_wmv[nm][1]),
                              flat2(small_wmv[nm][2])) for nm in small_names])
    small = {nm: [o.reshape(small_wmv[nm][0].shape) for o in outs] for nm, outs in zip(small_names, small_out)}

    order = ["norm_mix", "w_in", "pool_w", "pool_scale", "w_pool_proj", "conv_w", "w_conv_out", "w_o", "norm_ffn",
             "w_up", "ffn_conv_w", "ffn_conv_b", "w_down", "norm_final"]
    out = [loss, grad_x.reshape(nb, seq, D)]
    out += [big[nm][0] if nm in big else grads[nm] for nm in order]
    for idx in range(3):
        out += [big[nm][idx + 1] if nm in big else small[nm][idx] for nm in order]
    return tuple(out)
```

```python
import functools

import jax
import jax.numpy as jnp
from jax import lax
from jax.experimental import pallas as pl
from jax.experimental.pallas import tpu as pltpu
from jax.experimental.pallas import tpu_sc as plsc

F32 = jnp.float32
BF16 = jnp.bfloat16

NDEV = 8
D = 1024
NG = 4
CG = 256
WINS = (2, 4, 8, 16)
DIN = 6 * D
SH_IN = DIN // NDEV
NZT = DIN // CG
FF2 = 5632
SH_UP = FF2 // NDEV
FF = FF2 // 2
NCH = 4
SH_DN = FF // NDEV
RMS_EPS = 1e-6
HALO = 16

ADAM_LR = 0.001
ADAM_B1 = 0.9
ADAM_B2 = 0.999
ADAM_EPS = 1e-08
ADAM_WD = 0.01
ADAM_STEP = 10

TM_IN = 512
TM_BWD_IN = 256
TM_MIX = 256
TM_FFN = 256
FFN_CHUNKS_PER_STEP = 4
TK_WGRAD = 2048
MIX_POOL_PROJ, MIX_CONV_OUT, MIX_O = 0, 1, 2
MIX_COLS = 3 * D + CG
VMEM_BIG = 56 * 1024 * 1024
MESH = pl.DeviceIdType.MESH
ANY = pl.BlockSpec(memory_space=pl.ANY)


def _cparams(n_axes, vmem=None):
    return pltpu.CompilerParams(dimension_semantics=("arbitrary",) * n_axes, vmem_limit_bytes=vmem)


def _dot(a, b):
    return jnp.dot(a, b, preferred_element_type=F32)


def _dot_nt(a, b):
    return lax.dot_general(a, b, (((1,), (1,)), ((), ())), preferred_element_type=F32)


def _dot_tn(a, b):
    return lax.dot_general(a, b, (((0,), (0,)), ((), ())), preferred_element_type=F32)


def _shift_down(ext, s, lead):
    return pltpu.roll(ext, s, 0)[lead:]


def _shift_up(ext, s, tm):
    n = ext.shape[0]
    return pltpu.roll(ext, n - s, 0)[:tm]


def _rms_inv(x):
    return lax.rsqrt(jnp.mean(x * x, axis=-1, keepdims=True) + RMS_EPS)


def _rms_bwd(dh, xn, inv, g):
    dxn = dh * g
    return inv * (dxn - xn * jnp.mean(dxn * xn, axis=-1, keepdims=True))


def _pos():
    return lax.axis_index("x"), lax.axis_index("y"), lax.axis_index("c")


def _handshake(peers):
    barrier = pltpu.get_barrier_semaphore()
    for peer in peers:
        pl.semaphore_signal(barrier, inc=1, device_id=peer, device_id_type=MESH)
    pl.semaphore_wait(barrier, len(peers))


def _sequencer(body, out_type, n_sems, name, collective_id):
    return pl.kernel(
        body, out_type=out_type, mesh=plsc.ScalarSubcoreMesh(axis_name="sequencer", num_cores=1), name=name,
        scratch_types=[pltpu.SemaphoreType.DMA((n_sems,)), pltpu.SemaphoreType.DMA((n_sems,))],
        compiler_params=pltpu.CompilerParams(collective_id=collective_id))


def all_gather_blocks(shards, name, collective_id):
    n = len(shards)

    def body(*refs):
        ins, outs = refs[:n], refs[n:2 * n]
        send_sems, recv_sems = refs[2 * n:]
        x, y, c = _pos()
        sibling = (x, y, 1 - c)
        chips = [(1 - x, y), (x, 1 - y), (1 - x, 1 - y)]
        _handshake([sibling] + [(*chip, c) for chip in chips])

        def copy(w, k, block, to, src=None):
            slot = outs[w].at[4 * block[0] + 2 * block[1] + block[2]]
            return pltpu.make_async_remote_copy(
                src_ref=slot if src is None else src, dst_ref=slot,
                send_sem=send_sems.at[8 * w + k], recv_sem=recv_sems.at[8 * w + k], device_id=to, device_id_type=MESH)

        mine, first, passed = [], [], []
        for w in range(n):
            m = pltpu.make_async_copy(ins[w], outs[w].at[4 * x + 2 * y + c], send_sems.at[8 * w + 7])
            m.start()
            mine.append(m)
            first.append(copy(w, 0, (x, y, c), sibling, src=ins[w]))
            first += [copy(w, 1 + j, (x, y, c), (*chip, c), src=ins[w]) for j, chip in enumerate(chips)]
        for cp in first:
            cp.start()
        for w in range(n):
            for j, chip in enumerate(chips):
                copy(w, 1 + j, (*chip, c), (x, y, c)).wait_recv()
                fw = copy(w, 4 + j, (*chip, c), sibling)
                fw.start()
                passed.append(fw)
        for w in range(n):
            copy(w, 0, (x, y, 1 - c), (x, y, c)).wait_recv()
            for j, chip in enumerate(chips):
                copy(w, 4 + j, (*chip, 1 - c), (x, y, c)).wait_recv()
        for cp in first + passed:
            cp.wait_send()
        for m in mine:
            m.wait()

    out = [jax.ShapeDtypeStruct((NDEV,) + s.shape, s.dtype) for s in shards]
    return _sequencer(body, out, 8 * n, name, collective_id)(*shards)


def _exchange_small(v, reduce, name):
    rows = v.shape[0]

    def body(v_ref, out_ref, slots, send_sems, recv_sems, local_sem):
        x, y, c = _pos()
        me = 4 * x + 2 * y + c
        mine = pltpu.make_async_copy(v_ref, slots.at[me], local_sem)
        mine.start()
        offs = [(dx, dy, dc) for dx in (0, 1) for dy in (0, 1) for dc in (0, 1)][1:]

        def copy(k, src_slot, to):
            return pltpu.make_async_remote_copy(
                src_ref=v_ref, dst_ref=slots.at[src_slot], send_sem=send_sems.at[k], recv_sem=recv_sems.at[k],
                device_id=to, device_id_type=MESH)

        sends = []
        for k, (dx, dy, dc) in enumerate(offs):
            cp = copy(k, me, (x ^ dx, y ^ dy, c ^ dc))
            cp.start()
            sends.append(cp)
        for k, (dx, dy, dc) in enumerate(offs):
            copy(k, 4 * (x ^ dx) + 2 * (y ^ dy) + (c ^ dc), (x, y, c)).wait_recv()
        for cp in sends:
            cp.wait_send()
        mine.wait()
        if reduce:
            acc = slots[0]
            for d in range(1, NDEV):
                acc = acc + slots[d]
            out_ref[...] = acc
        else:
            out_ref[...] = slots[...]

    out = jax.ShapeDtypeStruct((rows, D) if reduce else (NDEV, rows, D), F32)
    return pl.pallas_call(
        body, name=name, out_shape=out,
        in_specs=[pl.BlockSpec(memory_space=pltpu.VMEM)], out_specs=pl.BlockSpec(memory_space=pltpu.VMEM),
        scratch_shapes=[pltpu.VMEM((NDEV, rows, D), F32), pltpu.SemaphoreType.DMA((7,)),
                        pltpu.SemaphoreType.DMA((7,)), pltpu.SemaphoreType.DMA],
    )(v)


def reduce_scatter_d2d(grads, name, collective_id):
    n = len(grads)

    def body(*refs):
        ins, outs = refs[:n], refs[n:2 * n]
        send_sems, recv_sems = refs[2 * n:]
        x, y, c = _pos()
        _handshake([(x, y, 1 - c)])
        cps = []
        for w in range(n):
            cp = pltpu.make_async_remote_copy(
                src_ref=ins[w].at[1 - c], dst_ref=outs[w], send_sem=send_sems.at[w], recv_sem=recv_sems.at[w],
                device_id=(x, y, 1 - c), device_id_type=MESH)
            cp.start()
            cps.append(cp)
        for cp in cps:
            cp.wait_recv()
        for cp in cps:
            cp.wait_send()

    out = [jax.ShapeDtypeStruct(g.shape[1:], F32) for g in grads]
    return _sequencer(body, out, n, name, collective_id)(*grads)


def reduce_scatter_ici(parts, name, collective_id):
    n = len(parts)

    def body(*refs):
        ins, outs = refs[:n], refs[n:2 * n]
        send_sems, recv_sems = refs[2 * n:]
        x, y, c = _pos()
        offs = [(1, 0), (0, 1), (1, 1)]
        _handshake([(x ^ dx, y ^ dy, c) for dx, dy in offs])
        cps = []
        for w in range(n):
            for k, (dx, dy) in enumerate(offs):
                ox, oy = x ^ dx, y ^ dy
                cp = pltpu.make_async_remote_copy(
                    src_ref=ins[w].at[2 * ox + oy], dst_ref=outs[w].at[2 * x + y],
                    send_sem=send_sems.at[3 * w + k], recv_sem=recv_sems.at[3 * w + k],
                    device_id=(ox, oy, c), device_id_type=MESH)
                cp.start()
                cps.append((cp, w, k, ox, oy))
        for cp, w, k, ox, oy in cps:
            pltpu.make_async_remote_copy(
                src_ref=ins[w].at[2 * ox + oy], dst_ref=outs[w].at[2 * ox + oy],
                send_sem=send_sems.at[3 * w + k], recv_sem=recv_sems.at[3 * w + k],
                device_id=(ox, oy, c), device_id_type=MESH).wait_recv()
        for cp, *_ in cps:
            cp.wait_send()

    out = [jax.ShapeDtypeStruct(p.shape, BF16) for p in parts]
    return _sequencer(body, out, 3 * n, name, collective_id)(*parts)


def fwd_in(x, g1, w_in_g, tm):
    t = x.shape[0]

    def body(x_ref, g_ref, w_ref, z_ref, h_ref):
        xf = x_ref[...]
        h = (xf * _rms_inv(xf) * g_ref[...]).astype(BF16)
        h_ref[...] = h
        for j in range(NDEV):
            r = _dot(h, w_ref[j])
            for q in range(3):
                z_ref[3 * j + q] = r[:, q * CG:(q + 1) * CG].astype(BF16)

    return pl.pallas_call(
        body, name="fwd_in", grid=(t // tm,),
        in_specs=[pl.BlockSpec((tm, D), lambda i: (i, 0)), pl.BlockSpec((1, D), lambda i: (0, 0)),
                  pl.BlockSpec((NDEV, D, SH_IN), lambda i: (0, 0, 0))],
        out_specs=[pl.BlockSpec((NZT, tm, CG), lambda i: (0, i, 0)), pl.BlockSpec((tm, D), lambda i: (i, 0))],
        out_shape=[jax.ShapeDtypeStruct((NZT, t, CG), BF16), jax.ShapeDtypeStruct((t, D), BF16)],
        compiler_params=_cparams(1, VMEM_BIG),
    )(x, g1, w_in_g)


def _pool_tile(z_ref, zh_ref, g, win, keep_hist, cnt):
    zt = z_ref[g].astype(F32)
    ext = jnp.concatenate([zh_ref[g].astype(F32) * keep_hist, zt], axis=0)
    s, sh = ext, 1
    while sh < win:
        s = s + pltpu.roll(s, sh, 0)
        sh *= 2
    return s[HALO:] / cnt - zt


def _conv_taps(ext, cur, w_ref, lanes, lead):
    x1 = _shift_down(ext, 1, lead)
    x2 = _shift_down(ext, 2, lead)
    out = w_ref[2:3, lanes] * cur + w_ref[1:2, lanes] * x1 + w_ref[0:1, lanes] * x2
    return out, x1, x2


def fwd_mix(z, x, pool_w, pool_scale, conv_w, wmix, tm, seq):
    t = x.shape[0]
    tps = seq // tm
    hb = tm // HALO

    def body(z_ref, zph_ref, zcvh_ref, x_ref, pw_ref, ps_ref, wpp_ref, cw_ref, wco_ref, wo_ref,
             x1_ref, yp_ref, yc_ref):
        it = pl.program_id(0) % tps
        keep_hist = jnp.where(it == 0, 0.0, 1.0)
        pos = it * tm + lax.broadcasted_iota(jnp.int32, (tm, 1), 0)
        p2 = []
        for g, win in enumerate(WINS):
            cnt = jnp.minimum(pos + 1, win).astype(F32)
            p = _pool_tile(z_ref, zph_ref, g, win, keep_hist, cnt)
            lanes = slice(g * CG, (g + 1) * CG)
            p2.append((_dot(p.astype(BF16), pw_ref[g]) * ps_ref[:, lanes]).astype(BF16))
        y_pool = _dot(jnp.concatenate(p2, axis=1), wpp_ref[...])
        u = []
        for q in range(NG):
            lanes = slice(q * CG, (q + 1) * CG)
            cv = z_ref[8 + q].astype(F32) * z_ref[12 + q].astype(F32)
            cvh = zcvh_ref[q].astype(F32) * zcvh_ref[4 + q].astype(F32) * keep_hist
            cc, _, _ = _conv_taps(jnp.concatenate([cvh, cv], axis=0), cv, cw_ref, lanes, HALO)
            u.append((z_ref[4 + q].astype(F32) * cc).astype(BF16))
        y_conv = _dot(jnp.concatenate(u, axis=1), wco_ref[...])
        ypb, ycb = y_pool.astype(BF16), y_conv.astype(BF16)
        yp_ref[...] = ypb
        yc_ref[...] = ycb
        merged = []
        for q in range(NG):
            lanes = slice(q * CG, (q + 1) * CG)
            sp = jax.nn.sigmoid(z_ref[16 + q].astype(F32))
            sc = jax.nn.sigmoid(z_ref[20 + q].astype(F32))
            merged.append((sp * ypb[:, lanes].astype(F32) + sc * ycb[:, lanes].astype(F32)).astype(BF16))
        x1_ref[...] = x_ref[...] + _dot(jnp.concatenate(merged, axis=1), wo_ref[...])

    def hist(i):
        return jnp.maximum(i * hb - 1, 0)

    const2 = lambda i: (0, 0)
    return pl.pallas_call(
        body, name="fwd_mix", grid=(t // tm,),
        in_specs=[pl.BlockSpec((NZT, tm, CG), lambda i: (0, i, 0)),
                  pl.BlockSpec((NG, HALO, CG), lambda i: (0, hist(i), 0)),
                  pl.BlockSpec((2 * NG, HALO, CG), lambda i: (1, hist(i), 0)),
                  pl.BlockSpec((tm, D), lambda i: (i, 0)),
                  pl.BlockSpec((NG, CG, CG), lambda i: (0, 0, 0)), pl.BlockSpec((1, D), const2),
                  pl.BlockSpec((D, D), lambda i: (0, MIX_POOL_PROJ)), pl.BlockSpec((3, D), const2),
                  pl.BlockSpec((D, D), lambda i: (0, MIX_CONV_OUT)), pl.BlockSpec((D, D), lambda i: (0, MIX_O))],
        out_specs=[pl.BlockSpec((tm, D), lambda i: (i, 0))] * 3,
        out_shape=[jax.ShapeDtypeStruct((t, D), F32), jax.ShapeDtypeStruct((t, D), BF16),
                   jax.ShapeDtypeStruct((t, D), BF16)],
        compiler_params=_cparams(1, VMEM_BIG),
    )(z, z, z, x, pool_w, pool_scale, wmix, conv_w, wmix, wmix)


def fwd_ffn(x1, g2, w_up_g, fcw, fcb, w_dn, gf, tgt, tm, seq, cps):
    t = x1.shape[0]
    tps = seq // tm

    def body(x1_ref, g2_ref, wup_ref, fcw_ref, fcb_ref, wdn_ref, gf_ref, tgt_ref,
             up_ref, pre_ref, act_ref, h2_ref, dx2_ref, vec_ref, hist_ref, d_ref):
        i, k = pl.program_id(0), pl.program_id(1)
        keep_hist = jnp.where(i % tps == 0, 0.0, 1.0)

        @pl.when((i == 0) & (k == 0))
        def _():
            vec_ref[...] = jnp.zeros_like(vec_ref)
            hist_ref[...] = jnp.zeros_like(hist_ref)

        @pl.when(k == 0)
        def _():
            x1v = x1_ref[...]
            h2_ref[...] = (x1v * _rms_inv(x1v) * g2_ref[...]).astype(BF16)
            d_ref[...] = jnp.zeros_like(d_ref)

        h2 = h2_ref[...]
        lanes = slice(0, SH_UP)
        d = d_ref[...]
        for c in range(cps):
            kc = k * cps + c
            conv = []
            for s in range(2):
                ub = _dot(h2, wup_ref[s, c]).astype(BF16)
                up_ref[s, c] = ub
                uf = ub.astype(F32)
                ext = jnp.concatenate([hist_ref[s, kc] * keep_hist, uf], axis=0)
                hist_ref[s, kc] = uf[tm - 8:]
                cc, _, _ = _conv_taps(ext, uf, fcw_ref.at[s, c], lanes, 8)
                conv.append(cc + fcb_ref[s, c])
                pre_ref[s, c] = conv[s].astype(BF16)
            a = (conv[0] * jax.nn.sigmoid(conv[0]) * conv[1]).astype(BF16)
            act_ref[c] = a
            d = d + _dot(a, wdn_ref[c])
        d_ref[...] = d

        @pl.when(k == NCH // cps - 1)
        def _():
            x2 = x1_ref[...] + d_ref[...]
            inv3 = _rms_inv(x2)
            xn = x2 * inv3
            diff = xn * gf_ref[...] - tgt_ref[...]
            dy = diff * (1.0 / D)
            vec_ref[0:1, :] += jnp.sum(dy * xn, axis=0, keepdims=True)
            vec_ref[1:2, :] += 0.5 * jnp.sum(jnp.mean(diff * diff, axis=-1))
            dx2_ref[...] = _rms_bwd(dy, xn, inv3, gf_ref[...])

    tile = lambda i, k: (i, 0)
    const2 = lambda i, k: (0, 0)
    pair = lambda i, k: (0, k, 0, 0)
    return pl.pallas_call(
        body, name="fwd_ffn", grid=(t // tm, NCH // cps),
        in_specs=[pl.BlockSpec((tm, D), tile), pl.BlockSpec((1, D), const2),
                  pl.BlockSpec((2, cps, D, SH_UP), pair), pl.BlockSpec((2, cps, 3, SH_UP), pair),
                  pl.BlockSpec((2, cps, 1, SH_UP), pair), pl.BlockSpec((cps, SH_UP, D), lambda i, k: (k, 0, 0)),
                  pl.BlockSpec((1, D), const2), pl.BlockSpec((tm, D), tile)],
        out_specs=[pl.BlockSpec((2, cps, tm, SH_UP), lambda i, k: (0, k, i, 0)),
                   pl.BlockSpec((2, cps, tm, SH_UP), lambda i, k: (0, k, i, 0)),
                   pl.BlockSpec((cps, tm, SH_UP), lambda i, k: (k, i, 0)),
                   pl.BlockSpec((tm, D), tile), pl.BlockSpec((tm, D), tile), pl.BlockSpec((8, D), const2)],
        out_shape=[jax.ShapeDtypeStruct((2, NCH, t, SH_UP), BF16), jax.ShapeDtypeStruct((2, NCH, t, SH_UP), BF16),
                   jax.ShapeDtypeStruct((NCH, t, SH_UP), BF16),
                   jax.ShapeDtypeStruct((t, D), BF16), jax.ShapeDtypeStruct((t, D), F32),
                   jax.ShapeDtypeStruct((8, D), F32)],
        scratch_shapes=[pltpu.VMEM((2, NCH, 8, SH_UP), F32), pltpu.VMEM((tm, D), F32)],
        compiler_params=_cparams(2, VMEM_BIG),
    )(x1, g2, w_up_g.reshape(2, NCH, D, SH_UP), fcw.reshape(2, NCH, 3, SH_UP), fcb.reshape(2, NCH, 1, SH_UP),
      w_dn, gf, tgt)


def bwd_ffn(dx2, x1, g2, up, pre, w_up_g, fcw, w_dn, tm, seq, cps):
    t = x1.shape[0]
    nt = t // tm
    tps = seq // tm

    def body(dx2_ref, x1_ref, g2_ref, up_ref, pre_ref, wup_ref, fcw_ref, wdn_ref,
             dup_ref, dx1_ref, gvec_ref, gn_ref, carry_ref, dh2_ref, acc_ref):
        i, k = pl.program_id(0), pl.program_id(1)
        it = (nt - 1 - i) % tps
        keep_next = jnp.where(it == tps - 1, 0.0, 1.0)

        @pl.when((i == 0) & (k == 0))
        def _():
            acc_ref[...] = jnp.zeros_like(acc_ref)
            gn_ref[...] = jnp.zeros_like(gn_ref)
            carry_ref[...] = jnp.zeros_like(carry_ref)

        @pl.when(k == 0)
        def _():
            dh2_ref[...] = jnp.zeros_like(dh2_ref)

        dxb = dx2_ref[...].astype(BF16)
        lanes = slice(0, SH_UP)
        dh2 = dh2_ref[...]
        for c in range(cps):
            kc = k * cps + c
            pre = [pre_ref[s, c].astype(F32) for s in range(2)]
            sg = jax.nn.sigmoid(pre[0])
            dact = _dot_nt(dxb, wdn_ref[c])
            dpre = [dact * pre[1] * (sg * (1.0 + pre[0] * (1.0 - sg))), dact * (pre[0] * sg)]
            for s in range(2):
                dc = dpre[s]
                ext = jnp.concatenate([dc, carry_ref[s, kc] * keep_next], axis=0)
                carry_ref[s, kc] = dc[:8]
                shifted = (_shift_up(ext, 2, tm), _shift_up(ext, 1, tm), dc)
                uf = up_ref[s, c].astype(F32)
                acc_ref[s, kc, 0:1, lanes] += jnp.sum(dc, axis=0, keepdims=True)
                for tap in range(3):
                    acc_ref[s, kc, tap + 1:tap + 2, lanes] += jnp.sum(shifted[tap] * uf, axis=0, keepdims=True)
                w = fcw_ref.at[s, c]
                du = w[2:3, :] * dc + w[1:2, :] * shifted[1] + w[0:1, :] * shifted[0]
                dub = du.astype(BF16)
                dup_ref[s, c] = dub
                dh2 = dh2 + _dot_nt(dub, wup_ref[s, c])
        dh2_ref[...] = dh2

        @pl.when(k == NCH // cps - 1)
        def _():
            x1v = x1_ref[...]
            inv2 = _rms_inv(x1v)
            xn = x1v * inv2
            gn_ref[0:1, :] += jnp.sum(dh2 * xn, axis=0, keepdims=True)
            dx1_ref[...] = dx2_ref[...] + _rms_bwd(dh2, xn, inv2, g2_ref[...])

        @pl.when((i == nt - 1) & (k == NCH // cps - 1))
        def _():
            gvec_ref[...] = acc_ref[...]

    rev = lambda i, k: (nt - 1 - i, 0)
    const2 = lambda i, k: (0, 0)
    pair = lambda i, k: (0, k, 0, 0)
    return pl.pallas_call(
        body, name="bwd_ffn", grid=(nt, NCH // cps),
        in_specs=[pl.BlockSpec((tm, D), rev), pl.BlockSpec((tm, D), rev), pl.BlockSpec((1, D), const2),
                  pl.BlockSpec((2, cps, tm, SH_UP), lambda i, k: (0, k, nt - 1 - i, 0)),
                  pl.BlockSpec((2, cps, tm, SH_UP), lambda i, k: (0, k, nt - 1 - i, 0)),
                  pl.BlockSpec((2, cps, D, SH_UP), pair), pl.BlockSpec((2, cps, 3, SH_UP), pair),
                  pl.BlockSpec((cps, SH_UP, D), lambda i, k: (k, 0, 0))],
        out_specs=[pl.BlockSpec((2, cps, tm, SH_UP), lambda i, k: (0, k, nt - 1 - i, 0)), pl.BlockSpec((tm, D), rev),
                   pl.BlockSpec((2, NCH, 8, D), lambda i, k: (0, 0, 0, 0)), pl.BlockSpec((8, D), const2)],
        out_shape=[jax.ShapeDtypeStruct((2, NCH, t, SH_UP), BF16), jax.ShapeDtypeStruct((t, D), F32),
                   jax.ShapeDtypeStruct((2, NCH, 8, D), F32), jax.ShapeDtypeStruct((8, D), F32)],
        scratch_shapes=[pltpu.VMEM((2, NCH, 8, SH_UP), F32), pltpu.VMEM((tm, D), F32),
                        pltpu.VMEM((2, NCH, 8, D), F32)],
        compiler_params=_cparams(2, VMEM_BIG),
    )(dx2, x1, g2, up, pre, w_up_g.reshape(2, NCH, D, SH_UP), fcw.reshape(2, NCH, 3, SH_UP), w_dn)


def bwd_mix(dx1, z, y_pool, y_conv, pool_w, pool_scale, conv_w, wmix, tm, seq):
    t = dx1.shape[0]
    nt = t // tm
    tps = seq // tm
    hb = tm // HALO

    def body(da_ref, z_ref, zph_ref, zcvh_ref, yp_ref, yc_ref, pw_ref, ps_ref, wpp_ref, cw_ref, wco_ref, wo_ref,
             dz_ref, mg_ref, p2_ref, u_ref, dyp_ref, dyc_ref, p_ref, dpw_ref, gvec_ref, cp_ref, cc_ref):
        i = pl.program_id(0)
        it = (nt - 1 - i) % tps
        keep_hist = jnp.where(it == 0, 0.0, 1.0)
        keep_next = jnp.where(it == tps - 1, 0.0, 1.0)
        pos = it * tm + lax.broadcasted_iota(jnp.int32, (tm, 1), 0)

        @pl.when(i == 0)
        def _():
            gvec_ref[...] = jnp.zeros_like(gvec_ref)
            cp_ref[...] = jnp.zeros_like(cp_ref)
            cc_ref[...] = jnp.zeros_like(cc_ref)

        dm = _dot_nt(da_ref[...].astype(BF16), wo_ref[...])
        merged, dyp, dyc = [], [], []
        for q in range(NG):
            lanes = slice(q * CG, (q + 1) * CG)
            sp = jax.nn.sigmoid(z_ref[16 + q].astype(F32))
            sc = jax.nn.sigmoid(z_ref[20 + q].astype(F32))
            yp = yp_ref[:, lanes].astype(F32)
            yc = yc_ref[:, lanes].astype(F32)
            dmq = dm[:, lanes]
            merged.append((sp * yp + sc * yc).astype(BF16))
            dyp.append((dmq * sp).astype(BF16))
            dyc.append((dmq * sc).astype(BF16))
            dz_ref[16 + q] = (dmq * yp * (sp * (1.0 - sp))).astype(BF16)
            dz_ref[20 + q] = (dmq * yc * (sc * (1.0 - sc))).astype(BF16)
        mg_ref[...] = jnp.concatenate(merged, axis=1)
        dypb = jnp.concatenate(dyp, axis=1)
        dycb = jnp.concatenate(dyc, axis=1)
        dyp_ref[...] = dypb
        dyc_ref[...] = dycb

        dp2 = _dot_nt(dypb, wpp_ref[...])
        p2 = []
        for g, win in enumerate(WINS):
            lanes = slice(g * CG, (g + 1) * CG)
            cnt = jnp.minimum(pos + 1, win).astype(F32)
            p = _pool_tile(z_ref, zph_ref, g, win, keep_hist, cnt)
            pb = p.astype(BF16)
            p_ref[g] = pb
            pw = _dot(pb, pw_ref[g])
            p2.append((pw * ps_ref[:, lanes]).astype(BF16))
            dp2g = dp2[:, lanes]
            gvec_ref[0:1, lanes] += jnp.sum(dp2g * pw, axis=0, keepdims=True)
            dpwb = (dp2g * ps_ref[:, lanes]).astype(BF16)
            dpw_ref[g] = dpwb
            dp = _dot_nt(dpwb, pw_ref[g])
            qv = dp / cnt
            ext = jnp.concatenate([qv, cp_ref[g] * keep_next], axis=0)
            cp_ref[g] = qv[:HALO]
            n = tm + HALO
            s, sh = ext, 1
            while sh < win:
                s = s + pltpu.roll(s, n - sh, 0)
                sh *= 2
            dz_ref[g] = (s[:tm] - dp).astype(BF16)
        p2_ref[...] = jnp.concatenate(p2, axis=1)

        du = _dot_nt(dycb, wco_ref[...])
        u = []
        for q in range(NG):
            lanes = slice(q * CG, (q + 1) * CG)
            zb = z_ref[4 + q].astype(F32)
            zc = z_ref[8 + q].astype(F32)
            zv = z_ref[12 + q].astype(F32)
            cv = zc * zv
            cvh = zcvh_ref[q].astype(F32) * zcvh_ref[4 + q].astype(F32) * keep_hist
            cc, cv1, cv2 = _conv_taps(jnp.concatenate([cvh, cv], axis=0), cv, cw_ref, lanes, HALO)
            u.append((zb * cc).astype(BF16))
            duq = du[:, lanes]
            dz_ref[4 + q] = (duq * cc).astype(BF16)
            dcc = duq * zb
            for tap, src in enumerate((cv2, cv1, cv)):
                gvec_ref[tap + 1:tap + 2, lanes] += jnp.sum(dcc * src, axis=0, keepdims=True)
            ext = jnp.concatenate([dcc, cc_ref[:, lanes] * keep_next], axis=0)
            cc_ref[:, lanes] = dcc[:8]
            dcv = (cw_ref[2:3, lanes] * dcc + cw_ref[1:2, lanes] * _shift_up(ext, 1, tm)
                   + cw_ref[0:1, lanes] * _shift_up(ext, 2, tm))
            dz_ref[8 + q] = (dcv * zv).astype(BF16)
            dz_ref[12 + q] = (dcv * zc).astype(BF16)
        u_ref[...] = jnp.concatenate(u, axis=1)

    def hist(i):
        return jnp.maximum((nt - 1 - i) * hb - 1, 0)

    rev = lambda i: (nt - 1 - i, 0)
    rev3 = lambda i: (0, nt - 1 - i, 0)
    const2 = lambda i: (0, 0)
    tok = jax.ShapeDtypeStruct((t, D), BF16)
    grp = jax.ShapeDtypeStruct((NG, t, CG), BF16)
    return pl.pallas_call(
        body, name="bwd_mix", grid=(nt,),
        in_specs=[pl.BlockSpec((tm, D), rev), pl.BlockSpec((NZT, tm, CG), rev3),
                  pl.BlockSpec((NG, HALO, CG), lambda i: (0, hist(i), 0)),
                  pl.BlockSpec((2 * NG, HALO, CG), lambda i: (1, hist(i), 0)),
                  pl.BlockSpec((tm, D), rev), pl.BlockSpec((tm, D), rev),
                  pl.BlockSpec((NG, CG, CG), lambda i: (0, 0, 0)), pl.BlockSpec((1, D), const2),
                  pl.BlockSpec((D, D), lambda i: (0, MIX_POOL_PROJ)), pl.BlockSpec((3, D), const2),
                  pl.BlockSpec((D, D), lambda i: (0, MIX_CONV_OUT)), pl.BlockSpec((D, D), lambda i: (0, MIX_O))],
        out_specs=[pl.BlockSpec((NZT, tm, CG), rev3)] + [pl.BlockSpec((tm, D), rev)] * 5
                  + [pl.BlockSpec((NG, tm, CG), rev3)] * 2 + [pl.BlockSpec((8, D), const2)],
        out_shape=[jax.ShapeDtypeStruct((NZT, t, CG), BF16), tok, tok, tok, tok, tok, grp, grp,
                   jax.ShapeDtypeStruct((8, D), F32)],
        scratch_shapes=[pltpu.VMEM((NG, HALO, CG), F32), pltpu.VMEM((8, D), F32)],
        compiler_params=_cparams(1, VMEM_BIG),
    )(dx1, z, z, z, y_pool, y_conv, pool_w, pool_scale, wmix, conv_w, wmix, wmix)


def bwd_in(dz, w_in_g, dx1, x, g1, tm):
    t = x.shape[0]

    def body(dz_ref, w_ref, dx1_ref, x_ref, g_ref, gx_ref, gn_ref):
        @pl.when(pl.program_id(0) == 0)
        def _():
            gn_ref[...] = jnp.zeros_like(gn_ref)

        dh = None
        for j in range(NDEV):
            dzc = jnp.concatenate([dz_ref[3 * j + q] for q in range(3)], axis=1)
            part = _dot_nt(dzc, w_ref[j])
            dh = part if dh is None else dh + part
        xv = x_ref[...]
        inv = _rms_inv(xv)
        xn = xv * inv
        gn_ref[0:1, :] += jnp.sum(dh * xn, axis=0, keepdims=True)
        gx_ref[...] = dx1_ref[...] + _rms_bwd(dh, xn, inv, g_ref[...])

    tile = lambda i: (i, 0)
    return pl.pallas_call(
        body, name="bwd_in", grid=(t // tm,),
        in_specs=[pl.BlockSpec((NZT, tm, CG), lambda i: (0, i, 0)),
                  pl.BlockSpec((NDEV, D, SH_IN), lambda i: (0, 0, 0)),
                  pl.BlockSpec((tm, D), tile), pl.BlockSpec((tm, D), tile), pl.BlockSpec((1, D), lambda i: (0, 0))],
        out_specs=[pl.BlockSpec((tm, D), tile), pl.BlockSpec((8, D), lambda i: (0, 0))],
        out_shape=[jax.ShapeDtypeStruct((t, D), F32), jax.ShapeDtypeStruct((8, D), F32)],
        compiler_params=_cparams(1, VMEM_BIG),
    )(dz, w_in_g, dx1, x, g1)


def _slot(j):
    return j % 2, j // 2


def wgrad_cols(a, b, q, name, tk):
    t, m = a.shape
    width = b.shape[3]

    def body(a_ref, b_ref, o_ref):
        @pl.when(pl.program_id(1) == 0)
        def _():
            o_ref[...] = jnp.zeros_like(o_ref)

        o_ref[...] += _dot_tn(a_ref[...], b_ref[...])

    return pl.pallas_call(
        body, name=name, grid=(NDEV, t // tk),
        in_specs=[pl.BlockSpec((tk, m), lambda j, k: (k, 0)),
                  pl.BlockSpec((None, None, tk, width), lambda j, k: (j, q, k, 0))],
        out_specs=pl.BlockSpec((None, None, m, width), lambda j, k: (j % 2, j // 2, 0, 0)),
        out_shape=jax.ShapeDtypeStruct((2, 4, m, width), F32),
        compiler_params=_cparams(2, VMEM_BIG),
    )(a, b)


def wgrad_down(act, dx2, tk):
    t = dx2.shape[0]

    def body(a_ref, b_ref, o_ref, acc_ref):
        kt = pl.program_id(1)

        @pl.when(kt == 0)
        def _():
            acc_ref[...] = jnp.zeros_like(acc_ref)

        acc_ref[...] += _dot_tn(a_ref[...], b_ref[...].astype(BF16))

        @pl.when(kt == pl.num_programs(1) - 1)
        def _():
            o_ref[0] = acc_ref[:SH_DN]
            o_ref[1] = acc_ref[SH_DN:]

    return pl.pallas_call(
        body, name="wgrad_down", grid=(NCH, t // tk),
        in_specs=[pl.BlockSpec((None, tk, SH_UP), lambda k, kt: (k, kt, 0)), pl.BlockSpec((tk, D), lambda k, kt: (kt, 0))],
        out_specs=pl.BlockSpec((2, None, SH_DN, D), lambda k, kt: (0, k, 0, 0)),
        out_shape=jax.ShapeDtypeStruct((2, 4, SH_DN, D), F32),
        scratch_shapes=[pltpu.VMEM((SH_UP, D), F32)],
        compiler_params=_cparams(2, VMEM_BIG),
    )(act, dx2)


def wgrad_square(a, b, name, tk):
    t = a.shape[0]

    def body(a_ref, b_ref, o_ref, acc_ref):
        kt = pl.program_id(0)

        @pl.when(kt == 0)
        def _():
            acc_ref[...] = jnp.zeros_like(acc_ref)

        acc_ref[...] += _dot_tn(a_ref[...], b_ref[...].astype(BF16))

        @pl.when(kt == pl.num_programs(0) - 1)
        def _():
            for j in range(NDEV):
                cc, xy = _slot(j)
                o_ref[cc, xy] = acc_ref[j * 128:(j + 1) * 128]

    return pl.pallas_call(
        body, name=name, grid=(t // tk,),
        in_specs=[pl.BlockSpec((tk, D), lambda k: (k, 0)), pl.BlockSpec((tk, D), lambda k: (k, 0))],
        out_specs=pl.BlockSpec((2, 4, 128, D), lambda k: (0, 0, 0, 0)),
        out_shape=jax.ShapeDtypeStruct((2, 4, 128, D), F32),
        scratch_shapes=[pltpu.VMEM((D, D), F32)],
        compiler_params=_cparams(1, VMEM_BIG),
    )(a, b)


def wgrad_pool(p, dpw, tk):
    t = p.shape[1]

    def body(a_ref, b_ref, o_ref):
        @pl.when(pl.program_id(0) == 0)
        def _():
            o_ref[...] = jnp.zeros_like(o_ref)

        for g in range(NG):
            o_ref[g] += _dot_tn(a_ref[g], b_ref[g])

    return pl.pallas_call(
        body, name="wgrad_pool", grid=(t // tk,),
        in_specs=[pl.BlockSpec((NG, tk, CG), lambda k: (0, k, 0))] * 2,
        out_specs=pl.BlockSpec((NG, CG, CG), lambda k: (0, 0, 0)),
        out_shape=jax.ShapeDtypeStruct((NG, CG, CG), F32),
        compiler_params=_cparams(1, VMEM_BIG),
    )(p, dpw)


def _adamw(w, g, m, v):
    m = ADAM_B1 * m + (1.0 - ADAM_B1) * g
    v = ADAM_B2 * v + (1.0 - ADAM_B2) * (g * g)
    m_hat = m / (1.0 - ADAM_B1 ** ADAM_STEP)
    v_hat = v / (1.0 - ADAM_B2 ** ADAM_STEP)
    delta = -ADAM_LR * (m_hat / (jnp.sqrt(v_hat) + ADAM_EPS) + ADAM_WD * w)
    return delta, m, v


def _row_block(r):
    return 512 if r % 512 == 0 else r


def chip_partial(place, g, from_sibling, name):
    _, _, r, c = g.shape

    def body(place_ref, g_ref, s_ref, o_ref):
        o_ref[...] = (g_ref[...] + s_ref[...]).astype(BF16)

    return pl.pallas_call(
        body, name=name,
        grid_spec=pltpu.PrefetchScalarGridSpec(
            num_scalar_prefetch=1, grid=(3,),
            in_specs=[pl.BlockSpec((None, None, r, c), lambda k, pr: (pr[0], pr[1] ^ (k + 1), 0, 0)),
                      pl.BlockSpec((None, r, c), lambda k, pr: (pr[1] ^ (k + 1), 0, 0))],
            out_specs=pl.BlockSpec((None, r, c), lambda k, pr: (pr[1] ^ (k + 1), 0, 0))),
        out_shape=jax.ShapeDtypeStruct((4, r, c), BF16),
        compiler_params=_cparams(1, VMEM_BIG),
    )(place, g, from_sibling)


def finish_adamw(place, gs, from_sibling, from_chips, w, m, v, name):
    n = len(gs)
    r = gs[0].shape[2]
    widths = [g.shape[3] for g in gs]
    c = sum(widths)
    br = _row_block(r)

    def body(place_ref, *refs):
        g_refs, s_refs, c_refs = refs[:n], refs[n:2 * n], refs[2 * n:5 * n]
        w_ref, m_ref, v_ref, og_ref, od_ref, om_ref, ov_ref = refs[5 * n:]
        cols = []
        for q in range(n):
            grad = g_refs[q][...] + s_refs[q][...]
            for k in range(3):
                grad = grad + c_refs[3 * q + k][...].astype(F32)
            cols.append(grad)
        grad = cols[0] if n == 1 else jnp.concatenate(cols, axis=1)
        og_ref[...] = grad
        od_ref[...], om_ref[...], ov_ref[...] = _adamw(w_ref[...], grad, m_ref[...], v_ref[...])

    def other(k, cq):
        return pl.BlockSpec((None, br, cq), lambda i, pr: (pr[1] ^ k, i, 0))

    row = pl.BlockSpec((br, c), lambda i, pr: (i, 0))
    out = jax.ShapeDtypeStruct((r, c), F32)
    in_specs = [pl.BlockSpec((None, None, br, cq), lambda i, pr: (pr[0], pr[1], i, 0)) for cq in widths]
    in_specs += [pl.BlockSpec((None, br, cq), lambda i, pr: (pr[1], i, 0)) for cq in widths]
    in_specs += [other(k, cq) for cq in widths for k in (1, 2, 3)]
    return pl.pallas_call(
        body, name=name,
        grid_spec=pltpu.PrefetchScalarGridSpec(
            num_scalar_prefetch=1, grid=(r // br,), in_specs=in_specs + [row, row, row], out_specs=[row] * 4),
        out_shape=[out] * 4,
        compiler_params=_cparams(1, VMEM_BIG),
    )(place, *gs, *from_sibling, *[fc for fc in from_chips for _ in range(3)], w, m, v)


def adamw_small(items):
    n = len(items)

    def body(*refs):
        ins, outs = refs[:4 * n], refs[4 * n:]
        for i in range(n):
            w, g, m, v = (r[...] for r in ins[4 * i:4 * i + 4])
            outs[3 * i][...], outs[3 * i + 1][...], outs[3 * i + 2][...] = _adamw(w, g, m, v)

    out = [jax.ShapeDtypeStruct(it[0].shape, F32) for it in items for _ in range(3)]
    res = pl.pallas_call(body, name="adamw_small", out_shape=out)(*[a for it in items for a in it])
    return [res[3 * i:3 * i + 3] for i in range(n)]


def kernel(x, norm_mix, w_in, pool_w, pool_scale, w_pool_proj, conv_w, w_conv_out, w_o, norm_ffn, w_up, ffn_conv_w, ffn_conv_b, w_down, norm_final, loss_target, m_norm_mix, m_w_in, m_pool_w, m_pool_scale, m_w_pool_proj, m_conv_w, m_w_conv_out, m_w_o, m_norm_ffn, m_w_up, m_ffn_conv_w, m_ffn_conv_b, m_w_down, m_norm_final, v_norm_mix, v_w_in, v_pool_w, v_pool_scale, v_w_pool_proj, v_conv_w, v_w_conv_out, v_w_o, v_norm_ffn, v_w_up, v_ffn_conv_w, v_ffn_conv_b, v_w_down, v_norm_final):
    nb, seq, _ = x.shape
    t = nb * seq
    tm_in = min(TM_IN, t)
    tm_mix = min(TM_MIX, seq)
    tm_ffn = min(TM_FFN, seq)
    tk = min(TK_WGRAD, t)
    xt = x.reshape(t, D)
    tgt = loss_target.reshape(t, D)
    xi, yi, ci = _pos()
    me = 4 * xi + 2 * yi + ci
    place = jnp.stack([ci, 2 * xi + yi]).astype(jnp.int32)

    tie = lax.optimization_barrier
    w_in_g, = all_gather_blocks([w_in[0].astype(BF16)], "all_gather_w_in", 0)
    taps = (jnp.pad(conv_w[0], ((0, 5), (0, D - 128))) + jnp.pad(ffn_conv_w[0], ((3, 2), (0, D - SH_UP))))
    taps_g = _exchange_small(taps, False, "all_gather_taps")
    mix_shard = jnp.concatenate(
        [w_pool_proj[0], w_conv_out[0], w_o[0], pool_w[0].reshape(NG * 32, CG)], axis=1).astype(BF16)
    mix_shard, taps_g = tie((mix_shard, taps_g))
    wmix_g, = all_gather_blocks([mix_shard], "all_gather_w_mix", 0)
    ffn_shards, w_in_g = tie(([w_up[0].astype(BF16), w_down[0].astype(BF16)], w_in_g))
    w_up_g, w_dn_g = all_gather_blocks(ffn_shards, "all_gather_w_ffn", 0)
    wmix = wmix_g.reshape(D, MIX_COLS)
    pool_w_f = wmix_g[:, :, 3 * D:].reshape(NDEV, NG, 32, CG).transpose(1, 0, 2, 3).reshape(NG, CG, CG)
    w_dn_f = w_dn_g.reshape(NCH, SH_UP, D)
    conv_w_f = taps_g[:, 0:3, :128].transpose(1, 0, 2).reshape(3, D)
    fcw_f = taps_g[:, 3:6, :SH_UP]
    fcb_f = ffn_conv_b.reshape(NDEV, 1, SH_UP)
    gfin = norm_final.reshape(1, D)

    z, h1 = fwd_in(xt, norm_mix, w_in_g, tm_in)
    x1, y_pool, y_conv = fwd_mix(z, xt, pool_w_f, pool_scale, conv_w_f, wmix, tm_mix, seq)
    up, pre, act, h2, dx2, ffn_vec = fwd_ffn(x1, norm_ffn, w_up_g, fcw_f, fcb_f, w_dn_f, gfin, tgt, tm_ffn, seq,
                                             FFN_CHUNKS_PER_STEP)

    def to_sibling(full, tag):
        return reduce_scatter_d2d(full, "reduce_scatter_d2d_" + tag, 1)

    def partials(full, from_sib, names):
        return [chip_partial(place, g, s, "chip_partial_" + nm) for g, s, nm in zip(full, from_sib, names)]

    def to_chips(parts, tag):
        return reduce_scatter_ici(parts, "reduce_scatter_ici_" + tag, 2)

    def finish(nm, gs, from_sib, from_chips, wmv):
        w, m, v = wmv
        rc = (gs[0].shape[2], sum(g.shape[3] for g in gs))
        outs = finish_adamw(place, gs, from_sib, from_chips, w.reshape(rc), m.reshape(rc), v.reshape(rc), "adamw_" + nm)
        return [o.reshape(w.shape) for o in outs]

    def after(x, dep):
        return tie((x, dep))[0]

    big = {}
    gw_dn = wgrad_down(act, dx2, tk)
    sib_dn = to_sibling([gw_dn], "w_down")
    d_up, dx1, g_ffn_vec, g_nffn = bwd_ffn(dx2, x1, norm_ffn, up, pre, w_up_g, fcw_f, w_dn_f, tm_ffn, seq,
                                           FFN_CHUNKS_PER_STEP)
    d_up, part_dn = tie((d_up, partials([gw_dn], sib_dn, ["w_down"])))
    chips_dn = to_chips(part_dn, "w_down")
    gw_up = wgrad_cols(h2, d_up.reshape(NDEV, 1, t, SH_UP), 0, "wgrad_up", tk)
    sib_up = to_sibling([after(gw_up, chips_dn)], "w_up")
    dz, merged, p2, u, dyp, dyc, p, dpw, g_mix_vec = bwd_mix(
        dx1, z, y_pool, y_conv, pool_w_f, pool_scale, conv_w_f, wmix, tm_mix, seq)
    merged, part_up = tie((merged, partials([gw_up], sib_up, ["w_up"])))
    chips_up = to_chips(part_up, "w_up")
    gw_o = wgrad_square(merged, dx1, "wgrad_o", tk)
    gw_pp = wgrad_square(p2, dyp, "wgrad_pool_proj", tk)
    gw_co = wgrad_square(u, dyc, "wgrad_conv_out", tk)
    gw_pool = wgrad_pool(p, dpw, tk).reshape(NG, 4, 2, 32, CG).transpose(2, 1, 0, 3, 4).reshape(2, 4, NG * 32, CG)
    dz8 = dz.reshape(NDEV, 3, t, CG)
    gw_in, sib_in, chips_in = [None] * 3, [None] * 3, [None] * 3
    gw_in[0] = wgrad_cols(h1, dz8, 0, "wgrad_in_0", tk)
    sib_a = to_sibling(after([gw_o, gw_pp], (chips_up, gw_in[0])), "mix_a")
    sib_b = to_sibling(after([gw_co, gw_pool], sib_a), "mix_b")
    sib_in[0] = to_sibling(after([gw_in[0]], sib_b), "w_in_0")
    gw_in[1] = wgrad_cols(h1, dz8, 1, "wgrad_in_1", tk)
    sib_in[1] = to_sibling(after([gw_in[1]], sib_in[0]), "w_in_1")
    h1, part_a, part_b, part_in0 = tie((h1, partials([gw_o, gw_pp], sib_a, ["w_o", "w_pool_proj"]),
                                        partials([gw_co, gw_pool], sib_b, ["w_conv_out", "pool_w"]),
                                        partials([gw_in[0]], sib_in[0], ["w_in_0"])))
    chips_a = to_chips(after(part_a, sib_in[1]), "mix_a")
    chips_b = to_chips(part_b, "mix_b")
    chips_in[0] = to_chips(part_in0, "w_in_0")
    h1, big["w_down"], big["w_up"] = tie((
        h1, finish("w_down", [gw_dn], sib_dn, chips_dn, (w_down, m_w_down, v_w_down)),
        finish("w_up", [gw_up], sib_up, chips_up, (w_up, m_w_up, v_w_up))))
    gw_in[2] = wgrad_cols(h1, dz8, 2, "wgrad_in_2", tk)
    sib_in[2] = to_sibling(after([gw_in[2]], (chips_a, chips_b, chips_in[0])), "w_in_2")
    dx1, part_in1, part_in2, big["w_o"], big["w_pool_proj"], big["w_conv_out"], big["pool_w"] = tie((
        dx1, partials([gw_in[1]], sib_in[1], ["w_in_1"]), partials([gw_in[2]], sib_in[2], ["w_in_2"]),
        finish("w_o", [gw_o], sib_a[:1], chips_a[:1], (w_o, m_w_o, v_w_o)),
        finish("w_pool_proj", [gw_pp], sib_a[1:], chips_a[1:], (w_pool_proj, m_w_pool_proj, v_w_pool_proj)),
        finish("w_conv_out", [gw_co], sib_b[:1], chips_b[:1], (w_conv_out, m_w_conv_out, v_w_conv_out)),
        finish("pool_w", [gw_pool], sib_b[1:], chips_b[1:], (pool_w, m_pool_w, v_pool_w))))
    chips_in[1] = to_chips(after(part_in1, sib_in[2]), "w_in_1")
    chips_in[2] = to_chips(part_in2, "w_in_2")
    grad_x, g_nmix = bwd_in(dz, w_in_g, dx1, xt, norm_mix, min(TM_BWD_IN, t))
    grad_x, chips_in = tie((grad_x, chips_in))
    big["w_in"] = finish("w_in", gw_in, [s[0] for s in sib_in], [c[0] for c in chips_in], (w_in, m_w_in, v_w_in))

    red = _exchange_small(
        jnp.concatenate([g_nmix, g_mix_vec, g_nffn, ffn_vec, g_ffn_vec.reshape(8 * NDEV, D)], axis=0), True,
        "all_reduce_small")
    g_norm_mix, g_pool_scale, g_norm_ffn = red[0:1], red[8:9], red[16:17]
    g_conv_w = lax.dynamic_slice(red, (9, me * 128), (3, 128))
    g_norm_final = red[24]
    loss = red[25, 0]
    g_fcb = red[32:].reshape(NDEV, 8, D)[:, 0, :SH_UP].reshape(1, FF2)
    g_fcw = lax.dynamic_slice(red, (33 + 8 * me, 0), (3, SH_UP))
    grads = {"norm_mix": g_norm_mix, "pool_scale": g_pool_scale, "norm_ffn": g_norm_ffn, "norm_final": g_norm_final,
             "ffn_conv_b": g_fcb, "conv_w": g_conv_w.reshape(1, 3, 128), "ffn_conv_w": g_fcw.reshape(1, 3, SH_UP)}
    small_wmv = {"norm_mix": (norm_mix, m_norm_mix, v_norm_mix), "pool_scale": (pool_scale, m_pool_scale, v_pool_scale),
                 "norm_ffn": (norm_ffn, m_norm_ffn, v_norm_ffn), "norm_final": (norm_final, m_norm_final, v_norm_final),
                 "ffn_conv_b": (ffn_conv_b, m_ffn_conv_b, v_ffn_conv_b), "conv_w": (conv_w, m_conv_w, v_conv_w),
                 "ffn_conv_w": (ffn_conv_w, m_ffn_conv_w, v_ffn_conv_w)}
    small_names = list(small_wmv)
    flat2 = lambda a: a.reshape(-1, a.shape[-1])
    small_out = adamw_small([(flat2(small_wmv[nm][0]), flat2(grads[nm]), flat2(small_wmv[nm][1]),
                              flat2(small_wmv[nm][2])) for nm in small_names])
    small = {nm: [o.reshape(small_wmv[nm][0].shape) for o in outs] for nm, outs in zip(small_names, small_out)}

    order = ["norm_mix", "w_in", "pool_w", "pool_scale", "w_pool_proj", "conv_w", "w_conv_out", "w_o", "norm_ffn",
             "w_up", "ffn_conv_w", "ffn_conv_b", "w_down", "norm_final"]
    out = [loss, grad_x.reshape(nb, seq, D)]
    out += [big[nm][0] if nm in big else grads[nm] for nm in order]
    for idx in range(3):
        out += [big[nm][idx + 1] if nm in big else small[nm][idx] for nm in order]
    return tuple(out)
```

```python
import functools

import jax
import jax.numpy as jnp
from jax import lax
from jax.experimental import pallas as pl
from jax.experimental.pallas import tpu as pltpu
from jax.experimental.pallas import tpu_sc as plsc

F32 = jnp.float32
BF16 = jnp.bfloat16

NDEV = 8
D = 1024
NG = 4
CG = 256
WINS = (2, 4, 8, 16)
DIN = 6 * D
SH_IN = DIN // NDEV
NZT = DIN // CG
FF2 = 5632
SH_UP = FF2 // NDEV
FF = FF2 // 2
NCH = 4
SH_DN = FF // NDEV
RMS_EPS = 1e-6
HALO = 16

ADAM_LR = 0.001
ADAM_B1 = 0.9
ADAM_B2 = 0.999
ADAM_EPS = 1e-08
ADAM_WD = 0.01
ADAM_STEP = 10

TM_IN = 512
TM_BWD_IN = 256
TM_MIX = 256
TM_FFN = 256
FFN_CHUNKS_PER_STEP = 4
TK_WGRAD = 2048
MIX_POOL_PROJ, MIX_CONV_OUT, MIX_O = 0, 1, 2
MIX_COLS = 3 * D + CG
VMEM_BIG = 56 * 1024 * 1024
MESH = pl.DeviceIdType.MESH
ANY = pl.BlockSpec(memory_space=pl.ANY)


def _cparams(n_axes, vmem=None):
    return pltpu.CompilerParams(dimension_semantics=("arbitrary",) * n_axes, vmem_limit_bytes=vmem)


def _dot(a, b):
    return jnp.dot(a, b, preferred_element_type=F32)


def _dot_nt(a, b):
    return lax.dot_general(a, b, (((1,), (1,)), ((), ())), preferred_element_type=F32)


def _dot_tn(a, b):
    return lax.dot_general(a, b, (((0,), (0,)), ((), ())), preferred_element_type=F32)


def _shift_down(ext, s, lead):
    return pltpu.roll(ext, s, 0)[lead:]


def _shift_up(ext, s, tm):
    n = ext.shape[0]
    return pltpu.roll(ext, n - s, 0)[:tm]


def _rms_inv(x):
    return lax.rsqrt(jnp.mean(x * x, axis=-1, keepdims=True) + RMS_EPS)


def _rms_bwd(dh, xn, inv, g):
    dxn = dh * g
    return inv * (dxn - xn * jnp.mean(dxn * xn, axis=-1, keepdims=True))


def _pos():
    return lax.axis_index("x"), lax.axis_index("y"), lax.axis_index("c")


def _handshake(peers):
    barrier = pltpu.get_barrier_semaphore()
    for peer in peers:
        pl.semaphore_signal(barrier, inc=1, device_id=peer, device_id_type=MESH)
    pl.semaphore_wait(barrier, len(peers))


def _sequencer(body, out_type, n_sems, name, collective_id):
    return pl.kernel(
        body, out_type=out_type, mesh=plsc.ScalarSubcoreMesh(axis_name="sequencer", num_cores=1), name=name,
        scratch_types=[pltpu.SemaphoreType.DMA((n_sems,)), pltpu.SemaphoreType.DMA((n_sems,))],
        compiler_params=pltpu.CompilerParams(collective_id=collective_id))


def all_gather_blocks(shards, name, collective_id):
    n = len(shards)

    def body(*refs):
        ins, outs = refs[:n], refs[n:2 * n]
        send_sems, recv_sems = refs[2 * n:]
        x, y, c = _pos()
        sibling = (x, y, 1 - c)
        chips = [(1 - x, y), (x, 1 - y), (1 - x, 1 - y)]
        _handshake([sibling] + [(*chip, c) for chip in chips])

        def copy(w, k, block, to, src=None):
            slot = outs[w].at[4 * block[0] + 2 * block[1] + block[2]]
            return pltpu.make_async_remote_copy(
                src_ref=slot if src is None else src, dst_ref=slot,
                send_sem=send_sems.at[8 * w + k], recv_sem=recv_sems.at[8 * w + k], device_id=to, device_id_type=MESH)

        mine, first, passed = [], [], []
        for w in range(n):
            m = pltpu.make_async_copy(ins[w], outs[w].at[4 * x + 2 * y + c], send_sems.at[8 * w + 7])
            m.start()
            mine.append(m)
            first.append(copy(w, 0, (x, y, c), sibling, src=ins[w]))
            first += [copy(w, 1 + j, (x, y, c), (*chip, c), src=ins[w]) for j, chip in enumerate(chips)]
        for cp in first:
            cp.start()
        for w in range(n):
            for j, chip in enumerate(chips):
                copy(w, 1 + j, (*chip, c), (x, y, c)).wait_recv()
                fw = copy(w, 4 + j, (*chip, c), sibling)
                fw.start()
                passed.append(fw)
        for w in range(n):
            copy(w, 0, (x, y, 1 - c), (x, y, c)).wait_recv()
            for j, chip in enumerate(chips):
                copy(w, 4 + j, (*chip, 1 - c), (x, y, c)).wait_recv()
        for cp in first + passed:
            cp.wait_send()
        for m in mine:
            m.wait()

    out = [jax.ShapeDtypeStruct((NDEV,) + s.shape, s.dtype) for s in shards]
    return _sequencer(body, out, 8 * n, name, collective_id)(*shards)


def _exchange_small(v, reduce, name):
    rows = v.shape[0]

    def body(v_ref, out_ref, slots, send_sems, recv_sems, local_sem):
        x, y, c = _pos()
        me = 4 * x + 2 * y + c
        mine = pltpu.make_async_copy(v_ref, slots.at[me], local_sem)
        mine.start()
        offs = [(dx, dy, dc) for dx in (0, 1) for dy in (0, 1) for dc in (0, 1)][1:]

        def copy(k, src_slot, to):
            return pltpu.make_async_remote_copy(
                src_ref=v_ref, dst_ref=slots.at[src_slot], send_sem=send_sems.at[k], recv_sem=recv_sems.at[k],
                device_id=to, device_id_type=MESH)

        sends = []
        for k, (dx, dy, dc) in enumerate(offs):
            cp = copy(k, me, (x ^ dx, y ^ dy, c ^ dc))
            cp.start()
            sends.append(cp)
        for k, (dx, dy, dc) in enumerate(offs):
            copy(k, 4 * (x ^ dx) + 2 * (y ^ dy) + (c ^ dc), (x, y, c)).wait_recv()
        for cp in sends:
            cp.wait_send()
        mine.wait()
        if reduce:
            acc = slots[0]
            for d in range(1, NDEV):
                acc = acc + slots[d]
            out_ref[...] = acc
        else:
            out_ref[...] = slots[...]

    out = jax.ShapeDtypeStruct((rows, D) if reduce else (NDEV, rows, D), F32)
    return pl.pallas_call(
        body, name=name, out_shape=out,
        in_specs=[pl.BlockSpec(memory_space=pltpu.VMEM)], out_specs=pl.BlockSpec(memory_space=pltpu.VMEM),
        scratch_shapes=[pltpu.VMEM((NDEV, rows, D), F32), pltpu.SemaphoreType.DMA((7,)),
                        pltpu.SemaphoreType.DMA((7,)), pltpu.SemaphoreType.DMA],
    )(v)


def reduce_scatter_d2d(grads, name, collective_id):
    n = len(grads)

    def body(*refs):
        ins, outs = refs[:n], refs[n:2 * n]
        send_sems, recv_sems = refs[2 * n:]
        x, y, c = _pos()
        _handshake([(x, y, 1 - c)])
        cps = []
        for w in range(n):
            cp = pltpu.make_async_remote_copy(
                src_ref=ins[w].at[1 - c], dst_ref=outs[w], send_sem=send_sems.at[w], recv_sem=recv_sems.at[w],
                device_id=(x, y, 1 - c), device_id_type=MESH)
            cp.start()
            cps.append(cp)
        for cp in cps:
            cp.wait_recv()
        for cp in cps:
            cp.wait_send()

    out = [jax.ShapeDtypeStruct(g.shape[1:], F32) for g in grads]
    return _sequencer(body, out, n, name, collective_id)(*grads)


def reduce_scatter_ici(parts, name, collective_id):
    n = len(parts)

    def body(*refs):
        ins, outs = refs[:n], refs[n:2 * n]
        send_sems, recv_sems = refs[2 * n:]
        x, y, c = _pos()
        offs = [(1, 0), (0, 1), (1, 1)]
        _handshake([(x ^ dx, y ^ dy, c) for dx, dy in offs])
        cps = []
        for w in range(n):
            for k, (dx, dy) in enumerate(offs):
                ox, oy = x ^ dx, y ^ dy
                cp = pltpu.make_async_remote_copy(
                    src_ref=ins[w].at[2 * ox + oy], dst_ref=outs[w].at[2 * x + y],
                    send_sem=send_sems.at[3 * w + k], recv_sem=recv_sems.at[3 * w + k],
                    device_id=(ox, oy, c), device_id_type=MESH)
                cp.start()
                cps.append((cp, w, k, ox, oy))
        for cp, w, k, ox, oy in cps:
            pltpu.make_async_remote_copy(
                src_ref=ins[w].at[2 * ox + oy], dst_ref=outs[w].at[2 * ox + oy],
                send_sem=send_sems.at[3 * w + k], recv_sem=recv_sems.at[3 * w + k],
                device_id=(ox, oy, c), device_id_type=MESH).wait_recv()
        for cp, *_ in cps:
            cp.wait_send()

    out = [jax.ShapeDtypeStruct(p.shape, BF16) for p in parts]
    return _sequencer(body, out, 3 * n, name, collective_id)(*parts)


def fwd_in(x, g1, w_in_g, tm):
    t = x.shape[0]

    def body(x_ref, g_ref, w_ref, z_ref, ht_ref):
        xf = x_ref[...]
        h = (xf * _rms_inv(xf) * g_ref[...]).astype(BF16)
        ht_ref[...] = h.T
        for j in range(NDEV):
            r = _dot(h, w_ref[j])
            for q in range(3):
                z_ref[3 * j + q] = r[:, q * CG:(q + 1) * CG].astype(BF16)

    return pl.pallas_call(
        body, name="fwd_in", grid=(t // tm,),
        in_specs=[pl.BlockSpec((tm, D), lambda i: (i, 0)), pl.BlockSpec((1, D), lambda i: (0, 0)),
                  pl.BlockSpec((NDEV, D, SH_IN), lambda i: (0, 0, 0))],
        out_specs=[pl.BlockSpec((NZT, tm, CG), lambda i: (0, i, 0)), pl.BlockSpec((D, tm), lambda i: (0, i))],
        out_shape=[jax.ShapeDtypeStruct((NZT, t, CG), BF16), jax.ShapeDtypeStruct((D, t), BF16)],
        compiler_params=_cparams(1, VMEM_BIG),
    )(x, g1, w_in_g)


def _pool_tile(z_ref, zh_ref, g, win, keep_hist, cnt):
    zt = z_ref[g].astype(F32)
    ext = jnp.concatenate([zh_ref[g].astype(F32) * keep_hist, zt], axis=0)
    s, sh = ext, 1
    while sh < win:
        s = s + pltpu.roll(s, sh, 0)
        sh *= 2
    return s[HALO:] / cnt - zt


def _conv_taps(ext, cur, w_ref, lanes, lead):
    x1 = _shift_down(ext, 1, lead)
    x2 = _shift_down(ext, 2, lead)
    out = w_ref[2:3, lanes] * cur + w_ref[1:2, lanes] * x1 + w_ref[0:1, lanes] * x2
    return out, x1, x2


def fwd_mix(z, x, pool_w, pool_scale, conv_w, wmix, tm, seq):
    t = x.shape[0]
    tps = seq // tm
    hb = tm // HALO

    def body(z_ref, zph_ref, zcvh_ref, x_ref, pw_ref, ps_ref, wpp_ref, cw_ref, wco_ref, wo_ref,
             x1_ref, yp_ref, yc_ref):
        it = pl.program_id(0) % tps
        keep_hist = jnp.where(it == 0, 0.0, 1.0)
        pos = it * tm + lax.broadcasted_iota(jnp.int32, (tm, 1), 0)
        p2 = []
        for g, win in enumerate(WINS):
            cnt = jnp.minimum(pos + 1, win).astype(F32)
            p = _pool_tile(z_ref, zph_ref, g, win, keep_hist, cnt)
            lanes = slice(g * CG, (g + 1) * CG)
            p2.append((_dot(p.astype(BF16), pw_ref[g]) * ps_ref[:, lanes]).astype(BF16))
        y_pool = _dot(jnp.concatenate(p2, axis=1), wpp_ref[...])
        u = []
        for q in range(NG):
            lanes = slice(q * CG, (q + 1) * CG)
            cv = z_ref[8 + q].astype(F32) * z_ref[12 + q].astype(F32)
            cvh = zcvh_ref[q].astype(F32) * zcvh_ref[4 + q].astype(F32) * keep_hist
            cc, _, _ = _conv_taps(jnp.concatenate([cvh, cv], axis=0), cv, cw_ref, lanes, HALO)
            u.append((z_ref[4 + q].astype(F32) * cc).astype(BF16))
        y_conv = _dot(jnp.concatenate(u, axis=1), wco_ref[...])
        ypb, ycb = y_pool.astype(BF16), y_conv.astype(BF16)
        yp_ref[...] = ypb
        yc_ref[...] = ycb
        merged = []
        for q in range(NG):
            lanes = slice(q * CG, (q + 1) * CG)
            sp = jax.nn.sigmoid(z_ref[16 + q].astype(F32))
            sc = jax.nn.sigmoid(z_ref[20 + q].astype(F32))
            merged.append((sp * ypb[:, lanes].astype(F32) + sc * ycb[:, lanes].astype(F32)).astype(BF16))
        x1_ref[...] = x_ref[...] + _dot(jnp.concatenate(merged, axis=1), wo_ref[...])

    def hist(i):
        return jnp.maximum(i * hb - 1, 0)

    const2 = lambda i: (0, 0)
    return pl.pallas_call(
        body, name="fwd_mix", grid=(t // tm,),
        in_specs=[pl.BlockSpec((NZT, tm, CG), lambda i: (0, i, 0)),
                  pl.BlockSpec((NG, HALO, CG), lambda i: (0, hist(i), 0)),
                  pl.BlockSpec((2 * NG, HALO, CG), lambda i: (1, hist(i), 0)),
                  pl.BlockSpec((tm, D), lambda i: (i, 0)),
                  pl.BlockSpec((NG, CG, CG), lambda i: (0, 0, 0)), pl.BlockSpec((1, D), const2),
                  pl.BlockSpec((D, D), lambda i: (0, MIX_POOL_PROJ)), pl.BlockSpec((3, D), const2),
                  pl.BlockSpec((D, D), lambda i: (0, MIX_CONV_OUT)), pl.BlockSpec((D, D), lambda i: (0, MIX_O))],
        out_specs=[pl.BlockSpec((tm, D), lambda i: (i, 0))] * 3,
        out_shape=[jax.ShapeDtypeStruct((t, D), F32), jax.ShapeDtypeStruct((t, D), BF16),
                   jax.ShapeDtypeStruct((t, D), BF16)],
        compiler_params=_cparams(1, VMEM_BIG),
    )(z, z, z, x, pool_w, pool_scale, wmix, conv_w, wmix, wmix)


def fwd_ffn(x1, g2, w_up_g, fcw, fcb, w_dn, gf, tgt, tm, seq, cps):
    t = x1.shape[0]
    tps = seq // tm

    def body(x1_ref, g2_ref, wup_ref, fcw_ref, fcb_ref, wdn_ref, gf_ref, tgt_ref,
             up_ref, pre_ref, actt_ref, h2t_ref, dx2_ref, vec_ref, hist_ref, d_ref, h2_ref):
        i, k = pl.program_id(0), pl.program_id(1)
        keep_hist = jnp.where(i % tps == 0, 0.0, 1.0)

        @pl.when((i == 0) & (k == 0))
        def _():
            vec_ref[...] = jnp.zeros_like(vec_ref)
            hist_ref[...] = jnp.zeros_like(hist_ref)

        @pl.when(k == 0)
        def _():
            x1v = x1_ref[...]
            h2 = (x1v * _rms_inv(x1v) * g2_ref[...]).astype(BF16)
            h2_ref[...] = h2
            h2t_ref[...] = h2.T
            d_ref[...] = jnp.zeros_like(d_ref)

        h2 = h2_ref[...]
        lanes = slice(0, SH_UP)
        d = d_ref[...]
        for c in range(cps):
            kc = k * cps + c
            conv = []
            for s in range(2):
                ub = _dot(h2, wup_ref[s, c]).astype(BF16)
                up_ref[s, c] = ub
                uf = ub.astype(F32)
                ext = jnp.concatenate([hist_ref[s, kc] * keep_hist, uf], axis=0)
                hist_ref[s, kc] = uf[tm - 8:]
                cc, _, _ = _conv_taps(ext, uf, fcw_ref.at[s, c], lanes, 8)
                conv.append(cc + fcb_ref[s, c])
                pre_ref[s, c] = conv[s].astype(BF16)
            a = (conv[0] * jax.nn.sigmoid(conv[0]) * conv[1]).astype(BF16)
            actt_ref[c] = a.T
            d = d + _dot(a, wdn_ref[c])
        d_ref[...] = d

        @pl.when(k == NCH // cps - 1)
        def _():
            x2 = x1_ref[...] + d_ref[...]
            inv3 = _rms_inv(x2)
            xn = x2 * inv3
            diff = xn * gf_ref[...] - tgt_ref[...]
            dy = diff * (1.0 / D)
            vec_ref[0:1, :] += jnp.sum(dy * xn, axis=0, keepdims=True)
            vec_ref[1:2, :] += 0.5 * jnp.sum(jnp.mean(diff * diff, axis=-1))
            dx2_ref[...] = _rms_bwd(dy, xn, inv3, gf_ref[...])

    tile = lambda i, k: (i, 0)
    const2 = lambda i, k: (0, 0)
    pair = lambda i, k: (0, k, 0, 0)
    return pl.pallas_call(
        body, name="fwd_ffn", grid=(t // tm, NCH // cps),
        in_specs=[pl.BlockSpec((tm, D), tile), pl.BlockSpec((1, D), const2),
                  pl.BlockSpec((2, cps, D, SH_UP), pair), pl.BlockSpec((2, cps, 3, SH_UP), pair),
                  pl.BlockSpec((2, cps, 1, SH_UP), pair), pl.BlockSpec((cps, SH_UP, D), lambda i, k: (k, 0, 0)),
                  pl.BlockSpec((1, D), const2), pl.BlockSpec((tm, D), tile)],
        out_specs=[pl.BlockSpec((2, cps, tm, SH_UP), lambda i, k: (0, k, i, 0)),
                   pl.BlockSpec((2, cps, tm, SH_UP), lambda i, k: (0, k, i, 0)),
                   pl.BlockSpec((cps, SH_UP, tm), lambda i, k: (k, 0, i)),
                   pl.BlockSpec((D, tm), lambda i, k: (0, i)), pl.BlockSpec((tm, D), tile), pl.BlockSpec((8, D), const2)],
        out_shape=[jax.ShapeDtypeStruct((2, NCH, t, SH_UP), BF16), jax.ShapeDtypeStruct((2, NCH, t, SH_UP), BF16),
                   jax.ShapeDtypeStruct((NCH, SH_UP, t), BF16),
                   jax.ShapeDtypeStruct((D, t), BF16), jax.ShapeDtypeStruct((t, D), F32),
                   jax.ShapeDtypeStruct((8, D), F32)],
        scratch_shapes=[pltpu.VMEM((2, NCH, 8, SH_UP), F32), pltpu.VMEM((tm, D), F32), pltpu.VMEM((tm, D), BF16)],
        compiler_params=_cparams(2, VMEM_BIG),
    )(x1, g2, w_up_g.reshape(2, NCH, D, SH_UP), fcw.reshape(2, NCH, 3, SH_UP), fcb.reshape(2, NCH, 1, SH_UP),
      w_dn, gf, tgt)


def bwd_ffn(dx2, x1, g2, up, pre, w_up_g, fcw, w_dn, tm, seq, cps):
    t = x1.shape[0]
    nt = t // tm
    tps = seq // tm

    def body(dx2_ref, x1_ref, g2_ref, up_ref, pre_ref, wup_ref, fcw_ref, wdn_ref,
             dup_ref, dx1_ref, gvec_ref, gn_ref, carry_ref, dh2_ref, acc_ref):
        i, k = pl.program_id(0), pl.program_id(1)
        it = (nt - 1 - i) % tps
        keep_next = jnp.where(it == tps - 1, 0.0, 1.0)

        @pl.when((i == 0) & (k == 0))
        def _():
            acc_ref[...] = jnp.zeros_like(acc_ref)
            gn_ref[...] = jnp.zeros_like(gn_ref)
            carry_ref[...] = jnp.zeros_like(carry_ref)

        @pl.when(k == 0)
        def _():
            dh2_ref[...] = jnp.zeros_like(dh2_ref)

        dxb = dx2_ref[...].astype(BF16)
        lanes = slice(0, SH_UP)
        dh2 = dh2_ref[...]
        for c in range(cps):
            kc = k * cps + c
            pre = [pre_ref[s, c].astype(F32) for s in range(2)]
            sg = jax.nn.sigmoid(pre[0])
            dact = _dot_nt(dxb, wdn_ref[c])
            dpre = [dact * pre[1] * (sg * (1.0 + pre[0] * (1.0 - sg))), dact * (pre[0] * sg)]
            for s in range(2):
                dc = dpre[s]
                ext = jnp.concatenate([dc, carry_ref[s, kc] * keep_next], axis=0)
                carry_ref[s, kc] = dc[:8]
                shifted = (_shift_up(ext, 2, tm), _shift_up(ext, 1, tm), dc)
                uf = up_ref[s, c].astype(F32)
                acc_ref[s, kc, 0:1, lanes] += jnp.sum(dc, axis=0, keepdims=True)
                for tap in range(3):
                    acc_ref[s, kc, tap + 1:tap + 2, lanes] += jnp.sum(shifted[tap] * uf, axis=0, keepdims=True)
                w = fcw_ref.at[s, c]
                du = w[2:3, :] * dc + w[1:2, :] * shifted[1] + w[0:1, :] * shifted[0]
                dub = du.astype(BF16)
                dup_ref[s, c] = dub
                dh2 = dh2 + _dot_nt(dub, wup_ref[s, c])
        dh2_ref[...] = dh2

        @pl.when(k == NCH // cps - 1)
        def _():
            x1v = x1_ref[...]
            inv2 = _rms_inv(x1v)
            xn = x1v * inv2
            gn_ref[0:1, :] += jnp.sum(dh2 * xn, axis=0, keepdims=True)
            dx1_ref[...] = dx2_ref[...] + _rms_bwd(dh2, xn, inv2, g2_ref[...])

        @pl.when((i == nt - 1) & (k == NCH // cps - 1))
        def _():
            gvec_ref[...] = acc_ref[...]

    rev = lambda i, k: (nt - 1 - i, 0)
    const2 = lambda i, k: (0, 0)
    pair = lambda i, k: (0, k, 0, 0)
    return pl.pallas_call(
        body, name="bwd_ffn", grid=(nt, NCH // cps),
        in_specs=[pl.BlockSpec((tm, D), rev), pl.BlockSpec((tm, D), rev), pl.BlockSpec((1, D), const2),
                  pl.BlockSpec((2, cps, tm, SH_UP), lambda i, k: (0, k, nt - 1 - i, 0)),
                  pl.BlockSpec((2, cps, tm, SH_UP), lambda i, k: (0, k, nt - 1 - i, 0)),
                  pl.BlockSpec((2, cps, D, SH_UP), pair), pl.BlockSpec((2, cps, 3, SH_UP), pair),
                  pl.BlockSpec((cps, SH_UP, D), lambda i, k: (k, 0, 0))],
        out_specs=[pl.BlockSpec((2, cps, tm, SH_UP), lambda i, k: (0, k, nt - 1 - i, 0)), pl.BlockSpec((tm, D), rev),
                   pl.BlockSpec((2, NCH, 8, D), lambda i, k: (0, 0, 0, 0)), pl.BlockSpec((8, D), const2)],
        out_shape=[jax.ShapeDtypeStruct((2, NCH, t, SH_UP), BF16), jax.ShapeDtypeStruct((t, D), F32),
                   jax.ShapeDtypeStruct((2, NCH, 8, D), F32), jax.ShapeDtypeStruct((8, D), F32)],
        scratch_shapes=[pltpu.VMEM((2, NCH, 8, SH_UP), F32), pltpu.VMEM((tm, D), F32),
                        pltpu.VMEM((2, NCH, 8, D), F32)],
        compiler_params=_cparams(2, VMEM_BIG),
    )(dx2, x1, g2, up, pre, w_up_g.reshape(2, NCH, D, SH_UP), fcw.reshape(2, NCH, 3, SH_UP), w_dn)


def bwd_mix(dx1, z, y_pool, y_conv, pool_w, pool_scale, conv_w, wmix, tm, seq):
    t = dx1.shape[0]
    nt = t // tm
    tps = seq // tm
    hb = tm // HALO

    def body(da_ref, z_ref, zph_ref, zcvh_ref, yp_ref, yc_ref, pw_ref, ps_ref, wpp_ref, cw_ref, wco_ref, wo_ref,
             dz_ref, mg_ref, p2_ref, u_ref, dyp_ref, dyc_ref, p_ref, dpw_ref, gvec_ref, cp_ref, cc_ref):
        i = pl.program_id(0)
        it = (nt - 1 - i) % tps
        keep_hist = jnp.where(it == 0, 0.0, 1.0)
        keep_next = jnp.where(it == tps - 1, 0.0, 1.0)
        pos = it * tm + lax.broadcasted_iota(jnp.int32, (tm, 1), 0)

        @pl.when(i == 0)
        def _():
            gvec_ref[...] = jnp.zeros_like(gvec_ref)
            cp_ref[...] = jnp.zeros_like(cp_ref)
            cc_ref[...] = jnp.zeros_like(cc_ref)

        dm = _dot_nt(da_ref[...].astype(BF16), wo_ref[...])
        merged, dyp, dyc = [], [], []
        for q in range(NG):
            lanes = slice(q * CG, (q + 1) * CG)
            sp = jax.nn.sigmoid(z_ref[16 + q].astype(F32))
            sc = jax.nn.sigmoid(z_ref[20 + q].astype(F32))
            yp = yp_ref[:, lanes].astype(F32)
            yc = yc_ref[:, lanes].astype(F32)
            dmq = dm[:, lanes]
            merged.append((sp * yp + sc * yc).astype(BF16))
            dyp.append((dmq * sp).astype(BF16))
            dyc.append((dmq * sc).astype(BF16))
            dz_ref[16 + q] = (dmq * yp * (sp * (1.0 - sp))).astype(BF16)
            dz_ref[20 + q] = (dmq * yc * (sc * (1.0 - sc))).astype(BF16)
        mg_ref[...] = jnp.concatenate(merged, axis=1)
        dypb = jnp.concatenate(dyp, axis=1)
        dycb = jnp.concatenate(dyc, axis=1)
        dyp_ref[...] = dypb
        dyc_ref[...] = dycb

        dp2 = _dot_nt(dypb, wpp_ref[...])
        p2 = []
        for g, win in enumerate(WINS):
            lanes = slice(g * CG, (g + 1) * CG)
            cnt = jnp.minimum(pos + 1, win).astype(F32)
            p = _pool_tile(z_ref, zph_ref, g, win, keep_hist, cnt)
            pb = p.astype(BF16)
            p_ref[g] = pb
            pw = _dot(pb, pw_ref[g])
            p2.append((pw * ps_ref[:, lanes]).astype(BF16))
            dp2g = dp2[:, lanes]
            gvec_ref[0:1, lanes] += jnp.sum(dp2g * pw, axis=0, keepdims=True)
            dpwb = (dp2g * ps_ref[:, lanes]).astype(BF16)
            dpw_ref[g] = dpwb
            dp = _dot_nt(dpwb, pw_ref[g])
            qv = dp / cnt
            ext = jnp.concatenate([qv, cp_ref[g] * keep_next], axis=0)
            cp_ref[g] = qv[:HALO]
            n = tm + HALO
            s, sh = ext, 1
            while sh < win:
                s = s + pltpu.roll(s, n - sh, 0)
                sh *= 2
            dz_ref[g] = (s[:tm] - dp).astype(BF16)
        p2_ref[...] = jnp.concatenate(p2, axis=1)

        du = _dot_nt(dycb, wco_ref[...])
        u = []
        for q in range(NG):
            lanes = slice(q * CG, (q + 1) * CG)
            zb = z_ref[4 + q].astype(F32)
            zc = z_ref[8 + q].astype(F32)
            zv = z_ref[12 + q].astype(F32)
            cv = zc * zv
            cvh = zcvh_ref[q].astype(F32) * zcvh_ref[4 + q].astype(F32) * keep_hist
            cc, cv1, cv2 = _conv_taps(jnp.concatenate([cvh, cv], axis=0), cv, cw_ref, lanes, HALO)
            u.append((zb * cc).astype(BF16))
            duq = du[:, lanes]
            dz_ref[4 + q] = (duq * cc).astype(BF16)
            dcc = duq * zb
            for tap, src in enumerate((cv2, cv1, cv)):
                gvec_ref[tap + 1:tap + 2, lanes] += jnp.sum(dcc * src, axis=0, keepdims=True)
            ext = jnp.concatenate([dcc, cc_ref[:, lanes] * keep_next], axis=0)
            cc_ref[:, lanes] = dcc[:8]
            dcv = (cw_ref[2:3, lanes] * dcc + cw_ref[1:2, lanes] * _shift_up(ext, 1, tm)
                   + cw_ref[0:1, lanes] * _shift_up(ext, 2, tm))
            dz_ref[8 + q] = (dcv * zv).astype(BF16)
            dz_ref[12 + q] = (dcv * zc).astype(BF16)
        u_ref[...] = jnp.concatenate(u, axis=1)

    def hist(i):
        return jnp.maximum((nt - 1 - i) * hb - 1, 0)

    rev = lambda i: (nt - 1 - i, 0)
    rev3 = lambda i: (0, nt - 1 - i, 0)
    const2 = lambda i: (0, 0)
    tok = jax.ShapeDtypeStruct((t, D), BF16)
    grp = jax.ShapeDtypeStruct((NG, t, CG), BF16)
    return pl.pallas_call(
        body, name="bwd_mix", grid=(nt,),
        in_specs=[pl.BlockSpec((tm, D), rev), pl.BlockSpec((NZT, tm, CG), rev3),
                  pl.BlockSpec((NG, HALO, CG), lambda i: (0, hist(i), 0)),
                  pl.BlockSpec((2 * NG, HALO, CG), lambda i: (1, hist(i), 0)),
                  pl.BlockSpec((tm, D), rev), pl.BlockSpec((tm, D), rev),
                  pl.BlockSpec((NG, CG, CG), lambda i: (0, 0, 0)), pl.BlockSpec((1, D), const2),
                  pl.BlockSpec((D, D), lambda i: (0, MIX_POOL_PROJ)), pl.BlockSpec((3, D), const2),
                  pl.BlockSpec((D, D), lambda i: (0, MIX_CONV_OUT)), pl.BlockSpec((D, D), lambda i: (0, MIX_O))],
        out_specs=[pl.BlockSpec((NZT, tm, CG), rev3)] + [pl.BlockSpec((tm, D), rev)] * 5
                  + [pl.BlockSpec((NG, tm, CG), rev3)] * 2 + [pl.BlockSpec((8, D), const2)],
        out_shape=[jax.ShapeDtypeStruct((NZT, t, CG), BF16), tok, tok, tok, tok, tok, grp, grp,
                   jax.ShapeDtypeStruct((8, D), F32)],
        scratch_shapes=[pltpu.VMEM((NG, HALO, CG), F32), pltpu.VMEM((8, D), F32)],
        compiler_params=_cparams(1, VMEM_BIG),
    )(dx1, z, z, z, y_pool, y_conv, pool_w, pool_scale, wmix, conv_w, wmix, wmix)


def bwd_in(dz, w_in_g, dx1, x, g1, tm):
    t = x.shape[0]

    def body(dz_ref, w_ref, dx1_ref, x_ref, g_ref, gx_ref, gn_ref):
        @pl.when(pl.program_id(0) == 0)
        def _():
            gn_ref[...] = jnp.zeros_like(gn_ref)

        dh = None
        for j in range(NDEV):
            dzc = jnp.concatenate([dz_ref[3 * j + q] for q in range(3)], axis=1)
            part = _dot_nt(dzc, w_ref[j])
            dh = part if dh is None else dh + part
        xv = x_ref[...]
        inv = _rms_inv(xv)
        xn = xv * inv
        gn_ref[0:1, :] += jnp.sum(dh * xn, axis=0, keepdims=True)
        gx_ref[...] = dx1_ref[...] + _rms_bwd(dh, xn, inv, g_ref[...])

    tile = lambda i: (i, 0)
    return pl.pallas_call(
        body, name="bwd_in", grid=(t // tm,),
        in_specs=[pl.BlockSpec((NZT, tm, CG), lambda i: (0, i, 0)),
                  pl.BlockSpec((NDEV, D, SH_IN), lambda i: (0, 0, 0)),
                  pl.BlockSpec((tm, D), tile), pl.BlockSpec((tm, D), tile), pl.BlockSpec((1, D), lambda i: (0, 0))],
        out_specs=[pl.BlockSpec((tm, D), tile), pl.BlockSpec((8, D), lambda i: (0, 0))],
        out_shape=[jax.ShapeDtypeStruct((t, D), F32), jax.ShapeDtypeStruct((8, D), F32)],
        compiler_params=_cparams(1, VMEM_BIG),
    )(dz, w_in_g, dx1, x, g1)


def _slot(j):
    return j % 2, j // 2


def wgrad_cols(at, b, q, name, tk):
    m, t = at.shape
    width = b.shape[3]

    def body(a_ref, b_ref, o_ref):
        @pl.when(pl.program_id(1) == 0)
        def _():
            o_ref[...] = jnp.zeros_like(o_ref)

        o_ref[...] += _dot(a_ref[...], b_ref[...])

    return pl.pallas_call(
        body, name=name, grid=(NDEV, t // tk),
        in_specs=[pl.BlockSpec((m, tk), lambda j, k: (0, k)),
                  pl.BlockSpec((None, None, tk, width), lambda j, k: (j, q, k, 0))],
        out_specs=pl.BlockSpec((None, None, m, width), lambda j, k: (j % 2, j // 2, 0, 0)),
        out_shape=jax.ShapeDtypeStruct((2, 4, m, width), F32),
        compiler_params=_cparams(2, VMEM_BIG),
    )(at, b)


def wgrad_down(actt, dx2, tk):
    t = dx2.shape[0]

    def body(a_ref, b_ref, o_ref, acc_ref):
        kt = pl.program_id(1)

        @pl.when(kt == 0)
        def _():
            acc_ref[...] = jnp.zeros_like(acc_ref)

        acc_ref[...] += _dot(a_ref[...], b_ref[...].astype(BF16))

        @pl.when(kt == pl.num_programs(1) - 1)
        def _():
            o_ref[0] = acc_ref[:SH_DN]
            o_ref[1] = acc_ref[SH_DN:]

    return pl.pallas_call(
        body, name="wgrad_down", grid=(NCH, t // tk),
        in_specs=[pl.BlockSpec((None, SH_UP, tk), lambda k, kt: (k, 0, kt)), pl.BlockSpec((tk, D), lambda k, kt: (kt, 0))],
        out_specs=pl.BlockSpec((2, None, SH_DN, D), lambda k, kt: (0, k, 0, 0)),
        out_shape=jax.ShapeDtypeStruct((2, 4, SH_DN, D), F32),
        scratch_shapes=[pltpu.VMEM((SH_UP, D), F32)],
        compiler_params=_cparams(2, VMEM_BIG),
    )(actt, dx2)


def wgrad_square(a, b, name, tk):
    t = a.shape[0]

    def body(a_ref, b_ref, o_ref, acc_ref):
        kt = pl.program_id(0)

        @pl.when(kt == 0)
        def _():
            acc_ref[...] = jnp.zeros_like(acc_ref)

        acc_ref[...] += _dot_tn(a_ref[...], b_ref[...].astype(BF16))

        @pl.when(kt == pl.num_programs(0) - 1)
        def _():
            for j in range(NDEV):
                cc, xy = _slot(j)
                o_ref[cc, xy] = acc_ref[j * 128:(j + 1) * 128]

    return pl.pallas_call(
        body, name=name, grid=(t // tk,),
        in_specs=[pl.BlockSpec((tk, D), lambda k: (k, 0)), pl.BlockSpec((tk, D), lambda k: (k, 0))],
        out_specs=pl.BlockSpec((2, 4, 128, D), lambda k: (0, 0, 0, 0)),
        out_shape=jax.ShapeDtypeStruct((2, 4, 128, D), F32),
        scratch_shapes=[pltpu.VMEM((D, D), F32)],
        compiler_params=_cparams(1, VMEM_BIG),
    )(a, b)


def wgrad_pool(p, dpw, tk):
    t = p.shape[1]

    def body(a_ref, b_ref, o_ref):
        @pl.when(pl.program_id(0) == 0)
        def _():
            o_ref[...] = jnp.zeros_like(o_ref)

        for g in range(NG):
            o_ref[g] += _dot_tn(a_ref[g], b_ref[g])

    return pl.pallas_call(
        body, name="wgrad_pool", grid=(t // tk,),
        in_specs=[pl.BlockSpec((NG, tk, CG), lambda k: (0, k, 0))] * 2,
        out_specs=pl.BlockSpec((NG, CG, CG), lambda k: (0, 0, 0)),
        out_shape=jax.ShapeDtypeStruct((NG, CG, CG), F32),
        compiler_params=_cparams(1, VMEM_BIG),
    )(p, dpw)


def _adamw(w, g, m, v):
    m = ADAM_B1 * m + (1.0 - ADAM_B1) * g
    v = ADAM_B2 * v + (1.0 - ADAM_B2) * (g * g)
    m_hat = m / (1.0 - ADAM_B1 ** ADAM_STEP)
    v_hat = v / (1.0 - ADAM_B2 ** ADAM_STEP)
    delta = -ADAM_LR * (m_hat / (jnp.sqrt(v_hat) + ADAM_EPS) + ADAM_WD * w)
    return delta, m, v


def _row_block(r):
    return 512 if r % 512 == 0 else r


def chip_partial(place, g, from_sibling, name):
    _, _, r, c = g.shape

    def body(place_ref, g_ref, s_ref, o_ref):
        o_ref[...] = (g_ref[...] + s_ref[...]).astype(BF16)

    return pl.pallas_call(
        body, name=name,
        grid_spec=pltpu.PrefetchScalarGridSpec(
            num_scalar_prefetch=1, grid=(3,),
            in_specs=[pl.BlockSpec((None, None, r, c), lambda k, pr: (pr[0], pr[1] ^ (k + 1), 0, 0)),
                      pl.BlockSpec((None, r, c), lambda k, pr: (pr[1] ^ (k + 1), 0, 0))],
            out_specs=pl.BlockSpec((None, r, c), lambda k, pr: (pr[1] ^ (k + 1), 0, 0))),
        out_shape=jax.ShapeDtypeStruct((4, r, c), BF16),
        compiler_params=_cparams(1, VMEM_BIG),
    )(place, g, from_sibling)


def finish_adamw(place, gs, from_sibling, from_chips, w, m, v, name):
    n = len(gs)
    r = gs[0].shape[2]
    widths = [g.shape[3] for g in gs]
    c = sum(widths)
    br = _row_block(r)

    def body(place_ref, *refs):
        g_refs, s_refs, c_refs = refs[:n], refs[n:2 * n], refs[2 * n:5 * n]
        w_ref, m_ref, v_ref, og_ref, od_ref, om_ref, ov_ref = refs[5 * n:]
        cols = []
        for q in range(n):
            grad = g_refs[q][...] + s_refs[q][...]
            for k in range(3):
                grad = grad + c_refs[3 * q + k][...].astype(F32)
            cols.append(grad)
        grad = cols[0] if n == 1 else jnp.concatenate(cols, axis=1)
        og_ref[...] = grad
        od_ref[...], om_ref[...], ov_ref[...] = _adamw(w_ref[...], grad, m_ref[...], v_ref[...])

    def other(k, cq):
        return pl.BlockSpec((None, br, cq), lambda i, pr: (pr[1] ^ k, i, 0))

    row = pl.BlockSpec((br, c), lambda i, pr: (i, 0))
    out = jax.ShapeDtypeStruct((r, c), F32)
    in_specs = [pl.BlockSpec((None, None, br, cq), lambda i, pr: (pr[0], pr[1], i, 0)) for cq in widths]
    in_specs += [pl.BlockSpec((None, br, cq), lambda i, pr: (pr[1], i, 0)) for cq in widths]
    in_specs += [other(k, cq) for cq in widths for k in (1, 2, 3)]
    return pl.pallas_call(
        body, name=name,
        grid_spec=pltpu.PrefetchScalarGridSpec(
            num_scalar_prefetch=1, grid=(r // br,), in_specs=in_specs + [row, row, row], out_specs=[row] * 4),
        out_shape=[out] * 4,
        compiler_params=_cparams(1, VMEM_BIG),
    )(place, *gs, *from_sibling, *[fc for fc in from_chips for _ in range(3)], w, m, v)


def adamw_small(items):
    n = len(items)

    def body(*refs):
        ins, outs = refs[:4 * n], refs[4 * n:]
        for i in range(n):
            w, g, m, v = (r[...] for r in ins[4 * i:4 * i + 4])
            outs[3 * i][...], outs[3 * i + 1][...], outs[3 * i + 2][...] = _adamw(w, g, m, v)

    out = [jax.ShapeDtypeStruct(it[0].shape, F32) for it in items for _ in range(3)]
    res = pl.pallas_call(body, name="adamw_small", out_shape=out)(*[a for it in items for a in it])
    return [res[3 * i:3 * i + 3] for i in range(n)]


def kernel(x, norm_mix, w_in, pool_w, pool_scale, w_pool_proj, conv_w, w_conv_out, w_o, norm_ffn, w_up, ffn_conv_w, ffn_conv_b, w_down, norm_final, loss_target, m_norm_mix, m_w_in, m_pool_w, m_pool_scale, m_w_pool_proj, m_conv_w, m_w_conv_out, m_w_o, m_norm_ffn, m_w_up, m_ffn_conv_w, m_ffn_conv_b, m_w_down, m_norm_final, v_norm_mix, v_w_in, v_pool_w, v_pool_scale, v_w_pool_proj, v_conv_w, v_w_conv_out, v_w_o, v_norm_ffn, v_w_up, v_ffn_conv_w, v_ffn_conv_b, v_w_down, v_norm_final):
    nb, seq, _ = x.shape
    t = nb * seq
    tm_in = min(TM_IN, t)
    tm_mix = min(TM_MIX, seq)
    tm_ffn = min(TM_FFN, seq)
    tk = min(TK_WGRAD, t)
    xt = x.reshape(t, D)
    tgt = loss_target.reshape(t, D)
    xi, yi, ci = _pos()
    me = 4 * xi + 2 * yi + ci
    place = jnp.stack([ci, 2 * xi + yi]).astype(jnp.int32)

    tie = lax.optimization_barrier
    w_in_g, = all_gather_blocks([w_in[0].astype(BF16)], "all_gather_w_in", 0)
    taps = (jnp.pad(conv_w[0], ((0, 5), (0, D - 128))) + jnp.pad(ffn_conv_w[0], ((3, 2), (0, D - SH_UP))))
    taps_g = _exchange_small(taps, False, "all_gather_taps")
    mix_shard = jnp.concatenate(
        [w_pool_proj[0], w_conv_out[0], w_o[0], pool_w[0].reshape(NG * 32, CG)], axis=1).astype(BF16)
    mix_shard, taps_g = tie((mix_shard, taps_g))
    wmix_g, = all_gather_blocks([mix_shard], "all_gather_w_mix", 0)
    ffn_shards, w_in_g = tie(([w_up[0].astype(BF16), w_down[0].astype(BF16)], w_in_g))
    w_up_g, w_dn_g = all_gather_blocks(ffn_shards, "all_gather_w_ffn", 0)
    wmix = wmix_g.reshape(D, MIX_COLS)
    pool_w_f = wmix_g[:, :, 3 * D:].reshape(NDEV, NG, 32, CG).transpose(1, 0, 2, 3).reshape(NG, CG, CG)
    w_dn_f = w_dn_g.reshape(NCH, SH_UP, D)
    conv_w_f = taps_g[:, 0:3, :128].transpose(1, 0, 2).reshape(3, D)
    fcw_f = taps_g[:, 3:6, :SH_UP]
    fcb_f = ffn_conv_b.reshape(NDEV, 1, SH_UP)
    gfin = norm_final.reshape(1, D)

    z, h1 = fwd_in(xt, norm_mix, w_in_g, tm_in)
    x1, y_pool, y_conv = fwd_mix(z, xt, pool_w_f, pool_scale, conv_w_f, wmix, tm_mix, seq)
    up, pre, act, h2, dx2, ffn_vec = fwd_ffn(x1, norm_ffn, w_up_g, fcw_f, fcb_f, w_dn_f, gfin, tgt, tm_ffn, seq,
                                             FFN_CHUNKS_PER_STEP)

    def to_sibling(full, tag):
        return reduce_scatter_d2d(full, "reduce_scatter_d2d_" + tag, 1)

    def partials(full, from_sib, names):
        return [chip_partial(place, g, s, "chip_partial_" + nm) for g, s, nm in zip(full, from_sib, names)]

    def to_chips(parts, tag):
        return reduce_scatter_ici(parts, "reduce_scatter_ici_" + tag, 2)

    def finish(nm, gs, from_sib, from_chips, wmv):
        w, m, v = wmv
        rc = (gs[0].shape[2], sum(g.shape[3] for g in gs))
        outs = finish_adamw(place, gs, from_sib, from_chips, w.reshape(rc), m.reshape(rc), v.reshape(rc), "adamw_" + nm)
        return [o.reshape(w.shape) for o in outs]

    def after(x, dep):
        return tie((x, dep))[0]

    big = {}
    gw_dn = wgrad_down(act, dx2, tk)
    sib_dn = to_sibling([gw_dn], "w_down")
    d_up, dx1, g_ffn_vec, g_nffn = bwd_ffn(dx2, x1, norm_ffn, up, pre, w_up_g, fcw_f, w_dn_f, tm_ffn, seq,
                                           FFN_CHUNKS_PER_STEP)
    d_up, part_dn = tie((d_up, partials([gw_dn], sib_dn, ["w_down"])))
    chips_dn = to_chips(part_dn, "w_down")
    gw_up = wgrad_cols(h2, d_up.reshape(NDEV, 1, t, SH_UP), 0, "wgrad_up", tk)
    sib_up = to_sibling([after(gw_up, chips_dn)], "w_up")
    dz, merged, p2, u, dyp, dyc, p, dpw, g_mix_vec = bwd_mix(
        dx1, z, y_pool, y_conv, pool_w_f, pool_scale, conv_w_f, wmix, tm_mix, seq)
    merged, part_up = tie((merged, partials([gw_up], sib_up, ["w_up"])))
    chips_up = to_chips(part_up, "w_up")
    gw_o = wgrad_square(merged, dx1, "wgrad_o", tk)
    gw_pp = wgrad_square(p2, dyp, "wgrad_pool_proj", tk)
    gw_co = wgrad_square(u, dyc, "wgrad_conv_out", tk)
    gw_pool = wgrad_pool(p, dpw, tk).reshape(NG, 4, 2, 32, CG).transpose(2, 1, 0, 3, 4).reshape(2, 4, NG * 32, CG)
    dz8 = dz.reshape(NDEV, 3, t, CG)
    gw_in, sib_in, chips_in = [None] * 3, [None] * 3, [None] * 3
    gw_in[0] = wgrad_cols(h1, dz8, 0, "wgrad_in_0", tk)
    sib_a = to_sibling(after([gw_o, gw_pp], (chips_up, gw_in[0])), "mix_a")
    sib_b = to_sibling(after([gw_co, gw_pool], sib_a), "mix_b")
    sib_in[0] = to_sibling(after([gw_in[0]], sib_b), "w_in_0")
    gw_in[1] = wgrad_cols(h1, dz8, 1, "wgrad_in_1", tk)
    sib_in[1] = to_sibling(after([gw_in[1]], sib_in[0]), "w_in_1")
    h1, part_a, part_b, part_in0 = tie((h1, partials([gw_o, gw_pp], sib_a, ["w_o", "w_pool_proj"]),
                                        partials([gw_co, gw_pool], sib_b, ["w_conv_out", "pool_w"]),
                                        partials([gw_in[0]], sib_in[0], ["w_in_0"])))
    chips_a = to_chips(after(part_a, sib_in[1]), "mix_a")
    chips_b = to_chips(part_b, "mix_b")
    chips_in[0] = to_chips(part_in0, "w_in_0")
    h1, big["w_down"], big["w_up"] = tie((
        h1, finish("w_down", [gw_dn], sib_dn, chips_dn, (w_down, m_w_down, v_w_down)),
        finish("w_up", [gw_up], sib_up, chips_up, (w_up, m_w_up, v_w_up))))
    gw_in[2] = wgrad_cols(h1, dz8, 2, "wgrad_in_2", tk)
    sib_in[2] = to_sibling(after([gw_in[2]], (chips_a, chips_b, chips_in[0])), "w_in_2")
    dx1, part_in1, part_in2, big["w_o"], big["w_pool_proj"], big["w_conv_out"], big["pool_w"] = tie((
        dx1, partials([gw_in[1]], sib_in[1], ["w_in_1"]), partials([gw_in[2]], sib_in[2], ["w_in_2"]),
        finish("w_o", [gw_o], sib_a[:1], chips_a[:1], (w_o, m_w_o, v_w_o)),
        finish("w_pool_proj", [gw_pp], sib_a[1:], chips_a[1:], (w_pool_proj, m_w_pool_proj, v_w_pool_proj)),
        finish("w_conv_out", [gw_co], sib_b[:1], chips_b[:1], (w_conv_out, m_w_conv_out, v_w_conv_out)),
        finish("pool_w", [gw_pool], sib_b[1:], chips_b[1:], (pool_w, m_pool_w, v_pool_w))))
    chips_in[1] = to_chips(after(part_in1, sib_in[2]), "w_in_1")
    chips_in[2] = to_chips(part_in2, "w_in_2")
    grad_x, g_nmix = bwd_in(dz, w_in_g, dx1, xt, norm_mix, min(TM_BWD_IN, t))
    grad_x, chips_in = tie((grad_x, chips_in))
    big["w_in"] = finish("w_in", gw_in, [s[0] for s in sib_in], [c[0] for c in chips_in], (w_in, m_w_in, v_w_in))

    red = _exchange_small(
        jnp.concatenate([g_nmix, g_mix_vec, g_nffn, ffn_vec, g_ffn_vec.reshape(8 * NDEV, D)], axis=0), True,
        "all_reduce_small")
    g_norm_mix, g_pool_scale, g_norm_ffn = red[0:1], red[8:9], red[16:17]
    g_conv_w = lax.dynamic_slice(red, (9, me * 128), (3, 128))
    g_norm_final = red[24]
    loss = red[25, 0]
    g_fcb = red[32:].reshape(NDEV, 8, D)[:, 0, :SH_UP].reshape(1, FF2)
    g_fcw = lax.dynamic_slice(red, (33 + 8 * me, 0), (3, SH_UP))
    grads = {"norm_mix": g_norm_mix, "pool_scale": g_pool_scale, "norm_ffn": g_norm_ffn, "norm_final": g_norm_final,
             "ffn_conv_b": g_fcb, "conv_w": g_conv_w.reshape(1, 3, 128), "ffn_conv_w": g_fcw.reshape(1, 3, SH_UP)}
    small_wmv = {"norm_mix": (norm_mix, m_norm_mix, v_norm_mix), "pool_scale": (pool_scale, m_pool_scale, v_pool_scale),
                 "norm_ffn": (norm_ffn, m_norm_ffn, v_norm_ffn), "norm_final": (norm_final, m_norm_final, v_norm_final),
                 "ffn_conv_b": (ffn_conv_b, m_ffn_conv_b, v_ffn_conv_b), "conv_w": (conv_w, m_conv_w, v_conv_w),
                 "ffn_conv_w": (ffn_conv_w, m_ffn_conv_w, v_ffn_conv_w)}
    small_names = list(small_wmv)
    flat2 = lambda a: a.reshape(-1, a.shape[-1])
    small_out = adamw_small([(flat2(small_wmv[nm][0]), flat2(grads[nm]), flat2(small_wmv[nm][1]),
                              flat2(small_wmv[nm][2])) for nm in small_names])
    small = {nm: [o.reshape(small_wmv[nm][0].shape) for o in outs] for nm, outs in zip(small_names, small_out)}

    order = ["norm_mix", "w_in", "pool_w", "pool_scale", "w_pool_proj", "conv_w", "w_conv_out", "w_o", "norm_ffn",
             "w_up", "ffn_conv_w", "ffn_conv_b", "w_down", "norm_final"]
    out = [loss, grad_x.reshape(nb, seq, D)]
    out += [big[nm][0] if nm in big else grads[nm] for nm in order]
    for idx in range(3):
        out += [big[nm][idx + 1] if nm in big else small[nm][idx] for nm in order]
    return tuple(out)
```

```python
import functools

import jax
import jax.numpy as jnp
from jax import lax
from jax.experimental import pallas as pl
from jax.experimental.pallas import tpu as pltpu
from jax.experimental.pallas import tpu_sc as plsc

F32 = jnp.float32
BF16 = jnp.bfloat16

NDEV = 8
D = 1024
NG = 4
CG = 256
WINS = (2, 4, 8, 16)
DIN = 6 * D
SH_IN = DIN // NDEV
NZT = DIN // CG
FF2 = 5632
SH_UP = FF2 // NDEV
FF = FF2 // 2
NCH = 4
SH_DN = FF // NDEV
RMS_EPS = 1e-6
HALO = 16

ADAM_LR = 0.001
ADAM_B1 = 0.9
ADAM_B2 = 0.999
ADAM_EPS = 1e-08
ADAM_WD = 0.01
ADAM_STEP = 10

TM_IN = 512
TM_BWD_IN = 256
TM_MIX = 256
TM_FFN = 256
FFN_CHUNKS_PER_STEP = 4
TK_WGRAD = 2048
MIX_POOL_PROJ, MIX_CONV_OUT, MIX_O = 0, 1, 2
MIX_COLS = 3 * D + CG
VMEM_BIG = 56 * 1024 * 1024
MESH = pl.DeviceIdType.MESH
ANY = pl.BlockSpec(memory_space=pl.ANY)


def _cparams(n_axes, vmem=None):
    return pltpu.CompilerParams(dimension_semantics=("arbitrary",) * n_axes, vmem_limit_bytes=vmem)


def _dot(a, b):
    return jnp.dot(a, b, preferred_element_type=F32)


def _dot_nt(a, b):
    return lax.dot_general(a, b, (((1,), (1,)), ((), ())), preferred_element_type=F32)


def _dot_tn(a, b):
    return lax.dot_general(a, b, (((0,), (0,)), ((), ())), preferred_element_type=F32)


def _shift_down(ext, s, lead):
    return pltpu.roll(ext, s, 0)[lead:]


def _shift_up(ext, s, tm):
    n = ext.shape[0]
    return pltpu.roll(ext, n - s, 0)[:tm]


def _rms_inv(x):
    return lax.rsqrt(jnp.mean(x * x, axis=-1, keepdims=True) + RMS_EPS)


def _rms_bwd(dh, xn, inv, g):
    dxn = dh * g
    return inv * (dxn - xn * jnp.mean(dxn * xn, axis=-1, keepdims=True))


def _pos():
    return lax.axis_index("x"), lax.axis_index("y"), lax.axis_index("c")


def _handshake(peers):
    barrier = pltpu.get_barrier_semaphore()
    for peer in peers:
        pl.semaphore_signal(barrier, inc=1, device_id=peer, device_id_type=MESH)
    pl.semaphore_wait(barrier, len(peers))


def _sequencer(body, out_type, n_sems, name, collective_id):
    return pl.kernel(
        body, out_type=out_type, mesh=plsc.ScalarSubcoreMesh(axis_name="sequencer", num_cores=1), name=name,
        scratch_types=[pltpu.SemaphoreType.DMA((n_sems,)), pltpu.SemaphoreType.DMA((n_sems,))],
        compiler_params=pltpu.CompilerParams(collective_id=collective_id))


def all_gather_blocks(shards, name, collective_id):
    n = len(shards)

    def body(*refs):
        ins, outs = refs[:n], refs[n:2 * n]
        send_sems, recv_sems = refs[2 * n:]
        x, y, c = _pos()
        sibling = (x, y, 1 - c)
        chips = [(1 - x, y), (x, 1 - y), (1 - x, 1 - y)]
        _handshake([sibling] + [(*chip, c) for chip in chips])

        def copy(w, k, block, to, src=None):
            slot = outs[w].at[4 * block[0] + 2 * block[1] + block[2]]
            return pltpu.make_async_remote_copy(
                src_ref=slot if src is None else src, dst_ref=slot,
                send_sem=send_sems.at[8 * w + k], recv_sem=recv_sems.at[8 * w + k], device_id=to, device_id_type=MESH)

        mine, first, passed = [], [], []
        for w in range(n):
            m = pltpu.make_async_copy(ins[w], outs[w].at[4 * x + 2 * y + c], send_sems.at[8 * w + 7])
            m.start()
            mine.append(m)
            first.append(copy(w, 0, (x, y, c), sibling, src=ins[w]))
            first += [copy(w, 1 + j, (x, y, c), (*chip, c), src=ins[w]) for j, chip in enumerate(chips)]
        for cp in first:
            cp.start()
        for w in range(n):
            for j, chip in enumerate(chips):
                copy(w, 1 + j, (*chip, c), (x, y, c)).wait_recv()
                fw = copy(w, 4 + j, (*chip, c), sibling)
                fw.start()
                passed.append(fw)
        for w in range(n):
            copy(w, 0, (x, y, 1 - c), (x, y, c)).wait_recv()
            for j, chip in enumerate(chips):
                copy(w, 4 + j, (*chip, 1 - c), (x, y, c)).wait_recv()
        for cp in first + passed:
            cp.wait_send()
        for m in mine:
            m.wait()

    out = [jax.ShapeDtypeStruct((NDEV,) + s.shape, s.dtype) for s in shards]
    return _sequencer(body, out, 8 * n, name, collective_id)(*shards)


def _exchange_small(v, reduce, name):
    rows = v.shape[0]

    def body(v_ref, out_ref, slots, send_sems, recv_sems, local_sem):
        x, y, c = _pos()
        me = 4 * x + 2 * y + c
        mine = pltpu.make_async_copy(v_ref, slots.at[me], local_sem)
        mine.start()
        offs = [(dx, dy, dc) for dx in (0, 1) for dy in (0, 1) for dc in (0, 1)][1:]

        def copy(k, src_slot, to):
            return pltpu.make_async_remote_copy(
                src_ref=v_ref, dst_ref=slots.at[src_slot], send_sem=send_sems.at[k], recv_sem=recv_sems.at[k],
                device_id=to, device_id_type=MESH)

        sends = []
        for k, (dx, dy, dc) in enumerate(offs):
            cp = copy(k, me, (x ^ dx, y ^ dy, c ^ dc))
            cp.start()
            sends.append(cp)
        for k, (dx, dy, dc) in enumerate(offs):
            copy(k, 4 * (x ^ dx) + 2 * (y ^ dy) + (c ^ dc), (x, y, c)).wait_recv()
        for cp in sends:
            cp.wait_send()
        mine.wait()
        if reduce:
            acc = slots[0]
            for d in range(1, NDEV):
                acc = acc + slots[d]
            out_ref[...] = acc
        else:
            out_ref[...] = slots[...]

    out = jax.ShapeDtypeStruct((rows, D) if reduce else (NDEV, rows, D), F32)
    return pl.pallas_call(
        body, name=name, out_shape=out,
        in_specs=[pl.BlockSpec(memory_space=pltpu.VMEM)], out_specs=pl.BlockSpec(memory_space=pltpu.VMEM),
        scratch_shapes=[pltpu.VMEM((NDEV, rows, D), F32), pltpu.SemaphoreType.DMA((7,)),
                        pltpu.SemaphoreType.DMA((7,)), pltpu.SemaphoreType.DMA],
    )(v)


def reduce_scatter_d2d(grads, name, collective_id):
    n = len(grads)

    def body(*refs):
        ins, outs = refs[:n], refs[n:2 * n]
        send_sems, recv_sems = refs[2 * n:]
        x, y, c = _pos()
        _handshake([(x, y, 1 - c)])
        cps = []
        for w in range(n):
            cp = pltpu.make_async_remote_copy(
                src_ref=ins[w].at[1 - c], dst_ref=outs[w], send_sem=send_sems.at[w], recv_sem=recv_sems.at[w],
                device_id=(x, y, 1 - c), device_id_type=MESH)
            cp.start()
            cps.append(cp)
        for cp in cps:
            cp.wait_recv()
        for cp in cps:
            cp.wait_send()

    out = [jax.ShapeDtypeStruct(g.shape[1:], F32) for g in grads]
    return _sequencer(body, out, n, name, collective_id)(*grads)


def reduce_scatter_ici(parts, name, collective_id):
    n = len(parts)

    def body(*refs):
        ins, outs = refs[:n], refs[n:2 * n]
        send_sems, recv_sems = refs[2 * n:]
        x, y, c = _pos()
        offs = [(1, 0), (0, 1), (1, 1)]
        _handshake([(x ^ dx, y ^ dy, c) for dx, dy in offs])
        cps = []
        for w in range(n):
            for k, (dx, dy) in enumerate(offs):
                ox, oy = x ^ dx, y ^ dy
                cp = pltpu.make_async_remote_copy(
                    src_ref=ins[w].at[2 * ox + oy], dst_ref=outs[w].at[2 * x + y],
                    send_sem=send_sems.at[3 * w + k], recv_sem=recv_sems.at[3 * w + k],
                    device_id=(ox, oy, c), device_id_type=MESH)
                cp.start()
                cps.append((cp, w, k, ox, oy))
        for cp, w, k, ox, oy in cps:
            pltpu.make_async_remote_copy(
                src_ref=ins[w].at[2 * ox + oy], dst_ref=outs[w].at[2 * ox + oy],
                send_sem=send_sems.at[3 * w + k], recv_sem=recv_sems.at[3 * w + k],
                device_id=(ox, oy, c), device_id_type=MESH).wait_recv()
        for cp, *_ in cps:
            cp.wait_send()

    out = [jax.ShapeDtypeStruct(p.shape, BF16) for p in parts]
    return _sequencer(body, out, 3 * n, name, collective_id)(*parts)


def fwd_in(x, g1, w_in_g, tm):
    t = x.shape[0]

    def body(x_ref, g_ref, w_ref, z_ref, ht_ref):
        xf = x_ref[...]
        h = (xf * _rms_inv(xf) * g_ref[...]).astype(BF16)
        ht_ref[...] = h.T
        for j in range(NDEV):
            r = _dot(h, w_ref[j])
            for q in range(3):
                z_ref[3 * j + q] = r[:, q * CG:(q + 1) * CG].astype(BF16)

    return pl.pallas_call(
        body, name="fwd_in", grid=(t // tm,),
        in_specs=[pl.BlockSpec((tm, D), lambda i: (i, 0)), pl.BlockSpec((1, D), lambda i: (0, 0)),
                  pl.BlockSpec((NDEV, D, SH_IN), lambda i: (0, 0, 0))],
        out_specs=[pl.BlockSpec((NZT, tm, CG), lambda i: (0, i, 0)), pl.BlockSpec((D, tm), lambda i: (0, i))],
        out_shape=[jax.ShapeDtypeStruct((NZT, t, CG), BF16), jax.ShapeDtypeStruct((D, t), BF16)],
        compiler_params=_cparams(1, VMEM_BIG),
    )(x, g1, w_in_g)


def _pool_tile(z_ref, zh_ref, g, win, keep_hist, cnt):
    zt = z_ref[g].astype(F32)
    ext = jnp.concatenate([zh_ref[g].astype(F32) * keep_hist, zt], axis=0)
    s, sh = ext, 1
    while sh < win:
        s = s + pltpu.roll(s, sh, 0)
        sh *= 2
    return s[HALO:] / cnt - zt


def _conv_taps(ext, cur, w_ref, lanes, lead):
    x1 = _shift_down(ext, 1, lead)
    x2 = _shift_down(ext, 2, lead)
    out = w_ref[2:3, lanes] * cur + w_ref[1:2, lanes] * x1 + w_ref[0:1, lanes] * x2
    return out, x1, x2


def fwd_mix(z, x, pool_w, pool_scale, conv_w, wmix, tm, seq):
    t = x.shape[0]
    tps = seq // tm
    hb = tm // HALO

    def body(z_ref, zph_ref, zcvh_ref, x_ref, pw_ref, ps_ref, wpp_ref, cw_ref, wco_ref, wo_ref,
             x1_ref, yp_ref, yc_ref):
        it = pl.program_id(0) % tps
        keep_hist = jnp.where(it == 0, 0.0, 1.0)
        pos = it * tm + lax.broadcasted_iota(jnp.int32, (tm, 1), 0)
        p2 = []
        for g, win in enumerate(WINS):
            cnt = jnp.minimum(pos + 1, win).astype(F32)
            p = _pool_tile(z_ref, zph_ref, g, win, keep_hist, cnt)
            lanes = slice(g * CG, (g + 1) * CG)
            p2.append((_dot(p.astype(BF16), pw_ref[g]) * ps_ref[:, lanes]).astype(BF16))
        y_pool = _dot(jnp.concatenate(p2, axis=1), wpp_ref[...])
        u = []
        for q in range(NG):
            lanes = slice(q * CG, (q + 1) * CG)
            cv = z_ref[8 + q].astype(F32) * z_ref[12 + q].astype(F32)
            cvh = zcvh_ref[q].astype(F32) * zcvh_ref[4 + q].astype(F32) * keep_hist
            cc, _, _ = _conv_taps(jnp.concatenate([cvh, cv], axis=0), cv, cw_ref, lanes, HALO)
            u.append((z_ref[4 + q].astype(F32) * cc).astype(BF16))
        y_conv = _dot(jnp.concatenate(u, axis=1), wco_ref[...])
        ypb, ycb = y_pool.astype(BF16), y_conv.astype(BF16)
        yp_ref[...] = ypb
        yc_ref[...] = ycb
        merged = []
        for q in range(NG):
            lanes = slice(q * CG, (q + 1) * CG)
            sp = jax.nn.sigmoid(z_ref[16 + q].astype(F32))
            sc = jax.nn.sigmoid(z_ref[20 + q].astype(F32))
            merged.append((sp * ypb[:, lanes].astype(F32) + sc * ycb[:, lanes].astype(F32)).astype(BF16))
        x1_ref[...] = x_ref[...] + _dot(jnp.concatenate(merged, axis=1), wo_ref[...])

    def hist(i):
        return jnp.maximum(i * hb - 1, 0)

    const2 = lambda i: (0, 0)
    return pl.pallas_call(
        body, name="fwd_mix", grid=(t // tm,),
        in_specs=[pl.BlockSpec((NZT, tm, CG), lambda i: (0, i, 0)),
                  pl.BlockSpec((NG, HALO, CG), lambda i: (0, hist(i), 0)),
                  pl.BlockSpec((2 * NG, HALO, CG), lambda i: (1, hist(i), 0)),
                  pl.BlockSpec((tm, D), lambda i: (i, 0)),
                  pl.BlockSpec((NG, CG, CG), lambda i: (0, 0, 0)), pl.BlockSpec((1, D), const2),
                  pl.BlockSpec((D, D), lambda i: (0, MIX_POOL_PROJ)), pl.BlockSpec((3, D), const2),
                  pl.BlockSpec((D, D), lambda i: (0, MIX_CONV_OUT)), pl.BlockSpec((D, D), lambda i: (0, MIX_O))],
        out_specs=[pl.BlockSpec((tm, D), lambda i: (i, 0))] * 3,
        out_shape=[jax.ShapeDtypeStruct((t, D), F32), jax.ShapeDtypeStruct((t, D), BF16),
                   jax.ShapeDtypeStruct((t, D), BF16)],
        compiler_params=_cparams(1, VMEM_BIG),
    )(z, z, z, x, pool_w, pool_scale, wmix, conv_w, wmix, wmix)


def fwd_ffn(x1, g2, w_up_g, fcw, fcb, w_dn, gf, tgt, tm, seq, cps):
    t = x1.shape[0]
    tps = seq // tm

    def body(x1_ref, g2_ref, wup_ref, fcw_ref, fcb_ref, wdn_ref, gf_ref, tgt_ref,
             up_ref, pre_ref, actt_ref, h2t_ref, dx2_ref, vec_ref, hist_ref, d_ref, h2_ref):
        i, k = pl.program_id(0), pl.program_id(1)
        keep_hist = jnp.where(i % tps == 0, 0.0, 1.0)

        @pl.when((i == 0) & (k == 0))
        def _():
            vec_ref[...] = jnp.zeros_like(vec_ref)
            hist_ref[...] = jnp.zeros_like(hist_ref)

        @pl.when(k == 0)
        def _():
            x1v = x1_ref[...]
            h2 = (x1v * _rms_inv(x1v) * g2_ref[...]).astype(BF16)
            h2_ref[...] = h2
            h2t_ref[...] = h2.T
            d_ref[...] = jnp.zeros_like(d_ref)

        h2 = h2_ref[...]
        lanes = slice(0, SH_UP)
        d = d_ref[...]
        for c in range(cps):
            kc = k * cps + c
            conv = []
            for s in range(2):
                ub = _dot(h2, wup_ref[s, c]).astype(BF16)
                up_ref[s, c] = ub
                uf = ub.astype(F32)
                ext = jnp.concatenate([hist_ref[s, kc] * keep_hist, uf], axis=0)
                hist_ref[s, kc] = uf[tm - 8:]
                cc, _, _ = _conv_taps(ext, uf, fcw_ref.at[s, c], lanes, 8)
                conv.append(cc + fcb_ref[s, c])
                pre_ref[s, c] = conv[s].astype(BF16)
            a = (conv[0] * jax.nn.sigmoid(conv[0]) * conv[1]).astype(BF16)
            actt_ref[c] = a.T
            d = d + _dot(a, wdn_ref[c])
        d_ref[...] = d

        @pl.when(k == NCH // cps - 1)
        def _():
            x2 = x1_ref[...] + d_ref[...]
            inv3 = _rms_inv(x2)
            xn = x2 * inv3
            diff = xn * gf_ref[...] - tgt_ref[...]
            dy = diff * (1.0 / D)
            vec_ref[0:1, :] += jnp.sum(dy * xn, axis=0, keepdims=True)
            vec_ref[1:2, :] += 0.5 * jnp.sum(jnp.mean(diff * diff, axis=-1))
            dx2_ref[...] = _rms_bwd(dy, xn, inv3, gf_ref[...])

    tile = lambda i, k: (i, 0)
    const2 = lambda i, k: (0, 0)
    pair = lambda i, k: (0, k, 0, 0)
    return pl.pallas_call(
        body, name="fwd_ffn", grid=(t // tm, NCH // cps),
        in_specs=[pl.BlockSpec((tm, D), tile), pl.BlockSpec((1, D), const2),
                  pl.BlockSpec((2, cps, D, SH_UP), pair), pl.BlockSpec((2, cps, 3, SH_UP), pair),
                  pl.BlockSpec((2, cps, 1, SH_UP), pair), pl.BlockSpec((cps, SH_UP, D), lambda i, k: (k, 0, 0)),
                  pl.BlockSpec((1, D), const2), pl.BlockSpec((tm, D), tile)],
        out_specs=[pl.BlockSpec((2, cps, tm, SH_UP), lambda i, k: (0, k, i, 0)),
                   pl.BlockSpec((2, cps, tm, SH_UP), lambda i, k: (0, k, i, 0)),
                   pl.BlockSpec((cps, SH_UP, tm), lambda i, k: (k, 0, i)),
                   pl.BlockSpec((D, tm), lambda i, k: (0, i)), pl.BlockSpec((tm, D), tile), pl.BlockSpec((8, D), const2)],
        out_shape=[jax.ShapeDtypeStruct((2, NCH, t, SH_UP), BF16), jax.ShapeDtypeStruct((2, NCH, t, SH_UP), BF16),
                   jax.ShapeDtypeStruct((NCH, SH_UP, t), BF16),
                   jax.ShapeDtypeStruct((D, t), BF16), jax.ShapeDtypeStruct((t, D), F32),
                   jax.ShapeDtypeStruct((8, D), F32)],
        scratch_shapes=[pltpu.VMEM((2, NCH, 8, SH_UP), F32), pltpu.VMEM((tm, D), F32), pltpu.VMEM((tm, D), BF16)],
        compiler_params=_cparams(2, VMEM_BIG),
    )(x1, g2, w_up_g.reshape(2, NCH, D, SH_UP), fcw.reshape(2, NCH, 3, SH_UP), fcb.reshape(2, NCH, 1, SH_UP),
      w_dn, gf, tgt)


def bwd_ffn(dx2, x1, g2, up, pre, w_up_g, fcw, w_dn, tm, seq, cps):
    t = x1.shape[0]
    nt = t // tm
    tps = seq // tm

    def body(dx2_ref, x1_ref, g2_ref, up_ref, pre_ref, wup_ref, fcw_ref, wdn_ref,
             dup_ref, dx1_ref, gvec_ref, gn_ref, carry_ref, dh2_ref, acc_ref):
        i, k = pl.program_id(0), pl.program_id(1)
        it = (nt - 1 - i) % tps
        keep_next = jnp.where(it == tps - 1, 0.0, 1.0)

        @pl.when((i == 0) & (k == 0))
        def _():
            acc_ref[...] = jnp.zeros_like(acc_ref)
            gn_ref[...] = jnp.zeros_like(gn_ref)
            carry_ref[...] = jnp.zeros_like(carry_ref)

        @pl.when(k == 0)
        def _():
            dh2_ref[...] = jnp.zeros_like(dh2_ref)

        dxb = dx2_ref[...].astype(BF16)
        lanes = slice(0, SH_UP)
        dh2 = dh2_ref[...]
        for c in range(cps):
            kc = k * cps + c
            pre = [pre_ref[s, c].astype(F32) for s in range(2)]
            sg = jax.nn.sigmoid(pre[0])
            dact = _dot_nt(dxb, wdn_ref[c])
            dpre = [dact * pre[1] * (sg * (1.0 + pre[0] * (1.0 - sg))), dact * (pre[0] * sg)]
            for s in range(2):
                dc = dpre[s]
                ext = jnp.concatenate([dc, carry_ref[s, kc] * keep_next], axis=0)
                carry_ref[s, kc] = dc[:8]
                shifted = (_shift_up(ext, 2, tm), _shift_up(ext, 1, tm), dc)
                uf = up_ref[s, c].astype(F32)
                acc_ref[s, kc, 0:1, lanes] += jnp.sum(dc, axis=0, keepdims=True)
                for tap in range(3):
                    acc_ref[s, kc, tap + 1:tap + 2, lanes] += jnp.sum(shifted[tap] * uf, axis=0, keepdims=True)
                w = fcw_ref.at[s, c]
                du = w[2:3, :] * dc + w[1:2, :] * shifted[1] + w[0:1, :] * shifted[0]
                dub = du.astype(BF16)
                dup_ref[s, c] = dub
                dh2 = dh2 + _dot_nt(dub, wup_ref[s, c])
        dh2_ref[...] = dh2

        @pl.when(k == NCH // cps - 1)
        def _():
            x1v = x1_ref[...]
            inv2 = _rms_inv(x1v)
            xn = x1v * inv2
            gn_ref[0:1, :] += jnp.sum(dh2 * xn, axis=0, keepdims=True)
            dx1_ref[...] = dx2_ref[...] + _rms_bwd(dh2, xn, inv2, g2_ref[...])

        @pl.when((i == nt - 1) & (k == NCH // cps - 1))
        def _():
            gvec_ref[...] = acc_ref[...]

    rev = lambda i, k: (nt - 1 - i, 0)
    const2 = lambda i, k: (0, 0)
    pair = lambda i, k: (0, k, 0, 0)
    return pl.pallas_call(
        body, name="bwd_ffn", grid=(nt, NCH // cps),
        in_specs=[pl.BlockSpec((tm, D), rev), pl.BlockSpec((tm, D), rev), pl.BlockSpec((1, D), const2),
                  pl.BlockSpec((2, cps, tm, SH_UP), lambda i, k: (0, k, nt - 1 - i, 0)),
                  pl.BlockSpec((2, cps, tm, SH_UP), lambda i, k: (0, k, nt - 1 - i, 0)),
                  pl.BlockSpec((2, cps, D, SH_UP), pair), pl.BlockSpec((2, cps, 3, SH_UP), pair),
                  pl.BlockSpec((cps, SH_UP, D), lambda i, k: (k, 0, 0))],
        out_specs=[pl.BlockSpec((2, cps, tm, SH_UP), lambda i, k: (0, k, nt - 1 - i, 0)), pl.BlockSpec((tm, D), rev),
                   pl.BlockSpec((2, NCH, 8, D), lambda i, k: (0, 0, 0, 0)), pl.BlockSpec((8, D), const2)],
        out_shape=[jax.ShapeDtypeStruct((2, NCH, t, SH_UP), BF16), jax.ShapeDtypeStruct((t, D), F32),
                   jax.ShapeDtypeStruct((2, NCH, 8, D), F32), jax.ShapeDtypeStruct((8, D), F32)],
        scratch_shapes=[pltpu.VMEM((2, NCH, 8, SH_UP), F32), pltpu.VMEM((tm, D), F32),
                        pltpu.VMEM((2, NCH, 8, D), F32)],
        compiler_params=_cparams(2, VMEM_BIG),
    )(dx2, x1, g2, up, pre, w_up_g.reshape(2, NCH, D, SH_UP), fcw.reshape(2, NCH, 3, SH_UP), w_dn)


def bwd_mix(dx1, z, y_pool, y_conv, pool_w, pool_scale, conv_w, wmix, tm, seq):
    t = dx1.shape[0]
    nt = t // tm
    tps = seq // tm
    hb = tm // HALO

    def body(da_ref, z_ref, zph_ref, zcvh_ref, yp_ref, yc_ref, pw_ref, ps_ref, wpp_ref, cw_ref, wco_ref, wo_ref,
             dz_ref, mg_ref, p2_ref, u_ref, dyp_ref, dyc_ref, p_ref, dpw_ref, gvec_ref, cp_ref, cc_ref):
        i = pl.program_id(0)
        it = (nt - 1 - i) % tps
        keep_hist = jnp.where(it == 0, 0.0, 1.0)
        keep_next = jnp.where(it == tps - 1, 0.0, 1.0)
        pos = it * tm + lax.broadcasted_iota(jnp.int32, (tm, 1), 0)

        @pl.when(i == 0)
        def _():
            gvec_ref[...] = jnp.zeros_like(gvec_ref)
            cp_ref[...] = jnp.zeros_like(cp_ref)
            cc_ref[...] = jnp.zeros_like(cc_ref)

        dm = _dot_nt(da_ref[...].astype(BF16), wo_ref[...])
        merged, dyp, dyc = [], [], []
        for q in range(NG):
            lanes = slice(q * CG, (q + 1) * CG)
            sp = jax.nn.sigmoid(z_ref[16 + q].astype(F32))
            sc = jax.nn.sigmoid(z_ref[20 + q].astype(F32))
            yp = yp_ref[:, lanes].astype(F32)
            yc = yc_ref[:, lanes].astype(F32)
            dmq = dm[:, lanes]
            merged.append((sp * yp + sc * yc).astype(BF16))
            dyp.append((dmq * sp).astype(BF16))
            dyc.append((dmq * sc).astype(BF16))
            dz_ref[16 + q] = (dmq * yp * (sp * (1.0 - sp))).astype(BF16)
            dz_ref[20 + q] = (dmq * yc * (sc * (1.0 - sc))).astype(BF16)
        mg_ref[...] = jnp.concatenate(merged, axis=1)
        dypb = jnp.concatenate(dyp, axis=1)
        dycb = jnp.concatenate(dyc, axis=1)
        dyp_ref[...] = dypb
        dyc_ref[...] = dycb

        dp2 = _dot_nt(dypb, wpp_ref[...])
        p2 = []
        for g, win in enumerate(WINS):
            lanes = slice(g * CG, (g + 1) * CG)
            cnt = jnp.minimum(pos + 1, win).astype(F32)
            p = _pool_tile(z_ref, zph_ref, g, win, keep_hist, cnt)
            pb = p.astype(BF16)
            p_ref[g] = pb
            pw = _dot(pb, pw_ref[g])
            p2.append((pw * ps_ref[:, lanes]).astype(BF16))
            dp2g = dp2[:, lanes]
            gvec_ref[0:1, lanes] += jnp.sum(dp2g * pw, axis=0, keepdims=True)
            dpwb = (dp2g * ps_ref[:, lanes]).astype(BF16)
            dpw_ref[g] = dpwb
            dp = _dot_nt(dpwb, pw_ref[g])
            qv = dp / cnt
            ext = jnp.concatenate([qv, cp_ref[g] * keep_next], axis=0)
            cp_ref[g] = qv[:HALO]
            n = tm + HALO
            s, sh = ext, 1
            while sh < win:
                s = s + pltpu.roll(s, n - sh, 0)
                sh *= 2
            dz_ref[g] = (s[:tm] - dp).astype(BF16)
        p2_ref[...] = jnp.concatenate(p2, axis=1)

        du = _dot_nt(dycb, wco_ref[...])
        u = []
        for q in range(NG):
            lanes = slice(q * CG, (q + 1) * CG)
            zb = z_ref[4 + q].astype(F32)
            zc = z_ref[8 + q].astype(F32)
            zv = z_ref[12 + q].astype(F32)
            cv = zc * zv
            cvh = zcvh_ref[q].astype(F32) * zcvh_ref[4 + q].astype(F32) * keep_hist
            cc, cv1, cv2 = _conv_taps(jnp.concatenate([cvh, cv], axis=0), cv, cw_ref, lanes, HALO)
            u.append((zb * cc).astype(BF16))
            duq = du[:, lanes]
            dz_ref[4 + q] = (duq * cc).astype(BF16)
            dcc = duq * zb
            for tap, src in enumerate((cv2, cv1, cv)):
                gvec_ref[tap + 1:tap + 2, lanes] += jnp.sum(dcc * src, axis=0, keepdims=True)
            ext = jnp.concatenate([dcc, cc_ref[:, lanes] * keep_next], axis=0)
            cc_ref[:, lanes] = dcc[:8]
            dcv = (cw_ref[2:3, lanes] * dcc + cw_ref[1:2, lanes] * _shift_up(ext, 1, tm)
                   + cw_ref[0:1, lanes] * _shift_up(ext, 2, tm))
            dz_ref[8 + q] = (dcv * zv).astype(BF16)
            dz_ref[12 + q] = (dcv * zc).astype(BF16)
        u_ref[...] = jnp.concatenate(u, axis=1)

    def hist(i):
        return jnp.maximum((nt - 1 - i) * hb - 1, 0)

    rev = lambda i: (nt - 1 - i, 0)
    rev3 = lambda i: (0, nt - 1 - i, 0)
    const2 = lambda i: (0, 0)
    tok = jax.ShapeDtypeStruct((t, D), BF16)
    grp = jax.ShapeDtypeStruct((NG, t, CG), BF16)
    return pl.pallas_call(
        body, name="bwd_mix", grid=(nt,),
        in_specs=[pl.BlockSpec((tm, D), rev), pl.BlockSpec((NZT, tm, CG), rev3),
                  pl.BlockSpec((NG, HALO, CG), lambda i: (0, hist(i), 0)),
                  pl.BlockSpec((2 * NG, HALO, CG), lambda i: (1, hist(i), 0)),
                  pl.BlockSpec((tm, D), rev), pl.BlockSpec((tm, D), rev),
                  pl.BlockSpec((NG, CG, CG), lambda i: (0, 0, 0)), pl.BlockSpec((1, D), const2),
                  pl.BlockSpec((D, D), lambda i: (0, MIX_POOL_PROJ)), pl.BlockSpec((3, D), const2),
                  pl.BlockSpec((D, D), lambda i: (0, MIX_CONV_OUT)), pl.BlockSpec((D, D), lambda i: (0, MIX_O))],
        out_specs=[pl.BlockSpec((NZT, tm, CG), rev3)] + [pl.BlockSpec((tm, D), rev)] * 5
                  + [pl.BlockSpec((NG, tm, CG), rev3)] * 2 + [pl.BlockSpec((8, D), const2)],
        out_shape=[jax.ShapeDtypeStruct((NZT, t, CG), BF16), tok, tok, tok, tok, tok, grp, grp,
                   jax.ShapeDtypeStruct((8, D), F32)],
        scratch_shapes=[pltpu.VMEM((NG, HALO, CG), F32), pltpu.VMEM((8, D), F32)],
        compiler_params=_cparams(1, VMEM_BIG),
    )(dx1, z, z, z, y_pool, y_conv, pool_w, pool_scale, wmix, conv_w, wmix, wmix)


def bwd_in(dz, w_in_g, dx1, x, g1, tm):
    t = x.shape[0]

    def body(dz_ref, w_ref, dx1_ref, x_ref, g_ref, gx_ref, gn_ref):
        @pl.when(pl.program_id(0) == 0)
        def _():
            gn_ref[...] = jnp.zeros_like(gn_ref)

        dh = None
        for j in range(NDEV):
            dzc = jnp.concatenate([dz_ref[3 * j + q] for q in range(3)], axis=1)
            part = _dot_nt(dzc, w_ref[j])
            dh = part if dh is None else dh + part
        xv = x_ref[...]
        inv = _rms_inv(xv)
        xn = xv * inv
        gn_ref[0:1, :] += jnp.sum(dh * xn, axis=0, keepdims=True)
        gx_ref[...] = dx1_ref[...] + _rms_bwd(dh, xn, inv, g_ref[...])

    tile = lambda i: (i, 0)
    return pl.pallas_call(
        body, name="bwd_in", grid=(t // tm,),
        in_specs=[pl.BlockSpec((NZT, tm, CG), lambda i: (0, i, 0)),
                  pl.BlockSpec((NDEV, D, SH_IN), lambda i: (0, 0, 0)),
                  pl.BlockSpec((tm, D), tile), pl.BlockSpec((tm, D), tile), pl.BlockSpec((1, D), lambda i: (0, 0))],
        out_specs=[pl.BlockSpec((tm, D), tile), pl.BlockSpec((8, D), lambda i: (0, 0))],
        out_shape=[jax.ShapeDtypeStruct((t, D), F32), jax.ShapeDtypeStruct((8, D), F32)],
        compiler_params=_cparams(1, VMEM_BIG),
    )(dz, w_in_g, dx1, x, g1)


def _slot(j):
    return j % 2, j // 2


def wgrad_cols(at, b, q, name, tk):
    m, t = at.shape
    width = b.shape[3]

    def body(a_ref, b_ref, o_ref):
        @pl.when(pl.program_id(1) == 0)
        def _():
            o_ref[...] = jnp.zeros_like(o_ref)

        o_ref[...] += _dot(a_ref[...], b_ref[...])

    return pl.pallas_call(
        body, name=name, grid=(NDEV, t // tk),
        in_specs=[pl.BlockSpec((m, tk), lambda j, k: (0, k)),
                  pl.BlockSpec((None, None, tk, width), lambda j, k: (j, q, k, 0))],
        out_specs=pl.BlockSpec((None, None, m, width), lambda j, k: (j % 2, j // 2, 0, 0)),
        out_shape=jax.ShapeDtypeStruct((2, 4, m, width), F32),
        compiler_params=_cparams(2, VMEM_BIG),
    )(at, b)


def wgrad_cols_resident(at, b, q, name, tk):
    m, t = at.shape
    width = b.shape[3]

    def body(a_ref, b_ref, o_ref):
        k, j = pl.program_id(0), pl.program_id(1)

        @pl.when((k == 0) & (j == 0))
        def _():
            o_ref[...] = jnp.zeros_like(o_ref)

        o_ref[j % 2, j // 2] += _dot(a_ref[...], b_ref[...])

    return pl.pallas_call(
        body, name=name, grid=(t // tk, NDEV),
        in_specs=[pl.BlockSpec((m, tk), lambda k, j: (0, k)),
                  pl.BlockSpec((None, None, tk, width), lambda k, j: (j, q, k, 0))],
        out_specs=pl.BlockSpec((2, 4, m, width), lambda k, j: (0, 0, 0, 0)),
        out_shape=jax.ShapeDtypeStruct((2, 4, m, width), F32),
        compiler_params=_cparams(2, VMEM_BIG),
    )(at, b)


def wgrad_down(actt, dx2, tk):
    t = dx2.shape[0]

    def body(a_ref, b_ref, o_ref, acc_ref):
        kt = pl.program_id(1)

        @pl.when(kt == 0)
        def _():
            acc_ref[...] = jnp.zeros_like(acc_ref)

        acc_ref[...] += _dot(a_ref[...], b_ref[...].astype(BF16))

        @pl.when(kt == pl.num_programs(1) - 1)
        def _():
            o_ref[0] = acc_ref[:SH_DN]
            o_ref[1] = acc_ref[SH_DN:]

    return pl.pallas_call(
        body, name="wgrad_down", grid=(NCH, t // tk),
        in_specs=[pl.BlockSpec((None, SH_UP, tk), lambda k, kt: (k, 0, kt)), pl.BlockSpec((tk, D), lambda k, kt: (kt, 0))],
        out_specs=pl.BlockSpec((2, None, SH_DN, D), lambda k, kt: (0, k, 0, 0)),
        out_shape=jax.ShapeDtypeStruct((2, 4, SH_DN, D), F32),
        scratch_shapes=[pltpu.VMEM((SH_UP, D), F32)],
        compiler_params=_cparams(2, VMEM_BIG),
    )(actt, dx2)


def wgrad_square(a, b, name, tk):
    t = a.shape[0]

    def body(a_ref, b_ref, o_ref, acc_ref):
        kt = pl.program_id(0)

        @pl.when(kt == 0)
        def _():
            acc_ref[...] = jnp.zeros_like(acc_ref)

        acc_ref[...] += _dot_tn(a_ref[...], b_ref[...].astype(BF16))

        @pl.when(kt == pl.num_programs(0) - 1)
        def _():
            for j in range(NDEV):
                cc, xy = _slot(j)
                o_ref[cc, xy] = acc_ref[j * 128:(j + 1) * 128]

    return pl.pallas_call(
        body, name=name, grid=(t // tk,),
        in_specs=[pl.BlockSpec((tk, D), lambda k: (k, 0)), pl.BlockSpec((tk, D), lambda k: (k, 0))],
        out_specs=pl.BlockSpec((2, 4, 128, D), lambda k: (0, 0, 0, 0)),
        out_shape=jax.ShapeDtypeStruct((2, 4, 128, D), F32),
        scratch_shapes=[pltpu.VMEM((D, D), F32)],
        compiler_params=_cparams(1, VMEM_BIG),
    )(a, b)


def wgrad_pool(p, dpw, tk):
    t = p.shape[1]

    def body(a_ref, b_ref, o_ref):
        @pl.when(pl.program_id(0) == 0)
        def _():
            o_ref[...] = jnp.zeros_like(o_ref)

        for g in range(NG):
            o_ref[g] += _dot_tn(a_ref[g], b_ref[g])

    return pl.pallas_call(
        body, name="wgrad_pool", grid=(t // tk,),
        in_specs=[pl.BlockSpec((NG, tk, CG), lambda k: (0, k, 0))] * 2,
        out_specs=pl.BlockSpec((NG, CG, CG), lambda k: (0, 0, 0)),
        out_shape=jax.ShapeDtypeStruct((NG, CG, CG), F32),
        compiler_params=_cparams(1, VMEM_BIG),
    )(p, dpw)


def _adamw(w, g, m, v):
    m = ADAM_B1 * m + (1.0 - ADAM_B1) * g
    v = ADAM_B2 * v + (1.0 - ADAM_B2) * (g * g)
    m_hat = m / (1.0 - ADAM_B1 ** ADAM_STEP)
    v_hat = v / (1.0 - ADAM_B2 ** ADAM_STEP)
    delta = -ADAM_LR * (m_hat / (jnp.sqrt(v_hat) + ADAM_EPS) + ADAM_WD * w)
    return delta, m, v


def _row_block(r):
    return 512 if r % 512 == 0 else r


def chip_partial(place, g, from_sibling, name):
    _, _, r, c = g.shape

    def body(place_ref, g_ref, s_ref, o_ref):
        o_ref[...] = (g_ref[...] + s_ref[...]).astype(BF16)

    return pl.pallas_call(
        body, name=name,
        grid_spec=pltpu.PrefetchScalarGridSpec(
            num_scalar_prefetch=1, grid=(3,),
            in_specs=[pl.BlockSpec((None, None, r, c), lambda k, pr: (pr[0], pr[1] ^ (k + 1), 0, 0)),
                      pl.BlockSpec((None, r, c), lambda k, pr: (pr[1] ^ (k + 1), 0, 0))],
            out_specs=pl.BlockSpec((None, r, c), lambda k, pr: (pr[1] ^ (k + 1), 0, 0))),
        out_shape=jax.ShapeDtypeStruct((4, r, c), BF16),
        compiler_params=_cparams(1, VMEM_BIG),
    )(place, g, from_sibling)


def finish_adamw(place, gs, from_sibling, from_chips, w, m, v, name):
    n = len(gs)
    r = gs[0].shape[2]
    widths = [g.shape[3] for g in gs]
    c = sum(widths)
    br = _row_block(r)

    def body(place_ref, *refs):
        g_refs, s_refs, c_refs = refs[:n], refs[n:2 * n], refs[2 * n:5 * n]
        w_ref, m_ref, v_ref, og_ref, od_ref, om_ref, ov_ref = refs[5 * n:]
        cols = []
        for q in range(n):
            grad = g_refs[q][...] + s_refs[q][...]
            for k in range(3):
                grad = grad + c_refs[3 * q + k][...].astype(F32)
            cols.append(grad)
        grad = cols[0] if n == 1 else jnp.concatenate(cols, axis=1)
        og_ref[...] = grad
        od_ref[...], om_ref[...], ov_ref[...] = _adamw(w_ref[...], grad, m_ref[...], v_ref[...])

    def other(k, cq):
        return pl.BlockSpec((None, br, cq), lambda i, pr: (pr[1] ^ k, i, 0))

    row = pl.BlockSpec((br, c), lambda i, pr: (i, 0))
    out = jax.ShapeDtypeStruct((r, c), F32)
    in_specs = [pl.BlockSpec((None, None, br, cq), lambda i, pr: (pr[0], pr[1], i, 0)) for cq in widths]
    in_specs += [pl.BlockSpec((None, br, cq), lambda i, pr: (pr[1], i, 0)) for cq in widths]
    in_specs += [other(k, cq) for cq in widths for k in (1, 2, 3)]
    return pl.pallas_call(
        body, name=name,
        grid_spec=pltpu.PrefetchScalarGridSpec(
            num_scalar_prefetch=1, grid=(r // br,), in_specs=in_specs + [row, row, row], out_specs=[row] * 4),
        out_shape=[out] * 4,
        compiler_params=_cparams(1, VMEM_BIG),
    )(place, *gs, *from_sibling, *[fc for fc in from_chips for _ in range(3)], w, m, v)


def adamw_small(items):
    n = len(items)

    def body(*refs):
        ins, outs = refs[:4 * n], refs[4 * n:]
        for i in range(n):
            w, g, m, v = (r[...] for r in ins[4 * i:4 * i + 4])
            outs[3 * i][...], outs[3 * i + 1][...], outs[3 * i + 2][...] = _adamw(w, g, m, v)

    out = [jax.ShapeDtypeStruct(it[0].shape, F32) for it in items for _ in range(3)]
    res = pl.pallas_call(body, name="adamw_small", out_shape=out)(*[a for it in items for a in it])
    return [res[3 * i:3 * i + 3] for i in range(n)]


def kernel(x, norm_mix, w_in, pool_w, pool_scale, w_pool_proj, conv_w, w_conv_out, w_o, norm_ffn, w_up, ffn_conv_w, ffn_conv_b, w_down, norm_final, loss_target, m_norm_mix, m_w_in, m_pool_w, m_pool_scale, m_w_pool_proj, m_conv_w, m_w_conv_out, m_w_o, m_norm_ffn, m_w_up, m_ffn_conv_w, m_ffn_conv_b, m_w_down, m_norm_final, v_norm_mix, v_w_in, v_pool_w, v_pool_scale, v_w_pool_proj, v_conv_w, v_w_conv_out, v_w_o, v_norm_ffn, v_w_up, v_ffn_conv_w, v_ffn_conv_b, v_w_down, v_norm_final):
    nb, seq, _ = x.shape
    t = nb * seq
    tm_in = min(TM_IN, t)
    tm_mix = min(TM_MIX, seq)
    tm_ffn = min(TM_FFN, seq)
    tk = min(TK_WGRAD, t)
    xt = x.reshape(t, D)
    tgt = loss_target.reshape(t, D)
    xi, yi, ci = _pos()
    me = 4 * xi + 2 * yi + ci
    place = jnp.stack([ci, 2 * xi + yi]).astype(jnp.int32)

    tie = lax.optimization_barrier
    w_in_g, = all_gather_blocks([w_in[0].astype(BF16)], "all_gather_w_in", 0)
    taps = (jnp.pad(conv_w[0], ((0, 5), (0, D - 128))) + jnp.pad(ffn_conv_w[0], ((3, 2), (0, D - SH_UP))))
    taps_g = _exchange_small(taps, False, "all_gather_taps")
    mix_shard = jnp.concatenate(
        [w_pool_proj[0], w_conv_out[0], w_o[0], pool_w[0].reshape(NG * 32, CG)], axis=1).astype(BF16)
    mix_shard, taps_g = tie((mix_shard, taps_g))
    wmix_g, = all_gather_blocks([mix_shard], "all_gather_w_mix", 0)
    ffn_shards, w_in_g = tie(([w_up[0].astype(BF16), w_down[0].astype(BF16)], w_in_g))
    w_up_g, w_dn_g = all_gather_blocks(ffn_shards, "all_gather_w_ffn", 0)
    wmix = wmix_g.reshape(D, MIX_COLS)
    pool_w_f = wmix_g[:, :, 3 * D:].reshape(NDEV, NG, 32, CG).transpose(1, 0, 2, 3).reshape(NG, CG, CG)
    w_dn_f = w_dn_g.reshape(NCH, SH_UP, D)
    conv_w_f = taps_g[:, 0:3, :128].transpose(1, 0, 2).reshape(3, D)
    fcw_f = taps_g[:, 3:6, :SH_UP]
    fcb_f = ffn_conv_b.reshape(NDEV, 1, SH_UP)
    gfin = norm_final.reshape(1, D)

    z, h1 = fwd_in(xt, norm_mix, w_in_g, tm_in)
    x1, y_pool, y_conv = fwd_mix(z, xt, pool_w_f, pool_scale, conv_w_f, wmix, tm_mix, seq)
    up, pre, act, h2, dx2, ffn_vec = fwd_ffn(x1, norm_ffn, w_up_g, fcw_f, fcb_f, w_dn_f, gfin, tgt, tm_ffn, seq,
                                             FFN_CHUNKS_PER_STEP)

    def to_sibling(full, tag):
        return reduce_scatter_d2d(full, "reduce_scatter_d2d_" + tag, 1)

    def partials(full, from_sib, names):
        return [chip_partial(place, g, s, "chip_partial_" + nm) for g, s, nm in zip(full, from_sib, names)]

    def to_chips(parts, tag):
        return reduce_scatter_ici(parts, "reduce_scatter_ici_" + tag, 2)

    def finish(nm, gs, from_sib, from_chips, wmv):
        w, m, v = wmv
        rc = (gs[0].shape[2], sum(g.shape[3] for g in gs))
        outs = finish_adamw(place, gs, from_sib, from_chips, w.reshape(rc), m.reshape(rc), v.reshape(rc), "adamw_" + nm)
        return [o.reshape(w.shape) for o in outs]

    def after(x, dep):
        return tie((x, dep))[0]

    big = {}
    gw_dn = wgrad_down(act, dx2, tk)
    sib_dn = to_sibling([gw_dn], "w_down")
    d_up, dx1, g_ffn_vec, g_nffn = bwd_ffn(dx2, x1, norm_ffn, up, pre, w_up_g, fcw_f, w_dn_f, tm_ffn, seq,
                                           FFN_CHUNKS_PER_STEP)
    d_up, part_dn = tie((d_up, partials([gw_dn], sib_dn, ["w_down"])))
    chips_dn = to_chips(part_dn, "w_down")
    gw_up = wgrad_cols(h2, d_up.reshape(NDEV, 1, t, SH_UP), 0, "wgrad_up", tk)
    sib_up = to_sibling([after(gw_up, chips_dn)], "w_up")
    dz, merged, p2, u, dyp, dyc, p, dpw, g_mix_vec = bwd_mix(
        dx1, z, y_pool, y_conv, pool_w_f, pool_scale, conv_w_f, wmix, tm_mix, seq)
    merged, part_up = tie((merged, partials([gw_up], sib_up, ["w_up"])))
    chips_up = to_chips(part_up, "w_up")
    gw_o = wgrad_square(merged, dx1, "wgrad_o", tk)
    gw_pp = wgrad_square(p2, dyp, "wgrad_pool_proj", tk)
    gw_co = wgrad_square(u, dyc, "wgrad_conv_out", tk)
    gw_pool = wgrad_pool(p, dpw, tk).reshape(NG, 4, 2, 32, CG).transpose(2, 1, 0, 3, 4).reshape(2, 4, NG * 32, CG)
    dz8 = dz.reshape(NDEV, 3, t, CG)
    gw_in, sib_in, chips_in = [None] * 3, [None] * 3, [None] * 3
    gw_in[0] = wgrad_cols_resident(h1, dz8, 0, "wgrad_in_0", tk)
    sib_a = to_sibling(after([gw_o, gw_pp], (chips_up, gw_in[0])), "mix_a")
    sib_b = to_sibling(after([gw_co, gw_pool], sib_a), "mix_b")
    sib_in[0] = to_sibling(after([gw_in[0]], sib_b), "w_in_0")
    gw_in[1] = wgrad_cols_resident(h1, dz8, 1, "wgrad_in_1", tk)
    sib_in[1] = to_sibling(after([gw_in[1]], sib_in[0]), "w_in_1")
    h1, part_a, part_b, part_in0 = tie((h1, partials([gw_o, gw_pp], sib_a, ["w_o", "w_pool_proj"]),
                                        partials([gw_co, gw_pool], sib_b, ["w_conv_out", "pool_w"]),
                                        partials([gw_in[0]], sib_in[0], ["w_in_0"])))
    chips_a = to_chips(after(part_a, sib_in[1]), "mix_a")
    chips_b = to_chips(part_b, "mix_b")
    chips_in[0] = to_chips(part_in0, "w_in_0")
    h1, big["w_down"], big["w_up"] = tie((
        h1, finish("w_down", [gw_dn], sib_dn, chips_dn, (w_down, m_w_down, v_w_down)),
        finish("w_up", [gw_up], sib_up, chips_up, (w_up, m_w_up, v_w_up))))
    gw_in[2] = wgrad_cols_resident(h1, dz8, 2, "wgrad_in_2", tk)
    sib_in[2] = to_sibling(after([gw_in[2]], (chips_a, chips_b, chips_in[0])), "w_in_2")
    dx1, part_in1, part_in2, big["w_o"], big["w_pool_proj"], big["w_conv_out"], big["pool_w"] = tie((
        dx1, partials([gw_in[1]], sib_in[1], ["w_in_1"]), partials([gw_in[2]], sib_in[2], ["w_in_2"]),
        finish("w_o", [gw_o], sib_a[:1], chips_a[:1], (w_o, m_w_o, v_w_o)),
        finish("w_pool_proj", [gw_pp], sib_a[1:], chips_a[1:], (w_pool_proj, m_w_pool_proj, v_w_pool_proj)),
        finish("w_conv_out", [gw_co], sib_b[:1], chips_b[:1], (w_conv_out, m_w_conv_out, v_w_conv_out)),
        finish("pool_w", [gw_pool], sib_b[1:], chips_b[1:], (pool_w, m_pool_w, v_pool_w))))
    chips_in[1] = to_chips(after(part_in1, sib_in[2]), "w_in_1")
    chips_in[2] = to_chips(part_in2, "w_in_2")
    grad_x, g_nmix = bwd_in(dz, w_in_g, dx1, xt, norm_mix, min(TM_BWD_IN, t))
    grad_x, chips_in = tie((grad_x, chips_in))
    big["w_in"] = finish("w_in", gw_in, [s[0] for s in sib_in], [c[0] for c in chips_in], (w_in, m_w_in, v_w_in))

    red = _exchange_small(
        jnp.concatenate([g_nmix, g_mix_vec, g_nffn, ffn_vec, g_ffn_vec.reshape(8 * NDEV, D)], axis=0), True,
        "all_reduce_small")
    g_norm_mix, g_pool_scale, g_norm_ffn = red[0:1], red[8:9], red[16:17]
    g_conv_w = lax.dynamic_slice(red, (9, me * 128), (3, 128))
    g_norm_final = red[24]
    loss = red[25, 0]
    g_fcb = red[32:].reshape(NDEV, 8, D)[:, 0, :SH_UP].reshape(1, FF2)
    g_fcw = lax.dynamic_slice(red, (33 + 8 * me, 0), (3, SH_UP))
    grads = {"norm_mix": g_norm_mix, "pool_scale": g_pool_scale, "norm_ffn": g_norm_ffn, "norm_final": g_norm_final,
             "ffn_conv_b": g_fcb, "conv_w": g_conv_w.reshape(1, 3, 128), "ffn_conv_w": g_fcw.reshape(1, 3, SH_UP)}
    small_wmv = {"norm_mix": (norm_mix, m_norm_mix, v_norm_mix), "pool_scale": (pool_scale, m_pool_scale, v_pool_scale),
                 "norm_ffn": (norm_ffn, m_norm_ffn, v_norm_ffn), "norm_final": (norm_final, m_norm_final, v_norm_final),
                 "ffn_conv_b": (ffn_conv_b, m_ffn_conv_b, v_ffn_conv_b), "conv_w": (conv_w, m_conv_w, v_conv_w),
                 "ffn_conv_w": (ffn_conv_w, m_ffn_conv_w, v_ffn_conv_w)}
    small_names = list(small_wmv)
    flat2 = lambda a: a.reshape(-1, a.shape[-1])
    small_out = adamw_small([(flat2(small_wmv[nm][0]), flat2(grads[nm]), flat2(small_wmv[nm][1]),
                              flat2(small_wmv[nm][2])) for nm in small_names])
    small = {nm: [o.reshape(small_wmv[nm][0].shape) for o in outs] for nm, outs in zip(small_names, small_out)}

    order = ["norm_mix", "w_in", "pool_w", "pool_scale", "w_pool_proj", "conv_w", "w_conv_out", "w_o", "norm_ffn",
             "w_up", "ffn_conv_w", "ffn_conv_b", "w_down", "norm_final"]
    out = [loss, grad_x.reshape(nb, seq, D)]
    out += [big[nm][0] if nm in big else grads[nm] for nm in order]
    for idx in range(3):
        out += [big[nm][idx + 1] if nm in big else small[nm][idx] for nm in order]
    return tuple(out)
```

```python
import functools

import jax
import jax.numpy as jnp
from jax import lax
from jax.experimental import pallas as pl
from jax.experimental.pallas import tpu as pltpu
from jax.experimental.pallas import tpu_sc as plsc

F32 = jnp.float32
BF16 = jnp.bfloat16

NDEV = 8
D = 1024
NG = 4
CG = 256
WINS = (2, 4, 8, 16)
DIN = 6 * D
SH_IN = DIN // NDEV
NZT = DIN // CG
FF2 = 5632
SH_UP = FF2 // NDEV
FF = FF2 // 2
NCH = 4
SH_DN = FF // NDEV
RMS_EPS = 1e-6
HALO = 16

ADAM_LR = 0.001
ADAM_B1 = 0.9
ADAM_B2 = 0.999
ADAM_EPS = 1e-08
ADAM_WD = 0.01
ADAM_STEP = 10

TM_IN = 512
TM_BWD_IN = 256
TM_MIX = 256
TM_FFN = 256
FFN_CHUNKS_PER_STEP = 4
TK_WGRAD = 2048
MIX_POOL_PROJ, MIX_CONV_OUT, MIX_O = 0, 1, 2
MIX_COLS = 3 * D + CG
VMEM_BIG = 56 * 1024 * 1024
MESH = pl.DeviceIdType.MESH
ANY = pl.BlockSpec(memory_space=pl.ANY)


def _cparams(n_axes, vmem=None):
    return pltpu.CompilerParams(dimension_semantics=("arbitrary",) * n_axes, vmem_limit_bytes=vmem)


def _dot(a, b):
    return jnp.dot(a, b, preferred_element_type=F32)


def _dot_nt(a, b):
    return lax.dot_general(a, b, (((1,), (1,)), ((), ())), preferred_element_type=F32)


def _dot_tn(a, b):
    return lax.dot_general(a, b, (((0,), (0,)), ((), ())), preferred_element_type=F32)


def _shift_down(ext, s, lead):
    return pltpu.roll(ext, s, 0)[lead:]


def _shift_up(ext, s, tm):
    n = ext.shape[0]
    return pltpu.roll(ext, n - s, 0)[:tm]


def _rms_inv(x):
    return lax.rsqrt(jnp.mean(x * x, axis=-1, keepdims=True) + RMS_EPS)


def _rms_bwd(dh, xn, inv, g):
    dxn = dh * g
    return inv * (dxn - xn * jnp.mean(dxn * xn, axis=-1, keepdims=True))


def _pos():
    return lax.axis_index("x"), lax.axis_index("y"), lax.axis_index("c")


def _handshake(peers):
    barrier = pltpu.get_barrier_semaphore()
    for peer in peers:
        pl.semaphore_signal(barrier, inc=1, device_id=peer, device_id_type=MESH)
    pl.semaphore_wait(barrier, len(peers))


def _sequencer(body, out_type, n_sems, name, collective_id):
    return pl.kernel(
        body, out_type=out_type, mesh=plsc.ScalarSubcoreMesh(axis_name="sequencer", num_cores=1), name=name,
        scratch_types=[pltpu.SemaphoreType.DMA((n_sems,)), pltpu.SemaphoreType.DMA((n_sems,))],
        compiler_params=pltpu.CompilerParams(collective_id=collective_id))


def all_gather_blocks(shards, name, collective_id):
    n = len(shards)

    def body(*refs):
        ins, outs = refs[:n], refs[n:2 * n]
        send_sems, recv_sems = refs[2 * n:]
        x, y, c = _pos()
        sibling = (x, y, 1 - c)
        chips = [(1 - x, y), (x, 1 - y), (1 - x, 1 - y)]
        _handshake([sibling] + [(*chip, c) for chip in chips])

        def copy(w, k, block, to, src=None):
            slot = outs[w].at[4 * block[0] + 2 * block[1] + block[2]]
            return pltpu.make_async_remote_copy(
                src_ref=slot if src is None else src, dst_ref=slot,
                send_sem=send_sems.at[8 * w + k], recv_sem=recv_sems.at[8 * w + k], device_id=to, device_id_type=MESH)

        mine, first, passed = [], [], []
        for w in range(n):
            m = pltpu.make_async_copy(ins[w], outs[w].at[4 * x + 2 * y + c], send_sems.at[8 * w + 7])
            m.start()
            mine.append(m)
            first.append(copy(w, 0, (x, y, c), sibling, src=ins[w]))
            first += [copy(w, 1 + j, (x, y, c), (*chip, c), src=ins[w]) for j, chip in enumerate(chips)]
        for cp in first:
            cp.start()
        for w in range(n):
            for j, chip in enumerate(chips):
                copy(w, 1 + j, (*chip, c), (x, y, c)).wait_recv()
                fw = copy(w, 4 + j, (*chip, c), sibling)
                fw.start()
                passed.append(fw)
        for w in range(n):
            copy(w, 0, (x, y, 1 - c), (x, y, c)).wait_recv()
            for j, chip in enumerate(chips):
                copy(w, 4 + j, (*chip, 1 - c), (x, y, c)).wait_recv()
        for cp in first + passed:
            cp.wait_send()
        for m in mine:
            m.wait()

    out = [jax.ShapeDtypeStruct((NDEV,) + s.shape, s.dtype) for s in shards]
    return _sequencer(body, out, 8 * n, name, collective_id)(*shards)


def _exchange_small(v, reduce, name, gathered=None):
    rows = v.shape[0]

    def body(*refs):
        if gathered is None:
            v_ref, out_ref, slots, send_sems, recv_sems, local_sem = refs
        else:
            v_ref, g_ref, out_ref, gsum_ref, slots, send_sems, recv_sems, local_sem = refs
        x, y, c = _pos()
        me = 4 * x + 2 * y + c
        mine = pltpu.make_async_copy(v_ref, slots.at[me], local_sem)
        mine.start()
        offs = [(dx, dy, dc) for dx in (0, 1) for dy in (0, 1) for dc in (0, 1)][1:]

        def copy(k, src_slot, to):
            return pltpu.make_async_remote_copy(
                src_ref=v_ref, dst_ref=slots.at[src_slot], send_sem=send_sems.at[k], recv_sem=recv_sems.at[k],
                device_id=to, device_id_type=MESH)

        sends = []
        for k, (dx, dy, dc) in enumerate(offs):
            cp = copy(k, me, (x ^ dx, y ^ dy, c ^ dc))
            cp.start()
            sends.append(cp)
        for k, (dx, dy, dc) in enumerate(offs):
            copy(k, 4 * (x ^ dx) + 2 * (y ^ dy) + (c ^ dc), (x, y, c)).wait_recv()
        for cp in sends:
            cp.wait_send()
        mine.wait()
        if reduce:
            acc = slots[0]
            for d in range(1, NDEV):
                acc = acc + slots[d]
            out_ref[...] = acc
        else:
            out_ref[...] = slots[...]
        if gathered is not None:
            acc = g_ref[0]
            for d in range(1, NDEV):
                acc = acc + g_ref[d]
            gsum_ref[...] = acc

    vmem = pl.BlockSpec(memory_space=pltpu.VMEM)
    out = jax.ShapeDtypeStruct((rows, D) if reduce else (NDEV, rows, D), F32)
    args, out_shape, out_specs = [v], out, vmem
    if gathered is not None:
        args.append(gathered)
        out_shape, out_specs = [out, jax.ShapeDtypeStruct(gathered.shape[1:], F32)], [vmem, vmem]
    return pl.pallas_call(
        body, name=name, out_shape=out_shape, in_specs=[vmem] * len(args), out_specs=out_specs,
        scratch_shapes=[pltpu.VMEM((NDEV, rows, D), F32), pltpu.SemaphoreType.DMA((7,)),
                        pltpu.SemaphoreType.DMA((7,)), pltpu.SemaphoreType.DMA],
    )(*args)


def reduce_scatter_d2d(grads, name, collective_id):
    n = len(grads)

    def body(*refs):
        ins, outs = refs[:n], refs[n:2 * n]
        send_sems, recv_sems = refs[2 * n:]
        x, y, c = _pos()
        _handshake([(x, y, 1 - c)])
        cps = []
        for w in range(n):
            cp = pltpu.make_async_remote_copy(
                src_ref=ins[w].at[1 - c], dst_ref=outs[w], send_sem=send_sems.at[w], recv_sem=recv_sems.at[w],
                device_id=(x, y, 1 - c), device_id_type=MESH)
            cp.start()
            cps.append(cp)
        for cp in cps:
            cp.wait_recv()
        for cp in cps:
            cp.wait_send()

    out = [jax.ShapeDtypeStruct(g.shape[1:], F32) for g in grads]
    return _sequencer(body, out, n, name, collective_id)(*grads)


def reduce_scatter_ici(parts, name, collective_id):
    n = len(parts)

    def body(*refs):
        ins, outs = refs[:n], refs[n:2 * n]
        send_sems, recv_sems = refs[2 * n:]
        x, y, c = _pos()
        offs = [(1, 0), (0, 1), (1, 1)]
        _handshake([(x ^ dx, y ^ dy, c) for dx, dy in offs])
        cps = []
        for w in range(n):
            for k, (dx, dy) in enumerate(offs):
                ox, oy = x ^ dx, y ^ dy
                cp = pltpu.make_async_remote_copy(
                    src_ref=ins[w].at[2 * ox + oy], dst_ref=outs[w].at[2 * x + y],
                    send_sem=send_sems.at[3 * w + k], recv_sem=recv_sems.at[3 * w + k],
                    device_id=(ox, oy, c), device_id_type=MESH)
                cp.start()
                cps.append((cp, w, k, ox, oy))
        for cp, w, k, ox, oy in cps:
            pltpu.make_async_remote_copy(
                src_ref=ins[w].at[2 * ox + oy], dst_ref=outs[w].at[2 * ox + oy],
                send_sem=send_sems.at[3 * w + k], recv_sem=recv_sems.at[3 * w + k],
                device_id=(ox, oy, c), device_id_type=MESH).wait_recv()
        for cp, *_ in cps:
            cp.wait_send()

    out = [jax.ShapeDtypeStruct(p.shape, BF16) for p in parts]
    return _sequencer(body, out, 3 * n, name, collective_id)(*parts)


def fwd_in(x, g1, w_in_g, tm):
    t = x.shape[0]

    def body(x_ref, g_ref, w_ref, z_ref, ht_ref):
        xf = x_ref[...]
        h = (xf * _rms_inv(xf) * g_ref[...]).astype(BF16)
        ht_ref[...] = h.T
        for j in range(NDEV):
            r = _dot(h, w_ref[j])
            for q in range(3):
                z_ref[3 * j + q] = r[:, q * CG:(q + 1) * CG].astype(BF16)

    return pl.pallas_call(
        body, name="fwd_in", grid=(t // tm,),
        in_specs=[pl.BlockSpec((tm, D), lambda i: (i, 0)), pl.BlockSpec((1, D), lambda i: (0, 0)),
                  pl.BlockSpec((NDEV, D, SH_IN), lambda i: (0, 0, 0))],
        out_specs=[pl.BlockSpec((NZT, tm, CG), lambda i: (0, i, 0)), pl.BlockSpec((D, tm), lambda i: (0, i))],
        out_shape=[jax.ShapeDtypeStruct((NZT, t, CG), BF16), jax.ShapeDtypeStruct((D, t), BF16)],
        compiler_params=_cparams(1, VMEM_BIG),
    )(x, g1, w_in_g)


def _pool_tile(z_ref, zh_ref, g, win, keep_hist, cnt):
    zt = z_ref[g].astype(F32)
    ext = jnp.concatenate([zh_ref[g].astype(F32) * keep_hist, zt], axis=0)
    s, sh = ext, 1
    while sh < win:
        s = s + pltpu.roll(s, sh, 0)
        sh *= 2
    return s[HALO:] / cnt - zt


def _conv_taps(ext, cur, w_ref, lanes, lead):
    x1 = _shift_down(ext, 1, lead)
    x2 = _shift_down(ext, 2, lead)
    out = w_ref[2:3, lanes] * cur + w_ref[1:2, lanes] * x1 + w_ref[0:1, lanes] * x2
    return out, x1, x2


def fwd_mix(z, x, pool_w, pool_scale, conv_w, wmix, tm, seq):
    t = x.shape[0]
    tps = seq // tm
    hb = tm // HALO

    def body(z_ref, zph_ref, zcvh_ref, x_ref, pw_ref, ps_ref, wpp_ref, cw_ref, wco_ref, wo_ref,
             x1_ref, yp_ref, yc_ref):
        it = pl.program_id(0) % tps
        keep_hist = jnp.where(it == 0, 0.0, 1.0)
        pos = it * tm + lax.broadcasted_iota(jnp.int32, (tm, 1), 0)
        p2 = []
        for g, win in enumerate(WINS):
            cnt = jnp.minimum(pos + 1, win).astype(F32)
            p = _pool_tile(z_ref, zph_ref, g, win, keep_hist, cnt)
            lanes = slice(g * CG, (g + 1) * CG)
            p2.append((_dot(p.astype(BF16), pw_ref[g]) * ps_ref[:, lanes]).astype(BF16))
        y_pool = _dot(jnp.concatenate(p2, axis=1), wpp_ref[...])
        u = []
        for q in range(NG):
            lanes = slice(q * CG, (q + 1) * CG)
            cv = z_ref[8 + q].astype(F32) * z_ref[12 + q].astype(F32)
            cvh = zcvh_ref[q].astype(F32) * zcvh_ref[4 + q].astype(F32) * keep_hist
            cc, _, _ = _conv_taps(jnp.concatenate([cvh, cv], axis=0), cv, cw_ref, lanes, HALO)
            u.append((z_ref[4 + q].astype(F32) * cc).astype(BF16))
        y_conv = _dot(jnp.concatenate(u, axis=1), wco_ref[...])
        ypb, ycb = y_pool.astype(BF16), y_conv.astype(BF16)
        yp_ref[...] = ypb
        yc_ref[...] = ycb
        merged = []
        for q in range(NG):
            lanes = slice(q * CG, (q + 1) * CG)
            sp = jax.nn.sigmoid(z_ref[16 + q].astype(F32))
            sc = jax.nn.sigmoid(z_ref[20 + q].astype(F32))
            merged.append((sp * ypb[:, lanes].astype(F32) + sc * ycb[:, lanes].astype(F32)).astype(BF16))
        x1_ref[...] = x_ref[...] + _dot(jnp.concatenate(merged, axis=1), wo_ref[...])

    def hist(i):
        return jnp.maximum(i * hb - 1, 0)

    const2 = lambda i: (0, 0)
    return pl.pallas_call(
        body, name="fwd_mix", grid=(t // tm,),
        in_specs=[pl.BlockSpec((NZT, tm, CG), lambda i: (0, i, 0)),
                  pl.BlockSpec((NG, HALO, CG), lambda i: (0, hist(i), 0)),
                  pl.BlockSpec((2 * NG, HALO, CG), lambda i: (1, hist(i), 0)),
                  pl.BlockSpec((tm, D), lambda i: (i, 0)),
                  pl.BlockSpec((NG, CG, CG), lambda i: (0, 0, 0)), pl.BlockSpec((1, D), const2),
                  pl.BlockSpec((D, D), lambda i: (0, MIX_POOL_PROJ)), pl.BlockSpec((3, D), const2),
                  pl.BlockSpec((D, D), lambda i: (0, MIX_CONV_OUT)), pl.BlockSpec((D, D), lambda i: (0, MIX_O))],
        out_specs=[pl.BlockSpec((tm, D), lambda i: (i, 0))] * 3,
        out_shape=[jax.ShapeDtypeStruct((t, D), F32), jax.ShapeDtypeStruct((t, D), BF16),
                   jax.ShapeDtypeStruct((t, D), BF16)],
        compiler_params=_cparams(1, VMEM_BIG),
    )(z, z, z, x, pool_w, pool_scale, wmix, conv_w, wmix, wmix)


def fwd_ffn(x1, g2, w_up_g, fcw, fcb, w_dn, gf, tgt, tm, seq, cps):
    t = x1.shape[0]
    tps = seq // tm

    def body(x1_ref, g2_ref, wup_ref, fcw_ref, fcb_ref, wdn_ref, gf_ref, tgt_ref,
             up_ref, pre_ref, actt_ref, h2t_ref, dx2_ref, vec_ref, hist_ref, d_ref, h2_ref):
        i, k = pl.program_id(0), pl.program_id(1)
        keep_hist = jnp.where(i % tps == 0, 0.0, 1.0)

        @pl.when((i == 0) & (k == 0))
        def _():
            vec_ref[...] = jnp.zeros_like(vec_ref)
            hist_ref[...] = jnp.zeros_like(hist_ref)

        @pl.when(k == 0)
        def _():
            x1v = x1_ref[...]
            h2 = (x1v * _rms_inv(x1v) * g2_ref[...]).astype(BF16)
            h2_ref[...] = h2
            h2t_ref[...] = h2.T
            d_ref[...] = jnp.zeros_like(d_ref)

        h2 = h2_ref[...]
        lanes = slice(0, SH_UP)
        d = d_ref[...]
        for c in range(cps):
            kc = k * cps + c
            conv = []
            for s in range(2):
                ub = _dot(h2, wup_ref[s, c]).astype(BF16)
                up_ref[s, c] = ub
                uf = ub.astype(F32)
                ext = jnp.concatenate([hist_ref[s, kc] * keep_hist, uf], axis=0)
                hist_ref[s, kc] = uf[tm - 8:]
                cc, _, _ = _conv_taps(ext, uf, fcw_ref.at[s, c], lanes, 8)
                conv.append(cc + fcb_ref[s, c])
                pre_ref[s, c] = conv[s].astype(BF16)
            a = (conv[0] * jax.nn.sigmoid(conv[0]) * conv[1]).astype(BF16)
            actt_ref[c] = a.T
            d = d + _dot(a, wdn_ref[c])
        d_ref[...] = d

        @pl.when(k == NCH // cps - 1)
        def _():
            x2 = x1_ref[...] + d_ref[...]
            inv3 = _rms_inv(x2)
            xn = x2 * inv3
            diff = xn * gf_ref[...] - tgt_ref[...]
            dy = diff * (1.0 / D)
            vec_ref[0:1, :] += jnp.sum(dy * xn, axis=0, keepdims=True)
            vec_ref[1:2, :] += 0.5 * jnp.sum(jnp.mean(diff * diff, axis=-1))
            dx2_ref[...] = _rms_bwd(dy, xn, inv3, gf_ref[...])

    tile = lambda i, k: (i, 0)
    const2 = lambda i, k: (0, 0)
    pair = lambda i, k: (0, k, 0, 0)
    return pl.pallas_call(
        body, name="fwd_ffn", grid=(t // tm, NCH // cps),
        in_specs=[pl.BlockSpec((tm, D), tile), pl.BlockSpec((1, D), const2),
                  pl.BlockSpec((2, cps, D, SH_UP), pair), pl.BlockSpec((2, cps, 3, SH_UP), pair),
                  pl.BlockSpec((2, cps, 1, SH_UP), pair), pl.BlockSpec((cps, SH_UP, D), lambda i, k: (k, 0, 0)),
                  pl.BlockSpec((1, D), const2), pl.BlockSpec((tm, D), tile)],
        out_specs=[pl.BlockSpec((2, cps, tm, SH_UP), lambda i, k: (0, k, i, 0)),
                   pl.BlockSpec((2, cps, tm, SH_UP), lambda i, k: (0, k, i, 0)),
                   pl.BlockSpec((cps, SH_UP, tm), lambda i, k: (k, 0, i)),
                   pl.BlockSpec((D, tm), lambda i, k: (0, i)), pl.BlockSpec((tm, D), tile), pl.BlockSpec((8, D), const2)],
        out_shape=[jax.ShapeDtypeStruct((2, NCH, t, SH_UP), BF16), jax.ShapeDtypeStruct((2, NCH, t, SH_UP), BF16),
                   jax.ShapeDtypeStruct((NCH, SH_UP, t), BF16),
                   jax.ShapeDtypeStruct((D, t), BF16), jax.ShapeDtypeStruct((t, D), F32),
                   jax.ShapeDtypeStruct((8, D), F32)],
        scratch_shapes=[pltpu.VMEM((2, NCH, 8, SH_UP), F32), pltpu.VMEM((tm, D), F32), pltpu.VMEM((tm, D), BF16)],
        compiler_params=_cparams(2, VMEM_BIG),
    )(x1, g2, w_up_g.reshape(2, NCH, D, SH_UP), fcw.reshape(2, NCH, 3, SH_UP), fcb.reshape(2, NCH, 1, SH_UP),
      w_dn, gf, tgt)


def bwd_ffn(dx2, x1, g2, up, pre, w_up_g, fcw, w_dn, tm, seq, cps):
    t = x1.shape[0]
    nt = t // tm
    tps = seq // tm

    def body(dx2_ref, x1_ref, g2_ref, up_ref, pre_ref, wup_ref, fcw_ref, wdn_ref,
             dup_ref, dx1_ref, gvec_ref, gn_ref, carry_ref, dh2_ref, acc_ref):
        i, k = pl.program_id(0), pl.program_id(1)
        it = (nt - 1 - i) % tps
        keep_next = jnp.where(it == tps - 1, 0.0, 1.0)

        @pl.when((i == 0) & (k == 0))
        def _():
            acc_ref[...] = jnp.zeros_like(acc_ref)
            gn_ref[...] = jnp.zeros_like(gn_ref)
            carry_ref[...] = jnp.zeros_like(carry_ref)

        @pl.when(k == 0)
        def _():
            dh2_ref[...] = jnp.zeros_like(dh2_ref)

        dxb = dx2_ref[...].astype(BF16)
        lanes = slice(0, SH_UP)
        dh2 = dh2_ref[...]
        for c in range(cps):
            kc = k * cps + c
            pre = [pre_ref[s, c].astype(F32) for s in range(2)]
            sg = jax.nn.sigmoid(pre[0])
            dact = _dot_nt(dxb, wdn_ref[c])
            dpre = [dact * pre[1] * (sg * (1.0 + pre[0] * (1.0 - sg))), dact * (pre[0] * sg)]
            for s in range(2):
                dc = dpre[s]
                ext = jnp.concatenate([dc, carry_ref[s, kc] * keep_next], axis=0)
                carry_ref[s, kc] = dc[:8]
                shifted = (_shift_up(ext, 2, tm), _shift_up(ext, 1, tm), dc)
                uf = up_ref[s, c].astype(F32)
                acc_ref[s, kc, 0:1, lanes] += jnp.sum(dc, axis=0, keepdims=True)
                for tap in range(3):
                    acc_ref[s, kc, tap + 1:tap + 2, lanes] += jnp.sum(shifted[tap] * uf, axis=0, keepdims=True)
                w = fcw_ref.at[s, c]
                du = w[2:3, :] * dc + w[1:2, :] * shifted[1] + w[0:1, :] * shifted[0]
                dub = du.astype(BF16)
                dup_ref[s, c] = dub
                dh2 = dh2 + _dot_nt(dub, wup_ref[s, c])
        dh2_ref[...] = dh2

        @pl.when(k == NCH // cps - 1)
        def _():
            x1v = x1_ref[...]
            inv2 = _rms_inv(x1v)
            xn = x1v * inv2
            gn_ref[0:1, :] += jnp.sum(dh2 * xn, axis=0, keepdims=True)
            dx1_ref[...] = dx2_ref[...] + _rms_bwd(dh2, xn, inv2, g2_ref[...])

        @pl.when((i == nt - 1) & (k == NCH // cps - 1))
        def _():
            gvec_ref[...] = acc_ref[...]

    rev = lambda i, k: (nt - 1 - i, 0)
    const2 = lambda i, k: (0, 0)
    pair = lambda i, k: (0, k, 0, 0)
    return pl.pallas_call(
        body, name="bwd_ffn", grid=(nt, NCH // cps),
        in_specs=[pl.BlockSpec((tm, D), rev), pl.BlockSpec((tm, D), rev), pl.BlockSpec((1, D), const2),
                  pl.BlockSpec((2, cps, tm, SH_UP), lambda i, k: (0, k, nt - 1 - i, 0)),
                  pl.BlockSpec((2, cps, tm, SH_UP), lambda i, k: (0, k, nt - 1 - i, 0)),
                  pl.BlockSpec((2, cps, D, SH_UP), pair), pl.BlockSpec((2, cps, 3, SH_UP), pair),
                  pl.BlockSpec((cps, SH_UP, D), lambda i, k: (k, 0, 0))],
        out_specs=[pl.BlockSpec((2, cps, tm, SH_UP), lambda i, k: (0, k, nt - 1 - i, 0)), pl.BlockSpec((tm, D), rev),
                   pl.BlockSpec((2, NCH, 8, D), lambda i, k: (0, 0, 0, 0)), pl.BlockSpec((8, D), const2)],
        out_shape=[jax.ShapeDtypeStruct((2, NCH, t, SH_UP), BF16), jax.ShapeDtypeStruct((t, D), F32),
                   jax.ShapeDtypeStruct((2, NCH, 8, D), F32), jax.ShapeDtypeStruct((8, D), F32)],
        scratch_shapes=[pltpu.VMEM((2, NCH, 8, SH_UP), F32), pltpu.VMEM((tm, D), F32),
                        pltpu.VMEM((2, NCH, 8, D), F32)],
        compiler_params=_cparams(2, VMEM_BIG),
    )(dx2, x1, g2, up, pre, w_up_g.reshape(2, NCH, D, SH_UP), fcw.reshape(2, NCH, 3, SH_UP), w_dn)


def bwd_mix(dx1, z, y_pool, y_conv, pool_w, pool_scale, conv_w, wmix, tm, seq):
    t = dx1.shape[0]
    nt = t // tm
    tps = seq // tm
    hb = tm // HALO

    def body(da_ref, z_ref, zph_ref, zcvh_ref, yp_ref, yc_ref, pw_ref, ps_ref, wpp_ref, cw_ref, wco_ref, wo_ref,
             dz_ref, mg_ref, p2_ref, u_ref, dyp_ref, dyc_ref, p_ref, dpw_ref, gvec_ref, cp_ref, cc_ref):
        i = pl.program_id(0)
        it = (nt - 1 - i) % tps
        keep_hist = jnp.where(it == 0, 0.0, 1.0)
        keep_next = jnp.where(it == tps - 1, 0.0, 1.0)
        pos = it * tm + lax.broadcasted_iota(jnp.int32, (tm, 1), 0)

        @pl.when(i == 0)
        def _():
            gvec_ref[...] = jnp.zeros_like(gvec_ref)
            cp_ref[...] = jnp.zeros_like(cp_ref)
            cc_ref[...] = jnp.zeros_like(cc_ref)

        dm = _dot_nt(da_ref[...].astype(BF16), wo_ref[...])
        merged, dyp, dyc = [], [], []
        for q in range(NG):
            lanes = slice(q * CG, (q + 1) * CG)
            sp = jax.nn.sigmoid(z_ref[16 + q].astype(F32))
            sc = jax.nn.sigmoid(z_ref[20 + q].astype(F32))
            yp = yp_ref[:, lanes].astype(F32)
            yc = yc_ref[:, lanes].astype(F32)
            dmq = dm[:, lanes]
            merged.append((sp * yp + sc * yc).astype(BF16))
            dyp.append((dmq * sp).astype(BF16))
            dyc.append((dmq * sc).astype(BF16))
            dz_ref[16 + q] = (dmq * yp * (sp * (1.0 - sp))).astype(BF16)
            dz_ref[20 + q] = (dmq * yc * (sc * (1.0 - sc))).astype(BF16)
        mg_ref[...] = jnp.concatenate(merged, axis=1)
        dypb = jnp.concatenate(dyp, axis=1)
        dycb = jnp.concatenate(dyc, axis=1)
        dyp_ref[...] = dypb
        dyc_ref[...] = dycb

        dp2 = _dot_nt(dypb, wpp_ref[...])
        p2 = []
        for g, win in enumerate(WINS):
            lanes = slice(g * CG, (g + 1) * CG)
            cnt = jnp.minimum(pos + 1, win).astype(F32)
            p = _pool_tile(z_ref, zph_ref, g, win, keep_hist, cnt)
            pb = p.astype(BF16)
            p_ref[g] = pb
            pw = _dot(pb, pw_ref[g])
            p2.append((pw * ps_ref[:, lanes]).astype(BF16))
            dp2g = dp2[:, lanes]
            gvec_ref[0:1, lanes] += jnp.sum(dp2g * pw, axis=0, keepdims=True)
            dpwb = (dp2g * ps_ref[:, lanes]).astype(BF16)
            dpw_ref[g] = dpwb
            dp = _dot_nt(dpwb, pw_ref[g])
            qv = dp / cnt
            ext = jnp.concatenate([qv, cp_ref[g] * keep_next], axis=0)
            cp_ref[g] = qv[:HALO]
            n = tm + HALO
            s, sh = ext, 1
            while sh < win:
                s = s + pltpu.roll(s, n - sh, 0)
                sh *= 2
            dz_ref[g] = (s[:tm] - dp).astype(BF16)
        p2_ref[...] = jnp.concatenate(p2, axis=1)

        du = _dot_nt(dycb, wco_ref[...])
        u = []
        for q in range(NG):
            lanes = slice(q * CG, (q + 1) * CG)
            zb = z_ref[4 + q].astype(F32)
            zc = z_ref[8 + q].astype(F32)
            zv = z_ref[12 + q].astype(F32)
            cv = zc * zv
            cvh = zcvh_ref[q].astype(F32) * zcvh_ref[4 + q].astype(F32) * keep_hist
            cc, cv1, cv2 = _conv_taps(jnp.concatenate([cvh, cv], axis=0), cv, cw_ref, lanes, HALO)
            u.append((zb * cc).astype(BF16))
            duq = du[:, lanes]
            dz_ref[4 + q] = (duq * cc).astype(BF16)
            dcc = duq * zb
            for tap, src in enumerate((cv2, cv1, cv)):
                gvec_ref[tap + 1:tap + 2, lanes] += jnp.sum(dcc * src, axis=0, keepdims=True)
            ext = jnp.concatenate([dcc, cc_ref[:, lanes] * keep_next], axis=0)
            cc_ref[:, lanes] = dcc[:8]
            dcv = (cw_ref[2:3, lanes] * dcc + cw_ref[1:2, lanes] * _shift_up(ext, 1, tm)
                   + cw_ref[0:1, lanes] * _shift_up(ext, 2, tm))
            dz_ref[8 + q] = (dcv * zv).astype(BF16)
            dz_ref[12 + q] = (dcv * zc).astype(BF16)
        u_ref[...] = jnp.concatenate(u, axis=1)

    def hist(i):
        return jnp.maximum((nt - 1 - i) * hb - 1, 0)

    rev = lambda i: (nt - 1 - i, 0)
    rev3 = lambda i: (0, nt - 1 - i, 0)
    const2 = lambda i: (0, 0)
    tok = jax.ShapeDtypeStruct((t, D), BF16)
    grp = jax.ShapeDtypeStruct((NG, t, CG), BF16)
    return pl.pallas_call(
        body, name="bwd_mix", grid=(nt,),
        in_specs=[pl.BlockSpec((tm, D), rev), pl.BlockSpec((NZT, tm, CG), rev3),
                  pl.BlockSpec((NG, HALO, CG), lambda i: (0, hist(i), 0)),
                  pl.BlockSpec((2 * NG, HALO, CG), lambda i: (1, hist(i), 0)),
                  pl.BlockSpec((tm, D), rev), pl.BlockSpec((tm, D), rev),
                  pl.BlockSpec((NG, CG, CG), lambda i: (0, 0, 0)), pl.BlockSpec((1, D), const2),
                  pl.BlockSpec((D, D), lambda i: (0, MIX_POOL_PROJ)), pl.BlockSpec((3, D), const2),
                  pl.BlockSpec((D, D), lambda i: (0, MIX_CONV_OUT)), pl.BlockSpec((D, D), lambda i: (0, MIX_O))],
        out_specs=[pl.BlockSpec((NZT, tm, CG), rev3)] + [pl.BlockSpec((tm, D), rev)] * 5
                  + [pl.BlockSpec((NG, tm, CG), rev3)] * 2 + [pl.BlockSpec((8, D), const2)],
        out_shape=[jax.ShapeDtypeStruct((NZT, t, CG), BF16), tok, tok, tok, tok, tok, grp, grp,
                   jax.ShapeDtypeStruct((8, D), F32)],
        scratch_shapes=[pltpu.VMEM((NG, HALO, CG), F32), pltpu.VMEM((8, D), F32)],
        compiler_params=_cparams(1, VMEM_BIG),
    )(dx1, z, z, z, y_pool, y_conv, pool_w, pool_scale, wmix, conv_w, wmix, wmix)


def bwd_in(dz, w_in_g, dx1, x, g1, tm):
    t = x.shape[0]

    def body(dz_ref, w_ref, dx1_ref, x_ref, g_ref, gx_ref, gn_ref):
        @pl.when(pl.program_id(0) == 0)
        def _():
            gn_ref[...] = jnp.zeros_like(gn_ref)

        dh = None
        for j in range(NDEV):
            dzc = jnp.concatenate([dz_ref[3 * j + q] for q in range(3)], axis=1)
            part = _dot_nt(dzc, w_ref[j])
            dh = part if dh is None else dh + part
        xv = x_ref[...]
        inv = _rms_inv(xv)
        xn = xv * inv
        gn_ref[0:1, :] += jnp.sum(dh * xn, axis=0, keepdims=True)
        gx_ref[...] = dx1_ref[...] + _rms_bwd(dh, xn, inv, g_ref[...])

    tile = lambda i: (i, 0)
    return pl.pallas_call(
        body, name="bwd_in", grid=(t // tm,),
        in_specs=[pl.BlockSpec((NZT, tm, CG), lambda i: (0, i, 0)),
                  pl.BlockSpec((NDEV, D, SH_IN), lambda i: (0, 0, 0)),
                  pl.BlockSpec((tm, D), tile), pl.BlockSpec((tm, D), tile), pl.BlockSpec((1, D), lambda i: (0, 0))],
        out_specs=[pl.BlockSpec((tm, D), tile), pl.BlockSpec((8, D), lambda i: (0, 0))],
        out_shape=[jax.ShapeDtypeStruct((t, D), F32), jax.ShapeDtypeStruct((8, D), F32)],
        compiler_params=_cparams(1, VMEM_BIG),
    )(dz, w_in_g, dx1, x, g1)


def _slot(j):
    return j % 2, j // 2


def wgrad_cols(at, b, q, name, tk):
    m, t = at.shape
    width = b.shape[3]

    def body(a_ref, b_ref, o_ref):
        @pl.when(pl.program_id(1) == 0)
        def _():
            o_ref[...] = jnp.zeros_like(o_ref)

        o_ref[...] += _dot(a_ref[...], b_ref[...])

    return pl.pallas_call(
        body, name=name, grid=(NDEV, t // tk),
        in_specs=[pl.BlockSpec((m, tk), lambda j, k: (0, k)),
                  pl.BlockSpec((None, None, tk, width), lambda j, k: (j, q, k, 0))],
        out_specs=pl.BlockSpec((None, None, m, width), lambda j, k: (j % 2, j // 2, 0, 0)),
        out_shape=jax.ShapeDtypeStruct((2, 4, m, width), F32),
        compiler_params=_cparams(2, VMEM_BIG),
    )(at, b)


def wgrad_cols_resident(at, b, q, name, tk):
    m, t = at.shape
    width = b.shape[3]

    def body(a_ref, b_ref, o_ref):
        k, j = pl.program_id(0), pl.program_id(1)

        @pl.when((k == 0) & (j == 0))
        def _():
            o_ref[...] = jnp.zeros_like(o_ref)

        o_ref[j % 2, j // 2] += _dot(a_ref[...], b_ref[...])

    return pl.pallas_call(
        body, name=name, grid=(t // tk, NDEV),
        in_specs=[pl.BlockSpec((m, tk), lambda k, j: (0, k)),
                  pl.BlockSpec((None, None, tk, width), lambda k, j: (j, q, k, 0))],
        out_specs=pl.BlockSpec((2, 4, m, width), lambda k, j: (0, 0, 0, 0)),
        out_shape=jax.ShapeDtypeStruct((2, 4, m, width), F32),
        compiler_params=_cparams(2, VMEM_BIG),
    )(at, b)


def wgrad_down(actt, dx2, tk):
    t = dx2.shape[0]

    def body(a_ref, b_ref, o_ref, acc_ref):
        kt = pl.program_id(1)

        @pl.when(kt == 0)
        def _():
            acc_ref[...] = jnp.zeros_like(acc_ref)

        acc_ref[...] += _dot(a_ref[...], b_ref[...].astype(BF16))

        @pl.when(kt == pl.num_programs(1) - 1)
        def _():
            o_ref[0] = acc_ref[:SH_DN]
            o_ref[1] = acc_ref[SH_DN:]

    return pl.pallas_call(
        body, name="wgrad_down", grid=(NCH, t // tk),
        in_specs=[pl.BlockSpec((None, SH_UP, tk), lambda k, kt: (k, 0, kt)), pl.BlockSpec((tk, D), lambda k, kt: (kt, 0))],
        out_specs=pl.BlockSpec((2, None, SH_DN, D), lambda k, kt: (0, k, 0, 0)),
        out_shape=jax.ShapeDtypeStruct((2, 4, SH_DN, D), F32),
        scratch_shapes=[pltpu.VMEM((SH_UP, D), F32)],
        compiler_params=_cparams(2, VMEM_BIG),
    )(actt, dx2)


def wgrad_square(a, b, name, tk):
    t = a.shape[0]

    def body(a_ref, b_ref, o_ref, acc_ref):
        kt = pl.program_id(0)

        @pl.when(kt == 0)
        def _():
            acc_ref[...] = jnp.zeros_like(acc_ref)

        acc_ref[...] += _dot_tn(a_ref[...], b_ref[...].astype(BF16))

        @pl.when(kt == pl.num_programs(0) - 1)
        def _():
            for j in range(NDEV):
                cc, xy = _slot(j)
                o_ref[cc, xy] = acc_ref[j * 128:(j + 1) * 128]

    return pl.pallas_call(
        body, name=name, grid=(t // tk,),
        in_specs=[pl.BlockSpec((tk, D), lambda k: (k, 0)), pl.BlockSpec((tk, D), lambda k: (k, 0))],
        out_specs=pl.BlockSpec((2, 4, 128, D), lambda k: (0, 0, 0, 0)),
        out_shape=jax.ShapeDtypeStruct((2, 4, 128, D), F32),
        scratch_shapes=[pltpu.VMEM((D, D), F32)],
        compiler_params=_cparams(1, VMEM_BIG),
    )(a, b)


def wgrad_pool(p, dpw, tk):
    t = p.shape[1]

    def body(a_ref, b_ref, o_ref):
        @pl.when(pl.program_id(0) == 0)
        def _():
            o_ref[...] = jnp.zeros_like(o_ref)

        for g in range(NG):
            o_ref[g] += _dot_tn(a_ref[g], b_ref[g])

    return pl.pallas_call(
        body, name="wgrad_pool", grid=(t // tk,),
        in_specs=[pl.BlockSpec((NG, tk, CG), lambda k: (0, k, 0))] * 2,
        out_specs=pl.BlockSpec((NG, CG, CG), lambda k: (0, 0, 0)),
        out_shape=jax.ShapeDtypeStruct((NG, CG, CG), F32),
        compiler_params=_cparams(1, VMEM_BIG),
    )(p, dpw)


def _adamw(w, g, m, v):
    m = ADAM_B1 * m + (1.0 - ADAM_B1) * g
    v = ADAM_B2 * v + (1.0 - ADAM_B2) * (g * g)
    m_hat = m / (1.0 - ADAM_B1 ** ADAM_STEP)
    v_hat = v / (1.0 - ADAM_B2 ** ADAM_STEP)
    delta = -ADAM_LR * (m_hat / (jnp.sqrt(v_hat) + ADAM_EPS) + ADAM_WD * w)
    return delta, m, v


def _row_block(r):
    return 512 if r % 512 == 0 else r


def chip_partial(place, g, from_sibling, name):
    _, _, r, c = g.shape

    def body(place_ref, g_ref, s_ref, o_ref):
        o_ref[...] = (g_ref[...] + s_ref[...]).astype(BF16)

    return pl.pallas_call(
        body, name=name,
        grid_spec=pltpu.PrefetchScalarGridSpec(
            num_scalar_prefetch=1, grid=(3,),
            in_specs=[pl.BlockSpec((None, None, r, c), lambda k, pr: (pr[0], pr[1] ^ (k + 1), 0, 0)),
                      pl.BlockSpec((None, r, c), lambda k, pr: (pr[1] ^ (k + 1), 0, 0))],
            out_specs=pl.BlockSpec((None, r, c), lambda k, pr: (pr[1] ^ (k + 1), 0, 0))),
        out_shape=jax.ShapeDtypeStruct((4, r, c), BF16),
        compiler_params=_cparams(1, VMEM_BIG),
    )(place, g, from_sibling)


def finish_adamw(place, gs, from_sibling, from_chips, w, m, v, name):
    n = len(gs)
    r = gs[0].shape[2]
    widths = [g.shape[3] for g in gs]
    c = sum(widths)
    br = _row_block(r)

    def body(place_ref, *refs):
        g_refs, s_refs, c_refs = refs[:n], refs[n:2 * n], refs[2 * n:5 * n]
        w_ref, m_ref, v_ref, og_ref, od_ref, om_ref, ov_ref = refs[5 * n:]
        cols = []
        for q in range(n):
            grad = g_refs[q][...] + s_refs[q][...]
            for k in range(3):
                grad = grad + c_refs[3 * q + k][...].astype(F32)
            cols.append(grad)
        grad = cols[0] if n == 1 else jnp.concatenate(cols, axis=1)
        og_ref[...] = grad
        od_ref[...], om_ref[...], ov_ref[...] = _adamw(w_ref[...], grad, m_ref[...], v_ref[...])

    def other(k, cq):
        return pl.BlockSpec((None, br, cq), lambda i, pr: (pr[1] ^ k, i, 0))

    row = pl.BlockSpec((br, c), lambda i, pr: (i, 0))
    out = jax.ShapeDtypeStruct((r, c), F32)
    in_specs = [pl.BlockSpec((None, None, br, cq), lambda i, pr: (pr[0], pr[1], i, 0)) for cq in widths]
    in_specs += [pl.BlockSpec((None, br, cq), lambda i, pr: (pr[1], i, 0)) for cq in widths]
    in_specs += [other(k, cq) for cq in widths for k in (1, 2, 3)]
    return pl.pallas_call(
        body, name=name,
        grid_spec=pltpu.PrefetchScalarGridSpec(
            num_scalar_prefetch=1, grid=(r // br,), in_specs=in_specs + [row, row, row], out_specs=[row] * 4),
        out_shape=[out] * 4,
        compiler_params=_cparams(1, VMEM_BIG),
    )(place, *gs, *from_sibling, *[fc for fc in from_chips for _ in range(3)], w, m, v)


def adamw_small(items):
    n = len(items)

    def body(*refs):
        ins, outs = refs[:4 * n], refs[4 * n:]
        for i in range(n):
            w, g, m, v = (r[...] for r in ins[4 * i:4 * i + 4])
            outs[3 * i][...], outs[3 * i + 1][...], outs[3 * i + 2][...] = _adamw(w, g, m, v)

    out = [jax.ShapeDtypeStruct(it[0].shape, F32) for it in items for _ in range(3)]
    res = pl.pallas_call(body, name="adamw_small", out_shape=out)(*[a for it in items for a in it])
    return [res[3 * i:3 * i + 3] for i in range(n)]


def kernel(x, norm_mix, w_in, pool_w, pool_scale, w_pool_proj, conv_w, w_conv_out, w_o, norm_ffn, w_up, ffn_conv_w, ffn_conv_b, w_down, norm_final, loss_target, m_norm_mix, m_w_in, m_pool_w, m_pool_scale, m_w_pool_proj, m_conv_w, m_w_conv_out, m_w_o, m_norm_ffn, m_w_up, m_ffn_conv_w, m_ffn_conv_b, m_w_down, m_norm_final, v_norm_mix, v_w_in, v_pool_w, v_pool_scale, v_w_pool_proj, v_conv_w, v_w_conv_out, v_w_o, v_norm_ffn, v_w_up, v_ffn_conv_w, v_ffn_conv_b, v_w_down, v_norm_final):
    nb, seq, _ = x.shape
    t = nb * seq
    tm_in = min(TM_IN, t)
    tm_mix = min(TM_MIX, seq)
    tm_ffn = min(TM_FFN, seq)
    tk = min(TK_WGRAD, t)
    xt = x.reshape(t, D)
    tgt = loss_target.reshape(t, D)
    xi, yi, ci = _pos()
    me = 4 * xi + 2 * yi + ci
    place = jnp.stack([ci, 2 * xi + yi]).astype(jnp.int32)

    tie = lax.optimization_barrier
    w_in_g, = all_gather_blocks([w_in[0].astype(BF16)], "all_gather_w_in", 0)
    taps = (jnp.pad(conv_w[0], ((0, 5), (0, D - 128))) + jnp.pad(ffn_conv_w[0], ((3, 2), (0, D - SH_UP))))
    taps_g = _exchange_small(taps, False, "all_gather_taps")
    mix_shard = jnp.concatenate(
        [w_pool_proj[0], w_conv_out[0], w_o[0], pool_w[0].reshape(NG * 32, CG)], axis=1).astype(BF16)
    mix_shard, taps_g = tie((mix_shard, taps_g))
    wmix_g, = all_gather_blocks([mix_shard], "all_gather_w_mix", 0)
    ffn_shards, w_in_g = tie(([w_up[0].astype(BF16), w_down[0].astype(BF16)], w_in_g))
    w_up_g, w_dn_g = all_gather_blocks(ffn_shards, "all_gather_w_ffn", 0)
    wmix = wmix_g.reshape(D, MIX_COLS)
    pool_w_f = wmix_g[:, :, 3 * D:].reshape(NDEV, NG, 32, CG).transpose(1, 0, 2, 3).reshape(NG, CG, CG)
    w_dn_f = w_dn_g.reshape(NCH, SH_UP, D)
    conv_w_f = taps_g[:, 0:3, :128].transpose(1, 0, 2).reshape(3, D)
    fcw_f = taps_g[:, 3:6, :SH_UP]
    fcb_f = ffn_conv_b.reshape(NDEV, 1, SH_UP)
    gfin = norm_final.reshape(1, D)

    z, h1 = fwd_in(xt, norm_mix, w_in_g, tm_in)
    x1, y_pool, y_conv = fwd_mix(z, xt, pool_w_f, pool_scale, conv_w_f, wmix, tm_mix, seq)
    up, pre, act, h2, dx2, ffn_vec = fwd_ffn(x1, norm_ffn, w_up_g, fcw_f, fcb_f, w_dn_f, gfin, tgt, tm_ffn, seq,
                                             FFN_CHUNKS_PER_STEP)

    def to_sibling(full, tag):
        return reduce_scatter_d2d(full, "reduce_scatter_d2d_" + tag, 1)

    def partials(full, from_sib, names):
        return [chip_partial(place, g, s, "chip_partial_" + nm) for g, s, nm in zip(full, from_sib, names)]

    def to_chips(parts, tag):
        return reduce_scatter_ici(parts, "reduce_scatter_ici_" + tag, 2)

    def finish(nm, gs, from_sib, from_chips, wmv):
        w, m, v = wmv
        rc = (gs[0].shape[2], sum(g.shape[3] for g in gs))
        outs = finish_adamw(place, gs, from_sib, from_chips, w.reshape(rc), m.reshape(rc), v.reshape(rc), "adamw_" + nm)
        return [o.reshape(w.shape) for o in outs]

    def after(x, dep):
        return tie((x, dep))[0]

    big = {}
    gw_dn = wgrad_down(act, dx2, tk)
    sib_dn = to_sibling([gw_dn], "w_down")
    d_up, dx1, g_ffn_vec, g_nffn = bwd_ffn(dx2, x1, norm_ffn, up, pre, w_up_g, fcw_f, w_dn_f, tm_ffn, seq,
                                           FFN_CHUNKS_PER_STEP)
    d_up, part_dn = tie((d_up, partials([gw_dn], sib_dn, ["w_down"])))
    chips_dn = to_chips(part_dn, "w_down")
    gw_up = wgrad_cols(h2, d_up.reshape(NDEV, 1, t, SH_UP), 0, "wgrad_up", tk)
    sib_up = to_sibling([after(gw_up, chips_dn)], "w_up")
    dz, merged, p2, u, dyp, dyc, p, dpw, g_mix_vec = bwd_mix(
        dx1, z, y_pool, y_conv, pool_w_f, pool_scale, conv_w_f, wmix, tm_mix, seq)
    merged, part_up = tie((merged, partials([gw_up], sib_up, ["w_up"])))
    chips_up = to_chips(part_up, "w_up")
    small_g, = all_gather_blocks(
        [after(jnp.concatenate([g_mix_vec, g_nffn, ffn_vec, g_ffn_vec.reshape(8 * NDEV, D)], axis=0), chips_up)],
        "all_gather_small", 0)
    gw_o = wgrad_square(merged, dx1, "wgrad_o", tk)
    gw_pp = wgrad_square(p2, dyp, "wgrad_pool_proj", tk)
    gw_co = wgrad_square(u, dyc, "wgrad_conv_out", tk)
    gw_pool = wgrad_pool(p, dpw, tk).reshape(NG, 4, 2, 32, CG).transpose(2, 1, 0, 3, 4).reshape(2, 4, NG * 32, CG)
    dz8 = dz.reshape(NDEV, 3, t, CG)
    gw_in, sib_in, chips_in = [None] * 3, [None] * 3, [None] * 3
    gw_in[0] = wgrad_cols_resident(h1, dz8, 0, "wgrad_in_0", tk)
    sib_a = to_sibling(after([gw_o, gw_pp], (chips_up, gw_in[0])), "mix_a")
    sib_b = to_sibling(after([gw_co, gw_pool], sib_a), "mix_b")
    sib_in[0] = to_sibling(after([gw_in[0]], sib_b), "w_in_0")
    gw_in[1] = wgrad_cols_resident(h1, dz8, 1, "wgrad_in_1", tk)
    sib_in[1] = to_sibling(after([gw_in[1]], sib_in[0]), "w_in_1")
    h1, part_a, part_b, part_in0 = tie((h1, partials([gw_o, gw_pp], sib_a, ["w_o", "w_pool_proj"]),
                                        partials([gw_co, gw_pool], sib_b, ["w_conv_out", "pool_w"]),
                                        partials([gw_in[0]], sib_in[0], ["w_in_0"])))
    chips_a = to_chips(after(part_a, sib_in[1]), "mix_a")
    chips_b = to_chips(part_b, "mix_b")
    chips_in[0] = to_chips(part_in0, "w_in_0")
    h1, big["w_down"], big["w_up"] = tie((
        h1, finish("w_down", [gw_dn], sib_dn, chips_dn, (w_down, m_w_down, v_w_down)),
        finish("w_up", [gw_up], sib_up, chips_up, (w_up, m_w_up, v_w_up))))
    gw_in[2] = wgrad_cols_resident(h1, dz8, 2, "wgrad_in_2", tk)
    sib_in[2] = to_sibling(after([gw_in[2]], (chips_a, chips_b, chips_in[0])), "w_in_2")
    dx1, part_in1, part_in2, big["w_o"], big["w_pool_proj"], big["w_conv_out"], big["pool_w"] = tie((
        dx1, partials([gw_in[1]], sib_in[1], ["w_in_1"]), partials([gw_in[2]], sib_in[2], ["w_in_2"]),
        finish("w_o", [gw_o], sib_a[:1], chips_a[:1], (w_o, m_w_o, v_w_o)),
        finish("w_pool_proj", [gw_pp], sib_a[1:], chips_a[1:], (w_pool_proj, m_w_pool_proj, v_w_pool_proj)),
        finish("w_conv_out", [gw_co], sib_b[:1], chips_b[:1], (w_conv_out, m_w_conv_out, v_w_conv_out)),
        finish("pool_w", [gw_pool], sib_b[1:], chips_b[1:], (pool_w, m_pool_w, v_pool_w))))
    chips_in[1] = to_chips(after(part_in1, sib_in[2]), "w_in_1")
    chips_in[2] = to_chips(part_in2, "w_in_2")
    grad_x, g_nmix = bwd_in(dz, w_in_g, dx1, xt, norm_mix, min(TM_BWD_IN, t))
    grad_x, chips_in = tie((grad_x, chips_in))
    big["w_in"] = finish("w_in", gw_in, [s[0] for s in sib_in], [c[0] for c in chips_in], (w_in, m_w_in, v_w_in))

    red_n, red = _exchange_small(g_nmix, True, "all_reduce_small", gathered=small_g)
    g_norm_mix, g_pool_scale, g_norm_ffn = red_n[0:1], red[0:1], red[8:9]
    g_conv_w = lax.dynamic_slice(red, (1, me * 128), (3, 128))
    g_norm_final = red[16]
    loss = red[17, 0]
    g_fcb = red[24:].reshape(NDEV, 8, D)[:, 0, :SH_UP].reshape(1, FF2)
    g_fcw = lax.dynamic_slice(red, (25 + 8 * me, 0), (3, SH_UP))
    grads = {"norm_mix": g_norm_mix, "pool_scale": g_pool_scale, "norm_ffn": g_norm_ffn, "norm_final": g_norm_final,
             "ffn_conv_b": g_fcb, "conv_w": g_conv_w.reshape(1, 3, 128), "ffn_conv_w": g_fcw.reshape(1, 3, SH_UP)}
    small_wmv = {"norm_mix": (norm_mix, m_norm_mix, v_norm_mix), "pool_scale": (pool_scale, m_pool_scale, v_pool_scale),
                 "norm_ffn": (norm_ffn, m_norm_ffn, v_norm_ffn), "norm_final": (norm_final, m_norm_final, v_norm_final),
                 "ffn_conv_b": (ffn_conv_b, m_ffn_conv_b, v_ffn_conv_b), "conv_w": (conv_w, m_conv_w, v_conv_w),
                 "ffn_conv_w": (ffn_conv_w, m_ffn_conv_w, v_ffn_conv_w)}
    small_names = list(small_wmv)
    flat2 = lambda a: a.reshape(-1, a.shape[-1])
    small_out = adamw_small([(flat2(small_wmv[nm][0]), flat2(grads[nm]), flat2(small_wmv[nm][1]),
                              flat2(small_wmv[nm][2])) for nm in small_names])
    small = {nm: [o.reshape(small_wmv[nm][0].shape) for o in outs] for nm, outs in zip(small_names, small_out)}

    order = ["norm_mix", "w_in", "pool_w", "pool_scale", "w_pool_proj", "conv_w", "w_conv_out", "w_o", "norm_ffn",
             "w_up", "ffn_conv_w", "ffn_conv_b", "w_down", "norm_final"]
    out = [loss, grad_x.reshape(nb, seq, D)]
    out += [big[nm][0] if nm in big else grads[nm] for nm in order]
    for idx in range(3):
        out += [big[nm][idx + 1] if nm in big else small[nm][idx] for nm in order]
    return tuple(out)
```

```python
import functools

import jax
import jax.numpy as jnp
from jax import lax
from jax.experimental import pallas as pl
from jax.experimental.pallas import tpu as pltpu
from jax.experimental.pallas import tpu_sc as plsc

F32 = jnp.float32
BF16 = jnp.bfloat16

NDEV = 8
D = 1024
NG = 4
CG = 256
WINS = (2, 4, 8, 16)
DIN = 6 * D
SH_IN = DIN // NDEV
NZT = DIN // CG
FF2 = 5632
SH_UP = FF2 // NDEV
FF = FF2 // 2
NCH = 4
SH_DN = FF // NDEV
RMS_EPS = 1e-6
HALO = 16

ADAM_LR = 0.001
ADAM_B1 = 0.9
ADAM_B2 = 0.999
ADAM_EPS = 1e-08
ADAM_WD = 0.01
ADAM_STEP = 10

TM_IN = 512
TM_BWD_IN = 256
TM_MIX = 256
TM_FFN = 256
FFN_CHUNKS_PER_STEP = 4
TK_WGRAD = 2048
MIX_POOL_PROJ, MIX_CONV_OUT, MIX_O = 0, 1, 2
MIX_COLS = 3 * D + CG
VMEM_BIG = 56 * 1024 * 1024
MESH = pl.DeviceIdType.MESH
ANY = pl.BlockSpec(memory_space=pl.ANY)


def _cparams(n_axes, vmem=None):
    return pltpu.CompilerParams(dimension_semantics=("arbitrary",) * n_axes, vmem_limit_bytes=vmem)


def _dot(a, b):
    return jnp.dot(a, b, preferred_element_type=F32)


def _dot_nt(a, b):
    return lax.dot_general(a, b, (((1,), (1,)), ((), ())), preferred_element_type=F32)


def _dot_tn(a, b):
    return lax.dot_general(a, b, (((0,), (0,)), ((), ())), preferred_element_type=F32)


def _shift_down(ext, s, lead):
    return pltpu.roll(ext, s, 0)[lead:]


def _shift_up(ext, s, tm):
    n = ext.shape[0]
    return pltpu.roll(ext, n - s, 0)[:tm]


def _rms_inv(x):
    return lax.rsqrt(jnp.mean(x * x, axis=-1, keepdims=True) + RMS_EPS)


def _rms_bwd(dh, xn, inv, g):
    dxn = dh * g
    return inv * (dxn - xn * jnp.mean(dxn * xn, axis=-1, keepdims=True))


def _pos():
    return lax.axis_index("x"), lax.axis_index("y"), lax.axis_index("c")


def _handshake(peers):
    barrier = pltpu.get_barrier_semaphore()
    for peer in peers:
        pl.semaphore_signal(barrier, inc=1, device_id=peer, device_id_type=MESH)
    pl.semaphore_wait(barrier, len(peers))


def _sequencer(body, out_type, n_sems, name, collective_id):
    return pl.kernel(
        body, out_type=out_type, mesh=plsc.ScalarSubcoreMesh(axis_name="sequencer", num_cores=1), name=name,
        scratch_types=[pltpu.SemaphoreType.DMA((n_sems,)), pltpu.SemaphoreType.DMA((n_sems,))],
        compiler_params=pltpu.CompilerParams(collective_id=collective_id))


def all_gather_blocks(shards, name, collective_id):
    n = len(shards)

    def body(*refs):
        ins, outs = refs[:n], refs[n:2 * n]
        send_sems, recv_sems = refs[2 * n:]
        x, y, c = _pos()
        sibling = (x, y, 1 - c)
        chips = [(1 - x, y), (x, 1 - y), (1 - x, 1 - y)]
        _handshake([sibling] + [(*chip, c) for chip in chips])

        def copy(w, k, block, to, src=None):
            slot = outs[w].at[4 * block[0] + 2 * block[1] + block[2]]
            return pltpu.make_async_remote_copy(
                src_ref=slot if src is None else src, dst_ref=slot,
                send_sem=send_sems.at[8 * w + k], recv_sem=recv_sems.at[8 * w + k], device_id=to, device_id_type=MESH)

        mine, first, passed = [], [], []
        for w in range(n):
            m = pltpu.make_async_copy(ins[w], outs[w].at[4 * x + 2 * y + c], send_sems.at[8 * w + 7])
            m.start()
            mine.append(m)
            first.append(copy(w, 0, (x, y, c), sibling, src=ins[w]))
            first += [copy(w, 1 + j, (x, y, c), (*chip, c), src=ins[w]) for j, chip in enumerate(chips)]
        for cp in first:
            cp.start()
        for w in range(n):
            for j, chip in enumerate(chips):
                copy(w, 1 + j, (*chip, c), (x, y, c)).wait_recv()
                fw = copy(w, 4 + j, (*chip, c), sibling)
                fw.start()
                passed.append(fw)
        for w in range(n):
            copy(w, 0, (x, y, 1 - c), (x, y, c)).wait_recv()
            for j, chip in enumerate(chips):
                copy(w, 4 + j, (*chip, 1 - c), (x, y, c)).wait_recv()
        for cp in first + passed:
            cp.wait_send()
        for m in mine:
            m.wait()

    out = [jax.ShapeDtypeStruct((NDEV,) + s.shape, s.dtype) for s in shards]
    return _sequencer(body, out, 8 * n, name, collective_id)(*shards)


def _exchange_small(v, reduce, name, gathered=None):
    rows = v.shape[0]

    def body(*refs):
        if gathered is None:
            v_ref, out_ref, slots, send_sems, recv_sems, local_sem = refs
        else:
            v_ref, g_ref, out_ref, gsum_ref, slots, send_sems, recv_sems, local_sem = refs
        x, y, c = _pos()
        me = 4 * x + 2 * y + c
        mine = pltpu.make_async_copy(v_ref, slots.at[me], local_sem)
        mine.start()
        offs = [(dx, dy, dc) for dx in (0, 1) for dy in (0, 1) for dc in (0, 1)][1:]

        def copy(k, src_slot, to):
            return pltpu.make_async_remote_copy(
                src_ref=v_ref, dst_ref=slots.at[src_slot], send_sem=send_sems.at[k], recv_sem=recv_sems.at[k],
                device_id=to, device_id_type=MESH)

        sends = []
        for k, (dx, dy, dc) in enumerate(offs):
            cp = copy(k, me, (x ^ dx, y ^ dy, c ^ dc))
            cp.start()
            sends.append(cp)
        for k, (dx, dy, dc) in enumerate(offs):
            copy(k, 4 * (x ^ dx) + 2 * (y ^ dy) + (c ^ dc), (x, y, c)).wait_recv()
        for cp in sends:
            cp.wait_send()
        mine.wait()
        if reduce:
            acc = slots[0]
            for d in range(1, NDEV):
                acc = acc + slots[d]
            out_ref[...] = acc
        else:
            out_ref[...] = slots[...]
        if gathered is not None:
            acc = g_ref[0]
            for d in range(1, NDEV):
                acc = acc + g_ref[d]
            gsum_ref[...] = acc

    vmem = pl.BlockSpec(memory_space=pltpu.VMEM)
    out = jax.ShapeDtypeStruct((rows, D) if reduce else (NDEV, rows, D), F32)
    args, out_shape, out_specs = [v], out, vmem
    if gathered is not None:
        args.append(gathered)
        out_shape, out_specs = [out, jax.ShapeDtypeStruct(gathered.shape[1:], F32)], [vmem, vmem]
    return pl.pallas_call(
        body, name=name, out_shape=out_shape, in_specs=[vmem] * len(args), out_specs=out_specs,
        scratch_shapes=[pltpu.VMEM((NDEV, rows, D), F32), pltpu.SemaphoreType.DMA((7,)),
                        pltpu.SemaphoreType.DMA((7,)), pltpu.SemaphoreType.DMA],
    )(*args)


def reduce_scatter_d2d(grads, name, collective_id):
    n = len(grads)

    def body(*refs):
        ins, outs = refs[:n], refs[n:2 * n]
        send_sems, recv_sems = refs[2 * n:]
        x, y, c = _pos()
        _handshake([(x, y, 1 - c)])
        cps = []
        for w in range(n):
            cp = pltpu.make_async_remote_copy(
                src_ref=ins[w].at[1 - c], dst_ref=outs[w], send_sem=send_sems.at[w], recv_sem=recv_sems.at[w],
                device_id=(x, y, 1 - c), device_id_type=MESH)
            cp.start()
            cps.append(cp)
        for cp in cps:
            cp.wait_recv()
        for cp in cps:
            cp.wait_send()

    out = [jax.ShapeDtypeStruct(g.shape[1:], F32) for g in grads]
    return _sequencer(body, out, n, name, collective_id)(*grads)


def reduce_scatter_ici(parts, name, collective_id):
    n = len(parts)

    def body(*refs):
        ins, outs = refs[:n], refs[n:2 * n]
        send_sems, recv_sems = refs[2 * n:]
        x, y, c = _pos()
        offs = [(1, 0), (0, 1), (1, 1)]
        _handshake([(x ^ dx, y ^ dy, c) for dx, dy in offs])
        cps = []
        for w in range(n):
            for k, (dx, dy) in enumerate(offs):
                ox, oy = x ^ dx, y ^ dy
                cp = pltpu.make_async_remote_copy(
                    src_ref=ins[w].at[2 * ox + oy], dst_ref=outs[w].at[2 * x + y],
                    send_sem=send_sems.at[3 * w + k], recv_sem=recv_sems.at[3 * w + k],
                    device_id=(ox, oy, c), device_id_type=MESH)
                cp.start()
                cps.append((cp, w, k, ox, oy))
        for cp, w, k, ox, oy in cps:
            pltpu.make_async_remote_copy(
                src_ref=ins[w].at[2 * ox + oy], dst_ref=outs[w].at[2 * ox + oy],
                send_sem=send_sems.at[3 * w + k], recv_sem=recv_sems.at[3 * w + k],
                device_id=(ox, oy, c), device_id_type=MESH).wait_recv()
        for cp, *_ in cps:
            cp.wait_send()

    out = [jax.ShapeDtypeStruct(p.shape, BF16) for p in parts]
    return _sequencer(body, out, 3 * n, name, collective_id)(*parts)


def fwd_in(x, g1, w_in_g, tm):
    t = x.shape[0]

    def body(x_ref, g_ref, w_ref, z_ref, ht_ref):
        xf = x_ref[...]
        h = (xf * _rms_inv(xf) * g_ref[...]).astype(BF16)
        ht_ref[...] = h.T
        for j in range(NDEV):
            r = _dot(h, w_ref[j])
            for q in range(3):
                z_ref[3 * j + q] = r[:, q * CG:(q + 1) * CG].astype(BF16)

    return pl.pallas_call(
        body, name="fwd_in", grid=(t // tm,),
        in_specs=[pl.BlockSpec((tm, D), lambda i: (i, 0)), pl.BlockSpec((1, D), lambda i: (0, 0)),
                  pl.BlockSpec((NDEV, D, SH_IN), lambda i: (0, 0, 0))],
        out_specs=[pl.BlockSpec((NZT, tm, CG), lambda i: (0, i, 0)), pl.BlockSpec((D, tm), lambda i: (0, i))],
        out_shape=[jax.ShapeDtypeStruct((NZT, t, CG), BF16), jax.ShapeDtypeStruct((D, t), BF16)],
        compiler_params=_cparams(1, VMEM_BIG),
        cost_estimate=pl.CostEstimate(flops=2 * t * D * DIN, transcendentals=t,
                                      bytes_accessed=4 * t * D + 2 * D * DIN + 2 * t * DIN + 2 * t * D),
    )(x, g1, w_in_g)


def _pool_tile(z_ref, zh_ref, g, win, keep_hist, cnt):
    zt = z_ref[g].astype(F32)
    ext = jnp.concatenate([zh_ref[g].astype(F32) * keep_hist, zt], axis=0)
    s, sh = ext, 1
    while sh < win:
        s = s + pltpu.roll(s, sh, 0)
        sh *= 2
    return s[HALO:] / cnt - zt


def _conv_taps(ext, cur, w_ref, lanes, lead):
    x1 = _shift_down(ext, 1, lead)
    x2 = _shift_down(ext, 2, lead)
    out = w_ref[2:3, lanes] * cur + w_ref[1:2, lanes] * x1 + w_ref[0:1, lanes] * x2
    return out, x1, x2


def fwd_mix(z, x, pool_w, pool_scale, conv_w, wmix, tm, seq):
    t = x.shape[0]
    tps = seq // tm
    hb = tm // HALO

    def body(z_ref, zph_ref, zcvh_ref, x_ref, pw_ref, ps_ref, wpp_ref, cw_ref, wco_ref, wo_ref,
             x1_ref, yp_ref, yc_ref):
        it = pl.program_id(0) % tps
        keep_hist = jnp.where(it == 0, 0.0, 1.0)
        pos = it * tm + lax.broadcasted_iota(jnp.int32, (tm, 1), 0)
        p2 = []
        for g, win in enumerate(WINS):
            cnt = jnp.minimum(pos + 1, win).astype(F32)
            p = _pool_tile(z_ref, zph_ref, g, win, keep_hist, cnt)
            lanes = slice(g * CG, (g + 1) * CG)
            p2.append((_dot(p.astype(BF16), pw_ref[g]) * ps_ref[:, lanes]).astype(BF16))
        y_pool = _dot(jnp.concatenate(p2, axis=1), wpp_ref[...])
        u = []
        for q in range(NG):
            lanes = slice(q * CG, (q + 1) * CG)
            cv = z_ref[8 + q].astype(F32) * z_ref[12 + q].astype(F32)
            cvh = zcvh_ref[q].astype(F32) * zcvh_ref[4 + q].astype(F32) * keep_hist
            cc, _, _ = _conv_taps(jnp.concatenate([cvh, cv], axis=0), cv, cw_ref, lanes, HALO)
            u.append((z_ref[4 + q].astype(F32) * cc).astype(BF16))
        y_conv = _dot(jnp.concatenate(u, axis=1), wco_ref[...])
        ypb, ycb = y_pool.astype(BF16), y_conv.astype(BF16)
        yp_ref[...] = ypb
        yc_ref[...] = ycb
        merged = []
        for q in range(NG):
            lanes = slice(q * CG, (q + 1) * CG)
            sp = jax.nn.sigmoid(z_ref[16 + q].astype(F32))
            sc = jax.nn.sigmoid(z_ref[20 + q].astype(F32))
            merged.append((sp * ypb[:, lanes].astype(F32) + sc * ycb[:, lanes].astype(F32)).astype(BF16))
        x1_ref[...] = x_ref[...] + _dot(jnp.concatenate(merged, axis=1), wo_ref[...])

    def hist(i):
        return jnp.maximum(i * hb - 1, 0)

    const2 = lambda i: (0, 0)
    return pl.pallas_call(
        body, name="fwd_mix", grid=(t // tm,),
        in_specs=[pl.BlockSpec((NZT, tm, CG), lambda i: (0, i, 0)),
                  pl.BlockSpec((NG, HALO, CG), lambda i: (0, hist(i), 0)),
                  pl.BlockSpec((2 * NG, HALO, CG), lambda i: (1, hist(i), 0)),
                  pl.BlockSpec((tm, D), lambda i: (i, 0)),
                  pl.BlockSpec((NG, CG, CG), lambda i: (0, 0, 0)), pl.BlockSpec((1, D), const2),
                  pl.BlockSpec((D, D), lambda i: (0, MIX_POOL_PROJ)), pl.BlockSpec((3, D), const2),
                  pl.BlockSpec((D, D), lambda i: (0, MIX_CONV_OUT)), pl.BlockSpec((D, D), lambda i: (0, MIX_O))],
        out_specs=[pl.BlockSpec((tm, D), lambda i: (i, 0))] * 3,
        out_shape=[jax.ShapeDtypeStruct((t, D), F32), jax.ShapeDtypeStruct((t, D), BF16),
                   jax.ShapeDtypeStruct((t, D), BF16)],
        compiler_params=_cparams(1, VMEM_BIG),
    )(z, z, z, x, pool_w, pool_scale, wmix, conv_w, wmix, wmix)


def fwd_up(x1, g2, w_up_g, fcw, fcb, tm, seq, cps):
    t = x1.shape[0]
    tps = seq // tm

    def body(x1_ref, g2_ref, wup_ref, fcw_ref, fcb_ref,
             up_ref, pre_ref, act_ref, actt_ref, h2t_ref, hist_ref, h2_ref):
        i, k = pl.program_id(0), pl.program_id(1)
        keep_hist = jnp.where(i % tps == 0, 0.0, 1.0)

        @pl.when((i == 0) & (k == 0))
        def _():
            hist_ref[...] = jnp.zeros_like(hist_ref)

        @pl.when(k == 0)
        def _():
            x1v = x1_ref[...]
            h2 = (x1v * _rms_inv(x1v) * g2_ref[...]).astype(BF16)
            h2_ref[...] = h2
            h2t_ref[...] = h2.T

        h2 = h2_ref[...]
        lanes = slice(0, SH_UP)
        for c in range(cps):
            kc = k * cps + c
            conv = []
            for s in range(2):
                ub = _dot(h2, wup_ref[s, c]).astype(BF16)
                up_ref[s, c] = ub
                uf = ub.astype(F32)
                ext = jnp.concatenate([hist_ref[s, kc] * keep_hist, uf], axis=0)
                hist_ref[s, kc] = uf[tm - 8:]
                cc, _, _ = _conv_taps(ext, uf, fcw_ref.at[s, c], lanes, 8)
                conv.append(cc + fcb_ref[s, c])
                pre_ref[s, c] = conv[s].astype(BF16)
            a = (conv[0] * jax.nn.sigmoid(conv[0]) * conv[1]).astype(BF16)
            act_ref[c] = a
            actt_ref[c] = a.T

    tile = lambda i, k: (i, 0)
    const2 = lambda i, k: (0, 0)
    pair = lambda i, k: (0, k, 0, 0)
    chunk = pl.BlockSpec((2, cps, tm, SH_UP), lambda i, k: (0, k, i, 0))
    return pl.pallas_call(
        body, name="fwd_up", grid=(t // tm, NCH // cps),
        in_specs=[pl.BlockSpec((tm, D), tile), pl.BlockSpec((1, D), const2),
                  pl.BlockSpec((2, cps, D, SH_UP), pair), pl.BlockSpec((2, cps, 3, SH_UP), pair),
                  pl.BlockSpec((2, cps, 1, SH_UP), pair)],
        out_specs=[chunk, chunk, pl.BlockSpec((cps, tm, SH_UP), lambda i, k: (k, i, 0)),
                   pl.BlockSpec((cps, SH_UP, tm), lambda i, k: (k, 0, i)), pl.BlockSpec((D, tm), lambda i, k: (0, i))],
        out_shape=[jax.ShapeDtypeStruct((2, NCH, t, SH_UP), BF16), jax.ShapeDtypeStruct((2, NCH, t, SH_UP), BF16),
                   jax.ShapeDtypeStruct((NCH, t, SH_UP), BF16), jax.ShapeDtypeStruct((NCH, SH_UP, t), BF16),
                   jax.ShapeDtypeStruct((D, t), BF16)],
        scratch_shapes=[pltpu.VMEM((2, NCH, 8, SH_UP), F32), pltpu.VMEM((tm, D), BF16)],
        compiler_params=_cparams(2, VMEM_BIG),
    )(x1, g2, w_up_g.reshape(2, NCH, D, SH_UP), fcw.reshape(2, NCH, 3, SH_UP), fcb.reshape(2, NCH, 1, SH_UP))


def fwd_down(x1, act, w_dn, gf, tgt, tm):
    t = x1.shape[0]

    def body(x1_ref, act_ref, wdn_ref, gf_ref, tgt_ref, dx2_ref, vec_ref):
        @pl.when(pl.program_id(0) == 0)
        def _():
            vec_ref[...] = jnp.zeros_like(vec_ref)

        d = None
        for c in range(NCH):
            part = _dot(act_ref[c], wdn_ref[c])
            d = part if d is None else d + part
        x2 = x1_ref[...] + d
        inv3 = _rms_inv(x2)
        xn = x2 * inv3
        diff = xn * gf_ref[...] - tgt_ref[...]
        dy = diff * (1.0 / D)
        vec_ref[0:1, :] += jnp.sum(dy * xn, axis=0, keepdims=True)
        vec_ref[1:2, :] += 0.5 * jnp.sum(jnp.mean(diff * diff, axis=-1))
        dx2_ref[...] = _rms_bwd(dy, xn, inv3, gf_ref[...])

    tile = lambda i: (i, 0)
    const2 = lambda i: (0, 0)
    return pl.pallas_call(
        body, name="fwd_down", grid=(t // tm,),
        in_specs=[pl.BlockSpec((tm, D), tile), pl.BlockSpec((NCH, tm, SH_UP), lambda i: (0, i, 0)),
                  pl.BlockSpec((NCH, SH_UP, D), lambda i: (0, 0, 0)), pl.BlockSpec((1, D), const2),
                  pl.BlockSpec((tm, D), tile)],
        out_specs=[pl.BlockSpec((tm, D), tile), pl.BlockSpec((8, D), const2)],
        out_shape=[jax.ShapeDtypeStruct((t, D), F32), jax.ShapeDtypeStruct((8, D), F32)],
        compiler_params=_cparams(1, VMEM_BIG),
    )(x1, act, w_dn, gf, tgt)


def bwd_ffn(dx2, x1, g2, up, pre, w_up_g, fcw, w_dn, tm, seq, cps):
    t = x1.shape[0]
    nt = t // tm
    tps = seq // tm

    def body(dx2_ref, x1_ref, g2_ref, up_ref, pre_ref, wup_ref, fcw_ref, wdn_ref,
             dup_ref, dx1_ref, gvec_ref, gn_ref, carry_ref, dh2_ref, acc_ref):
        i, k = pl.program_id(0), pl.program_id(1)
        it = (nt - 1 - i) % tps
        keep_next = jnp.where(it == tps - 1, 0.0, 1.0)

        @pl.when((i == 0) & (k == 0))
        def _():
            acc_ref[...] = jnp.zeros_like(acc_ref)
            gn_ref[...] = jnp.zeros_like(gn_ref)
            carry_ref[...] = jnp.zeros_like(carry_ref)

        @pl.when(k == 0)
        def _():
            dh2_ref[...] = jnp.zeros_like(dh2_ref)

        dxb = dx2_ref[...].astype(BF16)
        lanes = slice(0, SH_UP)
        dh2 = dh2_ref[...]
        for c in range(cps):
            kc = k * cps + c
            pre = [pre_ref[s, c].astype(F32) for s in range(2)]
            sg = jax.nn.sigmoid(pre[0])
            dact = _dot_nt(dxb, wdn_ref[c])
            dpre = [dact * pre[1] * (sg * (1.0 + pre[0] * (1.0 - sg))), dact * (pre[0] * sg)]
            for s in range(2):
                dc = dpre[s]
                ext = jnp.concatenate([dc, carry_ref[s, kc] * keep_next], axis=0)
                carry_ref[s, kc] = dc[:8]
                shifted = (_shift_up(ext, 2, tm), _shift_up(ext, 1, tm), dc)
                uf = up_ref[s, c].astype(F32)
                acc_ref[s, kc, 0:1, lanes] += jnp.sum(dc, axis=0, keepdims=True)
                for tap in range(3):
                    acc_ref[s, kc, tap + 1:tap + 2, lanes] += jnp.sum(shifted[tap] * uf, axis=0, keepdims=True)
                w = fcw_ref.at[s, c]
                du = w[2:3, :] * dc + w[1:2, :] * shifted[1] + w[0:1, :] * shifted[0]
                dub = du.astype(BF16)
                dup_ref[s, c] = dub
                dh2 = dh2 + _dot_nt(dub, wup_ref[s, c])
        dh2_ref[...] = dh2

        @pl.when(k == NCH // cps - 1)
        def _():
            x1v = x1_ref[...]
            inv2 = _rms_inv(x1v)
            xn = x1v * inv2
            gn_ref[0:1, :] += jnp.sum(dh2 * xn, axis=0, keepdims=True)
            dx1_ref[...] = dx2_ref[...] + _rms_bwd(dh2, xn, inv2, g2_ref[...])

        @pl.when((i == nt - 1) & (k == NCH // cps - 1))
        def _():
            gvec_ref[...] = acc_ref[...]

    rev = lambda i, k: (nt - 1 - i, 0)
    const2 = lambda i, k: (0, 0)
    pair = lambda i, k: (0, k, 0, 0)
    return pl.pallas_call(
        body, name="bwd_ffn", grid=(nt, NCH // cps),
        in_specs=[pl.BlockSpec((tm, D), rev), pl.BlockSpec((tm, D), rev), pl.BlockSpec((1, D), const2),
                  pl.BlockSpec((2, cps, tm, SH_UP), lambda i, k: (0, k, nt - 1 - i, 0)),
                  pl.BlockSpec((2, cps, tm, SH_UP), lambda i, k: (0, k, nt - 1 - i, 0)),
                  pl.BlockSpec((2, cps, D, SH_UP), pair), pl.BlockSpec((2, cps, 3, SH_UP), pair),
                  pl.BlockSpec((cps, SH_UP, D), lambda i, k: (k, 0, 0))],
        out_specs=[pl.BlockSpec((2, cps, tm, SH_UP), lambda i, k: (0, k, nt - 1 - i, 0)), pl.BlockSpec((tm, D), rev),
                   pl.BlockSpec((2, NCH, 8, D), lambda i, k: (0, 0, 0, 0)), pl.BlockSpec((8, D), const2)],
        out_shape=[jax.ShapeDtypeStruct((2, NCH, t, SH_UP), BF16), jax.ShapeDtypeStruct((t, D), F32),
                   jax.ShapeDtypeStruct((2, NCH, 8, D), F32), jax.ShapeDtypeStruct((8, D), F32)],
        scratch_shapes=[pltpu.VMEM((2, NCH, 8, SH_UP), F32), pltpu.VMEM((tm, D), F32),
                        pltpu.VMEM((2, NCH, 8, D), F32)],
        compiler_params=_cparams(2, VMEM_BIG),
    )(dx2, x1, g2, up, pre, w_up_g.reshape(2, NCH, D, SH_UP), fcw.reshape(2, NCH, 3, SH_UP), w_dn)


def bwd_mix(dx1, z, y_pool, y_conv, pool_w, pool_scale, conv_w, wmix, tm, seq):
    t = dx1.shape[0]
    nt = t // tm
    tps = seq // tm
    hb = tm // HALO

    def body(da_ref, z_ref, zph_ref, zcvh_ref, yp_ref, yc_ref, pw_ref, ps_ref, wpp_ref, cw_ref, wco_ref, wo_ref,
             dz_ref, mg_ref, p2_ref, u_ref, dyp_ref, dyc_ref, p_ref, dpw_ref, gvec_ref, cp_ref, cc_ref):
        i = pl.program_id(0)
        it = (nt - 1 - i) % tps
        keep_hist = jnp.where(it == 0, 0.0, 1.0)
        keep_next = jnp.where(it == tps - 1, 0.0, 1.0)
        pos = it * tm + lax.broadcasted_iota(jnp.int32, (tm, 1), 0)

        @pl.when(i == 0)
        def _():
            gvec_ref[...] = jnp.zeros_like(gvec_ref)
            cp_ref[...] = jnp.zeros_like(cp_ref)
            cc_ref[...] = jnp.zeros_like(cc_ref)

        dm = _dot_nt(da_ref[...].astype(BF16), wo_ref[...])
        merged, dyp, dyc = [], [], []
        for q in range(NG):
            lanes = slice(q * CG, (q + 1) * CG)
            sp = jax.nn.sigmoid(z_ref[16 + q].astype(F32))
            sc = jax.nn.sigmoid(z_ref[20 + q].astype(F32))
            yp = yp_ref[:, lanes].astype(F32)
            yc = yc_ref[:, lanes].astype(F32)
            dmq = dm[:, lanes]
            merged.append((sp * yp + sc * yc).astype(BF16))
            dyp.append((dmq * sp).astype(BF16))
            dyc.append((dmq * sc).astype(BF16))
            dz_ref[16 + q] = (dmq * yp * (sp * (1.0 - sp))).astype(BF16)
            dz_ref[20 + q] = (dmq * yc * (sc * (1.0 - sc))).astype(BF16)
        mg_ref[...] = jnp.concatenate(merged, axis=1)
        dypb = jnp.concatenate(dyp, axis=1)
        dycb = jnp.concatenate(dyc, axis=1)
        dyp_ref[...] = dypb
        dyc_ref[...] = dycb

        dp2 = _dot_nt(dypb, wpp_ref[...])
        p2 = []
        for g, win in enumerate(WINS):
            lanes = slice(g * CG, (g + 1) * CG)
            cnt = jnp.minimum(pos + 1, win).astype(F32)
            p = _pool_tile(z_ref, zph_ref, g, win, keep_hist, cnt)
            pb = p.astype(BF16)
            p_ref[g] = pb
            pw = _dot(pb, pw_ref[g])
            p2.append((pw * ps_ref[:, lanes]).astype(BF16))
            dp2g = dp2[:, lanes]
            gvec_ref[0:1, lanes] += jnp.sum(dp2g * pw, axis=0, keepdims=True)
            dpwb = (dp2g * ps_ref[:, lanes]).astype(BF16)
            dpw_ref[g] = dpwb
            dp = _dot_nt(dpwb, pw_ref[g])
            qv = dp / cnt
            ext = jnp.concatenate([qv, cp_ref[g] * keep_next], axis=0)
            cp_ref[g] = qv[:HALO]
            n = tm + HALO
            s, sh = ext, 1
            while sh < win:
                s = s + pltpu.roll(s, n - sh, 0)
                sh *= 2
            dz_ref[g] = (s[:tm] - dp).astype(BF16)
        p2_ref[...] = jnp.concatenate(p2, axis=1)

        du = _dot_nt(dycb, wco_ref[...])
        u = []
        for q in range(NG):
            lanes = slice(q * CG, (q + 1) * CG)
            zb = z_ref[4 + q].astype(F32)
            zc = z_ref[8 + q].astype(F32)
            zv = z_ref[12 + q].astype(F32)
            cv = zc * zv
            cvh = zcvh_ref[q].astype(F32) * zcvh_ref[4 + q].astype(F32) * keep_hist
            cc, cv1, cv2 = _conv_taps(jnp.concatenate([cvh, cv], axis=0), cv, cw_ref, lanes, HALO)
            u.append((zb * cc).astype(BF16))
            duq = du[:, lanes]
            dz_ref[4 + q] = (duq * cc).astype(BF16)
            dcc = duq * zb
            for tap, src in enumerate((cv2, cv1, cv)):
                gvec_ref[tap + 1:tap + 2, lanes] += jnp.sum(dcc * src, axis=0, keepdims=True)
            ext = jnp.concatenate([dcc, cc_ref[:, lanes] * keep_next], axis=0)
            cc_ref[:, lanes] = dcc[:8]
            dcv = (cw_ref[2:3, lanes] * dcc + cw_ref[1:2, lanes] * _shift_up(ext, 1, tm)
                   + cw_ref[0:1, lanes] * _shift_up(ext, 2, tm))
            dz_ref[8 + q] = (dcv * zv).astype(BF16)
            dz_ref[12 + q] = (dcv * zc).astype(BF16)
        u_ref[...] = jnp.concatenate(u, axis=1)

    def hist(i):
        return jnp.maximum((nt - 1 - i) * hb - 1, 0)

    rev = lambda i: (nt - 1 - i, 0)
    rev3 = lambda i: (0, nt - 1 - i, 0)
    const2 = lambda i: (0, 0)
    tok = jax.ShapeDtypeStruct((t, D), BF16)
    grp = jax.ShapeDtypeStruct((NG, t, CG), BF16)
    return pl.pallas_call(
        body, name="bwd_mix", grid=(nt,),
        in_specs=[pl.BlockSpec((tm, D), rev), pl.BlockSpec((NZT, tm, CG), rev3),
                  pl.BlockSpec((NG, HALO, CG), lambda i: (0, hist(i), 0)),
                  pl.BlockSpec((2 * NG, HALO, CG), lambda i: (1, hist(i), 0)),
                  pl.BlockSpec((tm, D), rev), pl.BlockSpec((tm, D), rev),
                  pl.BlockSpec((NG, CG, CG), lambda i: (0, 0, 0)), pl.BlockSpec((1, D), const2),
                  pl.BlockSpec((D, D), lambda i: (0, MIX_POOL_PROJ)), pl.BlockSpec((3, D), const2),
                  pl.BlockSpec((D, D), lambda i: (0, MIX_CONV_OUT)), pl.BlockSpec((D, D), lambda i: (0, MIX_O))],
        out_specs=[pl.BlockSpec((NZT, tm, CG), rev3)] + [pl.BlockSpec((tm, D), rev)] * 5
                  + [pl.BlockSpec((NG, tm, CG), rev3)] * 2 + [pl.BlockSpec((8, D), const2)],
        out_shape=[jax.ShapeDtypeStruct((NZT, t, CG), BF16), tok, tok, tok, tok, tok, grp, grp,
                   jax.ShapeDtypeStruct((8, D), F32)],
        scratch_shapes=[pltpu.VMEM((NG, HALO, CG), F32), pltpu.VMEM((8, D), F32)],
        compiler_params=_cparams(1, VMEM_BIG),
    )(dx1, z, z, z, y_pool, y_conv, pool_w, pool_scale, wmix, conv_w, wmix, wmix)


def bwd_in(dz, w_in_g, dx1, x, g1, tm):
    t = x.shape[0]

    def body(dz_ref, w_ref, dx1_ref, x_ref, g_ref, gx_ref, gn_ref):
        @pl.when(pl.program_id(0) == 0)
        def _():
            gn_ref[...] = jnp.zeros_like(gn_ref)

        dh = None
        for j in range(NDEV):
            dzc = jnp.concatenate([dz_ref[3 * j + q] for q in range(3)], axis=1)
            part = _dot_nt(dzc, w_ref[j])
            dh = part if dh is None else dh + part
        xv = x_ref[...]
        inv = _rms_inv(xv)
        xn = xv * inv
        gn_ref[0:1, :] += jnp.sum(dh * xn, axis=0, keepdims=True)
        gx_ref[...] = dx1_ref[...] + _rms_bwd(dh, xn, inv, g_ref[...])

    tile = lambda i: (i, 0)
    return pl.pallas_call(
        body, name="bwd_in", grid=(t // tm,),
        in_specs=[pl.BlockSpec((NZT, tm, CG), lambda i: (0, i, 0)),
                  pl.BlockSpec((NDEV, D, SH_IN), lambda i: (0, 0, 0)),
                  pl.BlockSpec((tm, D), tile), pl.BlockSpec((tm, D), tile), pl.BlockSpec((1, D), lambda i: (0, 0))],
        out_specs=[pl.BlockSpec((tm, D), tile), pl.BlockSpec((8, D), lambda i: (0, 0))],
        out_shape=[jax.ShapeDtypeStruct((t, D), F32), jax.ShapeDtypeStruct((8, D), F32)],
        compiler_params=_cparams(1, VMEM_BIG),
    )(dz, w_in_g, dx1, x, g1)


def _slot(j):
    return j % 2, j // 2


def wgrad_cols(at, b, q, name, tk):
    m, t = at.shape
    width = b.shape[3]

    def body(a_ref, b_ref, o_ref):
        @pl.when(pl.program_id(1) == 0)
        def _():
            o_ref[...] = jnp.zeros_like(o_ref)

        o_ref[...] += _dot(a_ref[...], b_ref[...])

    return pl.pallas_call(
        body, name=name, grid=(NDEV, t // tk),
        in_specs=[pl.BlockSpec((m, tk), lambda j, k: (0, k)),
                  pl.BlockSpec((None, None, tk, width), lambda j, k: (j, q, k, 0))],
        out_specs=pl.BlockSpec((None, None, m, width), lambda j, k: (j % 2, j // 2, 0, 0)),
        out_shape=jax.ShapeDtypeStruct((2, 4, m, width), F32),
        compiler_params=_cparams(2, VMEM_BIG),
    )(at, b)


def wgrad_cols_resident(at, b, q, name, tk):
    m, t = at.shape
    width = b.shape[3]

    def body(a_ref, b_ref, o_ref):
        k, j = pl.program_id(0), pl.program_id(1)

        @pl.when((k == 0) & (j == 0))
        def _():
            o_ref[...] = jnp.zeros_like(o_ref)

        o_ref[j % 2, j // 2] += _dot(a_ref[...], b_ref[...])

    return pl.pallas_call(
        body, name=name, grid=(t // tk, NDEV),
        in_specs=[pl.BlockSpec((m, tk), lambda k, j: (0, k)),
                  pl.BlockSpec((None, None, tk, width), lambda k, j: (j, q, k, 0))],
        out_specs=pl.BlockSpec((2, 4, m, width), lambda k, j: (0, 0, 0, 0)),
        out_shape=jax.ShapeDtypeStruct((2, 4, m, width), F32),
        compiler_params=_cparams(2, VMEM_BIG),
    )(at, b)


def wgrad_down(actt, dx2, tk):
    t = dx2.shape[0]

    def body(a_ref, b_ref, o_ref, acc_ref):
        kt = pl.program_id(1)

        @pl.when(kt == 0)
        def _():
            acc_ref[...] = jnp.zeros_like(acc_ref)

        acc_ref[...] += _dot(a_ref[...], b_ref[...].astype(BF16))

        @pl.when(kt == pl.num_programs(1) - 1)
        def _():
            o_ref[0] = acc_ref[:SH_DN]
            o_ref[1] = acc_ref[SH_DN:]

    return pl.pallas_call(
        body, name="wgrad_down", grid=(NCH, t // tk),
        in_specs=[pl.BlockSpec((None, SH_UP, tk), lambda k, kt: (k, 0, kt)), pl.BlockSpec((tk, D), lambda k, kt: (kt, 0))],
        out_specs=pl.BlockSpec((2, None, SH_DN, D), lambda k, kt: (0, k, 0, 0)),
        out_shape=jax.ShapeDtypeStruct((2, 4, SH_DN, D), F32),
        scratch_shapes=[pltpu.VMEM((SH_UP, D), F32)],
        compiler_params=_cparams(2, VMEM_BIG),
    )(actt, dx2)


def wgrad_square(a, b, name, tk):
    t = a.shape[0]

    def body(a_ref, b_ref, o_ref, acc_ref):
        kt = pl.program_id(0)

        @pl.when(kt == 0)
        def _():
            acc_ref[...] = jnp.zeros_like(acc_ref)

        acc_ref[...] += _dot_tn(a_ref[...], b_ref[...].astype(BF16))

        @pl.when(kt == pl.num_programs(0) - 1)
        def _():
            for j in range(NDEV):
                cc, xy = _slot(j)
                o_ref[cc, xy] = acc_ref[j * 128:(j + 1) * 128]

    return pl.pallas_call(
        body, name=name, grid=(t // tk,),
        in_specs=[pl.BlockSpec((tk, D), lambda k: (k, 0)), pl.BlockSpec((tk, D), lambda k: (k, 0))],
        out_specs=pl.BlockSpec((2, 4, 128, D), lambda k: (0, 0, 0, 0)),
        out_shape=jax.ShapeDtypeStruct((2, 4, 128, D), F32),
        scratch_shapes=[pltpu.VMEM((D, D), F32)],
        compiler_params=_cparams(1, VMEM_BIG),
    )(a, b)


def wgrad_pool(p, dpw, tk):
    t = p.shape[1]

    def body(a_ref, b_ref, o_ref):
        @pl.when(pl.program_id(0) == 0)
        def _():
            o_ref[...] = jnp.zeros_like(o_ref)

        for g in range(NG):
            o_ref[g] += _dot_tn(a_ref[g], b_ref[g])

    return pl.pallas_call(
        body, name="wgrad_pool", grid=(t // tk,),
        in_specs=[pl.BlockSpec((NG, tk, CG), lambda k: (0, k, 0))] * 2,
        out_specs=pl.BlockSpec((NG, CG, CG), lambda k: (0, 0, 0)),
        out_shape=jax.ShapeDtypeStruct((NG, CG, CG), F32),
        compiler_params=_cparams(1, VMEM_BIG),
    )(p, dpw)


def _adamw(w, g, m, v):
    m = ADAM_B1 * m + (1.0 - ADAM_B1) * g
    v = ADAM_B2 * v + (1.0 - ADAM_B2) * (g * g)
    m_hat = m / (1.0 - ADAM_B1 ** ADAM_STEP)
    v_hat = v / (1.0 - ADAM_B2 ** ADAM_STEP)
    delta = -ADAM_LR * (m_hat / (jnp.sqrt(v_hat) + ADAM_EPS) + ADAM_WD * w)
    return delta, m, v


def _row_block(r):
    return 512 if r % 512 == 0 else r


def chip_partial(place, g, from_sibling, name):
    _, _, r, c = g.shape

    def body(place_ref, g_ref, s_ref, o_ref):
        o_ref[...] = (g_ref[...] + s_ref[...]).astype(BF16)

    return pl.pallas_call(
        body, name=name,
        grid_spec=pltpu.PrefetchScalarGridSpec(
            num_scalar_prefetch=1, grid=(3,),
            in_specs=[pl.BlockSpec((None, None, r, c), lambda k, pr: (pr[0], pr[1] ^ (k + 1), 0, 0)),
                      pl.BlockSpec((None, r, c), lambda k, pr: (pr[1] ^ (k + 1), 0, 0))],
            out_specs=pl.BlockSpec((None, r, c), lambda k, pr: (pr[1] ^ (k + 1), 0, 0))),
        out_shape=jax.ShapeDtypeStruct((4, r, c), BF16),
        compiler_params=_cparams(1, VMEM_BIG),
    )(place, g, from_sibling)


def finish_adamw(place, gs, from_sibling, from_chips, w, m, v, name):
    n = len(gs)
    r = gs[0].shape[2]
    widths = [g.shape[3] for g in gs]
    c = sum(widths)
    br = _row_block(r)

    def body(place_ref, *refs):
        g_refs, s_refs, c_refs = refs[:n], refs[n:2 * n], refs[2 * n:5 * n]
        w_ref, m_ref, v_ref, og_ref, od_ref, om_ref, ov_ref = refs[5 * n:]
        cols = []
        for q in range(n):
            grad = g_refs[q][...] + s_refs[q][...]
            for k in range(3):
                grad = grad + c_refs[3 * q + k][...].astype(F32)
            cols.append(grad)
        grad = cols[0] if n == 1 else jnp.concatenate(cols, axis=1)
        og_ref[...] = grad
        od_ref[...], om_ref[...], ov_ref[...] = _adamw(w_ref[...], grad, m_ref[...], v_ref[...])

    def other(k, cq):
        return pl.BlockSpec((None, br, cq), lambda i, pr: (pr[1] ^ k, i, 0))

    row = pl.BlockSpec((br, c), lambda i, pr: (i, 0))
    out = jax.ShapeDtypeStruct((r, c), F32)
    in_specs = [pl.BlockSpec((None, None, br, cq), lambda i, pr: (pr[0], pr[1], i, 0)) for cq in widths]
    in_specs += [pl.BlockSpec((None, br, cq), lambda i, pr: (pr[1], i, 0)) for cq in widths]
    in_specs += [other(k, cq) for cq in widths for k in (1, 2, 3)]
    return pl.pallas_call(
        body, name=name,
        grid_spec=pltpu.PrefetchScalarGridSpec(
            num_scalar_prefetch=1, grid=(r // br,), in_specs=in_specs + [row, row, row], out_specs=[row] * 4),
        out_shape=[out] * 4,
        compiler_params=_cparams(1, VMEM_BIG),
    )(place, *gs, *from_sibling, *[fc for fc in from_chips for _ in range(3)], w, m, v)


def adamw_small(items):
    n = len(items)

    def body(*refs):
        ins, outs = refs[:4 * n], refs[4 * n:]
        for i in range(n):
            w, g, m, v = (r[...] for r in ins[4 * i:4 * i + 4])
            outs[3 * i][...], outs[3 * i + 1][...], outs[3 * i + 2][...] = _adamw(w, g, m, v)

    out = [jax.ShapeDtypeStruct(it[0].shape, F32) for it in items for _ in range(3)]
    res = pl.pallas_call(body, name="adamw_small", out_shape=out)(*[a for it in items for a in it])
    return [res[3 * i:3 * i + 3] for i in range(n)]


def kernel(x, norm_mix, w_in, pool_w, pool_scale, w_pool_proj, conv_w, w_conv_out, w_o, norm_ffn, w_up, ffn_conv_w, ffn_conv_b, w_down, norm_final, loss_target, m_norm_mix, m_w_in, m_pool_w, m_pool_scale, m_w_pool_proj, m_conv_w, m_w_conv_out, m_w_o, m_norm_ffn, m_w_up, m_ffn_conv_w, m_ffn_conv_b, m_w_down, m_norm_final, v_norm_mix, v_w_in, v_pool_w, v_pool_scale, v_w_pool_proj, v_conv_w, v_w_conv_out, v_w_o, v_norm_ffn, v_w_up, v_ffn_conv_w, v_ffn_conv_b, v_w_down, v_norm_final):
    nb, seq, _ = x.shape
    t = nb * seq
    tm_in = min(TM_IN, t)
    tm_mix = min(TM_MIX, seq)
    tm_ffn = min(TM_FFN, seq)
    tk = min(TK_WGRAD, t)
    xt = x.reshape(t, D)
    tgt = loss_target.reshape(t, D)
    xi, yi, ci = _pos()
    me = 4 * xi + 2 * yi + ci
    place = jnp.stack([ci, 2 * xi + yi]).astype(jnp.int32)

    tie = lax.optimization_barrier
    w_in_g, = all_gather_blocks([w_in[0].astype(BF16)], "all_gather_w_in", 0)
    taps = (jnp.pad(conv_w[0], ((0, 5), (0, D - 128))) + jnp.pad(ffn_conv_w[0], ((3, 2), (0, D - SH_UP))))
    taps_g = _exchange_small(taps, False, "all_gather_taps")
    mix_shard = jnp.concatenate(
        [w_pool_proj[0], w_conv_out[0], w_o[0], pool_w[0].reshape(NG * 32, CG)], axis=1).astype(BF16)
    mix_shard, taps_g = tie((mix_shard, taps_g))
    wmix_g, = all_gather_blocks([mix_shard], "all_gather_w_mix", 0)
    ffn_shards, w_in_g = tie(([w_up[0].astype(BF16), w_down[0].astype(BF16)], w_in_g))
    w_up_g, = all_gather_blocks(ffn_shards[:1], "all_gather_w_up", 0)
    w_dn_g, = all_gather_blocks(ffn_shards[1:], "all_gather_w_down", 0)
    wmix = wmix_g.reshape(D, MIX_COLS)
    pool_w_f = wmix_g[:, :, 3 * D:].reshape(NDEV, NG, 32, CG).transpose(1, 0, 2, 3).reshape(NG, CG, CG)
    w_dn_f = w_dn_g.reshape(NCH, SH_UP, D)
    conv_w_f = taps_g[:, 0:3, :128].transpose(1, 0, 2).reshape(3, D)
    fcw_f = taps_g[:, 3:6, :SH_UP]
    fcb_f = ffn_conv_b.reshape(NDEV, 1, SH_UP)
    gfin = norm_final.reshape(1, D)

    z, h1 = fwd_in(xt, norm_mix, w_in_g, tm_in)
    x1, y_pool, y_conv = fwd_mix(z, xt, pool_w_f, pool_scale, conv_w_f, wmix, tm_mix, seq)
    up, pre, act_tok, act, h2 = fwd_up(x1, norm_ffn, w_up_g, fcw_f, fcb_f, tm_ffn, seq, FFN_CHUNKS_PER_STEP)
    dx2, ffn_vec = fwd_down(x1, act_tok, w_dn_f, gfin, tgt, min(TM_IN, t))

    def to_sibling(full, tag):
        return reduce_scatter_d2d(full, "reduce_scatter_d2d_" + tag, 1)

    def partials(full, from_sib, names):
        return [chip_partial(place, g, s, "chip_partial_" + nm) for g, s, nm in zip(full, from_sib, names)]

    def to_chips(parts, tag):
        return reduce_scatter_ici(parts, "reduce_scatter_ici_" + tag, 2)

    def finish(nm, gs, from_sib, from_chips, wmv):
        w, m, v = wmv
        rc = (gs[0].shape[2], sum(g.shape[3] for g in gs))
        outs = finish_adamw(place, gs, from_sib, from_chips, w.reshape(rc), m.reshape(rc), v.reshape(rc), "adamw_" + nm)
        return [o.reshape(w.shape) for o in outs]

    def after(x, dep):
        return tie((x, dep))[0]

    big = {}
    gw_dn = wgrad_down(act, dx2, tk)
    sib_dn = to_sibling([gw_dn], "w_down")
    d_up, dx1, g_ffn_vec, g_nffn = bwd_ffn(dx2, x1, norm_ffn, up, pre, w_up_g, fcw_f, w_dn_f, tm_ffn, seq,
                                           FFN_CHUNKS_PER_STEP)
    d_up, part_dn = tie((d_up, partials([gw_dn], sib_dn, ["w_down"])))
    chips_dn = to_chips(part_dn, "w_down")
    gw_up = wgrad_cols(h2, d_up.reshape(NDEV, 1, t, SH_UP), 0, "wgrad_up", tk)
    sib_up = to_sibling([after(gw_up, chips_dn)], "w_up")
    dz, merged, p2, u, dyp, dyc, p, dpw, g_mix_vec = bwd_mix(
        dx1, z, y_pool, y_conv, pool_w_f, pool_scale, conv_w_f, wmix, tm_mix, seq)
    merged, part_up = tie((merged, partials([gw_up], sib_up, ["w_up"])))
    chips_up = to_chips(part_up, "w_up")
    small_g, = all_gather_blocks(
        [after(jnp.concatenate([g_mix_vec, g_nffn, ffn_vec, g_ffn_vec.reshape(8 * NDEV, D)], axis=0), chips_up)],
        "all_gather_small", 0)
    gw_o = wgrad_square(merged, dx1, "wgrad_o", tk)
    gw_pp = wgrad_square(p2, dyp, "wgrad_pool_proj", tk)
    gw_co = wgrad_square(u, dyc, "wgrad_conv_out", tk)
    gw_pool = wgrad_pool(p, dpw, tk).reshape(NG, 4, 2, 32, CG).transpose(2, 1, 0, 3, 4).reshape(2, 4, NG * 32, CG)
    dz8 = dz.reshape(NDEV, 3, t, CG)
    gw_in, sib_in, chips_in = [None] * 3, [None] * 3, [None] * 3
    gw_in[0] = wgrad_cols_resident(h1, dz8, 0, "wgrad_in_0", tk)
    sib_a = to_sibling(after([gw_o, gw_pp], (chips_up, gw_in[0])), "mix_a")
    sib_b = to_sibling(after([gw_co, gw_pool], sib_a), "mix_b")
    sib_in[0] = to_sibling(after([gw_in[0]], sib_b), "w_in_0")
    gw_in[1] = wgrad_cols_resident(h1, dz8, 1, "wgrad_in_1", tk)
    sib_in[1] = to_sibling(after([gw_in[1]], sib_in[0]), "w_in_1")
    h1, part_a, part_b, part_in0 = tie((h1, partials([gw_o, gw_pp], sib_a, ["w_o", "w_pool_proj"]),
                                        partials([gw_co, gw_pool], sib_b, ["w_conv_out", "pool_w"]),
                                        partials([gw_in[0]], sib_in[0], ["w_in_0"])))
    chips_a = to_chips(after(part_a, sib_in[1]), "mix_a")
    chips_b = to_chips(part_b, "mix_b")
    chips_in[0] = to_chips(part_in0, "w_in_0")
    h1, big["w_down"], big["w_up"] = tie((
        h1, finish("w_down", [gw_dn], sib_dn, chips_dn, (w_down, m_w_down, v_w_down)),
        finish("w_up", [gw_up], sib_up, chips_up, (w_up, m_w_up, v_w_up))))
    gw_in[2] = wgrad_cols_resident(h1, dz8, 2, "wgrad_in_2", tk)
    sib_in[2] = to_sibling(after([gw_in[2]], (chips_a, chips_b, chips_in[0])), "w_in_2")
    dx1, part_in1, part_in2, big["w_o"], big["w_pool_proj"], big["w_conv_out"], big["pool_w"] = tie((
        dx1, partials([gw_in[1]], sib_in[1], ["w_in_1"]), partials([gw_in[2]], sib_in[2], ["w_in_2"]),
        finish("w_o", [gw_o], sib_a[:1], chips_a[:1], (w_o, m_w_o, v_w_o)),
        finish("w_pool_proj", [gw_pp], sib_a[1:], chips_a[1:], (w_pool_proj, m_w_pool_proj, v_w_pool_proj)),
        finish("w_conv_out", [gw_co], sib_b[:1], chips_b[:1], (w_conv_out, m_w_conv_out, v_w_conv_out)),
        finish("pool_w", [gw_pool], sib_b[1:], chips_b[1:], (pool_w, m_pool_w, v_pool_w))))
    chips_in[1] = to_chips(after(part_in1, sib_in[2]), "w_in_1")
    chips_in[2] = to_chips(part_in2, "w_in_2")
    grad_x, g_nmix = bwd_in(dz, w_in_g, dx1, xt, norm_mix, min(TM_BWD_IN, t))
    grad_x, chips_in = tie((grad_x, chips_in))
    big["w_in"] = finish("w_in", gw_in, [s[0] for s in sib_in], [c[0] for c in chips_in], (w_in, m_w_in, v_w_in))

    red_n, red = _exchange_small(g_nmix, True, "all_reduce_small", gathered=small_g)
    g_norm_mix, g_pool_scale, g_norm_ffn = red_n[0:1], red[0:1], red[8:9]
    g_conv_w = lax.dynamic_slice(red, (1, me * 128), (3, 128))
    g_norm_final = red[16]
    loss = red[17, 0]
    g_fcb = red[24:].reshape(NDEV, 8, D)[:, 0, :SH_UP].reshape(1, FF2)
    g_fcw = lax.dynamic_slice(red, (25 + 8 * me, 0), (3, SH_UP))
    grads = {"norm_mix": g_norm_mix, "pool_scale": g_pool_scale, "norm_ffn": g_norm_ffn, "norm_final": g_norm_final,
             "ffn_conv_b": g_fcb, "conv_w": g_conv_w.reshape(1, 3, 128), "ffn_conv_w": g_fcw.reshape(1, 3, SH_UP)}
    small_wmv = {"norm_mix": (norm_mix, m_norm_mix, v_norm_mix), "pool_scale": (pool_scale, m_pool_scale, v_pool_scale),
                 "norm_ffn": (norm_ffn, m_norm_ffn, v_norm_ffn), "norm_final": (norm_final, m_norm_final, v_norm_final),
                 "ffn_conv_b": (ffn_conv_b, m_ffn_conv_b, v_ffn_conv_b), "conv_w": (conv_w, m_conv_w, v_conv_w),
                 "ffn_conv_w": (ffn_conv_w, m_ffn_conv_w, v_ffn_conv_w)}
    small_names = list(small_wmv)
    flat2 = lambda a: a.reshape(-1, a.shape[-1])
    small_out = adamw_small([(flat2(small_wmv[nm][0]), flat2(grads[nm]), flat2(small_wmv[nm][1]),
                              flat2(small_wmv[nm][2])) for nm in small_names])
    small = {nm: [o.reshape(small_wmv[nm][0].shape) for o in outs] for nm, outs in zip(small_names, small_out)}

    order = ["norm_mix", "w_in", "pool_w", "pool_scale", "w_pool_proj", "conv_w", "w_conv_out", "w_o", "norm_ffn",
             "w_up", "ffn_conv_w", "ffn_conv_b", "w_down", "norm_final"]
    out = [loss, grad_x.reshape(nb, seq, D)]
    out += [big[nm][0] if nm in big else grads[nm] for nm in order]
    for idx in range(3):
        out += [big[nm][idx + 1] if nm in big else small[nm][idx] for nm in order]
    return tuple(out)
```

```python
import functools

import jax
import jax.numpy as jnp
from jax import lax
from jax.experimental import pallas as pl
from jax.experimental.pallas import tpu as pltpu
from jax.experimental.pallas import tpu_sc as plsc

F32 = jnp.float32
BF16 = jnp.bfloat16

NDEV = 8
D = 1024
NG = 4
CG = 256
WINS = (2, 4, 8, 16)
DIN = 6 * D
SH_IN = DIN // NDEV
NZT = DIN // CG
FF2 = 5632
SH_UP = FF2 // NDEV
FF = FF2 // 2
NCH = 4
SH_DN = FF // NDEV
RMS_EPS = 1e-6
HALO = 16

ADAM_LR = 0.001
ADAM_B1 = 0.9
ADAM_B2 = 0.999
ADAM_EPS = 1e-08
ADAM_WD = 0.01
ADAM_STEP = 10

TM_IN = 512
TM_BWD_IN = 256
TM_MIX = 256
TM_FFN = 256
FFN_CHUNKS_PER_STEP = 4
TK_WGRAD = 2048
MIX_POOL_PROJ, MIX_CONV_OUT, MIX_O = 0, 1, 2
MIX_COLS = 3 * D + CG
VMEM_BIG = 56 * 1024 * 1024
MESH = pl.DeviceIdType.MESH
ANY = pl.BlockSpec(memory_space=pl.ANY)


def _cparams(n_axes, vmem=None):
    return pltpu.CompilerParams(dimension_semantics=("arbitrary",) * n_axes, vmem_limit_bytes=vmem)


def _dot(a, b):
    return jnp.dot(a, b, preferred_element_type=F32)


def _dot_nt(a, b):
    return lax.dot_general(a, b, (((1,), (1,)), ((), ())), preferred_element_type=F32)


def _dot_tn(a, b):
    return lax.dot_general(a, b, (((0,), (0,)), ((), ())), preferred_element_type=F32)


def _shift_down(ext, s, lead):
    return pltpu.roll(ext, s, 0)[lead:]


def _shift_up(ext, s, tm):
    n = ext.shape[0]
    return pltpu.roll(ext, n - s, 0)[:tm]


def _rms_inv(x):
    return lax.rsqrt(jnp.mean(x * x, axis=-1, keepdims=True) + RMS_EPS)


def _rms_bwd(dh, xn, inv, g):
    dxn = dh * g
    return inv * (dxn - xn * jnp.mean(dxn * xn, axis=-1, keepdims=True))


def _pos():
    return lax.axis_index("x"), lax.axis_index("y"), lax.axis_index("c")


def _handshake(peers):
    barrier = pltpu.get_barrier_semaphore()
    for peer in peers:
        pl.semaphore_signal(barrier, inc=1, device_id=peer, device_id_type=MESH)
    pl.semaphore_wait(barrier, len(peers))


def _sequencer(body, out_type, n_sems, name, collective_id):
    return pl.kernel(
        body, out_type=out_type, mesh=plsc.ScalarSubcoreMesh(axis_name="sequencer", num_cores=1), name=name,
        scratch_types=[pltpu.SemaphoreType.DMA((n_sems,)), pltpu.SemaphoreType.DMA((n_sems,))],
        compiler_params=pltpu.CompilerParams(collective_id=collective_id))


def all_gather_blocks(shards, name, collective_id):
    n = len(shards)

    def body(*refs):
        ins, outs = refs[:n], refs[n:2 * n]
        send_sems, recv_sems = refs[2 * n:]
        x, y, c = _pos()
        sibling = (x, y, 1 - c)
        chips = [(1 - x, y), (x, 1 - y), (1 - x, 1 - y)]
        _handshake([sibling] + [(*chip, c) for chip in chips])

        def copy(w, k, block, to, src=None):
            slot = outs[w].at[4 * block[0] + 2 * block[1] + block[2]]
            return pltpu.make_async_remote_copy(
                src_ref=slot if src is None else src, dst_ref=slot,
                send_sem=send_sems.at[8 * w + k], recv_sem=recv_sems.at[8 * w + k], device_id=to, device_id_type=MESH)

        mine, first, passed = [], [], []
        for w in range(n):
            m = pltpu.make_async_copy(ins[w], outs[w].at[4 * x + 2 * y + c], send_sems.at[8 * w + 7])
            m.start()
            mine.append(m)
            first.append(copy(w, 0, (x, y, c), sibling, src=ins[w]))
            first += [copy(w, 1 + j, (x, y, c), (*chip, c), src=ins[w]) for j, chip in enumerate(chips)]
        for cp in first:
            cp.start()
        for w in range(n):
            for j, chip in enumerate(chips):
                copy(w, 1 + j, (*chip, c), (x, y, c)).wait_recv()
                fw = copy(w, 4 + j, (*chip, c), sibling)
                fw.start()
                passed.append(fw)
        for w in range(n):
            copy(w, 0, (x, y, 1 - c), (x, y, c)).wait_recv()
            for j, chip in enumerate(chips):
                copy(w, 4 + j, (*chip, 1 - c), (x, y, c)).wait_recv()
        for cp in first + passed:
            cp.wait_send()
        for m in mine:
            m.wait()

    out = [jax.ShapeDtypeStruct((NDEV,) + s.shape, s.dtype) for s in shards]
    return _sequencer(body, out, 8 * n, name, collective_id)(*shards)


def _exchange_small(v, reduce, name, gathered=None):
    rows = v.shape[0]

    def body(*refs):
        if gathered is None:
            v_ref, out_ref, slots, send_sems, recv_sems, local_sem = refs
        else:
            v_ref, g_ref, out_ref, gsum_ref, slots, send_sems, recv_sems, local_sem = refs
        x, y, c = _pos()
        me = 4 * x + 2 * y + c
        mine = pltpu.make_async_copy(v_ref, slots.at[me], local_sem)
        mine.start()
        offs = [(dx, dy, dc) for dx in (0, 1) for dy in (0, 1) for dc in (0, 1)][1:]

        def copy(k, src_slot, to):
            return pltpu.make_async_remote_copy(
                src_ref=v_ref, dst_ref=slots.at[src_slot], send_sem=send_sems.at[k], recv_sem=recv_sems.at[k],
                device_id=to, device_id_type=MESH)

        sends = []
        for k, (dx, dy, dc) in enumerate(offs):
            cp = copy(k, me, (x ^ dx, y ^ dy, c ^ dc))
            cp.start()
            sends.append(cp)
        for k, (dx, dy, dc) in enumerate(offs):
            copy(k, 4 * (x ^ dx) + 2 * (y ^ dy) + (c ^ dc), (x, y, c)).wait_recv()
        for cp in sends:
            cp.wait_send()
        mine.wait()
        if reduce:
            acc = slots[0]
            for d in range(1, NDEV):
                acc = acc + slots[d]
            out_ref[...] = acc
        else:
            out_ref[...] = slots[...]
        if gathered is not None:
            acc = g_ref[0]
            for d in range(1, NDEV):
                acc = acc + g_ref[d]
            gsum_ref[...] = acc

    vmem = pl.BlockSpec(memory_space=pltpu.VMEM)
    out = jax.ShapeDtypeStruct((rows, D) if reduce else (NDEV, rows, D), F32)
    args, out_shape, out_specs = [v], out, vmem
    if gathered is not None:
        args.append(gathered)
        out_shape, out_specs = [out, jax.ShapeDtypeStruct(gathered.shape[1:], F32)], [vmem, vmem]
    return pl.pallas_call(
        body, name=name, out_shape=out_shape, in_specs=[vmem] * len(args), out_specs=out_specs,
        scratch_shapes=[pltpu.VMEM((NDEV, rows, D), F32), pltpu.SemaphoreType.DMA((7,)),
                        pltpu.SemaphoreType.DMA((7,)), pltpu.SemaphoreType.DMA],
    )(*args)


def reduce_scatter_d2d(grads, name, collective_id):
    n = len(grads)

    def body(*refs):
        ins, outs = refs[:n], refs[n:2 * n]
        send_sems, recv_sems = refs[2 * n:]
        x, y, c = _pos()
        _handshake([(x, y, 1 - c)])
        cps = []
        for w in range(n):
            cp = pltpu.make_async_remote_copy(
                src_ref=ins[w].at[1 - c], dst_ref=outs[w], send_sem=send_sems.at[w], recv_sem=recv_sems.at[w],
                device_id=(x, y, 1 - c), device_id_type=MESH)
            cp.start()
            cps.append(cp)
        for cp in cps:
            cp.wait_recv()
        for cp in cps:
            cp.wait_send()

    out = [jax.ShapeDtypeStruct(g.shape[1:], F32) for g in grads]
    return _sequencer(body, out, n, name, collective_id)(*grads)


def reduce_scatter_ici(parts, name, collective_id):
    n = len(parts)

    def body(*refs):
        ins, outs = refs[:n], refs[n:2 * n]
        send_sems, recv_sems = refs[2 * n:]
        x, y, c = _pos()
        offs = [(1, 0), (0, 1), (1, 1)]
        _handshake([(x ^ dx, y ^ dy, c) for dx, dy in offs])
        cps = []
        for w in range(n):
            for k, (dx, dy) in enumerate(offs):
                ox, oy = x ^ dx, y ^ dy
                cp = pltpu.make_async_remote_copy(
                    src_ref=ins[w].at[2 * ox + oy], dst_ref=outs[w].at[2 * x + y],
                    send_sem=send_sems.at[3 * w + k], recv_sem=recv_sems.at[3 * w + k],
                    device_id=(ox, oy, c), device_id_type=MESH)
                cp.start()
                cps.append((cp, w, k, ox, oy))
        for cp, w, k, ox, oy in cps:
            pltpu.make_async_remote_copy(
                src_ref=ins[w].at[2 * ox + oy], dst_ref=outs[w].at[2 * ox + oy],
                send_sem=send_sems.at[3 * w + k], recv_sem=recv_sems.at[3 * w + k],
                device_id=(ox, oy, c), device_id_type=MESH).wait_recv()
        for cp, *_ in cps:
            cp.wait_send()

    out = [jax.ShapeDtypeStruct(p.shape, BF16) for p in parts]
    return _sequencer(body, out, 3 * n, name, collective_id)(*parts)


def fwd_in(x, g1, w_in_g, tm):
    t = x.shape[0]

    def body(x_ref, g_ref, w_ref, z_ref, ht_ref):
        xf = x_ref[...]
        h = (xf * _rms_inv(xf) * g_ref[...]).astype(BF16)
        ht_ref[...] = h.T
        for j in range(NDEV):
            r = _dot(h, w_ref[j])
            for q in range(3):
                z_ref[3 * j + q] = r[:, q * CG:(q + 1) * CG].astype(BF16)

    return pl.pallas_call(
        body, name="fwd_in", grid=(t // tm,),
        in_specs=[pl.BlockSpec((tm, D), lambda i: (i, 0)), pl.BlockSpec((1, D), lambda i: (0, 0)),
                  pl.BlockSpec((NDEV, D, SH_IN), lambda i: (0, 0, 0))],
        out_specs=[pl.BlockSpec((NZT, tm, CG), lambda i: (0, i, 0)), pl.BlockSpec((D, tm), lambda i: (0, i))],
        out_shape=[jax.ShapeDtypeStruct((NZT, t, CG), BF16), jax.ShapeDtypeStruct((D, t), BF16)],
        compiler_params=_cparams(1, VMEM_BIG),
        cost_estimate=pl.CostEstimate(flops=2 * t * D * DIN, transcendentals=t,
                                      bytes_accessed=4 * t * D + 2 * D * DIN + 2 * t * DIN + 2 * t * D),
    )(x, g1, w_in_g)


def _pool_tile(z_ref, zh_ref, g, win, keep_hist, cnt):
    zt = z_ref[g].astype(F32)
    ext = jnp.concatenate([zh_ref[g].astype(F32) * keep_hist, zt], axis=0)
    s, sh = ext, 1
    while sh < win:
        s = s + pltpu.roll(s, sh, 0)
        sh *= 2
    return s[HALO:] / cnt - zt


def _conv_taps(ext, cur, w_ref, lanes, lead):
    x1 = _shift_down(ext, 1, lead)
    x2 = _shift_down(ext, 2, lead)
    out = w_ref[2:3, lanes] * cur + w_ref[1:2, lanes] * x1 + w_ref[0:1, lanes] * x2
    return out, x1, x2


def fwd_mix(z, x, pool_w, pool_scale, conv_w, wmix, tm, seq):
    t = x.shape[0]
    tps = seq // tm
    hb = tm // HALO

    def body(z_ref, zph_ref, zcvh_ref, x_ref, pw_ref, ps_ref, wpp_ref, cw_ref, wco_ref, wo_ref,
             x1_ref, yp_ref, yc_ref):
        it = pl.program_id(0) % tps
        keep_hist = jnp.where(it == 0, 0.0, 1.0)
        pos = it * tm + lax.broadcasted_iota(jnp.int32, (tm, 1), 0)
        p2 = []
        for g, win in enumerate(WINS):
            cnt = jnp.minimum(pos + 1, win).astype(F32)
            p = _pool_tile(z_ref, zph_ref, g, win, keep_hist, cnt)
            lanes = slice(g * CG, (g + 1) * CG)
            p2.append((_dot(p.astype(BF16), pw_ref[g]) * ps_ref[:, lanes]).astype(BF16))
        y_pool = _dot(jnp.concatenate(p2, axis=1), wpp_ref[...])
        u = []
        for q in range(NG):
            lanes = slice(q * CG, (q + 1) * CG)
            cv = z_ref[8 + q].astype(F32) * z_ref[12 + q].astype(F32)
            cvh = zcvh_ref[q].astype(F32) * zcvh_ref[4 + q].astype(F32) * keep_hist
            cc, _, _ = _conv_taps(jnp.concatenate([cvh, cv], axis=0), cv, cw_ref, lanes, HALO)
            u.append((z_ref[4 + q].astype(F32) * cc).astype(BF16))
        y_conv = _dot(jnp.concatenate(u, axis=1), wco_ref[...])
        ypb, ycb = y_pool.astype(BF16), y_conv.astype(BF16)
        yp_ref[...] = ypb
        yc_ref[...] = ycb
        merged = []
        for q in range(NG):
            lanes = slice(q * CG, (q + 1) * CG)
            sp = jax.nn.sigmoid(z_ref[16 + q].astype(F32))
            sc = jax.nn.sigmoid(z_ref[20 + q].astype(F32))
            merged.append((sp * ypb[:, lanes].astype(F32) + sc * ycb[:, lanes].astype(F32)).astype(BF16))
        x1_ref[...] = x_ref[...] + _dot(jnp.concatenate(merged, axis=1), wo_ref[...])

    def hist(i):
        return jnp.maximum(i * hb - 1, 0)

    const2 = lambda i: (0, 0)
    return pl.pallas_call(
        body, name="fwd_mix", grid=(t // tm,),
        in_specs=[pl.BlockSpec((NZT, tm, CG), lambda i: (0, i, 0)),
                  pl.BlockSpec((NG, HALO, CG), lambda i: (0, hist(i), 0)),
                  pl.BlockSpec((2 * NG, HALO, CG), lambda i: (1, hist(i), 0)),
                  pl.BlockSpec((tm, D), lambda i: (i, 0)),
                  pl.BlockSpec((NG, CG, CG), lambda i: (0, 0, 0)), pl.BlockSpec((1, D), const2),
                  pl.BlockSpec((D, D), lambda i: (0, MIX_POOL_PROJ)), pl.BlockSpec((3, D), const2),
                  pl.BlockSpec((D, D), lambda i: (0, MIX_CONV_OUT)), pl.BlockSpec((D, D), lambda i: (0, MIX_O))],
        out_specs=[pl.BlockSpec((tm, D), lambda i: (i, 0))] * 3,
        out_shape=[jax.ShapeDtypeStruct((t, D), F32), jax.ShapeDtypeStruct((t, D), BF16),
                   jax.ShapeDtypeStruct((t, D), BF16)],
        compiler_params=_cparams(1, VMEM_BIG),
    )(z, z, z, x, pool_w, pool_scale, wmix, conv_w, wmix, wmix)


def fwd_up(x1, g2, w_up_g, fcw, fcb, tm, seq, cps):
    t = x1.shape[0]
    tps = seq // tm

    def body(x1_ref, g2_ref, wup_ref, fcw_ref, fcb_ref,
             up_ref, pre_ref, act_ref, actt_ref, h2t_ref, hist_ref, h2_ref):
        i, k = pl.program_id(0), pl.program_id(1)
        keep_hist = jnp.where(i % tps == 0, 0.0, 1.0)

        @pl.when((i == 0) & (k == 0))
        def _():
            hist_ref[...] = jnp.zeros_like(hist_ref)

        @pl.when(k == 0)
        def _():
            x1v = x1_ref[...]
            h2 = (x1v * _rms_inv(x1v) * g2_ref[...]).astype(BF16)
            h2_ref[...] = h2
            h2t_ref[...] = h2.T

        h2 = h2_ref[...]
        lanes = slice(0, SH_UP)
        for c in range(cps):
            kc = k * cps + c
            conv = []
            for s in range(2):
                ub = _dot(h2, wup_ref[s, c]).astype(BF16)
                up_ref[s, c] = ub
                uf = ub.astype(F32)
                ext = jnp.concatenate([hist_ref[s, kc] * keep_hist, uf], axis=0)
                hist_ref[s, kc] = uf[tm - 8:]
                cc, _, _ = _conv_taps(ext, uf, fcw_ref.at[s, c], lanes, 8)
                conv.append(cc + fcb_ref[s, c])
                pre_ref[s, c] = conv[s].astype(BF16)
            a = (conv[0] * jax.nn.sigmoid(conv[0]) * conv[1]).astype(BF16)
            act_ref[c] = a
            actt_ref[c] = a.T

    tile = lambda i, k: (i, 0)
    const2 = lambda i, k: (0, 0)
    pair = lambda i, k: (0, k, 0, 0)
    chunk = pl.BlockSpec((2, cps, tm, SH_UP), lambda i, k: (0, k, i, 0))
    return pl.pallas_call(
        body, name="fwd_up", grid=(t // tm, NCH // cps),
        in_specs=[pl.BlockSpec((tm, D), tile), pl.BlockSpec((1, D), const2),
                  pl.BlockSpec((2, cps, D, SH_UP), pair), pl.BlockSpec((2, cps, 3, SH_UP), pair),
                  pl.BlockSpec((2, cps, 1, SH_UP), pair)],
        out_specs=[chunk, chunk, pl.BlockSpec((cps, tm, SH_UP), lambda i, k: (k, i, 0)),
                   pl.BlockSpec((cps, SH_UP, tm), lambda i, k: (k, 0, i)), pl.BlockSpec((D, tm), lambda i, k: (0, i))],
        out_shape=[jax.ShapeDtypeStruct((2, NCH, t, SH_UP), BF16), jax.ShapeDtypeStruct((2, NCH, t, SH_UP), BF16),
                   jax.ShapeDtypeStruct((NCH, t, SH_UP), BF16), jax.ShapeDtypeStruct((NCH, SH_UP, t), BF16),
                   jax.ShapeDtypeStruct((D, t), BF16)],
        scratch_shapes=[pltpu.VMEM((2, NCH, 8, SH_UP), F32), pltpu.VMEM((tm, D), BF16)],
        compiler_params=_cparams(2, VMEM_BIG),
    )(x1, g2, w_up_g.reshape(2, NCH, D, SH_UP), fcw.reshape(2, NCH, 3, SH_UP), fcb.reshape(2, NCH, 1, SH_UP))


def fwd_down(x1, act, w_dn, gf, tgt, tm):
    t = x1.shape[0]

    def body(x1_ref, act_ref, wdn_ref, gf_ref, tgt_ref, dx2_ref, vec_ref):
        @pl.when(pl.program_id(0) == 0)
        def _():
            vec_ref[...] = jnp.zeros_like(vec_ref)

        d = None
        for c in range(NCH):
            part = _dot(act_ref[c], wdn_ref[c])
            d = part if d is None else d + part
        x2 = x1_ref[...] + d
        inv3 = _rms_inv(x2)
        xn = x2 * inv3
        diff = xn * gf_ref[...] - tgt_ref[...]
        dy = diff * (1.0 / D)
        vec_ref[0:1, :] += jnp.sum(dy * xn, axis=0, keepdims=True)
        vec_ref[1:2, :] += 0.5 * jnp.sum(jnp.mean(diff * diff, axis=-1))
        dx2_ref[...] = _rms_bwd(dy, xn, inv3, gf_ref[...])

    tile = lambda i: (i, 0)
    const2 = lambda i: (0, 0)
    return pl.pallas_call(
        body, name="fwd_down", grid=(t // tm,),
        in_specs=[pl.BlockSpec((tm, D), tile), pl.BlockSpec((NCH, tm, SH_UP), lambda i: (0, i, 0)),
                  pl.BlockSpec((NCH, SH_UP, D), lambda i: (0, 0, 0)), pl.BlockSpec((1, D), const2),
                  pl.BlockSpec((tm, D), tile)],
        out_specs=[pl.BlockSpec((tm, D), tile), pl.BlockSpec((8, D), const2)],
        out_shape=[jax.ShapeDtypeStruct((t, D), F32), jax.ShapeDtypeStruct((8, D), F32)],
        compiler_params=_cparams(1, VMEM_BIG),
    )(x1, act, w_dn, gf, tgt)


def bwd_ffn(dx2, x1, g2, up, pre, w_up_g, fcw, w_dn, tm, seq, cps):
    t = x1.shape[0]
    nt = t // tm
    tps = seq // tm

    def body(dx2_ref, x1_ref, g2_ref, up_ref, pre_ref, wup_ref, fcw_ref, wdn_ref,
             dup_ref, dx1_ref, gvec_ref, gn_ref, carry_ref, dh2_ref, acc_ref):
        i, k = pl.program_id(0), pl.program_id(1)
        it = (nt - 1 - i) % tps
        keep_next = jnp.where(it == tps - 1, 0.0, 1.0)

        @pl.when((i == 0) & (k == 0))
        def _():
            acc_ref[...] = jnp.zeros_like(acc_ref)
            gn_ref[...] = jnp.zeros_like(gn_ref)
            carry_ref[...] = jnp.zeros_like(carry_ref)

        @pl.when(k == 0)
        def _():
            dh2_ref[...] = jnp.zeros_like(dh2_ref)

        dxb = dx2_ref[...].astype(BF16)
        lanes = slice(0, SH_UP)
        dh2 = dh2_ref[...]
        for c in range(cps):
            kc = k * cps + c
            pre = [pre_ref[s, c].astype(F32) for s in range(2)]
            sg = jax.nn.sigmoid(pre[0])
            dact = _dot_nt(dxb, wdn_ref[c])
            dpre = [dact * pre[1] * (sg * (1.0 + pre[0] * (1.0 - sg))), dact * (pre[0] * sg)]
            for s in range(2):
                dc = dpre[s]
                ext = jnp.concatenate([dc, carry_ref[s, kc] * keep_next], axis=0)
                carry_ref[s, kc] = dc[:8]
                shifted = (_shift_up(ext, 2, tm), _shift_up(ext, 1, tm), dc)
                uf = up_ref[s, c].astype(F32)
                acc_ref[s, kc, 0:1, lanes] += jnp.sum(dc, axis=0, keepdims=True)
                for tap in range(3):
                    acc_ref[s, kc, tap + 1:tap + 2, lanes] += jnp.sum(shifted[tap] * uf, axis=0, keepdims=True)
                w = fcw_ref.at[s, c]
                du = w[2:3, :] * dc + w[1:2, :] * shifted[1] + w[0:1, :] * shifted[0]
                dub = du.astype(BF16)
                dup_ref[s, c] = dub
                dh2 = dh2 + _dot_nt(dub, wup_ref[s, c])
        dh2_ref[...] = dh2

        @pl.when(k == NCH // cps - 1)
        def _():
            x1v = x1_ref[...]
            inv2 = _rms_inv(x1v)
            xn = x1v * inv2
            gn_ref[0:1, :] += jnp.sum(dh2 * xn, axis=0, keepdims=True)
            dx1_ref[...] = dx2_ref[...] + _rms_bwd(dh2, xn, inv2, g2_ref[...])

        @pl.when((i == nt - 1) & (k == NCH // cps - 1))
        def _():
            gvec_ref[...] = acc_ref[...]

    rev = lambda i, k: (nt - 1 - i, 0)
    const2 = lambda i, k: (0, 0)
    pair = lambda i, k: (0, k, 0, 0)
    return pl.pallas_call(
        body, name="bwd_ffn", grid=(nt, NCH // cps),
        in_specs=[pl.BlockSpec((tm, D), rev), pl.BlockSpec((tm, D), rev), pl.BlockSpec((1, D), const2),
                  pl.BlockSpec((2, cps, tm, SH_UP), lambda i, k: (0, k, nt - 1 - i, 0)),
                  pl.BlockSpec((2, cps, tm, SH_UP), lambda i, k: (0, k, nt - 1 - i, 0)),
                  pl.BlockSpec((2, cps, D, SH_UP), pair), pl.BlockSpec((2, cps, 3, SH_UP), pair),
                  pl.BlockSpec((cps, SH_UP, D), lambda i, k: (k, 0, 0))],
        out_specs=[pl.BlockSpec((2, cps, tm, SH_UP), lambda i, k: (0, k, nt - 1 - i, 0)), pl.BlockSpec((tm, D), rev),
                   pl.BlockSpec((2, NCH, 8, D), lambda i, k: (0, 0, 0, 0)), pl.BlockSpec((8, D), const2)],
        out_shape=[jax.ShapeDtypeStruct((2, NCH, t, SH_UP), BF16), jax.ShapeDtypeStruct((t, D), F32),
                   jax.ShapeDtypeStruct((2, NCH, 8, D), F32), jax.ShapeDtypeStruct((8, D), F32)],
        scratch_shapes=[pltpu.VMEM((2, NCH, 8, SH_UP), F32), pltpu.VMEM((tm, D), F32),
                        pltpu.VMEM((2, NCH, 8, D), F32)],
        compiler_params=_cparams(2, VMEM_BIG),
    )(dx2, x1, g2, up, pre, w_up_g.reshape(2, NCH, D, SH_UP), fcw.reshape(2, NCH, 3, SH_UP), w_dn)


def bwd_mix(dx1, z, y_pool, y_conv, pool_w, pool_scale, conv_w, wmix, tm, seq):
    t = dx1.shape[0]
    nt = t // tm
    tps = seq // tm
    hb = tm // HALO

    def body(da_ref, z_ref, zph_ref, zcvh_ref, yp_ref, yc_ref, pw_ref, ps_ref, wpp_ref, cw_ref, wco_ref, wo_ref,
             dz_ref, mg_ref, p2_ref, u_ref, dyp_ref, dyc_ref, p_ref, dpw_ref, gvec_ref, cp_ref, cc_ref):
        i = pl.program_id(0)
        it = (nt - 1 - i) % tps
        keep_hist = jnp.where(it == 0, 0.0, 1.0)
        keep_next = jnp.where(it == tps - 1, 0.0, 1.0)
        pos = it * tm + lax.broadcasted_iota(jnp.int32, (tm, 1), 0)

        @pl.when(i == 0)
        def _():
            gvec_ref[...] = jnp.zeros_like(gvec_ref)
            cp_ref[...] = jnp.zeros_like(cp_ref)
            cc_ref[...] = jnp.zeros_like(cc_ref)

        dm = _dot_nt(da_ref[...].astype(BF16), wo_ref[...])
        merged, dyp, dyc = [], [], []
        for q in range(NG):
            lanes = slice(q * CG, (q + 1) * CG)
            sp = jax.nn.sigmoid(z_ref[16 + q].astype(F32))
            sc = jax.nn.sigmoid(z_ref[20 + q].astype(F32))
            yp = yp_ref[:, lanes].astype(F32)
            yc = yc_ref[:, lanes].astype(F32)
            dmq = dm[:, lanes]
            merged.append((sp * yp + sc * yc).astype(BF16))
            dyp.append((dmq * sp).astype(BF16))
            dyc.append((dmq * sc).astype(BF16))
            dz_ref[16 + q] = (dmq * yp * (sp * (1.0 - sp))).astype(BF16)
            dz_ref[20 + q] = (dmq * yc * (sc * (1.0 - sc))).astype(BF16)
        mg_ref[...] = jnp.concatenate(merged, axis=1)
        dypb = jnp.concatenate(dyp, axis=1)
        dycb = jnp.concatenate(dyc, axis=1)
        dyp_ref[...] = dypb
        dyc_ref[...] = dycb

        dp2 = _dot_nt(dypb, wpp_ref[...])
        p2 = []
        for g, win in enumerate(WINS):
            lanes = slice(g * CG, (g + 1) * CG)
            cnt = jnp.minimum(pos + 1, win).astype(F32)
            p = _pool_tile(z_ref, zph_ref, g, win, keep_hist, cnt)
            pb = p.astype(BF16)
            p_ref[g] = pb
            pw = _dot(pb, pw_ref[g])
            p2.append((pw * ps_ref[:, lanes]).astype(BF16))
            dp2g = dp2[:, lanes]
            gvec_ref[0:1, lanes] += jnp.sum(dp2g * pw, axis=0, keepdims=True)
            dpwb = (dp2g * ps_ref[:, lanes]).astype(BF16)
            dpw_ref[g] = dpwb
            dp = _dot_nt(dpwb, pw_ref[g])
            qv = dp / cnt
            ext = jnp.concatenate([qv, cp_ref[g] * keep_next], axis=0)
            cp_ref[g] = qv[:HALO]
            n = tm + HALO
            s, sh = ext, 1
            while sh < win:
                s = s + pltpu.roll(s, n - sh, 0)
                sh *= 2
            dz_ref[g] = (s[:tm] - dp).astype(BF16)
        p2_ref[...] = jnp.concatenate(p2, axis=1)

        du = _dot_nt(dycb, wco_ref[...])
        u = []
        for q in range(NG):
            lanes = slice(q * CG, (q + 1) * CG)
            zb = z_ref[4 + q].astype(F32)
            zc = z_ref[8 + q].astype(F32)
            zv = z_ref[12 + q].astype(F32)
            cv = zc * zv
            cvh = zcvh_ref[q].astype(F32) * zcvh_ref[4 + q].astype(F32) * keep_hist
            cc, cv1, cv2 = _conv_taps(jnp.concatenate([cvh, cv], axis=0), cv, cw_ref, lanes, HALO)
            u.append((zb * cc).astype(BF16))
            duq = du[:, lanes]
            dz_ref[4 + q] = (duq * cc).astype(BF16)
            dcc = duq * zb
            for tap, src in enumerate((cv2, cv1, cv)):
                gvec_ref[tap + 1:tap + 2, lanes] += jnp.sum(dcc * src, axis=0, keepdims=True)
            ext = jnp.concatenate([dcc, cc_ref[:, lanes] * keep_next], axis=0)
            cc_ref[:, lanes] = dcc[:8]
            dcv = (cw_ref[2:3, lanes] * dcc + cw_ref[1:2, lanes] * _shift_up(ext, 1, tm)
                   + cw_ref[0:1, lanes] * _shift_up(ext, 2, tm))
            dz_ref[8 + q] = (dcv * zv).astype(BF16)
            dz_ref[12 + q] = (dcv * zc).astype(BF16)
        u_ref[...] = jnp.concatenate(u, axis=1)

    def hist(i):
        return jnp.maximum((nt - 1 - i) * hb - 1, 0)

    rev = lambda i: (nt - 1 - i, 0)
    rev3 = lambda i: (0, nt - 1 - i, 0)
    const2 = lambda i: (0, 0)
    tok = jax.ShapeDtypeStruct((t, D), BF16)
    grp = jax.ShapeDtypeStruct((NG, t, CG), BF16)
    return pl.pallas_call(
        body, name="bwd_mix", grid=(nt,),
        in_specs=[pl.BlockSpec((tm, D), rev), pl.BlockSpec((NZT, tm, CG), rev3),
                  pl.BlockSpec((NG, HALO, CG), lambda i: (0, hist(i), 0)),
                  pl.BlockSpec((2 * NG, HALO, CG), lambda i: (1, hist(i), 0)),
                  pl.BlockSpec((tm, D), rev), pl.BlockSpec((tm, D), rev),
                  pl.BlockSpec((NG, CG, CG), lambda i: (0, 0, 0)), pl.BlockSpec((1, D), const2),
                  pl.BlockSpec((D, D), lambda i: (0, MIX_POOL_PROJ)), pl.BlockSpec((3, D), const2),
                  pl.BlockSpec((D, D), lambda i: (0, MIX_CONV_OUT)), pl.BlockSpec((D, D), lambda i: (0, MIX_O))],
        out_specs=[pl.BlockSpec((NZT, tm, CG), rev3)] + [pl.BlockSpec((tm, D), rev)] * 5
                  + [pl.BlockSpec((NG, tm, CG), rev3)] * 2 + [pl.BlockSpec((8, D), const2)],
        out_shape=[jax.ShapeDtypeStruct((NZT, t, CG), BF16), tok, tok, tok, tok, tok, grp, grp,
                   jax.ShapeDtypeStruct((8, D), F32)],
        scratch_shapes=[pltpu.VMEM((NG, HALO, CG), F32), pltpu.VMEM((8, D), F32)],
        compiler_params=_cparams(1, VMEM_BIG),
    )(dx1, z, z, z, y_pool, y_conv, pool_w, pool_scale, wmix, conv_w, wmix, wmix)


def bwd_in(dz, w_in_g, dx1, x, g1, tm):
    t = x.shape[0]

    def body(dz_ref, w_ref, dx1_ref, x_ref, g_ref, gx_ref, gn_ref):
        @pl.when(pl.program_id(0) == 0)
        def _():
            gn_ref[...] = jnp.zeros_like(gn_ref)

        dh = None
        for j in range(NDEV):
            dzc = jnp.concatenate([dz_ref[3 * j + q] for q in range(3)], axis=1)
            part = _dot_nt(dzc, w_ref[j])
            dh = part if dh is None else dh + part
        xv = x_ref[...]
        inv = _rms_inv(xv)
        xn = xv * inv
        gn_ref[0:1, :] += jnp.sum(dh * xn, axis=0, keepdims=True)
        gx_ref[...] = dx1_ref[...] + _rms_bwd(dh, xn, inv, g_ref[...])

    tile = lambda i: (i, 0)
    return pl.pallas_call(
        body, name="bwd_in", grid=(t // tm,),
        in_specs=[pl.BlockSpec((NZT, tm, CG), lambda i: (0, i, 0)),
                  pl.BlockSpec((NDEV, D, SH_IN), lambda i: (0, 0, 0)),
                  pl.BlockSpec((tm, D), tile), pl.BlockSpec((tm, D), tile), pl.BlockSpec((1, D), lambda i: (0, 0))],
        out_specs=[pl.BlockSpec((tm, D), tile), pl.BlockSpec((8, D), lambda i: (0, 0))],
        out_shape=[jax.ShapeDtypeStruct((t, D), F32), jax.ShapeDtypeStruct((8, D), F32)],
        compiler_params=_cparams(1, VMEM_BIG),
    )(dz, w_in_g, dx1, x, g1)


def _slot(j):
    return j % 2, j // 2


def wgrad_cols(at, b, q, name, tk):
    m, t = at.shape
    width = b.shape[3]

    def body(a_ref, b_ref, o_ref):
        @pl.when(pl.program_id(1) == 0)
        def _():
            o_ref[...] = jnp.zeros_like(o_ref)

        o_ref[...] += _dot(a_ref[...], b_ref[...])

    return pl.pallas_call(
        body, name=name, grid=(NDEV, t // tk),
        in_specs=[pl.BlockSpec((m, tk), lambda j, k: (0, k)),
                  pl.BlockSpec((None, None, tk, width), lambda j, k: (j, q, k, 0))],
        out_specs=pl.BlockSpec((None, None, m, width), lambda j, k: (j % 2, j // 2, 0, 0)),
        out_shape=jax.ShapeDtypeStruct((2, 4, m, width), F32),
        compiler_params=_cparams(2, VMEM_BIG),
    )(at, b)


def wgrad_cols_resident(at, b, q, name, tk):
    m, t = at.shape
    width = b.shape[3]

    def body(a_ref, b_ref, o_ref):
        k, j = pl.program_id(0), pl.program_id(1)

        @pl.when((k == 0) & (j == 0))
        def _():
            o_ref[...] = jnp.zeros_like(o_ref)

        o_ref[j % 2, j // 2] += _dot(a_ref[...], b_ref[...])

    return pl.pallas_call(
        body, name=name, grid=(t // tk, NDEV),
        in_specs=[pl.BlockSpec((m, tk), lambda k, j: (0, k)),
                  pl.BlockSpec((None, None, tk, width), lambda k, j: (j, q, k, 0))],
        out_specs=pl.BlockSpec((2, 4, m, width), lambda k, j: (0, 0, 0, 0)),
        out_shape=jax.ShapeDtypeStruct((2, 4, m, width), F32),
        compiler_params=_cparams(2, VMEM_BIG),
    )(at, b)


def wgrad_down(actt, dx2, tk):
    t = dx2.shape[0]

    def body(a_ref, b_ref, o_ref, acc_ref):
        kt = pl.program_id(1)

        @pl.when(kt == 0)
        def _():
            acc_ref[...] = jnp.zeros_like(acc_ref)

        acc_ref[...] += _dot(a_ref[...], b_ref[...].astype(BF16))

        @pl.when(kt == pl.num_programs(1) - 1)
        def _():
            o_ref[0] = acc_ref[:SH_DN]
            o_ref[1] = acc_ref[SH_DN:]

    return pl.pallas_call(
        body, name="wgrad_down", grid=(NCH, t // tk),
        in_specs=[pl.BlockSpec((None, SH_UP, tk), lambda k, kt: (k, 0, kt)), pl.BlockSpec((tk, D), lambda k, kt: (kt, 0))],
        out_specs=pl.BlockSpec((2, None, SH_DN, D), lambda k, kt: (0, k, 0, 0)),
        out_shape=jax.ShapeDtypeStruct((2, 4, SH_DN, D), F32),
        scratch_shapes=[pltpu.VMEM((SH_UP, D), F32)],
        compiler_params=_cparams(2, VMEM_BIG),
    )(actt, dx2)


def wgrad_square(a, b, name, tk):
    t = a.shape[0]

    def body(a_ref, b_ref, o_ref, acc_ref):
        kt = pl.program_id(0)

        @pl.when(kt == 0)
        def _():
            acc_ref[...] = jnp.zeros_like(acc_ref)

        acc_ref[...] += _dot_tn(a_ref[...], b_ref[...].astype(BF16))

        @pl.when(kt == pl.num_programs(0) - 1)
        def _():
            for j in range(NDEV):
                cc, xy = _slot(j)
                o_ref[cc, xy] = acc_ref[j * 128:(j + 1) * 128]

    return pl.pallas_call(
        body, name=name, grid=(t // tk,),
        in_specs=[pl.BlockSpec((tk, D), lambda k: (k, 0)), pl.BlockSpec((tk, D), lambda k: (k, 0))],
        out_specs=pl.BlockSpec((2, 4, 128, D), lambda k: (0, 0, 0, 0)),
        out_shape=jax.ShapeDtypeStruct((2, 4, 128, D), F32),
        scratch_shapes=[pltpu.VMEM((D, D), F32)],
        compiler_params=_cparams(1, VMEM_BIG),
    )(a, b)


def wgrad_pool(p, dpw, tk):
    t = p.shape[1]

    def body(a_ref, b_ref, o_ref):
        @pl.when(pl.program_id(0) == 0)
        def _():
            o_ref[...] = jnp.zeros_like(o_ref)

        for g in range(NG):
            o_ref[g] += _dot_tn(a_ref[g], b_ref[g])

    return pl.pallas_call(
        body, name="wgrad_pool", grid=(t // tk,),
        in_specs=[pl.BlockSpec((NG, tk, CG), lambda k: (0, k, 0))] * 2,
        out_specs=pl.BlockSpec((NG, CG, CG), lambda k: (0, 0, 0)),
        out_shape=jax.ShapeDtypeStruct((NG, CG, CG), F32),
        compiler_params=_cparams(1, VMEM_BIG),
    )(p, dpw)


def _adamw(w, g, m, v):
    m = ADAM_B1 * m + (1.0 - ADAM_B1) * g
    v = ADAM_B2 * v + (1.0 - ADAM_B2) * (g * g)
    m_hat = m / (1.0 - ADAM_B1 ** ADAM_STEP)
    v_hat = v / (1.0 - ADAM_B2 ** ADAM_STEP)
    delta = -ADAM_LR * (m_hat / (jnp.sqrt(v_hat) + ADAM_EPS) + ADAM_WD * w)
    return delta, m, v


def _row_block(r):
    return 512 if r % 512 == 0 else r


def chip_partial(place, g, from_sibling, name):
    _, _, r, c = g.shape

    def body(place_ref, g_ref, s_ref, o_ref):
        o_ref[...] = (g_ref[...] + s_ref[...]).astype(BF16)

    return pl.pallas_call(
        body, name=name,
        grid_spec=pltpu.PrefetchScalarGridSpec(
            num_scalar_prefetch=1, grid=(3,),
            in_specs=[pl.BlockSpec((None, None, r, c), lambda k, pr: (pr[0], pr[1] ^ (k + 1), 0, 0)),
                      pl.BlockSpec((None, r, c), lambda k, pr: (pr[1] ^ (k + 1), 0, 0))],
            out_specs=pl.BlockSpec((None, r, c), lambda k, pr: (pr[1] ^ (k + 1), 0, 0))),
        out_shape=jax.ShapeDtypeStruct((4, r, c), BF16),
        compiler_params=_cparams(1, VMEM_BIG),
    )(place, g, from_sibling)


def finish_adamw(place, gs, from_sibling, from_chips, w, m, v, name):
    n = len(gs)
    r = gs[0].shape[2]
    widths = [g.shape[3] for g in gs]
    c = sum(widths)
    br = _row_block(r)

    def body(place_ref, *refs):
        g_refs, s_refs, c_refs = refs[:n], refs[n:2 * n], refs[2 * n:5 * n]
        w_ref, m_ref, v_ref, og_ref, od_ref, om_ref, ov_ref = refs[5 * n:]
        cols = []
        for q in range(n):
            grad = g_refs[q][...] + s_refs[q][...]
            for k in range(3):
                grad = grad + c_refs[3 * q + k][...].astype(F32)
            cols.append(grad)
        grad = cols[0] if n == 1 else jnp.concatenate(cols, axis=1)
        og_ref[...] = grad
        od_ref[...], om_ref[...], ov_ref[...] = _adamw(w_ref[...], grad, m_ref[...], v_ref[...])

    def other(k, cq):
        return pl.BlockSpec((None, br, cq), lambda i, pr: (pr[1] ^ k, i, 0))

    row = pl.BlockSpec((br, c), lambda i, pr: (i, 0))
    out = jax.ShapeDtypeStruct((r, c), F32)
    in_specs = [pl.BlockSpec((None, None, br, cq), lambda i, pr: (pr[0], pr[1], i, 0)) for cq in widths]
    in_specs += [pl.BlockSpec((None, br, cq), lambda i, pr: (pr[1], i, 0)) for cq in widths]
    in_specs += [other(k, cq) for cq in widths for k in (1, 2, 3)]
    return pl.pallas_call(
        body, name=name,
        grid_spec=pltpu.PrefetchScalarGridSpec(
            num_scalar_prefetch=1, grid=(r // br,), in_specs=in_specs + [row, row, row], out_specs=[row] * 4),
        out_shape=[out] * 4,
        compiler_params=_cparams(1, VMEM_BIG),
    )(place, *gs, *from_sibling, *[fc for fc in from_chips for _ in range(3)], w, m, v)


def adamw_small(items):
    n = len(items)

    def body(*refs):
        ins, outs = refs[:4 * n], refs[4 * n:]
        for i in range(n):
            w, g, m, v = (r[...] for r in ins[4 * i:4 * i + 4])
            outs[3 * i][...], outs[3 * i + 1][...], outs[3 * i + 2][...] = _adamw(w, g, m, v)

    out = [jax.ShapeDtypeStruct(it[0].shape, F32) for it in items for _ in range(3)]
    res = pl.pallas_call(body, name="adamw_small", out_shape=out)(*[a for it in items for a in it])
    return [res[3 * i:3 * i + 3] for i in range(n)]


def kernel(x, norm_mix, w_in, pool_w, pool_scale, w_pool_proj, conv_w, w_conv_out, w_o, norm_ffn, w_up, ffn_conv_w, ffn_conv_b, w_down, norm_final, loss_target, m_norm_mix, m_w_in, m_pool_w, m_pool_scale, m_w_pool_proj, m_conv_w, m_w_conv_out, m_w_o, m_norm_ffn, m_w_up, m_ffn_conv_w, m_ffn_conv_b, m_w_down, m_norm_final, v_norm_mix, v_w_in, v_pool_w, v_pool_scale, v_w_pool_proj, v_conv_w, v_w_conv_out, v_w_o, v_norm_ffn, v_w_up, v_ffn_conv_w, v_ffn_conv_b, v_w_down, v_norm_final):
    nb, seq, _ = x.shape
    t = nb * seq
    tm_in = min(TM_IN, t)
    tm_mix = min(TM_MIX, seq)
    tm_ffn = min(TM_FFN, seq)
    tk = min(TK_WGRAD, t)
    xt = x.reshape(t, D)
    tgt = loss_target.reshape(t, D)
    xi, yi, ci = _pos()
    me = 4 * xi + 2 * yi + ci
    place = jnp.stack([ci, 2 * xi + yi]).astype(jnp.int32)

    tie = lax.optimization_barrier
    w_in_g, = all_gather_blocks([w_in[0].astype(BF16)], "all_gather_w_in", 0)
    taps = (jnp.pad(conv_w[0], ((0, 5), (0, D - 128))) + jnp.pad(ffn_conv_w[0], ((3, 2), (0, D - SH_UP))))
    taps_g = _exchange_small(taps, False, "all_gather_taps")
    mix_shard = jnp.concatenate(
        [w_pool_proj[0], w_conv_out[0], w_o[0], pool_w[0].reshape(NG * 32, CG)], axis=1).astype(BF16)
    mix_shard, taps_g = tie((mix_shard, taps_g))
    wmix_g, = all_gather_blocks([mix_shard], "all_gather_w_mix", 0)
    ffn_shards, w_in_g = tie(([w_up[0].astype(BF16), w_down[0].astype(BF16)], w_in_g))
    w_up_g, = all_gather_blocks(ffn_shards[:1], "all_gather_w_up", 0)
    w_dn_g, = all_gather_blocks(ffn_shards[1:], "all_gather_w_down", 0)
    w_dn_f = w_dn_g.reshape(NCH, SH_UP, D)
    conv_w_f = taps_g[:, 0:3, :128].transpose(1, 0, 2).reshape(3, D)
    fcw_f = taps_g[:, 3:6, :SH_UP]
    fcb_f = ffn_conv_b.reshape(NDEV, 1, SH_UP)
    gfin = norm_final.reshape(1, D)

    z, h1 = fwd_in(xt, norm_mix, w_in_g, tm_in)
    wmix_g, z = tie((wmix_g, z))
    wmix = wmix_g.reshape(D, MIX_COLS)
    pool_w_f = wmix_g[:, :, 3 * D:].reshape(NDEV, NG, 32, CG).transpose(1, 0, 2, 3).reshape(NG, CG, CG)
    x1, y_pool, y_conv = fwd_mix(z, xt, pool_w_f, pool_scale, conv_w_f, wmix, tm_mix, seq)
    up, pre, act_tok, act, h2 = fwd_up(x1, norm_ffn, w_up_g, fcw_f, fcb_f, tm_ffn, seq, FFN_CHUNKS_PER_STEP)
    dx2, ffn_vec = fwd_down(x1, act_tok, w_dn_f, gfin, tgt, min(TM_IN, t))

    def to_sibling(full, tag):
        return reduce_scatter_d2d(full, "reduce_scatter_d2d_" + tag, 1)

    def partials(full, from_sib, names):
        return [chip_partial(place, g, s, "chip_partial_" + nm) for g, s, nm in zip(full, from_sib, names)]

    def to_chips(parts, tag):
        return reduce_scatter_ici(parts, "reduce_scatter_ici_" + tag, 2)

    def finish(nm, gs, from_sib, from_chips, wmv):
        w, m, v = wmv
        rc = (gs[0].shape[2], sum(g.shape[3] for g in gs))
        outs = finish_adamw(place, gs, from_sib, from_chips, w.reshape(rc), m.reshape(rc), v.reshape(rc), "adamw_" + nm)
        return [o.reshape(w.shape) for o in outs]

    def after(x, dep):
        return tie((x, dep))[0]

    big = {}
    gw_dn = wgrad_down(act, dx2, tk)
    sib_dn = to_sibling([gw_dn], "w_down")
    d_up, dx1, g_ffn_vec, g_nffn = bwd_ffn(dx2, x1, norm_ffn, up, pre, w_up_g, fcw_f, w_dn_f, tm_ffn, seq,
                                           FFN_CHUNKS_PER_STEP)
    d_up, part_dn = tie((d_up, partials([gw_dn], sib_dn, ["w_down"])))
    chips_dn = to_chips(part_dn, "w_down")
    gw_up = wgrad_cols(h2, d_up.reshape(NDEV, 1, t, SH_UP), 0, "wgrad_up", tk)
    sib_up = to_sibling([after(gw_up, chips_dn)], "w_up")
    dz, merged, p2, u, dyp, dyc, p, dpw, g_mix_vec = bwd_mix(
        dx1, z, y_pool, y_conv, pool_w_f, pool_scale, conv_w_f, wmix, tm_mix, seq)
    merged, part_up = tie((merged, partials([gw_up], sib_up, ["w_up"])))
    chips_up = to_chips(part_up, "w_up")
    small_g, = all_gather_blocks(
        [after(jnp.concatenate([g_mix_vec, g_nffn, ffn_vec, g_ffn_vec.reshape(8 * NDEV, D)], axis=0), chips_up)],
        "all_gather_small", 0)
    gw_o = wgrad_square(merged, dx1, "wgrad_o", tk)
    gw_pp = wgrad_square(p2, dyp, "wgrad_pool_proj", tk)
    gw_co = wgrad_square(u, dyc, "wgrad_conv_out", tk)
    gw_pool = wgrad_pool(p, dpw, tk).reshape(NG, 4, 2, 32, CG).transpose(2, 1, 0, 3, 4).reshape(2, 4, NG * 32, CG)
    dz8 = dz.reshape(NDEV, 3, t, CG)
    gw_in, sib_in, chips_in = [None] * 3, [None] * 3, [None] * 3
    gw_in[0] = wgrad_cols_resident(h1, dz8, 0, "wgrad_in_0", tk)
    sib_a = to_sibling(after([gw_o, gw_pp], (chips_up, gw_in[0])), "mix_a")
    sib_b = to_sibling(after([gw_co, gw_pool], sib_a), "mix_b")
    sib_in[0] = to_sibling(after([gw_in[0]], sib_b), "w_in_0")
    gw_in[1] = wgrad_cols_resident(h1, dz8, 1, "wgrad_in_1", tk)
    sib_in[1] = to_sibling(after([gw_in[1]], sib_in[0]), "w_in_1")
    h1, part_a, part_b, part_in0 = tie((h1, partials([gw_o, gw_pp], sib_a, ["w_o", "w_pool_proj"]),
                                        partials([gw_co, gw_pool], sib_b, ["w_conv_out", "pool_w"]),
                                        partials([gw_in[0]], sib_in[0], ["w_in_0"])))
    chips_a = to_chips(after(part_a, sib_in[1]), "mix_a")
    chips_b = to_chips(part_b, "mix_b")
    chips_in[0] = to_chips(part_in0, "w_in_0")
    h1, big["w_down"], big["w_up"] = tie((
        h1, finish("w_down", [gw_dn], sib_dn, chips_dn, (w_down, m_w_down, v_w_down)),
        finish("w_up", [gw_up], sib_up, chips_up, (w_up, m_w_up, v_w_up))))
    gw_in[2] = wgrad_cols_resident(h1, dz8, 2, "wgrad_in_2", tk)
    sib_in[2] = to_sibling(after([gw_in[2]], (chips_a, chips_b, chips_in[0])), "w_in_2")
    dx1, part_in1, part_in2, big["w_o"], big["w_pool_proj"], big["w_conv_out"], big["pool_w"] = tie((
        dx1, partials([gw_in[1]], sib_in[1], ["w_in_1"]), partials([gw_in[2]], sib_in[2], ["w_in_2"]),
        finish("w_o", [gw_o], sib_a[:1], chips_a[:1], (w_o, m_w_o, v_w_o)),
        finish("w_pool_proj", [gw_pp], sib_a[1:], chips_a[1:], (w_pool_proj, m_w_pool_proj, v_w_pool_proj)),
        finish("w_conv_out", [gw_co], sib_b[:1], chips_b[:1], (w_conv_out, m_w_conv_out, v_w_conv_out)),
        finish("pool_w", [gw_pool], sib_b[1:], chips_b[1:], (pool_w, m_pool_w, v_pool_w))))
    chips_in[1] = to_chips(after(part_in1, sib_in[2]), "w_in_1")
    chips_in[2] = to_chips(part_in2, "w_in_2")
    grad_x, g_nmix = bwd_in(dz, w_in_g, dx1, xt, norm_mix, min(TM_BWD_IN, t))
    grad_x, chips_in = tie((grad_x, chips_in))
    big["w_in"] = finish("w_in", gw_in, [s[0] for s in sib_in], [c[0] for c in chips_in], (w_in, m_w_in, v_w_in))

    red_n, red = _exchange_small(g_nmix, True, "all_reduce_small", gathered=small_g)
    g_norm_mix, g_pool_scale, g_norm_ffn = red_n[0:1], red[0:1], red[8:9]
    g_conv_w = lax.dynamic_slice(red, (1, me * 128), (3, 128))
    g_norm_final = red[16]
    loss = red[17, 0]
    g_fcb = red[24:].reshape(NDEV, 8, D)[:, 0, :SH_UP].reshape(1, FF2)
    g_fcw = lax.dynamic_slice(red, (25 + 8 * me, 0), (3, SH_UP))
    grads = {"norm_mix": g_norm_mix, "pool_scale": g_pool_scale, "norm_ffn": g_norm_ffn, "norm_final": g_norm_final,
             "ffn_conv_b": g_fcb, "conv_w": g_conv_w.reshape(1, 3, 128), "ffn_conv_w": g_fcw.reshape(1, 3, SH_UP)}
    small_wmv = {"norm_mix": (norm_mix, m_norm_mix, v_norm_mix), "pool_scale": (pool_scale, m_pool_scale, v_pool_scale),
                 "norm_ffn": (norm_ffn, m_norm_ffn, v_norm_ffn), "norm_final": (norm_final, m_norm_final, v_norm_final),
                 "ffn_conv_b": (ffn_conv_b, m_ffn_conv_b, v_ffn_conv_b), "conv_w": (conv_w, m_conv_w, v_conv_w),
                 "ffn_conv_w": (ffn_conv_w, m_ffn_conv_w, v_ffn_conv_w)}
    small_names = list(small_wmv)
    flat2 = lambda a: a.reshape(-1, a.shape[-1])
    small_out = adamw_small([(flat2(small_wmv[nm][0]), flat2(grads[nm]), flat2(small_wmv[nm][1]),
                              flat2(small_wmv[nm][2])) for nm in small_names])
    small = {nm: [o.reshape(small_wmv[nm][0].shape) for o in outs] for nm, outs in zip(small_names, small_out)}

    order = ["norm_mix", "w_in", "pool_w", "pool_scale", "w_pool_proj", "conv_w", "w_conv_out", "w_o", "norm_ffn",
             "w_up", "ffn_conv_w", "ffn_conv_b", "w_down", "norm_final"]
    out = [loss, grad_x.reshape(nb, seq, D)]
    out += [big[nm][0] if nm in big else grads[nm] for nm in order]
    for idx in range(3):
        out += [big[nm][idx + 1] if nm in big else small[nm][idx] for nm in order]
    return tuple(out)
```

```python
import functools

import jax
import jax.numpy as jnp
from jax import lax
from jax.experimental import pallas as pl
from jax.experimental.pallas import tpu as pltpu
from jax.experimental.pallas import tpu_sc as plsc

F32 = jnp.float32
BF16 = jnp.bfloat16

NDEV = 8
D = 1024
NG = 4
CG = 256
WINS = (2, 4, 8, 16)
DIN = 6 * D
SH_IN = DIN // NDEV
NZT = DIN // CG
FF2 = 5632
SH_UP = FF2 // NDEV
FF = FF2 // 2
NCH = 4
SH_DN = FF // NDEV
RMS_EPS = 1e-6
HALO = 16

ADAM_LR = 0.001
ADAM_B1 = 0.9
ADAM_B2 = 0.999
ADAM_EPS = 1e-08
ADAM_WD = 0.01
ADAM_STEP = 10

TM_IN = 512
TM_BWD_IN = 256
TM_MIX = 256
TM_FFN = 256
FFN_CHUNKS_PER_STEP = 4
TK_WGRAD = 2048
MIX_POOL_PROJ, MIX_CONV_OUT, MIX_O = 0, 1, 2
MIX_COLS = 3 * D + CG
VMEM_BIG = 56 * 1024 * 1024
MESH = pl.DeviceIdType.MESH
ANY = pl.BlockSpec(memory_space=pl.ANY)


def _cparams(n_axes, vmem=None):
    return pltpu.CompilerParams(dimension_semantics=("arbitrary",) * n_axes, vmem_limit_bytes=vmem)


def _dot(a, b):
    return jnp.dot(a, b, preferred_element_type=F32)


def _dot_nt(a, b):
    return lax.dot_general(a, b, (((1,), (1,)), ((), ())), preferred_element_type=F32)


def _dot_tn(a, b):
    return lax.dot_general(a, b, (((0,), (0,)), ((), ())), preferred_element_type=F32)


def _shift_down(ext, s, lead):
    return pltpu.roll(ext, s, 0)[lead:]


def _shift_up(ext, s, tm):
    n = ext.shape[0]
    return pltpu.roll(ext, n - s, 0)[:tm]


def _rms_inv(x):
    return lax.rsqrt(jnp.mean(x * x, axis=-1, keepdims=True) + RMS_EPS)


def _rms_bwd(dh, xn, inv, g):
    dxn = dh * g
    return inv * (dxn - xn * jnp.mean(dxn * xn, axis=-1, keepdims=True))


def _pos():
    return lax.axis_index("x"), lax.axis_index("y"), lax.axis_index("c")


def _handshake(peers):
    barrier = pltpu.get_barrier_semaphore()
    for peer in peers:
        pl.semaphore_signal(barrier, inc=1, device_id=peer, device_id_type=MESH)
    pl.semaphore_wait(barrier, len(peers))


def _sequencer(body, out_type, n_sems, name, collective_id):
    return pl.kernel(
        body, out_type=out_type, mesh=plsc.ScalarSubcoreMesh(axis_name="sequencer", num_cores=1), name=name,
        scratch_types=[pltpu.SemaphoreType.DMA((n_sems,)), pltpu.SemaphoreType.DMA((n_sems,))],
        compiler_params=pltpu.CompilerParams(collective_id=collective_id))


def all_gather_blocks(shards, name, collective_id):
    n = len(shards)

    def body(*refs):
        ins, outs = refs[:n], refs[n:2 * n]
        send_sems, recv_sems = refs[2 * n:]
        x, y, c = _pos()
        sibling = (x, y, 1 - c)
        chips = [(1 - x, y), (x, 1 - y), (1 - x, 1 - y)]
        _handshake([sibling] + [(*chip, c) for chip in chips])

        def copy(w, k, block, to, src=None):
            slot = outs[w].at[4 * block[0] + 2 * block[1] + block[2]]
            return pltpu.make_async_remote_copy(
                src_ref=slot if src is None else src, dst_ref=slot,
                send_sem=send_sems.at[8 * w + k], recv_sem=recv_sems.at[8 * w + k], device_id=to, device_id_type=MESH)

        mine, first, passed = [], [], []
        for w in range(n):
            m = pltpu.make_async_copy(ins[w], outs[w].at[4 * x + 2 * y + c], send_sems.at[8 * w + 7])
            m.start()
            mine.append(m)
            first.append(copy(w, 0, (x, y, c), sibling, src=ins[w]))
            first += [copy(w, 1 + j, (x, y, c), (*chip, c), src=ins[w]) for j, chip in enumerate(chips)]
        for cp in first:
            cp.start()
        for w in range(n):
            for j, chip in enumerate(chips):
                copy(w, 1 + j, (*chip, c), (x, y, c)).wait_recv()
                fw = copy(w, 4 + j, (*chip, c), sibling)
                fw.start()
                passed.append(fw)
        for w in range(n):
            copy(w, 0, (x, y, 1 - c), (x, y, c)).wait_recv()
            for j, chip in enumerate(chips):
                copy(w, 4 + j, (*chip, 1 - c), (x, y, c)).wait_recv()
        for cp in first + passed:
            cp.wait_send()
        for m in mine:
            m.wait()

    out = [jax.ShapeDtypeStruct((NDEV,) + s.shape, s.dtype) for s in shards]
    return _sequencer(body, out, 8 * n, name, collective_id)(*shards)


def _exchange_small(v, reduce, name, gathered=None):
    rows = v.shape[0]

    def body(*refs):
        if gathered is None:
            v_ref, out_ref, slots, send_sems, recv_sems, local_sem = refs
        else:
            v_ref, g_ref, out_ref, gsum_ref, slots, send_sems, recv_sems, local_sem = refs
        x, y, c = _pos()
        me = 4 * x + 2 * y + c
        mine = pltpu.make_async_copy(v_ref, slots.at[me], local_sem)
        mine.start()
        offs = [(dx, dy, dc) for dx in (0, 1) for dy in (0, 1) for dc in (0, 1)][1:]

        def copy(k, src_slot, to):
            return pltpu.make_async_remote_copy(
                src_ref=v_ref, dst_ref=slots.at[src_slot], send_sem=send_sems.at[k], recv_sem=recv_sems.at[k],
                device_id=to, device_id_type=MESH)

        sends = []
        for k, (dx, dy, dc) in enumerate(offs):
            cp = copy(k, me, (x ^ dx, y ^ dy, c ^ dc))
            cp.start()
            sends.append(cp)
        for k, (dx, dy, dc) in enumerate(offs):
            copy(k, 4 * (x ^ dx) + 2 * (y ^ dy) + (c ^ dc), (x, y, c)).wait_recv()
        for cp in sends:
            cp.wait_send()
        mine.wait()
        if reduce:
            acc = slots[0]
            for d in range(1, NDEV):
                acc = acc + slots[d]
            out_ref[...] = acc
        else:
            out_ref[...] = slots[...]
        if gathered is not None:
            acc = g_ref[0]
            for d in range(1, NDEV):
                acc = acc + g_ref[d]
            gsum_ref[...] = acc

    vmem = pl.BlockSpec(memory_space=pltpu.VMEM)
    out = jax.ShapeDtypeStruct((rows, D) if reduce else (NDEV, rows, D), F32)
    args, out_shape, out_specs = [v], out, vmem
    if gathered is not None:
        args.append(gathered)
        out_shape, out_specs = [out, jax.ShapeDtypeStruct(gathered.shape[1:], F32)], [vmem, vmem]
    return pl.pallas_call(
        body, name=name, out_shape=out_shape, in_specs=[vmem] * len(args), out_specs=out_specs,
        scratch_shapes=[pltpu.VMEM((NDEV, rows, D), F32), pltpu.SemaphoreType.DMA((7,)),
                        pltpu.SemaphoreType.DMA((7,)), pltpu.SemaphoreType.DMA],
    )(*args)


def reduce_scatter_d2d(grads, name, collective_id):
    n = len(grads)

    def body(*refs):
        ins, outs = refs[:n], refs[n:2 * n]
        send_sems, recv_sems = refs[2 * n:]
        x, y, c = _pos()
        _handshake([(x, y, 1 - c)])
        cps = []
        for w in range(n):
            cp = pltpu.make_async_remote_copy(
                src_ref=ins[w].at[1 - c], dst_ref=outs[w], send_sem=send_sems.at[w], recv_sem=recv_sems.at[w],
                device_id=(x, y, 1 - c), device_id_type=MESH)
            cp.start()
            cps.append(cp)
        for cp in cps:
            cp.wait_recv()
        for cp in cps:
            cp.wait_send()

    out = [jax.ShapeDtypeStruct(g.shape[1:], F32) for g in grads]
    return _sequencer(body, out, n, name, collective_id)(*grads)


def reduce_scatter_ici(parts, name, collective_id):
    n = len(parts)

    def body(*refs):
        ins, outs = refs[:n], refs[n:2 * n]
        send_sems, recv_sems = refs[2 * n:]
        x, y, c = _pos()
        offs = [(1, 0), (0, 1), (1, 1)]
        _handshake([(x ^ dx, y ^ dy, c) for dx, dy in offs])
        cps = []
        for w in range(n):
            for k, (dx, dy) in enumerate(offs):
                ox, oy = x ^ dx, y ^ dy
                cp = pltpu.make_async_remote_copy(
                    src_ref=ins[w].at[2 * ox + oy], dst_ref=outs[w].at[2 * x + y],
                    send_sem=send_sems.at[3 * w + k], recv_sem=recv_sems.at[3 * w + k],
                    device_id=(ox, oy, c), device_id_type=MESH)
                cp.start()
                cps.append((cp, w, k, ox, oy))
        for cp, w, k, ox, oy in cps:
            pltpu.make_async_remote_copy(
                src_ref=ins[w].at[2 * ox + oy], dst_ref=outs[w].at[2 * ox + oy],
                send_sem=send_sems.at[3 * w + k], recv_sem=recv_sems.at[3 * w + k],
                device_id=(ox, oy, c), device_id_type=MESH).wait_recv()
        for cp, *_ in cps:
            cp.wait_send()

    out = [jax.ShapeDtypeStruct(p.shape, BF16) for p in parts]
    return _sequencer(body, out, 3 * n, name, collective_id)(*parts)


def fwd_in(x, g1, w_in_g, tm):
    t = x.shape[0]

    def body(x_ref, g_ref, w_ref, z_ref, ht_ref):
        xf = x_ref[...]
        h = (xf * _rms_inv(xf) * g_ref[...]).astype(BF16)
        ht_ref[...] = h.T
        for j in range(NDEV):
            r = _dot(h, w_ref[j])
            for q in range(3):
                z_ref[3 * j + q] = r[:, q * CG:(q + 1) * CG].astype(BF16)

    return pl.pallas_call(
        body, name="fwd_in", grid=(t // tm,),
        in_specs=[pl.BlockSpec((tm, D), lambda i: (i, 0)), pl.BlockSpec((1, D), lambda i: (0, 0)),
                  pl.BlockSpec((NDEV, D, SH_IN), lambda i: (0, 0, 0))],
        out_specs=[pl.BlockSpec((NZT, tm, CG), lambda i: (0, i, 0)), pl.BlockSpec((D, tm), lambda i: (0, i))],
        out_shape=[jax.ShapeDtypeStruct((NZT, t, CG), BF16), jax.ShapeDtypeStruct((D, t), BF16)],
        compiler_params=_cparams(1, VMEM_BIG),
        cost_estimate=pl.CostEstimate(flops=2 * t * D * DIN, transcendentals=t,
                                      bytes_accessed=4 * t * D + 2 * D * DIN + 2 * t * DIN + 2 * t * D),
    )(x, g1, w_in_g)


def _pool_tile(z_ref, zh_ref, g, win, keep_hist, cnt):
    zt = z_ref[g].astype(F32)
    ext = jnp.concatenate([zh_ref[g].astype(F32) * keep_hist, zt], axis=0)
    s, sh = ext, 1
    while sh < win:
        s = s + pltpu.roll(s, sh, 0)
        sh *= 2
    return s[HALO:] / cnt - zt


def _conv_taps(ext, cur, w_ref, lanes, lead):
    x1 = _shift_down(ext, 1, lead)
    x2 = _shift_down(ext, 2, lead)
    out = w_ref[2:3, lanes] * cur + w_ref[1:2, lanes] * x1 + w_ref[0:1, lanes] * x2
    return out, x1, x2


def fwd_mix(z, x, pool_w, pool_scale, conv_w, wmix, tm, seq):
    t = x.shape[0]
    tps = seq // tm
    hb = tm // HALO

    def body(z_ref, zph_ref, zcvh_ref, x_ref, pw_ref, ps_ref, wpp_ref, cw_ref, wco_ref, wo_ref,
             x1_ref, yp_ref, yc_ref):
        it = pl.program_id(0) % tps
        keep_hist = jnp.where(it == 0, 0.0, 1.0)
        pos = it * tm + lax.broadcasted_iota(jnp.int32, (tm, 1), 0)
        p2 = []
        for g, win in enumerate(WINS):
            cnt = jnp.minimum(pos + 1, win).astype(F32)
            p = _pool_tile(z_ref, zph_ref, g, win, keep_hist, cnt)
            lanes = slice(g * CG, (g + 1) * CG)
            p2.append((_dot(p.astype(BF16), pw_ref[g]) * ps_ref[:, lanes]).astype(BF16))
        y_pool = _dot(jnp.concatenate(p2, axis=1), wpp_ref[...])
        u = []
        for q in range(NG):
            lanes = slice(q * CG, (q + 1) * CG)
            cv = z_ref[8 + q].astype(F32) * z_ref[12 + q].astype(F32)
            cvh = zcvh_ref[q].astype(F32) * zcvh_ref[4 + q].astype(F32) * keep_hist
            cc, _, _ = _conv_taps(jnp.concatenate([cvh, cv], axis=0), cv, cw_ref, lanes, HALO)
            u.append((z_ref[4 + q].astype(F32) * cc).astype(BF16))
        y_conv = _dot(jnp.concatenate(u, axis=1), wco_ref[...])
        ypb, ycb = y_pool.astype(BF16), y_conv.astype(BF16)
        yp_ref[...] = ypb
        yc_ref[...] = ycb
        merged = []
        for q in range(NG):
            lanes = slice(q * CG, (q + 1) * CG)
            sp = jax.nn.sigmoid(z_ref[16 + q].astype(F32))
            sc = jax.nn.sigmoid(z_ref[20 + q].astype(F32))
            merged.append((sp * ypb[:, lanes].astype(F32) + sc * ycb[:, lanes].astype(F32)).astype(BF16))
        x1_ref[...] = x_ref[...] + _dot(jnp.concatenate(merged, axis=1), wo_ref[...])

    def hist(i):
        return jnp.maximum(i * hb - 1, 0)

    const2 = lambda i: (0, 0)
    return pl.pallas_call(
        body, name="fwd_mix", grid=(t // tm,),
        in_specs=[pl.BlockSpec((NZT, tm, CG), lambda i: (0, i, 0)),
                  pl.BlockSpec((NG, HALO, CG), lambda i: (0, hist(i), 0)),
                  pl.BlockSpec((2 * NG, HALO, CG), lambda i: (1, hist(i), 0)),
                  pl.BlockSpec((tm, D), lambda i: (i, 0)),
                  pl.BlockSpec((NG, CG, CG), lambda i: (0, 0, 0)), pl.BlockSpec((1, D), const2),
                  pl.BlockSpec((D, D), lambda i: (0, MIX_POOL_PROJ)), pl.BlockSpec((3, D), const2),
                  pl.BlockSpec((D, D), lambda i: (0, MIX_CONV_OUT)), pl.BlockSpec((D, D), lambda i: (0, MIX_O))],
        out_specs=[pl.BlockSpec((tm, D), lambda i: (i, 0))] * 3,
        out_shape=[jax.ShapeDtypeStruct((t, D), F32), jax.ShapeDtypeStruct((t, D), BF16),
                   jax.ShapeDtypeStruct((t, D), BF16)],
        compiler_params=_cparams(1, VMEM_BIG),
    )(z, z, z, x, pool_w, pool_scale, wmix, conv_w, wmix, wmix)


def fwd_up(x1, g2, w_up_g, fcw, fcb, tm, seq, cps):
    t = x1.shape[0]
    tps = seq // tm

    def body(x1_ref, g2_ref, wup_ref, fcw_ref, fcb_ref,
             up_ref, pre_ref, act_ref, actt_ref, h2t_ref, hist_ref, h2_ref):
        i, k = pl.program_id(0), pl.program_id(1)
        keep_hist = jnp.where(i % tps == 0, 0.0, 1.0)

        @pl.when((i == 0) & (k == 0))
        def _():
            hist_ref[...] = jnp.zeros_like(hist_ref)

        @pl.when(k == 0)
        def _():
            x1v = x1_ref[...]
            h2 = (x1v * _rms_inv(x1v) * g2_ref[...]).astype(BF16)
            h2_ref[...] = h2
            h2t_ref[...] = h2.T

        h2 = h2_ref[...]
        lanes = slice(0, SH_UP)
        for c in range(cps):
            kc = k * cps + c
            conv = []
            for s in range(2):
                ub = _dot(h2, wup_ref[s, c]).astype(BF16)
                up_ref[s, c] = ub
                uf = ub.astype(F32)
                ext = jnp.concatenate([hist_ref[s, kc] * keep_hist, uf], axis=0)
                hist_ref[s, kc] = uf[tm - 8:]
                cc, _, _ = _conv_taps(ext, uf, fcw_ref.at[s, c], lanes, 8)
                conv.append(cc + fcb_ref[s, c])
                pre_ref[s, c] = conv[s].astype(BF16)
            a = (conv[0] * jax.nn.sigmoid(conv[0]) * conv[1]).astype(BF16)
            act_ref[c] = a
            actt_ref[c] = a.T

    tile = lambda i, k: (i, 0)
    const2 = lambda i, k: (0, 0)
    pair = lambda i, k: (0, k, 0, 0)
    chunk = pl.BlockSpec((2, cps, tm, SH_UP), lambda i, k: (0, k, i, 0))
    return pl.pallas_call(
        body, name="fwd_up", grid=(t // tm, NCH // cps),
        in_specs=[pl.BlockSpec((tm, D), tile), pl.BlockSpec((1, D), const2),
                  pl.BlockSpec((2, cps, D, SH_UP), pair), pl.BlockSpec((2, cps, 3, SH_UP), pair),
                  pl.BlockSpec((2, cps, 1, SH_UP), pair)],
        out_specs=[chunk, chunk, pl.BlockSpec((cps, tm, SH_UP), lambda i, k: (k, i, 0)),
                   pl.BlockSpec((cps, SH_UP, tm), lambda i, k: (k, 0, i)), pl.BlockSpec((D, tm), lambda i, k: (0, i))],
        out_shape=[jax.ShapeDtypeStruct((2, NCH, t, SH_UP), BF16), jax.ShapeDtypeStruct((2, NCH, t, SH_UP), BF16),
                   jax.ShapeDtypeStruct((NCH, t, SH_UP), BF16), jax.ShapeDtypeStruct((NCH, SH_UP, t), BF16),
                   jax.ShapeDtypeStruct((D, t), BF16)],
        scratch_shapes=[pltpu.VMEM((2, NCH, 8, SH_UP), F32), pltpu.VMEM((tm, D), BF16)],
        compiler_params=_cparams(2, VMEM_BIG),
    )(x1, g2, w_up_g.reshape(2, NCH, D, SH_UP), fcw.reshape(2, NCH, 3, SH_UP), fcb.reshape(2, NCH, 1, SH_UP))


def fwd_down(x1, act, w_dn, gf, tgt, tm):
    t = x1.shape[0]

    def body(x1_ref, act_ref, wdn_ref, gf_ref, tgt_ref, dx2_ref, vec_ref):
        @pl.when(pl.program_id(0) == 0)
        def _():
            vec_ref[...] = jnp.zeros_like(vec_ref)

        d = None
        for c in range(NCH):
            part = _dot(act_ref[c], wdn_ref[c])
            d = part if d is None else d + part
        x2 = x1_ref[...] + d
        inv3 = _rms_inv(x2)
        xn = x2 * inv3
        diff = xn * gf_ref[...] - tgt_ref[...]
        dy = diff * (1.0 / D)
        vec_ref[0:1, :] += jnp.sum(dy * xn, axis=0, keepdims=True)
        vec_ref[1:2, :] += 0.5 * jnp.sum(jnp.mean(diff * diff, axis=-1))
        dx2_ref[...] = _rms_bwd(dy, xn, inv3, gf_ref[...])

    tile = lambda i: (i, 0)
    const2 = lambda i: (0, 0)
    return pl.pallas_call(
        body, name="fwd_down", grid=(t // tm,),
        in_specs=[pl.BlockSpec((tm, D), tile), pl.BlockSpec((NCH, tm, SH_UP), lambda i: (0, i, 0)),
                  pl.BlockSpec((NCH, SH_UP, D), lambda i: (0, 0, 0)), pl.BlockSpec((1, D), const2),
                  pl.BlockSpec((tm, D), tile)],
        out_specs=[pl.BlockSpec((tm, D), tile), pl.BlockSpec((8, D), const2)],
        out_shape=[jax.ShapeDtypeStruct((t, D), F32), jax.ShapeDtypeStruct((8, D), F32)],
        compiler_params=_cparams(1, VMEM_BIG),
    )(x1, act, w_dn, gf, tgt)


def bwd_ffn(dx2, x1, g2, up, pre, w_up_g, fcw, w_dn, tm, seq, cps):
    t = x1.shape[0]
    nt = t // tm
    tps = seq // tm

    def body(dx2_ref, x1_ref, g2_ref, up_ref, pre_ref, wup_ref, fcw_ref, wdn_ref,
             dup_ref, dx1_ref, gvec_ref, gn_ref, carry_ref, dh2_ref, acc_ref):
        i, k = pl.program_id(0), pl.program_id(1)
        it = (nt - 1 - i) % tps
        keep_next = jnp.where(it == tps - 1, 0.0, 1.0)

        @pl.when((i == 0) & (k == 0))
        def _():
            acc_ref[...] = jnp.zeros_like(acc_ref)
            gn_ref[...] = jnp.zeros_like(gn_ref)
            carry_ref[...] = jnp.zeros_like(carry_ref)

        @pl.when(k == 0)
        def _():
            dh2_ref[...] = jnp.zeros_like(dh2_ref)

        dxb = dx2_ref[...].astype(BF16)
        lanes = slice(0, SH_UP)
        dh2 = dh2_ref[...]
        for c in range(cps):
            kc = k * cps + c
            pre = [pre_ref[s, c].astype(F32) for s in range(2)]
            sg = jax.nn.sigmoid(pre[0])
            dact = _dot_nt(dxb, wdn_ref[c])
            dpre = [dact * pre[1] * (sg * (1.0 + pre[0] * (1.0 - sg))), dact * (pre[0] * sg)]
            for s in range(2):
                dc = dpre[s]
                ext = jnp.concatenate([dc, carry_ref[s, kc] * keep_next], axis=0)
                carry_ref[s, kc] = dc[:8]
                shifted = (_shift_up(ext, 2, tm), _shift_up(ext, 1, tm), dc)
                uf = up_ref[s, c].astype(F32)
                acc_ref[s, kc, 0:1, lanes] += jnp.sum(dc, axis=0, keepdims=True)
                for tap in range(3):
                    acc_ref[s, kc, tap + 1:tap + 2, lanes] += jnp.sum(shifted[tap] * uf, axis=0, keepdims=True)
                w = fcw_ref.at[s, c]
                du = w[2:3, :] * dc + w[1:2, :] * shifted[1] + w[0:1, :] * shifted[0]
                dub = du.astype(BF16)
                dup_ref[s, c] = dub
                dh2 = dh2 + _dot_nt(dub, wup_ref[s, c])
        dh2_ref[...] = dh2

        @pl.when(k == NCH // cps - 1)
        def _():
            x1v = x1_ref[...]
            inv2 = _rms_inv(x1v)
            xn = x1v * inv2
            gn_ref[0:1, :] += jnp.sum(dh2 * xn, axis=0, keepdims=True)
            dx1_ref[...] = dx2_ref[...] + _rms_bwd(dh2, xn, inv2, g2_ref[...])

        @pl.when((i == nt - 1) & (k == NCH // cps - 1))
        def _():
            gvec_ref[...] = acc_ref[...]

    rev = lambda i, k: (nt - 1 - i, 0)
    const2 = lambda i, k: (0, 0)
    pair = lambda i, k: (0, k, 0, 0)
    return pl.pallas_call(
        body, name="bwd_ffn", grid=(nt, NCH // cps),
        in_specs=[pl.BlockSpec((tm, D), rev), pl.BlockSpec((tm, D), rev), pl.BlockSpec((1, D), const2),
                  pl.BlockSpec((2, cps, tm, SH_UP), lambda i, k: (0, k, nt - 1 - i, 0)),
                  pl.BlockSpec((2, cps, tm, SH_UP), lambda i, k: (0, k, nt - 1 - i, 0)),
                  pl.BlockSpec((2, cps, D, SH_UP), pair), pl.BlockSpec((2, cps, 3, SH_UP), pair),
                  pl.BlockSpec((cps, SH_UP, D), lambda i, k: (k, 0, 0))],
        out_specs=[pl.BlockSpec((2, cps, tm, SH_UP), lambda i, k: (0, k, nt - 1 - i, 0)), pl.BlockSpec((tm, D), rev),
                   pl.BlockSpec((2, NCH, 8, D), lambda i, k: (0, 0, 0, 0)), pl.BlockSpec((8, D), const2)],
        out_shape=[jax.ShapeDtypeStruct((2, NCH, t, SH_UP), BF16), jax.ShapeDtypeStruct((t, D), F32),
                   jax.ShapeDtypeStruct((2, NCH, 8, D), F32), jax.ShapeDtypeStruct((8, D), F32)],
        scratch_shapes=[pltpu.VMEM((2, NCH, 8, SH_UP), F32), pltpu.VMEM((tm, D), F32),
                        pltpu.VMEM((2, NCH, 8, D), F32)],
        compiler_params=_cparams(2, VMEM_BIG),
    )(dx2, x1, g2, up, pre, w_up_g.reshape(2, NCH, D, SH_UP), fcw.reshape(2, NCH, 3, SH_UP), w_dn)


def bwd_mix(dx1, z, y_pool, y_conv, pool_w, pool_scale, conv_w, wmix, tm, seq):
    t = dx1.shape[0]
    nt = t // tm
    tps = seq // tm
    hb = tm // HALO

    def body(da_ref, z_ref, zph_ref, zcvh_ref, yp_ref, yc_ref, pw_ref, ps_ref, wpp_ref, cw_ref, wco_ref, wo_ref,
             dz_ref, mg_ref, p2_ref, u_ref, dyp_ref, dyc_ref, p_ref, dpw_ref, gvec_ref, cp_ref, cc_ref):
        i = pl.program_id(0)
        it = (nt - 1 - i) % tps
        keep_hist = jnp.where(it == 0, 0.0, 1.0)
        keep_next = jnp.where(it == tps - 1, 0.0, 1.0)
        pos = it * tm + lax.broadcasted_iota(jnp.int32, (tm, 1), 0)

        @pl.when(i == 0)
        def _():
            gvec_ref[...] = jnp.zeros_like(gvec_ref)
            cp_ref[...] = jnp.zeros_like(cp_ref)
            cc_ref[...] = jnp.zeros_like(cc_ref)

        dm = _dot_nt(da_ref[...].astype(BF16), wo_ref[...])
        merged, dyp, dyc = [], [], []
        for q in range(NG):
            lanes = slice(q * CG, (q + 1) * CG)
            sp = jax.nn.sigmoid(z_ref[16 + q].astype(F32))
            sc = jax.nn.sigmoid(z_ref[20 + q].astype(F32))
            yp = yp_ref[:, lanes].astype(F32)
            yc = yc_ref[:, lanes].astype(F32)
            dmq = dm[:, lanes]
            merged.append((sp * yp + sc * yc).astype(BF16))
            dyp.append((dmq * sp).astype(BF16))
            dyc.append((dmq * sc).astype(BF16))
            dz_ref[16 + q] = (dmq * yp * (sp * (1.0 - sp))).astype(BF16)
            dz_ref[20 + q] = (dmq * yc * (sc * (1.0 - sc))).astype(BF16)
        mg_ref[...] = jnp.concatenate(merged, axis=1)
        dypb = jnp.concatenate(dyp, axis=1)
        dycb = jnp.concatenate(dyc, axis=1)
        dyp_ref[...] = dypb
        dyc_ref[...] = dycb

        dp2 = _dot_nt(dypb, wpp_ref[...])
        p2 = []
        for g, win in enumerate(WINS):
            lanes = slice(g * CG, (g + 1) * CG)
            cnt = jnp.minimum(pos + 1, win).astype(F32)
            p = _pool_tile(z_ref, zph_ref, g, win, keep_hist, cnt)
            pb = p.astype(BF16)
            p_ref[g] = pb
            pw = _dot(pb, pw_ref[g])
            p2.append((pw * ps_ref[:, lanes]).astype(BF16))
            dp2g = dp2[:, lanes]
            gvec_ref[0:1, lanes] += jnp.sum(dp2g * pw, axis=0, keepdims=True)
            dpwb = (dp2g * ps_ref[:, lanes]).astype(BF16)
            dpw_ref[g] = dpwb
            dp = _dot_nt(dpwb, pw_ref[g])
            qv = dp / cnt
            ext = jnp.concatenate([qv, cp_ref[g] * keep_next], axis=0)
            cp_ref[g] = qv[:HALO]
            n = tm + HALO
            s, sh = ext, 1
            while sh < win:
                s = s + pltpu.roll(s, n - sh, 0)
                sh *= 2
            dz_ref[g] = (s[:tm] - dp).astype(BF16)
        p2_ref[...] = jnp.concatenate(p2, axis=1)

        du = _dot_nt(dycb, wco_ref[...])
        u = []
        for q in range(NG):
            lanes = slice(q * CG, (q + 1) * CG)
            zb = z_ref[4 + q].astype(F32)
            zc = z_ref[8 + q].astype(F32)
            zv = z_ref[12 + q].astype(F32)
            cv = zc * zv
            cvh = zcvh_ref[q].astype(F32) * zcvh_ref[4 + q].astype(F32) * keep_hist
            cc, cv1, cv2 = _conv_taps(jnp.concatenate([cvh, cv], axis=0), cv, cw_ref, lanes, HALO)
            u.append((zb * cc).astype(BF16))
            duq = du[:, lanes]
            dz_ref[4 + q] = (duq * cc).astype(BF16)
            dcc = duq * zb
            for tap, src in enumerate((cv2, cv1, cv)):
                gvec_ref[tap + 1:tap + 2, lanes] += jnp.sum(dcc * src, axis=0, keepdims=True)
            ext = jnp.concatenate([dcc, cc_ref[:, lanes] * keep_next], axis=0)
            cc_ref[:, lanes] = dcc[:8]
            dcv = (cw_ref[2:3, lanes] * dcc + cw_ref[1:2, lanes] * _shift_up(ext, 1, tm)
                   + cw_ref[0:1, lanes] * _shift_up(ext, 2, tm))
            dz_ref[8 + q] = (dcv * zv).astype(BF16)
            dz_ref[12 + q] = (dcv * zc).astype(BF16)
        u_ref[...] = jnp.concatenate(u, axis=1)

    def hist(i):
        return jnp.maximum((nt - 1 - i) * hb - 1, 0)

    rev = lambda i: (nt - 1 - i, 0)
    rev3 = lambda i: (0, nt - 1 - i, 0)
    const2 = lambda i: (0, 0)
    tok = jax.ShapeDtypeStruct((t, D), BF16)
    grp = jax.ShapeDtypeStruct((NG, t, CG), BF16)
    return pl.pallas_call(
        body, name="bwd_mix", grid=(nt,),
        in_specs=[pl.BlockSpec((tm, D), rev), pl.BlockSpec((NZT, tm, CG), rev3),
                  pl.BlockSpec((NG, HALO, CG), lambda i: (0, hist(i), 0)),
                  pl.BlockSpec((2 * NG, HALO, CG), lambda i: (1, hist(i), 0)),
                  pl.BlockSpec((tm, D), rev), pl.BlockSpec((tm, D), rev),
                  pl.BlockSpec((NG, CG, CG), lambda i: (0, 0, 0)), pl.BlockSpec((1, D), const2),
                  pl.BlockSpec((D, D), lambda i: (0, MIX_POOL_PROJ)), pl.BlockSpec((3, D), const2),
                  pl.BlockSpec((D, D), lambda i: (0, MIX_CONV_OUT)), pl.BlockSpec((D, D), lambda i: (0, MIX_O))],
        out_specs=[pl.BlockSpec((NZT, tm, CG), rev3)] + [pl.BlockSpec((tm, D), rev)] * 5
                  + [pl.BlockSpec((NG, tm, CG), rev3)] * 2 + [pl.BlockSpec((8, D), const2)],
        out_shape=[jax.ShapeDtypeStruct((NZT, t, CG), BF16), tok, tok, tok, tok, tok, grp, grp,
                   jax.ShapeDtypeStruct((8, D), F32)],
        scratch_shapes=[pltpu.VMEM((NG, HALO, CG), F32), pltpu.VMEM((8, D), F32)],
        compiler_params=_cparams(1, VMEM_BIG),
    )(dx1, z, z, z, y_pool, y_conv, pool_w, pool_scale, wmix, conv_w, wmix, wmix)


def bwd_in(dz, w_in_g, dx1, x, g1, tm):
    t = x.shape[0]

    def body(dz_ref, w_ref, dx1_ref, x_ref, g_ref, gx_ref, gn_ref):
        @pl.when(pl.program_id(0) == 0)
        def _():
            gn_ref[...] = jnp.zeros_like(gn_ref)

        dh = None
        for j in range(NDEV):
            dzc = jnp.concatenate([dz_ref[3 * j + q] for q in range(3)], axis=1)
            part = _dot_nt(dzc, w_ref[j])
            dh = part if dh is None else dh + part
        xv = x_ref[...]
        inv = _rms_inv(xv)
        xn = xv * inv
        gn_ref[0:1, :] += jnp.sum(dh * xn, axis=0, keepdims=True)
        gx_ref[...] = dx1_ref[...] + _rms_bwd(dh, xn, inv, g_ref[...])

    tile = lambda i: (i, 0)
    return pl.pallas_call(
        body, name="bwd_in", grid=(t // tm,),
        in_specs=[pl.BlockSpec((NZT, tm, CG), lambda i: (0, i, 0)),
                  pl.BlockSpec((NDEV, D, SH_IN), lambda i: (0, 0, 0)),
                  pl.BlockSpec((tm, D), tile), pl.BlockSpec((tm, D), tile), pl.BlockSpec((1, D), lambda i: (0, 0))],
        out_specs=[pl.BlockSpec((tm, D), tile), pl.BlockSpec((8, D), lambda i: (0, 0))],
        out_shape=[jax.ShapeDtypeStruct((t, D), F32), jax.ShapeDtypeStruct((8, D), F32)],
        compiler_params=_cparams(1, VMEM_BIG),
    )(dz, w_in_g, dx1, x, g1)


def _slot(j):
    return j % 2, j // 2


def wgrad_cols(at, b, q, name, tk):
    m, t = at.shape
    width = b.shape[3]

    def body(a_ref, b_ref, o_ref):
        @pl.when(pl.program_id(1) == 0)
        def _():
            o_ref[...] = jnp.zeros_like(o_ref)

        o_ref[...] += _dot(a_ref[...], b_ref[...])

    return pl.pallas_call(
        body, name=name, grid=(NDEV, t // tk),
        in_specs=[pl.BlockSpec((m, tk), lambda j, k: (0, k)),
                  pl.BlockSpec((None, None, tk, width), lambda j, k: (j, q, k, 0))],
        out_specs=pl.BlockSpec((None, None, m, width), lambda j, k: (j % 2, j // 2, 0, 0)),
        out_shape=jax.ShapeDtypeStruct((2, 4, m, width), F32),
        compiler_params=_cparams(2, VMEM_BIG),
    )(at, b)


def wgrad_cols_resident(at, b, q, name, tk):
    m, t = at.shape
    width = b.shape[3]

    def body(a_ref, b_ref, o_ref):
        k, j = pl.program_id(0), pl.program_id(1)

        @pl.when((k == 0) & (j == 0))
        def _():
            o_ref[...] = jnp.zeros_like(o_ref)

        o_ref[j % 2, j // 2] += _dot(a_ref[...], b_ref[...])

    return pl.pallas_call(
        body, name=name, grid=(t // tk, NDEV),
        in_specs=[pl.BlockSpec((m, tk), lambda k, j: (0, k)),
                  pl.BlockSpec((None, None, tk, width), lambda k, j: (j, q, k, 0))],
        out_specs=pl.BlockSpec((2, 4, m, width), lambda k, j: (0, 0, 0, 0)),
        out_shape=jax.ShapeDtypeStruct((2, 4, m, width), F32),
        compiler_params=_cparams(2, VMEM_BIG),
    )(at, b)


def wgrad_down(actt, dx2, tk):
    t = dx2.shape[0]

    def body(a_ref, b_ref, o_ref, acc_ref):
        kt = pl.program_id(1)

        @pl.when(kt == 0)
        def _():
            acc_ref[...] = jnp.zeros_like(acc_ref)

        acc_ref[...] += _dot(a_ref[...], b_ref[...].astype(BF16))

        @pl.when(kt == pl.num_programs(1) - 1)
        def _():
            o_ref[0] = acc_ref[:SH_DN]
            o_ref[1] = acc_ref[SH_DN:]

    return pl.pallas_call(
        body, name="wgrad_down", grid=(NCH, t // tk),
        in_specs=[pl.BlockSpec((None, SH_UP, tk), lambda k, kt: (k, 0, kt)), pl.BlockSpec((tk, D), lambda k, kt: (kt, 0))],
        out_specs=pl.BlockSpec((2, None, SH_DN, D), lambda k, kt: (0, k, 0, 0)),
        out_shape=jax.ShapeDtypeStruct((2, 4, SH_DN, D), F32),
        scratch_shapes=[pltpu.VMEM((SH_UP, D), F32)],
        compiler_params=_cparams(2, VMEM_BIG),
    )(actt, dx2)


def wgrad_square(a, b, name, tk):
    t = a.shape[0]

    def body(a_ref, b_ref, o_ref, acc_ref):
        kt = pl.program_id(0)

        @pl.when(kt == 0)
        def _():
            acc_ref[...] = jnp.zeros_like(acc_ref)

        acc_ref[...] += _dot_tn(a_ref[...], b_ref[...].astype(BF16))

        @pl.when(kt == pl.num_programs(0) - 1)
        def _():
            for j in range(NDEV):
                cc, xy = _slot(j)
                o_ref[cc, xy] = acc_ref[j * 128:(j + 1) * 128]

    return pl.pallas_call(
        body, name=name, grid=(t // tk,),
        in_specs=[pl.BlockSpec((tk, D), lambda k: (k, 0)), pl.BlockSpec((tk, D), lambda k: (k, 0))],
        out_specs=pl.BlockSpec((2, 4, 128, D), lambda k: (0, 0, 0, 0)),
        out_shape=jax.ShapeDtypeStruct((2, 4, 128, D), F32),
        scratch_shapes=[pltpu.VMEM((D, D), F32)],
        compiler_params=_cparams(1, VMEM_BIG),
    )(a, b)


def wgrad_pool(p, dpw, tk):
    t = p.shape[1]

    def body(a_ref, b_ref, o_ref):
        @pl.when(pl.program_id(0) == 0)
        def _():
            o_ref[...] = jnp.zeros_like(o_ref)

        for g in range(NG):
            o_ref[g] += _dot_tn(a_ref[g], b_ref[g])

    return pl.pallas_call(
        body, name="wgrad_pool", grid=(t // tk,),
        in_specs=[pl.BlockSpec((NG, tk, CG), lambda k: (0, k, 0))] * 2,
        out_specs=pl.BlockSpec((NG, CG, CG), lambda k: (0, 0, 0)),
        out_shape=jax.ShapeDtypeStruct((NG, CG, CG), F32),
        compiler_params=_cparams(1, VMEM_BIG),
    )(p, dpw)


def _adamw(w, g, m, v):
    m = ADAM_B1 * m + (1.0 - ADAM_B1) * g
    v = ADAM_B2 * v + (1.0 - ADAM_B2) * (g * g)
    m_hat = m / (1.0 - ADAM_B1 ** ADAM_STEP)
    v_hat = v / (1.0 - ADAM_B2 ** ADAM_STEP)
    delta = -ADAM_LR * (m_hat / (jnp.sqrt(v_hat) + ADAM_EPS) + ADAM_WD * w)
    return delta, m, v


def _row_block(r):
    return 512 if r % 512 == 0 else r


def chip_partial(place, g, from_sibling, name):
    _, _, r, c = g.shape

    def body(place_ref, g_ref, s_ref, o_ref):
        o_ref[...] = (g_ref[...] + s_ref[...]).astype(BF16)

    return pl.pallas_call(
        body, name=name,
        grid_spec=pltpu.PrefetchScalarGridSpec(
            num_scalar_prefetch=1, grid=(3,),
            in_specs=[pl.BlockSpec((None, None, r, c), lambda k, pr: (pr[0], pr[1] ^ (k + 1), 0, 0)),
                      pl.BlockSpec((None, r, c), lambda k, pr: (pr[1] ^ (k + 1), 0, 0))],
            out_specs=pl.BlockSpec((None, r, c), lambda k, pr: (pr[1] ^ (k + 1), 0, 0))),
        out_shape=jax.ShapeDtypeStruct((4, r, c), BF16),
        compiler_params=_cparams(1, VMEM_BIG),
    )(place, g, from_sibling)


def finish_adamw(place, gs, from_sibling, from_chips, w, m, v, name):
    n = len(gs)
    r = gs[0].shape[2]
    widths = [g.shape[3] for g in gs]
    c = sum(widths)
    br = _row_block(r)

    def body(place_ref, *refs):
        g_refs, s_refs, c_refs = refs[:n], refs[n:2 * n], refs[2 * n:5 * n]
        w_ref, m_ref, v_ref, og_ref, od_ref, om_ref, ov_ref = refs[5 * n:]
        cols = []
        for q in range(n):
            grad = g_refs[q][...] + s_refs[q][...]
            for k in range(3):
                grad = grad + c_refs[3 * q + k][...].astype(F32)
            cols.append(grad)
        grad = cols[0] if n == 1 else jnp.concatenate(cols, axis=1)
        og_ref[...] = grad
        od_ref[...], om_ref[...], ov_ref[...] = _adamw(w_ref[...], grad, m_ref[...], v_ref[...])

    def other(k, cq):
        return pl.BlockSpec((None, br, cq), lambda i, pr: (pr[1] ^ k, i, 0))

    row = pl.BlockSpec((br, c), lambda i, pr: (i, 0))
    out = jax.ShapeDtypeStruct((r, c), F32)
    in_specs = [pl.BlockSpec((None, None, br, cq), lambda i, pr: (pr[0], pr[1], i, 0)) for cq in widths]
    in_specs += [pl.BlockSpec((None, br, cq), lambda i, pr: (pr[1], i, 0)) for cq in widths]
    in_specs += [other(k, cq) for cq in widths for k in (1, 2, 3)]
    return pl.pallas_call(
        body, name=name,
        grid_spec=pltpu.PrefetchScalarGridSpec(
            num_scalar_prefetch=1, grid=(r // br,), in_specs=in_specs + [row, row, row], out_specs=[row] * 4),
        out_shape=[out] * 4,
        compiler_params=_cparams(1, VMEM_BIG),
    )(place, *gs, *from_sibling, *[fc for fc in from_chips for _ in range(3)], w, m, v)


def adamw_small(items):
    n = len(items)

    def body(*refs):
        ins, outs = refs[:4 * n], refs[4 * n:]
        for i in range(n):
            w, g, m, v = (r[...] for r in ins[4 * i:4 * i + 4])
            outs[3 * i][...], outs[3 * i + 1][...], outs[3 * i + 2][...] = _adamw(w, g, m, v)

    out = [jax.ShapeDtypeStruct(it[0].shape, F32) for it in items for _ in range(3)]
    res = pl.pallas_call(body, name="adamw_small", out_shape=out)(*[a for it in items for a in it])
    return [res[3 * i:3 * i + 3] for i in range(n)]


def kernel(x, norm_mix, w_in, pool_w, pool_scale, w_pool_proj, conv_w, w_conv_out, w_o, norm_ffn, w_up, ffn_conv_w, ffn_conv_b, w_down, norm_final, loss_target, m_norm_mix, m_w_in, m_pool_w, m_pool_scale, m_w_pool_proj, m_conv_w, m_w_conv_out, m_w_o, m_norm_ffn, m_w_up, m_ffn_conv_w, m_ffn_conv_b, m_w_down, m_norm_final, v_norm_mix, v_w_in, v_pool_w, v_pool_scale, v_w_pool_proj, v_conv_w, v_w_conv_out, v_w_o, v_norm_ffn, v_w_up, v_ffn_conv_w, v_ffn_conv_b, v_w_down, v_norm_final):
    nb, seq, _ = x.shape
    t = nb * seq
    tm_in = min(TM_IN, t)
    tm_mix = min(TM_MIX, seq)
    tm_ffn = min(TM_FFN, seq)
    tk = min(TK_WGRAD, t)
    xt = x.reshape(t, D)
    tgt = loss_target.reshape(t, D)
    xi, yi, ci = _pos()
    me = 4 * xi + 2 * yi + ci
    place = jnp.stack([ci, 2 * xi + yi]).astype(jnp.int32)

    tie = lax.optimization_barrier
    w_in_g, = all_gather_blocks([w_in[0].astype(BF16)], "all_gather_w_in", 0)
    taps = (jnp.pad(conv_w[0], ((0, 5), (0, D - 128))) + jnp.pad(ffn_conv_w[0], ((3, 2), (0, D - SH_UP))))
    taps_g = _exchange_small(taps, False, "all_gather_taps")
    mix_shard = jnp.concatenate(
        [w_pool_proj[0], w_conv_out[0], w_o[0], pool_w[0].reshape(NG * 32, CG)], axis=1).astype(BF16)
    mix_shard, taps_g = tie((mix_shard, taps_g))
    wmix_g, = all_gather_blocks([mix_shard], "all_gather_w_mix", 0)
    ffn_shards, w_in_g = tie(([w_up[0].astype(BF16), w_down[0].astype(BF16)], w_in_g))
    w_up_g, = all_gather_blocks(ffn_shards[:1], "all_gather_w_up", 0)
    w_dn_g, = all_gather_blocks(ffn_shards[1:], "all_gather_w_down", 0)
    w_dn_f = w_dn_g.reshape(NCH, SH_UP, D)
    conv_w_f = taps_g[:, 0:3, :128].transpose(1, 0, 2).reshape(3, D)
    fcw_f = taps_g[:, 3:6, :SH_UP]
    fcb_f = ffn_conv_b.reshape(NDEV, 1, SH_UP)
    gfin = norm_final.reshape(1, D)

    z, h1 = fwd_in(xt, norm_mix, w_in_g, tm_in)
    wmix_g, z = tie((wmix_g, z))
    wmix = wmix_g.reshape(D, MIX_COLS)
    pool_w_f = wmix_g[:, :, 3 * D:].reshape(NDEV, NG, 32, CG).transpose(1, 0, 2, 3).reshape(NG, CG, CG)
    x1, y_pool, y_conv = fwd_mix(z, xt, pool_w_f, pool_scale, conv_w_f, wmix, tm_mix, seq)
    up, pre, act_tok, act, h2 = fwd_up(x1, norm_ffn, w_up_g, fcw_f, fcb_f, tm_ffn, seq, FFN_CHUNKS_PER_STEP)
    dx2, ffn_vec = fwd_down(x1, act_tok, w_dn_f, gfin, tgt, min(TM_IN, t))

    def to_sibling(full, tag):
        return reduce_scatter_d2d(full, "reduce_scatter_d2d_" + tag, 1)

    def partials(full, from_sib, names):
        return [chip_partial(place, g, s, "chip_partial_" + nm) for g, s, nm in zip(full, from_sib, names)]

    def to_chips(parts, tag):
        return reduce_scatter_ici(parts, "reduce_scatter_ici_" + tag, 2)

    def finish(nm, gs, from_sib, from_chips, wmv):
        w, m, v = wmv
        rc = (gs[0].shape[2], sum(g.shape[3] for g in gs))
        outs = finish_adamw(place, gs, from_sib, from_chips, w.reshape(rc), m.reshape(rc), v.reshape(rc), "adamw_" + nm)
        return [o.reshape(w.shape) for o in outs]

    def after(x, dep):
        return tie((x, dep))[0]

    big = {}
    d_up, dx1, g_ffn_vec, g_nffn = bwd_ffn(dx2, x1, norm_ffn, up, pre, w_up_g, fcw_f, w_dn_f, tm_ffn, seq,
                                           FFN_CHUNKS_PER_STEP)
    gw_up = wgrad_cols(h2, d_up.reshape(NDEV, 1, t, SH_UP), 0, "wgrad_up", tk)
    sib_up = to_sibling([gw_up], "w_up")
    gw_dn = wgrad_down(act, after(dx2, gw_up), tk)
    sib_dn = to_sibling([after(gw_dn, sib_up)], "w_down")
    dx1, part_up = tie((dx1, partials([gw_up], sib_up, ["w_up"])))
    chips_up = to_chips(part_up, "w_up")
    dz, merged, p2, u, dyp, dyc, p, dpw, g_mix_vec = bwd_mix(
        dx1, z, y_pool, y_conv, pool_w_f, pool_scale, conv_w_f, wmix, tm_mix, seq)
    merged, part_dn = tie((merged, partials([gw_dn], sib_dn, ["w_down"])))
    chips_dn = to_chips(part_dn, "w_down")
    gw_o = wgrad_square(merged, dx1, "wgrad_o", tk)
    gw_pp = wgrad_square(p2, dyp, "wgrad_pool_proj", tk)
    gw_co = wgrad_square(u, dyc, "wgrad_conv_out", tk)
    gw_pool = wgrad_pool(p, dpw, tk).reshape(NG, 4, 2, 32, CG).transpose(2, 1, 0, 3, 4).reshape(2, 4, NG * 32, CG)
    dz8 = dz.reshape(NDEV, 3, t, CG)
    gw_in, sib_in, chips_in = [None] * 3, [None] * 3, [None] * 3
    gw_in[0] = wgrad_cols_resident(h1, dz8, 0, "wgrad_in_0", tk)
    sib_a = to_sibling(after([gw_o, gw_pp], (chips_up, chips_dn, gw_in[0], gw_co)), "mix_a")
    sib_b = to_sibling(after([gw_co, gw_pool], sib_a), "mix_b")
    sib_in[0] = to_sibling(after([gw_in[0]], sib_b), "w_in_0")
    gw_in[1] = wgrad_cols_resident(h1, dz8, 1, "wgrad_in_1", tk)
    sib_in[1] = to_sibling(after([gw_in[1]], sib_in[0]), "w_in_1")
    h1, part_a, part_b, part_in0 = tie((h1, partials([gw_o, gw_pp], sib_a, ["w_o", "w_pool_proj"]),
                                        partials([gw_co, gw_pool], sib_b, ["w_conv_out", "pool_w"]),
                                        partials([gw_in[0]], sib_in[0], ["w_in_0"])))
    chips_a = to_chips(after(part_a, sib_in[1]), "mix_a")
    chips_b = to_chips(part_b, "mix_b")
    chips_in[0] = to_chips(part_in0, "w_in_0")
    h1, big["w_down"], big["w_up"] = tie((
        h1, finish("w_down", [gw_dn], sib_dn, chips_dn, (w_down, m_w_down, v_w_down)),
        finish("w_up", [gw_up], sib_up, chips_up, (w_up, m_w_up, v_w_up))))
    gw_in[2] = wgrad_cols_resident(h1, dz8, 2, "wgrad_in_2", tk)
    sib_in[2] = to_sibling(after([gw_in[2]], (chips_a, chips_b, chips_in[0])), "w_in_2")
    dx1, part_in1, part_in2, big["w_o"], big["w_pool_proj"], big["w_conv_out"], big["pool_w"] = tie((
        dx1, partials([gw_in[1]], sib_in[1], ["w_in_1"]), partials([gw_in[2]], sib_in[2], ["w_in_2"]),
        finish("w_o", [gw_o], sib_a[:1], chips_a[:1], (w_o, m_w_o, v_w_o)),
        finish("w_pool_proj", [gw_pp], sib_a[1:], chips_a[1:], (w_pool_proj, m_w_pool_proj, v_w_pool_proj)),
        finish("w_conv_out", [gw_co], sib_b[:1], chips_b[:1], (w_conv_out, m_w_conv_out, v_w_conv_out)),
        finish("pool_w", [gw_pool], sib_b[1:], chips_b[1:], (pool_w, m_pool_w, v_pool_w))))
    chips_in[1] = to_chips(after(part_in1, sib_in[2]), "w_in_1")
    chips_in[2] = to_chips(part_in2, "w_in_2")
    small_g, = all_gather_blocks(
        [after(jnp.concatenate([g_mix_vec, g_nffn, ffn_vec, g_ffn_vec.reshape(8 * NDEV, D)], axis=0), sib_in[2])],
        "all_gather_small", 0)
    grad_x, g_nmix = bwd_in(dz, w_in_g, dx1, xt, norm_mix, min(TM_BWD_IN, t))
    grad_x, chips_in = tie((grad_x, chips_in))
    big["w_in"] = finish("w_in", gw_in, [s[0] for s in sib_in], [c[0] for c in chips_in], (w_in, m_w_in, v_w_in))

    red_n, red = _exchange_small(g_nmix, True, "all_reduce_small", gathered=small_g)
    g_norm_mix, g_pool_scale, g_norm_ffn = red_n[0:1], red[0:1], red[8:9]
    g_conv_w = lax.dynamic_slice(red, (1, me * 128), (3, 128))
    g_norm_final = red[16]
    loss = red[17, 0]
    g_fcb = red[24:].reshape(NDEV, 8, D)[:, 0, :SH_UP].reshape(1, FF2)
    g_fcw = lax.dynamic_slice(red, (25 + 8 * me, 0), (3, SH_UP))
    grads = {"norm_mix": g_norm_mix, "pool_scale": g_pool_scale, "norm_ffn": g_norm_ffn, "norm_final": g_norm_final,
             "ffn_conv_b": g_fcb, "conv_w": g_conv_w.reshape(1, 3, 128), "ffn_conv_w": g_fcw.reshape(1, 3, SH_UP)}
    small_wmv = {"norm_mix": (norm_mix, m_norm_mix, v_norm_mix), "pool_scale": (pool_scale, m_pool_scale, v_pool_scale),
                 "norm_ffn": (norm_ffn, m_norm_ffn, v_norm_ffn), "norm_final": (norm_final, m_norm_final, v_norm_final),
                 "ffn_conv_b": (ffn_conv_b, m_ffn_conv_b, v_ffn_conv_b), "conv_w": (conv_w, m_conv_w, v_conv_w),
                 "ffn_conv_w": (ffn_conv_w, m_ffn_conv_w, v_ffn_conv_w)}
    small_names = list(small_wmv)
    flat2 = lambda a: a.reshape(1, -1) if a.ndim == 1 else a
    small_out = adamw_small([(flat2(small_wmv[nm][0]), flat2(grads[nm]), flat2(small_wmv[nm][1]),
                              flat2(small_wmv[nm][2])) for nm in small_names])
    small = {nm: [o.reshape(small_wmv[nm][0].shape) for o in outs] for nm, outs in zip(small_names, small_out)}

    order = ["norm_mix", "w_in", "pool_w", "pool_scale", "w_pool_proj", "conv_w", "w_conv_out", "w_o", "norm_ffn",
             "w_up", "ffn_conv_w", "ffn_conv_b", "w_down", "norm_final"]
    out = [loss, grad_x.reshape(nb, seq, D)]
    out += [big[nm][0] if nm in big else grads[nm] for nm in order]
    for idx in range(3):
        out += [big[nm][idx + 1] if nm in big else small[nm][idx] for nm in order]
    return tuple(out)
```

```python
import functools

import jax
import jax.numpy as jnp
from jax import lax
from jax.experimental import pallas as pl
from jax.experimental.pallas import tpu as pltpu
from jax.experimental.pallas import tpu_sc as plsc

F32 = jnp.float32
BF16 = jnp.bfloat16

NDEV = 8
D = 1024
NG = 4
CG = 256
WINS = (2, 4, 8, 16)
DIN = 6 * D
SH_IN = DIN // NDEV
NZT = DIN // CG
FF2 = 5632
SH_UP = FF2 // NDEV
FF = FF2 // 2
NCH = 4
SH_DN = FF // NDEV
RMS_EPS = 1e-6
HALO = 16

ADAM_LR = 0.001
ADAM_B1 = 0.9
ADAM_B2 = 0.999
ADAM_EPS = 1e-08
ADAM_WD = 0.01
ADAM_STEP = 10

TM_IN = 512
TM_BWD_IN = 256
TM_MIX = 256
TM_FFN = 256
FFN_CHUNKS_PER_STEP = 4
TK_WGRAD = 2048
MIX_POOL_PROJ, MIX_CONV_OUT, MIX_O = 0, 1, 2
MIX_COLS = 3 * D + CG
VMEM_BIG = 56 * 1024 * 1024
MESH = pl.DeviceIdType.MESH
ANY = pl.BlockSpec(memory_space=pl.ANY)


def _cparams(n_axes, vmem=None):
    return pltpu.CompilerParams(dimension_semantics=("arbitrary",) * n_axes, vmem_limit_bytes=vmem)


def _dot(a, b):
    return jnp.dot(a, b, preferred_element_type=F32)


def _dot_nt(a, b):
    return lax.dot_general(a, b, (((1,), (1,)), ((), ())), preferred_element_type=F32)


def _dot_tn(a, b):
    return lax.dot_general(a, b, (((0,), (0,)), ((), ())), preferred_element_type=F32)


def _shift_down(ext, s, lead):
    return pltpu.roll(ext, s, 0)[lead:]


def _shift_up(ext, s, tm):
    n = ext.shape[0]
    return pltpu.roll(ext, n - s, 0)[:tm]


def _rms_inv(x):
    return lax.rsqrt(jnp.mean(x * x, axis=-1, keepdims=True) + RMS_EPS)


def _rms_bwd(dh, xn, inv, g):
    dxn = dh * g
    return inv * (dxn - xn * jnp.mean(dxn * xn, axis=-1, keepdims=True))


def _pos():
    return lax.axis_index("x"), lax.axis_index("y"), lax.axis_index("c")


def _handshake(peers):
    barrier = pltpu.get_barrier_semaphore()
    for peer in peers:
        pl.semaphore_signal(barrier, inc=1, device_id=peer, device_id_type=MESH)
    pl.semaphore_wait(barrier, len(peers))


def _sequencer(body, out_type, n_sems, name, collective_id):
    return pl.kernel(
        body, out_type=out_type, mesh=plsc.ScalarSubcoreMesh(axis_name="sequencer", num_cores=1), name=name,
        scratch_types=[pltpu.SemaphoreType.DMA((n_sems,)), pltpu.SemaphoreType.DMA((n_sems,))],
        compiler_params=pltpu.CompilerParams(collective_id=collective_id))


def all_gather_blocks(shards, name, collective_id):
    n = len(shards)

    def body(*refs):
        ins, outs = refs[:n], refs[n:2 * n]
        send_sems, recv_sems = refs[2 * n:]
        x, y, c = _pos()
        sibling = (x, y, 1 - c)
        chips = [(1 - x, y), (x, 1 - y), (1 - x, 1 - y)]
        _handshake([sibling] + [(*chip, c) for chip in chips])

        def copy(w, k, block, to, src=None):
            slot = outs[w].at[4 * block[0] + 2 * block[1] + block[2]]
            return pltpu.make_async_remote_copy(
                src_ref=slot if src is None else src, dst_ref=slot,
                send_sem=send_sems.at[8 * w + k], recv_sem=recv_sems.at[8 * w + k], device_id=to, device_id_type=MESH)

        mine, first, passed = [], [], []
        for w in range(n):
            m = pltpu.make_async_copy(ins[w], outs[w].at[4 * x + 2 * y + c], send_sems.at[8 * w + 7])
            m.start()
            mine.append(m)
            first.append(copy(w, 0, (x, y, c), sibling, src=ins[w]))
            first += [copy(w, 1 + j, (x, y, c), (*chip, c), src=ins[w]) for j, chip in enumerate(chips)]
        for cp in first:
            cp.start()
        for w in range(n):
            for j, chip in enumerate(chips):
                copy(w, 1 + j, (*chip, c), (x, y, c)).wait_recv()
                fw = copy(w, 4 + j, (*chip, c), sibling)
                fw.start()
                passed.append(fw)
        for w in range(n):
            copy(w, 0, (x, y, 1 - c), (x, y, c)).wait_recv()
            for j, chip in enumerate(chips):
                copy(w, 4 + j, (*chip, 1 - c), (x, y, c)).wait_recv()
        for cp in first + passed:
            cp.wait_send()
        for m in mine:
            m.wait()

    out = [jax.ShapeDtypeStruct((NDEV,) + s.shape, s.dtype) for s in shards]
    return _sequencer(body, out, 8 * n, name, collective_id)(*shards)


def _exchange_small(v, reduce, name, gathered=None):
    rows = v.shape[0]

    def body(*refs):
        if gathered is None:
            v_ref, out_ref, slots, send_sems, recv_sems, local_sem = refs
        else:
            v_ref, g_ref, out_ref, gsum_ref, slots, send_sems, recv_sems, local_sem = refs
        x, y, c = _pos()
        me = 4 * x + 2 * y + c
        mine = pltpu.make_async_copy(v_ref, slots.at[me], local_sem)
        mine.start()
        offs = [(dx, dy, dc) for dx in (0, 1) for dy in (0, 1) for dc in (0, 1)][1:]

        def copy(k, src_slot, to):
            return pltpu.make_async_remote_copy(
                src_ref=v_ref, dst_ref=slots.at[src_slot], send_sem=send_sems.at[k], recv_sem=recv_sems.at[k],
                device_id=to, device_id_type=MESH)

        sends = []
        for k, (dx, dy, dc) in enumerate(offs):
            cp = copy(k, me, (x ^ dx, y ^ dy, c ^ dc))
            cp.start()
            sends.append(cp)
        for k, (dx, dy, dc) in enumerate(offs):
            copy(k, 4 * (x ^ dx) + 2 * (y ^ dy) + (c ^ dc), (x, y, c)).wait_recv()
        for cp in sends:
            cp.wait_send()
        mine.wait()
        if reduce:
            acc = slots[0]
            for d in range(1, NDEV):
                acc = acc + slots[d]
            out_ref[...] = acc
        else:
            out_ref[...] = slots[...]
        if gathered is not None:
            acc = g_ref[0]
            for d in range(1, NDEV):
                acc = acc + g_ref[d]
            gsum_ref[...] = acc

    vmem = pl.BlockSpec(memory_space=pltpu.VMEM)
    out = jax.ShapeDtypeStruct((rows, D) if reduce else (NDEV, rows, D), F32)
    args, out_shape, out_specs = [v], out, vmem
    if gathered is not None:
        args.append(gathered)
        out_shape, out_specs = [out, jax.ShapeDtypeStruct(gathered.shape[1:], F32)], [vmem, vmem]
    return pl.pallas_call(
        body, name=name, out_shape=out_shape, in_specs=[vmem] * len(args), out_specs=out_specs,
        scratch_shapes=[pltpu.VMEM((NDEV, rows, D), F32), pltpu.SemaphoreType.DMA((7,)),
                        pltpu.SemaphoreType.DMA((7,)), pltpu.SemaphoreType.DMA],
    )(*args)


def reduce_scatter_d2d(grads, name, collective_id):
    n = len(grads)

    def body(*refs):
        ins, outs = refs[:n], refs[n:2 * n]
        send_sems, recv_sems = refs[2 * n:]
        x, y, c = _pos()
        _handshake([(x, y, 1 - c)])
        cps = []
        for w in range(n):
            cp = pltpu.make_async_remote_copy(
                src_ref=ins[w].at[1 - c], dst_ref=outs[w], send_sem=send_sems.at[w], recv_sem=recv_sems.at[w],
                device_id=(x, y, 1 - c), device_id_type=MESH)
            cp.start()
            cps.append(cp)
        for cp in cps:
            cp.wait_recv()
        for cp in cps:
            cp.wait_send()

    out = [jax.ShapeDtypeStruct(g.shape[1:], F32) for g in grads]
    return _sequencer(body, out, n, name, collective_id)(*grads)


def reduce_scatter_ici(parts, name, collective_id):
    n = len(parts)

    def body(*refs):
        ins, outs = refs[:n], refs[n:2 * n]
        send_sems, recv_sems = refs[2 * n:]
        x, y, c = _pos()
        offs = [(1, 0), (0, 1), (1, 1)]
        _handshake([(x ^ dx, y ^ dy, c) for dx, dy in offs])
        cps = []
        for w in range(n):
            for k, (dx, dy) in enumerate(offs):
                ox, oy = x ^ dx, y ^ dy
                cp = pltpu.make_async_remote_copy(
                    src_ref=ins[w].at[2 * ox + oy], dst_ref=outs[w].at[2 * x + y],
                    send_sem=send_sems.at[3 * w + k], recv_sem=recv_sems.at[3 * w + k],
                    device_id=(ox, oy, c), device_id_type=MESH)
                cp.start()
                cps.append((cp, w, k, ox, oy))
        for cp, w, k, ox, oy in cps:
            pltpu.make_async_remote_copy(
                src_ref=ins[w].at[2 * ox + oy], dst_ref=outs[w].at[2 * ox + oy],
                send_sem=send_sems.at[3 * w + k], recv_sem=recv_sems.at[3 * w + k],
                device_id=(ox, oy, c), device_id_type=MESH).wait_recv()
        for cp, *_ in cps:
            cp.wait_send()

    out = [jax.ShapeDtypeStruct(p.shape, BF16) for p in parts]
    return _sequencer(body, out, 3 * n, name, collective_id)(*parts)


def fwd_in(x, g1, w_in_g, tm):
    t = x.shape[0]

    def body(x_ref, g_ref, w_ref, z_ref, ht_ref):
        xf = x_ref[...]
        h = (xf * _rms_inv(xf) * g_ref[...]).astype(BF16)
        ht_ref[...] = h.T
        for j in range(NDEV):
            r = _dot(h, w_ref[j])
            for q in range(3):
                z_ref[3 * j + q] = r[:, q * CG:(q + 1) * CG].astype(BF16)

    return pl.pallas_call(
        body, name="fwd_in", grid=(t // tm,),
        in_specs=[pl.BlockSpec((tm, D), lambda i: (i, 0)), pl.BlockSpec((1, D), lambda i: (0, 0)),
                  pl.BlockSpec((NDEV, D, SH_IN), lambda i: (0, 0, 0))],
        out_specs=[pl.BlockSpec((NZT, tm, CG), lambda i: (0, i, 0)), pl.BlockSpec((D, tm), lambda i: (0, i))],
        out_shape=[jax.ShapeDtypeStruct((NZT, t, CG), BF16), jax.ShapeDtypeStruct((D, t), BF16)],
        compiler_params=_cparams(1, VMEM_BIG),
        cost_estimate=pl.CostEstimate(flops=2 * t * D * DIN, transcendentals=t,
                                      bytes_accessed=4 * t * D + 2 * D * DIN + 2 * t * DIN + 2 * t * D),
    )(x, g1, w_in_g)


def _pool_tile(z_ref, zh_ref, g, win, keep_hist, cnt):
    zt = z_ref[g].astype(F32)
    ext = jnp.concatenate([zh_ref[g].astype(F32) * keep_hist, zt], axis=0)
    s, sh = ext, 1
    while sh < win:
        s = s + pltpu.roll(s, sh, 0)
        sh *= 2
    return s[HALO:] / cnt - zt


def _conv_taps(ext, cur, w_ref, lanes, lead):
    x1 = _shift_down(ext, 1, lead)
    x2 = _shift_down(ext, 2, lead)
    out = w_ref[2:3, lanes] * cur + w_ref[1:2, lanes] * x1 + w_ref[0:1, lanes] * x2
    return out, x1, x2


def fwd_mix(z, x, pool_w, pool_scale, conv_w, wmix, tm, seq):
    t = x.shape[0]
    tps = seq // tm
    hb = tm // HALO

    def body(z_ref, zph_ref, zcvh_ref, x_ref, pw_ref, ps_ref, wpp_ref, cw_ref, wco_ref, wo_ref,
             x1_ref, yp_ref, yc_ref):
        it = pl.program_id(0) % tps
        keep_hist = jnp.where(it == 0, 0.0, 1.0)
        pos = it * tm + lax.broadcasted_iota(jnp.int32, (tm, 1), 0)
        p2 = []
        for g, win in enumerate(WINS):
            cnt = jnp.minimum(pos + 1, win).astype(F32)
            p = _pool_tile(z_ref, zph_ref, g, win, keep_hist, cnt)
            lanes = slice(g * CG, (g + 1) * CG)
            p2.append((_dot(p.astype(BF16), pw_ref[g]) * ps_ref[:, lanes]).astype(BF16))
        y_pool = _dot(jnp.concatenate(p2, axis=1), wpp_ref[...])
        u = []
        for q in range(NG):
            lanes = slice(q * CG, (q + 1) * CG)
            cv = z_ref[8 + q].astype(F32) * z_ref[12 + q].astype(F32)
            cvh = zcvh_ref[q].astype(F32) * zcvh_ref[4 + q].astype(F32) * keep_hist
            cc, _, _ = _conv_taps(jnp.concatenate([cvh, cv], axis=0), cv, cw_ref, lanes, HALO)
            u.append((z_ref[4 + q].astype(F32) * cc).astype(BF16))
        y_conv = _dot(jnp.concatenate(u, axis=1), wco_ref[...])
        ypb, ycb = y_pool.astype(BF16), y_conv.astype(BF16)
        yp_ref[...] = ypb
        yc_ref[...] = ycb
        merged = []
        for q in range(NG):
            lanes = slice(q * CG, (q + 1) * CG)
            sp = jax.nn.sigmoid(z_ref[16 + q].astype(F32))
            sc = jax.nn.sigmoid(z_ref[20 + q].astype(F32))
            merged.append((sp * ypb[:, lanes].astype(F32) + sc * ycb[:, lanes].astype(F32)).astype(BF16))
        x1_ref[...] = x_ref[...] + _dot(jnp.concatenate(merged, axis=1), wo_ref[...])

    def hist(i):
        return jnp.maximum(i * hb - 1, 0)

    const2 = lambda i: (0, 0)
    return pl.pallas_call(
        body, name="fwd_mix", grid=(t // tm,),
        in_specs=[pl.BlockSpec((NZT, tm, CG), lambda i: (0, i, 0)),
                  pl.BlockSpec((NG, HALO, CG), lambda i: (0, hist(i), 0)),
                  pl.BlockSpec((2 * NG, HALO, CG), lambda i: (1, hist(i), 0)),
                  pl.BlockSpec((tm, D), lambda i: (i, 0)),
                  pl.BlockSpec((NG, CG, CG), lambda i: (0, 0, 0)), pl.BlockSpec((1, D), const2),
                  pl.BlockSpec((D, D), lambda i: (0, MIX_POOL_PROJ)), pl.BlockSpec((3, D), const2),
                  pl.BlockSpec((D, D), lambda i: (0, MIX_CONV_OUT)), pl.BlockSpec((D, D), lambda i: (0, MIX_O))],
        out_specs=[pl.BlockSpec((tm, D), lambda i: (i, 0))] * 3,
        out_shape=[jax.ShapeDtypeStruct((t, D), F32), jax.ShapeDtypeStruct((t, D), BF16),
                   jax.ShapeDtypeStruct((t, D), BF16)],
        compiler_params=_cparams(1, VMEM_BIG),
    )(z, z, z, x, pool_w, pool_scale, wmix, conv_w, wmix, wmix)


def fwd_up(x1, g2, w_up_g, fcw, fcb, tm, seq, cps):
    t = x1.shape[0]
    tps = seq // tm

    def body(x1_ref, g2_ref, wup_ref, fcw_ref, fcb_ref,
             up_ref, pre_ref, act_ref, actt_ref, h2t_ref, hist_ref, h2_ref):
        i, k = pl.program_id(0), pl.program_id(1)
        keep_hist = jnp.where(i % tps == 0, 0.0, 1.0)

        @pl.when((i == 0) & (k == 0))
        def _():
            hist_ref[...] = jnp.zeros_like(hist_ref)

        @pl.when(k == 0)
        def _():
            x1v = x1_ref[...]
            h2 = (x1v * _rms_inv(x1v) * g2_ref[...]).astype(BF16)
            h2_ref[...] = h2
            h2t_ref[...] = h2.T

        h2 = h2_ref[...]
        lanes = slice(0, SH_UP)
        for c in range(cps):
            kc = k * cps + c
            conv = []
            for s in range(2):
                ub = _dot(h2, wup_ref[s, c]).astype(BF16)
                up_ref[s, c] = ub
                uf = ub.astype(F32)
                ext = jnp.concatenate([hist_ref[s, kc] * keep_hist, uf], axis=0)
                hist_ref[s, kc] = uf[tm - 8:]
                cc, _, _ = _conv_taps(ext, uf, fcw_ref.at[s, c], lanes, 8)
                conv.append(cc + fcb_ref[s, c])
                pre_ref[s, c] = conv[s].astype(BF16)
            a = (conv[0] * jax.nn.sigmoid(conv[0]) * conv[1]).astype(BF16)
            act_ref[c] = a
            actt_ref[c] = a.T

    tile = lambda i, k: (i, 0)
    const2 = lambda i, k: (0, 0)
    pair = lambda i, k: (0, k, 0, 0)
    chunk = pl.BlockSpec((2, cps, tm, SH_UP), lambda i, k: (0, k, i, 0))
    return pl.pallas_call(
        body, name="fwd_up", grid=(t // tm, NCH // cps),
        in_specs=[pl.BlockSpec((tm, D), tile), pl.BlockSpec((1, D), const2),
                  pl.BlockSpec((2, cps, D, SH_UP), pair), pl.BlockSpec((2, cps, 3, SH_UP), pair),
                  pl.BlockSpec((2, cps, 1, SH_UP), pair)],
        out_specs=[chunk, chunk, pl.BlockSpec((cps, tm, SH_UP), lambda i, k: (k, i, 0)),
                   pl.BlockSpec((cps, SH_UP, tm), lambda i, k: (k, 0, i)), pl.BlockSpec((D, tm), lambda i, k: (0, i))],
        out_shape=[jax.ShapeDtypeStruct((2, NCH, t, SH_UP), BF16), jax.ShapeDtypeStruct((2, NCH, t, SH_UP), BF16),
                   jax.ShapeDtypeStruct((NCH, t, SH_UP), BF16), jax.ShapeDtypeStruct((NCH, SH_UP, t), BF16),
                   jax.ShapeDtypeStruct((D, t), BF16)],
        scratch_shapes=[pltpu.VMEM((2, NCH, 8, SH_UP), F32), pltpu.VMEM((tm, D), BF16)],
        compiler_params=_cparams(2, VMEM_BIG),
    )(x1, g2, w_up_g.reshape(2, NCH, D, SH_UP), fcw.reshape(2, NCH, 3, SH_UP), fcb.reshape(2, NCH, 1, SH_UP))


def fwd_down(x1, act, w_dn, gf, tgt, tm):
    t = x1.shape[0]

    def body(x1_ref, act_ref, wdn_ref, gf_ref, tgt_ref, dx2_ref, vec_ref):
        @pl.when(pl.program_id(0) == 0)
        def _():
            vec_ref[...] = jnp.zeros_like(vec_ref)

        d = None
        for c in range(NCH):
            part = _dot(act_ref[c], wdn_ref[c])
            d = part if d is None else d + part
        x2 = x1_ref[...] + d
        inv3 = _rms_inv(x2)
        xn = x2 * inv3
        diff = xn * gf_ref[...] - tgt_ref[...]
        dy = diff * (1.0 / D)
        vec_ref[0:1, :] += jnp.sum(dy * xn, axis=0, keepdims=True)
        vec_ref[1:2, :] += 0.5 * jnp.sum(jnp.mean(diff * diff, axis=-1))
        dx2_ref[...] = _rms_bwd(dy, xn, inv3, gf_ref[...])

    tile = lambda i: (i, 0)
    const2 = lambda i: (0, 0)
    return pl.pallas_call(
        body, name="fwd_down", grid=(t // tm,),
        in_specs=[pl.BlockSpec((tm, D), tile), pl.BlockSpec((NCH, tm, SH_UP), lambda i: (0, i, 0)),
                  pl.BlockSpec((NCH, SH_UP, D), lambda i: (0, 0, 0)), pl.BlockSpec((1, D), const2),
                  pl.BlockSpec((tm, D), tile)],
        out_specs=[pl.BlockSpec((tm, D), tile), pl.BlockSpec((8, D), const2)],
        out_shape=[jax.ShapeDtypeStruct((t, D), F32), jax.ShapeDtypeStruct((8, D), F32)],
        compiler_params=_cparams(1, VMEM_BIG),
    )(x1, act, w_dn, gf, tgt)


def bwd_ffn(dx2, x1, g2, up, pre, w_up_g, fcw, w_dn, tm, seq, cps):
    t = x1.shape[0]
    nt = t // tm
    tps = seq // tm

    def body(dx2_ref, x1_ref, g2_ref, up_ref, pre_ref, wup_ref, fcw_ref, wdn_ref,
             dup_ref, dx1_ref, gvec_ref, gn_ref, carry_ref, dh2_ref, acc_ref):
        i, k = pl.program_id(0), pl.program_id(1)
        it = (nt - 1 - i) % tps
        keep_next = jnp.where(it == tps - 1, 0.0, 1.0)

        @pl.when((i == 0) & (k == 0))
        def _():
            acc_ref[...] = jnp.zeros_like(acc_ref)
            gn_ref[...] = jnp.zeros_like(gn_ref)
            carry_ref[...] = jnp.zeros_like(carry_ref)

        @pl.when(k == 0)
        def _():
            dh2_ref[...] = jnp.zeros_like(dh2_ref)

        dxb = dx2_ref[...].astype(BF16)
        lanes = slice(0, SH_UP)
        dh2 = dh2_ref[...]
        for c in range(cps):
            kc = k * cps + c
            pre = [pre_ref[s, c].astype(F32) for s in range(2)]
            sg = jax.nn.sigmoid(pre[0])
            dact = _dot_nt(dxb, wdn_ref[c])
            dpre = [dact * pre[1] * (sg * (1.0 + pre[0] * (1.0 - sg))), dact * (pre[0] * sg)]
            for s in range(2):
                dc = dpre[s]
                ext = jnp.concatenate([dc, carry_ref[s, kc] * keep_next], axis=0)
                carry_ref[s, kc] = dc[:8]
                shifted = (_shift_up(ext, 2, tm), _shift_up(ext, 1, tm), dc)
                uf = up_ref[s, c].astype(F32)
                acc_ref[s, kc, 0:1, lanes] += jnp.sum(dc, axis=0, keepdims=True)
                for tap in range(3):
                    acc_ref[s, kc, tap + 1:tap + 2, lanes] += jnp.sum(shifted[tap] * uf, axis=0, keepdims=True)
                w = fcw_ref.at[s, c]
                du = w[2:3, :] * dc + w[1:2, :] * shifted[1] + w[0:1, :] * shifted[0]
                dub = du.astype(BF16)
                dup_ref[s, c] = dub
                dh2 = dh2 + _dot_nt(dub, wup_ref[s, c])
        dh2_ref[...] = dh2

        @pl.when(k == NCH // cps - 1)
        def _():
            x1v = x1_ref[...]
            inv2 = _rms_inv(x1v)
            xn = x1v * inv2
            gn_ref[0:1, :] += jnp.sum(dh2 * xn, axis=0, keepdims=True)
            dx1_ref[...] = dx2_ref[...] + _rms_bwd(dh2, xn, inv2, g2_ref[...])

        @pl.when((i == nt - 1) & (k == NCH // cps - 1))
        def _():
            gvec_ref[...] = acc_ref[...]

    rev = lambda i, k: (nt - 1 - i, 0)
    const2 = lambda i, k: (0, 0)
    pair = lambda i, k: (0, k, 0, 0)
    return pl.pallas_call(
        body, name="bwd_ffn", grid=(nt, NCH // cps),
        in_specs=[pl.BlockSpec((tm, D), rev), pl.BlockSpec((tm, D), rev), pl.BlockSpec((1, D), const2),
                  pl.BlockSpec((2, cps, tm, SH_UP), lambda i, k: (0, k, nt - 1 - i, 0)),
                  pl.BlockSpec((2, cps, tm, SH_UP), lambda i, k: (0, k, nt - 1 - i, 0)),
                  pl.BlockSpec((2, cps, D, SH_UP), pair), pl.BlockSpec((2, cps, 3, SH_UP), pair),
                  pl.BlockSpec((cps, SH_UP, D), lambda i, k: (k, 0, 0))],
        out_specs=[pl.BlockSpec((2, cps, tm, SH_UP), lambda i, k: (0, k, nt - 1 - i, 0)), pl.BlockSpec((tm, D), rev),
                   pl.BlockSpec((2, NCH, 8, D), lambda i, k: (0, 0, 0, 0)), pl.BlockSpec((8, D), const2)],
        out_shape=[jax.ShapeDtypeStruct((2, NCH, t, SH_UP), BF16), jax.ShapeDtypeStruct((t, D), F32),
                   jax.ShapeDtypeStruct((2, NCH, 8, D), F32), jax.ShapeDtypeStruct((8, D), F32)],
        scratch_shapes=[pltpu.VMEM((2, NCH, 8, SH_UP), F32), pltpu.VMEM((tm, D), F32),
                        pltpu.VMEM((2, NCH, 8, D), F32)],
        compiler_params=_cparams(2, VMEM_BIG),
    )(dx2, x1, g2, up, pre, w_up_g.reshape(2, NCH, D, SH_UP), fcw.reshape(2, NCH, 3, SH_UP), w_dn)


def bwd_mix(dx1, z, y_pool, y_conv, pool_w, pool_scale, conv_w, wmix, tm, seq):
    t = dx1.shape[0]
    nt = t // tm
    tps = seq // tm
    hb = tm // HALO

    def body(da_ref, z_ref, zph_ref, zcvh_ref, yp_ref, yc_ref, pw_ref, ps_ref, wpp_ref, cw_ref, wco_ref, wo_ref,
             dz_ref, mg_ref, p2_ref, u_ref, dyp_ref, dyc_ref, p_ref, dpw_ref, gvec_ref, cp_ref, cc_ref):
        i = pl.program_id(0)
        it = (nt - 1 - i) % tps
        keep_hist = jnp.where(it == 0, 0.0, 1.0)
        keep_next = jnp.where(it == tps - 1, 0.0, 1.0)
        pos = it * tm + lax.broadcasted_iota(jnp.int32, (tm, 1), 0)

        @pl.when(i == 0)
        def _():
            gvec_ref[...] = jnp.zeros_like(gvec_ref)
            cp_ref[...] = jnp.zeros_like(cp_ref)
            cc_ref[...] = jnp.zeros_like(cc_ref)

        dm = _dot_nt(da_ref[...].astype(BF16), wo_ref[...])
        merged, dyp, dyc = [], [], []
        for q in range(NG):
            lanes = slice(q * CG, (q + 1) * CG)
            sp = jax.nn.sigmoid(z_ref[16 + q].astype(F32))
            sc = jax.nn.sigmoid(z_ref[20 + q].astype(F32))
            yp = yp_ref[:, lanes].astype(F32)
            yc = yc_ref[:, lanes].astype(F32)
            dmq = dm[:, lanes]
            merged.append((sp * yp + sc * yc).astype(BF16))
            dyp.append((dmq * sp).astype(BF16))
            dyc.append((dmq * sc).astype(BF16))
            dz_ref[16 + q] = (dmq * yp * (sp * (1.0 - sp))).astype(BF16)
            dz_ref[20 + q] = (dmq * yc * (sc * (1.0 - sc))).astype(BF16)
        mg_ref[...] = jnp.concatenate(merged, axis=1)
        dypb = jnp.concatenate(dyp, axis=1)
        dycb = jnp.concatenate(dyc, axis=1)
        dyp_ref[...] = dypb
        dyc_ref[...] = dycb

        dp2 = _dot_nt(dypb, wpp_ref[...])
        p2 = []
        for g, win in enumerate(WINS):
            lanes = slice(g * CG, (g + 1) * CG)
            cnt = jnp.minimum(pos + 1, win).astype(F32)
            p = _pool_tile(z_ref, zph_ref, g, win, keep_hist, cnt)
            pb = p.astype(BF16)
            p_ref[g] = pb
            pw = _dot(pb, pw_ref[g])
            p2.append((pw * ps_ref[:, lanes]).astype(BF16))
            dp2g = dp2[:, lanes]
            gvec_ref[0:1, lanes] += jnp.sum(dp2g * pw, axis=0, keepdims=True)
            dpwb = (dp2g * ps_ref[:, lanes]).astype(BF16)
            dpw_ref[g] = dpwb
            dp = _dot_nt(dpwb, pw_ref[g])
            qv = dp / cnt
            ext = jnp.concatenate([qv, cp_ref[g] * keep_next], axis=0)
            cp_ref[g] = qv[:HALO]
            n = tm + HALO
            s, sh = ext, 1
            while sh < win:
                s = s + pltpu.roll(s, n - sh, 0)
                sh *= 2
            dz_ref[g] = (s[:tm] - dp).astype(BF16)
        p2_ref[...] = jnp.concatenate(p2, axis=1)

        du = _dot_nt(dycb, wco_ref[...])
        u = []
        for q in range(NG):
            lanes = slice(q * CG, (q + 1) * CG)
            zb = z_ref[4 + q].astype(F32)
            zc = z_ref[8 + q].astype(F32)
            zv = z_ref[12 + q].astype(F32)
            cv = zc * zv
            cvh = zcvh_ref[q].astype(F32) * zcvh_ref[4 + q].astype(F32) * keep_hist
            cc, cv1, cv2 = _conv_taps(jnp.concatenate([cvh, cv], axis=0), cv, cw_ref, lanes, HALO)
            u.append((zb * cc).astype(BF16))
            duq = du[:, lanes]
            dz_ref[4 + q] = (duq * cc).astype(BF16)
            dcc = duq * zb
            for tap, src in enumerate((cv2, cv1, cv)):
                gvec_ref[tap + 1:tap + 2, lanes] += jnp.sum(dcc * src, axis=0, keepdims=True)
            ext = jnp.concatenate([dcc, cc_ref[:, lanes] * keep_next], axis=0)
            cc_ref[:, lanes] = dcc[:8]
            dcv = (cw_ref[2:3, lanes] * dcc + cw_ref[1:2, lanes] * _shift_up(ext, 1, tm)
                   + cw_ref[0:1, lanes] * _shift_up(ext, 2, tm))
            dz_ref[8 + q] = (dcv * zv).astype(BF16)
            dz_ref[12 + q] = (dcv * zc).astype(BF16)
        u_ref[...] = jnp.concatenate(u, axis=1)

    def hist(i):
        return jnp.maximum((nt - 1 - i) * hb - 1, 0)

    rev = lambda i: (nt - 1 - i, 0)
    rev3 = lambda i: (0, nt - 1 - i, 0)
    const2 = lambda i: (0, 0)
    tok = jax.ShapeDtypeStruct((t, D), BF16)
    grp = jax.ShapeDtypeStruct((NG, t, CG), BF16)
    return pl.pallas_call(
        body, name="bwd_mix", grid=(nt,),
        in_specs=[pl.BlockSpec((tm, D), rev), pl.BlockSpec((NZT, tm, CG), rev3),
                  pl.BlockSpec((NG, HALO, CG), lambda i: (0, hist(i), 0)),
                  pl.BlockSpec((2 * NG, HALO, CG), lambda i: (1, hist(i), 0)),
                  pl.BlockSpec((tm, D), rev), pl.BlockSpec((tm, D), rev),
                  pl.BlockSpec((NG, CG, CG), lambda i: (0, 0, 0)), pl.BlockSpec((1, D), const2),
                  pl.BlockSpec((D, D), lambda i: (0, MIX_POOL_PROJ)), pl.BlockSpec((3, D), const2),
                  pl.BlockSpec((D, D), lambda i: (0, MIX_CONV_OUT)), pl.BlockSpec((D, D), lambda i: (0, MIX_O))],
        out_specs=[pl.BlockSpec((NZT, tm, CG), rev3)] + [pl.BlockSpec((tm, D), rev)] * 5
                  + [pl.BlockSpec((NG, tm, CG), rev3)] * 2 + [pl.BlockSpec((8, D), const2)],
        out_shape=[jax.ShapeDtypeStruct((NZT, t, CG), BF16), tok, tok, tok, tok, tok, grp, grp,
                   jax.ShapeDtypeStruct((8, D), F32)],
        scratch_shapes=[pltpu.VMEM((NG, HALO, CG), F32), pltpu.VMEM((8, D), F32)],
        compiler_params=_cparams(1, VMEM_BIG),
    )(dx1, z, z, z, y_pool, y_conv, pool_w, pool_scale, wmix, conv_w, wmix, wmix)


def bwd_in(dz, w_in_g, dx1, x, g1, tm):
    t = x.shape[0]

    def body(dz_ref, w_ref, dx1_ref, x_ref, g_ref, gx_ref, gn_ref):
        @pl.when(pl.program_id(0) == 0)
        def _():
            gn_ref[...] = jnp.zeros_like(gn_ref)

        dh = None
        for j in range(NDEV):
            dzc = jnp.concatenate([dz_ref[3 * j + q] for q in range(3)], axis=1)
            part = _dot_nt(dzc, w_ref[j])
            dh = part if dh is None else dh + part
        xv = x_ref[...]
        inv = _rms_inv(xv)
        xn = xv * inv
        gn_ref[0:1, :] += jnp.sum(dh * xn, axis=0, keepdims=True)
        gx_ref[...] = dx1_ref[...] + _rms_bwd(dh, xn, inv, g_ref[...])

    tile = lambda i: (i, 0)
    return pl.pallas_call(
        body, name="bwd_in", grid=(t // tm,),
        in_specs=[pl.BlockSpec((NZT, tm, CG), lambda i: (0, i, 0)),
                  pl.BlockSpec((NDEV, D, SH_IN), lambda i: (0, 0, 0)),
                  pl.BlockSpec((tm, D), tile), pl.BlockSpec((tm, D), tile), pl.BlockSpec((1, D), lambda i: (0, 0))],
        out_specs=[pl.BlockSpec((tm, D), tile), pl.BlockSpec((8, D), lambda i: (0, 0))],
        out_shape=[jax.ShapeDtypeStruct((t, D), F32), jax.ShapeDtypeStruct((8, D), F32)],
        compiler_params=_cparams(1, VMEM_BIG),
    )(dz, w_in_g, dx1, x, g1)


def _slot(j):
    return j % 2, j // 2


def wgrad_cols(at, b, q, name, tk):
    m, t = at.shape
    width = b.shape[3]

    def body(a_ref, b_ref, o_ref):
        @pl.when(pl.program_id(1) == 0)
        def _():
            o_ref[...] = jnp.zeros_like(o_ref)

        o_ref[...] += _dot(a_ref[...], b_ref[...])

    return pl.pallas_call(
        body, name=name, grid=(NDEV, t // tk),
        in_specs=[pl.BlockSpec((m, tk), lambda j, k: (0, k)),
                  pl.BlockSpec((None, None, tk, width), lambda j, k: (j, q, k, 0))],
        out_specs=pl.BlockSpec((None, None, m, width), lambda j, k: (j % 2, j // 2, 0, 0)),
        out_shape=jax.ShapeDtypeStruct((2, 4, m, width), F32),
        compiler_params=_cparams(2, VMEM_BIG),
    )(at, b)


def wgrad_cols_resident(at, b, q, name, tk):
    m, t = at.shape
    width = b.shape[3]

    def body(a_ref, b_ref, o_ref):
        k, j = pl.program_id(0), pl.program_id(1)

        @pl.when((k == 0) & (j == 0))
        def _():
            o_ref[...] = jnp.zeros_like(o_ref)

        o_ref[j % 2, j // 2] += _dot(a_ref[...], b_ref[...])

    return pl.pallas_call(
        body, name=name, grid=(t // tk, NDEV),
        in_specs=[pl.BlockSpec((m, tk), lambda k, j: (0, k)),
                  pl.BlockSpec((None, None, tk, width), lambda k, j: (j, q, k, 0))],
        out_specs=pl.BlockSpec((2, 4, m, width), lambda k, j: (0, 0, 0, 0)),
        out_shape=jax.ShapeDtypeStruct((2, 4, m, width), F32),
        compiler_params=_cparams(2, VMEM_BIG),
    )(at, b)


def wgrad_down(actt, dx2, tk):
    t = dx2.shape[0]

    def body(a_ref, b_ref, o_ref, acc_ref):
        kt = pl.program_id(1)

        @pl.when(kt == 0)
        def _():
            acc_ref[...] = jnp.zeros_like(acc_ref)

        acc_ref[...] += _dot(a_ref[...], b_ref[...].astype(BF16))

        @pl.when(kt == pl.num_programs(1) - 1)
        def _():
            o_ref[0] = acc_ref[:SH_DN]
            o_ref[1] = acc_ref[SH_DN:]

    return pl.pallas_call(
        body, name="wgrad_down", grid=(NCH, t // tk),
        in_specs=[pl.BlockSpec((None, SH_UP, tk), lambda k, kt: (k, 0, kt)), pl.BlockSpec((tk, D), lambda k, kt: (kt, 0))],
        out_specs=pl.BlockSpec((2, None, SH_DN, D), lambda k, kt: (0, k, 0, 0)),
        out_shape=jax.ShapeDtypeStruct((2, 4, SH_DN, D), F32),
        scratch_shapes=[pltpu.VMEM((SH_UP, D), F32)],
        compiler_params=_cparams(2, VMEM_BIG),
    )(actt, dx2)


def wgrad_square(a, b, name, tk):
    t = a.shape[0]

    def body(a_ref, b_ref, o_ref, acc_ref):
        kt = pl.program_id(0)

        @pl.when(kt == 0)
        def _():
            acc_ref[...] = jnp.zeros_like(acc_ref)

        acc_ref[...] += _dot_tn(a_ref[...], b_ref[...].astype(BF16))

        @pl.when(kt == pl.num_programs(0) - 1)
        def _():
            for j in range(NDEV):
                cc, xy = _slot(j)
                o_ref[cc, xy] = acc_ref[j * 128:(j + 1) * 128]

    return pl.pallas_call(
        body, name=name, grid=(t // tk,),
        in_specs=[pl.BlockSpec((tk, D), lambda k: (k, 0)), pl.BlockSpec((tk, D), lambda k: (k, 0))],
        out_specs=pl.BlockSpec((2, 4, 128, D), lambda k: (0, 0, 0, 0)),
        out_shape=jax.ShapeDtypeStruct((2, 4, 128, D), F32),
        scratch_shapes=[pltpu.VMEM((D, D), F32)],
        compiler_params=_cparams(1, VMEM_BIG),
    )(a, b)


def wgrad_pool(p, dpw, tk):
    t = p.shape[1]

    def body(a_ref, b_ref, o_ref):
        @pl.when(pl.program_id(0) == 0)
        def _():
            o_ref[...] = jnp.zeros_like(o_ref)

        for g in range(NG):
            o_ref[g] += _dot_tn(a_ref[g], b_ref[g])

    return pl.pallas_call(
        body, name="wgrad_pool", grid=(t // tk,),
        in_specs=[pl.BlockSpec((NG, tk, CG), lambda k: (0, k, 0))] * 2,
        out_specs=pl.BlockSpec((NG, CG, CG), lambda k: (0, 0, 0)),
        out_shape=jax.ShapeDtypeStruct((NG, CG, CG), F32),
        compiler_params=_cparams(1, VMEM_BIG),
    )(p, dpw)


def _adamw(w, g, m, v):
    m = ADAM_B1 * m + (1.0 - ADAM_B1) * g
    v = ADAM_B2 * v + (1.0 - ADAM_B2) * (g * g)
    m_hat = m / (1.0 - ADAM_B1 ** ADAM_STEP)
    v_hat = v / (1.0 - ADAM_B2 ** ADAM_STEP)
    delta = -ADAM_LR * (m_hat / (jnp.sqrt(v_hat) + ADAM_EPS) + ADAM_WD * w)
    return delta, m, v


def _row_block(r):
    return 512 if r % 512 == 0 else r


def chip_partial(place, g, from_sibling, name):
    _, _, r, c = g.shape

    def body(place_ref, g_ref, s_ref, o_ref):
        o_ref[...] = (g_ref[...] + s_ref[...]).astype(BF16)

    return pl.pallas_call(
        body, name=name,
        grid_spec=pltpu.PrefetchScalarGridSpec(
            num_scalar_prefetch=1, grid=(3,),
            in_specs=[pl.BlockSpec((None, None, r, c), lambda k, pr: (pr[0], pr[1] ^ (k + 1), 0, 0)),
                      pl.BlockSpec((None, r, c), lambda k, pr: (pr[1] ^ (k + 1), 0, 0))],
            out_specs=pl.BlockSpec((None, r, c), lambda k, pr: (pr[1] ^ (k + 1), 0, 0))),
        out_shape=jax.ShapeDtypeStruct((4, r, c), BF16),
        compiler_params=_cparams(1, VMEM_BIG),
    )(place, g, from_sibling)


def finish_adamw(place, gs, from_sibling, from_chips, w, m, v, name):
    n = len(gs)
    r = gs[0].shape[2]
    widths = [g.shape[3] for g in gs]
    c = sum(widths)
    br = _row_block(r)

    def body(place_ref, *refs):
        g_refs, s_refs, c_refs = refs[:n], refs[n:2 * n], refs[2 * n:5 * n]
        w_ref, m_ref, v_ref, og_ref, od_ref, om_ref, ov_ref = refs[5 * n:]
        cols = []
        for q in range(n):
            grad = g_refs[q][...] + s_refs[q][...]
            for k in range(3):
                grad = grad + c_refs[3 * q + k][...].astype(F32)
            cols.append(grad)
        grad = cols[0] if n == 1 else jnp.concatenate(cols, axis=1)
        og_ref[...] = grad
        od_ref[...], om_ref[...], ov_ref[...] = _adamw(w_ref[...], grad, m_ref[...], v_ref[...])

    def other(k, cq):
        return pl.BlockSpec((None, br, cq), lambda i, pr: (pr[1] ^ k, i, 0))

    row = pl.BlockSpec((br, c), lambda i, pr: (i, 0))
    out = jax.ShapeDtypeStruct((r, c), F32)
    in_specs = [pl.BlockSpec((None, None, br, cq), lambda i, pr: (pr[0], pr[1], i, 0)) for cq in widths]
    in_specs += [pl.BlockSpec((None, br, cq), lambda i, pr: (pr[1], i, 0)) for cq in widths]
    in_specs += [other(k, cq) for cq in widths for k in (1, 2, 3)]
    return pl.pallas_call(
        body, name=name,
        grid_spec=pltpu.PrefetchScalarGridSpec(
            num_scalar_prefetch=1, grid=(r // br,), in_specs=in_specs + [row, row, row], out_specs=[row] * 4),
        out_shape=[out] * 4,
        compiler_params=_cparams(1, VMEM_BIG),
    )(place, *gs, *from_sibling, *[fc for fc in from_chips for _ in range(3)], w, m, v)


def adamw_small(items):
    n = len(items)

    def body(*refs):
        ins, outs = refs[:4 * n], refs[4 * n:]
        for i in range(n):
            w, g, m, v = (r[...] for r in ins[4 * i:4 * i + 4])
            outs[3 * i][...], outs[3 * i + 1][...], outs[3 * i + 2][...] = _adamw(w, g, m, v)

    out = [jax.ShapeDtypeStruct(it[0].shape, F32) for it in items for _ in range(3)]
    res = pl.pallas_call(body, name="adamw_small", out_shape=out)(*[a for it in items for a in it])
    return [res[3 * i:3 * i + 3] for i in range(n)]


def kernel(x, norm_mix, w_in, pool_w, pool_scale, w_pool_proj, conv_w, w_conv_out, w_o, norm_ffn, w_up, ffn_conv_w, ffn_conv_b, w_down, norm_final, loss_target, m_norm_mix, m_w_in, m_pool_w, m_pool_scale, m_w_pool_proj, m_conv_w, m_w_conv_out, m_w_o, m_norm_ffn, m_w_up, m_ffn_conv_w, m_ffn_conv_b, m_w_down, m_norm_final, v_norm_mix, v_w_in, v_pool_w, v_pool_scale, v_w_pool_proj, v_conv_w, v_w_conv_out, v_w_o, v_norm_ffn, v_w_up, v_ffn_conv_w, v_ffn_conv_b, v_w_down, v_norm_final):
    nb, seq, _ = x.shape
    t = nb * seq
    tm_in = min(TM_IN, t)
    tm_mix = min(TM_MIX, seq)
    tm_ffn = min(TM_FFN, seq)
    tk = min(TK_WGRAD, t)
    xt = x.reshape(t, D)
    tgt = loss_target.reshape(t, D)
    xi, yi, ci = _pos()
    me = 4 * xi + 2 * yi + ci
    place = jnp.stack([ci, 2 * xi + yi]).astype(jnp.int32)

    tie = lax.optimization_barrier
    w_in_g, = all_gather_blocks([w_in[0].astype(BF16)], "all_gather_w_in", 0)
    taps = (jnp.pad(conv_w[0], ((0, 5), (0, D - 128))) + jnp.pad(ffn_conv_w[0], ((3, 2), (0, D - SH_UP))))
    taps_g = _exchange_small(taps, False, "all_gather_taps")
    mix_shard = jnp.concatenate(
        [w_pool_proj[0], w_conv_out[0], w_o[0], pool_w[0].reshape(NG * 32, CG)], axis=1).astype(BF16)
    mix_shard, taps_g = tie((mix_shard, taps_g))
    wmix_g, = all_gather_blocks([mix_shard], "all_gather_w_mix", 0)
    ffn_shards, w_in_g = tie(([w_up[0].astype(BF16), w_down[0].astype(BF16)], w_in_g))
    w_up_g, = all_gather_blocks(ffn_shards[:1], "all_gather_w_up", 0)
    w_dn_g, = all_gather_blocks(ffn_shards[1:], "all_gather_w_down", 0)
    w_dn_f = w_dn_g.reshape(NCH, SH_UP, D)
    conv_w_f = taps_g[:, 0:3, :128].transpose(1, 0, 2).reshape(3, D)
    fcw_f = taps_g[:, 3:6, :SH_UP]
    fcb_f = ffn_conv_b.reshape(NDEV, 1, SH_UP)
    gfin = norm_final.reshape(1, D)

    z, h1 = fwd_in(xt, norm_mix, w_in_g, tm_in)
    wmix_g, z = tie((wmix_g, z))
    wmix = wmix_g.reshape(D, MIX_COLS)
    pool_w_f = wmix_g[:, :, 3 * D:].reshape(NDEV, NG, 32, CG).transpose(1, 0, 2, 3).reshape(NG, CG, CG)
    x1, y_pool, y_conv = fwd_mix(z, xt, pool_w_f, pool_scale, conv_w_f, wmix, tm_mix, seq)
    up, pre, act_tok, act, h2 = fwd_up(x1, norm_ffn, w_up_g, fcw_f, fcb_f, tm_ffn, seq, FFN_CHUNKS_PER_STEP)
    dx2, ffn_vec = fwd_down(x1, act_tok, w_dn_f, gfin, tgt, min(TM_IN, t))

    def to_sibling(full, tag):
        return reduce_scatter_d2d(full, "reduce_scatter_d2d_" + tag, 1)

    def partials(full, from_sib, names):
        return [chip_partial(place, g, s, "chip_partial_" + nm) for g, s, nm in zip(full, from_sib, names)]

    def to_chips(parts, tag):
        return reduce_scatter_ici(parts, "reduce_scatter_ici_" + tag, 2)

    def finish(nm, gs, from_sib, from_chips, wmv):
        w, m, v = wmv
        rc = (gs[0].shape[2], sum(g.shape[3] for g in gs))
        outs = finish_adamw(place, gs, from_sib, from_chips, w.reshape(rc), m.reshape(rc), v.reshape(rc), "adamw_" + nm)
        return [o.reshape(w.shape) for o in outs]

    def after(x, dep):
        return tie((x, dep))[0]

    big = {}
    d_up, dx1, g_ffn_vec, g_nffn = bwd_ffn(dx2, x1, norm_ffn, up, pre, w_up_g, fcw_f, w_dn_f, tm_ffn, seq,
                                           FFN_CHUNKS_PER_STEP)
    gw_up = wgrad_cols(h2, d_up.reshape(NDEV, 1, t, SH_UP), 0, "wgrad_up", t)
    sib_up = to_sibling([gw_up], "w_up")
    gw_dn = wgrad_down(act, after(dx2, gw_up), tk)
    sib_dn = to_sibling([after(gw_dn, sib_up)], "w_down")
    dx1, part_up = tie((dx1, partials([gw_up], sib_up, ["w_up"])))
    chips_up = to_chips(part_up, "w_up")
    dz, merged, p2, u, dyp, dyc, p, dpw, g_mix_vec = bwd_mix(
        dx1, z, y_pool, y_conv, pool_w_f, pool_scale, conv_w_f, wmix, tm_mix, seq)
    merged, part_dn = tie((merged, partials([gw_dn], sib_dn, ["w_down"])))
    chips_dn = to_chips(part_dn, "w_down")
    gw_o = wgrad_square(merged, dx1, "wgrad_o", tk)
    gw_pp = wgrad_square(p2, dyp, "wgrad_pool_proj", tk)
    gw_co = wgrad_square(u, dyc, "wgrad_conv_out", tk)
    gw_pool = wgrad_pool(p, dpw, tk).reshape(NG, 4, 2, 32, CG).transpose(2, 1, 0, 3, 4).reshape(2, 4, NG * 32, CG)
    dz8 = dz.reshape(NDEV, 3, t, CG)
    gw_in, sib_in, chips_in = [None] * 3, [None] * 3, [None] * 3
    gw_in[0] = wgrad_cols_resident(h1, dz8, 0, "wgrad_in_0", t)
    sib_a = to_sibling(after([gw_o, gw_pp], (chips_up, chips_dn, gw_in[0], gw_co)), "mix_a")
    sib_b = to_sibling(after([gw_co, gw_pool], sib_a), "mix_b")
    sib_in[0] = to_sibling(after([gw_in[0]], sib_b), "w_in_0")
    gw_in[1] = wgrad_cols_resident(h1, dz8, 1, "wgrad_in_1", t)
    sib_in[1] = to_sibling(after([gw_in[1]], sib_in[0]), "w_in_1")
    h1, part_a, part_b, part_in0 = tie((h1, partials([gw_o, gw_pp], sib_a, ["w_o", "w_pool_proj"]),
                                        partials([gw_co, gw_pool], sib_b, ["w_conv_out", "pool_w"]),
                                        partials([gw_in[0]], sib_in[0], ["w_in_0"])))
    chips_a = to_chips(after(part_a, sib_in[1]), "mix_a")
    chips_b = to_chips(part_b, "mix_b")
    chips_in[0] = to_chips(part_in0, "w_in_0")
    h1, big["w_down"], big["w_up"] = tie((
        h1, finish("w_down", [gw_dn], sib_dn, chips_dn, (w_down, m_w_down, v_w_down)),
        finish("w_up", [gw_up], sib_up, chips_up, (w_up, m_w_up, v_w_up))))
    gw_in[2] = wgrad_cols_resident(h1, dz8, 2, "wgrad_in_2", t)
    sib_in[2] = to_sibling(after([gw_in[2]], (chips_a, chips_b, chips_in[0])), "w_in_2")
    dx1, part_in1, part_in2, big["w_o"], big["w_pool_proj"], big["w_conv_out"], big["pool_w"] = tie((
        dx1, partials([gw_in[1]], sib_in[1], ["w_in_1"]), partials([gw_in[2]], sib_in[2], ["w_in_2"]),
        finish("w_o", [gw_o], sib_a[:1], chips_a[:1], (w_o, m_w_o, v_w_o)),
        finish("w_pool_proj", [gw_pp], sib_a[1:], chips_a[1:], (w_pool_proj, m_w_pool_proj, v_w_pool_proj)),
        finish("w_conv_out", [gw_co], sib_b[:1], chips_b[:1], (w_conv_out, m_w_conv_out, v_w_conv_out)),
        finish("pool_w", [gw_pool], sib_b[1:], chips_b[1:], (pool_w, m_pool_w, v_pool_w))))
    chips_in[1] = to_chips(after(part_in1, sib_in[2]), "w_in_1")
    chips_in[2] = to_chips(part_in2, "w_in_2")
    small_g, = all_gather_blocks(
        [after(jnp.concatenate([g_mix_vec, g_nffn, ffn_vec, g_ffn_vec.reshape(8 * NDEV, D)], axis=0), sib_in[2])],
        "all_gather_small", 0)
    grad_x, g_nmix = bwd_in(dz, w_in_g, dx1, xt, norm_mix, min(TM_BWD_IN, t))
    grad_x, chips_in = tie((grad_x, chips_in))
    big["w_in"] = finish("w_in", gw_in, [s[0] for s in sib_in], [c[0] for c in chips_in], (w_in, m_w_in, v_w_in))

    red_n, red = _exchange_small(g_nmix, True, "all_reduce_small", gathered=small_g)
    g_norm_mix, g_pool_scale, g_norm_ffn = red_n[0:1], red[0:1], red[8:9]
    g_conv_w = lax.dynamic_slice(red, (1, me * 128), (3, 128))
    g_norm_final = red[16]
    loss = red[17, 0]
    g_fcb = red[24:].reshape(NDEV, 8, D)[:, 0, :SH_UP].reshape(1, FF2)
    g_fcw = lax.dynamic_slice(red, (25 + 8 * me, 0), (3, SH_UP))
    grads = {"norm_mix": g_norm_mix, "pool_scale": g_pool_scale, "norm_ffn": g_norm_ffn, "norm_final": g_norm_final,
             "ffn_conv_b": g_fcb, "conv_w": g_conv_w.reshape(1, 3, 128), "ffn_conv_w": g_fcw.reshape(1, 3, SH_UP)}
    small_wmv = {"norm_mix": (norm_mix, m_norm_mix, v_norm_mix), "pool_scale": (pool_scale, m_pool_scale, v_pool_scale),
                 "norm_ffn": (norm_ffn, m_norm_ffn, v_norm_ffn), "norm_final": (norm_final, m_norm_final, v_norm_final),
                 "ffn_conv_b": (ffn_conv_b, m_ffn_conv_b, v_ffn_conv_b), "conv_w": (conv_w, m_conv_w, v_conv_w),
                 "ffn_conv_w": (ffn_conv_w, m_ffn_conv_w, v_ffn_conv_w)}
    small_names = list(small_wmv)
    flat2 = lambda a: a.reshape(1, -1) if a.ndim == 1 else a
    small_out = adamw_small([(flat2(small_wmv[nm][0]), flat2(grads[nm]), flat2(small_wmv[nm][1]),
                              flat2(small_wmv[nm][2])) for nm in small_names])
    small = {nm: [o.reshape(small_wmv[nm][0].shape) for o in outs] for nm, outs in zip(small_names, small_out)}

    order = ["norm_mix", "w_in", "pool_w", "pool_scale", "w_pool_proj", "conv_w", "w_conv_out", "w_o", "norm_ffn",
             "w_up", "ffn_conv_w", "ffn_conv_b", "w_down", "norm_final"]
    out = [loss, grad_x.reshape(nb, seq, D)]
    out += [big[nm][0] if nm in big else grads[nm] for nm in order]
    for idx in range(3):
        out += [big[nm][idx + 1] if nm in big else small[nm][idx] for nm in order]
    return tuple(out)
```

```python
import functools

import jax
import jax.numpy as jnp
from jax import lax
from jax.experimental import pallas as pl
from jax.experimental.pallas import tpu as pltpu
from jax.experimental.pallas import tpu_sc as plsc

F32 = jnp.float32
BF16 = jnp.bfloat16

NDEV = 8
D = 1024
NG = 4
CG = 256
WINS = (2, 4, 8, 16)
DIN = 6 * D
SH_IN = DIN // NDEV
NZT = DIN // CG
FF2 = 5632
SH_UP = FF2 // NDEV
FF = FF2 // 2
NCH = 4
SH_DN = FF // NDEV
RMS_EPS = 1e-6
HALO = 16

ADAM_LR = 0.001
ADAM_B1 = 0.9
ADAM_B2 = 0.999
ADAM_EPS = 1e-08
ADAM_WD = 0.01
ADAM_STEP = 10

TM_IN = 512
TM_BWD_IN = 256
TM_MIX = 256
TM_FFN = 256
FFN_CHUNKS_PER_STEP = 4
TK_WGRAD = 2048
MIX_POOL_PROJ, MIX_CONV_OUT, MIX_O = 0, 1, 2
MIX_COLS = 3 * D + CG
VMEM_BIG = 56 * 1024 * 1024
MESH = pl.DeviceIdType.MESH
ANY = pl.BlockSpec(memory_space=pl.ANY)


def _cparams(n_axes, vmem=None):
    return pltpu.CompilerParams(dimension_semantics=("arbitrary",) * n_axes, vmem_limit_bytes=vmem)


def _dot(a, b):
    return jnp.dot(a, b, preferred_element_type=F32)


def _dot_nt(a, b):
    return lax.dot_general(a, b, (((1,), (1,)), ((), ())), preferred_element_type=F32)


def _dot_tn(a, b):
    return lax.dot_general(a, b, (((0,), (0,)), ((), ())), preferred_element_type=F32)


def _shift_down(ext, s, lead):
    return pltpu.roll(ext, s, 0)[lead:]


def _shift_up(ext, s, tm):
    n = ext.shape[0]
    return pltpu.roll(ext, n - s, 0)[:tm]


def _rms_inv(x):
    return lax.rsqrt(jnp.mean(x * x, axis=-1, keepdims=True) + RMS_EPS)


def _rms_bwd(dh, xn, inv, g):
    dxn = dh * g
    return inv * (dxn - xn * jnp.mean(dxn * xn, axis=-1, keepdims=True))


def _pos():
    return lax.axis_index("x"), lax.axis_index("y"), lax.axis_index("c")


def _handshake(peers):
    barrier = pltpu.get_barrier_semaphore()
    for peer in peers:
        pl.semaphore_signal(barrier, inc=1, device_id=peer, device_id_type=MESH)
    pl.semaphore_wait(barrier, len(peers))


def _sequencer(body, out_type, n_sems, name, collective_id):
    return pl.kernel(
        body, out_type=out_type, mesh=plsc.ScalarSubcoreMesh(axis_name="sequencer", num_cores=1), name=name,
        scratch_types=[pltpu.SemaphoreType.DMA((n_sems,)), pltpu.SemaphoreType.DMA((n_sems,))],
        compiler_params=pltpu.CompilerParams(collective_id=collective_id))


def all_gather_blocks(shards, name, collective_id):
    n = len(shards)

    def body(*refs):
        ins, outs = refs[:n], refs[n:2 * n]
        send_sems, recv_sems = refs[2 * n:]
        x, y, c = _pos()
        sibling = (x, y, 1 - c)
        chips = [(1 - x, y), (x, 1 - y), (1 - x, 1 - y)]
        _handshake([sibling] + [(*chip, c) for chip in chips])

        def copy(w, k, block, to, src=None):
            slot = outs[w].at[4 * block[0] + 2 * block[1] + block[2]]
            return pltpu.make_async_remote_copy(
                src_ref=slot if src is None else src, dst_ref=slot,
                send_sem=send_sems.at[8 * w + k], recv_sem=recv_sems.at[8 * w + k], device_id=to, device_id_type=MESH)

        mine, first, passed = [], [], []
        for w in range(n):
            m = pltpu.make_async_copy(ins[w], outs[w].at[4 * x + 2 * y + c], send_sems.at[8 * w + 7])
            m.start()
            mine.append(m)
            first.append(copy(w, 0, (x, y, c), sibling, src=ins[w]))
            first += [copy(w, 1 + j, (x, y, c), (*chip, c), src=ins[w]) for j, chip in enumerate(chips)]
        for cp in first:
            cp.start()
        for w in range(n):
            for j, chip in enumerate(chips):
                copy(w, 1 + j, (*chip, c), (x, y, c)).wait_recv()
                fw = copy(w, 4 + j, (*chip, c), sibling)
                fw.start()
                passed.append(fw)
        for w in range(n):
            copy(w, 0, (x, y, 1 - c), (x, y, c)).wait_recv()
            for j, chip in enumerate(chips):
                copy(w, 4 + j, (*chip, 1 - c), (x, y, c)).wait_recv()
        for cp in first + passed:
            cp.wait_send()
        for m in mine:
            m.wait()

    out = [jax.ShapeDtypeStruct((NDEV,) + s.shape, s.dtype) for s in shards]
    return _sequencer(body, out, 8 * n, name, collective_id)(*shards)


def _exchange_small(v, reduce, name, gathered=None):
    rows = v.shape[0]

    def body(*refs):
        if gathered is None:
            v_ref, out_ref, slots, send_sems, recv_sems, local_sem = refs
        else:
            v_ref, g_ref, out_ref, gsum_ref, slots, send_sems, recv_sems, local_sem = refs
        x, y, c = _pos()
        me = 4 * x + 2 * y + c
        mine = pltpu.make_async_copy(v_ref, slots.at[me], local_sem)
        mine.start()
        offs = [(dx, dy, dc) for dx in (0, 1) for dy in (0, 1) for dc in (0, 1)][1:]

        def copy(k, src_slot, to):
            return pltpu.make_async_remote_copy(
                src_ref=v_ref, dst_ref=slots.at[src_slot], send_sem=send_sems.at[k], recv_sem=recv_sems.at[k],
                device_id=to, device_id_type=MESH)

        sends = []
        for k, (dx, dy, dc) in enumerate(offs):
            cp = copy(k, me, (x ^ dx, y ^ dy, c ^ dc))
            cp.start()
            sends.append(cp)
        for k, (dx, dy, dc) in enumerate(offs):
            copy(k, 4 * (x ^ dx) + 2 * (y ^ dy) + (c ^ dc), (x, y, c)).wait_recv()
        for cp in sends:
            cp.wait_send()
        mine.wait()
        if reduce:
            acc = slots[0]
            for d in range(1, NDEV):
                acc = acc + slots[d]
            out_ref[...] = acc
        else:
            out_ref[...] = slots[...]
        if gathered is not None:
            acc = g_ref[0]
            for d in range(1, NDEV):
                acc = acc + g_ref[d]
            gsum_ref[...] = acc

    vmem = pl.BlockSpec(memory_space=pltpu.VMEM)
    out = jax.ShapeDtypeStruct((rows, D) if reduce else (NDEV, rows, D), F32)
    args, out_shape, out_specs = [v], out, vmem
    if gathered is not None:
        args.append(gathered)
        out_shape, out_specs = [out, jax.ShapeDtypeStruct(gathered.shape[1:], F32)], [vmem, vmem]
    return pl.pallas_call(
        body, name=name, out_shape=out_shape, in_specs=[vmem] * len(args), out_specs=out_specs,
        scratch_shapes=[pltpu.VMEM((NDEV, rows, D), F32), pltpu.SemaphoreType.DMA((7,)),
                        pltpu.SemaphoreType.DMA((7,)), pltpu.SemaphoreType.DMA],
    )(*args)


def reduce_scatter_d2d(grads, name, collective_id):
    n = len(grads)

    def body(*refs):
        ins, outs = refs[:n], refs[n:2 * n]
        send_sems, recv_sems = refs[2 * n:]
        x, y, c = _pos()
        _handshake([(x, y, 1 - c)])
        cps = []
        for w in range(n):
            cp = pltpu.make_async_remote_copy(
                src_ref=ins[w].at[1 - c], dst_ref=outs[w], send_sem=send_sems.at[w], recv_sem=recv_sems.at[w],
                device_id=(x, y, 1 - c), device_id_type=MESH)
            cp.start()
            cps.append(cp)
        for cp in cps:
            cp.wait_recv()
        for cp in cps:
            cp.wait_send()

    out = [jax.ShapeDtypeStruct(g.shape[1:], F32) for g in grads]
    return _sequencer(body, out, n, name, collective_id)(*grads)


def reduce_scatter_ici(parts, name, collective_id):
    n = len(parts)

    def body(*refs):
        ins, outs = refs[:n], refs[n:2 * n]
        send_sems, recv_sems = refs[2 * n:]
        x, y, c = _pos()
        offs = [(1, 0), (0, 1), (1, 1)]
        _handshake([(x ^ dx, y ^ dy, c) for dx, dy in offs])
        cps = []
        for w in range(n):
            for k, (dx, dy) in enumerate(offs):
                ox, oy = x ^ dx, y ^ dy
                cp = pltpu.make_async_remote_copy(
                    src_ref=ins[w].at[2 * ox + oy], dst_ref=outs[w].at[2 * x + y],
                    send_sem=send_sems.at[3 * w + k], recv_sem=recv_sems.at[3 * w + k],
                    device_id=(ox, oy, c), device_id_type=MESH)
                cp.start()
                cps.append((cp, w, k, ox, oy))
        for cp, w, k, ox, oy in cps:
            pltpu.make_async_remote_copy(
                src_ref=ins[w].at[2 * ox + oy], dst_ref=outs[w].at[2 * ox + oy],
                send_sem=send_sems.at[3 * w + k], recv_sem=recv_sems.at[3 * w + k],
                device_id=(ox, oy, c), device_id_type=MESH).wait_recv()
        for cp, *_ in cps:
            cp.wait_send()

    out = [jax.ShapeDtypeStruct(p.shape, BF16) for p in parts]
    return _sequencer(body, out, 3 * n, name, collective_id)(*parts)


def fwd_in(x, g1, w_in_g, tm):
    t = x.shape[0]

    def body(x_ref, g_ref, w_ref, z_ref, ht_ref):
        xf = x_ref[...]
        h = (xf * _rms_inv(xf) * g_ref[...]).astype(BF16)
        ht_ref[...] = h.T
        for j in range(NDEV):
            r = _dot(h, w_ref[j])
            for q in range(3):
                z_ref[3 * j + q] = r[:, q * CG:(q + 1) * CG].astype(BF16)

    return pl.pallas_call(
        body, name="fwd_in", grid=(t // tm,),
        in_specs=[pl.BlockSpec((tm, D), lambda i: (i, 0)), pl.BlockSpec((1, D), lambda i: (0, 0)),
                  pl.BlockSpec((NDEV, D, SH_IN), lambda i: (0, 0, 0))],
        out_specs=[pl.BlockSpec((NZT, tm, CG), lambda i: (0, i, 0)), pl.BlockSpec((D, tm), lambda i: (0, i))],
        out_shape=[jax.ShapeDtypeStruct((NZT, t, CG), BF16), jax.ShapeDtypeStruct((D, t), BF16)],
        compiler_params=_cparams(1, VMEM_BIG),
        cost_estimate=pl.CostEstimate(flops=2 * t * D * DIN, transcendentals=t,
                                      bytes_accessed=4 * t * D + 2 * D * DIN + 2 * t * DIN + 2 * t * D),
    )(x, g1, w_in_g)


def _pool_tile(z_ref, zh_ref, g, win, keep_hist, cnt):
    zt = z_ref[g].astype(F32)
    ext = jnp.concatenate([zh_ref[g].astype(F32) * keep_hist, zt], axis=0)
    s, sh = ext, 1
    while sh < win:
        s = s + pltpu.roll(s, sh, 0)
        sh *= 2
    return s[HALO:] / cnt - zt


def _conv_taps(ext, cur, w_ref, lanes, lead):
    x1 = _shift_down(ext, 1, lead)
    x2 = _shift_down(ext, 2, lead)
    out = w_ref[2:3, lanes] * cur + w_ref[1:2, lanes] * x1 + w_ref[0:1, lanes] * x2
    return out, x1, x2


def fwd_mix(z, x, pool_w, pool_scale, conv_w, wmix, tm, seq):
    t = x.shape[0]
    tps = seq // tm
    hb = tm // HALO

    def body(z_ref, zph_ref, zcvh_ref, x_ref, pw_ref, ps_ref, wpp_ref, cw_ref, wco_ref, wo_ref,
             x1_ref, yp_ref, yc_ref):
        it = pl.program_id(0) % tps
        keep_hist = jnp.where(it == 0, 0.0, 1.0)
        pos = it * tm + lax.broadcasted_iota(jnp.int32, (tm, 1), 0)
        p2 = []
        for g, win in enumerate(WINS):
            cnt = jnp.minimum(pos + 1, win).astype(F32)
            p = _pool_tile(z_ref, zph_ref, g, win, keep_hist, cnt)
            lanes = slice(g * CG, (g + 1) * CG)
            p2.append((_dot(p.astype(BF16), pw_ref[g]) * ps_ref[:, lanes]).astype(BF16))
        y_pool = _dot(jnp.concatenate(p2, axis=1), wpp_ref[...])
        u = []
        for q in range(NG):
            lanes = slice(q * CG, (q + 1) * CG)
            cv = z_ref[8 + q].astype(F32) * z_ref[12 + q].astype(F32)
            cvh = zcvh_ref[q].astype(F32) * zcvh_ref[4 + q].astype(F32) * keep_hist
            cc, _, _ = _conv_taps(jnp.concatenate([cvh, cv], axis=0), cv, cw_ref, lanes, HALO)
            u.append((z_ref[4 + q].astype(F32) * cc).astype(BF16))
        y_conv = _dot(jnp.concatenate(u, axis=1), wco_ref[...])
        ypb, ycb = y_pool.astype(BF16), y_conv.astype(BF16)
        yp_ref[...] = ypb
        yc_ref[...] = ycb
        merged = []
        for q in range(NG):
            lanes = slice(q * CG, (q + 1) * CG)
            sp = jax.nn.sigmoid(z_ref[16 + q].astype(F32))
            sc = jax.nn.sigmoid(z_ref[20 + q].astype(F32))
            merged.append((sp * ypb[:, lanes].astype(F32) + sc * ycb[:, lanes].astype(F32)).astype(BF16))
        x1_ref[...] = x_ref[...] + _dot(jnp.concatenate(merged, axis=1), wo_ref[...])

    def hist(i):
        return jnp.maximum(i * hb - 1, 0)

    const2 = lambda i: (0, 0)
    return pl.pallas_call(
        body, name="fwd_mix", grid=(t // tm,),
        in_specs=[pl.BlockSpec((NZT, tm, CG), lambda i: (0, i, 0)),
                  pl.BlockSpec((NG, HALO, CG), lambda i: (0, hist(i), 0)),
                  pl.BlockSpec((2 * NG, HALO, CG), lambda i: (1, hist(i), 0)),
                  pl.BlockSpec((tm, D), lambda i: (i, 0)),
                  pl.BlockSpec((NG, CG, CG), lambda i: (0, 0, 0)), pl.BlockSpec((1, D), const2),
                  pl.BlockSpec((D, D), lambda i: (0, MIX_POOL_PROJ)), pl.BlockSpec((3, D), const2),
                  pl.BlockSpec((D, D), lambda i: (0, MIX_CONV_OUT)), pl.BlockSpec((D, D), lambda i: (0, MIX_O))],
        out_specs=[pl.BlockSpec((tm, D), lambda i: (i, 0))] * 3,
        out_shape=[jax.ShapeDtypeStruct((t, D), F32), jax.ShapeDtypeStruct((t, D), BF16),
                   jax.ShapeDtypeStruct((t, D), BF16)],
        compiler_params=_cparams(1, VMEM_BIG),
    )(z, z, z, x, pool_w, pool_scale, wmix, conv_w, wmix, wmix)


def fwd_up(x1, g2, w_up_g, fcw, fcb, tm, seq, cps):
    t = x1.shape[0]
    tps = seq // tm

    def body(x1_ref, g2_ref, wup_ref, fcw_ref, fcb_ref,
             up_ref, pre_ref, act_ref, actt_ref, h2t_ref, hist_ref, h2_ref):
        i, k = pl.program_id(0), pl.program_id(1)
        keep_hist = jnp.where(i % tps == 0, 0.0, 1.0)

        @pl.when((i == 0) & (k == 0))
        def _():
            hist_ref[...] = jnp.zeros_like(hist_ref)

        @pl.when(k == 0)
        def _():
            x1v = x1_ref[...]
            h2 = (x1v * _rms_inv(x1v) * g2_ref[...]).astype(BF16)
            h2_ref[...] = h2
            h2t_ref[...] = h2.T

        h2 = h2_ref[...]
        lanes = slice(0, SH_UP)
        for c in range(cps):
            kc = k * cps + c
            conv = []
            for s in range(2):
                ub = _dot(h2, wup_ref[s, c]).astype(BF16)
                up_ref[s, c] = ub
                uf = ub.astype(F32)
                ext = jnp.concatenate([hist_ref[s, kc] * keep_hist, uf], axis=0)
                hist_ref[s, kc] = uf[tm - 8:]
                cc, _, _ = _conv_taps(ext, uf, fcw_ref.at[s, c], lanes, 8)
                conv.append(cc + fcb_ref[s, c])
                pre_ref[s, c] = conv[s].astype(BF16)
            a = (conv[0] * jax.nn.sigmoid(conv[0]) * conv[1]).astype(BF16)
            act_ref[c] = a
            actt_ref[c] = a.T

    tile = lambda i, k: (i, 0)
    const2 = lambda i, k: (0, 0)
    pair = lambda i, k: (0, k, 0, 0)
    chunk = pl.BlockSpec((2, cps, tm, SH_UP), lambda i, k: (0, k, i, 0))
    return pl.pallas_call(
        body, name="fwd_up", grid=(t // tm, NCH // cps),
        in_specs=[pl.BlockSpec((tm, D), tile), pl.BlockSpec((1, D), const2),
                  pl.BlockSpec((2, cps, D, SH_UP), pair), pl.BlockSpec((2, cps, 3, SH_UP), pair),
                  pl.BlockSpec((2, cps, 1, SH_UP), pair)],
        out_specs=[chunk, chunk, pl.BlockSpec((cps, tm, SH_UP), lambda i, k: (k, i, 0)),
                   pl.BlockSpec((cps, SH_UP, tm), lambda i, k: (k, 0, i)), pl.BlockSpec((D, tm), lambda i, k: (0, i))],
        out_shape=[jax.ShapeDtypeStruct((2, NCH, t, SH_UP), BF16), jax.ShapeDtypeStruct((2, NCH, t, SH_UP), BF16),
                   jax.ShapeDtypeStruct((NCH, t, SH_UP), BF16), jax.ShapeDtypeStruct((NCH, SH_UP, t), BF16),
                   jax.ShapeDtypeStruct((D, t), BF16)],
        scratch_shapes=[pltpu.VMEM((2, NCH, 8, SH_UP), F32), pltpu.VMEM((tm, D), BF16)],
        compiler_params=_cparams(2, VMEM_BIG),
    )(x1, g2, w_up_g.reshape(2, NCH, D, SH_UP), fcw.reshape(2, NCH, 3, SH_UP), fcb.reshape(2, NCH, 1, SH_UP))


def fwd_down(x1, act, w_dn, gf, tgt, tm):
    t = x1.shape[0]

    def body(x1_ref, act_ref, wdn_ref, gf_ref, tgt_ref, dx2_ref, dx2b_ref, vec_ref):
        @pl.when(pl.program_id(0) == 0)
        def _():
            vec_ref[...] = jnp.zeros_like(vec_ref)

        d = None
        for c in range(NCH):
            part = _dot(act_ref[c], wdn_ref[c])
            d = part if d is None else d + part
        x2 = x1_ref[...] + d
        inv3 = _rms_inv(x2)
        xn = x2 * inv3
        diff = xn * gf_ref[...] - tgt_ref[...]
        dy = diff * (1.0 / D)
        vec_ref[0:1, :] += jnp.sum(dy * xn, axis=0, keepdims=True)
        vec_ref[1:2, :] += 0.5 * jnp.sum(jnp.mean(diff * diff, axis=-1))
        dx2 = _rms_bwd(dy, xn, inv3, gf_ref[...])
        dx2_ref[...] = dx2
        dx2b_ref[...] = dx2.astype(BF16)

    tile = lambda i: (i, 0)
    const2 = lambda i: (0, 0)
    return pl.pallas_call(
        body, name="fwd_down", grid=(t // tm,),
        in_specs=[pl.BlockSpec((tm, D), tile), pl.BlockSpec((NCH, tm, SH_UP), lambda i: (0, i, 0)),
                  pl.BlockSpec((NCH, SH_UP, D), lambda i: (0, 0, 0)), pl.BlockSpec((1, D), const2),
                  pl.BlockSpec((tm, D), tile)],
        out_specs=[pl.BlockSpec((tm, D), tile), pl.BlockSpec((tm, D), tile), pl.BlockSpec((8, D), const2)],
        out_shape=[jax.ShapeDtypeStruct((t, D), F32), jax.ShapeDtypeStruct((t, D), BF16),
                   jax.ShapeDtypeStruct((8, D), F32)],
        compiler_params=_cparams(1, VMEM_BIG),
    )(x1, act, w_dn, gf, tgt)


def bwd_ffn(dx2, x1, g2, up, pre, w_up_g, fcw, w_dn, tm, seq, cps):
    t = x1.shape[0]
    nt = t // tm
    tps = seq // tm

    def body(dx2_ref, x1_ref, g2_ref, up_ref, pre_ref, wup_ref, fcw_ref, wdn_ref,
             dup_ref, dx1_ref, gvec_ref, gn_ref, carry_ref, dh2_ref, acc_ref):
        i, k = pl.program_id(0), pl.program_id(1)
        it = (nt - 1 - i) % tps
        keep_next = jnp.where(it == tps - 1, 0.0, 1.0)

        @pl.when((i == 0) & (k == 0))
        def _():
            acc_ref[...] = jnp.zeros_like(acc_ref)
            gn_ref[...] = jnp.zeros_like(gn_ref)
            carry_ref[...] = jnp.zeros_like(carry_ref)

        @pl.when(k == 0)
        def _():
            dh2_ref[...] = jnp.zeros_like(dh2_ref)

        dxb = dx2_ref[...].astype(BF16)
        lanes = slice(0, SH_UP)
        dh2 = dh2_ref[...]
        for c in range(cps):
            kc = k * cps + c
            pre = [pre_ref[s, c].astype(F32) for s in range(2)]
            sg = jax.nn.sigmoid(pre[0])
            dact = _dot_nt(dxb, wdn_ref[c])
            dpre = [dact * pre[1] * (sg * (1.0 + pre[0] * (1.0 - sg))), dact * (pre[0] * sg)]
            for s in range(2):
                dc = dpre[s]
                ext = jnp.concatenate([dc, carry_ref[s, kc] * keep_next], axis=0)
                carry_ref[s, kc] = dc[:8]
                shifted = (_shift_up(ext, 2, tm), _shift_up(ext, 1, tm), dc)
                uf = up_ref[s, c].astype(F32)
                acc_ref[s, kc, 0:1, lanes] += jnp.sum(dc, axis=0, keepdims=True)
                for tap in range(3):
                    acc_ref[s, kc, tap + 1:tap + 2, lanes] += jnp.sum(shifted[tap] * uf, axis=0, keepdims=True)
                w = fcw_ref.at[s, c]
                du = w[2:3, :] * dc + w[1:2, :] * shifted[1] + w[0:1, :] * shifted[0]
                dub = du.astype(BF16)
                dup_ref[s, c] = dub
                dh2 = dh2 + _dot_nt(dub, wup_ref[s, c])
        dh2_ref[...] = dh2

        @pl.when(k == NCH // cps - 1)
        def _():
            x1v = x1_ref[...]
            inv2 = _rms_inv(x1v)
            xn = x1v * inv2
            gn_ref[0:1, :] += jnp.sum(dh2 * xn, axis=0, keepdims=True)
            dx1_ref[...] = dx2_ref[...] + _rms_bwd(dh2, xn, inv2, g2_ref[...])

        @pl.when((i == nt - 1) & (k == NCH // cps - 1))
        def _():
            gvec_ref[...] = acc_ref[...]

    rev = lambda i, k: (nt - 1 - i, 0)
    const2 = lambda i, k: (0, 0)
    pair = lambda i, k: (0, k, 0, 0)
    return pl.pallas_call(
        body, name="bwd_ffn", grid=(nt, NCH // cps),
        in_specs=[pl.BlockSpec((tm, D), rev), pl.BlockSpec((tm, D), rev), pl.BlockSpec((1, D), const2),
                  pl.BlockSpec((2, cps, tm, SH_UP), lambda i, k: (0, k, nt - 1 - i, 0)),
                  pl.BlockSpec((2, cps, tm, SH_UP), lambda i, k: (0, k, nt - 1 - i, 0)),
                  pl.BlockSpec((2, cps, D, SH_UP), pair), pl.BlockSpec((2, cps, 3, SH_UP), pair),
                  pl.BlockSpec((cps, SH_UP, D), lambda i, k: (k, 0, 0))],
        out_specs=[pl.BlockSpec((2, cps, tm, SH_UP), lambda i, k: (0, k, nt - 1 - i, 0)), pl.BlockSpec((tm, D), rev),
                   pl.BlockSpec((2, NCH, 8, D), lambda i, k: (0, 0, 0, 0)), pl.BlockSpec((8, D), const2)],
        out_shape=[jax.ShapeDtypeStruct((2, NCH, t, SH_UP), BF16), jax.ShapeDtypeStruct((t, D), F32),
                   jax.ShapeDtypeStruct((2, NCH, 8, D), F32), jax.ShapeDtypeStruct((8, D), F32)],
        scratch_shapes=[pltpu.VMEM((2, NCH, 8, SH_UP), F32), pltpu.VMEM((tm, D), F32),
                        pltpu.VMEM((2, NCH, 8, D), F32)],
        compiler_params=_cparams(2, VMEM_BIG),
    )(dx2, x1, g2, up, pre, w_up_g.reshape(2, NCH, D, SH_UP), fcw.reshape(2, NCH, 3, SH_UP), w_dn)


def bwd_mix(dx1, z, y_pool, y_conv, pool_w, pool_scale, conv_w, wmix, tm, seq):
    t = dx1.shape[0]
    nt = t // tm
    tps = seq // tm
    hb = tm // HALO

    def body(da_ref, z_ref, zph_ref, zcvh_ref, yp_ref, yc_ref, pw_ref, ps_ref, wpp_ref, cw_ref, wco_ref, wo_ref,
             dz_ref, mg_ref, p2_ref, u_ref, dyp_ref, dyc_ref, p_ref, dpw_ref, gvec_ref, cp_ref, cc_ref):
        i = pl.program_id(0)
        it = (nt - 1 - i) % tps
        keep_hist = jnp.where(it == 0, 0.0, 1.0)
        keep_next = jnp.where(it == tps - 1, 0.0, 1.0)
        pos = it * tm + lax.broadcasted_iota(jnp.int32, (tm, 1), 0)

        @pl.when(i == 0)
        def _():
            gvec_ref[...] = jnp.zeros_like(gvec_ref)
            cp_ref[...] = jnp.zeros_like(cp_ref)
            cc_ref[...] = jnp.zeros_like(cc_ref)

        dm = _dot_nt(da_ref[...].astype(BF16), wo_ref[...])
        merged, dyp, dyc = [], [], []
        for q in range(NG):
            lanes = slice(q * CG, (q + 1) * CG)
            sp = jax.nn.sigmoid(z_ref[16 + q].astype(F32))
            sc = jax.nn.sigmoid(z_ref[20 + q].astype(F32))
            yp = yp_ref[:, lanes].astype(F32)
            yc = yc_ref[:, lanes].astype(F32)
            dmq = dm[:, lanes]
            merged.append((sp * yp + sc * yc).astype(BF16))
            dyp.append((dmq * sp).astype(BF16))
            dyc.append((dmq * sc).astype(BF16))
            dz_ref[16 + q] = (dmq * yp * (sp * (1.0 - sp))).astype(BF16)
            dz_ref[20 + q] = (dmq * yc * (sc * (1.0 - sc))).astype(BF16)
        mg_ref[...] = jnp.concatenate(merged, axis=1)
        dypb = jnp.concatenate(dyp, axis=1)
        dycb = jnp.concatenate(dyc, axis=1)
        dyp_ref[...] = dypb
        dyc_ref[...] = dycb

        dp2 = _dot_nt(dypb, wpp_ref[...])
        p2 = []
        for g, win in enumerate(WINS):
            lanes = slice(g * CG, (g + 1) * CG)
            cnt = jnp.minimum(pos + 1, win).astype(F32)
            p = _pool_tile(z_ref, zph_ref, g, win, keep_hist, cnt)
            pb = p.astype(BF16)
            p_ref[g] = pb
            pw = _dot(pb, pw_ref[g])
            p2.append((pw * ps_ref[:, lanes]).astype(BF16))
            dp2g = dp2[:, lanes]
            gvec_ref[0:1, lanes] += jnp.sum(dp2g * pw, axis=0, keepdims=True)
            dpwb = (dp2g * ps_ref[:, lanes]).astype(BF16)
            dpw_ref[g] = dpwb
            dp = _dot_nt(dpwb, pw_ref[g])
            qv = dp / cnt
            ext = jnp.concatenate([qv, cp_ref[g] * keep_next], axis=0)
            cp_ref[g] = qv[:HALO]
            n = tm + HALO
            s, sh = ext, 1
            while sh < win:
                s = s + pltpu.roll(s, n - sh, 0)
                sh *= 2
            dz_ref[g] = (s[:tm] - dp).astype(BF16)
        p2_ref[...] = jnp.concatenate(p2, axis=1)

        du = _dot_nt(dycb, wco_ref[...])
        u = []
        for q in range(NG):
            lanes = slice(q * CG, (q + 1) * CG)
            zb = z_ref[4 + q].astype(F32)
            zc = z_ref[8 + q].astype(F32)
            zv = z_ref[12 + q].astype(F32)
            cv = zc * zv
            cvh = zcvh_ref[q].astype(F32) * zcvh_ref[4 + q].astype(F32) * keep_hist
            cc, cv1, cv2 = _conv_taps(jnp.concatenate([cvh, cv], axis=0), cv, cw_ref, lanes, HALO)
            u.append((zb * cc).astype(BF16))
            duq = du[:, lanes]
            dz_ref[4 + q] = (duq * cc).astype(BF16)
            dcc = duq * zb
            for tap, src in enumerate((cv2, cv1, cv)):
                gvec_ref[tap + 1:tap + 2, lanes] += jnp.sum(dcc * src, axis=0, keepdims=True)
            ext = jnp.concatenate([dcc, cc_ref[:, lanes] * keep_next], axis=0)
            cc_ref[:, lanes] = dcc[:8]
            dcv = (cw_ref[2:3, lanes] * dcc + cw_ref[1:2, lanes] * _shift_up(ext, 1, tm)
                   + cw_ref[0:1, lanes] * _shift_up(ext, 2, tm))
            dz_ref[8 + q] = (dcv * zv).astype(BF16)
            dz_ref[12 + q] = (dcv * zc).astype(BF16)
        u_ref[...] = jnp.concatenate(u, axis=1)

    def hist(i):
        return jnp.maximum((nt - 1 - i) * hb - 1, 0)

    rev = lambda i: (nt - 1 - i, 0)
    rev3 = lambda i: (0, nt - 1 - i, 0)
    const2 = lambda i: (0, 0)
    tok = jax.ShapeDtypeStruct((t, D), BF16)
    grp = jax.ShapeDtypeStruct((NG, t, CG), BF16)
    return pl.pallas_call(
        body, name="bwd_mix", grid=(nt,),
        in_specs=[pl.BlockSpec((tm, D), rev), pl.BlockSpec((NZT, tm, CG), rev3),
                  pl.BlockSpec((NG, HALO, CG), lambda i: (0, hist(i), 0)),
                  pl.BlockSpec((2 * NG, HALO, CG), lambda i: (1, hist(i), 0)),
                  pl.BlockSpec((tm, D), rev), pl.BlockSpec((tm, D), rev),
                  pl.BlockSpec((NG, CG, CG), lambda i: (0, 0, 0)), pl.BlockSpec((1, D), const2),
                  pl.BlockSpec((D, D), lambda i: (0, MIX_POOL_PROJ)), pl.BlockSpec((3, D), const2),
                  pl.BlockSpec((D, D), lambda i: (0, MIX_CONV_OUT)), pl.BlockSpec((D, D), lambda i: (0, MIX_O))],
        out_specs=[pl.BlockSpec((NZT, tm, CG), rev3)] + [pl.BlockSpec((tm, D), rev)] * 5
                  + [pl.BlockSpec((NG, tm, CG), rev3)] * 2 + [pl.BlockSpec((8, D), const2)],
        out_shape=[jax.ShapeDtypeStruct((NZT, t, CG), BF16), tok, tok, tok, tok, tok, grp, grp,
                   jax.ShapeDtypeStruct((8, D), F32)],
        scratch_shapes=[pltpu.VMEM((NG, HALO, CG), F32), pltpu.VMEM((8, D), F32)],
        compiler_params=_cparams(1, VMEM_BIG),
    )(dx1, z, z, z, y_pool, y_conv, pool_w, pool_scale, wmix, conv_w, wmix, wmix)


def bwd_in(dz, w_in_g, dx1, x, g1, tm):
    t = x.shape[0]

    def body(dz_ref, w_ref, dx1_ref, x_ref, g_ref, gx_ref, gn_ref):
        @pl.when(pl.program_id(0) == 0)
        def _():
            gn_ref[...] = jnp.zeros_like(gn_ref)

        dh = None
        for j in range(NDEV):
            dzc = jnp.concatenate([dz_ref[3 * j + q] for q in range(3)], axis=1)
            part = _dot_nt(dzc, w_ref[j])
            dh = part if dh is None else dh + part
        xv = x_ref[...]
        inv = _rms_inv(xv)
        xn = xv * inv
        gn_ref[0:1, :] += jnp.sum(dh * xn, axis=0, keepdims=True)
        gx_ref[...] = dx1_ref[...] + _rms_bwd(dh, xn, inv, g_ref[...])

    tile = lambda i: (i, 0)
    return pl.pallas_call(
        body, name="bwd_in", grid=(t // tm,),
        in_specs=[pl.BlockSpec((NZT, tm, CG), lambda i: (0, i, 0)),
                  pl.BlockSpec((NDEV, D, SH_IN), lambda i: (0, 0, 0)),
                  pl.BlockSpec((tm, D), tile), pl.BlockSpec((tm, D), tile), pl.BlockSpec((1, D), lambda i: (0, 0))],
        out_specs=[pl.BlockSpec((tm, D), tile), pl.BlockSpec((8, D), lambda i: (0, 0))],
        out_shape=[jax.ShapeDtypeStruct((t, D), F32), jax.ShapeDtypeStruct((8, D), F32)],
        compiler_params=_cparams(1, VMEM_BIG),
    )(dz, w_in_g, dx1, x, g1)


def _slot(j):
    return j % 2, j // 2


def wgrad_cols(at, b, q, name, tk):
    m, t = at.shape
    width = b.shape[3]

    def body(a_ref, b_ref, o_ref):
        @pl.when(pl.program_id(1) == 0)
        def _():
            o_ref[...] = jnp.zeros_like(o_ref)

        o_ref[...] += _dot(a_ref[...], b_ref[...])

    return pl.pallas_call(
        body, name=name, grid=(NDEV, t // tk),
        in_specs=[pl.BlockSpec((m, tk), lambda j, k: (0, k)),
                  pl.BlockSpec((None, None, tk, width), lambda j, k: (j, q, k, 0))],
        out_specs=pl.BlockSpec((None, None, m, width), lambda j, k: (j % 2, j // 2, 0, 0)),
        out_shape=jax.ShapeDtypeStruct((2, 4, m, width), F32),
        compiler_params=_cparams(2, VMEM_BIG),
    )(at, b)


def wgrad_cols_resident(at, b, q, name, tk):
    m, t = at.shape
    width = b.shape[3]

    def body(a_ref, b_ref, o_ref):
        k, j = pl.program_id(0), pl.program_id(1)

        @pl.when((k == 0) & (j == 0))
        def _():
            o_ref[...] = jnp.zeros_like(o_ref)

        o_ref[j % 2, j // 2] += _dot(a_ref[...], b_ref[...])

    return pl.pallas_call(
        body, name=name, grid=(t // tk, NDEV),
        in_specs=[pl.BlockSpec((m, tk), lambda k, j: (0, k)),
                  pl.BlockSpec((None, None, tk, width), lambda k, j: (j, q, k, 0))],
        out_specs=pl.BlockSpec((2, 4, m, width), lambda k, j: (0, 0, 0, 0)),
        out_shape=jax.ShapeDtypeStruct((2, 4, m, width), F32),
        compiler_params=_cparams(2, VMEM_BIG),
    )(at, b)


def wgrad_down(actt, dx2b):
    t = dx2b.shape[0]

    def body(a_ref, b_ref, o_ref):
        r = _dot(a_ref[...], b_ref[...])
        o_ref[0] = r[:SH_DN]
        o_ref[1] = r[SH_DN:]

    return pl.pallas_call(
        body, name="wgrad_down", grid=(NCH,),
        in_specs=[pl.BlockSpec((None, SH_UP, t), lambda k: (k, 0, 0)), pl.BlockSpec((t, D), lambda k: (0, 0))],
        out_specs=pl.BlockSpec((2, None, SH_DN, D), lambda k: (0, k, 0, 0)),
        out_shape=jax.ShapeDtypeStruct((2, 4, SH_DN, D), F32),
        compiler_params=_cparams(1, VMEM_BIG),
    )(actt, dx2b)


def wgrad_square(a, b, name, tk):
    t = a.shape[0]

    def body(a_ref, b_ref, o_ref, acc_ref):
        kt = pl.program_id(0)

        @pl.when(kt == 0)
        def _():
            acc_ref[...] = jnp.zeros_like(acc_ref)

        acc_ref[...] += _dot_tn(a_ref[...], b_ref[...].astype(BF16))

        @pl.when(kt == pl.num_programs(0) - 1)
        def _():
            for j in range(NDEV):
                cc, xy = _slot(j)
                o_ref[cc, xy] = acc_ref[j * 128:(j + 1) * 128]

    return pl.pallas_call(
        body, name=name, grid=(t // tk,),
        in_specs=[pl.BlockSpec((tk, D), lambda k: (k, 0)), pl.BlockSpec((tk, D), lambda k: (k, 0))],
        out_specs=pl.BlockSpec((2, 4, 128, D), lambda k: (0, 0, 0, 0)),
        out_shape=jax.ShapeDtypeStruct((2, 4, 128, D), F32),
        scratch_shapes=[pltpu.VMEM((D, D), F32)],
        compiler_params=_cparams(1, VMEM_BIG),
    )(a, b)


def wgrad_pool(p, dpw, tk):
    t = p.shape[1]

    def body(a_ref, b_ref, o_ref):
        @pl.when(pl.program_id(0) == 0)
        def _():
            o_ref[...] = jnp.zeros_like(o_ref)

        for g in range(NG):
            o_ref[g] += _dot_tn(a_ref[g], b_ref[g])

    return pl.pallas_call(
        body, name="wgrad_pool", grid=(t // tk,),
        in_specs=[pl.BlockSpec((NG, tk, CG), lambda k: (0, k, 0))] * 2,
        out_specs=pl.BlockSpec((NG, CG, CG), lambda k: (0, 0, 0)),
        out_shape=jax.ShapeDtypeStruct((NG, CG, CG), F32),
        compiler_params=_cparams(1, VMEM_BIG),
    )(p, dpw)


def _adamw(w, g, m, v):
    m = ADAM_B1 * m + (1.0 - ADAM_B1) * g
    v = ADAM_B2 * v + (1.0 - ADAM_B2) * (g * g)
    m_hat = m / (1.0 - ADAM_B1 ** ADAM_STEP)
    v_hat = v / (1.0 - ADAM_B2 ** ADAM_STEP)
    delta = -ADAM_LR * (m_hat / (jnp.sqrt(v_hat) + ADAM_EPS) + ADAM_WD * w)
    return delta, m, v


def _row_block(r):
    return 512 if r % 512 == 0 else r


def chip_partial(place, g, from_sibling, name):
    _, _, r, c = g.shape

    def body(place_ref, g_ref, s_ref, o_ref):
        o_ref[...] = (g_ref[...] + s_ref[...]).astype(BF16)

    return pl.pallas_call(
        body, name=name,
        grid_spec=pltpu.PrefetchScalarGridSpec(
            num_scalar_prefetch=1, grid=(3,),
            in_specs=[pl.BlockSpec((None, None, r, c), lambda k, pr: (pr[0], pr[1] ^ (k + 1), 0, 0)),
                      pl.BlockSpec((None, r, c), lambda k, pr: (pr[1] ^ (k + 1), 0, 0))],
            out_specs=pl.BlockSpec((None, r, c), lambda k, pr: (pr[1] ^ (k + 1), 0, 0))),
        out_shape=jax.ShapeDtypeStruct((4, r, c), BF16),
        compiler_params=_cparams(1, VMEM_BIG),
    )(place, g, from_sibling)


def finish_adamw(place, gs, from_sibling, from_chips, w, m, v, name):
    n = len(gs)
    r = gs[0].shape[2]
    widths = [g.shape[3] for g in gs]
    c = sum(widths)
    br = _row_block(r)

    def body(place_ref, *refs):
        g_refs, s_refs, c_refs = refs[:n], refs[n:2 * n], refs[2 * n:5 * n]
        w_ref, m_ref, v_ref, og_ref, od_ref, om_ref, ov_ref = refs[5 * n:]
        cols = []
        for q in range(n):
            grad = g_refs[q][...] + s_refs[q][...]
            for k in range(3):
                grad = grad + c_refs[3 * q + k][...].astype(F32)
            cols.append(grad)
        grad = cols[0] if n == 1 else jnp.concatenate(cols, axis=1)
        og_ref[...] = grad
        od_ref[...], om_ref[...], ov_ref[...] = _adamw(w_ref[...], grad, m_ref[...], v_ref[...])

    def other(k, cq):
        return pl.BlockSpec((None, br, cq), lambda i, pr: (pr[1] ^ k, i, 0))

    row = pl.BlockSpec((br, c), lambda i, pr: (i, 0))
    out = jax.ShapeDtypeStruct((r, c), F32)
    in_specs = [pl.BlockSpec((None, None, br, cq), lambda i, pr: (pr[0], pr[1], i, 0)) for cq in widths]
    in_specs += [pl.BlockSpec((None, br, cq), lambda i, pr: (pr[1], i, 0)) for cq in widths]
    in_specs += [other(k, cq) for cq in widths for k in (1, 2, 3)]
    return pl.pallas_call(
        body, name=name,
        grid_spec=pltpu.PrefetchScalarGridSpec(
            num_scalar_prefetch=1, grid=(r // br,), in_specs=in_specs + [row, row, row], out_specs=[row] * 4),
        out_shape=[out] * 4,
        compiler_params=_cparams(1, VMEM_BIG),
    )(place, *gs, *from_sibling, *[fc for fc in from_chips for _ in range(3)], w, m, v)


def adamw_small(items):
    n = len(items)

    def body(*refs):
        ins, outs = refs[:4 * n], refs[4 * n:]
        for i in range(n):
            w, g, m, v = (r[...] for r in ins[4 * i:4 * i + 4])
            outs[3 * i][...], outs[3 * i + 1][...], outs[3 * i + 2][...] = _adamw(w, g, m, v)

    out = [jax.ShapeDtypeStruct(it[0].shape, F32) for it in items for _ in range(3)]
    res = pl.pallas_call(body, name="adamw_small", out_shape=out)(*[a for it in items for a in it])
    return [res[3 * i:3 * i + 3] for i in range(n)]


def kernel(x, norm_mix, w_in, pool_w, pool_scale, w_pool_proj, conv_w, w_conv_out, w_o, norm_ffn, w_up, ffn_conv_w, ffn_conv_b, w_down, norm_final, loss_target, m_norm_mix, m_w_in, m_pool_w, m_pool_scale, m_w_pool_proj, m_conv_w, m_w_conv_out, m_w_o, m_norm_ffn, m_w_up, m_ffn_conv_w, m_ffn_conv_b, m_w_down, m_norm_final, v_norm_mix, v_w_in, v_pool_w, v_pool_scale, v_w_pool_proj, v_conv_w, v_w_conv_out, v_w_o, v_norm_ffn, v_w_up, v_ffn_conv_w, v_ffn_conv_b, v_w_down, v_norm_final):
    nb, seq, _ = x.shape
    t = nb * seq
    tm_in = min(TM_IN, t)
    tm_mix = min(TM_MIX, seq)
    tm_ffn = min(TM_FFN, seq)
    tk = min(TK_WGRAD, t)
    xt = x.reshape(t, D)
    tgt = loss_target.reshape(t, D)
    xi, yi, ci = _pos()
    me = 4 * xi + 2 * yi + ci
    place = jnp.stack([ci, 2 * xi + yi]).astype(jnp.int32)

    tie = lax.optimization_barrier
    w_in_g, = all_gather_blocks([w_in[0].astype(BF16)], "all_gather_w_in", 0)
    taps = (jnp.pad(conv_w[0], ((0, 5), (0, D - 128))) + jnp.pad(ffn_conv_w[0], ((3, 2), (0, D - SH_UP))))
    taps_g = _exchange_small(taps, False, "all_gather_taps")
    mix_shard = jnp.concatenate(
        [w_pool_proj[0], w_conv_out[0], w_o[0], pool_w[0].reshape(NG * 32, CG)], axis=1).astype(BF16)
    mix_shard, taps_g = tie((mix_shard, taps_g))
    wmix_g, = all_gather_blocks([mix_shard], "all_gather_w_mix", 0)
    ffn_shards, w_in_g = tie(([w_up[0].astype(BF16), w_down[0].astype(BF16)], w_in_g))
    w_up_g, = all_gather_blocks(ffn_shards[:1], "all_gather_w_up", 0)
    w_dn_g, = all_gather_blocks(ffn_shards[1:], "all_gather_w_down", 0)
    w_dn_f = w_dn_g.reshape(NCH, SH_UP, D)
    conv_w_f = taps_g[:, 0:3, :128].transpose(1, 0, 2).reshape(3, D)
    fcw_f = taps_g[:, 3:6, :SH_UP]
    fcb_f = ffn_conv_b.reshape(NDEV, 1, SH_UP)
    gfin = norm_final.reshape(1, D)

    z, h1 = fwd_in(xt, norm_mix, w_in_g, tm_in)
    wmix_g, z = tie((wmix_g, z))
    wmix = wmix_g.reshape(D, MIX_COLS)
    pool_w_f = wmix_g[:, :, 3 * D:].reshape(NDEV, NG, 32, CG).transpose(1, 0, 2, 3).reshape(NG, CG, CG)
    x1, y_pool, y_conv = fwd_mix(z, xt, pool_w_f, pool_scale, conv_w_f, wmix, tm_mix, seq)
    up, pre, act_tok, act, h2 = fwd_up(x1, norm_ffn, w_up_g, fcw_f, fcb_f, tm_ffn, seq, FFN_CHUNKS_PER_STEP)
    dx2, dx2b, ffn_vec = fwd_down(x1, act_tok, w_dn_f, gfin, tgt, min(TM_IN, t))

    def to_sibling(full, tag):
        return reduce_scatter_d2d(full, "reduce_scatter_d2d_" + tag, 1)

    def partials(full, from_sib, names):
        return [chip_partial(place, g, s, "chip_partial_" + nm) for g, s, nm in zip(full, from_sib, names)]

    def to_chips(parts, tag):
        return reduce_scatter_ici(parts, "reduce_scatter_ici_" + tag, 2)

    def finish(nm, gs, from_sib, from_chips, wmv):
        w, m, v = wmv
        rc = (gs[0].shape[2], sum(g.shape[3] for g in gs))
        outs = finish_adamw(place, gs, from_sib, from_chips, w.reshape(rc), m.reshape(rc), v.reshape(rc), "adamw_" + nm)
        return [o.reshape(w.shape) for o in outs]

    def after(x, dep):
        return tie((x, dep))[0]

    big = {}
    d_up, dx1, g_ffn_vec, g_nffn = bwd_ffn(dx2, x1, norm_ffn, up, pre, w_up_g, fcw_f, w_dn_f, tm_ffn, seq,
                                           FFN_CHUNKS_PER_STEP)
    gw_up = wgrad_cols(h2, d_up.reshape(NDEV, 1, t, SH_UP), 0, "wgrad_up", t)
    sib_up = to_sibling([gw_up], "w_up")
    gw_dn = wgrad_down(act, after(dx2b, gw_up))
    sib_dn = to_sibling([after(gw_dn, sib_up)], "w_down")
    dx1, part_up = tie((dx1, partials([gw_up], sib_up, ["w_up"])))
    chips_up = to_chips(part_up, "w_up")
    dz, merged, p2, u, dyp, dyc, p, dpw, g_mix_vec = bwd_mix(
        dx1, z, y_pool, y_conv, pool_w_f, pool_scale, conv_w_f, wmix, tm_mix, seq)
    merged, part_dn = tie((merged, partials([gw_dn], sib_dn, ["w_down"])))
    chips_dn = to_chips(part_dn, "w_down")
    gw_o = wgrad_square(merged, dx1, "wgrad_o", tk)
    gw_pp = wgrad_square(p2, dyp, "wgrad_pool_proj", tk)
    gw_co = wgrad_square(u, dyc, "wgrad_conv_out", tk)
    gw_pool = wgrad_pool(p, dpw, tk).reshape(NG, 4, 2, 32, CG).transpose(2, 1, 0, 3, 4).reshape(2, 4, NG * 32, CG)
    dz8 = dz.reshape(NDEV, 3, t, CG)
    gw_in, sib_in, chips_in = [None] * 3, [None] * 3, [None] * 3
    sib_a = to_sibling(after([gw_o, gw_pp], (chips_up, gw_pool)), "mix_a")
    sib_b = to_sibling(after([gw_co, gw_pool], sib_a), "mix_b")
    gw_in[0] = wgrad_cols_resident(h1, dz8, 0, "wgrad_in_0", t)
    h1, part_a, part_b = tie((h1, partials([gw_o, gw_pp], sib_a, ["w_o", "w_pool_proj"]),
                              partials([gw_co, gw_pool], sib_b, ["w_conv_out", "pool_w"])))
    chips_a = to_chips(after(part_a, chips_dn), "mix_a")
    chips_b = to_chips(part_b, "mix_b")
    sib_in[0] = to_sibling(after([gw_in[0]], sib_b), "w_in_0")
    gw_in[1] = wgrad_cols_resident(h1, dz8, 1, "wgrad_in_1", t)
    h1, part_in0, gw_in[1] = tie((h1, partials([gw_in[0]], sib_in[0], ["w_in_0"]), gw_in[1]))
    chips_in[0] = to_chips(part_in0, "w_in_0")
    sib_in[1] = to_sibling(after([gw_in[1]], sib_in[0]), "w_in_1")
    h1, big["w_down"], big["w_up"] = tie((
        h1, finish("w_down", [gw_dn], sib_dn, chips_dn, (w_down, m_w_down, v_w_down)),
        finish("w_up", [gw_up], sib_up, chips_up, (w_up, m_w_up, v_w_up))))
    gw_in[2] = wgrad_cols_resident(h1, dz8, 2, "wgrad_in_2", t)
    sib_in[2] = to_sibling(after([gw_in[2]], (chips_a, chips_b, chips_in[0])), "w_in_2")
    dx1, part_in1, part_in2, big["w_o"], big["w_pool_proj"], big["w_conv_out"], big["pool_w"] = tie((
        dx1, partials([gw_in[1]], sib_in[1], ["w_in_1"]), partials([gw_in[2]], sib_in[2], ["w_in_2"]),
        finish("w_o", [gw_o], sib_a[:1], chips_a[:1], (w_o, m_w_o, v_w_o)),
        finish("w_pool_proj", [gw_pp], sib_a[1:], chips_a[1:], (w_pool_proj, m_w_pool_proj, v_w_pool_proj)),
        finish("w_conv_out", [gw_co], sib_b[:1], chips_b[:1], (w_conv_out, m_w_conv_out, v_w_conv_out)),
        finish("pool_w", [gw_pool], sib_b[1:], chips_b[1:], (pool_w, m_pool_w, v_pool_w))))
    chips_in[1] = to_chips(after(part_in1, sib_in[2]), "w_in_1")
    chips_in[2] = to_chips(part_in2, "w_in_2")
    small_g, = all_gather_blocks(
        [after(jnp.concatenate([g_mix_vec, g_nffn, ffn_vec, g_ffn_vec.reshape(8 * NDEV, D)], axis=0), sib_in[2])],
        "all_gather_small", 0)
    grad_x, g_nmix = bwd_in(dz, w_in_g, dx1, xt, norm_mix, min(TM_BWD_IN, t))
    grad_x, chips_in = tie((grad_x, chips_in))
    big["w_in"] = finish("w_in", gw_in, [s[0] for s in sib_in], [c[0] for c in chips_in], (w_in, m_w_in, v_w_in))

    red_n, red = _exchange_small(g_nmix, True, "all_reduce_small", gathered=small_g)
    g_norm_mix, g_pool_scale, g_norm_ffn = red_n[0:1], red[0:1], red[8:9]
    g_conv_w = lax.dynamic_slice(red, (1, me * 128), (3, 128))
    g_norm_final = red[16]
    loss = red[17, 0]
    g_fcb = red[24:].reshape(NDEV, 8, D)[:, 0, :SH_UP].reshape(1, FF2)
    g_fcw = lax.dynamic_slice(red, (25 + 8 * me, 0), (3, SH_UP))
    grads = {"norm_mix": g_norm_mix, "pool_scale": g_pool_scale, "norm_ffn": g_norm_ffn, "norm_final": g_norm_final,
             "ffn_conv_b": g_fcb, "conv_w": g_conv_w.reshape(1, 3, 128), "ffn_conv_w": g_fcw.reshape(1, 3, SH_UP)}
    small_wmv = {"norm_mix": (norm_mix, m_norm_mix, v_norm_mix), "pool_scale": (pool_scale, m_pool_scale, v_pool_scale),
                 "norm_ffn": (norm_ffn, m_norm_ffn, v_norm_ffn), "norm_final": (norm_final, m_norm_final, v_norm_final),
                 "ffn_conv_b": (ffn_conv_b, m_ffn_conv_b, v_ffn_conv_b), "conv_w": (conv_w, m_conv_w, v_conv_w),
                 "ffn_conv_w": (ffn_conv_w, m_ffn_conv_w, v_ffn_conv_w)}
    small_names = list(small_wmv)
    flat2 = lambda a: a.reshape(1, -1) if a.ndim == 1 else a
    small_out = adamw_small([(flat2(small_wmv[nm][0]), flat2(grads[nm]), flat2(small_wmv[nm][1]),
                              flat2(small_wmv[nm][2])) for nm in small_names])
    small = {nm: [o.reshape(small_wmv[nm][0].shape) for o in outs] for nm, outs in zip(small_names, small_out)}

    order = ["norm_mix", "w_in", "pool_w", "pool_scale", "w_pool_proj", "conv_w", "w_conv_out", "w_o", "norm_ffn",
             "w_up", "ffn_conv_w", "ffn_conv_b", "w_down", "norm_final"]
    out = [loss, grad_x.reshape(nb, seq, D)]
    out += [big[nm][0] if nm in big else grads[nm] for nm in order]
    for idx in range(3):
        out += [big[nm][idx + 1] if nm in big else small[nm][idx] for nm in order]
    return tuple(out)
```

```python
import jax
import jax.numpy as jnp
from jax import lax
from jax.experimental import pallas as pl
from jax.experimental.pallas import tpu as pltpu
from jax.experimental.pallas import tpu_sc as plsc

F32 = jnp.float32
BF16 = jnp.bfloat16

NDEV = 8
D = 1024
NG = 4
CG = 256
WINS = (2, 4, 8, 16)
DIN = 6 * D
SH_IN = DIN // NDEV
NZT = DIN // CG
FF2 = 5632
SH_UP = FF2 // NDEV
FF = FF2 // 2
NCH = 4
SH_DN = FF // NDEV
RMS_EPS = 1e-6
HALO = 16

ADAM_LR = 0.001
ADAM_B1 = 0.9
ADAM_B2 = 0.999
ADAM_EPS = 1e-08
ADAM_WD = 0.01
ADAM_STEP = 10

TM_IN = 512
TM_BWD_IN = 256
TM_MIX = 256
TM_FFN = 256
TK_WGRAD = 2048
MIX_POOL_PROJ, MIX_CONV_OUT, MIX_O = 0, 1, 2
MIX_COLS = 3 * D + CG
VMEM_BIG = 56 * 1024 * 1024
MESH = pl.DeviceIdType.MESH
ANY = pl.BlockSpec(memory_space=pl.ANY)


def _cparams(n_axes, vmem=None):
    return pltpu.CompilerParams(dimension_semantics=("arbitrary",) * n_axes, vmem_limit_bytes=vmem)


def _dot(a, b):
    return jnp.dot(a, b, preferred_element_type=F32)


def _dot_nt(a, b):
    return lax.dot_general(a, b, (((1,), (1,)), ((), ())), preferred_element_type=F32)


def _dot_tn(a, b):
    return lax.dot_general(a, b, (((0,), (0,)), ((), ())), preferred_element_type=F32)


def _shift_down(ext, s, lead):
    return pltpu.roll(ext, s, 0)[lead:]


def _shift_up(ext, s, tm):
    n = ext.shape[0]
    return pltpu.roll(ext, n - s, 0)[:tm]


def _rms_inv(x):
    return lax.rsqrt(jnp.mean(x * x, axis=-1, keepdims=True) + RMS_EPS)


def _rms_bwd(dh, xn, inv, g):
    dxn = dh * g
    return inv * (dxn - xn * jnp.mean(dxn * xn, axis=-1, keepdims=True))


def _pos():
    return lax.axis_index("x"), lax.axis_index("y"), lax.axis_index("c")


def _handshake(peers):
    barrier = pltpu.get_barrier_semaphore()
    for peer in peers:
        pl.semaphore_signal(barrier, inc=1, device_id=peer, device_id_type=MESH)
    pl.semaphore_wait(barrier, len(peers))


def _sequencer(body, out_type, n_sems, name, collective_id):
    return pl.kernel(
        body, out_type=out_type, mesh=plsc.ScalarSubcoreMesh(axis_name="sequencer", num_cores=1), name=name,
        scratch_types=[pltpu.SemaphoreType.DMA((n_sems,)), pltpu.SemaphoreType.DMA((n_sems,))],
        compiler_params=pltpu.CompilerParams(collective_id=collective_id))


def all_gather_blocks(shards, name, collective_id):
    n = len(shards)

    def body(*refs):
        ins, outs = refs[:n], refs[n:2 * n]
        send_sems, recv_sems = refs[2 * n:]
        x, y, c = _pos()
        me, sibling = (x, y, c), (x, y, 1 - c)
        first_chip, second_chip, diagonal = (x ^ (1 - c), y ^ c), (x ^ c, y ^ (1 - c)), (1 - x, 1 - y)
        first, second = (*first_chip, c), (*second_chip, c)
        _handshake([sibling, first, second])

        def copy(w, k, block, to, src=None):
            slot = outs[w].at[4 * block[0] + 2 * block[1] + block[2]]
            return pltpu.make_async_remote_copy(
                src_ref=slot if src is None else src, dst_ref=slot,
                send_sem=send_sems.at[8 * w + k], recv_sem=recv_sems.at[8 * w + k], device_id=to, device_id_type=MESH)

        mine, sent = [], []
        for w in range(n):
            m = pltpu.make_async_copy(ins[w], outs[w].at[4 * x + 2 * y + c], send_sems.at[8 * w + 7])
            m.start()
            mine.append(m)
            sent += [copy(w, k, me, to, src=ins[w]) for k, to in enumerate((sibling, first, second))]
        for cp in sent:
            cp.start()
        for k, chip in ((1, first_chip), (2, second_chip), (3, diagonal)):
            for w in range(n):
                copy(w, k, (*chip, c), me).wait_recv()
                onward = [copy(w, 3 + k, (*chip, c), sibling)] + ([copy(w, 3, (*chip, c), second)] if k == 1 else [])
                for cp in onward:
                    cp.start()
                sent += onward
        for w in range(n):
            copy(w, 0, sibling, me).wait_recv()
            for k, chip in ((4, second_chip), (5, first_chip), (6, diagonal)):
                copy(w, k, (*chip, 1 - c), me).wait_recv()
        for cp in sent:
            cp.wait_send()
        for m in mine:
            m.wait()

    out = [jax.ShapeDtypeStruct((NDEV,) + s.shape, s.dtype) for s in shards]
    return _sequencer(body, out, 8 * n, name, collective_id)(*shards)


def _exchange_small(v, reduce, name, gathered=None):
    rows = v.shape[0]

    def body(*refs):
        if gathered is None:
            v_ref, out_ref, slots, send_sems, recv_sems, local_sem = refs
        else:
            v_ref, g_ref, out_ref, gsum_ref, slots, send_sems, recv_sems, local_sem = refs
        x, y, c = _pos()
        me = 4 * x + 2 * y + c
        mine = pltpu.make_async_copy(v_ref, slots.at[me], local_sem)
        mine.start()
        offs = [(dx, dy, dc) for dx in (0, 1) for dy in (0, 1) for dc in (0, 1)][1:]

        def copy(k, src_slot, to):
            return pltpu.make_async_remote_copy(
                src_ref=v_ref, dst_ref=slots.at[src_slot], send_sem=send_sems.at[k], recv_sem=recv_sems.at[k],
                device_id=to, device_id_type=MESH)

        sends = []
        for k, (dx, dy, dc) in enumerate(offs):
            cp = copy(k, me, (x ^ dx, y ^ dy, c ^ dc))
            cp.start()
            sends.append(cp)
        for k, (dx, dy, dc) in enumerate(offs):
            copy(k, 4 * (x ^ dx) + 2 * (y ^ dy) + (c ^ dc), (x, y, c)).wait_recv()
        for cp in sends:
            cp.wait_send()
        mine.wait()
        if reduce:
            acc = slots[0]
            for d in range(1, NDEV):
                acc = acc + slots[d]
            out_ref[...] = acc
        else:
            out_ref[...] = slots[...]
        if gathered is not None:
            acc = g_ref[0]
            for d in range(1, NDEV):
                acc = acc + g_ref[d]
            gsum_ref[...] = acc

    vmem = pl.BlockSpec(memory_space=pltpu.VMEM)
    out = jax.ShapeDtypeStruct((rows, D) if reduce else (NDEV, rows, D), F32)
    args, out_shape, out_specs = [v], out, vmem
    if gathered is not None:
        args.append(gathered)
        out_shape, out_specs = [out, jax.ShapeDtypeStruct(gathered.shape[1:], F32)], [vmem, vmem]
    return pl.pallas_call(
        body, name=name, out_shape=out_shape, in_specs=[vmem] * len(args), out_specs=out_specs,
        scratch_shapes=[pltpu.VMEM((NDEV, rows, D), F32), pltpu.SemaphoreType.DMA((7,)),
                        pltpu.SemaphoreType.DMA((7,)), pltpu.SemaphoreType.DMA],
    )(*args)


def reduce_scatter_d2d(grads, name, collective_id):
    n = len(grads)

    def body(*refs):
        ins, outs = refs[:n], refs[n:2 * n]
        send_sems, recv_sems = refs[2 * n:]
        x, y, c = _pos()
        _handshake([(x, y, 1 - c)])
        cps = []
        for w in range(n):
            cp = pltpu.make_async_remote_copy(
                src_ref=ins[w].at[1 - c], dst_ref=outs[w], send_sem=send_sems.at[w], recv_sem=recv_sems.at[w],
                device_id=(x, y, 1 - c), device_id_type=MESH)
            cp.start()
            cps.append(cp)
        for cp in cps:
            cp.wait_recv()
        for cp in cps:
            cp.wait_send()

    out = [jax.ShapeDtypeStruct(g.shape[1:], F32) for g in grads]
    return _sequencer(body, out, n, name, collective_id)(*grads)


def reduce_scatter_ici(parts, name, collective_id):
    n = len(parts)

    def body(*refs):
        ins, outs = refs[:n], refs[n:2 * n]
        send_sems, recv_sems = refs[2 * n:]
        x, y, c = _pos()
        offs = [(1, 0), (0, 1), (1, 1)]
        _handshake([(x ^ dx, y ^ dy, c) for dx, dy in offs])
        cps = []
        for w in range(n):
            for k, (dx, dy) in enumerate(offs):
                ox, oy = x ^ dx, y ^ dy
                cp = pltpu.make_async_remote_copy(
                    src_ref=ins[w].at[2 * ox + oy], dst_ref=outs[w].at[2 * x + y],
                    send_sem=send_sems.at[3 * w + k], recv_sem=recv_sems.at[3 * w + k],
                    device_id=(ox, oy, c), device_id_type=MESH)
                cp.start()
                cps.append((cp, w, k, ox, oy))
        for cp, w, k, ox, oy in cps:
            pltpu.make_async_remote_copy(
                src_ref=ins[w].at[2 * ox + oy], dst_ref=outs[w].at[2 * ox + oy],
                send_sem=send_sems.at[3 * w + k], recv_sem=recv_sems.at[3 * w + k],
                device_id=(ox, oy, c), device_id_type=MESH).wait_recv()
        for cp, *_ in cps:
            cp.wait_send()

    out = [jax.ShapeDtypeStruct(p.shape, BF16) for p in parts]
    return _sequencer(body, out, 3 * n, name, collective_id)(*parts)


def fwd_in(x, g1, w_in_g, tm):
    t = x.shape[0]

    def body(x_ref, g_ref, w_ref, z_ref, ht_ref):
        xf = x_ref[...]
        h = (xf * _rms_inv(xf) * g_ref[...]).astype(BF16)
        ht_ref[...] = h.T
        for j in range(NDEV):
            r = _dot(h, w_ref[j])
            for q in range(3):
                z_ref[3 * j + q] = r[:, q * CG:(q + 1) * CG].astype(BF16)

    return pl.pallas_call(
        body, name="fwd_in", grid=(t // tm,),
        in_specs=[pl.BlockSpec((tm, D), lambda i: (i, 0)), pl.BlockSpec((1, D), lambda i: (0, 0)),
                  pl.BlockSpec((NDEV, D, SH_IN), lambda i: (0, 0, 0))],
        out_specs=[pl.BlockSpec((NZT, tm, CG), lambda i: (0, i, 0)), pl.BlockSpec((D, tm), lambda i: (0, i))],
        out_shape=[jax.ShapeDtypeStruct((NZT, t, CG), BF16), jax.ShapeDtypeStruct((D, t), BF16)],
        compiler_params=_cparams(1, VMEM_BIG),
        cost_estimate=pl.CostEstimate(flops=2 * t * D * DIN, transcendentals=t,
                                      bytes_accessed=4 * t * D + 2 * D * DIN + 2 * t * DIN + 2 * t * D),
    )(x, g1, w_in_g)


def _pool_tile(z_ref, zh_ref, g, win, keep_hist, cnt):
    zt = z_ref[g].astype(F32)
    ext = jnp.concatenate([zh_ref[g].astype(F32) * keep_hist, zt], axis=0)
    s, sh = ext, 1
    while sh < win:
        s = s + pltpu.roll(s, sh, 0)
        sh *= 2
    return s[HALO:] / cnt - zt


def _conv_taps(ext, cur, w_ref, lanes, lead):
    x1 = _shift_down(ext, 1, lead)
    x2 = _shift_down(ext, 2, lead)
    out = w_ref[2:3, lanes] * cur + w_ref[1:2, lanes] * x1 + w_ref[0:1, lanes] * x2
    return out, x1, x2


def fwd_mix(z, x, pool_w, pool_scale, conv_w, wmix, tm, seq):
    t = x.shape[0]
    tps = seq // tm
    hb = tm // HALO

    def body(z_ref, zph_ref, zcvh_ref, x_ref, pw_ref, ps_ref, wpp_ref, cw_ref, wco_ref, wo_ref,
             x1_ref, yp_ref, yc_ref):
        it = pl.program_id(0) % tps
        keep_hist = jnp.where(it == 0, 0.0, 1.0)
        pos = it * tm + lax.broadcasted_iota(jnp.int32, (tm, 1), 0)
        p2 = []
        for g, win in enumerate(WINS):
            cnt = jnp.minimum(pos + 1, win).astype(F32)
            p = _pool_tile(z_ref, zph_ref, g, win, keep_hist, cnt)
            lanes = slice(g * CG, (g + 1) * CG)
            p2.append((_dot(p.astype(BF16), pw_ref[g]) * ps_ref[:, lanes]).astype(BF16))
        y_pool = _dot(jnp.concatenate(p2, axis=1), wpp_ref[...])
        u = []
        for q in range(NG):
            lanes = slice(q * CG, (q + 1) * CG)
            cv = z_ref[8 + q].astype(F32) * z_ref[12 + q].astype(F32)
            cvh = zcvh_ref[q].astype(F32) * zcvh_ref[4 + q].astype(F32) * keep_hist
            cc, _, _ = _conv_taps(jnp.concatenate([cvh, cv], axis=0), cv, cw_ref, lanes, HALO)
            u.append((z_ref[4 + q].astype(F32) * cc).astype(BF16))
        y_conv = _dot(jnp.concatenate(u, axis=1), wco_ref[...])
        ypb, ycb = y_pool.astype(BF16), y_conv.astype(BF16)
        yp_ref[...] = ypb
        yc_ref[...] = ycb
        merged = []
        for q in range(NG):
            lanes = slice(q * CG, (q + 1) * CG)
            sp = jax.nn.sigmoid(z_ref[16 + q].astype(F32))
            sc = jax.nn.sigmoid(z_ref[20 + q].astype(F32))
            merged.append((sp * ypb[:, lanes].astype(F32) + sc * ycb[:, lanes].astype(F32)).astype(BF16))
        x1_ref[...] = x_ref[...] + _dot(jnp.concatenate(merged, axis=1), wo_ref[...])

    def hist(i):
        return jnp.maximum(i * hb - 1, 0)

    const2 = lambda i: (0, 0)
    return pl.pallas_call(
        body, name="fwd_mix", grid=(t // tm,),
        in_specs=[pl.BlockSpec((NZT, tm, CG), lambda i: (0, i, 0)),
                  pl.BlockSpec((NG, HALO, CG), lambda i: (0, hist(i), 0)),
                  pl.BlockSpec((2 * NG, HALO, CG), lambda i: (1, hist(i), 0)),
                  pl.BlockSpec((tm, D), lambda i: (i, 0)),
                  pl.BlockSpec((NG, CG, CG), lambda i: (0, 0, 0)), pl.BlockSpec((1, D), const2),
                  pl.BlockSpec((D, D), lambda i: (0, MIX_POOL_PROJ)), pl.BlockSpec((3, D), const2),
                  pl.BlockSpec((D, D), lambda i: (0, MIX_CONV_OUT)), pl.BlockSpec((D, D), lambda i: (0, MIX_O))],
        out_specs=[pl.BlockSpec((tm, D), lambda i: (i, 0))] * 3,
        out_shape=[jax.ShapeDtypeStruct((t, D), F32), jax.ShapeDtypeStruct((t, D), BF16),
                   jax.ShapeDtypeStruct((t, D), BF16)],
        compiler_params=_cparams(1, VMEM_BIG),
    )(z, z, z, x, pool_w, pool_scale, wmix, conv_w, wmix, wmix)


def fwd_up(x1, g2, w_up_g, fcw, fcb, tm, seq):
    t = x1.shape[0]
    tps = seq // tm

    def body(x1_ref, g2_ref, wup_ref, fcw_ref, fcb_ref, up_ref, pre_ref, act_ref, actt_ref, h2t_ref, hist_ref):
        i = pl.program_id(0)
        keep_hist = jnp.where(i % tps == 0, 0.0, 1.0)

        @pl.when(i == 0)
        def _():
            hist_ref[...] = jnp.zeros_like(hist_ref)

        x1v = x1_ref[...]
        h2 = (x1v * _rms_inv(x1v) * g2_ref[...]).astype(BF16)
        h2t_ref[...] = h2.T
        lanes = slice(0, SH_UP)
        for c in range(NCH):
            conv = []
            for s in range(2):
                ub = _dot(h2, wup_ref[s, c]).astype(BF16)
                up_ref[s, c] = ub
                uf = ub.astype(F32)
                ext = jnp.concatenate([hist_ref[s, c] * keep_hist, uf], axis=0)
                hist_ref[s, c] = uf[tm - 8:]
                cc, _, _ = _conv_taps(ext, uf, fcw_ref.at[s, c], lanes, 8)
                conv.append(cc + fcb_ref[s, c])
                pre_ref[s, c] = conv[s].astype(BF16)
            a = (conv[0] * jax.nn.sigmoid(conv[0]) * conv[1]).astype(BF16)
            act_ref[c] = a
            actt_ref[c] = a.T

    tile = lambda i: (i, 0)
    const2 = lambda i: (0, 0)
    whole = lambda i: (0, 0, 0, 0)
    chunks = pl.BlockSpec((2, NCH, tm, SH_UP), lambda i: (0, 0, i, 0))
    return pl.pallas_call(
        body, name="fwd_up", grid=(t // tm,),
        in_specs=[pl.BlockSpec((tm, D), tile), pl.BlockSpec((1, D), const2),
                  pl.BlockSpec((2, NCH, D, SH_UP), whole), pl.BlockSpec((2, NCH, 3, SH_UP), whole),
                  pl.BlockSpec((2, NCH, 1, SH_UP), whole)],
        out_specs=[chunks, chunks, pl.BlockSpec((NCH, tm, SH_UP), lambda i: (0, i, 0)),
                   pl.BlockSpec((NCH, SH_UP, tm), lambda i: (0, 0, i)), pl.BlockSpec((D, tm), lambda i: (0, i))],
        out_shape=[jax.ShapeDtypeStruct((2, NCH, t, SH_UP), BF16), jax.ShapeDtypeStruct((2, NCH, t, SH_UP), BF16),
                   jax.ShapeDtypeStruct((NCH, t, SH_UP), BF16), jax.ShapeDtypeStruct((NCH, SH_UP, t), BF16),
                   jax.ShapeDtypeStruct((D, t), BF16)],
        scratch_shapes=[pltpu.VMEM((2, NCH, 8, SH_UP), F32)],
        compiler_params=_cparams(1, VMEM_BIG),
    )(x1, g2, w_up_g.reshape(2, NCH, D, SH_UP), fcw.reshape(2, NCH, 3, SH_UP), fcb.reshape(2, NCH, 1, SH_UP))


def fwd_down(x1, act, w_dn, gf, tgt, tm):
    t = x1.shape[0]

    def body(x1_ref, act_ref, wdn_ref, gf_ref, tgt_ref, dx2_ref, dx2b_ref, vec_ref):
        @pl.when(pl.program_id(0) == 0)
        def _():
            vec_ref[...] = jnp.zeros_like(vec_ref)

        d = None
        for c in range(NCH):
            part = _dot(act_ref[c], wdn_ref[c])
            d = part if d is None else d + part
        x2 = x1_ref[...] + d
        inv3 = _rms_inv(x2)
        xn = x2 * inv3
        diff = xn * gf_ref[...] - tgt_ref[...]
        dy = diff * (1.0 / D)
        vec_ref[0:1, :] += jnp.sum(dy * xn, axis=0, keepdims=True)
        vec_ref[1:2, :] += 0.5 * jnp.sum(jnp.mean(diff * diff, axis=-1))
        dx2 = _rms_bwd(dy, xn, inv3, gf_ref[...])
        dx2_ref[...] = dx2
        dx2b_ref[...] = dx2.astype(BF16)

    tile = lambda i: (i, 0)
    const2 = lambda i: (0, 0)
    return pl.pallas_call(
        body, name="fwd_down", grid=(t // tm,),
        in_specs=[pl.BlockSpec((tm, D), tile), pl.BlockSpec((NCH, tm, SH_UP), lambda i: (0, i, 0)),
                  pl.BlockSpec((NCH, SH_UP, D), lambda i: (0, 0, 0)), pl.BlockSpec((1, D), const2),
                  pl.BlockSpec((tm, D), tile)],
        out_specs=[pl.BlockSpec((tm, D), tile), pl.BlockSpec((tm, D), tile), pl.BlockSpec((8, D), const2)],
        out_shape=[jax.ShapeDtypeStruct((t, D), F32), jax.ShapeDtypeStruct((t, D), BF16),
                   jax.ShapeDtypeStruct((8, D), F32)],
        compiler_params=_cparams(1, VMEM_BIG),
    )(x1, act, w_dn, gf, tgt)


def bwd_ffn(dx2, x1, g2, up, pre, w_up_g, fcw, w_dn, tm, seq):
    t = x1.shape[0]
    nt = t // tm
    tps = seq // tm

    def body(dx2_ref, x1_ref, g2_ref, up_ref, pre_ref, wup_ref, fcw_ref, wdn_ref,
             dup_ref, dx1_ref, gvec_ref, gn_ref, carry_ref):
        i = pl.program_id(0)
        it = (nt - 1 - i) % tps
        keep_next = jnp.where(it == tps - 1, 0.0, 1.0)

        @pl.when(i == 0)
        def _():
            gvec_ref[...] = jnp.zeros_like(gvec_ref)
            gn_ref[...] = jnp.zeros_like(gn_ref)
            carry_ref[...] = jnp.zeros_like(carry_ref)

        dx2v = dx2_ref[...]
        dxb = dx2v.astype(BF16)
        lanes = slice(0, SH_UP)
        dh2 = None
        for c in range(NCH):
            pre = [pre_ref[s, c].astype(F32) for s in range(2)]
            sg = jax.nn.sigmoid(pre[0])
            dact = _dot_nt(dxb, wdn_ref[c])
            dpre = [dact * pre[1] * (sg * (1.0 + pre[0] * (1.0 - sg))), dact * (pre[0] * sg)]
            for s in range(2):
                dc = dpre[s]
                ext = jnp.concatenate([dc, carry_ref[s, c] * keep_next], axis=0)
                carry_ref[s, c] = dc[:8]
                shifted = (_shift_up(ext, 2, tm), _shift_up(ext, 1, tm), dc)
                uf = up_ref[s, c].astype(F32)
                gvec_ref[s, c, 0:1, lanes] += jnp.sum(dc, axis=0, keepdims=True)
                for tap in range(3):
                    gvec_ref[s, c, tap + 1:tap + 2, lanes] += jnp.sum(shifted[tap] * uf, axis=0, keepdims=True)
                w = fcw_ref.at[s, c]
                du = w[2:3, :] * dc + w[1:2, :] * shifted[1] + w[0:1, :] * shifted[0]
                dub = du.astype(BF16)
                dup_ref[s, c] = dub
                part = _dot_nt(dub, wup_ref[s, c])
                dh2 = part if dh2 is None else dh2 + part
        x1v = x1_ref[...]
        inv2 = _rms_inv(x1v)
        xn = x1v * inv2
        gn_ref[0:1, :] += jnp.sum(dh2 * xn, axis=0, keepdims=True)
        dx1_ref[...] = dx2v + _rms_bwd(dh2, xn, inv2, g2_ref[...])

    rev = lambda i: (nt - 1 - i, 0)
    const2 = lambda i: (0, 0)
    whole = lambda i: (0, 0, 0, 0)
    chunks = pl.BlockSpec((2, NCH, tm, SH_UP), lambda i: (0, 0, nt - 1 - i, 0))
    return pl.pallas_call(
        body, name="bwd_ffn", grid=(nt,),
        in_specs=[pl.BlockSpec((tm, D), rev), pl.BlockSpec((tm, D), rev), pl.BlockSpec((1, D), const2),
                  chunks, chunks, pl.BlockSpec((2, NCH, D, SH_UP), whole), pl.BlockSpec((2, NCH, 3, SH_UP), whole),
                  pl.BlockSpec((NCH, SH_UP, D), lambda i: (0, 0, 0))],
        out_specs=[chunks, pl.BlockSpec((tm, D), rev), pl.BlockSpec((2, NCH, 8, D), whole),
                   pl.BlockSpec((8, D), const2)],
        out_shape=[jax.ShapeDtypeStruct((2, NCH, t, SH_UP), BF16), jax.ShapeDtypeStruct((t, D), F32),
                   jax.ShapeDtypeStruct((2, NCH, 8, D), F32), jax.ShapeDtypeStruct((8, D), F32)],
        scratch_shapes=[pltpu.VMEM((2, NCH, 8, SH_UP), F32)],
        compiler_params=_cparams(1, VMEM_BIG),
    )(dx2, x1, g2, up, pre, w_up_g.reshape(2, NCH, D, SH_UP), fcw.reshape(2, NCH, 3, SH_UP), w_dn)


def bwd_mix(dx1, z, y_pool, y_conv, pool_w, pool_scale, conv_w, wmix, tm, seq):
    t = dx1.shape[0]
    nt = t // tm
    tps = seq // tm
    hb = tm // HALO

    def body(da_ref, z_ref, zph_ref, zcvh_ref, yp_ref, yc_ref, pw_ref, ps_ref, wpp_ref, cw_ref, wco_ref, wo_ref,
             dz_ref, mg_ref, p2_ref, u_ref, dyp_ref, dyc_ref, p_ref, dpw_ref, gvec_ref, cp_ref, cc_ref):
        i = pl.program_id(0)
        it = (nt - 1 - i) % tps
        keep_hist = jnp.where(it == 0, 0.0, 1.0)
        keep_next = jnp.where(it == tps - 1, 0.0, 1.0)
        pos = it * tm + lax.broadcasted_iota(jnp.int32, (tm, 1), 0)

        @pl.when(i == 0)
        def _():
            gvec_ref[...] = jnp.zeros_like(gvec_ref)
            cp_ref[...] = jnp.zeros_like(cp_ref)
            cc_ref[...] = jnp.zeros_like(cc_ref)

        dm = _dot_nt(da_ref[...].astype(BF16), wo_ref[...])
        merged, dyp, dyc = [], [], []
        for q in range(NG):
            lanes = slice(q * CG, (q + 1) * CG)
            sp = jax.nn.sigmoid(z_ref[16 + q].astype(F32))
            sc = jax.nn.sigmoid(z_ref[20 + q].astype(F32))
            yp = yp_ref[:, lanes].astype(F32)
            yc = yc_ref[:, lanes].astype(F32)
            dmq = dm[:, lanes]
            merged.append((sp * yp + sc * yc).astype(BF16))
            dyp.append((dmq * sp).astype(BF16))
            dyc.append((dmq * sc).astype(BF16))
            dz_ref[16 + q] = (dmq * yp * (sp * (1.0 - sp))).astype(BF16)
            dz_ref[20 + q] = (dmq * yc * (sc * (1.0 - sc))).astype(BF16)
        mg_ref[...] = jnp.concatenate(merged, axis=1)
        dypb = jnp.concatenate(dyp, axis=1)
        dycb = jnp.concatenate(dyc, axis=1)
        dyp_ref[...] = dypb
        dyc_ref[...] = dycb

        dp2 = _dot_nt(dypb, wpp_ref[...])
        p2 = []
        for g, win in enumerate(WINS):
            lanes = slice(g * CG, (g + 1) * CG)
            cnt = jnp.minimum(pos + 1, win).astype(F32)
            p = _pool_tile(z_ref, zph_ref, g, win, keep_hist, cnt)
            pb = p.astype(BF16)
            p_ref[g] = pb
            pw = _dot(pb, pw_ref[g])
            p2.append((pw * ps_ref[:, lanes]).astype(BF16))
            dp2g = dp2[:, lanes]
            gvec_ref[0:1, lanes] += jnp.sum(dp2g * pw, axis=0, keepdims=True)
            dpwb = (dp2g * ps_ref[:, lanes]).astype(BF16)
            dpw_ref[g] = dpwb
            dp = _dot_nt(dpwb, pw_ref[g])
            qv = dp / cnt
            ext = jnp.concatenate([qv, cp_ref[g] * keep_next], axis=0)
            cp_ref[g] = qv[:HALO]
            n = tm + HALO
            s, sh = ext, 1
            while sh < win:
                s = s + pltpu.roll(s, n - sh, 0)
                sh *= 2
            dz_ref[g] = (s[:tm] - dp).astype(BF16)
        p2_ref[...] = jnp.concatenate(p2, axis=1)

        du = _dot_nt(dycb, wco_ref[...])
        u = []
        for q in range(NG):
            lanes = slice(q * CG, (q + 1) * CG)
            zb = z_ref[4 + q].astype(F32)
            zc = z_ref[8 + q].astype(F32)
            zv = z_ref[12 + q].astype(F32)
            cv = zc * zv
            cvh = zcvh_ref[q].astype(F32) * zcvh_ref[4 + q].astype(F32) * keep_hist
            cc, cv1, cv2 = _conv_taps(jnp.concatenate([cvh, cv], axis=0), cv, cw_ref, lanes, HALO)
            u.append((zb * cc).astype(BF16))
            duq = du[:, lanes]
            dz_ref[4 + q] = (duq * cc).astype(BF16)
            dcc = duq * zb
            for tap, src in enumerate((cv2, cv1, cv)):
                gvec_ref[tap + 1:tap + 2, lanes] += jnp.sum(dcc * src, axis=0, keepdims=True)
            ext = jnp.concatenate([dcc, cc_ref[:, lanes] * keep_next], axis=0)
            cc_ref[:, lanes] = dcc[:8]
            dcv = (cw_ref[2:3, lanes] * dcc + cw_ref[1:2, lanes] * _shift_up(ext, 1, tm)
                   + cw_ref[0:1, lanes] * _shift_up(ext, 2, tm))
            dz_ref[8 + q] = (dcv * zv).astype(BF16)
            dz_ref[12 + q] = (dcv * zc).astype(BF16)
        u_ref[...] = jnp.concatenate(u, axis=1)

    def hist(i):
        return jnp.maximum((nt - 1 - i) * hb - 1, 0)

    rev = lambda i: (nt - 1 - i, 0)
    rev3 = lambda i: (0, nt - 1 - i, 0)
    const2 = lambda i: (0, 0)
    tok = jax.ShapeDtypeStruct((t, D), BF16)
    grp = jax.ShapeDtypeStruct((NG, t, CG), BF16)
    return pl.pallas_call(
        body, name="bwd_mix", grid=(nt,),
        in_specs=[pl.BlockSpec((tm, D), rev), pl.BlockSpec((NZT, tm, CG), rev3),
                  pl.BlockSpec((NG, HALO, CG), lambda i: (0, hist(i), 0)),
                  pl.BlockSpec((2 * NG, HALO, CG), lambda i: (1, hist(i), 0)),
                  pl.BlockSpec((tm, D), rev), pl.BlockSpec((tm, D), rev),
                  pl.BlockSpec((NG, CG, CG), lambda i: (0, 0, 0)), pl.BlockSpec((1, D), const2),
                  pl.BlockSpec((D, D), lambda i: (0, MIX_POOL_PROJ)), pl.BlockSpec((3, D), const2),
                  pl.BlockSpec((D, D), lambda i: (0, MIX_CONV_OUT)), pl.BlockSpec((D, D), lambda i: (0, MIX_O))],
        out_specs=[pl.BlockSpec((NZT, tm, CG), rev3)] + [pl.BlockSpec((tm, D), rev)] * 5
                  + [pl.BlockSpec((NG, tm, CG), rev3)] * 2 + [pl.BlockSpec((8, D), const2)],
        out_shape=[jax.ShapeDtypeStruct((NZT, t, CG), BF16), tok, tok, tok, tok, tok, grp, grp,
                   jax.ShapeDtypeStruct((8, D), F32)],
        scratch_shapes=[pltpu.VMEM((NG, HALO, CG), F32), pltpu.VMEM((8, D), F32)],
        compiler_params=_cparams(1, VMEM_BIG),
    )(dx1, z, z, z, y_pool, y_conv, pool_w, pool_scale, wmix, conv_w, wmix, wmix)


def bwd_in(dz, w_in_g, dx1, x, g1, tm):
    t = x.shape[0]

    def body(dz_ref, w_ref, dx1_ref, x_ref, g_ref, gx_ref, gn_ref):
        @pl.when(pl.program_id(0) == 0)
        def _():
            gn_ref[...] = jnp.zeros_like(gn_ref)

        dh = None
        for j in range(NDEV):
            dzc = jnp.concatenate([dz_ref[3 * j + q] for q in range(3)], axis=1)
            part = _dot_nt(dzc, w_ref[j])
            dh = part if dh is None else dh + part
        xv = x_ref[...]
        inv = _rms_inv(xv)
        xn = xv * inv
        gn_ref[0:1, :] += jnp.sum(dh * xn, axis=0, keepdims=True)
        gx_ref[...] = dx1_ref[...] + _rms_bwd(dh, xn, inv, g_ref[...])

    tile = lambda i: (i, 0)
    return pl.pallas_call(
        body, name="bwd_in", grid=(t // tm,),
        in_specs=[pl.BlockSpec((NZT, tm, CG), lambda i: (0, i, 0)),
                  pl.BlockSpec((NDEV, D, SH_IN), lambda i: (0, 0, 0)),
                  pl.BlockSpec((tm, D), tile), pl.BlockSpec((tm, D), tile), pl.BlockSpec((1, D), lambda i: (0, 0))],
        out_specs=[pl.BlockSpec((tm, D), tile), pl.BlockSpec((8, D), lambda i: (0, 0))],
        out_shape=[jax.ShapeDtypeStruct((t, D), F32), jax.ShapeDtypeStruct((8, D), F32)],
        compiler_params=_cparams(1, VMEM_BIG),
    )(dz, w_in_g, dx1, x, g1)


def _slot(j):
    return j % 2, j // 2


def wgrad_cols(at, b, q, name, tk):
    m, t = at.shape
    width = b.shape[3]

    def body(a_ref, b_ref, o_ref):
        @pl.when(pl.program_id(1) == 0)
        def _():
            o_ref[...] = jnp.zeros_like(o_ref)

        o_ref[...] += _dot(a_ref[...], b_ref[...])

    return pl.pallas_call(
        body, name=name, grid=(NDEV, t // tk),
        in_specs=[pl.BlockSpec((m, tk), lambda j, k: (0, k)),
                  pl.BlockSpec((None, None, tk, width), lambda j, k: (j, q, k, 0))],
        out_specs=pl.BlockSpec((None, None, m, width), lambda j, k: (j % 2, j // 2, 0, 0)),
        out_shape=jax.ShapeDtypeStruct((2, 4, m, width), F32),
        compiler_params=_cparams(2, VMEM_BIG),
    )(at, b)


def wgrad_cols_resident(at, b, q, name, tk):
    m, t = at.shape
    width = b.shape[3]

    def body(a_ref, b_ref, o_ref):
        k, j = pl.program_id(0), pl.program_id(1)

        @pl.when((k == 0) & (j == 0))
        def _():
            o_ref[...] = jnp.zeros_like(o_ref)

        o_ref[j % 2, j // 2] += _dot(a_ref[...], b_ref[...])

    return pl.pallas_call(
        body, name=name, grid=(t // tk, NDEV),
        in_specs=[pl.BlockSpec((m, tk), lambda k, j: (0, k)),
                  pl.BlockSpec((None, None, tk, width), lambda k, j: (j, q, k, 0))],
        out_specs=pl.BlockSpec((2, 4, m, width), lambda k, j: (0, 0, 0, 0)),
        out_shape=jax.ShapeDtypeStruct((2, 4, m, width), F32),
        compiler_params=_cparams(2, VMEM_BIG),
    )(at, b)


def wgrad_down(actt, dx2b):
    t = dx2b.shape[0]

    def body(a_ref, b_ref, o_ref):
        r = _dot(a_ref[...], b_ref[...])
        o_ref[0] = r[:SH_DN]
        o_ref[1] = r[SH_DN:]

    return pl.pallas_call(
        body, name="wgrad_down", grid=(NCH,),
        in_specs=[pl.BlockSpec((None, SH_UP, t), lambda k: (k, 0, 0)), pl.BlockSpec((t, D), lambda k: (0, 0))],
        out_specs=pl.BlockSpec((2, None, SH_DN, D), lambda k: (0, k, 0, 0)),
        out_shape=jax.ShapeDtypeStruct((2, 4, SH_DN, D), F32),
        compiler_params=_cparams(1, VMEM_BIG),
    )(actt, dx2b)


def wgrad_square(a, b, name, tk):
    t = a.shape[0]

    def body(a_ref, b_ref, o_ref, acc_ref):
        kt = pl.program_id(0)

        @pl.when(kt == 0)
        def _():
            acc_ref[...] = jnp.zeros_like(acc_ref)

        acc_ref[...] += _dot_tn(a_ref[...], b_ref[...].astype(BF16))

        @pl.when(kt == pl.num_programs(0) - 1)
        def _():
            for j in range(NDEV):
                cc, xy = _slot(j)
                o_ref[cc, xy] = acc_ref[j * 128:(j + 1) * 128]

    return pl.pallas_call(
        body, name=name, grid=(t // tk,),
        in_specs=[pl.BlockSpec((tk, D), lambda k: (k, 0)), pl.BlockSpec((tk, D), lambda k: (k, 0))],
        out_specs=pl.BlockSpec((2, 4, 128, D), lambda k: (0, 0, 0, 0)),
        out_shape=jax.ShapeDtypeStruct((2, 4, 128, D), F32),
        scratch_shapes=[pltpu.VMEM((D, D), F32)],
        compiler_params=_cparams(1, VMEM_BIG),
    )(a, b)


def wgrad_pool(p, dpw, tk):
    t = p.shape[1]

    def body(a_ref, b_ref, o_ref):
        @pl.when(pl.program_id(0) == 0)
        def _():
            o_ref[...] = jnp.zeros_like(o_ref)

        for g in range(NG):
            o_ref[g] += _dot_tn(a_ref[g], b_ref[g])

    return pl.pallas_call(
        body, name="wgrad_pool", grid=(t // tk,),
        in_specs=[pl.BlockSpec((NG, tk, CG), lambda k: (0, k, 0))] * 2,
        out_specs=pl.BlockSpec((NG, CG, CG), lambda k: (0, 0, 0)),
        out_shape=jax.ShapeDtypeStruct((NG, CG, CG), F32),
        compiler_params=_cparams(1, VMEM_BIG),
    )(p, dpw)


def _adamw(w, g, m, v):
    m = ADAM_B1 * m + (1.0 - ADAM_B1) * g
    v = ADAM_B2 * v + (1.0 - ADAM_B2) * (g * g)
    m_hat = m / (1.0 - ADAM_B1 ** ADAM_STEP)
    v_hat = v / (1.0 - ADAM_B2 ** ADAM_STEP)
    delta = -ADAM_LR * (m_hat / (jnp.sqrt(v_hat) + ADAM_EPS) + ADAM_WD * w)
    return delta, m, v


def _row_block(r):
    return 512 if r % 512 == 0 else r


def chip_partial(place, g, from_sibling, name):
    _, _, r, c = g.shape

    def body(place_ref, g_ref, s_ref, o_ref):
        o_ref[...] = (g_ref[...] + s_ref[...]).astype(BF16)

    return pl.pallas_call(
        body, name=name,
        grid_spec=pltpu.PrefetchScalarGridSpec(
            num_scalar_prefetch=1, grid=(3,),
            in_specs=[pl.BlockSpec((None, None, r, c), lambda k, pr: (pr[0], pr[1] ^ (k + 1), 0, 0)),
                      pl.BlockSpec((None, r, c), lambda k, pr: (pr[1] ^ (k + 1), 0, 0))],
            out_specs=pl.BlockSpec((None, r, c), lambda k, pr: (pr[1] ^ (k + 1), 0, 0))),
        out_shape=jax.ShapeDtypeStruct((4, r, c), BF16),
        compiler_params=_cparams(1, VMEM_BIG),
    )(place, g, from_sibling)


def finish_adamw(place, gs, from_sibling, from_chips, w, m, v, name):
    n = len(gs)
    r = gs[0].shape[2]
    widths = [g.shape[3] for g in gs]
    c = sum(widths)
    br = _row_block(r)

    def body(place_ref, *refs):
        g_refs, s_refs, c_refs = refs[:n], refs[n:2 * n], refs[2 * n:5 * n]
        w_ref, m_ref, v_ref, og_ref, od_ref, om_ref, ov_ref = refs[5 * n:]
        cols = []
        for q in range(n):
            grad = g_refs[q][...] + s_refs[q][...]
            for k in range(3):
                grad = grad + c_refs[3 * q + k][...].astype(F32)
            cols.append(grad)
        grad = cols[0] if n == 1 else jnp.concatenate(cols, axis=1)
        og_ref[...] = grad
        od_ref[...], om_ref[...], ov_ref[...] = _adamw(w_ref[...], grad, m_ref[...], v_ref[...])

    def other(k, cq):
        return pl.BlockSpec((None, br, cq), lambda i, pr: (pr[1] ^ k, i, 0))

    row = pl.BlockSpec((br, c), lambda i, pr: (i, 0))
    out = jax.ShapeDtypeStruct((r, c), F32)
    in_specs = [pl.BlockSpec((None, None, br, cq), lambda i, pr: (pr[0], pr[1], i, 0)) for cq in widths]
    in_specs += [pl.BlockSpec((None, br, cq), lambda i, pr: (pr[1], i, 0)) for cq in widths]
    in_specs += [other(k, cq) for cq in widths for k in (1, 2, 3)]
    return pl.pallas_call(
        body, name=name,
        grid_spec=pltpu.PrefetchScalarGridSpec(
            num_scalar_prefetch=1, grid=(r // br,), in_specs=in_specs + [row, row, row], out_specs=[row] * 4),
        out_shape=[out] * 4,
        compiler_params=_cparams(1, VMEM_BIG),
    )(place, *gs, *from_sibling, *[fc for fc in from_chips for _ in range(3)], w, m, v)


def adamw_small(items):
    n = len(items)

    def body(*refs):
        ins, outs = refs[:4 * n], refs[4 * n:]
        for i in range(n):
            w, g, m, v = (r[...] for r in ins[4 * i:4 * i + 4])
            outs[3 * i][...], outs[3 * i + 1][...], outs[3 * i + 2][...] = _adamw(w, g, m, v)

    out = [jax.ShapeDtypeStruct(it[0].shape, F32) for it in items for _ in range(3)]
    res = pl.pallas_call(body, name="adamw_small", out_shape=out)(*[a for it in items for a in it])
    return [res[3 * i:3 * i + 3] for i in range(n)]


def kernel(x, norm_mix, w_in, pool_w, pool_scale, w_pool_proj, conv_w, w_conv_out, w_o, norm_ffn, w_up, ffn_conv_w, ffn_conv_b, w_down, norm_final, loss_target, m_norm_mix, m_w_in, m_pool_w, m_pool_scale, m_w_pool_proj, m_conv_w, m_w_conv_out, m_w_o, m_norm_ffn, m_w_up, m_ffn_conv_w, m_ffn_conv_b, m_w_down, m_norm_final, v_norm_mix, v_w_in, v_pool_w, v_pool_scale, v_w_pool_proj, v_conv_w, v_w_conv_out, v_w_o, v_norm_ffn, v_w_up, v_ffn_conv_w, v_ffn_conv_b, v_w_down, v_norm_final):
    nb, seq, _ = x.shape
    t = nb * seq
    tm_in = min(TM_IN, t)
    tm_mix = min(TM_MIX, seq)
    tm_ffn = min(TM_FFN, seq)
    tk = min(TK_WGRAD, t)
    xt = x.reshape(t, D)
    tgt = loss_target.reshape(t, D)
    xi, yi, ci = _pos()
    me = 4 * xi + 2 * yi + ci
    place = jnp.stack([ci, 2 * xi + yi]).astype(jnp.int32)

    tie = lax.optimization_barrier
    w_in_g, = all_gather_blocks([w_in[0].astype(BF16)], "all_gather_w_in", 0)
    taps = (jnp.pad(conv_w[0], ((0, 5), (0, D - 128))) + jnp.pad(ffn_conv_w[0], ((3, 2), (0, D - SH_UP))))
    taps_g = _exchange_small(taps, False, "all_gather_taps")
    mix_shard = jnp.concatenate(
        [w_pool_proj[0], w_conv_out[0], w_o[0], pool_w[0].reshape(NG * 32, CG)], axis=1).astype(BF16)
    mix_shard, taps_g = tie((mix_shard, taps_g))
    wmix_g, = all_gather_blocks([mix_shard], "all_gather_w_mix", 0)
    ffn_shards, w_in_g = tie(([w_up[0].astype(BF16), w_down[0].astype(BF16)], w_in_g))
    w_up_g, = all_gather_blocks(ffn_shards[:1], "all_gather_w_up", 0)
    w_dn_g, = all_gather_blocks(ffn_shards[1:], "all_gather_w_down", 0)
    w_dn_f = w_dn_g.reshape(NCH, SH_UP, D)
    conv_w_f = taps_g[:, 0:3, :128].transpose(1, 0, 2).reshape(3, D)
    fcw_f = taps_g[:, 3:6, :SH_UP]
    fcb_f = ffn_conv_b.reshape(NDEV, 1, SH_UP)
    gfin = norm_final.reshape(1, D)

    z, h1 = fwd_in(xt, norm_mix, w_in_g, tm_in)
    wmix_g, z = tie((wmix_g, z))
    wmix = wmix_g.reshape(D, MIX_COLS)
    pool_w_f = wmix_g[:, :, 3 * D:].reshape(NDEV, NG, 32, CG).transpose(1, 0, 2, 3).reshape(NG, CG, CG)
    x1, y_pool, y_conv = fwd_mix(z, xt, pool_w_f, pool_scale, conv_w_f, wmix, tm_mix, seq)
    up, pre, act_tok, act, h2 = fwd_up(x1, norm_ffn, w_up_g, fcw_f, fcb_f, tm_ffn, seq)
    dx2, dx2b, ffn_vec = fwd_down(x1, act_tok, w_dn_f, gfin, tgt, min(TM_IN, t))

    def to_sibling(full, tag):
        return reduce_scatter_d2d(full, "reduce_scatter_d2d_" + tag, 1)

    def partials(full, from_sib, names):
        return [chip_partial(place, g, s, "chip_partial_" + nm) for g, s, nm in zip(full, from_sib, names)]

    def to_chips(parts, tag):
        return reduce_scatter_ici(parts, "reduce_scatter_ici_" + tag, 2)

    def finish(nm, gs, from_sib, from_chips, wmv):
        w, m, v = wmv
        rc = (gs[0].shape[2], sum(g.shape[3] for g in gs))
        outs = finish_adamw(place, gs, from_sib, from_chips, w.reshape(rc), m.reshape(rc), v.reshape(rc), "adamw_" + nm)
        return [o.reshape(w.shape) for o in outs]

    def after(x, dep):
        return tie((x, dep))[0]

    big = {}
    d_up, dx1, g_ffn_vec, g_nffn = bwd_ffn(dx2, x1, norm_ffn, up, pre, w_up_g, fcw_f, w_dn_f, tm_ffn, seq)
    gw_up = wgrad_cols(h2, d_up.reshape(NDEV, 1, t, SH_UP), 0, "wgrad_up", t)
    sib_up = to_sibling([gw_up], "w_up")
    gw_dn = wgrad_down(act, after(dx2b, gw_up))
    sib_dn = to_sibling([after(gw_dn, sib_up)], "w_down")
    dx1, part_up = tie((dx1, partials([gw_up], sib_up, ["w_up"])))
    chips_up = to_chips(part_up, "w_up")
    dz, merged, p2, u, dyp, dyc, p, dpw, g_mix_vec = bwd_mix(
        dx1, z, y_pool, y_conv, pool_w_f, pool_scale, conv_w_f, wmix, tm_mix, seq)
    merged, part_dn = tie((merged, partials([gw_dn], sib_dn, ["w_down"])))
    chips_dn = to_chips(part_dn, "w_down")
    gw_o = wgrad_square(merged, dx1, "wgrad_o", tk)
    gw_pp = wgrad_square(p2, dyp, "wgrad_pool_proj", tk)
    gw_co = wgrad_square(u, dyc, "wgrad_conv_out", tk)
    gw_pool = wgrad_pool(p, dpw, tk).reshape(NG, 4, 2, 32, CG).transpose(2, 1, 0, 3, 4).reshape(2, 4, NG * 32, CG)
    dz8 = dz.reshape(NDEV, 3, t, CG)
    gw_in, sib_in, chips_in = [None] * 3, [None] * 3, [None] * 3
    sib_a = to_sibling(after([gw_o, gw_pp], (chips_up, gw_pool)), "mix_a")
    sib_b = to_sibling(after([gw_co, gw_pool], sib_a), "mix_b")
    gw_in[0] = wgrad_cols_resident(h1, dz8, 0, "wgrad_in_0", t)
    h1, part_a, part_b = tie((h1, partials([gw_o, gw_pp], sib_a, ["w_o", "w_pool_proj"]),
                              partials([gw_co, gw_pool], sib_b, ["w_conv_out", "pool_w"])))
    chips_a = to_chips(after(part_a, chips_dn), "mix_a")
    chips_b = to_chips(part_b, "mix_b")
    sib_in[0] = to_sibling(after([gw_in[0]], sib_b), "w_in_0")
    gw_in[1] = wgrad_cols_resident(h1, dz8, 1, "wgrad_in_1", t)
    h1, part_in0, gw_in[1] = tie((h1, partials([gw_in[0]], sib_in[0], ["w_in_0"]), gw_in[1]))
    chips_in[0] = to_chips(part_in0, "w_in_0")
    sib_in[1] = to_sibling(after([gw_in[1]], sib_in[0]), "w_in_1")
    h1, big["w_down"], big["w_up"] = tie((
        h1, finish("w_down", [gw_dn], sib_dn, chips_dn, (w_down, m_w_down, v_w_down)),
        finish("w_up", [gw_up], sib_up, chips_up, (w_up, m_w_up, v_w_up))))
    gw_in[2] = wgrad_cols_resident(h1, dz8, 2, "wgrad_in_2", t)
    sib_in[2] = to_sibling(after([gw_in[2]], (chips_a, chips_b, chips_in[0])), "w_in_2")
    dx1, part_in1, part_in2, big["w_o"], big["w_pool_proj"], big["w_conv_out"], big["pool_w"] = tie((
        dx1, partials([gw_in[1]], sib_in[1], ["w_in_1"]), partials([gw_in[2]], sib_in[2], ["w_in_2"]),
        finish("w_o", [gw_o], sib_a[:1], chips_a[:1], (w_o, m_w_o, v_w_o)),
        finish("w_pool_proj", [gw_pp], sib_a[1:], chips_a[1:], (w_pool_proj, m_w_pool_proj, v_w_pool_proj)),
        finish("w_conv_out", [gw_co], sib_b[:1], chips_b[:1], (w_conv_out, m_w_conv_out, v_w_conv_out)),
        finish("pool_w", [gw_pool], sib_b[1:], chips_b[1:], (pool_w, m_pool_w, v_pool_w))))
    chips_in[1] = to_chips(after(part_in1, sib_in[2]), "w_in_1")
    chips_in[2] = to_chips(part_in2, "w_in_2")
    small_g, = all_gather_blocks(
        [after(jnp.concatenate([g_mix_vec, g_nffn, ffn_vec, g_ffn_vec.reshape(8 * NDEV, D)], axis=0), sib_in[2])],
        "all_gather_small", 0)
    grad_x, g_nmix = bwd_in(dz, w_in_g, dx1, xt, norm_mix, min(TM_BWD_IN, t))
    grad_x, chips_in = tie((grad_x, chips_in))
    big["w_in"] = finish("w_in", gw_in, [s[0] for s in sib_in], [c[0] for c in chips_in], (w_in, m_w_in, v_w_in))

    red_n, red = _exchange_small(g_nmix, True, "all_reduce_small", gathered=small_g)
    g_norm_mix, g_pool_scale, g_norm_ffn = red_n[0:1], red[0:1], red[8:9]
    g_conv_w = lax.dynamic_slice(red, (1, me * 128), (3, 128))
    g_norm_final = red[16]
    loss = red[17, 0]
    g_fcb = red[24:].reshape(NDEV, 8, D)[:, 0, :SH_UP].reshape(1, FF2)
    g_fcw = lax.dynamic_slice(red, (25 + 8 * me, 0), (3, SH_UP))
    grads = {"norm_mix": g_norm_mix, "pool_scale": g_pool_scale, "norm_ffn": g_norm_ffn, "norm_final": g_norm_final,
             "ffn_conv_b": g_fcb, "conv_w": g_conv_w.reshape(1, 3, 128), "ffn_conv_w": g_fcw.reshape(1, 3, SH_UP)}
    small_wmv = {"norm_mix": (norm_mix, m_norm_mix, v_norm_mix), "pool_scale": (pool_scale, m_pool_scale, v_pool_scale),
                 "norm_ffn": (norm_ffn, m_norm_ffn, v_norm_ffn), "norm_final": (norm_final, m_norm_final, v_norm_final),
                 "ffn_conv_b": (ffn_conv_b, m_ffn_conv_b, v_ffn_conv_b), "conv_w": (conv_w, m_conv_w, v_conv_w),
                 "ffn_conv_w": (ffn_conv_w, m_ffn_conv_w, v_ffn_conv_w)}
    small_names = list(small_wmv)
    flat2 = lambda a: a.reshape(1, -1) if a.ndim == 1 else a
    small_out = adamw_small([(flat2(small_wmv[nm][0]), flat2(grads[nm]), flat2(small_wmv[nm][1]),
                              flat2(small_wmv[nm][2])) for nm in small_names])
    small = {nm: [o.reshape(small_wmv[nm][0].shape) for o in outs] for nm, outs in zip(small_names, small_out)}

    order = ["norm_mix", "w_in", "pool_w", "pool_scale", "w_pool_proj", "conv_w", "w_conv_out", "w_o", "norm_ffn",
             "w_up", "ffn_conv_w", "ffn_conv_b", "w_down", "norm_final"]
    out = [loss, grad_x.reshape(nb, seq, D)]
    out += [big[nm][0] if nm in big else grads[nm] for nm in order]
    for idx in range(3):
        out += [big[nm][idx + 1] if nm in big else small[nm][idx] for nm in order]
    return tuple(out)
```

```python
import jax
import jax.numpy as jnp
from jax import lax
from jax.experimental import pallas as pl
from jax.experimental.pallas import tpu as pltpu
from jax.experimental.pallas import tpu_sc as plsc

F32 = jnp.float32
BF16 = jnp.bfloat16

NDEV = 8
D = 1024
NG = 4
CG = 256
WINS = (2, 4, 8, 16)
DIN = 6 * D
SH_IN = DIN // NDEV
NZT = DIN // CG
FF2 = 5632
SH_UP = FF2 // NDEV
FF = FF2 // 2
NCH = 4
SH_DN = FF // NDEV
RMS_EPS = 1e-6
HALO = 16

ADAM_LR = 0.001
ADAM_B1 = 0.9
ADAM_B2 = 0.999
ADAM_EPS = 1e-08
ADAM_WD = 0.01
ADAM_STEP = 10

TM_IN = 512
TM_BWD_IN = 256
TM_MIX = 256
TM_FFN = 256
TK_WGRAD = 2048
MIX_POOL_PROJ, MIX_CONV_OUT, MIX_O = 0, 1, 2
MIX_COLS = 3 * D + CG
VMEM_BIG = 56 * 1024 * 1024
MESH = pl.DeviceIdType.MESH
ANY = pl.BlockSpec(memory_space=pl.ANY)


def _cparams(n_axes, vmem=None):
    return pltpu.CompilerParams(dimension_semantics=("arbitrary",) * n_axes, vmem_limit_bytes=vmem)


def _dot(a, b):
    return jnp.dot(a, b, preferred_element_type=F32)


def _dot_nt(a, b):
    return lax.dot_general(a, b, (((1,), (1,)), ((), ())), preferred_element_type=F32)


def _dot_tn(a, b):
    return lax.dot_general(a, b, (((0,), (0,)), ((), ())), preferred_element_type=F32)


def _shift_down(ext, s, lead):
    return pltpu.roll(ext, s, 0)[lead:]


def _shift_up(ext, s, tm):
    n = ext.shape[0]
    return pltpu.roll(ext, n - s, 0)[:tm]


def _rms_inv(x):
    return lax.rsqrt(jnp.mean(x * x, axis=-1, keepdims=True) + RMS_EPS)


def _rms_bwd(dh, xn, inv, g):
    dxn = dh * g
    return inv * (dxn - xn * jnp.mean(dxn * xn, axis=-1, keepdims=True))


def _pos():
    return lax.axis_index("x"), lax.axis_index("y"), lax.axis_index("c")


def _handshake(peers):
    barrier = pltpu.get_barrier_semaphore()
    for peer in peers:
        pl.semaphore_signal(barrier, inc=1, device_id=peer, device_id_type=MESH)
    pl.semaphore_wait(barrier, len(peers))


def _sequencer(body, out_type, n_sems, name, collective_id):
    return pl.kernel(
        body, out_type=out_type, mesh=plsc.ScalarSubcoreMesh(axis_name="sequencer", num_cores=1), name=name,
        scratch_types=[pltpu.SemaphoreType.DMA((n_sems,)), pltpu.SemaphoreType.DMA((n_sems,))],
        compiler_params=pltpu.CompilerParams(collective_id=collective_id))


def all_gather_blocks(shards, name, collective_id):
    n = len(shards)

    def body(*refs):
        ins, outs = refs[:n], refs[n:2 * n]
        send_sems, recv_sems = refs[2 * n:]
        x, y, c = _pos()
        me, sibling = (x, y, c), (x, y, 1 - c)
        first_chip, second_chip, diagonal = (x ^ (1 - c), y ^ c), (x ^ c, y ^ (1 - c)), (1 - x, 1 - y)
        first, second = (*first_chip, c), (*second_chip, c)
        _handshake([sibling, first, second])

        def copy(w, k, block, to, src=None):
            slot = outs[w].at[4 * block[0] + 2 * block[1] + block[2]]
            return pltpu.make_async_remote_copy(
                src_ref=slot if src is None else src, dst_ref=slot,
                send_sem=send_sems.at[8 * w + k], recv_sem=recv_sems.at[8 * w + k], device_id=to, device_id_type=MESH)

        mine, sent = [], []
        for w in range(n):
            m = pltpu.make_async_copy(ins[w], outs[w].at[4 * x + 2 * y + c], send_sems.at[8 * w + 7])
            m.start()
            mine.append(m)
            sent += [copy(w, k, me, to, src=ins[w]) for k, to in enumerate((sibling, first, second))]
        for cp in sent:
            cp.start()
        for k, chip in ((1, first_chip), (2, second_chip), (3, diagonal)):
            for w in range(n):
                copy(w, k, (*chip, c), me).wait_recv()
                onward = [copy(w, 3 + k, (*chip, c), sibling)] + ([copy(w, 3, (*chip, c), second)] if k == 1 else [])
                for cp in onward:
                    cp.start()
                sent += onward
        for w in range(n):
            copy(w, 0, sibling, me).wait_recv()
            for k, chip in ((4, second_chip), (5, first_chip), (6, diagonal)):
                copy(w, k, (*chip, 1 - c), me).wait_recv()
        for cp in sent:
            cp.wait_send()
        for m in mine:
            m.wait()

    out = [jax.ShapeDtypeStruct((NDEV,) + s.shape, s.dtype) for s in shards]
    return _sequencer(body, out, 8 * n, name, collective_id)(*shards)


def _exchange_small(v, reduce, name, gathered=None):
    rows = v.shape[0]

    def body(*refs):
        if gathered is None:
            v_ref, out_ref, slots, send_sems, recv_sems, local_sem = refs
        else:
            v_ref, g_ref, out_ref, gsum_ref, slots, send_sems, recv_sems, local_sem = refs
        x, y, c = _pos()
        me = 4 * x + 2 * y + c
        mine = pltpu.make_async_copy(v_ref, slots.at[me], local_sem)
        mine.start()
        offs = [(dx, dy, dc) for dx in (0, 1) for dy in (0, 1) for dc in (0, 1)][1:]

        def copy(k, src_slot, to):
            return pltpu.make_async_remote_copy(
                src_ref=v_ref, dst_ref=slots.at[src_slot], send_sem=send_sems.at[k], recv_sem=recv_sems.at[k],
                device_id=to, device_id_type=MESH)

        sends = []
        for k, (dx, dy, dc) in enumerate(offs):
            cp = copy(k, me, (x ^ dx, y ^ dy, c ^ dc))
            cp.start()
            sends.append(cp)
        for k, (dx, dy, dc) in enumerate(offs):
            copy(k, 4 * (x ^ dx) + 2 * (y ^ dy) + (c ^ dc), (x, y, c)).wait_recv()
        for cp in sends:
            cp.wait_send()
        mine.wait()
        if reduce:
            acc = slots[0]
            for d in range(1, NDEV):
                acc = acc + slots[d]
            out_ref[...] = acc
        else:
            out_ref[...] = slots[...]
        if gathered is not None:
            acc = g_ref[0]
            for d in range(1, NDEV):
                acc = acc + g_ref[d]
            gsum_ref[...] = acc

    vmem = pl.BlockSpec(memory_space=pltpu.VMEM)
    out = jax.ShapeDtypeStruct((rows, D) if reduce else (NDEV, rows, D), F32)
    args, out_shape, out_specs = [v], out, vmem
    if gathered is not None:
        args.append(gathered)
        out_shape, out_specs = [out, jax.ShapeDtypeStruct(gathered.shape[1:], F32)], [vmem, vmem]
    return pl.pallas_call(
        body, name=name, out_shape=out_shape, in_specs=[vmem] * len(args), out_specs=out_specs,
        scratch_shapes=[pltpu.VMEM((NDEV, rows, D), F32), pltpu.SemaphoreType.DMA((7,)),
                        pltpu.SemaphoreType.DMA((7,)), pltpu.SemaphoreType.DMA],
    )(*args)


def reduce_scatter_d2d(grads, name, collective_id):
    n = len(grads)

    def body(*refs):
        ins, outs = refs[:n], refs[n:2 * n]
        send_sems, recv_sems = refs[2 * n:]
        x, y, c = _pos()
        _handshake([(x, y, 1 - c)])
        cps = []
        for w in range(n):
            cp = pltpu.make_async_remote_copy(
                src_ref=ins[w].at[1 - c], dst_ref=outs[w], send_sem=send_sems.at[w], recv_sem=recv_sems.at[w],
                device_id=(x, y, 1 - c), device_id_type=MESH)
            cp.start()
            cps.append(cp)
        for cp in cps:
            cp.wait_recv()
        for cp in cps:
            cp.wait_send()

    out = [jax.ShapeDtypeStruct(g.shape[1:], F32) for g in grads]
    return _sequencer(body, out, n, name, collective_id)(*grads)


def reduce_scatter_ici(parts, name, collective_id):
    n = len(parts)

    def body(*refs):
        ins, outs = refs[:n], refs[n:2 * n]
        send_sems, recv_sems = refs[2 * n:]
        x, y, c = _pos()
        offs = [(1, 0), (0, 1), (1, 1)]
        _handshake([(x ^ dx, y ^ dy, c) for dx, dy in offs])
        cps = []
        for w in range(n):
            for k, (dx, dy) in enumerate(offs):
                ox, oy = x ^ dx, y ^ dy
                cp = pltpu.make_async_remote_copy(
                    src_ref=ins[w].at[2 * ox + oy], dst_ref=outs[w].at[2 * x + y],
                    send_sem=send_sems.at[3 * w + k], recv_sem=recv_sems.at[3 * w + k],
                    device_id=(ox, oy, c), device_id_type=MESH)
                cp.start()
                cps.append((cp, w, k, ox, oy))
        for cp, w, k, ox, oy in cps:
            pltpu.make_async_remote_copy(
                src_ref=ins[w].at[2 * ox + oy], dst_ref=outs[w].at[2 * ox + oy],
                send_sem=send_sems.at[3 * w + k], recv_sem=recv_sems.at[3 * w + k],
                device_id=(ox, oy, c), device_id_type=MESH).wait_recv()
        for cp, *_ in cps:
            cp.wait_send()

    out = [jax.ShapeDtypeStruct(p.shape, BF16) for p in parts]
    return _sequencer(body, out, 3 * n, name, collective_id)(*parts)


def fwd_in(x, g1, w_in_pieces, tm):
    t = x.shape[0]
    tile = lambda i: (i, 0)
    w_spec = pl.BlockSpec((NDEV, D, CG), lambda i: (0, 0, 0))
    z_spec = pl.BlockSpec((NDEV, tm, CG), lambda i: (0, i, 0))
    z_shape = jax.ShapeDtypeStruct((NDEV, t, CG), BF16)

    def cost(other_bytes, transcendentals):
        return pl.CostEstimate(flops=2 * t * D * NDEV * CG, transcendentals=transcendentals,
                               bytes_accessed=other_bytes + 2 * D * NDEV * CG + 2 * t * NDEV * CG)

    def first(x_ref, g_ref, w_ref, z_ref, h_ref, ht_ref):
        xf = x_ref[...]
        h = (xf * _rms_inv(xf) * g_ref[...]).astype(BF16)
        h_ref[...] = h
        ht_ref[...] = h.T
        for j in range(NDEV):
            z_ref[j] = _dot(h, w_ref[j]).astype(BF16)

    z0, h, ht = pl.pallas_call(
        first, name="fwd_in_0", grid=(t // tm,),
        in_specs=[pl.BlockSpec((tm, D), tile), pl.BlockSpec((1, D), lambda i: (0, 0)), w_spec],
        out_specs=[z_spec, pl.BlockSpec((tm, D), tile), pl.BlockSpec((D, tm), lambda i: (0, i))],
        out_shape=[z_shape, jax.ShapeDtypeStruct((t, D), BF16), jax.ShapeDtypeStruct((D, t), BF16)],
        compiler_params=_cparams(1, VMEM_BIG), cost_estimate=cost(8 * t * D + 4 * D, t),
    )(x, g1, w_in_pieces[0])
    zs = [z0]
    for q in (1, 2):
        h, zs[-1] = lax.optimization_barrier((h, zs[-1]))

        def later(h_ref, w_ref, z_ref):
            hb = h_ref[...]
            for j in range(NDEV):
                z_ref[j] = _dot(hb, w_ref[j]).astype(BF16)

        zs.append(pl.pallas_call(
            later, name=f"fwd_in_{q}", grid=(t // tm,),
            in_specs=[pl.BlockSpec((tm, D), tile), w_spec], out_specs=z_spec, out_shape=z_shape,
            compiler_params=_cparams(1, VMEM_BIG), cost_estimate=cost(2 * t * D, 0),
        )(h, w_in_pieces[q]))
    return zs, ht


def _z_tile(z_refs, n):
    return z_refs[n % 3][n // 3]


def _pool_tile(z, zh, win, keep_hist, cnt):
    zt = z.astype(F32)
    ext = jnp.concatenate([zh.astype(F32) * keep_hist, zt], axis=0)
    s, sh = ext, 1
    while sh < win:
        s = s + pltpu.roll(s, sh, 0)
        sh *= 2
    return s[HALO:] / cnt - zt


def _conv_taps(ext, cur, w_ref, lanes, lead):
    x1 = _shift_down(ext, 1, lead)
    x2 = _shift_down(ext, 2, lead)
    out = w_ref[2:3, lanes] * cur + w_ref[1:2, lanes] * x1 + w_ref[0:1, lanes] * x2
    return out, x1, x2


def fwd_mix(zs, x, pool_w, pool_scale, conv_w, wmix, tm, seq):
    t = x.shape[0]
    tps = seq // tm
    hb = tm // HALO

    def body(z0_ref, z1_ref, z2_ref, zh0_ref, zh1_ref, zh2_ref, x_ref, pw_ref, ps_ref, wpp_ref, cw_ref, wco_ref,
             wo_ref, x1_ref, yp_ref, yc_ref):
        z_refs, zh_refs = (z0_ref, z1_ref, z2_ref), (zh0_ref, zh1_ref, zh2_ref)
        it = pl.program_id(0) % tps
        keep_hist = jnp.where(it == 0, 0.0, 1.0)
        pos = it * tm + lax.broadcasted_iota(jnp.int32, (tm, 1), 0)
        p2 = []
        for g, win in enumerate(WINS):
            cnt = jnp.minimum(pos + 1, win).astype(F32)
            p = _pool_tile(_z_tile(z_refs, g), _z_tile(zh_refs, g), win, keep_hist, cnt)
            lanes = slice(g * CG, (g + 1) * CG)
            p2.append((_dot(p.astype(BF16), pw_ref[g]) * ps_ref[:, lanes]).astype(BF16))
        y_pool = _dot(jnp.concatenate(p2, axis=1), wpp_ref[...])
        u = []
        for q in range(NG):
            lanes = slice(q * CG, (q + 1) * CG)
            cv = _z_tile(z_refs, 8 + q).astype(F32) * _z_tile(z_refs, 12 + q).astype(F32)
            cvh = _z_tile(zh_refs, 8 + q).astype(F32) * _z_tile(zh_refs, 12 + q).astype(F32) * keep_hist
            cc, _, _ = _conv_taps(jnp.concatenate([cvh, cv], axis=0), cv, cw_ref, lanes, HALO)
            u.append((_z_tile(z_refs, 4 + q).astype(F32) * cc).astype(BF16))
        y_conv = _dot(jnp.concatenate(u, axis=1), wco_ref[...])
        ypb, ycb = y_pool.astype(BF16), y_conv.astype(BF16)
        yp_ref[...] = ypb
        yc_ref[...] = ycb
        merged = []
        for q in range(NG):
            lanes = slice(q * CG, (q + 1) * CG)
            sp = jax.nn.sigmoid(_z_tile(z_refs, 16 + q).astype(F32))
            sc = jax.nn.sigmoid(_z_tile(z_refs, 20 + q).astype(F32))
            merged.append((sp * ypb[:, lanes].astype(F32) + sc * ycb[:, lanes].astype(F32)).astype(BF16))
        x1_ref[...] = x_ref[...] + _dot(jnp.concatenate(merged, axis=1), wo_ref[...])

    def hist(i):
        return jnp.maximum(i * hb - 1, 0)

    const2 = lambda i: (0, 0)
    return pl.pallas_call(
        body, name="fwd_mix", grid=(t // tm,),
        in_specs=[pl.BlockSpec((NDEV, tm, CG), lambda i: (0, i, 0))] * 3
                 + [pl.BlockSpec((NDEV, HALO, CG), lambda i: (0, hist(i), 0))] * 3
                 + [pl.BlockSpec((tm, D), lambda i: (i, 0)),
                    pl.BlockSpec((NG, CG, CG), lambda i: (0, 0, 0)), pl.BlockSpec((1, D), const2),
                    pl.BlockSpec((D, D), lambda i: (0, MIX_POOL_PROJ)), pl.BlockSpec((3, D), const2),
                    pl.BlockSpec((D, D), lambda i: (0, MIX_CONV_OUT)), pl.BlockSpec((D, D), lambda i: (0, MIX_O))],
        out_specs=[pl.BlockSpec((tm, D), lambda i: (i, 0))] * 3,
        out_shape=[jax.ShapeDtypeStruct((t, D), F32), jax.ShapeDtypeStruct((t, D), BF16),
                   jax.ShapeDtypeStruct((t, D), BF16)],
        compiler_params=_cparams(1, VMEM_BIG),
    )(*zs, *zs, x, pool_w, pool_scale, wmix, conv_w, wmix, wmix)


def fwd_up(x1, g2, w_up_g, fcw, fcb, tm, seq):
    t = x1.shape[0]
    tps = seq // tm

    def body(x1_ref, g2_ref, wup_ref, fcw_ref, fcb_ref, up_ref, pre_ref, act_ref, actt_ref, h2t_ref, hist_ref):
        i = pl.program_id(0)
        keep_hist = jnp.where(i % tps == 0, 0.0, 1.0)

        @pl.when(i == 0)
        def _():
            hist_ref[...] = jnp.zeros_like(hist_ref)

        x1v = x1_ref[...]
        h2 = (x1v * _rms_inv(x1v) * g2_ref[...]).astype(BF16)
        h2t_ref[...] = h2.T
        lanes = slice(0, SH_UP)
        for c in range(NCH):
            conv = []
            for s in range(2):
                ub = _dot(h2, wup_ref[s, c]).astype(BF16)
                up_ref[s, c] = ub
                uf = ub.astype(F32)
                ext = jnp.concatenate([hist_ref[s, c] * keep_hist, uf], axis=0)
                hist_ref[s, c] = uf[tm - 8:]
                cc, _, _ = _conv_taps(ext, uf, fcw_ref.at[s, c], lanes, 8)
                conv.append(cc + fcb_ref[s, c])
                pre_ref[s, c] = conv[s].astype(BF16)
            a = (conv[0] * jax.nn.sigmoid(conv[0]) * conv[1]).astype(BF16)
            act_ref[c] = a
            actt_ref[c] = a.T

    tile = lambda i: (i, 0)
    const2 = lambda i: (0, 0)
    whole = lambda i: (0, 0, 0, 0)
    chunks = pl.BlockSpec((2, NCH, tm, SH_UP), lambda i: (0, 0, i, 0))
    return pl.pallas_call(
        body, name="fwd_up", grid=(t // tm,),
        in_specs=[pl.BlockSpec((tm, D), tile), pl.BlockSpec((1, D), const2),
                  pl.BlockSpec((2, NCH, D, SH_UP), whole), pl.BlockSpec((2, NCH, 3, SH_UP), whole),
                  pl.BlockSpec((2, NCH, 1, SH_UP), whole)],
        out_specs=[chunks, chunks, pl.BlockSpec((NCH, tm, SH_UP), lambda i: (0, i, 0)),
                   pl.BlockSpec((NCH, SH_UP, tm), lambda i: (0, 0, i)), pl.BlockSpec((D, tm), lambda i: (0, i))],
        out_shape=[jax.ShapeDtypeStruct((2, NCH, t, SH_UP), BF16), jax.ShapeDtypeStruct((2, NCH, t, SH_UP), BF16),
                   jax.ShapeDtypeStruct((NCH, t, SH_UP), BF16), jax.ShapeDtypeStruct((NCH, SH_UP, t), BF16),
                   jax.ShapeDtypeStruct((D, t), BF16)],
        scratch_shapes=[pltpu.VMEM((2, NCH, 8, SH_UP), F32)],
        compiler_params=_cparams(1, VMEM_BIG),
    )(x1, g2, w_up_g.reshape(2, NCH, D, SH_UP), fcw.reshape(2, NCH, 3, SH_UP), fcb.reshape(2, NCH, 1, SH_UP))


def fwd_down(x1, act, w_dn, gf, tgt, tm):
    t = x1.shape[0]

    def body(x1_ref, act_ref, wdn_ref, gf_ref, tgt_ref, dx2_ref, dx2b_ref, vec_ref):
        @pl.when(pl.program_id(0) == 0)
        def _():
            vec_ref[...] = jnp.zeros_like(vec_ref)

        d = None
        for c in range(NCH):
            part = _dot(act_ref[c], wdn_ref[c])
            d = part if d is None else d + part
        x2 = x1_ref[...] + d
        inv3 = _rms_inv(x2)
        xn = x2 * inv3
        diff = xn * gf_ref[...] - tgt_ref[...]
        dy = diff * (1.0 / D)
        vec_ref[0:1, :] += jnp.sum(dy * xn, axis=0, keepdims=True)
        vec_ref[1:2, :] += 0.5 * jnp.sum(jnp.mean(diff * diff, axis=-1))
        dx2 = _rms_bwd(dy, xn, inv3, gf_ref[...])
        dx2_ref[...] = dx2
        dx2b_ref[...] = dx2.astype(BF16)

    tile = lambda i: (i, 0)
    const2 = lambda i: (0, 0)
    return pl.pallas_call(
        body, name="fwd_down", grid=(t // tm,),
        in_specs=[pl.BlockSpec((tm, D), tile), pl.BlockSpec((NCH, tm, SH_UP), lambda i: (0, i, 0)),
                  pl.BlockSpec((NCH, SH_UP, D), lambda i: (0, 0, 0)), pl.BlockSpec((1, D), const2),
                  pl.BlockSpec((tm, D), tile)],
        out_specs=[pl.BlockSpec((tm, D), tile), pl.BlockSpec((tm, D), tile), pl.BlockSpec((8, D), const2)],
        out_shape=[jax.ShapeDtypeStruct((t, D), F32), jax.ShapeDtypeStruct((t, D), BF16),
                   jax.ShapeDtypeStruct((8, D), F32)],
        compiler_params=_cparams(1, VMEM_BIG),
    )(x1, act, w_dn, gf, tgt)


def bwd_ffn(dx2, x1, g2, up, pre, w_up_g, fcw, w_dn, tm, seq):
    t = x1.shape[0]
    nt = t // tm
    tps = seq // tm

    def body(dx2_ref, x1_ref, g2_ref, up_ref, pre_ref, wup_ref, fcw_ref, wdn_ref,
             dup_ref, dx1_ref, gvec_ref, gn_ref, carry_ref):
        i = pl.program_id(0)
        it = (nt - 1 - i) % tps
        keep_next = jnp.where(it == tps - 1, 0.0, 1.0)

        @pl.when(i == 0)
        def _():
            gvec_ref[...] = jnp.zeros_like(gvec_ref)
            gn_ref[...] = jnp.zeros_like(gn_ref)
            carry_ref[...] = jnp.zeros_like(carry_ref)

        dx2v = dx2_ref[...]
        dxb = dx2v.astype(BF16)
        lanes = slice(0, SH_UP)
        dh2 = None
        for c in range(NCH):
            pre = [pre_ref[s, c].astype(F32) for s in range(2)]
            sg = jax.nn.sigmoid(pre[0])
            dact = _dot_nt(dxb, wdn_ref[c])
            dpre = [dact * pre[1] * (sg * (1.0 + pre[0] * (1.0 - sg))), dact * (pre[0] * sg)]
            for s in range(2):
                dc = dpre[s]
                ext = jnp.concatenate([dc, carry_ref[s, c] * keep_next], axis=0)
                carry_ref[s, c] = dc[:8]
                shifted = (_shift_up(ext, 2, tm), _shift_up(ext, 1, tm), dc)
                uf = up_ref[s, c].astype(F32)
                gvec_ref[s, c, 0:1, lanes] += jnp.sum(dc, axis=0, keepdims=True)
                for tap in range(3):
                    gvec_ref[s, c, tap + 1:tap + 2, lanes] += jnp.sum(shifted[tap] * uf, axis=0, keepdims=True)
                w = fcw_ref.at[s, c]
                du = w[2:3, :] * dc + w[1:2, :] * shifted[1] + w[0:1, :] * shifted[0]
                dub = du.astype(BF16)
                dup_ref[s, c] = dub
                part = _dot_nt(dub, wup_ref[s, c])
                dh2 = part if dh2 is None else dh2 + part
        x1v = x1_ref[...]
        inv2 = _rms_inv(x1v)
        xn = x1v * inv2
        gn_ref[0:1, :] += jnp.sum(dh2 * xn, axis=0, keepdims=True)
        dx1_ref[...] = dx2v + _rms_bwd(dh2, xn, inv2, g2_ref[...])

    rev = lambda i: (nt - 1 - i, 0)
    const2 = lambda i: (0, 0)
    whole = lambda i: (0, 0, 0, 0)
    chunks = pl.BlockSpec((2, NCH, tm, SH_UP), lambda i: (0, 0, nt - 1 - i, 0))
    return pl.pallas_call(
        body, name="bwd_ffn", grid=(nt,),
        in_specs=[pl.BlockSpec((tm, D), rev), pl.BlockSpec((tm, D), rev), pl.BlockSpec((1, D), const2),
                  chunks, chunks, pl.BlockSpec((2, NCH, D, SH_UP), whole), pl.BlockSpec((2, NCH, 3, SH_UP), whole),
                  pl.BlockSpec((NCH, SH_UP, D), lambda i: (0, 0, 0))],
        out_specs=[chunks, pl.BlockSpec((tm, D), rev), pl.BlockSpec((2, NCH, 8, D), whole),
                   pl.BlockSpec((8, D), const2)],
        out_shape=[jax.ShapeDtypeStruct((2, NCH, t, SH_UP), BF16), jax.ShapeDtypeStruct((t, D), F32),
                   jax.ShapeDtypeStruct((2, NCH, 8, D), F32), jax.ShapeDtypeStruct((8, D), F32)],
        scratch_shapes=[pltpu.VMEM((2, NCH, 8, SH_UP), F32)],
        compiler_params=_cparams(1, VMEM_BIG),
    )(dx2, x1, g2, up, pre, w_up_g.reshape(2, NCH, D, SH_UP), fcw.reshape(2, NCH, 3, SH_UP), w_dn)


def bwd_mix(dx1, zs, y_pool, y_conv, pool_w, pool_scale, conv_w, wmix, tm, seq):
    t = dx1.shape[0]
    nt = t // tm
    tps = seq // tm
    hb = tm // HALO

    def body(da_ref, z0_ref, z1_ref, z2_ref, zh0_ref, zh1_ref, zh2_ref, yp_ref, yc_ref, pw_ref, ps_ref, wpp_ref,
             cw_ref, wco_ref, wo_ref,
             dz_ref, mg_ref, p2_ref, u_ref, dyp_ref, dyc_ref, p_ref, dpw_ref, gvec_ref, cp_ref, cc_ref):
        z_refs, zh_refs = (z0_ref, z1_ref, z2_ref), (zh0_ref, zh1_ref, zh2_ref)
        i = pl.program_id(0)
        it = (nt - 1 - i) % tps
        keep_hist = jnp.where(it == 0, 0.0, 1.0)
        keep_next = jnp.where(it == tps - 1, 0.0, 1.0)
        pos = it * tm + lax.broadcasted_iota(jnp.int32, (tm, 1), 0)

        @pl.when(i == 0)
        def _():
            gvec_ref[...] = jnp.zeros_like(gvec_ref)
            cp_ref[...] = jnp.zeros_like(cp_ref)
            cc_ref[...] = jnp.zeros_like(cc_ref)

        dm = _dot_nt(da_ref[...].astype(BF16), wo_ref[...])
        merged, dyp, dyc = [], [], []
        for q in range(NG):
            lanes = slice(q * CG, (q + 1) * CG)
            sp = jax.nn.sigmoid(_z_tile(z_refs, 16 + q).astype(F32))
            sc = jax.nn.sigmoid(_z_tile(z_refs, 20 + q).astype(F32))
            yp = yp_ref[:, lanes].astype(F32)
            yc = yc_ref[:, lanes].astype(F32)
            dmq = dm[:, lanes]
            merged.append((sp * yp + sc * yc).astype(BF16))
            dyp.append((dmq * sp).astype(BF16))
            dyc.append((dmq * sc).astype(BF16))
            dz_ref[16 + q] = (dmq * yp * (sp * (1.0 - sp))).astype(BF16)
            dz_ref[20 + q] = (dmq * yc * (sc * (1.0 - sc))).astype(BF16)
        mg_ref[...] = jnp.concatenate(merged, axis=1)
        dypb = jnp.concatenate(dyp, axis=1)
        dycb = jnp.concatenate(dyc, axis=1)
        dyp_ref[...] = dypb
        dyc_ref[...] = dycb

        dp2 = _dot_nt(dypb, wpp_ref[...])
        p2 = []
        for g, win in enumerate(WINS):
            lanes = slice(g * CG, (g + 1) * CG)
            cnt = jnp.minimum(pos + 1, win).astype(F32)
            p = _pool_tile(_z_tile(z_refs, g), _z_tile(zh_refs, g), win, keep_hist, cnt)
            pb = p.astype(BF16)
            p_ref[g] = pb
            pw = _dot(pb, pw_ref[g])
            p2.append((pw * ps_ref[:, lanes]).astype(BF16))
            dp2g = dp2[:, lanes]
            gvec_ref[0:1, lanes] += jnp.sum(dp2g * pw, axis=0, keepdims=True)
            dpwb = (dp2g * ps_ref[:, lanes]).astype(BF16)
            dpw_ref[g] = dpwb
            dp = _dot_nt(dpwb, pw_ref[g])
            qv = dp / cnt
            ext = jnp.concatenate([qv, cp_ref[g] * keep_next], axis=0)
            cp_ref[g] = qv[:HALO]
            n = tm + HALO
            s, sh = ext, 1
            while sh < win:
                s = s + pltpu.roll(s, n - sh, 0)
                sh *= 2
            dz_ref[g] = (s[:tm] - dp).astype(BF16)
        p2_ref[...] = jnp.concatenate(p2, axis=1)

        du = _dot_nt(dycb, wco_ref[...])
        u = []
        for q in range(NG):
            lanes = slice(q * CG, (q + 1) * CG)
            zb = _z_tile(z_refs, 4 + q).astype(F32)
            zc = _z_tile(z_refs, 8 + q).astype(F32)
            zv = _z_tile(z_refs, 12 + q).astype(F32)
            cv = zc * zv
            cvh = _z_tile(zh_refs, 8 + q).astype(F32) * _z_tile(zh_refs, 12 + q).astype(F32) * keep_hist
            cc, cv1, cv2 = _conv_taps(jnp.concatenate([cvh, cv], axis=0), cv, cw_ref, lanes, HALO)
            u.append((zb * cc).astype(BF16))
            duq = du[:, lanes]
            dz_ref[4 + q] = (duq * cc).astype(BF16)
            dcc = duq * zb
            for tap, src in enumerate((cv2, cv1, cv)):
                gvec_ref[tap + 1:tap + 2, lanes] += jnp.sum(dcc * src, axis=0, keepdims=True)
            ext = jnp.concatenate([dcc, cc_ref[:, lanes] * keep_next], axis=0)
            cc_ref[:, lanes] = dcc[:8]
            dcv = (cw_ref[2:3, lanes] * dcc + cw_ref[1:2, lanes] * _shift_up(ext, 1, tm)
                   + cw_ref[0:1, lanes] * _shift_up(ext, 2, tm))
            dz_ref[8 + q] = (dcv * zv).astype(BF16)
            dz_ref[12 + q] = (dcv * zc).astype(BF16)
        u_ref[...] = jnp.concatenate(u, axis=1)

    def hist(i):
        return jnp.maximum((nt - 1 - i) * hb - 1, 0)

    rev = lambda i: (nt - 1 - i, 0)
    rev3 = lambda i: (0, nt - 1 - i, 0)
    const2 = lambda i: (0, 0)
    tok = jax.ShapeDtypeStruct((t, D), BF16)
    grp = jax.ShapeDtypeStruct((NG, t, CG), BF16)
    return pl.pallas_call(
        body, name="bwd_mix", grid=(nt,),
        in_specs=[pl.BlockSpec((tm, D), rev)] + [pl.BlockSpec((NDEV, tm, CG), rev3)] * 3
                 + [pl.BlockSpec((NDEV, HALO, CG), lambda i: (0, hist(i), 0))] * 3
                 + [pl.BlockSpec((tm, D), rev), pl.BlockSpec((tm, D), rev),
                    pl.BlockSpec((NG, CG, CG), lambda i: (0, 0, 0)), pl.BlockSpec((1, D), const2),
                    pl.BlockSpec((D, D), lambda i: (0, MIX_POOL_PROJ)), pl.BlockSpec((3, D), const2),
                    pl.BlockSpec((D, D), lambda i: (0, MIX_CONV_OUT)), pl.BlockSpec((D, D), lambda i: (0, MIX_O))],
        out_specs=[pl.BlockSpec((NZT, tm, CG), rev3)] + [pl.BlockSpec((tm, D), rev)] * 5
                  + [pl.BlockSpec((NG, tm, CG), rev3)] * 2 + [pl.BlockSpec((8, D), const2)],
        out_shape=[jax.ShapeDtypeStruct((NZT, t, CG), BF16), tok, tok, tok, tok, tok, grp, grp,
                   jax.ShapeDtypeStruct((8, D), F32)],
        scratch_shapes=[pltpu.VMEM((NG, HALO, CG), F32), pltpu.VMEM((8, D), F32)],
        compiler_params=_cparams(1, VMEM_BIG),
    )(dx1, *zs, *zs, y_pool, y_conv, pool_w, pool_scale, wmix, conv_w, wmix, wmix)


def bwd_in(dz, w_in_pieces, dx1, x, g1, tm):
    t = x.shape[0]

    def body(dz_ref, w0_ref, w1_ref, w2_ref, dx1_ref, x_ref, g_ref, gx_ref, gn_ref):
        @pl.when(pl.program_id(0) == 0)
        def _():
            gn_ref[...] = jnp.zeros_like(gn_ref)

        dh = None
        for j in range(NDEV):
            for q, w_ref in enumerate((w0_ref, w1_ref, w2_ref)):
                part = _dot_nt(dz_ref[3 * j + q], w_ref[j])
                dh = part if dh is None else dh + part
        xv = x_ref[...]
        inv = _rms_inv(xv)
        xn = xv * inv
        gn_ref[0:1, :] += jnp.sum(dh * xn, axis=0, keepdims=True)
        gx_ref[...] = dx1_ref[...] + _rms_bwd(dh, xn, inv, g_ref[...])

    tile = lambda i: (i, 0)
    return pl.pallas_call(
        body, name="bwd_in", grid=(t // tm,),
        in_specs=[pl.BlockSpec((NZT, tm, CG), lambda i: (0, i, 0))]
                 + [pl.BlockSpec((NDEV, D, CG), lambda i: (0, 0, 0))] * 3
                 + [pl.BlockSpec((tm, D), tile), pl.BlockSpec((tm, D), tile), pl.BlockSpec((1, D), lambda i: (0, 0))],
        out_specs=[pl.BlockSpec((tm, D), tile), pl.BlockSpec((8, D), lambda i: (0, 0))],
        out_shape=[jax.ShapeDtypeStruct((t, D), F32), jax.ShapeDtypeStruct((8, D), F32)],
        compiler_params=_cparams(1, VMEM_BIG),
    )(dz, *w_in_pieces, dx1, x, g1)


def _slot(j):
    return j % 2, j // 2


def wgrad_cols(at, b, q, name, tk):
    m, t = at.shape
    width = b.shape[3]

    def body(a_ref, b_ref, o_ref):
        @pl.when(pl.program_id(1) == 0)
        def _():
            o_ref[...] = jnp.zeros_like(o_ref)

        o_ref[...] += _dot(a_ref[...], b_ref[...])

    return pl.pallas_call(
        body, name=name, grid=(NDEV, t // tk),
        in_specs=[pl.BlockSpec((m, tk), lambda j, k: (0, k)),
                  pl.BlockSpec((None, None, tk, width), lambda j, k: (j, q, k, 0))],
        out_specs=pl.BlockSpec((None, None, m, width), lambda j, k: (j % 2, j // 2, 0, 0)),
        out_shape=jax.ShapeDtypeStruct((2, 4, m, width), F32),
        compiler_params=_cparams(2, VMEM_BIG),
    )(at, b)


def wgrad_cols_resident(at, b, q, name, tk):
    m, t = at.shape
    width = b.shape[3]

    def body(a_ref, b_ref, o_ref):
        k, j = pl.program_id(0), pl.program_id(1)

        @pl.when((k == 0) & (j == 0))
        def _():
            o_ref[...] = jnp.zeros_like(o_ref)

        o_ref[j % 2, j // 2] += _dot(a_ref[...], b_ref[...])

    return pl.pallas_call(
        body, name=name, grid=(t // tk, NDEV),
        in_specs=[pl.BlockSpec((m, tk), lambda k, j: (0, k)),
                  pl.BlockSpec((None, None, tk, width), lambda k, j: (j, q, k, 0))],
        out_specs=pl.BlockSpec((2, 4, m, width), lambda k, j: (0, 0, 0, 0)),
        out_shape=jax.ShapeDtypeStruct((2, 4, m, width), F32),
        compiler_params=_cparams(2, VMEM_BIG),
    )(at, b)


def wgrad_down(actt, dx2b):
    t = dx2b.shape[0]

    def body(a_ref, b_ref, o_ref):
        r = _dot(a_ref[...], b_ref[...])
        o_ref[0] = r[:SH_DN]
        o_ref[1] = r[SH_DN:]

    return pl.pallas_call(
        body, name="wgrad_down", grid=(NCH,),
        in_specs=[pl.BlockSpec((None, SH_UP, t), lambda k: (k, 0, 0)), pl.BlockSpec((t, D), lambda k: (0, 0))],
        out_specs=pl.BlockSpec((2, None, SH_DN, D), lambda k: (0, k, 0, 0)),
        out_shape=jax.ShapeDtypeStruct((2, 4, SH_DN, D), F32),
        compiler_params=_cparams(1, VMEM_BIG),
    )(actt, dx2b)


def wgrad_square(a, b, name, tk):
    t = a.shape[0]

    def body(a_ref, b_ref, o_ref, acc_ref):
        kt = pl.program_id(0)

        @pl.when(kt == 0)
        def _():
            acc_ref[...] = jnp.zeros_like(acc_ref)

        acc_ref[...] += _dot_tn(a_ref[...], b_ref[...].astype(BF16))

        @pl.when(kt == pl.num_programs(0) - 1)
        def _():
            for j in range(NDEV):
                cc, xy = _slot(j)
                o_ref[cc, xy] = acc_ref[j * 128:(j + 1) * 128]

    return pl.pallas_call(
        body, name=name, grid=(t // tk,),
        in_specs=[pl.BlockSpec((tk, D), lambda k: (k, 0)), pl.BlockSpec((tk, D), lambda k: (k, 0))],
        out_specs=pl.BlockSpec((2, 4, 128, D), lambda k: (0, 0, 0, 0)),
        out_shape=jax.ShapeDtypeStruct((2, 4, 128, D), F32),
        scratch_shapes=[pltpu.VMEM((D, D), F32)],
        compiler_params=_cparams(1, VMEM_BIG),
    )(a, b)


def wgrad_pool(p, dpw, tk):
    t = p.shape[1]

    def body(a_ref, b_ref, o_ref):
        @pl.when(pl.program_id(0) == 0)
        def _():
            o_ref[...] = jnp.zeros_like(o_ref)

        for g in range(NG):
            o_ref[g] += _dot_tn(a_ref[g], b_ref[g])

    return pl.pallas_call(
        body, name="wgrad_pool", grid=(t // tk,),
        in_specs=[pl.BlockSpec((NG, tk, CG), lambda k: (0, k, 0))] * 2,
        out_specs=pl.BlockSpec((NG, CG, CG), lambda k: (0, 0, 0)),
        out_shape=jax.ShapeDtypeStruct((NG, CG, CG), F32),
        compiler_params=_cparams(1, VMEM_BIG),
    )(p, dpw)


def _adamw(w, g, m, v):
    m = ADAM_B1 * m + (1.0 - ADAM_B1) * g
    v = ADAM_B2 * v + (1.0 - ADAM_B2) * (g * g)
    m_hat = m / (1.0 - ADAM_B1 ** ADAM_STEP)
    v_hat = v / (1.0 - ADAM_B2 ** ADAM_STEP)
    delta = -ADAM_LR * (m_hat / (jnp.sqrt(v_hat) + ADAM_EPS) + ADAM_WD * w)
    return delta, m, v


def _row_block(r):
    return 512 if r % 512 == 0 else r


def chip_partial(place, g, from_sibling, name):
    _, _, r, c = g.shape

    def body(place_ref, g_ref, s_ref, o_ref):
        o_ref[...] = (g_ref[...] + s_ref[...]).astype(BF16)

    return pl.pallas_call(
        body, name=name,
        grid_spec=pltpu.PrefetchScalarGridSpec(
            num_scalar_prefetch=1, grid=(3,),
            in_specs=[pl.BlockSpec((None, None, r, c), lambda k, pr: (pr[0], pr[1] ^ (k + 1), 0, 0)),
                      pl.BlockSpec((None, r, c), lambda k, pr: (pr[1] ^ (k + 1), 0, 0))],
            out_specs=pl.BlockSpec((None, r, c), lambda k, pr: (pr[1] ^ (k + 1), 0, 0))),
        out_shape=jax.ShapeDtypeStruct((4, r, c), BF16),
        compiler_params=_cparams(1, VMEM_BIG),
    )(place, g, from_sibling)


def finish_adamw(place, gs, from_sibling, from_chips, w, m, v, name):
    n = len(gs)
    r = gs[0].shape[2]
    widths = [g.shape[3] for g in gs]
    c = sum(widths)
    br = _row_block(r)

    def body(place_ref, *refs):
        g_refs, s_refs, c_refs = refs[:n], refs[n:2 * n], refs[2 * n:5 * n]
        w_ref, m_ref, v_ref, og_ref, od_ref, om_ref, ov_ref = refs[5 * n:]
        cols = []
        for q in range(n):
            grad = g_refs[q][...] + s_refs[q][...]
            for k in range(3):
                grad = grad + c_refs[3 * q + k][...].astype(F32)
            cols.append(grad)
        grad = cols[0] if n == 1 else jnp.concatenate(cols, axis=1)
        og_ref[...] = grad
        od_ref[...], om_ref[...], ov_ref[...] = _adamw(w_ref[...], grad, m_ref[...], v_ref[...])

    def other(k, cq):
        return pl.BlockSpec((None, br, cq), lambda i, pr: (pr[1] ^ k, i, 0))

    row = pl.BlockSpec((br, c), lambda i, pr: (i, 0))
    out = jax.ShapeDtypeStruct((r, c), F32)
    in_specs = [pl.BlockSpec((None, None, br, cq), lambda i, pr: (pr[0], pr[1], i, 0)) for cq in widths]
    in_specs += [pl.BlockSpec((None, br, cq), lambda i, pr: (pr[1], i, 0)) for cq in widths]
    in_specs += [other(k, cq) for cq in widths for k in (1, 2, 3)]
    return pl.pallas_call(
        body, name=name,
        grid_spec=pltpu.PrefetchScalarGridSpec(
            num_scalar_prefetch=1, grid=(r // br,), in_specs=in_specs + [row, row, row], out_specs=[row] * 4),
        out_shape=[out] * 4,
        compiler_params=_cparams(1, VMEM_BIG),
    )(place, *gs, *from_sibling, *[fc for fc in from_chips for _ in range(3)], w, m, v)


def adamw_small(items):
    n = len(items)

    def body(*refs):
        ins, outs = refs[:4 * n], refs[4 * n:]
        for i in range(n):
            w, g, m, v = (r[...] for r in ins[4 * i:4 * i + 4])
            outs[3 * i][...], outs[3 * i + 1][...], outs[3 * i + 2][...] = _adamw(w, g, m, v)

    out = [jax.ShapeDtypeStruct(it[0].shape, F32) for it in items for _ in range(3)]
    res = pl.pallas_call(body, name="adamw_small", out_shape=out)(*[a for it in items for a in it])
    return [res[3 * i:3 * i + 3] for i in range(n)]


def kernel(x, norm_mix, w_in, pool_w, pool_scale, w_pool_proj, conv_w, w_conv_out, w_o, norm_ffn, w_up, ffn_conv_w, ffn_conv_b, w_down, norm_final, loss_target, m_norm_mix, m_w_in, m_pool_w, m_pool_scale, m_w_pool_proj, m_conv_w, m_w_conv_out, m_w_o, m_norm_ffn, m_w_up, m_ffn_conv_w, m_ffn_conv_b, m_w_down, m_norm_final, v_norm_mix, v_w_in, v_pool_w, v_pool_scale, v_w_pool_proj, v_conv_w, v_w_conv_out, v_w_o, v_norm_ffn, v_w_up, v_ffn_conv_w, v_ffn_conv_b, v_w_down, v_norm_final):
    nb, seq, _ = x.shape
    t = nb * seq
    tm_in = min(TM_IN, t)
    tm_mix = min(TM_MIX, seq)
    tm_ffn = min(TM_FFN, seq)
    tk = min(TK_WGRAD, t)
    xt = x.reshape(t, D)
    tgt = loss_target.reshape(t, D)
    xi, yi, ci = _pos()
    me = 4 * xi + 2 * yi + ci
    place = jnp.stack([ci, 2 * xi + yi]).astype(jnp.int32)

    tie = lax.optimization_barrier
    w_in_b = w_in[0].astype(BF16)
    w_in_g = [all_gather_blocks([w_in_b[:, q * CG:(q + 1) * CG]], f"all_gather_w_in_{q}", 0)[0] for q in range(3)]
    taps = (jnp.pad(conv_w[0], ((0, 5), (0, D - 128))) + jnp.pad(ffn_conv_w[0], ((3, 2), (0, D - SH_UP))))
    taps_g = _exchange_small(taps, False, "all_gather_taps")
    mix_shard = jnp.concatenate(
        [w_pool_proj[0], w_conv_out[0], w_o[0], pool_w[0].reshape(NG * 32, CG)], axis=1).astype(BF16)
    mix_shard, taps_g = tie((mix_shard, taps_g))
    wmix_g, = all_gather_blocks([mix_shard], "all_gather_w_mix", 0)
    ffn_shards, w_in_g[0] = tie(([w_up[0].astype(BF16), w_down[0].astype(BF16)], w_in_g[0]))
    w_up_g, = all_gather_blocks(ffn_shards[:1], "all_gather_w_up", 0)
    w_dn_g, = all_gather_blocks(ffn_shards[1:], "all_gather_w_down", 0)
    w_dn_f = w_dn_g.reshape(NCH, SH_UP, D)
    conv_w_f = taps_g[:, 0:3, :128].transpose(1, 0, 2).reshape(3, D)
    fcw_f = taps_g[:, 3:6, :SH_UP]
    fcb_f = ffn_conv_b.reshape(NDEV, 1, SH_UP)
    gfin = norm_final.reshape(1, D)

    zs, h1 = fwd_in(xt, norm_mix, w_in_g, tm_in)
    wmix_g, zs = tie((wmix_g, zs))
    wmix = wmix_g.reshape(D, MIX_COLS)
    pool_w_f = wmix_g[:, :, 3 * D:].reshape(NDEV, NG, 32, CG).transpose(1, 0, 2, 3).reshape(NG, CG, CG)
    x1, y_pool, y_conv = fwd_mix(zs, xt, pool_w_f, pool_scale, conv_w_f, wmix, tm_mix, seq)
    up, pre, act_tok, act, h2 = fwd_up(x1, norm_ffn, w_up_g, fcw_f, fcb_f, tm_ffn, seq)
    dx2, dx2b, ffn_vec = fwd_down(x1, act_tok, w_dn_f, gfin, tgt, min(TM_IN, t))

    def to_sibling(full, tag):
        return reduce_scatter_d2d(full, "reduce_scatter_d2d_" + tag, 1)

    def partials(full, from_sib, names):
        return [chip_partial(place, g, s, "chip_partial_" + nm) for g, s, nm in zip(full, from_sib, names)]

    def to_chips(parts, tag):
        return reduce_scatter_ici(parts, "reduce_scatter_ici_" + tag, 2)

    def finish(nm, gs, from_sib, from_chips, wmv):
        w, m, v = wmv
        rc = (gs[0].shape[2], sum(g.shape[3] for g in gs))
        outs = finish_adamw(place, gs, from_sib, from_chips, w.reshape(rc), m.reshape(rc), v.reshape(rc), "adamw_" + nm)
        return [o.reshape(w.shape) for o in outs]

    def after(x, dep):
        return tie((x, dep))[0]

    big = {}
    d_up, dx1, g_ffn_vec, g_nffn = bwd_ffn(dx2, x1, norm_ffn, up, pre, w_up_g, fcw_f, w_dn_f, tm_ffn, seq)
    gw_up = wgrad_cols(h2, d_up.reshape(NDEV, 1, t, SH_UP), 0, "wgrad_up", t)
    sib_up = to_sibling([gw_up], "w_up")
    gw_dn = wgrad_down(act, after(dx2b, gw_up))
    sib_dn = to_sibling([after(gw_dn, sib_up)], "w_down")
    dx1, part_up = tie((dx1, partials([gw_up], sib_up, ["w_up"])))
    chips_up = to_chips(part_up, "w_up")
    dz, merged, p2, u, dyp, dyc, p, dpw, g_mix_vec = bwd_mix(
        dx1, zs, y_pool, y_conv, pool_w_f, pool_scale, conv_w_f, wmix, tm_mix, seq)
    merged, part_dn = tie((merged, partials([gw_dn], sib_dn, ["w_down"])))
    chips_dn = to_chips(part_dn, "w_down")
    gw_o = wgrad_square(merged, dx1, "wgrad_o", tk)
    gw_pp = wgrad_square(p2, dyp, "wgrad_pool_proj", tk)
    gw_co = wgrad_square(u, dyc, "wgrad_conv_out", tk)
    gw_pool = wgrad_pool(p, dpw, tk).reshape(NG, 4, 2, 32, CG).transpose(2, 1, 0, 3, 4).reshape(2, 4, NG * 32, CG)
    dz8 = dz.reshape(NDEV, 3, t, CG)
    gw_in, sib_in, chips_in = [None] * 3, [None] * 3, [None] * 3
    sib_a = to_sibling(after([gw_o, gw_pp], (chips_up, gw_pool)), "mix_a")
    sib_b = to_sibling(after([gw_co, gw_pool], sib_a), "mix_b")
    gw_in[0] = wgrad_cols_resident(h1, dz8, 0, "wgrad_in_0", t)
    h1, part_a, part_b = tie((h1, partials([gw_o, gw_pp], sib_a, ["w_o", "w_pool_proj"]),
                              partials([gw_co, gw_pool], sib_b, ["w_conv_out", "pool_w"])))
    chips_a = to_chips(after(part_a, chips_dn), "mix_a")
    chips_b = to_chips(part_b, "mix_b")
    sib_in[0] = to_sibling(after([gw_in[0]], sib_b), "w_in_0")
    gw_in[1] = wgrad_cols_resident(h1, dz8, 1, "wgrad_in_1", t)
    h1, part_in0, gw_in[1] = tie((h1, partials([gw_in[0]], sib_in[0], ["w_in_0"]), gw_in[1]))
    chips_in[0] = to_chips(part_in0, "w_in_0")
    sib_in[1] = to_sibling(after([gw_in[1]], sib_in[0]), "w_in_1")
    h1, big["w_down"], big["w_up"] = tie((
        h1, finish("w_down", [gw_dn], sib_dn, chips_dn, (w_down, m_w_down, v_w_down)),
        finish("w_up", [gw_up], sib_up, chips_up, (w_up, m_w_up, v_w_up))))
    gw_in[2] = wgrad_cols_resident(h1, dz8, 2, "wgrad_in_2", t)
    sib_in[2] = to_sibling(after([gw_in[2]], (chips_a, chips_b, chips_in[0])), "w_in_2")
    dx1, part_in1, part_in2, big["w_o"], big["w_pool_proj"], big["w_conv_out"], big["pool_w"] = tie((
        dx1, partials([gw_in[1]], sib_in[1], ["w_in_1"]), partials([gw_in[2]], sib_in[2], ["w_in_2"]),
        finish("w_o", [gw_o], sib_a[:1], chips_a[:1], (w_o, m_w_o, v_w_o)),
        finish("w_pool_proj", [gw_pp], sib_a[1:], chips_a[1:], (w_pool_proj, m_w_pool_proj, v_w_pool_proj)),
        finish("w_conv_out", [gw_co], sib_b[:1], chips_b[:1], (w_conv_out, m_w_conv_out, v_w_conv_out)),
        finish("pool_w", [gw_pool], sib_b[1:], chips_b[1:], (pool_w, m_pool_w, v_pool_w))))
    chips_in[1] = to_chips(after(part_in1, sib_in[2]), "w_in_1")
    chips_in[2] = to_chips(part_in2, "w_in_2")
    small_g, = all_gather_blocks(
        [after(jnp.concatenate([g_mix_vec, g_nffn, ffn_vec, g_ffn_vec.reshape(8 * NDEV, D)], axis=0), sib_in[2])],
        "all_gather_small", 0)
    grad_x, g_nmix = bwd_in(dz, w_in_g, dx1, xt, norm_mix, min(TM_BWD_IN, t))
    grad_x, chips_in = tie((grad_x, chips_in))
    big["w_in"] = finish("w_in", gw_in, [s[0] for s in sib_in], [c[0] for c in chips_in], (w_in, m_w_in, v_w_in))

    red_n, red = _exchange_small(g_nmix, True, "all_reduce_small", gathered=small_g)
    g_norm_mix, g_pool_scale, g_norm_ffn = red_n[0:1], red[0:1], red[8:9]
    g_conv_w = lax.dynamic_slice(red, (1, me * 128), (3, 128))
    g_norm_final = red[16]
    loss = red[17, 0]
    g_fcb = red[24:].reshape(NDEV, 8, D)[:, 0, :SH_UP].reshape(1, FF2)
    g_fcw = lax.dynamic_slice(red, (25 + 8 * me, 0), (3, SH_UP))
    grads = {"norm_mix": g_norm_mix, "pool_scale": g_pool_scale, "norm_ffn": g_norm_ffn, "norm_final": g_norm_final,
             "ffn_conv_b": g_fcb, "conv_w": g_conv_w.reshape(1, 3, 128), "ffn_conv_w": g_fcw.reshape(1, 3, SH_UP)}
    small_wmv = {"norm_mix": (norm_mix, m_norm_mix, v_norm_mix), "pool_scale": (pool_scale, m_pool_scale, v_pool_scale),
                 "norm_ffn": (norm_ffn, m_norm_ffn, v_norm_ffn), "norm_final": (norm_final, m_norm_final, v_norm_final),
                 "ffn_conv_b": (ffn_conv_b, m_ffn_conv_b, v_ffn_conv_b), "conv_w": (conv_w, m_conv_w, v_conv_w),
                 "ffn_conv_w": (ffn_conv_w, m_ffn_conv_w, v_ffn_conv_w)}
    small_names = list(small_wmv)
    flat2 = lambda a: a.reshape(1, -1) if a.ndim == 1 else a
    small_out = adamw_small([(flat2(small_wmv[nm][0]), flat2(grads[nm]), flat2(small_wmv[nm][1]),
                              flat2(small_wmv[nm][2])) for nm in small_names])
    small = {nm: [o.reshape(small_wmv[nm][0].shape) for o in outs] for nm, outs in zip(small_names, small_out)}

    order = ["norm_mix", "w_in", "pool_w", "pool_scale", "w_pool_proj", "conv_w", "w_conv_out", "w_o", "norm_ffn",
             "w_up", "ffn_conv_w", "ffn_conv_b", "w_down", "norm_final"]
    out = [loss, grad_x.reshape(nb, seq, D)]
    out += [big[nm][0] if nm in big else grads[nm] for nm in order]
    for idx in range(3):
        out += [big[nm][idx + 1] if nm in big else small[nm][idx] for nm in order]
    return tuple(out)
```

```python
import jax
import jax.numpy as jnp
from jax import lax
from jax.experimental import pallas as pl
from jax.experimental.pallas import tpu as pltpu
from jax.experimental.pallas import tpu_sc as plsc

F32 = jnp.float32
BF16 = jnp.bfloat16

NDEV = 8
D = 1024
NG = 4
CG = 256
WINS = (2, 4, 8, 16)
DIN = 6 * D
SH_IN = DIN // NDEV
NZT = DIN // CG
FF2 = 5632
SH_UP = FF2 // NDEV
FF = FF2 // 2
NCH = 4
SH_DN = FF // NDEV
RMS_EPS = 1e-6
HALO = 16

ADAM_LR = 0.001
ADAM_B1 = 0.9
ADAM_B2 = 0.999
ADAM_EPS = 1e-08
ADAM_WD = 0.01
ADAM_STEP = 10

TM_IN = 512
TM_BWD_IN = 256
TM_MIX = 256
TM_FFN = 256
TK_WGRAD = 2048
MIX_POOL_PROJ, MIX_CONV_OUT, MIX_O = 0, 1, 2
MIX_COLS = 3 * D + CG
VMEM_BIG = 56 * 1024 * 1024
MESH = pl.DeviceIdType.MESH
ANY = pl.BlockSpec(memory_space=pl.ANY)


def _cparams(n_axes, vmem=None):
    return pltpu.CompilerParams(dimension_semantics=("arbitrary",) * n_axes, vmem_limit_bytes=vmem)


def _dot(a, b):
    return jnp.dot(a, b, preferred_element_type=F32)


def _dot_nt(a, b):
    return lax.dot_general(a, b, (((1,), (1,)), ((), ())), preferred_element_type=F32)


def _dot_tn(a, b):
    return lax.dot_general(a, b, (((0,), (0,)), ((), ())), preferred_element_type=F32)


def _shift_down(ext, s, lead):
    return pltpu.roll(ext, s, 0)[lead:]


def _shift_up(ext, s, tm):
    n = ext.shape[0]
    return pltpu.roll(ext, n - s, 0)[:tm]


def _rms_inv(x):
    return lax.rsqrt(jnp.mean(x * x, axis=-1, keepdims=True) + RMS_EPS)


def _rms_bwd(dh, xn, inv, g):
    dxn = dh * g
    return inv * (dxn - xn * jnp.mean(dxn * xn, axis=-1, keepdims=True))


def _pos():
    return lax.axis_index("x"), lax.axis_index("y"), lax.axis_index("c")


def _handshake(peers):
    barrier = pltpu.get_barrier_semaphore()
    for peer in peers:
        pl.semaphore_signal(barrier, inc=1, device_id=peer, device_id_type=MESH)
    pl.semaphore_wait(barrier, len(peers))


def _sequencer(body, out_type, n_sems, name, collective_id):
    return pl.kernel(
        body, out_type=out_type, mesh=plsc.ScalarSubcoreMesh(axis_name="sequencer", num_cores=1), name=name,
        scratch_types=[pltpu.SemaphoreType.DMA((n_sems,)), pltpu.SemaphoreType.DMA((n_sems,))],
        compiler_params=pltpu.CompilerParams(collective_id=collective_id))


def all_gather_blocks(shards, name, collective_id):
    n = len(shards)

    def body(*refs):
        ins, outs = refs[:n], refs[n:2 * n]
        send_sems, recv_sems = refs[2 * n:]
        x, y, c = _pos()
        me, sibling = (x, y, c), (x, y, 1 - c)
        first_chip, second_chip, diagonal = (x ^ (1 - c), y ^ c), (x ^ c, y ^ (1 - c)), (1 - x, 1 - y)
        first, second = (*first_chip, c), (*second_chip, c)
        _handshake([sibling, first, second])

        def copy(w, k, block, to, src=None):
            slot = outs[w].at[4 * block[0] + 2 * block[1] + block[2]]
            return pltpu.make_async_remote_copy(
                src_ref=slot if src is None else src, dst_ref=slot,
                send_sem=send_sems.at[8 * w + k], recv_sem=recv_sems.at[8 * w + k], device_id=to, device_id_type=MESH)

        mine, sent = [], []
        for w in range(n):
            m = pltpu.make_async_copy(ins[w], outs[w].at[4 * x + 2 * y + c], send_sems.at[8 * w + 7])
            m.start()
            mine.append(m)
            sent += [copy(w, k, me, to, src=ins[w]) for k, to in enumerate((sibling, first, second))]
        for cp in sent:
            cp.start()
        for k, chip in ((1, first_chip), (2, second_chip), (3, diagonal)):
            for w in range(n):
                copy(w, k, (*chip, c), me).wait_recv()
                onward = [copy(w, 3 + k, (*chip, c), sibling)] + ([copy(w, 3, (*chip, c), second)] if k == 1 else [])
                for cp in onward:
                    cp.start()
                sent += onward
        for w in range(n):
            copy(w, 0, sibling, me).wait_recv()
            for k, chip in ((4, second_chip), (5, first_chip), (6, diagonal)):
                copy(w, k, (*chip, 1 - c), me).wait_recv()
        for cp in sent:
            cp.wait_send()
        for m in mine:
            m.wait()

    out = [jax.ShapeDtypeStruct((NDEV,) + s.shape, s.dtype) for s in shards]
    return _sequencer(body, out, 8 * n, name, collective_id)(*shards)


def _exchange_small(v, reduce, name, gathered=None):
    rows = v.shape[0]

    def body(*refs):
        if gathered is None:
            v_ref, out_ref, slots, send_sems, recv_sems, local_sem = refs
        else:
            v_ref, g_ref, out_ref, gsum_ref, slots, send_sems, recv_sems, local_sem = refs
        x, y, c = _pos()
        me = 4 * x + 2 * y + c
        mine = pltpu.make_async_copy(v_ref, slots.at[me], local_sem)
        mine.start()
        offs = [(dx, dy, dc) for dx in (0, 1) for dy in (0, 1) for dc in (0, 1)][1:]

        def copy(k, src_slot, to):
            return pltpu.make_async_remote_copy(
                src_ref=v_ref, dst_ref=slots.at[src_slot], send_sem=send_sems.at[k], recv_sem=recv_sems.at[k],
                device_id=to, device_id_type=MESH)

        sends = []
        for k, (dx, dy, dc) in enumerate(offs):
            cp = copy(k, me, (x ^ dx, y ^ dy, c ^ dc))
            cp.start()
            sends.append(cp)
        for k, (dx, dy, dc) in enumerate(offs):
            copy(k, 4 * (x ^ dx) + 2 * (y ^ dy) + (c ^ dc), (x, y, c)).wait_recv()
        for cp in sends:
            cp.wait_send()
        mine.wait()
        if reduce:
            acc = slots[0]
            for d in range(1, NDEV):
                acc = acc + slots[d]
            out_ref[...] = acc
        else:
            out_ref[...] = slots[...]
        if gathered is not None:
            acc = g_ref[0]
            for d in range(1, NDEV):
                acc = acc + g_ref[d]
            gsum_ref[...] = acc

    vmem = pl.BlockSpec(memory_space=pltpu.VMEM)
    out = jax.ShapeDtypeStruct((rows, D) if reduce else (NDEV, rows, D), F32)
    args, out_shape, out_specs = [v], out, vmem
    if gathered is not None:
        args.append(gathered)
        out_shape, out_specs = [out, jax.ShapeDtypeStruct(gathered.shape[1:], F32)], [vmem, vmem]
    return pl.pallas_call(
        body, name=name, out_shape=out_shape, in_specs=[vmem] * len(args), out_specs=out_specs,
        scratch_shapes=[pltpu.VMEM((NDEV, rows, D), F32), pltpu.SemaphoreType.DMA((7,)),
                        pltpu.SemaphoreType.DMA((7,)), pltpu.SemaphoreType.DMA],
    )(*args)


def reduce_scatter_d2d(grads, name, collective_id):
    n = len(grads)

    def body(*refs):
        ins, outs = refs[:n], refs[n:2 * n]
        send_sems, recv_sems = refs[2 * n:]
        x, y, c = _pos()
        _handshake([(x, y, 1 - c)])
        cps = []
        for w in range(n):
            cp = pltpu.make_async_remote_copy(
                src_ref=ins[w].at[1 - c], dst_ref=outs[w], send_sem=send_sems.at[w], recv_sem=recv_sems.at[w],
                device_id=(x, y, 1 - c), device_id_type=MESH)
            cp.start()
            cps.append(cp)
        for cp in cps:
            cp.wait_recv()
        for cp in cps:
            cp.wait_send()

    out = [jax.ShapeDtypeStruct(g.shape[1:], F32) for g in grads]
    return _sequencer(body, out, n, name, collective_id)(*grads)


def reduce_scatter_ici(parts, name, collective_id):
    n = len(parts)

    def body(*refs):
        ins, outs = refs[:n], refs[n:2 * n]
        send_sems, recv_sems = refs[2 * n:]
        x, y, c = _pos()
        offs = [(1, 0), (0, 1), (1, 1)]
        _handshake([(x ^ dx, y ^ dy, c) for dx, dy in offs])
        cps = []
        for w in range(n):
            for k, (dx, dy) in enumerate(offs):
                ox, oy = x ^ dx, y ^ dy
                cp = pltpu.make_async_remote_copy(
                    src_ref=ins[w].at[2 * ox + oy], dst_ref=outs[w].at[2 * x + y],
                    send_sem=send_sems.at[3 * w + k], recv_sem=recv_sems.at[3 * w + k],
                    device_id=(ox, oy, c), device_id_type=MESH)
                cp.start()
                cps.append((cp, w, k, ox, oy))
        for cp, w, k, ox, oy in cps:
            pltpu.make_async_remote_copy(
                src_ref=ins[w].at[2 * ox + oy], dst_ref=outs[w].at[2 * ox + oy],
                send_sem=send_sems.at[3 * w + k], recv_sem=recv_sems.at[3 * w + k],
                device_id=(ox, oy, c), device_id_type=MESH).wait_recv()
        for cp, *_ in cps:
            cp.wait_send()

    out = [jax.ShapeDtypeStruct(p.shape, BF16) for p in parts]
    return _sequencer(body, out, 3 * n, name, collective_id)(*parts)


def fwd_in(x, g1, w_in_pieces, tm):
    t = x.shape[0]
    tile = lambda i: (i, 0)
    w_spec = pl.BlockSpec((NDEV, D, CG), lambda i: (0, 0, 0))
    z_spec = pl.BlockSpec((NDEV, tm, CG), lambda i: (0, i, 0))
    z_shape = jax.ShapeDtypeStruct((NDEV, t, CG), BF16)

    def cost(other_bytes, transcendentals):
        return pl.CostEstimate(flops=2 * t * D * NDEV * CG, transcendentals=transcendentals,
                               bytes_accessed=other_bytes + 2 * D * NDEV * CG + 2 * t * NDEV * CG)

    def first(x_ref, g_ref, w_ref, z_ref, h_ref, ht_ref):
        xf = x_ref[...]
        h = (xf * _rms_inv(xf) * g_ref[...]).astype(BF16)
        h_ref[...] = h
        ht_ref[...] = h.T
        for j in range(NDEV):
            z_ref[j] = _dot(h, w_ref[j]).astype(BF16)

    z0, h, ht = pl.pallas_call(
        first, name="fwd_in_0", grid=(t // tm,),
        in_specs=[pl.BlockSpec((tm, D), tile), pl.BlockSpec((1, D), lambda i: (0, 0)), w_spec],
        out_specs=[z_spec, pl.BlockSpec((tm, D), tile), pl.BlockSpec((D, tm), lambda i: (0, i))],
        out_shape=[z_shape, jax.ShapeDtypeStruct((t, D), BF16), jax.ShapeDtypeStruct((D, t), BF16)],
        compiler_params=_cparams(1, VMEM_BIG), cost_estimate=cost(8 * t * D + 4 * D, t),
    )(x, g1, w_in_pieces[0])
    zs = [z0]
    for q in (1, 2):
        h, zs[-1] = lax.optimization_barrier((h, zs[-1]))

        def later(h_ref, w_ref, z_ref):
            hb = h_ref[...]
            for j in range(NDEV):
                z_ref[j] = _dot(hb, w_ref[j]).astype(BF16)

        zs.append(pl.pallas_call(
            later, name=f"fwd_in_{q}", grid=(t // tm,),
            in_specs=[pl.BlockSpec((tm, D), tile), w_spec], out_specs=z_spec, out_shape=z_shape,
            compiler_params=_cparams(1, VMEM_BIG), cost_estimate=cost(2 * t * D, 0),
        )(h, w_in_pieces[q]))
    return zs, ht


def _z_tile(z_refs, n):
    return z_refs[n % 3][n // 3]


def _pool_tile(z, zh, win, keep_hist, cnt):
    zt = z.astype(F32)
    ext = jnp.concatenate([zh.astype(F32) * keep_hist, zt], axis=0)
    s, sh = ext, 1
    while sh < win:
        s = s + pltpu.roll(s, sh, 0)
        sh *= 2
    return s[HALO:] / cnt - zt


def _conv_taps(ext, cur, w_ref, lanes, lead):
    x1 = _shift_down(ext, 1, lead)
    x2 = _shift_down(ext, 2, lead)
    out = w_ref[2:3, lanes] * cur + w_ref[1:2, lanes] * x1 + w_ref[0:1, lanes] * x2
    return out, x1, x2


def fwd_mix(zs, x, pool_w, pool_scale, conv_w, wmix, tm, seq):
    t = x.shape[0]
    tps = seq // tm
    hb = tm // HALO

    def body(z0_ref, z1_ref, z2_ref, zh0_ref, zh1_ref, zh2_ref, x_ref, pw_ref, ps_ref, wpp_ref, cw_ref, wco_ref,
             wo_ref, x1_ref, yp_ref, yc_ref):
        z_refs, zh_refs = (z0_ref, z1_ref, z2_ref), (zh0_ref, zh1_ref, zh2_ref)
        it = pl.program_id(0) % tps
        keep_hist = jnp.where(it == 0, 0.0, 1.0)
        pos = it * tm + lax.broadcasted_iota(jnp.int32, (tm, 1), 0)
        p2 = []
        for g, win in enumerate(WINS):
            cnt = jnp.minimum(pos + 1, win).astype(F32)
            p = _pool_tile(_z_tile(z_refs, g), _z_tile(zh_refs, g), win, keep_hist, cnt)
            lanes = slice(g * CG, (g + 1) * CG)
            p2.append((_dot(p.astype(BF16), pw_ref[g]) * ps_ref[:, lanes]).astype(BF16))
        y_pool = _dot(jnp.concatenate(p2, axis=1), wpp_ref[...])
        u = []
        for q in range(NG):
            lanes = slice(q * CG, (q + 1) * CG)
            cv = _z_tile(z_refs, 8 + q).astype(F32) * _z_tile(z_refs, 12 + q).astype(F32)
            cvh = _z_tile(zh_refs, 8 + q).astype(F32) * _z_tile(zh_refs, 12 + q).astype(F32) * keep_hist
            cc, _, _ = _conv_taps(jnp.concatenate([cvh, cv], axis=0), cv, cw_ref, lanes, HALO)
            u.append((_z_tile(z_refs, 4 + q).astype(F32) * cc).astype(BF16))
        y_conv = _dot(jnp.concatenate(u, axis=1), wco_ref[...])
        ypb, ycb = y_pool.astype(BF16), y_conv.astype(BF16)
        yp_ref[...] = ypb
        yc_ref[...] = ycb
        merged = []
        for q in range(NG):
            lanes = slice(q * CG, (q + 1) * CG)
            sp = jax.nn.sigmoid(_z_tile(z_refs, 16 + q).astype(F32))
            sc = jax.nn.sigmoid(_z_tile(z_refs, 20 + q).astype(F32))
            merged.append((sp * ypb[:, lanes].astype(F32) + sc * ycb[:, lanes].astype(F32)).astype(BF16))
        x1_ref[...] = x_ref[...] + _dot(jnp.concatenate(merged, axis=1), wo_ref[...])

    def hist(i):
        return jnp.maximum(i * hb - 1, 0)

    const2 = lambda i: (0, 0)
    return pl.pallas_call(
        body, name="fwd_mix", grid=(t // tm,),
        in_specs=[pl.BlockSpec((NDEV, tm, CG), lambda i: (0, i, 0))] * 3
                 + [pl.BlockSpec((NDEV, HALO, CG), lambda i: (0, hist(i), 0))] * 3
                 + [pl.BlockSpec((tm, D), lambda i: (i, 0)),
                    pl.BlockSpec((NG, CG, CG), lambda i: (0, 0, 0)), pl.BlockSpec((1, D), const2),
                    pl.BlockSpec((D, D), lambda i: (0, MIX_POOL_PROJ)), pl.BlockSpec((3, D), const2),
                    pl.BlockSpec((D, D), lambda i: (0, MIX_CONV_OUT)), pl.BlockSpec((D, D), lambda i: (0, MIX_O))],
        out_specs=[pl.BlockSpec((tm, D), lambda i: (i, 0))] * 3,
        out_shape=[jax.ShapeDtypeStruct((t, D), F32), jax.ShapeDtypeStruct((t, D), BF16),
                   jax.ShapeDtypeStruct((t, D), BF16)],
        compiler_params=_cparams(1, VMEM_BIG),
    )(*zs, *zs, x, pool_w, pool_scale, wmix, conv_w, wmix, wmix)


def fwd_up(x1, g2, w_up_g, fcw, fcb, tm, seq):
    t = x1.shape[0]
    tps = seq // tm

    def body(x1_ref, g2_ref, wup_ref, fcw_ref, fcb_ref, up_ref, pre_ref, act_ref, actt_ref, h2t_ref, hist_ref):
        i = pl.program_id(0)
        keep_hist = jnp.where(i % tps == 0, 0.0, 1.0)

        @pl.when(i == 0)
        def _():
            hist_ref[...] = jnp.zeros_like(hist_ref)

        x1v = x1_ref[...]
        h2 = (x1v * _rms_inv(x1v) * g2_ref[...]).astype(BF16)
        h2t_ref[...] = h2.T
        lanes = slice(0, SH_UP)
        for c in range(NCH):
            conv = []
            for s in range(2):
                ub = _dot(h2, wup_ref[s, c]).astype(BF16)
                up_ref[s, c] = ub
                uf = ub.astype(F32)
                ext = jnp.concatenate([hist_ref[s, c] * keep_hist, uf], axis=0)
                hist_ref[s, c] = uf[tm - 8:]
                cc, _, _ = _conv_taps(ext, uf, fcw_ref.at[s, c], lanes, 8)
                conv.append(cc + fcb_ref[s, c])
                pre_ref[s, c] = conv[s].astype(BF16)
            a = (conv[0] * jax.nn.sigmoid(conv[0]) * conv[1]).astype(BF16)
            act_ref[c] = a
            actt_ref[c] = a.T

    tile = lambda i: (i, 0)
    const2 = lambda i: (0, 0)
    whole = lambda i: (0, 0, 0, 0)
    chunks = pl.BlockSpec((2, NCH, tm, SH_UP), lambda i: (0, 0, i, 0))
    return pl.pallas_call(
        body, name="fwd_up", grid=(t // tm,),
        in_specs=[pl.BlockSpec((tm, D), tile), pl.BlockSpec((1, D), const2),
                  pl.BlockSpec((2, NCH, D, SH_UP), whole), pl.BlockSpec((2, NCH, 3, SH_UP), whole),
                  pl.BlockSpec((2, NCH, 1, SH_UP), whole)],
        out_specs=[chunks, chunks, pl.BlockSpec((NCH, tm, SH_UP), lambda i: (0, i, 0)),
                   pl.BlockSpec((NCH, SH_UP, tm), lambda i: (0, 0, i)), pl.BlockSpec((D, tm), lambda i: (0, i))],
        out_shape=[jax.ShapeDtypeStruct((2, NCH, t, SH_UP), BF16), jax.ShapeDtypeStruct((2, NCH, t, SH_UP), BF16),
                   jax.ShapeDtypeStruct((NCH, t, SH_UP), BF16), jax.ShapeDtypeStruct((NCH, SH_UP, t), BF16),
                   jax.ShapeDtypeStruct((D, t), BF16)],
        scratch_shapes=[pltpu.VMEM((2, NCH, 8, SH_UP), F32)],
        compiler_params=_cparams(1, VMEM_BIG),
    )(x1, g2, w_up_g.reshape(2, NCH, D, SH_UP), fcw.reshape(2, NCH, 3, SH_UP), fcb.reshape(2, NCH, 1, SH_UP))


def fwd_down(x1, act, w_dn, gf, tgt, tm):
    t = x1.shape[0]

    def body(x1_ref, act_ref, wdn_ref, gf_ref, tgt_ref, dx2_ref, dx2b_ref, vec_ref):
        @pl.when(pl.program_id(0) == 0)
        def _():
            vec_ref[...] = jnp.zeros_like(vec_ref)

        d = None
        for c in range(NCH):
            part = _dot(act_ref[c], wdn_ref[c])
            d = part if d is None else d + part
        x2 = x1_ref[...] + d
        inv3 = _rms_inv(x2)
        xn = x2 * inv3
        diff = xn * gf_ref[...] - tgt_ref[...]
        dy = diff * (1.0 / D)
        vec_ref[0:1, :] += jnp.sum(dy * xn, axis=0, keepdims=True)
        vec_ref[1:2, :] += 0.5 * jnp.sum(jnp.mean(diff * diff, axis=-1))
        dx2 = _rms_bwd(dy, xn, inv3, gf_ref[...])
        dx2_ref[...] = dx2
        dx2b_ref[...] = dx2.astype(BF16)

    tile = lambda i: (i, 0)
    const2 = lambda i: (0, 0)
    return pl.pallas_call(
        body, name="fwd_down", grid=(t // tm,),
        in_specs=[pl.BlockSpec((tm, D), tile), pl.BlockSpec((NCH, tm, SH_UP), lambda i: (0, i, 0)),
                  pl.BlockSpec((NCH, SH_UP, D), lambda i: (0, 0, 0)), pl.BlockSpec((1, D), const2),
                  pl.BlockSpec((tm, D), tile)],
        out_specs=[pl.BlockSpec((tm, D), tile), pl.BlockSpec((tm, D), tile), pl.BlockSpec((8, D), const2)],
        out_shape=[jax.ShapeDtypeStruct((t, D), F32), jax.ShapeDtypeStruct((t, D), BF16),
                   jax.ShapeDtypeStruct((8, D), F32)],
        compiler_params=_cparams(1, VMEM_BIG),
    )(x1, act, w_dn, gf, tgt)


def bwd_ffn(dx2, x1, g2, up, pre, w_up_g, fcw, w_dn, tm, seq):
    t = x1.shape[0]
    nt = t // tm
    tps = seq // tm

    def body(dx2_ref, x1_ref, g2_ref, up_ref, pre_ref, wup_ref, fcw_ref, wdn_ref,
             dup_ref, dx1_ref, gvec_ref, gn_ref, carry_ref):
        i = pl.program_id(0)
        it = (nt - 1 - i) % tps
        keep_next = jnp.where(it == tps - 1, 0.0, 1.0)

        @pl.when(i == 0)
        def _():
            gvec_ref[...] = jnp.zeros_like(gvec_ref)
            gn_ref[...] = jnp.zeros_like(gn_ref)
            carry_ref[...] = jnp.zeros_like(carry_ref)

        dx2v = dx2_ref[...]
        dxb = dx2v.astype(BF16)
        lanes = slice(0, SH_UP)
        dh2 = None
        for c in range(NCH):
            pre = [pre_ref[s, c].astype(F32) for s in range(2)]
            sg = jax.nn.sigmoid(pre[0])
            dact = _dot_nt(dxb, wdn_ref[c])
            dpre = [dact * pre[1] * (sg * (1.0 + pre[0] * (1.0 - sg))), dact * (pre[0] * sg)]
            for s in range(2):
                dc = dpre[s]
                ext = jnp.concatenate([dc, carry_ref[s, c] * keep_next], axis=0)
                carry_ref[s, c] = dc[:8]
                shifted = (_shift_up(ext, 2, tm), _shift_up(ext, 1, tm), dc)
                uf = up_ref[s, c].astype(F32)
                gvec_ref[s, c, 0:1, lanes] += jnp.sum(dc, axis=0, keepdims=True)
                for tap in range(3):
                    gvec_ref[s, c, tap + 1:tap + 2, lanes] += jnp.sum(shifted[tap] * uf, axis=0, keepdims=True)
                w = fcw_ref.at[s, c]
                du = w[2:3, :] * dc + w[1:2, :] * shifted[1] + w[0:1, :] * shifted[0]
                dub = du.astype(BF16)
                dup_ref[s, c] = dub
                part = _dot_nt(dub, wup_ref[s, c])
                dh2 = part if dh2 is None else dh2 + part
        x1v = x1_ref[...]
        inv2 = _rms_inv(x1v)
        xn = x1v * inv2
        gn_ref[0:1, :] += jnp.sum(dh2 * xn, axis=0, keepdims=True)
        dx1_ref[...] = dx2v + _rms_bwd(dh2, xn, inv2, g2_ref[...])

    rev = lambda i: (nt - 1 - i, 0)
    const2 = lambda i: (0, 0)
    whole = lambda i: (0, 0, 0, 0)
    chunks = pl.BlockSpec((2, NCH, tm, SH_UP), lambda i: (0, 0, nt - 1 - i, 0))
    return pl.pallas_call(
        body, name="bwd_ffn", grid=(nt,),
        in_specs=[pl.BlockSpec((tm, D), rev), pl.BlockSpec((tm, D), rev), pl.BlockSpec((1, D), const2),
                  chunks, chunks, pl.BlockSpec((2, NCH, D, SH_UP), whole), pl.BlockSpec((2, NCH, 3, SH_UP), whole),
                  pl.BlockSpec((NCH, SH_UP, D), lambda i: (0, 0, 0))],
        out_specs=[chunks, pl.BlockSpec((tm, D), rev), pl.BlockSpec((2, NCH, 8, D), whole),
                   pl.BlockSpec((8, D), const2)],
        out_shape=[jax.ShapeDtypeStruct((2, NCH, t, SH_UP), BF16), jax.ShapeDtypeStruct((t, D), F32),
                   jax.ShapeDtypeStruct((2, NCH, 8, D), F32), jax.ShapeDtypeStruct((8, D), F32)],
        scratch_shapes=[pltpu.VMEM((2, NCH, 8, SH_UP), F32)],
        compiler_params=_cparams(1, VMEM_BIG),
    )(dx2, x1, g2, up, pre, w_up_g.reshape(2, NCH, D, SH_UP), fcw.reshape(2, NCH, 3, SH_UP), w_dn)


def bwd_mix(dx1, zs, y_pool, y_conv, pool_w, pool_scale, conv_w, wmix, tm, seq):
    t = dx1.shape[0]
    nt = t // tm
    tps = seq // tm
    hb = tm // HALO

    def body(da_ref, z0_ref, z1_ref, z2_ref, zh0_ref, zh1_ref, zh2_ref, yp_ref, yc_ref, pw_ref, ps_ref, wpp_ref,
             cw_ref, wco_ref, wo_ref,
             dz_ref, mg_ref, p2_ref, u_ref, dyp_ref, dyc_ref, p_ref, dpw_ref, gvec_ref, cp_ref, cc_ref):
        z_refs, zh_refs = (z0_ref, z1_ref, z2_ref), (zh0_ref, zh1_ref, zh2_ref)
        i = pl.program_id(0)
        it = (nt - 1 - i) % tps
        keep_hist = jnp.where(it == 0, 0.0, 1.0)
        keep_next = jnp.where(it == tps - 1, 0.0, 1.0)
        pos = it * tm + lax.broadcasted_iota(jnp.int32, (tm, 1), 0)

        @pl.when(i == 0)
        def _():
            gvec_ref[...] = jnp.zeros_like(gvec_ref)
            cp_ref[...] = jnp.zeros_like(cp_ref)
            cc_ref[...] = jnp.zeros_like(cc_ref)

        dm = _dot_nt(da_ref[...].astype(BF16), wo_ref[...])
        merged, dyp, dyc = [], [], []
        for q in range(NG):
            lanes = slice(q * CG, (q + 1) * CG)
            sp = jax.nn.sigmoid(_z_tile(z_refs, 16 + q).astype(F32))
            sc = jax.nn.sigmoid(_z_tile(z_refs, 20 + q).astype(F32))
            yp = yp_ref[:, lanes].astype(F32)
            yc = yc_ref[:, lanes].astype(F32)
            dmq = dm[:, lanes]
            merged.append((sp * yp + sc * yc).astype(BF16))
            dyp.append((dmq * sp).astype(BF16))
            dyc.append((dmq * sc).astype(BF16))
            dz_ref[16 + q] = (dmq * yp * (sp * (1.0 - sp))).astype(BF16)
            dz_ref[20 + q] = (dmq * yc * (sc * (1.0 - sc))).astype(BF16)
        mg_ref[...] = jnp.concatenate(merged, axis=1)
        dypb = jnp.concatenate(dyp, axis=1)
        dycb = jnp.concatenate(dyc, axis=1)
        dyp_ref[...] = dypb
        dyc_ref[...] = dycb

        dp2 = _dot_nt(dypb, wpp_ref[...])
        p2 = []
        for g, win in enumerate(WINS):
            lanes = slice(g * CG, (g + 1) * CG)
            cnt = jnp.minimum(pos + 1, win).astype(F32)
            p = _pool_tile(_z_tile(z_refs, g), _z_tile(zh_refs, g), win, keep_hist, cnt)
            pb = p.astype(BF16)
            p_ref[g] = pb
            pw = _dot(pb, pw_ref[g])
            p2.append((pw * ps_ref[:, lanes]).astype(BF16))
            dp2g = dp2[:, lanes]
            gvec_ref[0:1, lanes] += jnp.sum(dp2g * pw, axis=0, keepdims=True)
            dpwb = (dp2g * ps_ref[:, lanes]).astype(BF16)
            dpw_ref[g] = dpwb
            dp = _dot_nt(dpwb, pw_ref[g])
            qv = dp / cnt
            ext = jnp.concatenate([qv, cp_ref[g] * keep_next], axis=0)
            cp_ref[g] = qv[:HALO]
            n = tm + HALO
            s, sh = ext, 1
            while sh < win:
                s = s + pltpu.roll(s, n - sh, 0)
                sh *= 2
            dz_ref[g] = (s[:tm] - dp).astype(BF16)
        p2_ref[...] = jnp.concatenate(p2, axis=1)

        du = _dot_nt(dycb, wco_ref[...])
        u = []
        for q in range(NG):
            lanes = slice(q * CG, (q + 1) * CG)
            zb = _z_tile(z_refs, 4 + q).astype(F32)
            zc = _z_tile(z_refs, 8 + q).astype(F32)
            zv = _z_tile(z_refs, 12 + q).astype(F32)
            cv = zc * zv
            cvh = _z_tile(zh_refs, 8 + q).astype(F32) * _z_tile(zh_refs, 12 + q).astype(F32) * keep_hist
            cc, cv1, cv2 = _conv_taps(jnp.concatenate([cvh, cv], axis=0), cv, cw_ref, lanes, HALO)
            u.append((zb * cc).astype(BF16))
            duq = du[:, lanes]
            dz_ref[4 + q] = (duq * cc).astype(BF16)
            dcc = duq * zb
            for tap, src in enumerate((cv2, cv1, cv)):
                gvec_ref[tap + 1:tap + 2, lanes] += jnp.sum(dcc * src, axis=0, keepdims=True)
            ext = jnp.concatenate([dcc, cc_ref[:, lanes] * keep_next], axis=0)
            cc_ref[:, lanes] = dcc[:8]
            dcv = (cw_ref[2:3, lanes] * dcc + cw_ref[1:2, lanes] * _shift_up(ext, 1, tm)
                   + cw_ref[0:1, lanes] * _shift_up(ext, 2, tm))
            dz_ref[8 + q] = (dcv * zv).astype(BF16)
            dz_ref[12 + q] = (dcv * zc).astype(BF16)
        u_ref[...] = jnp.concatenate(u, axis=1)

    def hist(i):
        return jnp.maximum((nt - 1 - i) * hb - 1, 0)

    rev = lambda i: (nt - 1 - i, 0)
    rev3 = lambda i: (0, nt - 1 - i, 0)
    const2 = lambda i: (0, 0)
    tok = jax.ShapeDtypeStruct((t, D), BF16)
    grp = jax.ShapeDtypeStruct((NG, t, CG), BF16)
    return pl.pallas_call(
        body, name="bwd_mix", grid=(nt,),
        in_specs=[pl.BlockSpec((tm, D), rev)] + [pl.BlockSpec((NDEV, tm, CG), rev3)] * 3
                 + [pl.BlockSpec((NDEV, HALO, CG), lambda i: (0, hist(i), 0))] * 3
                 + [pl.BlockSpec((tm, D), rev), pl.BlockSpec((tm, D), rev),
                    pl.BlockSpec((NG, CG, CG), lambda i: (0, 0, 0)), pl.BlockSpec((1, D), const2),
                    pl.BlockSpec((D, D), lambda i: (0, MIX_POOL_PROJ)), pl.BlockSpec((3, D), const2),
                    pl.BlockSpec((D, D), lambda i: (0, MIX_CONV_OUT)), pl.BlockSpec((D, D), lambda i: (0, MIX_O))],
        out_specs=[pl.BlockSpec((NZT, tm, CG), rev3)] + [pl.BlockSpec((tm, D), rev)] * 5
                  + [pl.BlockSpec((NG, tm, CG), rev3)] * 2 + [pl.BlockSpec((8, D), const2)],
        out_shape=[jax.ShapeDtypeStruct((NZT, t, CG), BF16), tok, tok, tok, tok, tok, grp, grp,
                   jax.ShapeDtypeStruct((8, D), F32)],
        scratch_shapes=[pltpu.VMEM((NG, HALO, CG), F32), pltpu.VMEM((8, D), F32)],
        compiler_params=_cparams(1, VMEM_BIG),
    )(dx1, *zs, *zs, y_pool, y_conv, pool_w, pool_scale, wmix, conv_w, wmix, wmix)


def bwd_in(dz, w_in_pieces, dx1, x, g1, tm):
    t = x.shape[0]

    def body(dz_ref, w0_ref, w1_ref, w2_ref, dx1_ref, x_ref, g_ref, gx_ref, gn_ref):
        @pl.when(pl.program_id(0) == 0)
        def _():
            gn_ref[...] = jnp.zeros_like(gn_ref)

        dh = None
        for j in range(NDEV):
            for q, w_ref in enumerate((w0_ref, w1_ref, w2_ref)):
                part = _dot_nt(dz_ref[3 * j + q], w_ref[j])
                dh = part if dh is None else dh + part
        xv = x_ref[...]
        inv = _rms_inv(xv)
        xn = xv * inv
        gn_ref[0:1, :] += jnp.sum(dh * xn, axis=0, keepdims=True)
        gx_ref[...] = dx1_ref[...] + _rms_bwd(dh, xn, inv, g_ref[...])

    tile = lambda i: (i, 0)
    return pl.pallas_call(
        body, name="bwd_in", grid=(t // tm,),
        in_specs=[pl.BlockSpec((NZT, tm, CG), lambda i: (0, i, 0))]
                 + [pl.BlockSpec((NDEV, D, CG), lambda i: (0, 0, 0))] * 3
                 + [pl.BlockSpec((tm, D), tile), pl.BlockSpec((tm, D), tile), pl.BlockSpec((1, D), lambda i: (0, 0))],
        out_specs=[pl.BlockSpec((tm, D), tile), pl.BlockSpec((8, D), lambda i: (0, 0))],
        out_shape=[jax.ShapeDtypeStruct((t, D), F32), jax.ShapeDtypeStruct((8, D), F32)],
        compiler_params=_cparams(1, VMEM_BIG),
    )(dz, *w_in_pieces, dx1, x, g1)


def _slot(j):
    return j % 2, j // 2


def wgrad_cols(at, b, q, name, tk):
    m, t = at.shape
    width = b.shape[3]

    def body(a_ref, b_ref, o_ref):
        @pl.when(pl.program_id(1) == 0)
        def _():
            o_ref[...] = jnp.zeros_like(o_ref)

        o_ref[...] += _dot(a_ref[...], b_ref[...])

    return pl.pallas_call(
        body, name=name, grid=(NDEV, t // tk),
        in_specs=[pl.BlockSpec((m, tk), lambda j, k: (0, k)),
                  pl.BlockSpec((None, None, tk, width), lambda j, k: (j, q, k, 0))],
        out_specs=pl.BlockSpec((None, None, m, width), lambda j, k: (j % 2, j // 2, 0, 0)),
        out_shape=jax.ShapeDtypeStruct((2, 4, m, width), F32),
        compiler_params=_cparams(2, VMEM_BIG),
    )(at, b)


def wgrad_cols_resident(at, b, q, name, tk):
    m, t = at.shape
    width = b.shape[3]

    def body(a_ref, b_ref, o_ref):
        k, j = pl.program_id(0), pl.program_id(1)

        @pl.when((k == 0) & (j == 0))
        def _():
            o_ref[...] = jnp.zeros_like(o_ref)

        o_ref[j % 2, j // 2] += _dot(a_ref[...], b_ref[...])

    return pl.pallas_call(
        body, name=name, grid=(t // tk, NDEV),
        in_specs=[pl.BlockSpec((m, tk), lambda k, j: (0, k)),
                  pl.BlockSpec((None, None, tk, width), lambda k, j: (j, q, k, 0))],
        out_specs=pl.BlockSpec((2, 4, m, width), lambda k, j: (0, 0, 0, 0)),
        out_shape=jax.ShapeDtypeStruct((2, 4, m, width), F32),
        compiler_params=_cparams(2, VMEM_BIG),
    )(at, b)


def wgrad_down(actt, dx2b):
    t = dx2b.shape[0]

    def body(a_ref, b_ref, o_ref):
        r = _dot(a_ref[...], b_ref[...])
        o_ref[0] = r[:SH_DN]
        o_ref[1] = r[SH_DN:]

    return pl.pallas_call(
        body, name="wgrad_down", grid=(NCH,),
        in_specs=[pl.BlockSpec((None, SH_UP, t), lambda k: (k, 0, 0)), pl.BlockSpec((t, D), lambda k: (0, 0))],
        out_specs=pl.BlockSpec((2, None, SH_DN, D), lambda k: (0, k, 0, 0)),
        out_shape=jax.ShapeDtypeStruct((2, 4, SH_DN, D), F32),
        compiler_params=_cparams(1, VMEM_BIG),
    )(actt, dx2b)


def wgrad_square(a, b, name, tk):
    t = a.shape[0]

    def body(a_ref, b_ref, o_ref, acc_ref):
        kt = pl.program_id(0)

        @pl.when(kt == 0)
        def _():
            acc_ref[...] = jnp.zeros_like(acc_ref)

        acc_ref[...] += _dot_tn(a_ref[...], b_ref[...].astype(BF16))

        @pl.when(kt == pl.num_programs(0) - 1)
        def _():
            for j in range(NDEV):
                cc, xy = _slot(j)
                o_ref[cc, xy] = acc_ref[j * 128:(j + 1) * 128]

    return pl.pallas_call(
        body, name=name, grid=(t // tk,),
        in_specs=[pl.BlockSpec((tk, D), lambda k: (k, 0)), pl.BlockSpec((tk, D), lambda k: (k, 0))],
        out_specs=pl.BlockSpec((2, 4, 128, D), lambda k: (0, 0, 0, 0)),
        out_shape=jax.ShapeDtypeStruct((2, 4, 128, D), F32),
        scratch_shapes=[pltpu.VMEM((D, D), F32)],
        compiler_params=_cparams(1, VMEM_BIG),
    )(a, b)


def wgrad_pool(p, dpw, tk):
    t = p.shape[1]

    def body(a_ref, b_ref, o_ref):
        @pl.when(pl.program_id(0) == 0)
        def _():
            o_ref[...] = jnp.zeros_like(o_ref)

        for g in range(NG):
            o_ref[g] += _dot_tn(a_ref[g], b_ref[g])

    return pl.pallas_call(
        body, name="wgrad_pool", grid=(t // tk,),
        in_specs=[pl.BlockSpec((NG, tk, CG), lambda k: (0, k, 0))] * 2,
        out_specs=pl.BlockSpec((NG, CG, CG), lambda k: (0, 0, 0)),
        out_shape=jax.ShapeDtypeStruct((NG, CG, CG), F32),
        compiler_params=_cparams(1, VMEM_BIG),
    )(p, dpw)


def _adamw(w, g, m, v):
    m = ADAM_B1 * m + (1.0 - ADAM_B1) * g
    v = ADAM_B2 * v + (1.0 - ADAM_B2) * (g * g)
    m_hat = m / (1.0 - ADAM_B1 ** ADAM_STEP)
    v_hat = v / (1.0 - ADAM_B2 ** ADAM_STEP)
    delta = -ADAM_LR * (m_hat / (jnp.sqrt(v_hat) + ADAM_EPS) + ADAM_WD * w)
    return delta, m, v


def _row_block(r):
    return 512 if r % 512 == 0 else r


def chip_partial(place, g, from_sibling, name):
    _, _, r, c = g.shape

    def body(place_ref, g_ref, s_ref, o_ref):
        o_ref[...] = (g_ref[...] + s_ref[...]).astype(BF16)

    return pl.pallas_call(
        body, name=name,
        grid_spec=pltpu.PrefetchScalarGridSpec(
            num_scalar_prefetch=1, grid=(3,),
            in_specs=[pl.BlockSpec((None, None, r, c), lambda k, pr: (pr[0], pr[1] ^ (k + 1), 0, 0)),
                      pl.BlockSpec((None, r, c), lambda k, pr: (pr[1] ^ (k + 1), 0, 0))],
            out_specs=pl.BlockSpec((None, r, c), lambda k, pr: (pr[1] ^ (k + 1), 0, 0))),
        out_shape=jax.ShapeDtypeStruct((4, r, c), BF16),
        compiler_params=_cparams(1, VMEM_BIG),
    )(place, g, from_sibling)


def finish_adamw(place, gs, from_sibling, from_chips, w, m, v, name, transposed=False):
    n = len(gs)
    r = gs[0].shape[2]
    widths = [g.shape[3] for g in gs]
    c = sum(widths)
    br = _row_block(r)

    def body(place_ref, *refs):
        g_refs, s_refs, c_refs = refs[:n], refs[n:2 * n], refs[2 * n:5 * n]
        w_ref, m_ref, v_ref, og_ref, od_ref, om_ref, ov_ref = refs[5 * n:]
        cols = []
        for q in range(n):
            grad = g_refs[q][...] + s_refs[q][...]
            for k in range(3):
                grad = grad + c_refs[3 * q + k][...].astype(F32)
            cols.append(grad)
        grad = cols[0] if n == 1 else jnp.concatenate(cols, axis=1)
        if transposed:
            grad = grad.T
        og_ref[...] = grad
        od_ref[...], om_ref[...], ov_ref[...] = _adamw(w_ref[...], grad, m_ref[...], v_ref[...])

    def other(k, cq):
        return pl.BlockSpec((None, br, cq), lambda i, pr: (pr[1] ^ k, i, 0))

    row = pl.BlockSpec((c, br), lambda i, pr: (0, i)) if transposed else pl.BlockSpec((br, c), lambda i, pr: (i, 0))
    out = jax.ShapeDtypeStruct((c, r) if transposed else (r, c), F32)
    in_specs = [pl.BlockSpec((None, None, br, cq), lambda i, pr: (pr[0], pr[1], i, 0)) for cq in widths]
    in_specs += [pl.BlockSpec((None, br, cq), lambda i, pr: (pr[1], i, 0)) for cq in widths]
    in_specs += [other(k, cq) for cq in widths for k in (1, 2, 3)]
    return pl.pallas_call(
        body, name=name,
        grid_spec=pltpu.PrefetchScalarGridSpec(
            num_scalar_prefetch=1, grid=(r // br,), in_specs=in_specs + [row, row, row], out_specs=[row] * 4),
        out_shape=[out] * 4,
        compiler_params=_cparams(1, VMEM_BIG),
    )(place, *gs, *from_sibling, *[fc for fc in from_chips for _ in range(3)], w, m, v)


def adamw_small(items):
    n = len(items)

    def body(*refs):
        ins, outs = refs[:4 * n], refs[4 * n:]
        for i in range(n):
            w, g, m, v = (r[...] for r in ins[4 * i:4 * i + 4])
            outs[3 * i][...], outs[3 * i + 1][...], outs[3 * i + 2][...] = _adamw(w, g, m, v)

    out = [jax.ShapeDtypeStruct(it[0].shape, F32) for it in items for _ in range(3)]
    res = pl.pallas_call(body, name="adamw_small", out_shape=out)(*[a for it in items for a in it])
    return [res[3 * i:3 * i + 3] for i in range(n)]


def kernel(x, norm_mix, w_in, pool_w, pool_scale, w_pool_proj, conv_w, w_conv_out, w_o, norm_ffn, w_up, ffn_conv_w, ffn_conv_b, w_down, norm_final, loss_target, m_norm_mix, m_w_in, m_pool_w, m_pool_scale, m_w_pool_proj, m_conv_w, m_w_conv_out, m_w_o, m_norm_ffn, m_w_up, m_ffn_conv_w, m_ffn_conv_b, m_w_down, m_norm_final, v_norm_mix, v_w_in, v_pool_w, v_pool_scale, v_w_pool_proj, v_conv_w, v_w_conv_out, v_w_o, v_norm_ffn, v_w_up, v_ffn_conv_w, v_ffn_conv_b, v_w_down, v_norm_final):
    nb, seq, _ = x.shape
    t = nb * seq
    tm_in = min(TM_IN, t)
    tm_mix = min(TM_MIX, seq)
    tm_ffn = min(TM_FFN, seq)
    tk = min(TK_WGRAD, t)
    xt = x.reshape(t, D)
    tgt = loss_target.reshape(t, D)
    xi, yi, ci = _pos()
    me = 4 * xi + 2 * yi + ci
    place = jnp.stack([ci, 2 * xi + yi]).astype(jnp.int32)

    tie = lax.optimization_barrier
    w_in_b = w_in[0].astype(BF16)
    w_in_g = [all_gather_blocks([w_in_b[:, q * CG:(q + 1) * CG]], f"all_gather_w_in_{q}", 0)[0] for q in range(3)]
    taps = (jnp.pad(conv_w[0], ((0, 5), (0, D - 128))) + jnp.pad(ffn_conv_w[0], ((3, 2), (0, D - SH_UP))))
    taps_g = _exchange_small(taps, False, "all_gather_taps")
    mix_shard = jnp.concatenate(
        [w_pool_proj[0], w_conv_out[0], w_o[0], pool_w[0].reshape(NG * 32, CG)], axis=1).astype(BF16)
    mix_shard, taps_g = tie((mix_shard, taps_g))
    wmix_g, = all_gather_blocks([mix_shard], "all_gather_w_mix", 0)
    ffn_shards, w_in_g[0] = tie(([w_up[0].astype(BF16), w_down[0].astype(BF16)], w_in_g[0]))
    w_up_g, = all_gather_blocks(ffn_shards[:1], "all_gather_w_up", 0)
    w_dn_g, = all_gather_blocks(ffn_shards[1:], "all_gather_w_down", 0)
    w_dn_f = w_dn_g.reshape(NCH, SH_UP, D)
    conv_w_f = taps_g[:, 0:3, :128].transpose(1, 0, 2).reshape(3, D)
    fcw_f = taps_g[:, 3:6, :SH_UP]
    fcb_f = ffn_conv_b.reshape(NDEV, 1, SH_UP)
    gfin = norm_final.reshape(1, D)

    zs, h1 = fwd_in(xt, norm_mix, w_in_g, tm_in)
    wmix_g, zs = tie((wmix_g, zs))
    wmix = wmix_g.reshape(D, MIX_COLS)
    pool_w_f = wmix_g[:, :, 3 * D:].reshape(NDEV, NG, 32, CG).transpose(1, 0, 2, 3).reshape(NG, CG, CG)
    x1, y_pool, y_conv = fwd_mix(zs, xt, pool_w_f, pool_scale, conv_w_f, wmix, tm_mix, seq)
    up, pre, act_tok, act, h2 = fwd_up(x1, norm_ffn, w_up_g, fcw_f, fcb_f, tm_ffn, seq)
    dx2, dx2b, ffn_vec = fwd_down(x1, act_tok, w_dn_f, gfin, tgt, min(TM_IN, t))

    def to_sibling(full, tag):
        return reduce_scatter_d2d(full, "reduce_scatter_d2d_" + tag, 1)

    def partials(full, from_sib, names):
        return [chip_partial(place, g, s, "chip_partial_" + nm) for g, s, nm in zip(full, from_sib, names)]

    def to_chips(parts, tag):
        return reduce_scatter_ici(parts, "reduce_scatter_ici_" + tag, 2)

    def finish(nm, gs, from_sib, from_chips, wmv, transposed=False):
        rc = (gs[0].shape[2], sum(g.shape[3] for g in gs))
        wmv2 = [a.reshape(rc).T if transposed else a.reshape(rc) for a in wmv]
        outs = finish_adamw(place, gs, from_sib, from_chips, *wmv2, "adamw_" + nm, transposed)
        return [(o.T if transposed else o).reshape(wmv[0].shape) for o in outs]

    def after(x, dep):
        return tie((x, dep))[0]

    big = {}
    d_up, dx1, g_ffn_vec, g_nffn = bwd_ffn(dx2, x1, norm_ffn, up, pre, w_up_g, fcw_f, w_dn_f, tm_ffn, seq)
    gw_up = wgrad_cols(h2, d_up.reshape(NDEV, 1, t, SH_UP), 0, "wgrad_up", t)
    sib_up = to_sibling([gw_up], "w_up")
    gw_dn = wgrad_down(act, after(dx2b, gw_up))
    sib_dn = to_sibling([after(gw_dn, sib_up)], "w_down")
    dx1, part_up = tie((dx1, partials([gw_up], sib_up, ["w_up"])))
    chips_up = to_chips(part_up, "w_up")
    dz, merged, p2, u, dyp, dyc, p, dpw, g_mix_vec = bwd_mix(
        dx1, zs, y_pool, y_conv, pool_w_f, pool_scale, conv_w_f, wmix, tm_mix, seq)
    merged, part_dn = tie((merged, partials([gw_dn], sib_dn, ["w_down"])))
    chips_dn = to_chips(part_dn, "w_down")
    gw_o = wgrad_square(merged, dx1, "wgrad_o", tk)
    gw_pp = wgrad_square(p2, dyp, "wgrad_pool_proj", tk)
    gw_co = wgrad_square(u, dyc, "wgrad_conv_out", tk)
    gw_pool = wgrad_pool(p, dpw, tk).reshape(NG, 4, 2, 32, CG).transpose(2, 1, 0, 3, 4).reshape(2, 4, NG * 32, CG)
    dz8 = dz.reshape(NDEV, 3, t, CG)
    gw_in, sib_in, chips_in = [None] * 3, [None] * 3, [None] * 3
    sib_a = to_sibling(after([gw_o, gw_pp], (chips_up, gw_pool)), "mix_a")
    sib_b = to_sibling(after([gw_co, gw_pool], sib_a), "mix_b")
    gw_in[0] = wgrad_cols_resident(h1, dz8, 0, "wgrad_in_0", t)
    h1, part_a, part_b = tie((h1, partials([gw_o, gw_pp], sib_a, ["w_o", "w_pool_proj"]),
                              partials([gw_co, gw_pool], sib_b, ["w_conv_out", "pool_w"])))
    chips_a = to_chips(after(part_a, chips_dn), "mix_a")
    chips_b = to_chips(part_b, "mix_b")
    sib_in[0] = to_sibling(after([gw_in[0]], sib_b), "w_in_0")
    gw_in[1] = wgrad_cols_resident(h1, dz8, 1, "wgrad_in_1", t)
    h1, part_in0, gw_in[1] = tie((h1, partials([gw_in[0]], sib_in[0], ["w_in_0"]), gw_in[1]))
    chips_in[0] = to_chips(part_in0, "w_in_0")
    sib_in[1] = to_sibling(after([gw_in[1]], sib_in[0]), "w_in_1")
    h1, big["w_down"], big["w_up"] = tie((
        h1, finish("w_down", [gw_dn], sib_dn, chips_dn, (w_down, m_w_down, v_w_down)),
        finish("w_up", [gw_up], sib_up, chips_up, (w_up, m_w_up, v_w_up), transposed=True)))
    gw_in[2] = wgrad_cols_resident(h1, dz8, 2, "wgrad_in_2", t)
    sib_in[2] = to_sibling(after([gw_in[2]], (chips_a, chips_b, chips_in[0])), "w_in_2")
    dx1, part_in1, part_in2, big["w_o"], big["w_pool_proj"], big["w_conv_out"], big["pool_w"] = tie((
        dx1, partials([gw_in[1]], sib_in[1], ["w_in_1"]), partials([gw_in[2]], sib_in[2], ["w_in_2"]),
        finish("w_o", [gw_o], sib_a[:1], chips_a[:1], (w_o, m_w_o, v_w_o)),
        finish("w_pool_proj", [gw_pp], sib_a[1:], chips_a[1:], (w_pool_proj, m_w_pool_proj, v_w_pool_proj)),
        finish("w_conv_out", [gw_co], sib_b[:1], chips_b[:1], (w_conv_out, m_w_conv_out, v_w_conv_out)),
        finish("pool_w", [gw_pool], sib_b[1:], chips_b[1:], (pool_w, m_pool_w, v_pool_w))))
    chips_in[1] = to_chips(after(part_in1, sib_in[2]), "w_in_1")
    chips_in[2] = to_chips(part_in2, "w_in_2")
    small_g, = all_gather_blocks(
        [after(jnp.concatenate([g_mix_vec, g_nffn, ffn_vec, g_ffn_vec.reshape(8 * NDEV, D)], axis=0), sib_in[2])],
        "all_gather_small", 0)
    grad_x, g_nmix = bwd_in(dz, w_in_g, dx1, xt, norm_mix, min(TM_BWD_IN, t))
    grad_x, chips_in = tie((grad_x, chips_in))
    big["w_in"] = finish("w_in", gw_in, [s[0] for s in sib_in], [c[0] for c in chips_in], (w_in, m_w_in, v_w_in))

    red_n, red = _exchange_small(g_nmix, True, "all_reduce_small", gathered=small_g)
    g_norm_mix, g_pool_scale, g_norm_ffn = red_n[0:1], red[0:1], red[8:9]
    g_conv_w = lax.dynamic_slice(red, (1, me * 128), (3, 128))
    g_norm_final = red[16]
    loss = red[17, 0]
    g_fcb = red[24:].reshape(NDEV, 8, D)[:, 0, :SH_UP].reshape(1, FF2)
    g_fcw = lax.dynamic_slice(red, (25 + 8 * me, 0), (3, SH_UP))
    grads = {"norm_mix": g_norm_mix, "pool_scale": g_pool_scale, "norm_ffn": g_norm_ffn, "norm_final": g_norm_final,
             "ffn_conv_b": g_fcb, "conv_w": g_conv_w.reshape(1, 3, 128), "ffn_conv_w": g_fcw.reshape(1, 3, SH_UP)}
    small_wmv = {"norm_mix": (norm_mix, m_norm_mix, v_norm_mix), "pool_scale": (pool_scale, m_pool_scale, v_pool_scale),
                 "norm_ffn": (norm_ffn, m_norm_ffn, v_norm_ffn), "norm_final": (norm_final, m_norm_final, v_norm_final),
                 "ffn_conv_b": (ffn_conv_b, m_ffn_conv_b, v_ffn_conv_b), "conv_w": (conv_w, m_conv_w, v_conv_w),
                 "ffn_conv_w": (ffn_conv_w, m_ffn_conv_w, v_ffn_conv_w)}
    small_names = list(small_wmv)
    flat2 = lambda a: a.reshape(1, -1) if a.ndim == 1 else (a.transpose(1, 0, 2) if a.ndim == 3 else a)
    unflat = lambda o, like: o.transpose(1, 0, 2) if like.ndim == 3 else o.reshape(like.shape)
    small_out = adamw_small([(flat2(small_wmv[nm][0]), flat2(grads[nm]), flat2(small_wmv[nm][1]),
                              flat2(small_wmv[nm][2])) for nm in small_names])
    small = {nm: [unflat(o, small_wmv[nm][0]) for o in outs] for nm, outs in zip(small_names, small_out)}

    order = ["norm_mix", "w_in", "pool_w", "pool_scale", "w_pool_proj", "conv_w", "w_conv_out", "w_o", "norm_ffn",
             "w_up", "ffn_conv_w", "ffn_conv_b", "w_down", "norm_final"]
    out = [loss, grad_x.reshape(nb, seq, D)]
    out += [big[nm][0] if nm in big else grads[nm] for nm in order]
    for idx in range(3):
        out += [big[nm][idx + 1] if nm in big else small[nm][idx] for nm in order]
    return tuple(out)
```

```python
import jax
import jax.numpy as jnp
from jax import lax
from jax.experimental import pallas as pl
from jax.experimental.pallas import tpu as pltpu
from jax.experimental.pallas import tpu_sc as plsc

F32 = jnp.float32
BF16 = jnp.bfloat16

NDEV = 8
D = 1024
NG = 4
CG = 256
WINS = (2, 4, 8, 16)
DIN = 6 * D
SH_IN = DIN // NDEV
NZT = DIN // CG
FF2 = 5632
SH_UP = FF2 // NDEV
FF = FF2 // 2
NCH = 4
SH_DN = FF // NDEV
RMS_EPS = 1e-6
HALO = 16

ADAM_LR = 0.001
ADAM_B1 = 0.9
ADAM_B2 = 0.999
ADAM_EPS = 1e-08
ADAM_WD = 0.01
ADAM_STEP = 10

TM_IN = 512
TM_BWD_IN = 256
TM_MIX = 256
TM_FFN = 256
TK_WGRAD = 1024
MIX_POOL_PROJ, MIX_CONV_OUT, MIX_O = 0, 1, 2
MIX_COLS = 3 * D + CG
VMEM_BIG = 56 * 1024 * 1024
MESH = pl.DeviceIdType.MESH
ANY = pl.BlockSpec(memory_space=pl.ANY)


def _cparams(n_axes, vmem=None):
    return pltpu.CompilerParams(dimension_semantics=("arbitrary",) * n_axes, vmem_limit_bytes=vmem)


def _dot(a, b):
    return jnp.dot(a, b, preferred_element_type=F32)


def _dot_nt(a, b):
    return lax.dot_general(a, b, (((1,), (1,)), ((), ())), preferred_element_type=F32)


def _dot_tn(a, b):
    return lax.dot_general(a, b, (((0,), (0,)), ((), ())), preferred_element_type=F32)


def _shift_down(ext, s, lead):
    return pltpu.roll(ext, s, 0)[lead:]


def _shift_up(ext, s, tm):
    n = ext.shape[0]
    return pltpu.roll(ext, n - s, 0)[:tm]


def _rms_inv(x):
    return lax.rsqrt(jnp.mean(x * x, axis=-1, keepdims=True) + RMS_EPS)


def _rms_bwd(dh, xn, inv, g):
    dxn = dh * g
    return inv * (dxn - xn * jnp.mean(dxn * xn, axis=-1, keepdims=True))


def _pos():
    return lax.axis_index("x"), lax.axis_index("y"), lax.axis_index("c")


def _handshake(peers):
    barrier = pltpu.get_barrier_semaphore()
    for peer in peers:
        pl.semaphore_signal(barrier, inc=1, device_id=peer, device_id_type=MESH)
    pl.semaphore_wait(barrier, len(peers))


def _sequencer(body, out_type, n_sems, name, collective_id):
    return pl.kernel(
        body, out_type=out_type, mesh=plsc.ScalarSubcoreMesh(axis_name="sequencer", num_cores=1), name=name,
        scratch_types=[pltpu.SemaphoreType.DMA((n_sems,)), pltpu.SemaphoreType.DMA((n_sems,))],
        compiler_params=pltpu.CompilerParams(collective_id=collective_id))


def all_gather_blocks(shards, name, collective_id):
    n = len(shards)

    def body(*refs):
        ins, outs = refs[:n], refs[n:2 * n]
        send_sems, recv_sems = refs[2 * n:]
        x, y, c = _pos()
        me, sibling = (x, y, c), (x, y, 1 - c)
        first_chip, second_chip, diagonal = (x ^ (1 - c), y ^ c), (x ^ c, y ^ (1 - c)), (1 - x, 1 - y)
        first, second = (*first_chip, c), (*second_chip, c)
        _handshake([sibling, first, second])

        def copy(w, k, block, to, src=None):
            slot = outs[w].at[4 * block[0] + 2 * block[1] + block[2]]
            return pltpu.make_async_remote_copy(
                src_ref=slot if src is None else src, dst_ref=slot,
                send_sem=send_sems.at[8 * w + k], recv_sem=recv_sems.at[8 * w + k], device_id=to, device_id_type=MESH)

        mine, sent = [], []
        for w in range(n):
            m = pltpu.make_async_copy(ins[w], outs[w].at[4 * x + 2 * y + c], send_sems.at[8 * w + 7])
            m.start()
            mine.append(m)
            sent += [copy(w, k, me, to, src=ins[w]) for k, to in enumerate((sibling, first, second))]
        for cp in sent:
            cp.start()
        for k, chip in ((1, first_chip), (2, second_chip), (3, diagonal)):
            for w in range(n):
                copy(w, k, (*chip, c), me).wait_recv()
                onward = [copy(w, 3 + k, (*chip, c), sibling)] + ([copy(w, 3, (*chip, c), second)] if k == 1 else [])
                for cp in onward:
                    cp.start()
                sent += onward
        for w in range(n):
            copy(w, 0, sibling, me).wait_recv()
            for k, chip in ((4, second_chip), (5, first_chip), (6, diagonal)):
                copy(w, k, (*chip, 1 - c), me).wait_recv()
        for cp in sent:
            cp.wait_send()
        for m in mine:
            m.wait()

    out = [jax.ShapeDtypeStruct((NDEV,) + s.shape, s.dtype) for s in shards]
    return _sequencer(body, out, 8 * n, name, collective_id)(*shards)


def _exchange_small(v, reduce, name, gathered=None):
    rows = v.shape[0]

    def body(*refs):
        if gathered is None:
            v_ref, out_ref, slots, send_sems, recv_sems, local_sem = refs
        else:
            v_ref, g_ref, out_ref, gsum_ref, slots, send_sems, recv_sems, local_sem = refs
        x, y, c = _pos()
        me = 4 * x + 2 * y + c
        mine = pltpu.make_async_copy(v_ref, slots.at[me], local_sem)
        mine.start()
        offs = [(dx, dy, dc) for dx in (0, 1) for dy in (0, 1) for dc in (0, 1)][1:]

        def copy(k, src_slot, to):
            return pltpu.make_async_remote_copy(
                src_ref=v_ref, dst_ref=slots.at[src_slot], send_sem=send_sems.at[k], recv_sem=recv_sems.at[k],
                device_id=to, device_id_type=MESH)

        sends = []
        for k, (dx, dy, dc) in enumerate(offs):
            cp = copy(k, me, (x ^ dx, y ^ dy, c ^ dc))
            cp.start()
            sends.append(cp)
        for k, (dx, dy, dc) in enumerate(offs):
            copy(k, 4 * (x ^ dx) + 2 * (y ^ dy) + (c ^ dc), (x, y, c)).wait_recv()
        for cp in sends:
            cp.wait_send()
        mine.wait()
        if reduce:
            acc = slots[0]
            for d in range(1, NDEV):
                acc = acc + slots[d]
            out_ref[...] = acc
        else:
            out_ref[...] = slots[...]
        if gathered is not None:
            acc = g_ref[0]
            for d in range(1, NDEV):
                acc = acc + g_ref[d]
            gsum_ref[...] = acc

    vmem = pl.BlockSpec(memory_space=pltpu.VMEM)
    out = jax.ShapeDtypeStruct((rows, D) if reduce else (NDEV, rows, D), F32)
    args, out_shape, out_specs = [v], out, vmem
    if gathered is not None:
        args.append(gathered)
        out_shape, out_specs = [out, jax.ShapeDtypeStruct(gathered.shape[1:], F32)], [vmem, vmem]
    return pl.pallas_call(
        body, name=name, out_shape=out_shape, in_specs=[vmem] * len(args), out_specs=out_specs,
        scratch_shapes=[pltpu.VMEM((NDEV, rows, D), F32), pltpu.SemaphoreType.DMA((7,)),
                        pltpu.SemaphoreType.DMA((7,)), pltpu.SemaphoreType.DMA],
    )(*args)


def reduce_scatter_d2d(grads, name, collective_id):
    n = len(grads)

    def body(*refs):
        ins, outs = refs[:n], refs[n:2 * n]
        send_sems, recv_sems = refs[2 * n:]
        x, y, c = _pos()
        _handshake([(x, y, 1 - c)])
        cps = []
        for w in range(n):
            cp = pltpu.make_async_remote_copy(
                src_ref=ins[w].at[1 - c], dst_ref=outs[w], send_sem=send_sems.at[w], recv_sem=recv_sems.at[w],
                device_id=(x, y, 1 - c), device_id_type=MESH)
            cp.start()
            cps.append(cp)
        for cp in cps:
            cp.wait_recv()
        for cp in cps:
            cp.wait_send()

    out = [jax.ShapeDtypeStruct(g.shape[1:], F32) for g in grads]
    return _sequencer(body, out, n, name, collective_id)(*grads)


def reduce_scatter_ici(parts, name, collective_id):
    n = len(parts)

    def body(*refs):
        ins, outs = refs[:n], refs[n:2 * n]
        send_sems, recv_sems = refs[2 * n:]
        x, y, c = _pos()
        offs = [(1, 0), (0, 1), (1, 1)]
        _handshake([(x ^ dx, y ^ dy, c) for dx, dy in offs])
        cps = []
        for w in range(n):
            for k, (dx, dy) in enumerate(offs):
                ox, oy = x ^ dx, y ^ dy
                cp = pltpu.make_async_remote_copy(
                    src_ref=ins[w].at[2 * ox + oy], dst_ref=outs[w].at[2 * x + y],
                    send_sem=send_sems.at[3 * w + k], recv_sem=recv_sems.at[3 * w + k],
                    device_id=(ox, oy, c), device_id_type=MESH)
                cp.start()
                cps.append((cp, w, k, ox, oy))
        for cp, w, k, ox, oy in cps:
            pltpu.make_async_remote_copy(
                src_ref=ins[w].at[2 * ox + oy], dst_ref=outs[w].at[2 * ox + oy],
                send_sem=send_sems.at[3 * w + k], recv_sem=recv_sems.at[3 * w + k],
                device_id=(ox, oy, c), device_id_type=MESH).wait_recv()
        for cp, *_ in cps:
            cp.wait_send()

    out = [jax.ShapeDtypeStruct(p.shape, BF16) for p in parts]
    return _sequencer(body, out, 3 * n, name, collective_id)(*parts)


def fwd_in(x, g1, w_in_pieces, tm):
    t = x.shape[0]
    tile = lambda i: (i, 0)
    w_spec = pl.BlockSpec((NDEV, D, CG), lambda i: (0, 0, 0))
    z_spec = pl.BlockSpec((NDEV, tm, CG), lambda i: (0, i, 0))
    z_shape = jax.ShapeDtypeStruct((NDEV, t, CG), BF16)

    def cost(other_bytes, transcendentals):
        return pl.CostEstimate(flops=2 * t * D * NDEV * CG, transcendentals=transcendentals,
                               bytes_accessed=other_bytes + 2 * D * NDEV * CG + 2 * t * NDEV * CG)

    def first(x_ref, g_ref, w_ref, z_ref, h_ref, ht_ref):
        xf = x_ref[...]
        h = (xf * _rms_inv(xf) * g_ref[...]).astype(BF16)
        h_ref[...] = h
        ht_ref[...] = h.T
        for j in range(NDEV):
            z_ref[j] = _dot(h, w_ref[j]).astype(BF16)

    z0, h, ht = pl.pallas_call(
        first, name="fwd_in_0", grid=(t // tm,),
        in_specs=[pl.BlockSpec((tm, D), tile), pl.BlockSpec((1, D), lambda i: (0, 0)), w_spec],
        out_specs=[z_spec, pl.BlockSpec((tm, D), tile), pl.BlockSpec((D, tm), lambda i: (0, i))],
        out_shape=[z_shape, jax.ShapeDtypeStruct((t, D), BF16), jax.ShapeDtypeStruct((D, t), BF16)],
        compiler_params=_cparams(1, VMEM_BIG), cost_estimate=cost(8 * t * D + 4 * D, t),
    )(x, g1, w_in_pieces[0])
    zs = [z0]
    for q in (1, 2):
        h, zs[-1] = lax.optimization_barrier((h, zs[-1]))

        def later(h_ref, w_ref, z_ref):
            hb = h_ref[...]
            for j in range(NDEV):
                z_ref[j] = _dot(hb, w_ref[j]).astype(BF16)

        zs.append(pl.pallas_call(
            later, name=f"fwd_in_{q}", grid=(t // tm,),
            in_specs=[pl.BlockSpec((tm, D), tile), w_spec], out_specs=z_spec, out_shape=z_shape,
            compiler_params=_cparams(1, VMEM_BIG), cost_estimate=cost(2 * t * D, 0),
        )(h, w_in_pieces[q]))
    return zs, ht


def _z_tile(z_refs, n):
    return z_refs[n % 3][n // 3]


def _pool_tile(z, zh, win, keep_hist, cnt):
    zt = z.astype(F32)
    ext = jnp.concatenate([zh.astype(F32) * keep_hist, zt], axis=0)
    s, sh = ext, 1
    while sh < win:
        s = s + pltpu.roll(s, sh, 0)
        sh *= 2
    return s[HALO:] / cnt - zt


def _conv_taps(ext, cur, w_ref, lanes, lead):
    x1 = _shift_down(ext, 1, lead)
    x2 = _shift_down(ext, 2, lead)
    out = w_ref[2:3, lanes] * cur + w_ref[1:2, lanes] * x1 + w_ref[0:1, lanes] * x2
    return out, x1, x2


def fwd_mix(zs, x, pool_w, pool_scale, conv_w, wmix, tm, seq):
    t = x.shape[0]
    tps = seq // tm
    hb = tm // HALO

    def body(z0_ref, z1_ref, z2_ref, zh0_ref, zh1_ref, zh2_ref, x_ref, pw_ref, ps_ref, wpp_ref, cw_ref, wco_ref,
             wo_ref, x1_ref, yp_ref, yc_ref):
        z_refs, zh_refs = (z0_ref, z1_ref, z2_ref), (zh0_ref, zh1_ref, zh2_ref)
        it = pl.program_id(0) % tps
        keep_hist = jnp.where(it == 0, 0.0, 1.0)
        pos = it * tm + lax.broadcasted_iota(jnp.int32, (tm, 1), 0)
        p2 = []
        for g, win in enumerate(WINS):
            cnt = jnp.minimum(pos + 1, win).astype(F32)
            p = _pool_tile(_z_tile(z_refs, g), _z_tile(zh_refs, g), win, keep_hist, cnt)
            lanes = slice(g * CG, (g + 1) * CG)
            p2.append((_dot(p.astype(BF16), pw_ref[g]) * ps_ref[:, lanes]).astype(BF16))
        y_pool = _dot(jnp.concatenate(p2, axis=1), wpp_ref[...])
        u = []
        for q in range(NG):
            lanes = slice(q * CG, (q + 1) * CG)
            cv = _z_tile(z_refs, 8 + q).astype(F32) * _z_tile(z_refs, 12 + q).astype(F32)
            cvh = _z_tile(zh_refs, 8 + q).astype(F32) * _z_tile(zh_refs, 12 + q).astype(F32) * keep_hist
            cc, _, _ = _conv_taps(jnp.concatenate([cvh, cv], axis=0), cv, cw_ref, lanes, HALO)
            u.append((_z_tile(z_refs, 4 + q).astype(F32) * cc).astype(BF16))
        y_conv = _dot(jnp.concatenate(u, axis=1), wco_ref[...])
        ypb, ycb = y_pool.astype(BF16), y_conv.astype(BF16)
        yp_ref[...] = ypb
        yc_ref[...] = ycb
        merged = []
        for q in range(NG):
            lanes = slice(q * CG, (q + 1) * CG)
            sp = jax.nn.sigmoid(_z_tile(z_refs, 16 + q).astype(F32))
            sc = jax.nn.sigmoid(_z_tile(z_refs, 20 + q).astype(F32))
            merged.append((sp * ypb[:, lanes].astype(F32) + sc * ycb[:, lanes].astype(F32)).astype(BF16))
        x1_ref[...] = x_ref[...] + _dot(jnp.concatenate(merged, axis=1), wo_ref[...])

    def hist(i):
        return jnp.maximum(i * hb - 1, 0)

    const2 = lambda i: (0, 0)
    return pl.pallas_call(
        body, name="fwd_mix", grid=(t // tm,),
        in_specs=[pl.BlockSpec((NDEV, tm, CG), lambda i: (0, i, 0))] * 3
                 + [pl.BlockSpec((NDEV, HALO, CG), lambda i: (0, hist(i), 0))] * 3
                 + [pl.BlockSpec((tm, D), lambda i: (i, 0)),
                    pl.BlockSpec((NG, CG, CG), lambda i: (0, 0, 0)), pl.BlockSpec((1, D), const2),
                    pl.BlockSpec((D, D), lambda i: (0, MIX_POOL_PROJ)), pl.BlockSpec((3, D), const2),
                    pl.BlockSpec((D, D), lambda i: (0, MIX_CONV_OUT)), pl.BlockSpec((D, D), lambda i: (0, MIX_O))],
        out_specs=[pl.BlockSpec((tm, D), lambda i: (i, 0))] * 3,
        out_shape=[jax.ShapeDtypeStruct((t, D), F32), jax.ShapeDtypeStruct((t, D), BF16),
                   jax.ShapeDtypeStruct((t, D), BF16)],
        compiler_params=_cparams(1, VMEM_BIG),
    )(*zs, *zs, x, pool_w, pool_scale, wmix, conv_w, wmix, wmix)


def fwd_up(x1, g2, w_up_g, fcw, fcb, tm, seq):
    t = x1.shape[0]
    tps = seq // tm

    def body(x1_ref, g2_ref, wup_ref, fcw_ref, fcb_ref, up_ref, pre_ref, act_ref, actt_ref, h2t_ref, hist_ref):
        i = pl.program_id(0)
        keep_hist = jnp.where(i % tps == 0, 0.0, 1.0)

        @pl.when(i == 0)
        def _():
            hist_ref[...] = jnp.zeros_like(hist_ref)

        x1v = x1_ref[...]
        h2 = (x1v * _rms_inv(x1v) * g2_ref[...]).astype(BF16)
        h2t_ref[...] = h2.T
        lanes = slice(0, SH_UP)
        for c in range(NCH):
            conv = []
            for s in range(2):
                ub = _dot(h2, wup_ref[s, c]).astype(BF16)
                up_ref[s, c] = ub
                uf = ub.astype(F32)
                ext = jnp.concatenate([hist_ref[s, c] * keep_hist, uf], axis=0)
                hist_ref[s, c] = uf[tm - 8:]
                cc, _, _ = _conv_taps(ext, uf, fcw_ref.at[s, c], lanes, 8)
                conv.append(cc + fcb_ref[s, c])
                pre_ref[s, c] = conv[s].astype(BF16)
            a = (conv[0] * jax.nn.sigmoid(conv[0]) * conv[1]).astype(BF16)
            act_ref[c] = a
            actt_ref[c] = a.T

    tile = lambda i: (i, 0)
    const2 = lambda i: (0, 0)
    whole = lambda i: (0, 0, 0, 0)
    chunks = pl.BlockSpec((2, NCH, tm, SH_UP), lambda i: (0, 0, i, 0))
    return pl.pallas_call(
        body, name="fwd_up", grid=(t // tm,),
        in_specs=[pl.BlockSpec((tm, D), tile), pl.BlockSpec((1, D), const2),
                  pl.BlockSpec((2, NCH, D, SH_UP), whole), pl.BlockSpec((2, NCH, 3, SH_UP), whole),
                  pl.BlockSpec((2, NCH, 1, SH_UP), whole)],
        out_specs=[chunks, chunks, pl.BlockSpec((NCH, tm, SH_UP), lambda i: (0, i, 0)),
                   pl.BlockSpec((NCH, SH_UP, tm), lambda i: (0, 0, i)), pl.BlockSpec((D, tm), lambda i: (0, i))],
        out_shape=[jax.ShapeDtypeStruct((2, NCH, t, SH_UP), BF16), jax.ShapeDtypeStruct((2, NCH, t, SH_UP), BF16),
                   jax.ShapeDtypeStruct((NCH, t, SH_UP), BF16), jax.ShapeDtypeStruct((NCH, SH_UP, t), BF16),
                   jax.ShapeDtypeStruct((D, t), BF16)],
        scratch_shapes=[pltpu.VMEM((2, NCH, 8, SH_UP), F32)],
        compiler_params=_cparams(1, VMEM_BIG),
    )(x1, g2, w_up_g.reshape(2, NCH, D, SH_UP), fcw.reshape(2, NCH, 3, SH_UP), fcb.reshape(2, NCH, 1, SH_UP))


def fwd_down(x1, act, w_dn, gf, tgt, tm):
    t = x1.shape[0]

    def body(x1_ref, act_ref, wdn_ref, gf_ref, tgt_ref, dx2_ref, dx2b_ref, vec_ref):
        @pl.when(pl.program_id(0) == 0)
        def _():
            vec_ref[...] = jnp.zeros_like(vec_ref)

        d = None
        for c in range(NCH):
            part = _dot(act_ref[c], wdn_ref[c])
            d = part if d is None else d + part
        x2 = x1_ref[...] + d
        inv3 = _rms_inv(x2)
        xn = x2 * inv3
        diff = xn * gf_ref[...] - tgt_ref[...]
        dy = diff * (1.0 / D)
        vec_ref[0:1, :] += jnp.sum(dy * xn, axis=0, keepdims=True)
        vec_ref[1:2, :] += 0.5 * jnp.sum(jnp.mean(diff * diff, axis=-1))
        dx2 = _rms_bwd(dy, xn, inv3, gf_ref[...])
        dx2_ref[...] = dx2
        dx2b_ref[...] = dx2.astype(BF16)

    tile = lambda i: (i, 0)
    const2 = lambda i: (0, 0)
    return pl.pallas_call(
        body, name="fwd_down", grid=(t // tm,),
        in_specs=[pl.BlockSpec((tm, D), tile), pl.BlockSpec((NCH, tm, SH_UP), lambda i: (0, i, 0)),
                  pl.BlockSpec((NCH, SH_UP, D), lambda i: (0, 0, 0)), pl.BlockSpec((1, D), const2),
                  pl.BlockSpec((tm, D), tile)],
        out_specs=[pl.BlockSpec((tm, D), tile), pl.BlockSpec((tm, D), tile), pl.BlockSpec((8, D), const2)],
        out_shape=[jax.ShapeDtypeStruct((t, D), F32), jax.ShapeDtypeStruct((t, D), BF16),
                   jax.ShapeDtypeStruct((8, D), F32)],
        compiler_params=_cparams(1, VMEM_BIG),
    )(x1, act, w_dn, gf, tgt)


def bwd_ffn(dx2, x1, g2, up, pre, w_up_g, fcw, w_dn, tm, seq):
    t = x1.shape[0]
    nt = t // tm
    tps = seq // tm

    def body(dx2_ref, x1_ref, g2_ref, up_ref, pre_ref, wup_ref, fcw_ref, wdn_ref,
             dup_ref, dx1_ref, gvec_ref, gn_ref, carry_ref):
        i = pl.program_id(0)
        it = (nt - 1 - i) % tps
        keep_next = jnp.where(it == tps - 1, 0.0, 1.0)

        @pl.when(i == 0)
        def _():
            gvec_ref[...] = jnp.zeros_like(gvec_ref)
            gn_ref[...] = jnp.zeros_like(gn_ref)
            carry_ref[...] = jnp.zeros_like(carry_ref)

        dx2v = dx2_ref[...]
        dxb = dx2v.astype(BF16)
        lanes = slice(0, SH_UP)
        dh2 = None
        for c in range(NCH):
            pre = [pre_ref[s, c].astype(F32) for s in range(2)]
            sg = jax.nn.sigmoid(pre[0])
            dact = _dot_nt(dxb, wdn_ref[c])
            dpre = [dact * pre[1] * (sg * (1.0 + pre[0] * (1.0 - sg))), dact * (pre[0] * sg)]
            for s in range(2):
                dc = dpre[s]
                ext = jnp.concatenate([dc, carry_ref[s, c] * keep_next], axis=0)
                carry_ref[s, c] = dc[:8]
                shifted = (_shift_up(ext, 2, tm), _shift_up(ext, 1, tm), dc)
                uf = up_ref[s, c].astype(F32)
                gvec_ref[s, c, 0:1, lanes] += jnp.sum(dc, axis=0, keepdims=True)
                for tap in range(3):
                    gvec_ref[s, c, tap + 1:tap + 2, lanes] += jnp.sum(shifted[tap] * uf, axis=0, keepdims=True)
                w = fcw_ref.at[s, c]
                du = w[2:3, :] * dc + w[1:2, :] * shifted[1] + w[0:1, :] * shifted[0]
                dub = du.astype(BF16)
                dup_ref[s, c] = dub
                part = _dot_nt(dub, wup_ref[s, c])
                dh2 = part if dh2 is None else dh2 + part
        x1v = x1_ref[...]
        inv2 = _rms_inv(x1v)
        xn = x1v * inv2
        gn_ref[0:1, :] += jnp.sum(dh2 * xn, axis=0, keepdims=True)
        dx1_ref[...] = dx2v + _rms_bwd(dh2, xn, inv2, g2_ref[...])

    rev = lambda i: (nt - 1 - i, 0)
    const2 = lambda i: (0, 0)
    whole = lambda i: (0, 0, 0, 0)
    chunks = pl.BlockSpec((2, NCH, tm, SH_UP), lambda i: (0, 0, nt - 1 - i, 0))
    return pl.pallas_call(
        body, name="bwd_ffn", grid=(nt,),
        in_specs=[pl.BlockSpec((tm, D), rev), pl.BlockSpec((tm, D), rev), pl.BlockSpec((1, D), const2),
                  chunks, chunks, pl.BlockSpec((2, NCH, D, SH_UP), whole), pl.BlockSpec((2, NCH, 3, SH_UP), whole),
                  pl.BlockSpec((NCH, SH_UP, D), lambda i: (0, 0, 0))],
        out_specs=[chunks, pl.BlockSpec((tm, D), rev), pl.BlockSpec((2, NCH, 8, D), whole),
                   pl.BlockSpec((8, D), const2)],
        out_shape=[jax.ShapeDtypeStruct((2, NCH, t, SH_UP), BF16), jax.ShapeDtypeStruct((t, D), F32),
                   jax.ShapeDtypeStruct((2, NCH, 8, D), F32), jax.ShapeDtypeStruct((8, D), F32)],
        scratch_shapes=[pltpu.VMEM((2, NCH, 8, SH_UP), F32)],
        compiler_params=_cparams(1, VMEM_BIG),
    )(dx2, x1, g2, up, pre, w_up_g.reshape(2, NCH, D, SH_UP), fcw.reshape(2, NCH, 3, SH_UP), w_dn)


def bwd_mix(dx1, zs, y_pool, y_conv, pool_w, pool_scale, conv_w, wmix, tm, seq):
    t = dx1.shape[0]
    nt = t // tm
    tps = seq // tm
    hb = tm // HALO

    def body(da_ref, z0_ref, z1_ref, z2_ref, zh0_ref, zh1_ref, zh2_ref, yp_ref, yc_ref, pw_ref, ps_ref, wpp_ref,
             cw_ref, wco_ref, wo_ref,
             dz_ref, mg_ref, p2_ref, u_ref, dyp_ref, dyc_ref, p_ref, dpw_ref, gvec_ref, cp_ref, cc_ref):
        z_refs, zh_refs = (z0_ref, z1_ref, z2_ref), (zh0_ref, zh1_ref, zh2_ref)
        i = pl.program_id(0)
        it = (nt - 1 - i) % tps
        keep_hist = jnp.where(it == 0, 0.0, 1.0)
        keep_next = jnp.where(it == tps - 1, 0.0, 1.0)
        pos = it * tm + lax.broadcasted_iota(jnp.int32, (tm, 1), 0)

        @pl.when(i == 0)
        def _():
            gvec_ref[...] = jnp.zeros_like(gvec_ref)
            cp_ref[...] = jnp.zeros_like(cp_ref)
            cc_ref[...] = jnp.zeros_like(cc_ref)

        dm = _dot_nt(da_ref[...].astype(BF16), wo_ref[...])
        merged, dyp, dyc = [], [], []
        for q in range(NG):
            lanes = slice(q * CG, (q + 1) * CG)
            sp = jax.nn.sigmoid(_z_tile(z_refs, 16 + q).astype(F32))
            sc = jax.nn.sigmoid(_z_tile(z_refs, 20 + q).astype(F32))
            yp = yp_ref[:, lanes].astype(F32)
            yc = yc_ref[:, lanes].astype(F32)
            dmq = dm[:, lanes]
            merged.append((sp * yp + sc * yc).astype(BF16))
            dyp.append((dmq * sp).astype(BF16))
            dyc.append((dmq * sc).astype(BF16))
            dz_ref[16 + q] = (dmq * yp * (sp * (1.0 - sp))).astype(BF16)
            dz_ref[20 + q] = (dmq * yc * (sc * (1.0 - sc))).astype(BF16)
        mg_ref[...] = jnp.concatenate(merged, axis=1)
        dypb = jnp.concatenate(dyp, axis=1)
        dycb = jnp.concatenate(dyc, axis=1)
        dyp_ref[...] = dypb
        dyc_ref[...] = dycb

        dp2 = _dot_nt(dypb, wpp_ref[...])
        p2 = []
        for g, win in enumerate(WINS):
            lanes = slice(g * CG, (g + 1) * CG)
            cnt = jnp.minimum(pos + 1, win).astype(F32)
            p = _pool_tile(_z_tile(z_refs, g), _z_tile(zh_refs, g), win, keep_hist, cnt)
            pb = p.astype(BF16)
            p_ref[g] = pb
            pw = _dot(pb, pw_ref[g])
            p2.append((pw * ps_ref[:, lanes]).astype(BF16))
            dp2g = dp2[:, lanes]
            gvec_ref[0:1, lanes] += jnp.sum(dp2g * pw, axis=0, keepdims=True)
            dpwb = (dp2g * ps_ref[:, lanes]).astype(BF16)
            dpw_ref[g] = dpwb
            dp = _dot_nt(dpwb, pw_ref[g])
            qv = dp / cnt
            ext = jnp.concatenate([qv, cp_ref[g] * keep_next], axis=0)
            cp_ref[g] = qv[:HALO]
            n = tm + HALO
            s, sh = ext, 1
            while sh < win:
                s = s + pltpu.roll(s, n - sh, 0)
                sh *= 2
            dz_ref[g] = (s[:tm] - dp).astype(BF16)
        p2_ref[...] = jnp.concatenate(p2, axis=1)

        du = _dot_nt(dycb, wco_ref[...])
        u = []
        for q in range(NG):
            lanes = slice(q * CG, (q + 1) * CG)
            zb = _z_tile(z_refs, 4 + q).astype(F32)
            zc = _z_tile(z_refs, 8 + q).astype(F32)
            zv = _z_tile(z_refs, 12 + q).astype(F32)
            cv = zc * zv
            cvh = _z_tile(zh_refs, 8 + q).astype(F32) * _z_tile(zh_refs, 12 + q).astype(F32) * keep_hist
            cc, cv1, cv2 = _conv_taps(jnp.concatenate([cvh, cv], axis=0), cv, cw_ref, lanes, HALO)
            u.append((zb * cc).astype(BF16))
            duq = du[:, lanes]
            dz_ref[4 + q] = (duq * cc).astype(BF16)
            dcc = duq * zb
            for tap, src in enumerate((cv2, cv1, cv)):
                gvec_ref[tap + 1:tap + 2, lanes] += jnp.sum(dcc * src, axis=0, keepdims=True)
            ext = jnp.concatenate([dcc, cc_ref[:, lanes] * keep_next], axis=0)
            cc_ref[:, lanes] = dcc[:8]
            dcv = (cw_ref[2:3, lanes] * dcc + cw_ref[1:2, lanes] * _shift_up(ext, 1, tm)
                   + cw_ref[0:1, lanes] * _shift_up(ext, 2, tm))
            dz_ref[8 + q] = (dcv * zv).astype(BF16)
            dz_ref[12 + q] = (dcv * zc).astype(BF16)
        u_ref[...] = jnp.concatenate(u, axis=1)

    def hist(i):
        return jnp.maximum((nt - 1 - i) * hb - 1, 0)

    rev = lambda i: (nt - 1 - i, 0)
    rev3 = lambda i: (0, nt - 1 - i, 0)
    const2 = lambda i: (0, 0)
    tok = jax.ShapeDtypeStruct((t, D), BF16)
    grp = jax.ShapeDtypeStruct((NG, t, CG), BF16)
    return pl.pallas_call(
        body, name="bwd_mix", grid=(nt,),
        in_specs=[pl.BlockSpec((tm, D), rev)] + [pl.BlockSpec((NDEV, tm, CG), rev3)] * 3
                 + [pl.BlockSpec((NDEV, HALO, CG), lambda i: (0, hist(i), 0))] * 3
                 + [pl.BlockSpec((tm, D), rev), pl.BlockSpec((tm, D), rev),
                    pl.BlockSpec((NG, CG, CG), lambda i: (0, 0, 0)), pl.BlockSpec((1, D), const2),
                    pl.BlockSpec((D, D), lambda i: (0, MIX_POOL_PROJ)), pl.BlockSpec((3, D), const2),
                    pl.BlockSpec((D, D), lambda i: (0, MIX_CONV_OUT)), pl.BlockSpec((D, D), lambda i: (0, MIX_O))],
        out_specs=[pl.BlockSpec((NZT, tm, CG), rev3)] + [pl.BlockSpec((tm, D), rev)] * 5
                  + [pl.BlockSpec((NG, tm, CG), rev3)] * 2 + [pl.BlockSpec((8, D), const2)],
        out_shape=[jax.ShapeDtypeStruct((NZT, t, CG), BF16), tok, tok, tok, tok, tok, grp, grp,
                   jax.ShapeDtypeStruct((8, D), F32)],
        scratch_shapes=[pltpu.VMEM((NG, HALO, CG), F32), pltpu.VMEM((8, D), F32)],
        compiler_params=_cparams(1, VMEM_BIG),
    )(dx1, *zs, *zs, y_pool, y_conv, pool_w, pool_scale, wmix, conv_w, wmix, wmix)


def bwd_in(dz, w_in_pieces, dx1, x, g1, tm):
    t = x.shape[0]

    def body(dz_ref, w0_ref, w1_ref, w2_ref, dx1_ref, x_ref, g_ref, gx_ref, gn_ref):
        @pl.when(pl.program_id(0) == 0)
        def _():
            gn_ref[...] = jnp.zeros_like(gn_ref)

        dh = None
        for j in range(NDEV):
            for q, w_ref in enumerate((w0_ref, w1_ref, w2_ref)):
                part = _dot_nt(dz_ref[3 * j + q], w_ref[j])
                dh = part if dh is None else dh + part
        xv = x_ref[...]
        inv = _rms_inv(xv)
        xn = xv * inv
        gn_ref[0:1, :] += jnp.sum(dh * xn, axis=0, keepdims=True)
        gx_ref[...] = dx1_ref[...] + _rms_bwd(dh, xn, inv, g_ref[...])

    tile = lambda i: (i, 0)
    return pl.pallas_call(
        body, name="bwd_in", grid=(t // tm,),
        in_specs=[pl.BlockSpec((NZT, tm, CG), lambda i: (0, i, 0))]
                 + [pl.BlockSpec((NDEV, D, CG), lambda i: (0, 0, 0))] * 3
                 + [pl.BlockSpec((tm, D), tile), pl.BlockSpec((tm, D), tile), pl.BlockSpec((1, D), lambda i: (0, 0))],
        out_specs=[pl.BlockSpec((tm, D), tile), pl.BlockSpec((8, D), lambda i: (0, 0))],
        out_shape=[jax.ShapeDtypeStruct((t, D), F32), jax.ShapeDtypeStruct((8, D), F32)],
        compiler_params=_cparams(1, VMEM_BIG),
    )(dz, *w_in_pieces, dx1, x, g1)


def _slot(j):
    return j % 2, j // 2


def wgrad_cols(at, b, q, name):
    m, t = at.shape
    width = b.shape[3]

    def body(a_ref, b_ref, o_ref):
        o_ref[...] = _dot(a_ref[...], b_ref[...])

    return pl.pallas_call(
        body, name=name, grid=(NDEV,),
        in_specs=[pl.BlockSpec((m, t), lambda j: (0, 0)),
                  pl.BlockSpec((None, None, t, width), lambda j: (j, q, 0, 0))],
        out_specs=pl.BlockSpec((None, None, m, width), lambda j: (j % 2, j // 2, 0, 0)),
        out_shape=jax.ShapeDtypeStruct((2, 4, m, width), F32),
        compiler_params=_cparams(1, VMEM_BIG),
    )(at, b)


def wgrad_down(actt, dx2b):
    t = dx2b.shape[0]

    def body(a_ref, b_ref, o_ref):
        r = _dot(a_ref[...], b_ref[...])
        o_ref[0] = r[:SH_DN]
        o_ref[1] = r[SH_DN:]

    return pl.pallas_call(
        body, name="wgrad_down", grid=(NCH,),
        in_specs=[pl.BlockSpec((None, SH_UP, t), lambda k: (k, 0, 0)), pl.BlockSpec((t, D), lambda k: (0, 0))],
        out_specs=pl.BlockSpec((2, None, SH_DN, D), lambda k: (0, k, 0, 0)),
        out_shape=jax.ShapeDtypeStruct((2, 4, SH_DN, D), F32),
        compiler_params=_cparams(1, VMEM_BIG),
    )(actt, dx2b)


def wgrad_square(a, b, name, tk):
    t = a.shape[0]

    def body(a_ref, b_ref, o_ref, acc_ref):
        kt = pl.program_id(0)

        @pl.when(kt == 0)
        def _():
            acc_ref[...] = jnp.zeros_like(acc_ref)

        acc_ref[...] += _dot_tn(a_ref[...], b_ref[...].astype(BF16))

        @pl.when(kt == pl.num_programs(0) - 1)
        def _():
            for j in range(NDEV):
                cc, xy = _slot(j)
                o_ref[cc, xy] = acc_ref[j * 128:(j + 1) * 128]

    return pl.pallas_call(
        body, name=name, grid=(t // tk,),
        in_specs=[pl.BlockSpec((tk, D), lambda k: (k, 0)), pl.BlockSpec((tk, D), lambda k: (k, 0))],
        out_specs=pl.BlockSpec((2, 4, 128, D), lambda k: (0, 0, 0, 0)),
        out_shape=jax.ShapeDtypeStruct((2, 4, 128, D), F32),
        scratch_shapes=[pltpu.VMEM((D, D), F32)],
        compiler_params=_cparams(1, VMEM_BIG),
    )(a, b)


def wgrad_pool(p, dpw, tk):
    t = p.shape[1]

    def body(a_ref, b_ref, o_ref):
        @pl.when(pl.program_id(0) == 0)
        def _():
            o_ref[...] = jnp.zeros_like(o_ref)

        for g in range(NG):
            o_ref[g] += _dot_tn(a_ref[g], b_ref[g])

    return pl.pallas_call(
        body, name="wgrad_pool", grid=(t // tk,),
        in_specs=[pl.BlockSpec((NG, tk, CG), lambda k: (0, k, 0))] * 2,
        out_specs=pl.BlockSpec((NG, CG, CG), lambda k: (0, 0, 0)),
        out_shape=jax.ShapeDtypeStruct((NG, CG, CG), F32),
        compiler_params=_cparams(1, VMEM_BIG),
    )(p, dpw)


def _adamw(w, g, m, v):
    m = ADAM_B1 * m + (1.0 - ADAM_B1) * g
    v = ADAM_B2 * v + (1.0 - ADAM_B2) * (g * g)
    m_hat = m / (1.0 - ADAM_B1 ** ADAM_STEP)
    v_hat = v / (1.0 - ADAM_B2 ** ADAM_STEP)
    delta = -ADAM_LR * (m_hat / (jnp.sqrt(v_hat) + ADAM_EPS) + ADAM_WD * w)
    return delta, m, v


def _row_block(r):
    return 512 if r % 512 == 0 else r


def chip_partial(place, g, from_sibling, name):
    _, _, r, c = g.shape
    br = 256 if r % 256 == 0 else r

    def body(place_ref, g_ref, s_ref, o_ref):
        o_ref[...] = (g_ref[...] + s_ref[...]).astype(BF16)

    return pl.pallas_call(
        body, name=name,
        grid_spec=pltpu.PrefetchScalarGridSpec(
            num_scalar_prefetch=1, grid=(3, r // br),
            in_specs=[pl.BlockSpec((None, None, br, c), lambda k, i, pr: (pr[0], pr[1] ^ (k + 1), i, 0)),
                      pl.BlockSpec((None, br, c), lambda k, i, pr: (pr[1] ^ (k + 1), i, 0))],
            out_specs=pl.BlockSpec((None, br, c), lambda k, i, pr: (pr[1] ^ (k + 1), i, 0))),
        out_shape=jax.ShapeDtypeStruct((4, r, c), BF16),
        compiler_params=_cparams(2, VMEM_BIG),
    )(place, g, from_sibling)


def finish_adamw(place, gs, from_sibling, from_chips, w, m, v, name, transposed=False):
    n = len(gs)
    r = gs[0].shape[2]
    widths = [g.shape[3] for g in gs]
    c = sum(widths)
    br = _row_block(r)

    def body(place_ref, *refs):
        g_refs, s_refs, c_refs = refs[:n], refs[n:2 * n], refs[2 * n:5 * n]
        w_ref, m_ref, v_ref, og_ref, od_ref, om_ref, ov_ref = refs[5 * n:]
        cols = []
        for q in range(n):
            grad = g_refs[q][...] + s_refs[q][...]
            for k in range(3):
                grad = grad + c_refs[3 * q + k][...].astype(F32)
            cols.append(grad)
        grad = cols[0] if n == 1 else jnp.concatenate(cols, axis=1)
        if transposed:
            grad = grad.T
        og_ref[...] = grad
        od_ref[...], om_ref[...], ov_ref[...] = _adamw(w_ref[...], grad, m_ref[...], v_ref[...])

    def other(k, cq):
        return pl.BlockSpec((None, br, cq), lambda i, pr: (pr[1] ^ k, i, 0))

    row = pl.BlockSpec((c, br), lambda i, pr: (0, i)) if transposed else pl.BlockSpec((br, c), lambda i, pr: (i, 0))
    out = jax.ShapeDtypeStruct((c, r) if transposed else (r, c), F32)
    in_specs = [pl.BlockSpec((None, None, br, cq), lambda i, pr: (pr[0], pr[1], i, 0)) for cq in widths]
    in_specs += [pl.BlockSpec((None, br, cq), lambda i, pr: (pr[1], i, 0)) for cq in widths]
    in_specs += [other(k, cq) for cq in widths for k in (1, 2, 3)]
    return pl.pallas_call(
        body, name=name,
        grid_spec=pltpu.PrefetchScalarGridSpec(
            num_scalar_prefetch=1, grid=(r // br,), in_specs=in_specs + [row, row, row], out_specs=[row] * 4),
        out_shape=[out] * 4,
        compiler_params=_cparams(1, VMEM_BIG),
    )(place, *gs, *from_sibling, *[fc for fc in from_chips for _ in range(3)], w, m, v)


def adamw_small(items):
    n = len(items)

    def body(*refs):
        ins, outs = refs[:4 * n], refs[4 * n:]
        for i in range(n):
            w, g, m, v = (r[...] for r in ins[4 * i:4 * i + 4])
            outs[3 * i][...], outs[3 * i + 1][...], outs[3 * i + 2][...] = _adamw(w, g, m, v)

    out = [jax.ShapeDtypeStruct(it[0].shape, F32) for it in items for _ in range(3)]
    res = pl.pallas_call(body, name="adamw_small", out_shape=out)(*[a for it in items for a in it])
    return [res[3 * i:3 * i + 3] for i in range(n)]


def kernel(x, norm_mix, w_in, pool_w, pool_scale, w_pool_proj, conv_w, w_conv_out, w_o, norm_ffn, w_up, ffn_conv_w, ffn_conv_b, w_down, norm_final, loss_target, m_norm_mix, m_w_in, m_pool_w, m_pool_scale, m_w_pool_proj, m_conv_w, m_w_conv_out, m_w_o, m_norm_ffn, m_w_up, m_ffn_conv_w, m_ffn_conv_b, m_w_down, m_norm_final, v_norm_mix, v_w_in, v_pool_w, v_pool_scale, v_w_pool_proj, v_conv_w, v_w_conv_out, v_w_o, v_norm_ffn, v_w_up, v_ffn_conv_w, v_ffn_conv_b, v_w_down, v_norm_final):
    nb, seq, _ = x.shape
    t = nb * seq
    tm_in = min(TM_IN, t)
    tm_mix = min(TM_MIX, seq)
    tm_ffn = min(TM_FFN, seq)
    tk = min(TK_WGRAD, t)
    xt = x.reshape(t, D)
    tgt = loss_target.reshape(t, D)
    xi, yi, ci = _pos()
    me = 4 * xi + 2 * yi + ci
    place = jnp.stack([ci, 2 * xi + yi]).astype(jnp.int32)

    tie = lax.optimization_barrier
    w_in_b = w_in[0].astype(BF16)
    w_in_g = [all_gather_blocks([w_in_b[:, q * CG:(q + 1) * CG]], f"all_gather_w_in_{q}", 0)[0] for q in range(3)]
    taps = (jnp.pad(conv_w[0], ((0, 5), (0, D - 128))) + jnp.pad(ffn_conv_w[0], ((3, 2), (0, D - SH_UP))))
    taps_g = _exchange_small(taps, False, "all_gather_taps")
    mix_shard = jnp.concatenate(
        [w_pool_proj[0], w_conv_out[0], w_o[0], pool_w[0].reshape(NG * 32, CG)], axis=1).astype(BF16)
    mix_shard, taps_g = tie((mix_shard, taps_g))
    wmix_g, = all_gather_blocks([mix_shard], "all_gather_w_mix", 0)
    ffn_shards, w_in_g[0] = tie(([w_up[0].astype(BF16), w_down[0].astype(BF16)], w_in_g[0]))
    w_up_g, = all_gather_blocks(ffn_shards[:1], "all_gather_w_up", 0)
    w_dn_g, = all_gather_blocks(ffn_shards[1:], "all_gather_w_down", 0)
    w_dn_f = w_dn_g.reshape(NCH, SH_UP, D)
    conv_w_f = taps_g[:, 0:3, :128].transpose(1, 0, 2).reshape(3, D)
    fcw_f = taps_g[:, 3:6, :SH_UP]
    fcb_f = ffn_conv_b.reshape(NDEV, 1, SH_UP)
    gfin = norm_final.reshape(1, D)

    zs, h1 = fwd_in(xt, norm_mix, w_in_g, tm_in)
    wmix_g, zs = tie((wmix_g, zs))
    wmix = wmix_g.reshape(D, MIX_COLS)
    pool_w_f = wmix_g[:, :, 3 * D:].reshape(NDEV, NG, 32, CG).transpose(1, 0, 2, 3).reshape(NG, CG, CG)
    x1, y_pool, y_conv = fwd_mix(zs, xt, pool_w_f, pool_scale, conv_w_f, wmix, tm_mix, seq)
    up, pre, act_tok, act, h2 = fwd_up(x1, norm_ffn, w_up_g, fcw_f, fcb_f, tm_ffn, seq)
    dx2, dx2b, ffn_vec = fwd_down(x1, act_tok, w_dn_f, gfin, tgt, min(TM_IN, t))

    def to_sibling(full, tag):
        return reduce_scatter_d2d(full, "reduce_scatter_d2d_" + tag, 1)

    def partials(full, from_sib, names):
        return [chip_partial(place, g, s, "chip_partial_" + nm) for g, s, nm in zip(full, from_sib, names)]

    def to_chips(parts, tag):
        return reduce_scatter_ici(parts, "reduce_scatter_ici_" + tag, 2)

    def finish(nm, gs, from_sib, from_chips, wmv, transposed=False):
        rc = (gs[0].shape[2], sum(g.shape[3] for g in gs))
        wmv2 = [a.reshape(rc).T if transposed else a.reshape(rc) for a in wmv]
        outs = finish_adamw(place, gs, from_sib, from_chips, *wmv2, "adamw_" + nm, transposed)
        return [(o.T if transposed else o).reshape(wmv[0].shape) for o in outs]

    def after(x, dep):
        return tie((x, dep))[0]

    big = {}
    d_up, dx1, g_ffn_vec, g_nffn = bwd_ffn(dx2, x1, norm_ffn, up, pre, w_up_g, fcw_f, w_dn_f, tm_ffn, seq)
    gw_up = wgrad_cols(h2, d_up.reshape(NDEV, 1, t, SH_UP), 0, "wgrad_up")
    sib_up = to_sibling([gw_up], "w_up")
    gw_dn = wgrad_down(act, after(dx2b, gw_up))
    sib_dn = to_sibling([after(gw_dn, sib_up)], "w_down")
    dx1, part_up = tie((dx1, partials([gw_up], sib_up, ["w_up"])))
    chips_up = to_chips(part_up, "w_up")
    dz, merged, p2, u, dyp, dyc, p, dpw, g_mix_vec = bwd_mix(
        dx1, zs, y_pool, y_conv, pool_w_f, pool_scale, conv_w_f, wmix, tm_mix, seq)
    merged, part_dn = tie((merged, partials([gw_dn], sib_dn, ["w_down"])))
    chips_dn = to_chips(part_dn, "w_down")
    gw_o = wgrad_square(merged, dx1, "wgrad_o", tk)
    gw_pp = wgrad_square(p2, dyp, "wgrad_pool_proj", tk)
    gw_co = wgrad_square(u, dyc, "wgrad_conv_out", tk)
    gw_pool = wgrad_pool(p, dpw, tk).reshape(NG, 4, 2, 32, CG).transpose(2, 1, 0, 3, 4).reshape(2, 4, NG * 32, CG)
    dz8 = dz.reshape(NDEV, 3, t, CG)
    gw_in, sib_in, chips_in = [None] * 3, [None] * 3, [None] * 3
    sib_a = to_sibling(after([gw_o, gw_pp], (chips_up, gw_pool)), "mix_a")
    sib_b = to_sibling(after([gw_co, gw_pool], sib_a), "mix_b")
    gw_in[0] = wgrad_cols(h1, dz8, 0, "wgrad_in_0")
    h1, part_a, part_b = tie((h1, partials([gw_o, gw_pp], sib_a, ["w_o", "w_pool_proj"]),
                              partials([gw_co, gw_pool], sib_b, ["w_conv_out", "pool_w"])))
    chips_a = to_chips(after(part_a, chips_dn), "mix_a")
    chips_b = to_chips(part_b, "mix_b")
    sib_in[0] = to_sibling(after([gw_in[0]], sib_b), "w_in_0")
    gw_in[1] = wgrad_cols(h1, dz8, 1, "wgrad_in_1")
    h1, part_in0, gw_in[1] = tie((h1, partials([gw_in[0]], sib_in[0], ["w_in_0"]), gw_in[1]))
    chips_in[0] = to_chips(part_in0, "w_in_0")
    sib_in[1] = to_sibling(after([gw_in[1]], sib_in[0]), "w_in_1")
    h1, big["w_down"], big["w_up"] = tie((
        h1, finish("w_down", [gw_dn], sib_dn, chips_dn, (w_down, m_w_down, v_w_down)),
        finish("w_up", [gw_up], sib_up, chips_up, (w_up, m_w_up, v_w_up), transposed=True)))
    gw_in[2] = wgrad_cols(h1, dz8, 2, "wgrad_in_2")
    sib_in[2] = to_sibling(after([gw_in[2]], (chips_a, chips_b, chips_in[0])), "w_in_2")
    dx1, part_in1, part_in2, big["w_o"], big["w_pool_proj"], big["w_conv_out"], big["pool_w"] = tie((
        dx1, partials([gw_in[1]], sib_in[1], ["w_in_1"]), partials([gw_in[2]], sib_in[2], ["w_in_2"]),
        finish("w_o", [gw_o], sib_a[:1], chips_a[:1], (w_o, m_w_o, v_w_o)),
        finish("w_pool_proj", [gw_pp], sib_a[1:], chips_a[1:], (w_pool_proj, m_w_pool_proj, v_w_pool_proj)),
        finish("w_conv_out", [gw_co], sib_b[:1], chips_b[:1], (w_conv_out, m_w_conv_out, v_w_conv_out)),
        finish("pool_w", [gw_pool], sib_b[1:], chips_b[1:], (pool_w, m_pool_w, v_pool_w))))
    chips_in[1] = to_chips(after(part_in1, sib_in[2]), "w_in_1")
    chips_in[2] = to_chips(part_in2, "w_in_2")
    small_g, = all_gather_blocks(
        [after(jnp.concatenate([g_mix_vec, g_nffn, ffn_vec, g_ffn_vec.reshape(8 * NDEV, D)], axis=0), sib_in[2])],
        "all_gather_small", 0)
    grad_x, g_nmix = bwd_in(dz, w_in_g, dx1, xt, norm_mix, min(TM_BWD_IN, t))
    grad_x, chips_in = tie((grad_x, chips_in))
    big["w_in"] = finish("w_in", gw_in, [s[0] for s in sib_in], [c[0] for c in chips_in], (w_in, m_w_in, v_w_in))

    red_n, red = _exchange_small(g_nmix, True, "all_reduce_small", gathered=small_g)
    g_norm_mix, g_pool_scale, g_norm_ffn = red_n[0:1], red[0:1], red[8:9]
    g_conv_w = lax.dynamic_slice(red, (1, me * 128), (3, 128))
    g_norm_final = red[16]
    loss = red[17, 0]
    g_fcb = red[24:].reshape(NDEV, 8, D)[:, 0, :SH_UP].reshape(1, FF2)
    g_fcw = lax.dynamic_slice(red, (25 + 8 * me, 0), (3, SH_UP))
    grads = {"norm_mix": g_norm_mix, "pool_scale": g_pool_scale, "norm_ffn": g_norm_ffn, "norm_final": g_norm_final,
             "ffn_conv_b": g_fcb, "conv_w": g_conv_w.reshape(1, 3, 128), "ffn_conv_w": g_fcw.reshape(1, 3, SH_UP)}
    small_wmv = {"norm_mix": (norm_mix, m_norm_mix, v_norm_mix), "pool_scale": (pool_scale, m_pool_scale, v_pool_scale),
                 "norm_ffn": (norm_ffn, m_norm_ffn, v_norm_ffn), "norm_final": (norm_final, m_norm_final, v_norm_final),
                 "ffn_conv_b": (ffn_conv_b, m_ffn_conv_b, v_ffn_conv_b), "conv_w": (conv_w, m_conv_w, v_conv_w),
                 "ffn_conv_w": (ffn_conv_w, m_ffn_conv_w, v_ffn_conv_w)}
    small_names = list(small_wmv)
    flat2 = lambda a: a.reshape(1, -1) if a.ndim == 1 else (a.transpose(1, 0, 2) if a.ndim == 3 else a)
    unflat = lambda o, like: o.transpose(1, 0, 2) if like.ndim == 3 else o.reshape(like.shape)
    small_out = adamw_small([(flat2(small_wmv[nm][0]), flat2(grads[nm]), flat2(small_wmv[nm][1]),
                              flat2(small_wmv[nm][2])) for nm in small_names])
    small = {nm: [unflat(o, small_wmv[nm][0]) for o in outs] for nm, outs in zip(small_names, small_out)}

    order = ["norm_mix", "w_in", "pool_w", "pool_scale", "w_pool_proj", "conv_w", "w_conv_out", "w_o", "norm_ffn",
             "w_up", "ffn_conv_w", "ffn_conv_b", "w_down", "norm_final"]
    out = [loss, grad_x.reshape(nb, seq, D)]
    out += [big[nm][0] if nm in big else grads[nm] for nm in order]
    for idx in range(3):
        out += [big[nm][idx + 1] if nm in big else small[nm][idx] for nm in order]
    return tuple(out)
```

```python
import jax
import jax.numpy as jnp
from jax import lax
from jax.experimental import pallas as pl
from jax.experimental.pallas import tpu as pltpu
from jax.experimental.pallas import tpu_sc as plsc

F32 = jnp.float32
BF16 = jnp.bfloat16

NDEV = 8
D = 1024
NG = 4
CG = 256
WINS = (2, 4, 8, 16)
DIN = 6 * D
SH_IN = DIN // NDEV
NZT = DIN // CG
FF2 = 5632
SH_UP = FF2 // NDEV
FF = FF2 // 2
NCH = 4
SH_DN = FF // NDEV
RMS_EPS = 1e-6
HALO = 16

ADAM_LR = 0.001
ADAM_B1 = 0.9
ADAM_B2 = 0.999
ADAM_EPS = 1e-08
ADAM_WD = 0.01
ADAM_STEP = 10

TM_IN = 512
TM_BWD_IN = 256
TM_MIX = 256
TM_FFN = 256
TK_WGRAD = 1024
MIX_POOL_PROJ, MIX_CONV_OUT, MIX_O = 0, 1, 2
MIX_COLS = 3 * D + CG
VMEM_BIG = 56 * 1024 * 1024
MESH = pl.DeviceIdType.MESH
ANY = pl.BlockSpec(memory_space=pl.ANY)


def _cparams(n_axes, vmem=None):
    return pltpu.CompilerParams(dimension_semantics=("arbitrary",) * n_axes, vmem_limit_bytes=vmem)


def _dot(a, b):
    return jnp.dot(a, b, preferred_element_type=F32)


def _dot_nt(a, b):
    return lax.dot_general(a, b, (((1,), (1,)), ((), ())), preferred_element_type=F32)


def _dot_tn(a, b):
    return lax.dot_general(a, b, (((0,), (0,)), ((), ())), preferred_element_type=F32)


def _shift_down(ext, s, lead):
    return pltpu.roll(ext, s, 0)[lead:]


def _shift_up(ext, s, tm):
    n = ext.shape[0]
    return pltpu.roll(ext, n - s, 0)[:tm]


def _rms_inv(x):
    return lax.rsqrt(jnp.mean(x * x, axis=-1, keepdims=True) + RMS_EPS)


def _rms_bwd(dh, xn, inv, g):
    dxn = dh * g
    return inv * (dxn - xn * jnp.mean(dxn * xn, axis=-1, keepdims=True))


def _pos():
    return lax.axis_index("x"), lax.axis_index("y"), lax.axis_index("c")


def _handshake(peers):
    barrier = pltpu.get_barrier_semaphore()
    for peer in peers:
        pl.semaphore_signal(barrier, inc=1, device_id=peer, device_id_type=MESH)
    pl.semaphore_wait(barrier, len(peers))


def _sequencer(body, out_type, n_sems, name, collective_id):
    return pl.kernel(
        body, out_type=out_type, mesh=plsc.ScalarSubcoreMesh(axis_name="sequencer", num_cores=1), name=name,
        scratch_types=[pltpu.SemaphoreType.DMA((n_sems,)), pltpu.SemaphoreType.DMA((n_sems,))],
        compiler_params=pltpu.CompilerParams(collective_id=collective_id))


def all_gather_blocks(shards, name, collective_id):
    n = len(shards)

    def body(*refs):
        ins, outs = refs[:n], refs[n:2 * n]
        send_sems, recv_sems = refs[2 * n:]
        x, y, c = _pos()
        me, sibling = (x, y, c), (x, y, 1 - c)
        first_chip, second_chip, diagonal = (x ^ (1 - c), y ^ c), (x ^ c, y ^ (1 - c)), (1 - x, 1 - y)
        first, second = (*first_chip, c), (*second_chip, c)
        _handshake([sibling, first, second])

        def copy(w, k, block, to, src=None):
            slot = outs[w].at[4 * block[0] + 2 * block[1] + block[2]]
            return pltpu.make_async_remote_copy(
                src_ref=slot if src is None else src, dst_ref=slot,
                send_sem=send_sems.at[8 * w + k], recv_sem=recv_sems.at[8 * w + k], device_id=to, device_id_type=MESH)

        mine, sent = [], []
        for w in range(n):
            m = pltpu.make_async_copy(ins[w], outs[w].at[4 * x + 2 * y + c], send_sems.at[8 * w + 7])
            m.start()
            mine.append(m)
            sent += [copy(w, k, me, to, src=ins[w]) for k, to in enumerate((sibling, first, second))]
        for cp in sent:
            cp.start()
        for k, chip in ((1, first_chip), (2, second_chip), (3, diagonal)):
            for w in range(n):
                copy(w, k, (*chip, c), me).wait_recv()
                onward = [copy(w, 3 + k, (*chip, c), sibling)] + ([copy(w, 3, (*chip, c), second)] if k == 1 else [])
                for cp in onward:
                    cp.start()
                sent += onward
        for w in range(n):
            copy(w, 0, sibling, me).wait_recv()
            for k, chip in ((4, second_chip), (5, first_chip), (6, diagonal)):
                copy(w, k, (*chip, 1 - c), me).wait_recv()
        for cp in sent:
            cp.wait_send()
        for m in mine:
            m.wait()

    out = [jax.ShapeDtypeStruct((NDEV,) + s.shape, s.dtype) for s in shards]
    return _sequencer(body, out, 8 * n, name, collective_id)(*shards)


def _exchange_small(v, reduce, name, gathered=None):
    rows = v.shape[0]

    def body(*refs):
        if gathered is None:
            v_ref, out_ref, slots, send_sems, recv_sems, local_sem = refs
        else:
            v_ref, g_ref, out_ref, gsum_ref, slots, send_sems, recv_sems, local_sem = refs
        x, y, c = _pos()
        me = 4 * x + 2 * y + c
        mine = pltpu.make_async_copy(v_ref, slots.at[me], local_sem)
        mine.start()
        offs = [(dx, dy, dc) for dx in (0, 1) for dy in (0, 1) for dc in (0, 1)][1:]

        def copy(k, src_slot, to):
            return pltpu.make_async_remote_copy(
                src_ref=v_ref, dst_ref=slots.at[src_slot], send_sem=send_sems.at[k], recv_sem=recv_sems.at[k],
                device_id=to, device_id_type=MESH)

        sends = []
        for k, (dx, dy, dc) in enumerate(offs):
            cp = copy(k, me, (x ^ dx, y ^ dy, c ^ dc))
            cp.start()
            sends.append(cp)
        for k, (dx, dy, dc) in enumerate(offs):
            copy(k, 4 * (x ^ dx) + 2 * (y ^ dy) + (c ^ dc), (x, y, c)).wait_recv()
        for cp in sends:
            cp.wait_send()
        mine.wait()
        if reduce:
            acc = slots[0]
            for d in range(1, NDEV):
                acc = acc + slots[d]
            out_ref[...] = acc
        else:
            out_ref[...] = slots[...]
        if gathered is not None:
            acc = g_ref[0]
            for d in range(1, NDEV):
                acc = acc + g_ref[d]
            gsum_ref[...] = acc

    vmem = pl.BlockSpec(memory_space=pltpu.VMEM)
    out = jax.ShapeDtypeStruct((rows, D) if reduce else (NDEV, rows, D), F32)
    args, out_shape, out_specs = [v], out, vmem
    if gathered is not None:
        args.append(gathered)
        out_shape, out_specs = [out, jax.ShapeDtypeStruct(gathered.shape[1:], F32)], [vmem, vmem]
    return pl.pallas_call(
        body, name=name, out_shape=out_shape, in_specs=[vmem] * len(args), out_specs=out_specs,
        scratch_shapes=[pltpu.VMEM((NDEV, rows, D), F32), pltpu.SemaphoreType.DMA((7,)),
                        pltpu.SemaphoreType.DMA((7,)), pltpu.SemaphoreType.DMA],
    )(*args)


def reduce_scatter_d2d(grads, name, collective_id):
    n = len(grads)

    def body(*refs):
        ins, outs = refs[:n], refs[n:2 * n]
        send_sems, recv_sems = refs[2 * n:]
        x, y, c = _pos()
        _handshake([(x, y, 1 - c)])
        cps = []
        for w in range(n):
            cp = pltpu.make_async_remote_copy(
                src_ref=ins[w].at[1 - c], dst_ref=outs[w], send_sem=send_sems.at[w], recv_sem=recv_sems.at[w],
                device_id=(x, y, 1 - c), device_id_type=MESH)
            cp.start()
            cps.append(cp)
        for cp in cps:
            cp.wait_recv()
        for cp in cps:
            cp.wait_send()

    out = [jax.ShapeDtypeStruct(g.shape[1:], F32) for g in grads]
    return _sequencer(body, out, n, name, collective_id)(*grads)


def reduce_scatter_ici(parts, name, collective_id):
    n = len(parts)

    def body(*refs):
        ins, outs = refs[:n], refs[n:2 * n]
        send_sems, recv_sems = refs[2 * n:]
        x, y, c = _pos()
        offs = [(1, 0), (0, 1), (1, 1)]
        _handshake([(x ^ dx, y ^ dy, c) for dx, dy in offs])
        cps = []
        for w in range(n):
            for k, (dx, dy) in enumerate(offs):
                ox, oy = x ^ dx, y ^ dy
                cp = pltpu.make_async_remote_copy(
                    src_ref=ins[w].at[2 * ox + oy], dst_ref=outs[w].at[2 * x + y],
                    send_sem=send_sems.at[3 * w + k], recv_sem=recv_sems.at[3 * w + k],
                    device_id=(ox, oy, c), device_id_type=MESH)
                cp.start()
                cps.append((cp, w, k, ox, oy))
        for cp, w, k, ox, oy in cps:
            pltpu.make_async_remote_copy(
                src_ref=ins[w].at[2 * ox + oy], dst_ref=outs[w].at[2 * ox + oy],
                send_sem=send_sems.at[3 * w + k], recv_sem=recv_sems.at[3 * w + k],
                device_id=(ox, oy, c), device_id_type=MESH).wait_recv()
        for cp, *_ in cps:
            cp.wait_send()

    out = [jax.ShapeDtypeStruct(p.shape, BF16) for p in parts]
    return _sequencer(body, out, 3 * n, name, collective_id)(*parts)


def fwd_in(x, g1, w_in_pieces, tm):
    t = x.shape[0]
    tile = lambda i: (i, 0)
    w_spec = pl.BlockSpec((NDEV, D, CG), lambda i: (0, 0, 0))
    z_spec = pl.BlockSpec((NDEV, tm, CG), lambda i: (0, i, 0))
    z_shape = jax.ShapeDtypeStruct((NDEV, t, CG), BF16)

    def cost(other_bytes, transcendentals):
        return pl.CostEstimate(flops=2 * t * D * NDEV * CG, transcendentals=transcendentals,
                               bytes_accessed=other_bytes + 2 * D * NDEV * CG + 2 * t * NDEV * CG)

    def first(x_ref, g_ref, w_ref, z_ref, h_ref, ht_ref):
        xf = x_ref[...]
        h = (xf * _rms_inv(xf) * g_ref[...]).astype(BF16)
        h_ref[...] = h
        ht_ref[...] = h.T
        for j in range(NDEV):
            z_ref[j] = _dot(h, w_ref[j]).astype(BF16)

    z0, h, ht = pl.pallas_call(
        first, name="fwd_in_0", grid=(t // tm,),
        in_specs=[pl.BlockSpec((tm, D), tile), pl.BlockSpec((1, D), lambda i: (0, 0)), w_spec],
        out_specs=[z_spec, pl.BlockSpec((tm, D), tile), pl.BlockSpec((D, tm), lambda i: (0, i))],
        out_shape=[z_shape, jax.ShapeDtypeStruct((t, D), BF16), jax.ShapeDtypeStruct((D, t), BF16)],
        compiler_params=_cparams(1, VMEM_BIG), cost_estimate=cost(8 * t * D + 4 * D, t),
    )(x, g1, w_in_pieces[0])
    zs = [z0]
    for q in (1, 2):
        h, zs[-1] = lax.optimization_barrier((h, zs[-1]))

        def later(h_ref, w_ref, z_ref):
            hb = h_ref[...]
            for j in range(NDEV):
                z_ref[j] = _dot(hb, w_ref[j]).astype(BF16)

        zs.append(pl.pallas_call(
            later, name=f"fwd_in_{q}", grid=(t // tm,),
            in_specs=[pl.BlockSpec((tm, D), tile), w_spec], out_specs=z_spec, out_shape=z_shape,
            compiler_params=_cparams(1, VMEM_BIG), cost_estimate=cost(2 * t * D, 0),
        )(h, w_in_pieces[q]))
    return zs, ht


def _z_tile(z_refs, n):
    return z_refs[n % 3][n // 3]


def _pool_tile(z, zh, win, keep_hist, cnt):
    zt = z.astype(F32)
    ext = jnp.concatenate([zh.astype(F32) * keep_hist, zt], axis=0)
    s, sh = ext, 1
    while sh < win:
        s = s + pltpu.roll(s, sh, 0)
        sh *= 2
    return s[HALO:] / cnt - zt


def _conv_taps(ext, cur, w_ref, lanes, lead):
    x1 = _shift_down(ext, 1, lead)
    x2 = _shift_down(ext, 2, lead)
    out = w_ref[2:3, lanes] * cur + w_ref[1:2, lanes] * x1 + w_ref[0:1, lanes] * x2
    return out, x1, x2


def fwd_mix(zs, x, pool_w, pool_scale, conv_w, wmix, tm, seq):
    t = x.shape[0]
    tps = seq // tm
    hb = tm // HALO

    def body(z0_ref, z1_ref, z2_ref, zh0_ref, zh1_ref, zh2_ref, x_ref, pw_ref, ps_ref, wpp_ref, cw_ref, wco_ref,
             wo_ref, x1_ref, yp_ref, yc_ref):
        z_refs, zh_refs = (z0_ref, z1_ref, z2_ref), (zh0_ref, zh1_ref, zh2_ref)
        it = pl.program_id(0) % tps
        keep_hist = jnp.where(it == 0, 0.0, 1.0)
        pos = it * tm + lax.broadcasted_iota(jnp.int32, (tm, 1), 0)
        p2 = []
        for g, win in enumerate(WINS):
            cnt = jnp.minimum(pos + 1, win).astype(F32)
            p = _pool_tile(_z_tile(z_refs, g), _z_tile(zh_refs, g), win, keep_hist, cnt)
            lanes = slice(g * CG, (g + 1) * CG)
            p2.append((_dot(p.astype(BF16), pw_ref[g]) * ps_ref[:, lanes]).astype(BF16))
        y_pool = _dot(jnp.concatenate(p2, axis=1), wpp_ref[...])
        u = []
        for q in range(NG):
            lanes = slice(q * CG, (q + 1) * CG)
            cv = _z_tile(z_refs, 8 + q).astype(F32) * _z_tile(z_refs, 12 + q).astype(F32)
            cvh = _z_tile(zh_refs, 8 + q).astype(F32) * _z_tile(zh_refs, 12 + q).astype(F32) * keep_hist
            cc, _, _ = _conv_taps(jnp.concatenate([cvh, cv], axis=0), cv, cw_ref, lanes, HALO)
            u.append((_z_tile(z_refs, 4 + q).astype(F32) * cc).astype(BF16))
        y_conv = _dot(jnp.concatenate(u, axis=1), wco_ref[...])
        ypb, ycb = y_pool.astype(BF16), y_conv.astype(BF16)
        yp_ref[...] = ypb
        yc_ref[...] = ycb
        merged = []
        for q in range(NG):
            lanes = slice(q * CG, (q + 1) * CG)
            sp = jax.nn.sigmoid(_z_tile(z_refs, 16 + q).astype(F32))
            sc = jax.nn.sigmoid(_z_tile(z_refs, 20 + q).astype(F32))
            merged.append((sp * ypb[:, lanes].astype(F32) + sc * ycb[:, lanes].astype(F32)).astype(BF16))
        x1_ref[...] = x_ref[...] + _dot(jnp.concatenate(merged, axis=1), wo_ref[...])

    def hist(i):
        return jnp.maximum(i * hb - 1, 0)

    const2 = lambda i: (0, 0)
    return pl.pallas_call(
        body, name="fwd_mix", grid=(t // tm,),
        in_specs=[pl.BlockSpec((NDEV, tm, CG), lambda i: (0, i, 0))] * 3
                 + [pl.BlockSpec((NDEV, HALO, CG), lambda i: (0, hist(i), 0))] * 3
                 + [pl.BlockSpec((tm, D), lambda i: (i, 0)),
                    pl.BlockSpec((NG, CG, CG), lambda i: (0, 0, 0)), pl.BlockSpec((1, D), const2),
                    pl.BlockSpec((D, D), lambda i: (0, MIX_POOL_PROJ)), pl.BlockSpec((3, D), const2),
                    pl.BlockSpec((D, D), lambda i: (0, MIX_CONV_OUT)), pl.BlockSpec((D, D), lambda i: (0, MIX_O))],
        out_specs=[pl.BlockSpec((tm, D), lambda i: (i, 0))] * 3,
        out_shape=[jax.ShapeDtypeStruct((t, D), F32), jax.ShapeDtypeStruct((t, D), BF16),
                   jax.ShapeDtypeStruct((t, D), BF16)],
        compiler_params=_cparams(1, VMEM_BIG),
    )(*zs, *zs, x, pool_w, pool_scale, wmix, conv_w, wmix, wmix)


def fwd_up(x1, g2, w_up_g, fcw, fcb, tm, seq):
    t = x1.shape[0]
    tps = seq // tm

    def body(x1_ref, g2_ref, wup_ref, fcw_ref, fcb_ref, up_ref, pre_ref, act_ref, actt_ref, h2t_ref, hist_ref):
        i = pl.program_id(0)
        keep_hist = jnp.where(i % tps == 0, 0.0, 1.0)

        @pl.when(i == 0)
        def _():
            hist_ref[...] = jnp.zeros_like(hist_ref)

        x1v = x1_ref[...]
        h2 = (x1v * _rms_inv(x1v) * g2_ref[...]).astype(BF16)
        h2t_ref[...] = h2.T
        lanes = slice(0, SH_UP)
        for c in range(NCH):
            conv = []
            for s in range(2):
                ub = _dot(h2, wup_ref[s, c]).astype(BF16)
                up_ref[s, c] = ub
                uf = ub.astype(F32)
                ext = jnp.concatenate([hist_ref[s, c] * keep_hist, uf], axis=0)
                hist_ref[s, c] = uf[tm - 8:]
                cc, _, _ = _conv_taps(ext, uf, fcw_ref.at[s, c], lanes, 8)
                conv.append(cc + fcb_ref[s, c])
                pre_ref[s, c] = conv[s].astype(BF16)
            a = (conv[0] * jax.nn.sigmoid(conv[0]) * conv[1]).astype(BF16)
            act_ref[c] = a
            actt_ref[c] = a.T

    tile = lambda i: (i, 0)
    const2 = lambda i: (0, 0)
    whole = lambda i: (0, 0, 0, 0)
    chunks = pl.BlockSpec((2, NCH, tm, SH_UP), lambda i: (0, 0, i, 0))
    return pl.pallas_call(
        body, name="fwd_up", grid=(t // tm,),
        in_specs=[pl.BlockSpec((tm, D), tile), pl.BlockSpec((1, D), const2),
                  pl.BlockSpec((2, NCH, D, SH_UP), whole), pl.BlockSpec((2, NCH, 3, SH_UP), whole),
                  pl.BlockSpec((2, NCH, 1, SH_UP), whole)],
        out_specs=[chunks, chunks, pl.BlockSpec((NCH, tm, SH_UP), lambda i: (0, i, 0)),
                   pl.BlockSpec((NCH, SH_UP, tm), lambda i: (0, 0, i)), pl.BlockSpec((D, tm), lambda i: (0, i))],
        out_shape=[jax.ShapeDtypeStruct((2, NCH, t, SH_UP), BF16), jax.ShapeDtypeStruct((2, NCH, t, SH_UP), BF16),
                   jax.ShapeDtypeStruct((NCH, t, SH_UP), BF16), jax.ShapeDtypeStruct((NCH, SH_UP, t), BF16),
                   jax.ShapeDtypeStruct((D, t), BF16)],
        scratch_shapes=[pltpu.VMEM((2, NCH, 8, SH_UP), F32)],
        compiler_params=_cparams(1, VMEM_BIG),
    )(x1, g2, w_up_g.reshape(2, NCH, D, SH_UP), fcw.reshape(2, NCH, 3, SH_UP), fcb.reshape(2, NCH, 1, SH_UP))


def fwd_down(x1, act, w_dn, gf, tgt, tm):
    t = x1.shape[0]

    def body(x1_ref, act_ref, wdn_ref, gf_ref, tgt_ref, dx2_ref, dx2b_ref, vec_ref):
        @pl.when(pl.program_id(0) == 0)
        def _():
            vec_ref[...] = jnp.zeros_like(vec_ref)

        d = None
        for c in range(NCH):
            part = _dot(act_ref[c], wdn_ref[c])
            d = part if d is None else d + part
        x2 = x1_ref[...] + d
        inv3 = _rms_inv(x2)
        xn = x2 * inv3
        diff = xn * gf_ref[...] - tgt_ref[...]
        dy = diff * (1.0 / D)
        vec_ref[0:1, :] += jnp.sum(dy * xn, axis=0, keepdims=True)
        vec_ref[1:2, :] += 0.5 * jnp.sum(jnp.mean(diff * diff, axis=-1))
        dx2 = _rms_bwd(dy, xn, inv3, gf_ref[...])
        dx2_ref[...] = dx2
        dx2b_ref[...] = dx2.astype(BF16)

    tile = lambda i: (i, 0)
    const2 = lambda i: (0, 0)
    return pl.pallas_call(
        body, name="fwd_down", grid=(t // tm,),
        in_specs=[pl.BlockSpec((tm, D), tile), pl.BlockSpec((NCH, tm, SH_UP), lambda i: (0, i, 0)),
                  pl.BlockSpec((NCH, SH_UP, D), lambda i: (0, 0, 0)), pl.BlockSpec((1, D), const2),
                  pl.BlockSpec((tm, D), tile)],
        out_specs=[pl.BlockSpec((tm, D), tile), pl.BlockSpec((tm, D), tile), pl.BlockSpec((8, D), const2)],
        out_shape=[jax.ShapeDtypeStruct((t, D), F32), jax.ShapeDtypeStruct((t, D), BF16),
                   jax.ShapeDtypeStruct((8, D), F32)],
        compiler_params=_cparams(1, VMEM_BIG),
    )(x1, act, w_dn, gf, tgt)


def bwd_ffn(dx2, x1, g2, up, pre, w_up_g, fcw, w_dn, tm, seq):
    t = x1.shape[0]
    nt = t // tm
    tps = seq // tm

    def body(dx2_ref, x1_ref, g2_ref, up_ref, pre_ref, wup_ref, fcw_ref, wdn_ref,
             dup_ref, dx1_ref, gvec_ref, gn_ref, carry_ref):
        i = pl.program_id(0)
        it = (nt - 1 - i) % tps
        keep_next = jnp.where(it == tps - 1, 0.0, 1.0)

        @pl.when(i == 0)
        def _():
            gvec_ref[...] = jnp.zeros_like(gvec_ref)
            gn_ref[...] = jnp.zeros_like(gn_ref)
            carry_ref[...] = jnp.zeros_like(carry_ref)

        dx2v = dx2_ref[...]
        dxb = dx2v.astype(BF16)
        lanes = slice(0, SH_UP)
        dh2 = None
        for c in range(NCH):
            pre = [pre_ref[s, c].astype(F32) for s in range(2)]
            sg = jax.nn.sigmoid(pre[0])
            dact = _dot_nt(dxb, wdn_ref[c])
            dpre = [dact * pre[1] * (sg * (1.0 + pre[0] * (1.0 - sg))), dact * (pre[0] * sg)]
            for s in range(2):
                dc = dpre[s]
                ext = jnp.concatenate([dc, carry_ref[s, c] * keep_next], axis=0)
                carry_ref[s, c] = dc[:8]
                shifted = (_shift_up(ext, 2, tm), _shift_up(ext, 1, tm), dc)
                uf = up_ref[s, c].astype(F32)
                gvec_ref[s, c, 0:1, lanes] += jnp.sum(dc, axis=0, keepdims=True)
                for tap in range(3):
                    gvec_ref[s, c, tap + 1:tap + 2, lanes] += jnp.sum(shifted[tap] * uf, axis=0, keepdims=True)
                w = fcw_ref.at[s, c]
                du = w[2:3, :] * dc + w[1:2, :] * shifted[1] + w[0:1, :] * shifted[0]
                dub = du.astype(BF16)
                dup_ref[s, c] = dub
                part = _dot_nt(dub, wup_ref[s, c])
                dh2 = part if dh2 is None else dh2 + part
        x1v = x1_ref[...]
        inv2 = _rms_inv(x1v)
        xn = x1v * inv2
        gn_ref[0:1, :] += jnp.sum(dh2 * xn, axis=0, keepdims=True)
        dx1_ref[...] = dx2v + _rms_bwd(dh2, xn, inv2, g2_ref[...])

    rev = lambda i: (nt - 1 - i, 0)
    const2 = lambda i: (0, 0)
    whole = lambda i: (0, 0, 0, 0)
    chunks = pl.BlockSpec((2, NCH, tm, SH_UP), lambda i: (0, 0, nt - 1 - i, 0))
    return pl.pallas_call(
        body, name="bwd_ffn", grid=(nt,),
        in_specs=[pl.BlockSpec((tm, D), rev), pl.BlockSpec((tm, D), rev), pl.BlockSpec((1, D), const2),
                  chunks, chunks, pl.BlockSpec((2, NCH, D, SH_UP), whole), pl.BlockSpec((2, NCH, 3, SH_UP), whole),
                  pl.BlockSpec((NCH, SH_UP, D), lambda i: (0, 0, 0))],
        out_specs=[chunks, pl.BlockSpec((tm, D), rev), pl.BlockSpec((2, NCH, 8, D), whole),
                   pl.BlockSpec((8, D), const2)],
        out_shape=[jax.ShapeDtypeStruct((2, NCH, t, SH_UP), BF16), jax.ShapeDtypeStruct((t, D), F32),
                   jax.ShapeDtypeStruct((2, NCH, 8, D), F32), jax.ShapeDtypeStruct((8, D), F32)],
        scratch_shapes=[pltpu.VMEM((2, NCH, 8, SH_UP), F32)],
        compiler_params=_cparams(1, VMEM_BIG),
    )(dx2, x1, g2, up, pre, w_up_g.reshape(2, NCH, D, SH_UP), fcw.reshape(2, NCH, 3, SH_UP), w_dn)


def bwd_mix(dx1, zs, y_pool, y_conv, pool_w, pool_scale, conv_w, wmix, tm, seq):
    t = dx1.shape[0]
    nt = t // tm
    tps = seq // tm
    hb = tm // HALO

    def body(da_ref, z0_ref, z1_ref, z2_ref, zh0_ref, zh1_ref, zh2_ref, yp_ref, yc_ref, pw_ref, ps_ref, wpp_ref,
             cw_ref, wco_ref, wo_ref,
             dz_ref, mg_ref, p2_ref, u_ref, dyp_ref, dyc_ref, p_ref, dpw_ref, gvec_ref, cp_ref, cc_ref):
        z_refs, zh_refs = (z0_ref, z1_ref, z2_ref), (zh0_ref, zh1_ref, zh2_ref)
        i = pl.program_id(0)
        it = (nt - 1 - i) % tps
        keep_hist = jnp.where(it == 0, 0.0, 1.0)
        keep_next = jnp.where(it == tps - 1, 0.0, 1.0)
        pos = it * tm + lax.broadcasted_iota(jnp.int32, (tm, 1), 0)

        @pl.when(i == 0)
        def _():
            gvec_ref[...] = jnp.zeros_like(gvec_ref)
            cp_ref[...] = jnp.zeros_like(cp_ref)
            cc_ref[...] = jnp.zeros_like(cc_ref)

        dm = _dot_nt(da_ref[...].astype(BF16), wo_ref[...])
        merged, dyp, dyc = [], [], []
        for q in range(NG):
            lanes = slice(q * CG, (q + 1) * CG)
            sp = jax.nn.sigmoid(_z_tile(z_refs, 16 + q).astype(F32))
            sc = jax.nn.sigmoid(_z_tile(z_refs, 20 + q).astype(F32))
            yp = yp_ref[:, lanes].astype(F32)
            yc = yc_ref[:, lanes].astype(F32)
            dmq = dm[:, lanes]
            merged.append((sp * yp + sc * yc).astype(BF16))
            dyp.append((dmq * sp).astype(BF16))
            dyc.append((dmq * sc).astype(BF16))
            dz_ref[16 + q] = (dmq * yp * (sp * (1.0 - sp))).astype(BF16)
            dz_ref[20 + q] = (dmq * yc * (sc * (1.0 - sc))).astype(BF16)
        mg_ref[...] = jnp.concatenate(merged, axis=1)
        dypb = jnp.concatenate(dyp, axis=1)
        dycb = jnp.concatenate(dyc, axis=1)
        dyp_ref[...] = dypb
        dyc_ref[...] = dycb

        dp2 = _dot_nt(dypb, wpp_ref[...])
        p2 = []
        for g, win in enumerate(WINS):
            lanes = slice(g * CG, (g + 1) * CG)
            cnt = jnp.minimum(pos + 1, win).astype(F32)
            p = _pool_tile(_z_tile(z_refs, g), _z_tile(zh_refs, g), win, keep_hist, cnt)
            pb = p.astype(BF16)
            p_ref[g] = pb
            pw = _dot(pb, pw_ref[g])
            p2.append((pw * ps_ref[:, lanes]).astype(BF16))
            dp2g = dp2[:, lanes]
            gvec_ref[0:1, lanes] += jnp.sum(dp2g * pw, axis=0, keepdims=True)
            dpwb = (dp2g * ps_ref[:, lanes]).astype(BF16)
            dpw_ref[g] = dpwb
            dp = _dot_nt(dpwb, pw_ref[g])
            qv = dp / cnt
            ext = jnp.concatenate([qv, cp_ref[g] * keep_next], axis=0)
            cp_ref[g] = qv[:HALO]
            n = tm + HALO
            s, sh = ext, 1
            while sh < win:
                s = s + pltpu.roll(s, n - sh, 0)
                sh *= 2
            dz_ref[g] = (s[:tm] - dp).astype(BF16)
        p2_ref[...] = jnp.concatenate(p2, axis=1)

        du = _dot_nt(dycb, wco_ref[...])
        u = []
        for q in range(NG):
            lanes = slice(q * CG, (q + 1) * CG)
            zb = _z_tile(z_refs, 4 + q).astype(F32)
            zc = _z_tile(z_refs, 8 + q).astype(F32)
            zv = _z_tile(z_refs, 12 + q).astype(F32)
            cv = zc * zv
            cvh = _z_tile(zh_refs, 8 + q).astype(F32) * _z_tile(zh_refs, 12 + q).astype(F32) * keep_hist
            cc, cv1, cv2 = _conv_taps(jnp.concatenate([cvh, cv], axis=0), cv, cw_ref, lanes, HALO)
            u.append((zb * cc).astype(BF16))
            duq = du[:, lanes]
            dz_ref[4 + q] = (duq * cc).astype(BF16)
            dcc = duq * zb
            for tap, src in enumerate((cv2, cv1, cv)):
                gvec_ref[tap + 1:tap + 2, lanes] += jnp.sum(dcc * src, axis=0, keepdims=True)
            ext = jnp.concatenate([dcc, cc_ref[:, lanes] * keep_next], axis=0)
            cc_ref[:, lanes] = dcc[:8]
            dcv = (cw_ref[2:3, lanes] * dcc + cw_ref[1:2, lanes] * _shift_up(ext, 1, tm)
                   + cw_ref[0:1, lanes] * _shift_up(ext, 2, tm))
            dz_ref[8 + q] = (dcv * zv).astype(BF16)
            dz_ref[12 + q] = (dcv * zc).astype(BF16)
        u_ref[...] = jnp.concatenate(u, axis=1)

    def hist(i):
        return jnp.maximum((nt - 1 - i) * hb - 1, 0)

    rev = lambda i: (nt - 1 - i, 0)
    rev3 = lambda i: (0, nt - 1 - i, 0)
    const2 = lambda i: (0, 0)
    tok = jax.ShapeDtypeStruct((t, D), BF16)
    grp = jax.ShapeDtypeStruct((NG, t, CG), BF16)
    return pl.pallas_call(
        body, name="bwd_mix", grid=(nt,),
        in_specs=[pl.BlockSpec((tm, D), rev)] + [pl.BlockSpec((NDEV, tm, CG), rev3)] * 3
                 + [pl.BlockSpec((NDEV, HALO, CG), lambda i: (0, hist(i), 0))] * 3
                 + [pl.BlockSpec((tm, D), rev), pl.BlockSpec((tm, D), rev),
                    pl.BlockSpec((NG, CG, CG), lambda i: (0, 0, 0)), pl.BlockSpec((1, D), const2),
                    pl.BlockSpec((D, D), lambda i: (0, MIX_POOL_PROJ)), pl.BlockSpec((3, D), const2),
                    pl.BlockSpec((D, D), lambda i: (0, MIX_CONV_OUT)), pl.BlockSpec((D, D), lambda i: (0, MIX_O))],
        out_specs=[pl.BlockSpec((NZT, tm, CG), rev3)] + [pl.BlockSpec((tm, D), rev)] * 5
                  + [pl.BlockSpec((NG, tm, CG), rev3)] * 2 + [pl.BlockSpec((8, D), const2)],
        out_shape=[jax.ShapeDtypeStruct((NZT, t, CG), BF16), tok, tok, tok, tok, tok, grp, grp,
                   jax.ShapeDtypeStruct((8, D), F32)],
        scratch_shapes=[pltpu.VMEM((NG, HALO, CG), F32), pltpu.VMEM((8, D), F32)],
        compiler_params=_cparams(1, VMEM_BIG),
    )(dx1, *zs, *zs, y_pool, y_conv, pool_w, pool_scale, wmix, conv_w, wmix, wmix)


def bwd_in(dz, w_in_pieces, dx1, x, g1, tm):
    t = x.shape[0]

    def body(dz_ref, w0_ref, w1_ref, w2_ref, dx1_ref, x_ref, g_ref, gx_ref, gn_ref):
        @pl.when(pl.program_id(0) == 0)
        def _():
            gn_ref[...] = jnp.zeros_like(gn_ref)

        dh = None
        for j in range(NDEV):
            for q, w_ref in enumerate((w0_ref, w1_ref, w2_ref)):
                part = _dot_nt(dz_ref[3 * j + q], w_ref[j])
                dh = part if dh is None else dh + part
        xv = x_ref[...]
        inv = _rms_inv(xv)
        xn = xv * inv
        gn_ref[0:1, :] += jnp.sum(dh * xn, axis=0, keepdims=True)
        gx_ref[...] = dx1_ref[...] + _rms_bwd(dh, xn, inv, g_ref[...])

    tile = lambda i: (i, 0)
    return pl.pallas_call(
        body, name="bwd_in", grid=(t // tm,),
        in_specs=[pl.BlockSpec((NZT, tm, CG), lambda i: (0, i, 0))]
                 + [pl.BlockSpec((NDEV, D, CG), lambda i: (0, 0, 0))] * 3
                 + [pl.BlockSpec((tm, D), tile), pl.BlockSpec((tm, D), tile), pl.BlockSpec((1, D), lambda i: (0, 0))],
        out_specs=[pl.BlockSpec((tm, D), tile), pl.BlockSpec((8, D), lambda i: (0, 0))],
        out_shape=[jax.ShapeDtypeStruct((t, D), F32), jax.ShapeDtypeStruct((8, D), F32)],
        compiler_params=_cparams(1, VMEM_BIG),
    )(dz, *w_in_pieces, dx1, x, g1)


def _slot(j):
    return j % 2, j // 2


def wgrad_cols(at, b, q, name):
    m, t = at.shape
    width = b.shape[3]

    def body(a_ref, b_ref, o_ref):
        o_ref[...] = _dot(a_ref[...], b_ref[...])

    return pl.pallas_call(
        body, name=name, grid=(NDEV,),
        in_specs=[pl.BlockSpec((m, t), lambda j: (0, 0)),
                  pl.BlockSpec((None, None, t, width), lambda j: (j, q, 0, 0))],
        out_specs=pl.BlockSpec((None, None, m, width), lambda j: (j % 2, j // 2, 0, 0)),
        out_shape=jax.ShapeDtypeStruct((2, 4, m, width), F32),
        compiler_params=_cparams(1, VMEM_BIG),
    )(at, b)


def wgrad_down(actt, dx2b):
    t = dx2b.shape[0]

    def body(a_ref, b_ref, o_ref):
        r = _dot(a_ref[...], b_ref[...])
        o_ref[0] = r[:SH_DN]
        o_ref[1] = r[SH_DN:]

    return pl.pallas_call(
        body, name="wgrad_down", grid=(NCH,),
        in_specs=[pl.BlockSpec((None, SH_UP, t), lambda k: (k, 0, 0)), pl.BlockSpec((t, D), lambda k: (0, 0))],
        out_specs=pl.BlockSpec((2, None, SH_DN, D), lambda k: (0, k, 0, 0)),
        out_shape=jax.ShapeDtypeStruct((2, 4, SH_DN, D), F32),
        compiler_params=_cparams(1, VMEM_BIG),
    )(actt, dx2b)


def wgrad_square(a, b, name, tk):
    t = a.shape[0]

    def body(a_ref, b_ref, o_ref, acc_ref):
        kt = pl.program_id(0)

        @pl.when(kt == 0)
        def _():
            acc_ref[...] = jnp.zeros_like(acc_ref)

        acc_ref[...] += _dot_tn(a_ref[...], b_ref[...].astype(BF16))

        @pl.when(kt == pl.num_programs(0) - 1)
        def _():
            for j in range(NDEV):
                cc, xy = _slot(j)
                o_ref[cc, xy] = acc_ref[j * 128:(j + 1) * 128]

    return pl.pallas_call(
        body, name=name, grid=(t // tk,),
        in_specs=[pl.BlockSpec((tk, D), lambda k: (k, 0)), pl.BlockSpec((tk, D), lambda k: (k, 0))],
        out_specs=pl.BlockSpec((2, 4, 128, D), lambda k: (0, 0, 0, 0)),
        out_shape=jax.ShapeDtypeStruct((2, 4, 128, D), F32),
        scratch_shapes=[pltpu.VMEM((D, D), F32)],
        compiler_params=_cparams(1, VMEM_BIG),
    )(a, b)


def wgrad_pool(p, dpw, tk):
    t = p.shape[1]

    def body(a_ref, b_ref, o_ref):
        @pl.when(pl.program_id(0) == 0)
        def _():
            o_ref[...] = jnp.zeros_like(o_ref)

        for g in range(NG):
            o_ref[g] += _dot_tn(a_ref[g], b_ref[g])

    return pl.pallas_call(
        body, name="wgrad_pool", grid=(t // tk,),
        in_specs=[pl.BlockSpec((NG, tk, CG), lambda k: (0, k, 0))] * 2,
        out_specs=pl.BlockSpec((NG, CG, CG), lambda k: (0, 0, 0)),
        out_shape=jax.ShapeDtypeStruct((NG, CG, CG), F32),
        compiler_params=_cparams(1, VMEM_BIG),
    )(p, dpw)


def _adamw(w, g, m, v):
    m = ADAM_B1 * m + (1.0 - ADAM_B1) * g
    v = ADAM_B2 * v + (1.0 - ADAM_B2) * (g * g)
    m_hat = m / (1.0 - ADAM_B1 ** ADAM_STEP)
    v_hat = v / (1.0 - ADAM_B2 ** ADAM_STEP)
    delta = -ADAM_LR * (m_hat / (jnp.sqrt(v_hat) + ADAM_EPS) + ADAM_WD * w)
    return delta, m, v


def _row_block(r):
    return 512 if r % 512 == 0 else r


def chip_partial(place, g, from_sibling, name):
    _, _, r, c = g.shape

    def body(place_ref, g_ref, s_ref, o_ref):
        o_ref[...] = (g_ref[...] + s_ref[...]).astype(BF16)

    return pl.pallas_call(
        body, name=name,
        grid_spec=pltpu.PrefetchScalarGridSpec(
            num_scalar_prefetch=1, grid=(3,),
            in_specs=[pl.BlockSpec((None, None, r, c), lambda k, pr: (pr[0], pr[1] ^ (k + 1), 0, 0)),
                      pl.BlockSpec((None, r, c), lambda k, pr: (pr[1] ^ (k + 1), 0, 0))],
            out_specs=pl.BlockSpec((None, r, c), lambda k, pr: (pr[1] ^ (k + 1), 0, 0))),
        out_shape=jax.ShapeDtypeStruct((4, r, c), BF16),
        compiler_params=_cparams(1, VMEM_BIG),
    )(place, g, from_sibling)


def finish_adamw(place, gs, from_sibling, from_chips, w, m, v, name, transposed=False):
    n = len(gs)
    r = gs[0].shape[2]
    widths = [g.shape[3] for g in gs]
    c = sum(widths)
    br = _row_block(r)

    def body(place_ref, *refs):
        g_refs, s_refs, c_refs = refs[:n], refs[n:2 * n], refs[2 * n:5 * n]
        w_ref, m_ref, v_ref, og_ref, od_ref, om_ref, ov_ref = refs[5 * n:]
        cols = []
        for q in range(n):
            grad = g_refs[q][...] + s_refs[q][...]
            for k in range(3):
                grad = grad + c_refs[3 * q + k][...].astype(F32)
            cols.append(grad)
        grad = cols[0] if n == 1 else jnp.concatenate(cols, axis=1)
        if transposed:
            grad = grad.T
        og_ref[...] = grad
        od_ref[...], om_ref[...], ov_ref[...] = _adamw(w_ref[...], grad, m_ref[...], v_ref[...])

    def other(k, cq):
        return pl.BlockSpec((None, br, cq), lambda i, pr: (pr[1] ^ k, i, 0))

    row = pl.BlockSpec((c, br), lambda i, pr: (0, i)) if transposed else pl.BlockSpec((br, c), lambda i, pr: (i, 0))
    out = jax.ShapeDtypeStruct((c, r) if transposed else (r, c), F32)
    in_specs = [pl.BlockSpec((None, None, br, cq), lambda i, pr: (pr[0], pr[1], i, 0)) for cq in widths]
    in_specs += [pl.BlockSpec((None, br, cq), lambda i, pr: (pr[1], i, 0)) for cq in widths]
    in_specs += [other(k, cq) for cq in widths for k in (1, 2, 3)]
    return pl.pallas_call(
        body, name=name,
        grid_spec=pltpu.PrefetchScalarGridSpec(
            num_scalar_prefetch=1, grid=(r // br,), in_specs=in_specs + [row, row, row], out_specs=[row] * 4),
        out_shape=[out] * 4,
        compiler_params=_cparams(1, VMEM_BIG),
    )(place, *gs, *from_sibling, *[fc for fc in from_chips for _ in range(3)], w, m, v)


def adamw_small(items):
    n = len(items)

    def body(*refs):
        ins, outs = refs[:4 * n], refs[4 * n:]
        for i in range(n):
            w, g, m, v = (r[...] for r in ins[4 * i:4 * i + 4])
            outs[3 * i][...], outs[3 * i + 1][...], outs[3 * i + 2][...] = _adamw(w, g, m, v)

    out = [jax.ShapeDtypeStruct(it[0].shape, F32) for it in items for _ in range(3)]
    res = pl.pallas_call(body, name="adamw_small", out_shape=out)(*[a for it in items for a in it])
    return [res[3 * i:3 * i + 3] for i in range(n)]


def kernel(x, norm_mix, w_in, pool_w, pool_scale, w_pool_proj, conv_w, w_conv_out, w_o, norm_ffn, w_up, ffn_conv_w, ffn_conv_b, w_down, norm_final, loss_target, m_norm_mix, m_w_in, m_pool_w, m_pool_scale, m_w_pool_proj, m_conv_w, m_w_conv_out, m_w_o, m_norm_ffn, m_w_up, m_ffn_conv_w, m_ffn_conv_b, m_w_down, m_norm_final, v_norm_mix, v_w_in, v_pool_w, v_pool_scale, v_w_pool_proj, v_conv_w, v_w_conv_out, v_w_o, v_norm_ffn, v_w_up, v_ffn_conv_w, v_ffn_conv_b, v_w_down, v_norm_final):
    nb, seq, _ = x.shape
    t = nb * seq
    tm_in = min(TM_IN, t)
    tm_mix = min(TM_MIX, seq)
    tm_ffn = min(TM_FFN, seq)
    tk = min(TK_WGRAD, t)
    xt = x.reshape(t, D)
    tgt = loss_target.reshape(t, D)
    xi, yi, ci = _pos()
    me = 4 * xi + 2 * yi + ci
    place = jnp.stack([ci, 2 * xi + yi]).astype(jnp.int32)

    tie = lax.optimization_barrier
    w_in_b = w_in[0].astype(BF16)
    w_in_g = [all_gather_blocks([w_in_b[:, q * CG:(q + 1) * CG]], f"all_gather_w_in_{q}", 0)[0] for q in range(3)]
    taps = (jnp.pad(conv_w[0], ((0, 5), (0, D - 128))) + jnp.pad(ffn_conv_w[0], ((3, 2), (0, D - SH_UP))))
    taps_g = _exchange_small(taps, False, "all_gather_taps")
    mix_shard = jnp.concatenate(
        [w_pool_proj[0], w_conv_out[0], w_o[0], pool_w[0].reshape(NG * 32, CG)], axis=1).astype(BF16)
    mix_shard, taps_g = tie((mix_shard, taps_g))
    wmix_g, = all_gather_blocks([mix_shard], "all_gather_w_mix", 0)
    ffn_shards, w_in_g[0] = tie(([w_up[0].astype(BF16), w_down[0].astype(BF16)], w_in_g[0]))
    w_up_g, = all_gather_blocks(ffn_shards[:1], "all_gather_w_up", 0)
    w_dn_g, = all_gather_blocks(ffn_shards[1:], "all_gather_w_down", 0)
    w_dn_f = w_dn_g.reshape(NCH, SH_UP, D)
    conv_w_f = taps_g[:, 0:3, :128].transpose(1, 0, 2).reshape(3, D)
    fcw_f = taps_g[:, 3:6, :SH_UP]
    fcb_f = ffn_conv_b.reshape(NDEV, 1, SH_UP)
    gfin = norm_final.reshape(1, D)

    zs, h1 = fwd_in(xt, norm_mix, w_in_g, tm_in)
    wmix_g, zs = tie((wmix_g, zs))
    wmix = wmix_g.reshape(D, MIX_COLS)
    pool_w_f = wmix_g[:, :, 3 * D:].reshape(NDEV, NG, 32, CG).transpose(1, 0, 2, 3).reshape(NG, CG, CG)
    x1, y_pool, y_conv = fwd_mix(zs, xt, pool_w_f, pool_scale, conv_w_f, wmix, tm_mix, seq)
    up, pre, act_tok, act, h2 = fwd_up(x1, norm_ffn, w_up_g, fcw_f, fcb_f, tm_ffn, seq)
    dx2, dx2b, ffn_vec = fwd_down(x1, act_tok, w_dn_f, gfin, tgt, min(TM_IN, t))

    def to_sibling(full, tag):
        return reduce_scatter_d2d(full, "reduce_scatter_d2d_" + tag, 1)

    def partials(full, from_sib, names):
        return [chip_partial(place, g, s, "chip_partial_" + nm) for g, s, nm in zip(full, from_sib, names)]

    def to_chips(parts, tag):
        return reduce_scatter_ici(parts, "reduce_scatter_ici_" + tag, 2)

    def finish(nm, gs, from_sib, from_chips, wmv, transposed=False):
        rc = (gs[0].shape[2], sum(g.shape[3] for g in gs))
        wmv2 = [a.reshape(rc).T if transposed else a.reshape(rc) for a in wmv]
        outs = finish_adamw(place, gs, from_sib, from_chips, *wmv2, "adamw_" + nm, transposed)
        return [(o.T if transposed else o).reshape(wmv[0].shape) for o in outs]

    def after(x, dep):
        return tie((x, dep))[0]

    big = {}
    d_up, dx1, g_ffn_vec, g_nffn = bwd_ffn(dx2, x1, norm_ffn, up, pre, w_up_g, fcw_f, w_dn_f, tm_ffn, seq)
    gw_up = wgrad_cols(h2, d_up.reshape(NDEV, 1, t, SH_UP), 0, "wgrad_up")
    sib_up = to_sibling([gw_up], "w_up")
    gw_dn = wgrad_down(act, after(dx2b, gw_up))
    sib_dn = to_sibling([after(gw_dn, sib_up)], "w_down")
    dx1, part_up = tie((dx1, partials([gw_up], sib_up, ["w_up"])))
    chips_up = to_chips(part_up, "w_up")
    dz, merged, p2, u, dyp, dyc, p, dpw, g_mix_vec = bwd_mix(
        dx1, zs, y_pool, y_conv, pool_w_f, pool_scale, conv_w_f, wmix, tm_mix, seq)
    merged, part_dn = tie((merged, partials([gw_dn], sib_dn, ["w_down"])))
    chips_dn = to_chips(part_dn, "w_down")
    gw_o = wgrad_square(merged, dx1, "wgrad_o", tk)
    gw_pp = wgrad_square(p2, dyp, "wgrad_pool_proj", tk)
    gw_co = wgrad_square(u, dyc, "wgrad_conv_out", tk)
    gw_pool = wgrad_pool(p, dpw, tk).reshape(NG, 4, 2, 32, CG).transpose(2, 1, 0, 3, 4).reshape(2, 4, NG * 32, CG)
    dz8 = dz.reshape(NDEV, 3, t, CG)
    gw_in, sib_in, chips_in = [None] * 3, [None] * 3, [None] * 3
    sib_a = to_sibling(after([gw_o, gw_pp], (chips_up, gw_pool)), "mix_a")
    sib_b = to_sibling(after([gw_co, gw_pool], sib_a), "mix_b")
    gw_in[0] = wgrad_cols(h1, dz8, 0, "wgrad_in_0")
    h1, part_a, part_b = tie((h1, partials([gw_o, gw_pp], sib_a, ["w_o", "w_pool_proj"]),
                              partials([gw_co, gw_pool], sib_b, ["w_conv_out", "pool_w"])))
    chips_a = to_chips(after(part_a, chips_dn), "mix_a")
    chips_b = to_chips(part_b, "mix_b")
    sib_in[0] = to_sibling(after([gw_in[0]], sib_b), "w_in_0")
    gw_in[1] = wgrad_cols(h1, dz8, 1, "wgrad_in_1")
    h1, part_in0, gw_in[1] = tie((h1, partials([gw_in[0]], sib_in[0], ["w_in_0"]), gw_in[1]))
    chips_in[0] = to_chips(part_in0, "w_in_0")
    sib_in[1] = to_sibling(after([gw_in[1]], sib_in[0]), "w_in_1")
    h1, big["w_down"], big["w_up"] = tie((
        h1, finish("w_down", [gw_dn], sib_dn, chips_dn, (w_down, m_w_down, v_w_down)),
        finish("w_up", [gw_up], sib_up, chips_up, (w_up, m_w_up, v_w_up), transposed=True)))
    gw_in[2] = wgrad_cols(h1, dz8, 2, "wgrad_in_2")
    sib_in[2] = to_sibling(after([gw_in[2]], (chips_a, chips_b, chips_in[0])), "w_in_2")
    sib_in[2], big["w_o"], big["w_pool_proj"], big["w_conv_out"], big["pool_w"] = tie((
        sib_in[2],
        finish("w_o", [gw_o], sib_a[:1], chips_a[:1], (w_o, m_w_o, v_w_o)),
        finish("w_pool_proj", [gw_pp], sib_a[1:], chips_a[1:], (w_pool_proj, m_w_pool_proj, v_w_pool_proj)),
        finish("w_conv_out", [gw_co], sib_b[:1], chips_b[:1], (w_conv_out, m_w_conv_out, v_w_conv_out)),
        finish("pool_w", [gw_pool], sib_b[1:], chips_b[1:], (pool_w, m_pool_w, v_pool_w))))
    dx1, part_in1, part_in2 = tie((
        dx1, partials([gw_in[1]], sib_in[1], ["w_in_1"]), partials([gw_in[2]], sib_in[2], ["w_in_2"])))
    chips_in[1] = to_chips(after(part_in1, sib_in[2]), "w_in_1")
    chips_in[2] = to_chips(part_in2, "w_in_2")
    small_g, = all_gather_blocks(
        [after(jnp.concatenate([g_mix_vec, g_nffn, ffn_vec, g_ffn_vec.reshape(8 * NDEV, D)], axis=0), sib_in[2])],
        "all_gather_small", 0)
    grad_x, g_nmix = bwd_in(dz, w_in_g, dx1, xt, norm_mix, min(TM_BWD_IN, t))
    grad_x, chips_in = tie((grad_x, chips_in))
    big["w_in"] = finish("w_in", gw_in, [s[0] for s in sib_in], [c[0] for c in chips_in], (w_in, m_w_in, v_w_in))

    red_n, red = _exchange_small(g_nmix, True, "all_reduce_small", gathered=small_g)
    g_norm_mix, g_pool_scale, g_norm_ffn = red_n[0:1], red[0:1], red[8:9]
    g_conv_w = lax.dynamic_slice(red, (1, me * 128), (3, 128))
    g_norm_final = red[16]
    loss = red[17, 0]
    g_fcb = red[24:].reshape(NDEV, 8, D)[:, 0, :SH_UP].reshape(1, FF2)
    g_fcw = lax.dynamic_slice(red, (25 + 8 * me, 0), (3, SH_UP))
    grads = {"norm_mix": g_norm_mix, "pool_scale": g_pool_scale, "norm_ffn": g_norm_ffn, "norm_final": g_norm_final,
             "ffn_conv_b": g_fcb, "conv_w": g_conv_w.reshape(1, 3, 128), "ffn_conv_w": g_fcw.reshape(1, 3, SH_UP)}
    small_wmv = {"norm_mix": (norm_mix, m_norm_mix, v_norm_mix), "pool_scale": (pool_scale, m_pool_scale, v_pool_scale),
                 "norm_ffn": (norm_ffn, m_norm_ffn, v_norm_ffn), "norm_final": (norm_final, m_norm_final, v_norm_final),
                 "ffn_conv_b": (ffn_conv_b, m_ffn_conv_b, v_ffn_conv_b), "conv_w": (conv_w, m_conv_w, v_conv_w),
                 "ffn_conv_w": (ffn_conv_w, m_ffn_conv_w, v_ffn_conv_w)}
    small_names = list(small_wmv)
    flat2 = lambda a: a.reshape(1, -1) if a.ndim == 1 else (a.transpose(1, 0, 2) if a.ndim == 3 else a)
    unflat = lambda o, like: o.transpose(1, 0, 2) if like.ndim == 3 else o.reshape(like.shape)
    small_out = adamw_small([(flat2(small_wmv[nm][0]), flat2(grads[nm]), flat2(small_wmv[nm][1]),
                              flat2(small_wmv[nm][2])) for nm in small_names])
    small = {nm: [unflat(o, small_wmv[nm][0]) for o in outs] for nm, outs in zip(small_names, small_out)}

    order = ["norm_mix", "w_in", "pool_w", "pool_scale", "w_pool_proj", "conv_w", "w_conv_out", "w_o", "norm_ffn",
             "w_up", "ffn_conv_w", "ffn_conv_b", "w_down", "norm_final"]
    out = [loss, grad_x.reshape(nb, seq, D)]
    out += [big[nm][0] if nm in big else grads[nm] for nm in order]
    for idx in range(3):
        out += [big[nm][idx + 1] if nm in big else small[nm][idx] for nm in order]
    return tuple(out)
```

```python
import jax
import jax.numpy as jnp
from jax import lax
from jax.experimental import pallas as pl
from jax.experimental.pallas import tpu as pltpu
from jax.experimental.pallas import tpu_sc as plsc

F32 = jnp.float32
BF16 = jnp.bfloat16

NDEV = 8
D = 1024
NG = 4
CG = 256
WINS = (2, 4, 8, 16)
DIN = 6 * D
SH_IN = DIN // NDEV
NZT = DIN // CG
FF2 = 5632
SH_UP = FF2 // NDEV
FF = FF2 // 2
NCH = 4
SH_DN = FF // NDEV
RMS_EPS = 1e-6
HALO = 16

ADAM_LR = 0.001
ADAM_B1 = 0.9
ADAM_B2 = 0.999
ADAM_EPS = 1e-08
ADAM_WD = 0.01
ADAM_STEP = 10

TM_IN = 512
TM_BWD_IN = 256
TM_MIX = 256
TM_FFN = 256
TK_WGRAD = 1024
MIX_POOL_PROJ, MIX_CONV_OUT, MIX_O = 0, 1, 2
MIX_COLS = 3 * D + CG
VMEM_BIG = 56 * 1024 * 1024
MESH = pl.DeviceIdType.MESH
ANY = pl.BlockSpec(memory_space=pl.ANY)


def _cparams(n_axes, vmem=None):
    return pltpu.CompilerParams(dimension_semantics=("arbitrary",) * n_axes, vmem_limit_bytes=vmem)


def _dot(a, b):
    return jnp.dot(a, b, preferred_element_type=F32)


def _dot_nt(a, b):
    return lax.dot_general(a, b, (((1,), (1,)), ((), ())), preferred_element_type=F32)


def _dot_tn(a, b):
    return lax.dot_general(a, b, (((0,), (0,)), ((), ())), preferred_element_type=F32)


def _shift_down(ext, s, lead):
    return pltpu.roll(ext, s, 0)[lead:]


def _shift_up(ext, s, tm):
    n = ext.shape[0]
    return pltpu.roll(ext, n - s, 0)[:tm]


def _rms_inv(x):
    return lax.rsqrt(jnp.mean(x * x, axis=-1, keepdims=True) + RMS_EPS)


def _rms_bwd(dh, xn, inv, g):
    dxn = dh * g
    return inv * (dxn - xn * jnp.mean(dxn * xn, axis=-1, keepdims=True))


def _pos():
    return lax.axis_index("x"), lax.axis_index("y"), lax.axis_index("c")


def _handshake(peers):
    barrier = pltpu.get_barrier_semaphore()
    for peer in peers:
        pl.semaphore_signal(barrier, inc=1, device_id=peer, device_id_type=MESH)
    pl.semaphore_wait(barrier, len(peers))


def _sequencer(body, out_type, n_sems, name, collective_id):
    return pl.kernel(
        body, out_type=out_type, mesh=plsc.ScalarSubcoreMesh(axis_name="sequencer", num_cores=1), name=name,
        scratch_types=[pltpu.SemaphoreType.DMA((n_sems,)), pltpu.SemaphoreType.DMA((n_sems,))],
        compiler_params=pltpu.CompilerParams(collective_id=collective_id))


def all_gather_blocks(shards, name, collective_id):
    n = len(shards)

    def body(*refs):
        ins, outs = refs[:n], refs[n:2 * n]
        send_sems, recv_sems = refs[2 * n:]
        x, y, c = _pos()
        me, sibling = (x, y, c), (x, y, 1 - c)
        first_chip, second_chip, diagonal = (x ^ (1 - c), y ^ c), (x ^ c, y ^ (1 - c)), (1 - x, 1 - y)
        first, second = (*first_chip, c), (*second_chip, c)
        _handshake([sibling, first, second])

        def copy(w, k, block, to, src=None):
            slot = outs[w].at[4 * block[0] + 2 * block[1] + block[2]]
            return pltpu.make_async_remote_copy(
                src_ref=slot if src is None else src, dst_ref=slot,
                send_sem=send_sems.at[8 * w + k], recv_sem=recv_sems.at[8 * w + k], device_id=to, device_id_type=MESH)

        mine, sent = [], []
        for w in range(n):
            m = pltpu.make_async_copy(ins[w], outs[w].at[4 * x + 2 * y + c], send_sems.at[8 * w + 7])
            m.start()
            mine.append(m)
            sent += [copy(w, k, me, to, src=ins[w]) for k, to in enumerate((sibling, first, second))]
        for cp in sent:
            cp.start()
        for k, chip in ((1, first_chip), (2, second_chip), (3, diagonal)):
            for w in range(n):
                copy(w, k, (*chip, c), me).wait_recv()
                onward = [copy(w, 3 + k, (*chip, c), sibling)] + ([copy(w, 3, (*chip, c), second)] if k == 1 else [])
                for cp in onward:
                    cp.start()
                sent += onward
        for w in range(n):
            copy(w, 0, sibling, me).wait_recv()
            for k, chip in ((4, second_chip), (5, first_chip), (6, diagonal)):
                copy(w, k, (*chip, 1 - c), me).wait_recv()
        for cp in sent:
            cp.wait_send()
        for m in mine:
            m.wait()

    out = [jax.ShapeDtypeStruct((NDEV,) + s.shape, s.dtype) for s in shards]
    return _sequencer(body, out, 8 * n, name, collective_id)(*shards)


def _exchange_small(v, reduce, name, gathered=None):
    rows = v.shape[0]

    def body(*refs):
        if gathered is None:
            v_ref, out_ref, slots, send_sems, recv_sems, local_sem = refs
        else:
            v_ref, g_ref, out_ref, gsum_ref, slots, send_sems, recv_sems, local_sem = refs
        x, y, c = _pos()
        me = 4 * x + 2 * y + c
        mine = pltpu.make_async_copy(v_ref, slots.at[me], local_sem)
        mine.start()
        offs = [(dx, dy, dc) for dx in (0, 1) for dy in (0, 1) for dc in (0, 1)][1:]

        def copy(k, src_slot, to):
            return pltpu.make_async_remote_copy(
                src_ref=v_ref, dst_ref=slots.at[src_slot], send_sem=send_sems.at[k], recv_sem=recv_sems.at[k],
                device_id=to, device_id_type=MESH)

        sends = []
        for k, (dx, dy, dc) in enumerate(offs):
            cp = copy(k, me, (x ^ dx, y ^ dy, c ^ dc))
            cp.start()
            sends.append(cp)
        for k, (dx, dy, dc) in enumerate(offs):
            copy(k, 4 * (x ^ dx) + 2 * (y ^ dy) + (c ^ dc), (x, y, c)).wait_recv()
        for cp in sends:
            cp.wait_send()
        mine.wait()
        if reduce:
            acc = slots[0]
            for d in range(1, NDEV):
                acc = acc + slots[d]
            out_ref[...] = acc
        else:
            out_ref[...] = slots[...]
        if gathered is not None:
            acc = g_ref[0]
            for d in range(1, NDEV):
                acc = acc + g_ref[d]
            gsum_ref[...] = acc

    vmem = pl.BlockSpec(memory_space=pltpu.VMEM)
    out = jax.ShapeDtypeStruct((rows, D) if reduce else (NDEV, rows, D), F32)
    args, out_shape, out_specs = [v], out, vmem
    if gathered is not None:
        args.append(gathered)
        out_shape, out_specs = [out, jax.ShapeDtypeStruct(gathered.shape[1:], F32)], [vmem, vmem]
    return pl.pallas_call(
        body, name=name, out_shape=out_shape, in_specs=[vmem] * len(args), out_specs=out_specs,
        scratch_shapes=[pltpu.VMEM((NDEV, rows, D), F32), pltpu.SemaphoreType.DMA((7,)),
                        pltpu.SemaphoreType.DMA((7,)), pltpu.SemaphoreType.DMA],
    )(*args)


def reduce_scatter_d2d(grads, name, collective_id):
    n = len(grads)

    def body(*refs):
        ins, outs = refs[:n], refs[n:2 * n]
        send_sems, recv_sems = refs[2 * n:]
        x, y, c = _pos()
        _handshake([(x, y, 1 - c)])
        cps = []
        for w in range(n):
            cp = pltpu.make_async_remote_copy(
                src_ref=ins[w].at[1 - c], dst_ref=outs[w], send_sem=send_sems.at[w], recv_sem=recv_sems.at[w],
                device_id=(x, y, 1 - c), device_id_type=MESH)
            cp.start()
            cps.append(cp)
        for cp in cps:
            cp.wait_recv()
        for cp in cps:
            cp.wait_send()

    out = [jax.ShapeDtypeStruct(g.shape[1:], F32) for g in grads]
    return _sequencer(body, out, n, name, collective_id)(*grads)


def reduce_scatter_ici(parts, name, collective_id):
    n = len(parts)

    def body(*refs):
        ins, outs = refs[:n], refs[n:2 * n]
        send_sems, recv_sems = refs[2 * n:]
        x, y, c = _pos()
        offs = [(1, 0), (0, 1), (1, 1)]
        _handshake([(x ^ dx, y ^ dy, c) for dx, dy in offs])
        cps = []
        for w in range(n):
            for k, (dx, dy) in enumerate(offs):
                ox, oy = x ^ dx, y ^ dy
                cp = pltpu.make_async_remote_copy(
                    src_ref=ins[w].at[2 * ox + oy], dst_ref=outs[w].at[2 * x + y],
                    send_sem=send_sems.at[3 * w + k], recv_sem=recv_sems.at[3 * w + k],
                    device_id=(ox, oy, c), device_id_type=MESH)
                cp.start()
                cps.append((cp, w, k, ox, oy))
        for cp, w, k, ox, oy in cps:
            pltpu.make_async_remote_copy(
                src_ref=ins[w].at[2 * ox + oy], dst_ref=outs[w].at[2 * ox + oy],
                send_sem=send_sems.at[3 * w + k], recv_sem=recv_sems.at[3 * w + k],
                device_id=(ox, oy, c), device_id_type=MESH).wait_recv()
        for cp, *_ in cps:
            cp.wait_send()

    out = [jax.ShapeDtypeStruct(p.shape, BF16) for p in parts]
    return _sequencer(body, out, 3 * n, name, collective_id)(*parts)


def fwd_in(x, g1, w_in_pieces, tm):
    t = x.shape[0]
    tile = lambda i: (i, 0)
    w_spec = pl.BlockSpec((NDEV, D, CG), lambda i: (0, 0, 0))
    z_spec = pl.BlockSpec((NDEV, tm, CG), lambda i: (0, i, 0))
    z_shape = jax.ShapeDtypeStruct((NDEV, t, CG), BF16)

    def cost(other_bytes, transcendentals):
        return pl.CostEstimate(flops=2 * t * D * NDEV * CG, transcendentals=transcendentals,
                               bytes_accessed=other_bytes + 2 * D * NDEV * CG + 2 * t * NDEV * CG)

    def first(x_ref, g_ref, w_ref, z_ref, h_ref, ht_ref):
        xf = x_ref[...]
        h = (xf * _rms_inv(xf) * g_ref[...]).astype(BF16)
        h_ref[...] = h
        ht_ref[...] = h.T
        for j in range(NDEV):
            z_ref[j] = _dot(h, w_ref[j]).astype(BF16)

    z0, h, ht = pl.pallas_call(
        first, name="fwd_in_0", grid=(t // tm,),
        in_specs=[pl.BlockSpec((tm, D), tile), pl.BlockSpec((1, D), lambda i: (0, 0)), w_spec],
        out_specs=[z_spec, pl.BlockSpec((tm, D), tile), pl.BlockSpec((D, tm), lambda i: (0, i))],
        out_shape=[z_shape, jax.ShapeDtypeStruct((t, D), BF16), jax.ShapeDtypeStruct((D, t), BF16)],
        compiler_params=_cparams(1, VMEM_BIG), cost_estimate=cost(8 * t * D + 4 * D, t),
    )(x, g1, w_in_pieces[0])
    zs = [z0]
    for q in (1, 2):
        h, zs[-1] = lax.optimization_barrier((h, zs[-1]))

        def later(h_ref, w_ref, z_ref):
            hb = h_ref[...]
            for j in range(NDEV):
                z_ref[j] = _dot(hb, w_ref[j]).astype(BF16)

        zs.append(pl.pallas_call(
            later, name=f"fwd_in_{q}", grid=(t // tm,),
            in_specs=[pl.BlockSpec((tm, D), tile), w_spec], out_specs=z_spec, out_shape=z_shape,
            compiler_params=_cparams(1, VMEM_BIG), cost_estimate=cost(2 * t * D, 0),
        )(h, w_in_pieces[q]))
    return zs, ht


def _z_tile(z_refs, n):
    return z_refs[n % 3][n // 3]


def _pool_tile(z, zh, win, keep_hist, cnt):
    zt = z.astype(F32)
    ext = jnp.concatenate([zh.astype(F32) * keep_hist, zt], axis=0)
    s, sh = ext, 1
    while sh < win:
        s = s + pltpu.roll(s, sh, 0)
        sh *= 2
    return s[HALO:] / cnt - zt


def _conv_taps(ext, cur, w_ref, lanes, lead):
    x1 = _shift_down(ext, 1, lead)
    x2 = _shift_down(ext, 2, lead)
    out = w_ref[2:3, lanes] * cur + w_ref[1:2, lanes] * x1 + w_ref[0:1, lanes] * x2
    return out, x1, x2


def fwd_mix(zs, x, pool_w, pool_scale, conv_w, wmix, tm, seq):
    t = x.shape[0]
    tps = seq // tm
    hb = tm // HALO

    def body(z0_ref, z1_ref, z2_ref, zh0_ref, zh1_ref, zh2_ref, x_ref, pw_ref, ps_ref, wpp_ref, cw_ref, wco_ref,
             wo_ref, x1_ref, yp_ref, yc_ref):
        z_refs, zh_refs = (z0_ref, z1_ref, z2_ref), (zh0_ref, zh1_ref, zh2_ref)
        it = pl.program_id(0) % tps
        keep_hist = jnp.where(it == 0, 0.0, 1.0)
        pos = it * tm + lax.broadcasted_iota(jnp.int32, (tm, 1), 0)
        p2 = []
        for g, win in enumerate(WINS):
            cnt = jnp.minimum(pos + 1, win).astype(F32)
            p = _pool_tile(_z_tile(z_refs, g), _z_tile(zh_refs, g), win, keep_hist, cnt)
            lanes = slice(g * CG, (g + 1) * CG)
            p2.append((_dot(p.astype(BF16), pw_ref[g]) * ps_ref[:, lanes]).astype(BF16))
        y_pool = _dot(jnp.concatenate(p2, axis=1), wpp_ref[...])
        u = []
        for q in range(NG):
            lanes = slice(q * CG, (q + 1) * CG)
            cv = _z_tile(z_refs, 8 + q).astype(F32) * _z_tile(z_refs, 12 + q).astype(F32)
            cvh = _z_tile(zh_refs, 8 + q).astype(F32) * _z_tile(zh_refs, 12 + q).astype(F32) * keep_hist
            cc, _, _ = _conv_taps(jnp.concatenate([cvh, cv], axis=0), cv, cw_ref, lanes, HALO)
            u.append((_z_tile(z_refs, 4 + q).astype(F32) * cc).astype(BF16))
        y_conv = _dot(jnp.concatenate(u, axis=1), wco_ref[...])
        ypb, ycb = y_pool.astype(BF16), y_conv.astype(BF16)
        yp_ref[...] = ypb
        yc_ref[...] = ycb
        merged = []
        for q in range(NG):
            lanes = slice(q * CG, (q + 1) * CG)
            sp = jax.nn.sigmoid(_z_tile(z_refs, 16 + q).astype(F32))
            sc = jax.nn.sigmoid(_z_tile(z_refs, 20 + q).astype(F32))
            merged.append((sp * ypb[:, lanes].astype(F32) + sc * ycb[:, lanes].astype(F32)).astype(BF16))
        x1_ref[...] = x_ref[...] + _dot(jnp.concatenate(merged, axis=1), wo_ref[...])

    def hist(i):
        return jnp.maximum(i * hb - 1, 0)

    const2 = lambda i: (0, 0)
    return pl.pallas_call(
        body, name="fwd_mix", grid=(t // tm,),
        in_specs=[pl.BlockSpec((NDEV, tm, CG), lambda i: (0, i, 0))] * 3
                 + [pl.BlockSpec((NDEV, HALO, CG), lambda i: (0, hist(i), 0))] * 3
                 + [pl.BlockSpec((tm, D), lambda i: (i, 0)),
                    pl.BlockSpec((NG, CG, CG), lambda i: (0, 0, 0)), pl.BlockSpec((1, D), const2),
                    pl.BlockSpec((D, D), lambda i: (0, MIX_POOL_PROJ)), pl.BlockSpec((3, D), const2),
                    pl.BlockSpec((D, D), lambda i: (0, MIX_CONV_OUT)), pl.BlockSpec((D, D), lambda i: (0, MIX_O))],
        out_specs=[pl.BlockSpec((tm, D), lambda i: (i, 0))] * 3,
        out_shape=[jax.ShapeDtypeStruct((t, D), F32), jax.ShapeDtypeStruct((t, D), BF16),
                   jax.ShapeDtypeStruct((t, D), BF16)],
        compiler_params=_cparams(1, VMEM_BIG),
    )(*zs, *zs, x, pool_w, pool_scale, wmix, conv_w, wmix, wmix)


def fwd_up(x1, g2, w_up_g, fcw, fcb, tm, seq):
    t = x1.shape[0]
    tps = seq // tm

    def body(x1_ref, g2_ref, wup_ref, fcw_ref, fcb_ref, up_ref, pre_ref, act_ref, actt_ref, h2t_ref, hist_ref):
        i = pl.program_id(0)
        keep_hist = jnp.where(i % tps == 0, 0.0, 1.0)

        @pl.when(i == 0)
        def _():
            hist_ref[...] = jnp.zeros_like(hist_ref)

        x1v = x1_ref[...]
        h2 = (x1v * _rms_inv(x1v) * g2_ref[...]).astype(BF16)
        h2t_ref[...] = h2.T
        lanes = slice(0, SH_UP)
        for c in range(NCH):
            conv = []
            for s in range(2):
                ub = _dot(h2, wup_ref[s, c]).astype(BF16)
                up_ref[s, c] = ub
                uf = ub.astype(F32)
                ext = jnp.concatenate([hist_ref[s, c] * keep_hist, uf], axis=0)
                hist_ref[s, c] = uf[tm - 8:]
                cc, _, _ = _conv_taps(ext, uf, fcw_ref.at[s, c], lanes, 8)
                conv.append(cc + fcb_ref[s, c])
                pre_ref[s, c] = conv[s].astype(BF16)
            a = (conv[0] * jax.nn.sigmoid(conv[0]) * conv[1]).astype(BF16)
            act_ref[c] = a
            actt_ref[c] = a.T

    tile = lambda i: (i, 0)
    const2 = lambda i: (0, 0)
    whole = lambda i: (0, 0, 0, 0)
    chunks = pl.BlockSpec((2, NCH, tm, SH_UP), lambda i: (0, 0, i, 0))
    return pl.pallas_call(
        body, name="fwd_up", grid=(t // tm,),
        in_specs=[pl.BlockSpec((tm, D), tile), pl.BlockSpec((1, D), const2),
                  pl.BlockSpec((2, NCH, D, SH_UP), whole), pl.BlockSpec((2, NCH, 3, SH_UP), whole),
                  pl.BlockSpec((2, NCH, 1, SH_UP), whole)],
        out_specs=[chunks, chunks, pl.BlockSpec((NCH, tm, SH_UP), lambda i: (0, i, 0)),
                   pl.BlockSpec((NCH, SH_UP, tm), lambda i: (0, 0, i)), pl.BlockSpec((D, tm), lambda i: (0, i))],
        out_shape=[jax.ShapeDtypeStruct((2, NCH, t, SH_UP), BF16), jax.ShapeDtypeStruct((2, NCH, t, SH_UP), BF16),
                   jax.ShapeDtypeStruct((NCH, t, SH_UP), BF16), jax.ShapeDtypeStruct((NCH, SH_UP, t), BF16),
                   jax.ShapeDtypeStruct((D, t), BF16)],
        scratch_shapes=[pltpu.VMEM((2, NCH, 8, SH_UP), F32)],
        compiler_params=_cparams(1, VMEM_BIG),
    )(x1, g2, w_up_g.reshape(2, NCH, D, SH_UP), fcw.reshape(2, NCH, 3, SH_UP), fcb.reshape(2, NCH, 1, SH_UP))


def fwd_down(x1, act, w_dn, gf, tgt, tm):
    t = x1.shape[0]

    def body(x1_ref, act_ref, wdn_ref, gf_ref, tgt_ref, dx2_ref, dx2b_ref, vec_ref):
        @pl.when(pl.program_id(0) == 0)
        def _():
            vec_ref[...] = jnp.zeros_like(vec_ref)

        d = None
        for c in range(NCH):
            part = _dot(act_ref[c], wdn_ref[c])
            d = part if d is None else d + part
        x2 = x1_ref[...] + d
        inv3 = _rms_inv(x2)
        xn = x2 * inv3
        diff = xn * gf_ref[...] - tgt_ref[...]
        dy = diff * (1.0 / D)
        vec_ref[0:1, :] += jnp.sum(dy * xn, axis=0, keepdims=True)
        vec_ref[1:2, :] += 0.5 * jnp.sum(jnp.mean(diff * diff, axis=-1))
        dx2 = _rms_bwd(dy, xn, inv3, gf_ref[...])
        dx2_ref[...] = dx2
        dx2b_ref[...] = dx2.astype(BF16)

    tile = lambda i: (i, 0)
    const2 = lambda i: (0, 0)
    return pl.pallas_call(
        body, name="fwd_down", grid=(t // tm,),
        in_specs=[pl.BlockSpec((tm, D), tile), pl.BlockSpec((NCH, tm, SH_UP), lambda i: (0, i, 0)),
                  pl.BlockSpec((NCH, SH_UP, D), lambda i: (0, 0, 0)), pl.BlockSpec((1, D), const2),
                  pl.BlockSpec((tm, D), tile)],
        out_specs=[pl.BlockSpec((tm, D), tile), pl.BlockSpec((tm, D), tile), pl.BlockSpec((8, D), const2)],
        out_shape=[jax.ShapeDtypeStruct((t, D), F32), jax.ShapeDtypeStruct((t, D), BF16),
                   jax.ShapeDtypeStruct((8, D), F32)],
        compiler_params=_cparams(1, VMEM_BIG),
    )(x1, act, w_dn, gf, tgt)


def bwd_ffn(dx2, x1, g2, up, pre, w_up_g, fcw, w_dn, tm, seq):
    t = x1.shape[0]
    nt = t // tm
    tps = seq // tm

    def body(dx2_ref, x1_ref, g2_ref, up_ref, pre_ref, wup_ref, fcw_ref, wdn_ref,
             dup_ref, dx1_ref, gvec_ref, gn_ref, carry_ref):
        i = pl.program_id(0)
        it = (nt - 1 - i) % tps
        keep_next = jnp.where(it == tps - 1, 0.0, 1.0)

        @pl.when(i == 0)
        def _():
            gvec_ref[...] = jnp.zeros_like(gvec_ref)
            gn_ref[...] = jnp.zeros_like(gn_ref)
            carry_ref[...] = jnp.zeros_like(carry_ref)

        dx2v = dx2_ref[...]
        dxb = dx2v.astype(BF16)
        lanes = slice(0, SH_UP)
        dh2 = None
        for c in range(NCH):
            pre = [pre_ref[s, c].astype(F32) for s in range(2)]
            sg = jax.nn.sigmoid(pre[0])
            dact = _dot_nt(dxb, wdn_ref[c])
            dpre = [dact * pre[1] * (sg * (1.0 + pre[0] * (1.0 - sg))), dact * (pre[0] * sg)]
            for s in range(2):
                dc = dpre[s]
                ext = jnp.concatenate([dc, carry_ref[s, c] * keep_next], axis=0)
                carry_ref[s, c] = dc[:8]
                shifted = (_shift_up(ext, 2, tm), _shift_up(ext, 1, tm), dc)
                uf = up_ref[s, c].astype(F32)
                gvec_ref[s, c, 0:1, lanes] += jnp.sum(dc, axis=0, keepdims=True)
                for tap in range(3):
                    gvec_ref[s, c, tap + 1:tap + 2, lanes] += jnp.sum(shifted[tap] * uf, axis=0, keepdims=True)
                w = fcw_ref.at[s, c]
                du = w[2:3, :] * dc + w[1:2, :] * shifted[1] + w[0:1, :] * shifted[0]
                dub = du.astype(BF16)
                dup_ref[s, c] = dub
                part = _dot_nt(dub, wup_ref[s, c])
                dh2 = part if dh2 is None else dh2 + part
        x1v = x1_ref[...]
        inv2 = _rms_inv(x1v)
        xn = x1v * inv2
        gn_ref[0:1, :] += jnp.sum(dh2 * xn, axis=0, keepdims=True)
        dx1_ref[...] = dx2v + _rms_bwd(dh2, xn, inv2, g2_ref[...])

    rev = lambda i: (nt - 1 - i, 0)
    const2 = lambda i: (0, 0)
    whole = lambda i: (0, 0, 0, 0)
    chunks = pl.BlockSpec((2, NCH, tm, SH_UP), lambda i: (0, 0, nt - 1 - i, 0))
    return pl.pallas_call(
        body, name="bwd_ffn", grid=(nt,),
        in_specs=[pl.BlockSpec((tm, D), rev), pl.BlockSpec((tm, D), rev), pl.BlockSpec((1, D), const2),
                  chunks, chunks, pl.BlockSpec((2, NCH, D, SH_UP), whole), pl.BlockSpec((2, NCH, 3, SH_UP), whole),
                  pl.BlockSpec((NCH, SH_UP, D), lambda i: (0, 0, 0))],
        out_specs=[chunks, pl.BlockSpec((tm, D), rev), pl.BlockSpec((2, NCH, 8, D), whole),
                   pl.BlockSpec((8, D), const2)],
        out_shape=[jax.ShapeDtypeStruct((2, NCH, t, SH_UP), BF16), jax.ShapeDtypeStruct((t, D), F32),
                   jax.ShapeDtypeStruct((2, NCH, 8, D), F32), jax.ShapeDtypeStruct((8, D), F32)],
        scratch_shapes=[pltpu.VMEM((2, NCH, 8, SH_UP), F32)],
        compiler_params=_cparams(1, VMEM_BIG),
    )(dx2, x1, g2, up, pre, w_up_g.reshape(2, NCH, D, SH_UP), fcw.reshape(2, NCH, 3, SH_UP), w_dn)


def bwd_mix(dx1, zs, y_pool, y_conv, pool_w, pool_scale, conv_w, wmix, tm, seq):
    t = dx1.shape[0]
    nt = t // tm
    tps = seq // tm
    hb = tm // HALO

    def body(da_ref, z0_ref, z1_ref, z2_ref, zh0_ref, zh1_ref, zh2_ref, yp_ref, yc_ref, pw_ref, ps_ref, wpp_ref,
             cw_ref, wco_ref, wo_ref,
             dz_ref, mg_ref, p2_ref, u_ref, dyp_ref, dyc_ref, p_ref, dpw_ref, gvec_ref, cp_ref, cc_ref):
        z_refs, zh_refs = (z0_ref, z1_ref, z2_ref), (zh0_ref, zh1_ref, zh2_ref)
        i = pl.program_id(0)
        it = (nt - 1 - i) % tps
        keep_hist = jnp.where(it == 0, 0.0, 1.0)
        keep_next = jnp.where(it == tps - 1, 0.0, 1.0)
        pos = it * tm + lax.broadcasted_iota(jnp.int32, (tm, 1), 0)

        @pl.when(i == 0)
        def _():
            gvec_ref[...] = jnp.zeros_like(gvec_ref)
            cp_ref[...] = jnp.zeros_like(cp_ref)
            cc_ref[...] = jnp.zeros_like(cc_ref)

        dm = _dot_nt(da_ref[...].astype(BF16), wo_ref[...])
        merged, dyp, dyc = [], [], []
        for q in range(NG):
            lanes = slice(q * CG, (q + 1) * CG)
            sp = jax.nn.sigmoid(_z_tile(z_refs, 16 + q).astype(F32))
            sc = jax.nn.sigmoid(_z_tile(z_refs, 20 + q).astype(F32))
            yp = yp_ref[:, lanes].astype(F32)
            yc = yc_ref[:, lanes].astype(F32)
            dmq = dm[:, lanes]
            merged.append((sp * yp + sc * yc).astype(BF16))
            dyp.append((dmq * sp).astype(BF16))
            dyc.append((dmq * sc).astype(BF16))
            dz_ref[16 + q] = (dmq * yp * (sp * (1.0 - sp))).astype(BF16)
            dz_ref[20 + q] = (dmq * yc * (sc * (1.0 - sc))).astype(BF16)
        mg_ref[...] = jnp.concatenate(merged, axis=1)
        dypb = jnp.concatenate(dyp, axis=1)
        dycb = jnp.concatenate(dyc, axis=1)
        dyp_ref[...] = dypb
        dyc_ref[...] = dycb

        dp2 = _dot_nt(dypb, wpp_ref[...])
        p2 = []
        for g, win in enumerate(WINS):
            lanes = slice(g * CG, (g + 1) * CG)
            cnt = jnp.minimum(pos + 1, win).astype(F32)
            p = _pool_tile(_z_tile(z_refs, g), _z_tile(zh_refs, g), win, keep_hist, cnt)
            pb = p.astype(BF16)
            p_ref[g] = pb
            pw = _dot(pb, pw_ref[g])
            p2.append((pw * ps_ref[:, lanes]).astype(BF16))
            dp2g = dp2[:, lanes]
            gvec_ref[0:1, lanes] += jnp.sum(dp2g * pw, axis=0, keepdims=True)
            dpwb = (dp2g * ps_ref[:, lanes]).astype(BF16)
            dpw_ref[g] = dpwb
            dp = _dot_nt(dpwb, pw_ref[g])
            qv = dp / cnt
            ext = jnp.concatenate([qv, cp_ref[g] * keep_next], axis=0)
            cp_ref[g] = qv[:HALO]
            n = tm + HALO
            s, sh = ext, 1
            while sh < win:
                s = s + pltpu.roll(s, n - sh, 0)
                sh *= 2
            dz_ref[g] = (s[:tm] - dp).astype(BF16)
        p2_ref[...] = jnp.concatenate(p2, axis=1)

        du = _dot_nt(dycb, wco_ref[...])
        u = []
        for q in range(NG):
            lanes = slice(q * CG, (q + 1) * CG)
            zb = _z_tile(z_refs, 4 + q).astype(F32)
            zc = _z_tile(z_refs, 8 + q).astype(F32)
            zv = _z_tile(z_refs, 12 + q).astype(F32)
            cv = zc * zv
            cvh = _z_tile(zh_refs, 8 + q).astype(F32) * _z_tile(zh_refs, 12 + q).astype(F32) * keep_hist
            cc, cv1, cv2 = _conv_taps(jnp.concatenate([cvh, cv], axis=0), cv, cw_ref, lanes, HALO)
            u.append((zb * cc).astype(BF16))
            duq = du[:, lanes]
            dz_ref[4 + q] = (duq * cc).astype(BF16)
            dcc = duq * zb
            for tap, src in enumerate((cv2, cv1, cv)):
                gvec_ref[tap + 1:tap + 2, lanes] += jnp.sum(dcc * src, axis=0, keepdims=True)
            ext = jnp.concatenate([dcc, cc_ref[:, lanes] * keep_next], axis=0)
            cc_ref[:, lanes] = dcc[:8]
            dcv = (cw_ref[2:3, lanes] * dcc + cw_ref[1:2, lanes] * _shift_up(ext, 1, tm)
                   + cw_ref[0:1, lanes] * _shift_up(ext, 2, tm))
            dz_ref[8 + q] = (dcv * zv).astype(BF16)
            dz_ref[12 + q] = (dcv * zc).astype(BF16)
        u_ref[...] = jnp.concatenate(u, axis=1)

    def hist(i):
        return jnp.maximum((nt - 1 - i) * hb - 1, 0)

    rev = lambda i: (nt - 1 - i, 0)
    rev3 = lambda i: (0, nt - 1 - i, 0)
    const2 = lambda i: (0, 0)
    tok = jax.ShapeDtypeStruct((t, D), BF16)
    grp = jax.ShapeDtypeStruct((NG, t, CG), BF16)
    return pl.pallas_call(
        body, name="bwd_mix", grid=(nt,),
        in_specs=[pl.BlockSpec((tm, D), rev)] + [pl.BlockSpec((NDEV, tm, CG), rev3)] * 3
                 + [pl.BlockSpec((NDEV, HALO, CG), lambda i: (0, hist(i), 0))] * 3
                 + [pl.BlockSpec((tm, D), rev), pl.BlockSpec((tm, D), rev),
                    pl.BlockSpec((NG, CG, CG), lambda i: (0, 0, 0)), pl.BlockSpec((1, D), const2),
                    pl.BlockSpec((D, D), lambda i: (0, MIX_POOL_PROJ)), pl.BlockSpec((3, D), const2),
                    pl.BlockSpec((D, D), lambda i: (0, MIX_CONV_OUT)), pl.BlockSpec((D, D), lambda i: (0, MIX_O))],
        out_specs=[pl.BlockSpec((NZT, tm, CG), rev3)] + [pl.BlockSpec((tm, D), rev)] * 5
                  + [pl.BlockSpec((NG, tm, CG), rev3)] * 2 + [pl.BlockSpec((8, D), const2)],
        out_shape=[jax.ShapeDtypeStruct((NZT, t, CG), BF16), tok, tok, tok, tok, tok, grp, grp,
                   jax.ShapeDtypeStruct((8, D), F32)],
        scratch_shapes=[pltpu.VMEM((NG, HALO, CG), F32), pltpu.VMEM((8, D), F32)],
        compiler_params=_cparams(1, VMEM_BIG),
    )(dx1, *zs, *zs, y_pool, y_conv, pool_w, pool_scale, wmix, conv_w, wmix, wmix)


def bwd_in(dz, w_in_pieces, dx1, x, g1, tm):
    t = x.shape[0]

    def body(dz_ref, w0_ref, w1_ref, w2_ref, dx1_ref, x_ref, g_ref, gx_ref, gn_ref):
        @pl.when(pl.program_id(0) == 0)
        def _():
            gn_ref[...] = jnp.zeros_like(gn_ref)

        dh = None
        for j in range(NDEV):
            for q, w_ref in enumerate((w0_ref, w1_ref, w2_ref)):
                part = _dot_nt(dz_ref[3 * j + q], w_ref[j])
                dh = part if dh is None else dh + part
        xv = x_ref[...]
        inv = _rms_inv(xv)
        xn = xv * inv
        gn_ref[0:1, :] += jnp.sum(dh * xn, axis=0, keepdims=True)
        gx_ref[...] = dx1_ref[...] + _rms_bwd(dh, xn, inv, g_ref[...])

    tile = lambda i: (i, 0)
    return pl.pallas_call(
        body, name="bwd_in", grid=(t // tm,),
        in_specs=[pl.BlockSpec((NZT, tm, CG), lambda i: (0, i, 0))]
                 + [pl.BlockSpec((NDEV, D, CG), lambda i: (0, 0, 0))] * 3
                 + [pl.BlockSpec((tm, D), tile), pl.BlockSpec((tm, D), tile), pl.BlockSpec((1, D), lambda i: (0, 0))],
        out_specs=[pl.BlockSpec((tm, D), tile), pl.BlockSpec((8, D), lambda i: (0, 0))],
        out_shape=[jax.ShapeDtypeStruct((t, D), F32), jax.ShapeDtypeStruct((8, D), F32)],
        compiler_params=_cparams(1, VMEM_BIG),
    )(dz, *w_in_pieces, dx1, x, g1)


def _slot(j):
    return j % 2, j // 2


def wgrad_cols(at, b, q, name):
    m, t = at.shape
    width = b.shape[3]

    def body(a_ref, b_ref, o_ref):
        o_ref[...] = _dot(a_ref[...], b_ref[...])

    return pl.pallas_call(
        body, name=name, grid=(NDEV,),
        in_specs=[pl.BlockSpec((m, t), lambda j: (0, 0)),
                  pl.BlockSpec((None, None, t, width), lambda j: (j, q, 0, 0))],
        out_specs=pl.BlockSpec((None, None, m, width), lambda j: (j % 2, j // 2, 0, 0)),
        out_shape=jax.ShapeDtypeStruct((2, 4, m, width), F32),
        compiler_params=_cparams(1, VMEM_BIG),
    )(at, b)


def wgrad_down(actt, dx2b):
    t = dx2b.shape[0]

    def body(a_ref, b_ref, o_ref):
        r = _dot(a_ref[...], b_ref[...])
        o_ref[0] = r[:SH_DN]
        o_ref[1] = r[SH_DN:]

    return pl.pallas_call(
        body, name="wgrad_down", grid=(NCH,),
        in_specs=[pl.BlockSpec((None, SH_UP, t), lambda k: (k, 0, 0)), pl.BlockSpec((t, D), lambda k: (0, 0))],
        out_specs=pl.BlockSpec((2, None, SH_DN, D), lambda k: (0, k, 0, 0)),
        out_shape=jax.ShapeDtypeStruct((2, 4, SH_DN, D), F32),
        compiler_params=_cparams(1, VMEM_BIG),
    )(actt, dx2b)


def wgrad_square(a, b, name, tk):
    t = a.shape[0]

    def body(a_ref, b_ref, o_ref, acc_ref):
        kt = pl.program_id(0)

        @pl.when(kt == 0)
        def _():
            acc_ref[...] = jnp.zeros_like(acc_ref)

        acc_ref[...] += _dot_tn(a_ref[...], b_ref[...].astype(BF16))

        @pl.when(kt == pl.num_programs(0) - 1)
        def _():
            for j in range(NDEV):
                cc, xy = _slot(j)
                o_ref[cc, xy] = acc_ref[j * 128:(j + 1) * 128]

    return pl.pallas_call(
        body, name=name, grid=(t // tk,),
        in_specs=[pl.BlockSpec((tk, D), lambda k: (k, 0)), pl.BlockSpec((tk, D), lambda k: (k, 0))],
        out_specs=pl.BlockSpec((2, 4, 128, D), lambda k: (0, 0, 0, 0)),
        out_shape=jax.ShapeDtypeStruct((2, 4, 128, D), F32),
        scratch_shapes=[pltpu.VMEM((D, D), F32)],
        compiler_params=_cparams(1, VMEM_BIG),
    )(a, b)


def wgrad_pool(p, dpw, tk):
    t = p.shape[1]

    def body(a_ref, b_ref, o_ref):
        @pl.when(pl.program_id(0) == 0)
        def _():
            o_ref[...] = jnp.zeros_like(o_ref)

        for g in range(NG):
            o_ref[g] += _dot_tn(a_ref[g], b_ref[g])

    return pl.pallas_call(
        body, name="wgrad_pool", grid=(t // tk,),
        in_specs=[pl.BlockSpec((NG, tk, CG), lambda k: (0, k, 0))] * 2,
        out_specs=pl.BlockSpec((NG, CG, CG), lambda k: (0, 0, 0)),
        out_shape=jax.ShapeDtypeStruct((NG, CG, CG), F32),
        compiler_params=_cparams(1, VMEM_BIG),
    )(p, dpw)


def _adamw(w, g, m, v):
    m = ADAM_B1 * m + (1.0 - ADAM_B1) * g
    v = ADAM_B2 * v + (1.0 - ADAM_B2) * (g * g)
    m_hat = m / (1.0 - ADAM_B1 ** ADAM_STEP)
    v_hat = v / (1.0 - ADAM_B2 ** ADAM_STEP)
    delta = -ADAM_LR * (m_hat / (jnp.sqrt(v_hat) + ADAM_EPS) + ADAM_WD * w)
    return delta, m, v


def _row_block(r):
    return 512 if r % 512 == 0 else r


def chip_partial(place, g, from_sibling, name):
    _, _, r, c = g.shape

    def body(place_ref, g_ref, s_ref, o_ref):
        o_ref[...] = (g_ref[...] + s_ref[...]).astype(BF16)

    return pl.pallas_call(
        body, name=name,
        grid_spec=pltpu.PrefetchScalarGridSpec(
            num_scalar_prefetch=1, grid=(3,),
            in_specs=[pl.BlockSpec((None, None, r, c), lambda k, pr: (pr[0], pr[1] ^ (k + 1), 0, 0)),
                      pl.BlockSpec((None, r, c), lambda k, pr: (pr[1] ^ (k + 1), 0, 0))],
            out_specs=pl.BlockSpec((None, r, c), lambda k, pr: (pr[1] ^ (k + 1), 0, 0))),
        out_shape=jax.ShapeDtypeStruct((4, r, c), BF16),
        compiler_params=_cparams(1, VMEM_BIG),
    )(place, g, from_sibling)


def finish_adamw(place, gs, from_sibling, from_chips, w, m, v, name, transposed=False):
    n = len(gs)
    r = gs[0].shape[2]
    widths = [g.shape[3] for g in gs]
    c = sum(widths)
    br = _row_block(r)

    def body(place_ref, *refs):
        g_refs, s_refs, c_refs = refs[:n], refs[n:2 * n], refs[2 * n:5 * n]
        w_ref, m_ref, v_ref, og_ref, od_ref, om_ref, ov_ref = refs[5 * n:]
        cols = []
        for q in range(n):
            grad = g_refs[q][...] + s_refs[q][...]
            for k in range(3):
                grad = grad + c_refs[3 * q + k][...].astype(F32)
            cols.append(grad)
        grad = cols[0] if n == 1 else jnp.concatenate(cols, axis=1)
        if transposed:
            grad = grad.T
        og_ref[...] = grad
        od_ref[...], om_ref[...], ov_ref[...] = _adamw(w_ref[...], grad, m_ref[...], v_ref[...])

    def other(k, cq):
        return pl.BlockSpec((None, br, cq), lambda i, pr: (pr[1] ^ k, i, 0))

    row = pl.BlockSpec((c, br), lambda i, pr: (0, i)) if transposed else pl.BlockSpec((br, c), lambda i, pr: (i, 0))
    out = jax.ShapeDtypeStruct((c, r) if transposed else (r, c), F32)
    in_specs = [pl.BlockSpec((None, None, br, cq), lambda i, pr: (pr[0], pr[1], i, 0)) for cq in widths]
    in_specs += [pl.BlockSpec((None, br, cq), lambda i, pr: (pr[1], i, 0)) for cq in widths]
    in_specs += [other(k, cq) for cq in widths for k in (1, 2, 3)]
    return pl.pallas_call(
        body, name=name,
        grid_spec=pltpu.PrefetchScalarGridSpec(
            num_scalar_prefetch=1, grid=(r // br,), in_specs=in_specs + [row, row, row], out_specs=[row] * 4),
        out_shape=[out] * 4,
        compiler_params=_cparams(1, VMEM_BIG),
    )(place, *gs, *from_sibling, *[fc for fc in from_chips for _ in range(3)], w, m, v)


def adamw_small(items):
    n = len(items)

    def body(*refs):
        ins, outs = refs[:4 * n], refs[4 * n:]
        for i in range(n):
            w, g, m, v = (r[...] for r in ins[4 * i:4 * i + 4])
            outs[3 * i][...], outs[3 * i + 1][...], outs[3 * i + 2][...] = _adamw(w, g, m, v)

    out = [jax.ShapeDtypeStruct(it[0].shape, F32) for it in items for _ in range(3)]
    res = pl.pallas_call(body, name="adamw_small", out_shape=out)(*[a for it in items for a in it])
    return [res[3 * i:3 * i + 3] for i in range(n)]


def kernel(x, norm_mix, w_in, pool_w, pool_scale, w_pool_proj, conv_w, w_conv_out, w_o, norm_ffn, w_up, ffn_conv_w, ffn_conv_b, w_down, norm_final, loss_target, m_norm_mix, m_w_in, m_pool_w, m_pool_scale, m_w_pool_proj, m_conv_w, m_w_conv_out, m_w_o, m_norm_ffn, m_w_up, m_ffn_conv_w, m_ffn_conv_b, m_w_down, m_norm_final, v_norm_mix, v_w_in, v_pool_w, v_pool_scale, v_w_pool_proj, v_conv_w, v_w_conv_out, v_w_o, v_norm_ffn, v_w_up, v_ffn_conv_w, v_ffn_conv_b, v_w_down, v_norm_final):
    nb, seq, _ = x.shape
    t = nb * seq
    tm_in = min(TM_IN, t)
    tm_mix = min(TM_MIX, seq)
    tm_ffn = min(TM_FFN, seq)
    tk = min(TK_WGRAD, t)
    xt = x.reshape(t, D)
    tgt = loss_target.reshape(t, D)
    xi, yi, ci = _pos()
    me = 4 * xi + 2 * yi + ci
    place = jnp.stack([ci, 2 * xi + yi]).astype(jnp.int32)

    tie = lax.optimization_barrier
    w_in_b = w_in[0].astype(BF16)
    w_in_g = [all_gather_blocks([w_in_b[:, q * CG:(q + 1) * CG]], f"all_gather_w_in_{q}", 0)[0] for q in range(3)]
    taps = (jnp.pad(conv_w[0], ((0, 5), (0, D - 128))) + jnp.pad(ffn_conv_w[0], ((3, 2), (0, D - SH_UP))))
    taps_g = _exchange_small(taps, False, "all_gather_taps")
    mix_shard = jnp.concatenate(
        [w_pool_proj[0], w_conv_out[0], w_o[0], pool_w[0].reshape(NG * 32, CG)], axis=1).astype(BF16)
    mix_shard, taps_g = tie((mix_shard, taps_g))
    wmix_g, = all_gather_blocks([mix_shard], "all_gather_w_mix", 0)
    ffn_shards, w_in_g[0] = tie(([w_up[0].astype(BF16), w_down[0].astype(BF16)], w_in_g[0]))
    w_up_g, = all_gather_blocks(ffn_shards[:1], "all_gather_w_up", 0)
    w_dn_g, = all_gather_blocks(ffn_shards[1:], "all_gather_w_down", 0)
    w_dn_f = w_dn_g.reshape(NCH, SH_UP, D)
    conv_w_f = taps_g[:, 0:3, :128].transpose(1, 0, 2).reshape(3, D)
    fcw_f = taps_g[:, 3:6, :SH_UP]
    fcb_f = ffn_conv_b.reshape(NDEV, 1, SH_UP)
    gfin = norm_final.reshape(1, D)

    zs, h1 = fwd_in(xt, norm_mix, w_in_g, tm_in)
    wmix_g, zs = tie((wmix_g, zs))
    wmix = wmix_g.reshape(D, MIX_COLS)
    pool_w_f = wmix_g[:, :, 3 * D:].reshape(NDEV, NG, 32, CG).transpose(1, 0, 2, 3).reshape(NG, CG, CG)
    x1, y_pool, y_conv = fwd_mix(zs, xt, pool_w_f, pool_scale, conv_w_f, wmix, tm_mix, seq)
    up, pre, act_tok, act, h2 = fwd_up(x1, norm_ffn, w_up_g, fcw_f, fcb_f, tm_ffn, seq)
    dx2, dx2b, ffn_vec = fwd_down(x1, act_tok, w_dn_f, gfin, tgt, min(TM_IN, t))

    def to_sibling(full, tag):
        return reduce_scatter_d2d(full, "reduce_scatter_d2d_" + tag, 1)

    def partials(full, from_sib, names):
        return [chip_partial(place, g, s, "chip_partial_" + nm) for g, s, nm in zip(full, from_sib, names)]

    def to_chips(parts, tag):
        return reduce_scatter_ici(parts, "reduce_scatter_ici_" + tag, 2)

    def finish(nm, gs, from_sib, from_chips, wmv, transposed=False):
        rc = (gs[0].shape[2], sum(g.shape[3] for g in gs))
        wmv2 = [a.reshape(rc).T if transposed else a.reshape(rc) for a in wmv]
        outs = finish_adamw(place, gs, from_sib, from_chips, *wmv2, "adamw_" + nm, transposed)
        return [(o.T if transposed else o).reshape(wmv[0].shape) for o in outs]

    def after(x, dep):
        return tie((x, dep))[0]

    big = {}
    d_up, dx1, g_ffn_vec, g_nffn = bwd_ffn(dx2, x1, norm_ffn, up, pre, w_up_g, fcw_f, w_dn_f, tm_ffn, seq)
    gw_up = wgrad_cols(h2, d_up.reshape(NDEV, 1, t, SH_UP), 0, "wgrad_up")
    sib_up = to_sibling([gw_up], "w_up")
    gw_dn = wgrad_down(act, after(dx2b, gw_up))
    sib_dn = to_sibling([after(gw_dn, sib_up)], "w_down")
    dx1, part_up = tie((dx1, partials([gw_up], sib_up, ["w_up"])))
    chips_up = to_chips(part_up, "w_up")
    dz, merged, p2, u, dyp, dyc, p, dpw, g_mix_vec = bwd_mix(
        dx1, zs, y_pool, y_conv, pool_w_f, pool_scale, conv_w_f, wmix, tm_mix, seq)
    merged, part_dn = tie((merged, partials([gw_dn], sib_dn, ["w_down"])))
    chips_dn = to_chips(part_dn, "w_down")
    gw_o = wgrad_square(merged, dx1, "wgrad_o", tk)
    gw_pp = wgrad_square(p2, dyp, "wgrad_pool_proj", tk)
    gw_co = wgrad_square(u, dyc, "wgrad_conv_out", tk)
    gw_pool = wgrad_pool(p, dpw, tk).reshape(NG, 4, 2, 32, CG).transpose(2, 1, 0, 3, 4).reshape(2, 4, NG * 32, CG)
    dz8 = dz.reshape(NDEV, 3, t, CG)
    gw_in, sib_in, chips_in = [None] * 3, [None] * 3, [None] * 3
    sib_a = to_sibling(after([gw_o, gw_pp], (chips_up, gw_pool, chips_dn)), "mix_a")
    sib_b = to_sibling(after([gw_co, gw_pool], sib_a), "mix_b")
    gw_in[0] = wgrad_cols(h1, dz8, 0, "wgrad_in_0")
    h1, part_a, part_b = tie((h1, partials([gw_o, gw_pp], sib_a, ["w_o", "w_pool_proj"]),
                              partials([gw_co, gw_pool], sib_b, ["w_conv_out", "pool_w"])))
    chips_a = to_chips(after(part_a, chips_dn), "mix_a")
    chips_b = to_chips(part_b, "mix_b")
    sib_in[0] = to_sibling(after([gw_in[0]], sib_b), "w_in_0")
    gw_in[1] = wgrad_cols(h1, dz8, 1, "wgrad_in_1")
    h1, part_in0, gw_in[1] = tie((h1, partials([gw_in[0]], sib_in[0], ["w_in_0"]), gw_in[1]))
    chips_in[0] = to_chips(part_in0, "w_in_0")
    sib_in[1] = to_sibling(after([gw_in[1]], sib_in[0]), "w_in_1")
    h1, big["w_down"], big["w_up"] = tie((
        h1, finish("w_down", [gw_dn], sib_dn, chips_dn, (w_down, m_w_down, v_w_down)),
        finish("w_up", [gw_up], sib_up, chips_up, (w_up, m_w_up, v_w_up), transposed=True)))
    gw_in[2] = wgrad_cols(h1, dz8, 2, "wgrad_in_2")
    sib_in[2] = to_sibling(after([gw_in[2]], (chips_a, chips_b, chips_in[0])), "w_in_2")
    sib_in[2], big["w_o"], big["w_pool_proj"], big["w_conv_out"], big["pool_w"] = tie((
        sib_in[2],
        finish("w_o", [gw_o], sib_a[:1], chips_a[:1], (w_o, m_w_o, v_w_o)),
        finish("w_pool_proj", [gw_pp], sib_a[1:], chips_a[1:], (w_pool_proj, m_w_pool_proj, v_w_pool_proj)),
        finish("w_conv_out", [gw_co], sib_b[:1], chips_b[:1], (w_conv_out, m_w_conv_out, v_w_conv_out)),
        finish("pool_w", [gw_pool], sib_b[1:], chips_b[1:], (pool_w, m_pool_w, v_pool_w))))
    dx1, part_in1, part_in2 = tie((
        dx1, partials([gw_in[1]], sib_in[1], ["w_in_1"]), partials([gw_in[2]], sib_in[2], ["w_in_2"])))
    chips_in[1] = to_chips(after(part_in1, sib_in[2]), "w_in_1")
    chips_in[2] = to_chips(part_in2, "w_in_2")
    small_g, = all_gather_blocks(
        [after(jnp.concatenate([g_mix_vec, g_nffn, ffn_vec, g_ffn_vec.reshape(8 * NDEV, D)], axis=0), sib_in[2])],
        "all_gather_small", 0)
    grad_x, g_nmix = bwd_in(dz, w_in_g, dx1, xt, norm_mix, min(TM_BWD_IN, t))
    grad_x, chips_in = tie((grad_x, chips_in))
    big["w_in"] = finish("w_in", gw_in, [s[0] for s in sib_in], [c[0] for c in chips_in], (w_in, m_w_in, v_w_in))

    red_n, red = _exchange_small(g_nmix, True, "all_reduce_small", gathered=small_g)
    g_norm_mix, g_pool_scale, g_norm_ffn = red_n[0:1], red[0:1], red[8:9]
    g_conv_w = lax.dynamic_slice(red, (1, me * 128), (3, 128))
    g_norm_final = red[16]
    loss = red[17, 0]
    g_fcb = red[24:].reshape(NDEV, 8, D)[:, 0, :SH_UP].reshape(1, FF2)
    g_fcw = lax.dynamic_slice(red, (25 + 8 * me, 0), (3, SH_UP))
    grads = {"norm_mix": g_norm_mix, "pool_scale": g_pool_scale, "norm_ffn": g_norm_ffn, "norm_final": g_norm_final,
             "ffn_conv_b": g_fcb, "conv_w": g_conv_w.reshape(1, 3, 128), "ffn_conv_w": g_fcw.reshape(1, 3, SH_UP)}
    small_wmv = {"norm_mix": (norm_mix, m_norm_mix, v_norm_mix), "pool_scale": (pool_scale, m_pool_scale, v_pool_scale),
                 "norm_ffn": (norm_ffn, m_norm_ffn, v_norm_ffn), "norm_final": (norm_final, m_norm_final, v_norm_final),
                 "ffn_conv_b": (ffn_conv_b, m_ffn_conv_b, v_ffn_conv_b), "conv_w": (conv_w, m_conv_w, v_conv_w),
                 "ffn_conv_w": (ffn_conv_w, m_ffn_conv_w, v_ffn_conv_w)}
    small_names = list(small_wmv)
    flat2 = lambda a: a.reshape(1, -1) if a.ndim == 1 else (a.transpose(1, 0, 2) if a.ndim == 3 else a)
    unflat = lambda o, like: o.transpose(1, 0, 2) if like.ndim == 3 else o.reshape(like.shape)
    small_out = adamw_small([(flat2(small_wmv[nm][0]), flat2(grads[nm]), flat2(small_wmv[nm][1]),
                              flat2(small_wmv[nm][2])) for nm in small_names])
    small = {nm: [unflat(o, small_wmv[nm][0]) for o in outs] for nm, outs in zip(small_names, small_out)}

    order = ["norm_mix", "w_in", "pool_w", "pool_scale", "w_pool_proj", "conv_w", "w_conv_out", "w_o", "norm_ffn",
             "w_up", "ffn_conv_w", "ffn_conv_b", "w_down", "norm_final"]
    out = [loss, grad_x.reshape(nb, seq, D)]
    out += [big[nm][0] if nm in big else grads[nm] for nm in order]
    for idx in range(3):
        out += [big[nm][idx + 1] if nm in big else small[nm][idx] for nm in order]
    return tuple(out)
```

```python
import jax
import jax.numpy as jnp
from jax import lax
from jax.experimental import pallas as pl
from jax.experimental.pallas import tpu as pltpu
from jax.experimental.pallas import tpu_sc as plsc

F32 = jnp.float32
BF16 = jnp.bfloat16

NDEV = 8
D = 1024
NG = 4
CG = 256
WINS = (2, 4, 8, 16)
DIN = 6 * D
SH_IN = DIN // NDEV
NZT = DIN // CG
FF2 = 5632
SH_UP = FF2 // NDEV
FF = FF2 // 2
NCH = 4
SH_DN = FF // NDEV
RMS_EPS = 1e-6
HALO = 16

ADAM_LR = 0.001
ADAM_B1 = 0.9
ADAM_B2 = 0.999
ADAM_EPS = 1e-08
ADAM_WD = 0.01
ADAM_STEP = 10

TM_IN = 512
TM_BWD_IN = 256
TM_MIX = 256
TM_FFN = 256
TK_WGRAD = 1024
MIX_POOL_PROJ, MIX_CONV_OUT, MIX_O = 0, 1, 2
MIX_COLS = 3 * D + CG
VMEM_BIG = 56 * 1024 * 1024
MESH = pl.DeviceIdType.MESH
ANY = pl.BlockSpec(memory_space=pl.ANY)


def _cparams(n_axes, vmem=None):
    return pltpu.CompilerParams(dimension_semantics=("arbitrary",) * n_axes, vmem_limit_bytes=vmem)


def _dot(a, b):
    return jnp.dot(a, b, preferred_element_type=F32)


def _dot_nt(a, b):
    return lax.dot_general(a, b, (((1,), (1,)), ((), ())), preferred_element_type=F32)


def _dot_tn(a, b):
    return lax.dot_general(a, b, (((0,), (0,)), ((), ())), preferred_element_type=F32)


def _shift_down(ext, s, lead):
    return pltpu.roll(ext, s, 0)[lead:]


def _shift_up(ext, s, tm):
    n = ext.shape[0]
    return pltpu.roll(ext, n - s, 0)[:tm]


def _rms_inv(x):
    return lax.rsqrt(jnp.mean(x * x, axis=-1, keepdims=True) + RMS_EPS)


def _rms_bwd(dh, xn, inv, g):
    dxn = dh * g
    return inv * (dxn - xn * jnp.mean(dxn * xn, axis=-1, keepdims=True))


def _pos():
    return lax.axis_index("x"), lax.axis_index("y"), lax.axis_index("c")


def _handshake(peers):
    barrier = pltpu.get_barrier_semaphore()
    for peer in peers:
        pl.semaphore_signal(barrier, inc=1, device_id=peer, device_id_type=MESH)
    pl.semaphore_wait(barrier, len(peers))


def _sequencer(body, out_type, n_sems, name, collective_id):
    return pl.kernel(
        body, out_type=out_type, mesh=plsc.ScalarSubcoreMesh(axis_name="sequencer", num_cores=1), name=name,
        scratch_types=[pltpu.SemaphoreType.DMA((n_sems,)), pltpu.SemaphoreType.DMA((n_sems,))],
        compiler_params=pltpu.CompilerParams(collective_id=collective_id))


def all_gather_blocks(shards, name, collective_id):
    n = len(shards)

    def body(*refs):
        ins, outs = refs[:n], refs[n:2 * n]
        send_sems, recv_sems = refs[2 * n:]
        x, y, c = _pos()
        me, sibling = (x, y, c), (x, y, 1 - c)
        first_chip, second_chip, diagonal = (x ^ (1 - c), y ^ c), (x ^ c, y ^ (1 - c)), (1 - x, 1 - y)
        first, second = (*first_chip, c), (*second_chip, c)
        _handshake([sibling, first, second])

        def copy(w, k, block, to, src=None):
            slot = outs[w].at[4 * block[0] + 2 * block[1] + block[2]]
            return pltpu.make_async_remote_copy(
                src_ref=slot if src is None else src, dst_ref=slot,
                send_sem=send_sems.at[8 * w + k], recv_sem=recv_sems.at[8 * w + k], device_id=to, device_id_type=MESH)

        mine, sent = [], []
        for w in range(n):
            m = pltpu.make_async_copy(ins[w], outs[w].at[4 * x + 2 * y + c], send_sems.at[8 * w + 7])
            m.start()
            mine.append(m)
            sent += [copy(w, k, me, to, src=ins[w]) for k, to in enumerate((sibling, first, second))]
        for cp in sent:
            cp.start()
        for k, chip in ((1, first_chip), (2, second_chip), (3, diagonal)):
            for w in range(n):
                copy(w, k, (*chip, c), me).wait_recv()
                onward = [copy(w, 3 + k, (*chip, c), sibling)] + ([copy(w, 3, (*chip, c), second)] if k == 1 else [])
                for cp in onward:
                    cp.start()
                sent += onward
        for w in range(n):
            copy(w, 0, sibling, me).wait_recv()
            for k, chip in ((4, second_chip), (5, first_chip), (6, diagonal)):
                copy(w, k, (*chip, 1 - c), me).wait_recv()
        for cp in sent:
            cp.wait_send()
        for m in mine:
            m.wait()

    out = [jax.ShapeDtypeStruct((NDEV,) + s.shape, s.dtype) for s in shards]
    return _sequencer(body, out, 8 * n, name, collective_id)(*shards)


def _exchange_small(v, name):
    rows = v.shape[0]

    def body(v_ref, out_ref, slots, send_sems, recv_sems, local_sem):
        x, y, c = _pos()
        me = 4 * x + 2 * y + c
        mine = pltpu.make_async_copy(v_ref, slots.at[me], local_sem)
        mine.start()
        offs = [(dx, dy, dc) for dx in (0, 1) for dy in (0, 1) for dc in (0, 1)][1:]

        def copy(k, src_slot, to):
            return pltpu.make_async_remote_copy(
                src_ref=v_ref, dst_ref=slots.at[src_slot], send_sem=send_sems.at[k], recv_sem=recv_sems.at[k],
                device_id=to, device_id_type=MESH)

        sends = []
        for k, (dx, dy, dc) in enumerate(offs):
            cp = copy(k, me, (x ^ dx, y ^ dy, c ^ dc))
            cp.start()
            sends.append(cp)
        for k, (dx, dy, dc) in enumerate(offs):
            copy(k, 4 * (x ^ dx) + 2 * (y ^ dy) + (c ^ dc), (x, y, c)).wait_recv()
        for cp in sends:
            cp.wait_send()
        mine.wait()
        out_ref[...] = slots[...]

    vmem = pl.BlockSpec(memory_space=pltpu.VMEM)
    return pl.pallas_call(
        body, name=name, out_shape=jax.ShapeDtypeStruct((NDEV, rows, D), F32), in_specs=[vmem], out_specs=vmem,
        scratch_shapes=[pltpu.VMEM((NDEV, rows, D), F32), pltpu.SemaphoreType.DMA((7,)),
                        pltpu.SemaphoreType.DMA((7,)), pltpu.SemaphoreType.DMA],
    )(v)


def sum_blocks(gathered, name):
    def body(*refs):
        for g_ref, o_ref in zip(refs[:len(gathered)], refs[len(gathered):]):
            acc = g_ref[0]
            for d in range(1, NDEV):
                acc = acc + g_ref[d]
            o_ref[...] = acc

    vmem = pl.BlockSpec(memory_space=pltpu.VMEM)
    return pl.pallas_call(
        body, name=name, out_shape=[jax.ShapeDtypeStruct(g.shape[1:], F32) for g in gathered],
        in_specs=[vmem] * len(gathered), out_specs=[vmem] * len(gathered),
    )(*gathered)


def reduce_scatter_d2d(grads, name, collective_id):
    n = len(grads)

    def body(*refs):
        ins, outs = refs[:n], refs[n:2 * n]
        send_sems, recv_sems = refs[2 * n:]
        x, y, c = _pos()
        _handshake([(x, y, 1 - c)])
        cps = []
        for w in range(n):
            cp = pltpu.make_async_remote_copy(
                src_ref=ins[w].at[1 - c], dst_ref=outs[w], send_sem=send_sems.at[w], recv_sem=recv_sems.at[w],
                device_id=(x, y, 1 - c), device_id_type=MESH)
            cp.start()
            cps.append(cp)
        for cp in cps:
            cp.wait_recv()
        for cp in cps:
            cp.wait_send()

    out = [jax.ShapeDtypeStruct(g.shape[1:], F32) for g in grads]
    return _sequencer(body, out, n, name, collective_id)(*grads)


def reduce_scatter_ici(parts, name, collective_id):
    n = len(parts)

    def body(*refs):
        ins, outs = refs[:n], refs[n:2 * n]
        send_sems, recv_sems = refs[2 * n:]
        x, y, c = _pos()
        offs = [(1, 0), (0, 1), (1, 1)]
        _handshake([(x ^ dx, y ^ dy, c) for dx, dy in offs])
        cps = []
        for w in range(n):
            for k, (dx, dy) in enumerate(offs):
                ox, oy = x ^ dx, y ^ dy
                cp = pltpu.make_async_remote_copy(
                    src_ref=ins[w].at[2 * ox + oy], dst_ref=outs[w].at[2 * x + y],
                    send_sem=send_sems.at[3 * w + k], recv_sem=recv_sems.at[3 * w + k],
                    device_id=(ox, oy, c), device_id_type=MESH)
                cp.start()
                cps.append((cp, w, k, ox, oy))
        for cp, w, k, ox, oy in cps:
            pltpu.make_async_remote_copy(
                src_ref=ins[w].at[2 * ox + oy], dst_ref=outs[w].at[2 * ox + oy],
                send_sem=send_sems.at[3 * w + k], recv_sem=recv_sems.at[3 * w + k],
                device_id=(ox, oy, c), device_id_type=MESH).wait_recv()
        for cp, *_ in cps:
            cp.wait_send()

    out = [jax.ShapeDtypeStruct(p.shape, BF16) for p in parts]
    return _sequencer(body, out, 3 * n, name, collective_id)(*parts)


def fwd_in(x, g1, w_in_pieces, tm):
    t = x.shape[0]
    tile = lambda i: (i, 0)
    w_spec = pl.BlockSpec((NDEV, D, CG), lambda i: (0, 0, 0))
    z_spec = pl.BlockSpec((NDEV, tm, CG), lambda i: (0, i, 0))
    z_shape = jax.ShapeDtypeStruct((NDEV, t, CG), BF16)

    def cost(other_bytes, transcendentals):
        return pl.CostEstimate(flops=2 * t * D * NDEV * CG, transcendentals=transcendentals,
                               bytes_accessed=other_bytes + 2 * D * NDEV * CG + 2 * t * NDEV * CG)

    def first(x_ref, g_ref, w_ref, z_ref, h_ref, ht_ref):
        xf = x_ref[...]
        h = (xf * _rms_inv(xf) * g_ref[...]).astype(BF16)
        h_ref[...] = h
        ht_ref[...] = h.T
        for j in range(NDEV):
            z_ref[j] = _dot(h, w_ref[j]).astype(BF16)

    z0, h, ht = pl.pallas_call(
        first, name="fwd_in_0", grid=(t // tm,),
        in_specs=[pl.BlockSpec((tm, D), tile), pl.BlockSpec((1, D), lambda i: (0, 0)), w_spec],
        out_specs=[z_spec, pl.BlockSpec((tm, D), tile), pl.BlockSpec((D, tm), lambda i: (0, i))],
        out_shape=[z_shape, jax.ShapeDtypeStruct((t, D), BF16), jax.ShapeDtypeStruct((D, t), BF16)],
        compiler_params=_cparams(1, VMEM_BIG), cost_estimate=cost(8 * t * D + 4 * D, t),
    )(x, g1, w_in_pieces[0])
    zs = [z0]
    for q in (1, 2):
        h, zs[-1] = lax.optimization_barrier((h, zs[-1]))

        def later(h_ref, w_ref, z_ref):
            hb = h_ref[...]
            for j in range(NDEV):
                z_ref[j] = _dot(hb, w_ref[j]).astype(BF16)

        zs.append(pl.pallas_call(
            later, name=f"fwd_in_{q}", grid=(t // tm,),
            in_specs=[pl.BlockSpec((tm, D), tile), w_spec], out_specs=z_spec, out_shape=z_shape,
            compiler_params=_cparams(1, VMEM_BIG), cost_estimate=cost(2 * t * D, 0),
        )(h, w_in_pieces[q]))
    return zs, ht


def _z_tile(z_refs, n):
    return z_refs[n % 3][n // 3]


def _pool_tile(z, zh, win, keep_hist, cnt):
    zt = z.astype(F32)
    ext = jnp.concatenate([zh.astype(F32) * keep_hist, zt], axis=0)
    s, sh = ext, 1
    while sh < win:
        s = s + pltpu.roll(s, sh, 0)
        sh *= 2
    return s[HALO:] / cnt - zt


def _conv_taps(ext, cur, w_ref, lanes, lead):
    x1 = _shift_down(ext, 1, lead)
    x2 = _shift_down(ext, 2, lead)
    out = w_ref[2:3, lanes] * cur + w_ref[1:2, lanes] * x1 + w_ref[0:1, lanes] * x2
    return out, x1, x2


def fwd_mix(zs, x, pool_w, pool_scale, conv_w, wmix, tm, seq):
    t = x.shape[0]
    tps = seq // tm
    hb = tm // HALO

    def body(z0_ref, z1_ref, z2_ref, zh0_ref, zh1_ref, zh2_ref, x_ref, pw_ref, ps_ref, wpp_ref, cw_ref, wco_ref,
             wo_ref, x1_ref, yp_ref, yc_ref):
        z_refs, zh_refs = (z0_ref, z1_ref, z2_ref), (zh0_ref, zh1_ref, zh2_ref)
        it = pl.program_id(0) % tps
        keep_hist = jnp.where(it == 0, 0.0, 1.0)
        pos = it * tm + lax.broadcasted_iota(jnp.int32, (tm, 1), 0)
        p2 = []
        for g, win in enumerate(WINS):
            cnt = jnp.minimum(pos + 1, win).astype(F32)
            p = _pool_tile(_z_tile(z_refs, g), _z_tile(zh_refs, g), win, keep_hist, cnt)
            lanes = slice(g * CG, (g + 1) * CG)
            p2.append((_dot(p.astype(BF16), pw_ref[g]) * ps_ref[:, lanes]).astype(BF16))
        y_pool = _dot(jnp.concatenate(p2, axis=1), wpp_ref[...])
        u = []
        for q in range(NG):
            lanes = slice(q * CG, (q + 1) * CG)
            cv = _z_tile(z_refs, 8 + q).astype(F32) * _z_tile(z_refs, 12 + q).astype(F32)
            cvh = _z_tile(zh_refs, 8 + q).astype(F32) * _z_tile(zh_refs, 12 + q).astype(F32) * keep_hist
            cc, _, _ = _conv_taps(jnp.concatenate([cvh, cv], axis=0), cv, cw_ref, lanes, HALO)
            u.append((_z_tile(z_refs, 4 + q).astype(F32) * cc).astype(BF16))
        y_conv = _dot(jnp.concatenate(u, axis=1), wco_ref[...])
        ypb, ycb = y_pool.astype(BF16), y_conv.astype(BF16)
        yp_ref[...] = ypb
        yc_ref[...] = ycb
        merged = []
        for q in range(NG):
            lanes = slice(q * CG, (q + 1) * CG)
            sp = jax.nn.sigmoid(_z_tile(z_refs, 16 + q).astype(F32))
            sc = jax.nn.sigmoid(_z_tile(z_refs, 20 + q).astype(F32))
            merged.append((sp * ypb[:, lanes].astype(F32) + sc * ycb[:, lanes].astype(F32)).astype(BF16))
        x1_ref[...] = x_ref[...] + _dot(jnp.concatenate(merged, axis=1), wo_ref[...])

    def hist(i):
        return jnp.maximum(i * hb - 1, 0)

    const2 = lambda i: (0, 0)
    return pl.pallas_call(
        body, name="fwd_mix", grid=(t // tm,),
        in_specs=[pl.BlockSpec((NDEV, tm, CG), lambda i: (0, i, 0))] * 3
                 + [pl.BlockSpec((NDEV, HALO, CG), lambda i: (0, hist(i), 0))] * 3
                 + [pl.BlockSpec((tm, D), lambda i: (i, 0)),
                    pl.BlockSpec((NG, CG, CG), lambda i: (0, 0, 0)), pl.BlockSpec((1, D), const2),
                    pl.BlockSpec((D, D), lambda i: (0, MIX_POOL_PROJ)), pl.BlockSpec((3, D), const2),
                    pl.BlockSpec((D, D), lambda i: (0, MIX_CONV_OUT)), pl.BlockSpec((D, D), lambda i: (0, MIX_O))],
        out_specs=[pl.BlockSpec((tm, D), lambda i: (i, 0))] * 3,
        out_shape=[jax.ShapeDtypeStruct((t, D), F32), jax.ShapeDtypeStruct((t, D), BF16),
                   jax.ShapeDtypeStruct((t, D), BF16)],
        compiler_params=_cparams(1, VMEM_BIG),
    )(*zs, *zs, x, pool_w, pool_scale, wmix, conv_w, wmix, wmix)


def fwd_up(x1, g2, w_up_g, fcw, fcb, tm, seq):
    t = x1.shape[0]
    tps = seq // tm

    def body(x1_ref, g2_ref, wup_ref, fcw_ref, fcb_ref, up_ref, pre_ref, act_ref, actt_ref, h2t_ref, hist_ref):
        i = pl.program_id(0)
        keep_hist = jnp.where(i % tps == 0, 0.0, 1.0)

        @pl.when(i == 0)
        def _():
            hist_ref[...] = jnp.zeros_like(hist_ref)

        x1v = x1_ref[...]
        h2 = (x1v * _rms_inv(x1v) * g2_ref[...]).astype(BF16)
        h2t_ref[...] = h2.T
        lanes = slice(0, SH_UP)
        for c in range(NCH):
            conv = []
            for s in range(2):
                ub = _dot(h2, wup_ref[s, c]).astype(BF16)
                up_ref[s, c] = ub
                uf = ub.astype(F32)
                ext = jnp.concatenate([hist_ref[s, c] * keep_hist, uf], axis=0)
                hist_ref[s, c] = uf[tm - 8:]
                cc, _, _ = _conv_taps(ext, uf, fcw_ref.at[s, c], lanes, 8)
                conv.append(cc + fcb_ref[s, c])
                pre_ref[s, c] = conv[s].astype(BF16)
            a = (conv[0] * jax.nn.sigmoid(conv[0]) * conv[1]).astype(BF16)
            act_ref[c] = a
            actt_ref[c] = a.T

    tile = lambda i: (i, 0)
    const2 = lambda i: (0, 0)
    whole = lambda i: (0, 0, 0, 0)
    chunks = pl.BlockSpec((2, NCH, tm, SH_UP), lambda i: (0, 0, i, 0))
    return pl.pallas_call(
        body, name="fwd_up", grid=(t // tm,),
        in_specs=[pl.BlockSpec((tm, D), tile), pl.BlockSpec((1, D), const2),
                  pl.BlockSpec((2, NCH, D, SH_UP), whole), pl.BlockSpec((2, NCH, 3, SH_UP), whole),
                  pl.BlockSpec((2, NCH, 1, SH_UP), whole)],
        out_specs=[chunks, chunks, pl.BlockSpec((NCH, tm, SH_UP), lambda i: (0, i, 0)),
                   pl.BlockSpec((NCH, SH_UP, tm), lambda i: (0, 0, i)), pl.BlockSpec((D, tm), lambda i: (0, i))],
        out_shape=[jax.ShapeDtypeStruct((2, NCH, t, SH_UP), BF16), jax.ShapeDtypeStruct((2, NCH, t, SH_UP), BF16),
                   jax.ShapeDtypeStruct((NCH, t, SH_UP), BF16), jax.ShapeDtypeStruct((NCH, SH_UP, t), BF16),
                   jax.ShapeDtypeStruct((D, t), BF16)],
        scratch_shapes=[pltpu.VMEM((2, NCH, 8, SH_UP), F32)],
        compiler_params=_cparams(1, VMEM_BIG),
    )(x1, g2, w_up_g.reshape(2, NCH, D, SH_UP), fcw.reshape(2, NCH, 3, SH_UP), fcb.reshape(2, NCH, 1, SH_UP))


def fwd_down(x1, act, w_dn, gf, tgt, tm):
    t = x1.shape[0]

    def body(x1_ref, act_ref, wdn_ref, gf_ref, tgt_ref, dx2_ref, dx2b_ref, vec_ref):
        @pl.when(pl.program_id(0) == 0)
        def _():
            vec_ref[...] = jnp.zeros_like(vec_ref)

        d = None
        for c in range(NCH):
            part = _dot(act_ref[c], wdn_ref[c])
            d = part if d is None else d + part
        x2 = x1_ref[...] + d
        inv3 = _rms_inv(x2)
        xn = x2 * inv3
        diff = xn * gf_ref[...] - tgt_ref[...]
        dy = diff * (1.0 / D)
        vec_ref[0:1, :] += jnp.sum(dy * xn, axis=0, keepdims=True)
        vec_ref[1:2, :] += 0.5 * jnp.sum(jnp.mean(diff * diff, axis=-1))
        dx2 = _rms_bwd(dy, xn, inv3, gf_ref[...])
        dx2_ref[...] = dx2
        dx2b_ref[...] = dx2.astype(BF16)

    tile = lambda i: (i, 0)
    const2 = lambda i: (0, 0)
    return pl.pallas_call(
        body, name="fwd_down", grid=(t // tm,),
        in_specs=[pl.BlockSpec((tm, D), tile), pl.BlockSpec((NCH, tm, SH_UP), lambda i: (0, i, 0)),
                  pl.BlockSpec((NCH, SH_UP, D), lambda i: (0, 0, 0)), pl.BlockSpec((1, D), const2),
                  pl.BlockSpec((tm, D), tile)],
        out_specs=[pl.BlockSpec((tm, D), tile), pl.BlockSpec((tm, D), tile), pl.BlockSpec((8, D), const2)],
        out_shape=[jax.ShapeDtypeStruct((t, D), F32), jax.ShapeDtypeStruct((t, D), BF16),
                   jax.ShapeDtypeStruct((8, D), F32)],
        compiler_params=_cparams(1, VMEM_BIG),
    )(x1, act, w_dn, gf, tgt)


def bwd_ffn(dx2, x1, g2, up, pre, w_up_g, fcw, w_dn, tm, seq):
    t = x1.shape[0]
    nt = t // tm
    tps = seq // tm

    def body(dx2_ref, x1_ref, g2_ref, up_ref, pre_ref, wup_ref, fcw_ref, wdn_ref,
             dup_ref, dx1_ref, gvec_ref, gn_ref, carry_ref):
        i = pl.program_id(0)
        it = (nt - 1 - i) % tps
        keep_next = jnp.where(it == tps - 1, 0.0, 1.0)

        @pl.when(i == 0)
        def _():
            gvec_ref[...] = jnp.zeros_like(gvec_ref)
            gn_ref[...] = jnp.zeros_like(gn_ref)
            carry_ref[...] = jnp.zeros_like(carry_ref)

        dx2v = dx2_ref[...]
        dxb = dx2v.astype(BF16)
        lanes = slice(0, SH_UP)
        dh2 = None
        for c in range(NCH):
            pre = [pre_ref[s, c].astype(F32) for s in range(2)]
            sg = jax.nn.sigmoid(pre[0])
            dact = _dot_nt(dxb, wdn_ref[c])
            dpre = [dact * pre[1] * (sg * (1.0 + pre[0] * (1.0 - sg))), dact * (pre[0] * sg)]
            for s in range(2):
                dc = dpre[s]
                ext = jnp.concatenate([dc, carry_ref[s, c] * keep_next], axis=0)
                carry_ref[s, c] = dc[:8]
                shifted = (_shift_up(ext, 2, tm), _shift_up(ext, 1, tm), dc)
                uf = up_ref[s, c].astype(F32)
                gvec_ref[s, c, 0:1, lanes] += jnp.sum(dc, axis=0, keepdims=True)
                for tap in range(3):
                    gvec_ref[s, c, tap + 1:tap + 2, lanes] += jnp.sum(shifted[tap] * uf, axis=0, keepdims=True)
                w = fcw_ref.at[s, c]
                du = w[2:3, :] * dc + w[1:2, :] * shifted[1] + w[0:1, :] * shifted[0]
                dub = du.astype(BF16)
                dup_ref[s, c] = dub
                part = _dot_nt(dub, wup_ref[s, c])
                dh2 = part if dh2 is None else dh2 + part
        x1v = x1_ref[...]
        inv2 = _rms_inv(x1v)
        xn = x1v * inv2
        gn_ref[0:1, :] += jnp.sum(dh2 * xn, axis=0, keepdims=True)
        dx1_ref[...] = dx2v + _rms_bwd(dh2, xn, inv2, g2_ref[...])

    rev = lambda i: (nt - 1 - i, 0)
    const2 = lambda i: (0, 0)
    whole = lambda i: (0, 0, 0, 0)
    chunks = pl.BlockSpec((2, NCH, tm, SH_UP), lambda i: (0, 0, nt - 1 - i, 0))
    return pl.pallas_call(
        body, name="bwd_ffn", grid=(nt,),
        in_specs=[pl.BlockSpec((tm, D), rev), pl.BlockSpec((tm, D), rev), pl.BlockSpec((1, D), const2),
                  chunks, chunks, pl.BlockSpec((2, NCH, D, SH_UP), whole), pl.BlockSpec((2, NCH, 3, SH_UP), whole),
                  pl.BlockSpec((NCH, SH_UP, D), lambda i: (0, 0, 0))],
        out_specs=[chunks, pl.BlockSpec((tm, D), rev), pl.BlockSpec((2, NCH, 8, D), whole),
                   pl.BlockSpec((8, D), const2)],
        out_shape=[jax.ShapeDtypeStruct((2, NCH, t, SH_UP), BF16), jax.ShapeDtypeStruct((t, D), F32),
                   jax.ShapeDtypeStruct((2, NCH, 8, D), F32), jax.ShapeDtypeStruct((8, D), F32)],
        scratch_shapes=[pltpu.VMEM((2, NCH, 8, SH_UP), F32)],
        compiler_params=_cparams(1, VMEM_BIG),
    )(dx2, x1, g2, up, pre, w_up_g.reshape(2, NCH, D, SH_UP), fcw.reshape(2, NCH, 3, SH_UP), w_dn)


def bwd_mix(dx1, zs, y_pool, y_conv, pool_w, pool_scale, conv_w, wmix, tm, seq):
    t = dx1.shape[0]
    nt = t // tm
    tps = seq // tm
    hb = tm // HALO

    def body(da_ref, z0_ref, z1_ref, z2_ref, zh0_ref, zh1_ref, zh2_ref, yp_ref, yc_ref, pw_ref, ps_ref, wpp_ref,
             cw_ref, wco_ref, wo_ref,
             dz_ref, mg_ref, p2_ref, u_ref, dyp_ref, dyc_ref, p_ref, dpw_ref, gvec_ref, cp_ref, cc_ref):
        z_refs, zh_refs = (z0_ref, z1_ref, z2_ref), (zh0_ref, zh1_ref, zh2_ref)
        i = pl.program_id(0)
        it = (nt - 1 - i) % tps
        keep_hist = jnp.where(it == 0, 0.0, 1.0)
        keep_next = jnp.where(it == tps - 1, 0.0, 1.0)
        pos = it * tm + lax.broadcasted_iota(jnp.int32, (tm, 1), 0)

        @pl.when(i == 0)
        def _():
            gvec_ref[...] = jnp.zeros_like(gvec_ref)
            cp_ref[...] = jnp.zeros_like(cp_ref)
            cc_ref[...] = jnp.zeros_like(cc_ref)

        dm = _dot_nt(da_ref[...].astype(BF16), wo_ref[...])
        merged, dyp, dyc = [], [], []
        for q in range(NG):
            lanes = slice(q * CG, (q + 1) * CG)
            sp = jax.nn.sigmoid(_z_tile(z_refs, 16 + q).astype(F32))
            sc = jax.nn.sigmoid(_z_tile(z_refs, 20 + q).astype(F32))
            yp = yp_ref[:, lanes].astype(F32)
            yc = yc_ref[:, lanes].astype(F32)
            dmq = dm[:, lanes]
            merged.append((sp * yp + sc * yc).astype(BF16))
            dyp.append((dmq * sp).astype(BF16))
            dyc.append((dmq * sc).astype(BF16))
            dz_ref[16 + q] = (dmq * yp * (sp * (1.0 - sp))).astype(BF16)
            dz_ref[20 + q] = (dmq * yc * (sc * (1.0 - sc))).astype(BF16)
        mg_ref[...] = jnp.concatenate(merged, axis=1)
        dypb = jnp.concatenate(dyp, axis=1)
        dycb = jnp.concatenate(dyc, axis=1)
        dyp_ref[...] = dypb
        dyc_ref[...] = dycb

        dp2 = _dot_nt(dypb, wpp_ref[...])
        p2 = []
        for g, win in enumerate(WINS):
            lanes = slice(g * CG, (g + 1) * CG)
            cnt = jnp.minimum(pos + 1, win).astype(F32)
            p = _pool_tile(_z_tile(z_refs, g), _z_tile(zh_refs, g), win, keep_hist, cnt)
            pb = p.astype(BF16)
            p_ref[g] = pb
            pw = _dot(pb, pw_ref[g])
            p2.append((pw * ps_ref[:, lanes]).astype(BF16))
            dp2g = dp2[:, lanes]
            gvec_ref[0:1, lanes] += jnp.sum(dp2g * pw, axis=0, keepdims=True)
            dpwb = (dp2g * ps_ref[:, lanes]).astype(BF16)
            dpw_ref[g] = dpwb
            dp = _dot_nt(dpwb, pw_ref[g])
            qv = dp / cnt
            ext = jnp.concatenate([qv, cp_ref[g] * keep_next], axis=0)
            cp_ref[g] = qv[:HALO]
            n = tm + HALO
            s, sh = ext, 1
            while sh < win:
                s = s + pltpu.roll(s, n - sh, 0)
                sh *= 2
            dz_ref[g] = (s[:tm] - dp).astype(BF16)
        p2_ref[...] = jnp.concatenate(p2, axis=1)

        du = _dot_nt(dycb, wco_ref[...])
        u = []
        for q in range(NG):
            lanes = slice(q * CG, (q + 1) * CG)
            zb = _z_tile(z_refs, 4 + q).astype(F32)
            zc = _z_tile(z_refs, 8 + q).astype(F32)
            zv = _z_tile(z_refs, 12 + q).astype(F32)
            cv = zc * zv
            cvh = _z_tile(zh_refs, 8 + q).astype(F32) * _z_tile(zh_refs, 12 + q).astype(F32) * keep_hist
            cc, cv1, cv2 = _conv_taps(jnp.concatenate([cvh, cv], axis=0), cv, cw_ref, lanes, HALO)
            u.append((zb * cc).astype(BF16))
            duq = du[:, lanes]
            dz_ref[4 + q] = (duq * cc).astype(BF16)
            dcc = duq * zb
            for tap, src in enumerate((cv2, cv1, cv)):
                gvec_ref[tap + 1:tap + 2, lanes] += jnp.sum(dcc * src, axis=0, keepdims=True)
            ext = jnp.concatenate([dcc, cc_ref[:, lanes] * keep_next], axis=0)
            cc_ref[:, lanes] = dcc[:8]
            dcv = (cw_ref[2:3, lanes] * dcc + cw_ref[1:2, lanes] * _shift_up(ext, 1, tm)
                   + cw_ref[0:1, lanes] * _shift_up(ext, 2, tm))
            dz_ref[8 + q] = (dcv * zv).astype(BF16)
            dz_ref[12 + q] = (dcv * zc).astype(BF16)
        u_ref[...] = jnp.concatenate(u, axis=1)

    def hist(i):
        return jnp.maximum((nt - 1 - i) * hb - 1, 0)

    rev = lambda i: (nt - 1 - i, 0)
    rev3 = lambda i: (0, nt - 1 - i, 0)
    const2 = lambda i: (0, 0)
    tok = jax.ShapeDtypeStruct((t, D), BF16)
    grp = jax.ShapeDtypeStruct((NG, t, CG), BF16)
    return pl.pallas_call(
        body, name="bwd_mix", grid=(nt,),
        in_specs=[pl.BlockSpec((tm, D), rev)] + [pl.BlockSpec((NDEV, tm, CG), rev3)] * 3
                 + [pl.BlockSpec((NDEV, HALO, CG), lambda i: (0, hist(i), 0))] * 3
                 + [pl.BlockSpec((tm, D), rev), pl.BlockSpec((tm, D), rev),
                    pl.BlockSpec((NG, CG, CG), lambda i: (0, 0, 0)), pl.BlockSpec((1, D), const2),
                    pl.BlockSpec((D, D), lambda i: (0, MIX_POOL_PROJ)), pl.BlockSpec((3, D), const2),
                    pl.BlockSpec((D, D), lambda i: (0, MIX_CONV_OUT)), pl.BlockSpec((D, D), lambda i: (0, MIX_O))],
        out_specs=[pl.BlockSpec((NZT, tm, CG), rev3)] + [pl.BlockSpec((tm, D), rev)] * 5
                  + [pl.BlockSpec((NG, tm, CG), rev3)] * 2 + [pl.BlockSpec((8, D), const2)],
        out_shape=[jax.ShapeDtypeStruct((NZT, t, CG), BF16), tok, tok, tok, tok, tok, grp, grp,
                   jax.ShapeDtypeStruct((8, D), F32)],
        scratch_shapes=[pltpu.VMEM((NG, HALO, CG), F32), pltpu.VMEM((8, D), F32)],
        compiler_params=_cparams(1, VMEM_BIG),
    )(dx1, *zs, *zs, y_pool, y_conv, pool_w, pool_scale, wmix, conv_w, wmix, wmix)


def bwd_in(dz, w_in_pieces, dx1, x, g1, tm):
    t = x.shape[0]

    def body(dz_ref, w0_ref, w1_ref, w2_ref, dx1_ref, x_ref, g_ref, gx_ref, gn_ref):
        @pl.when(pl.program_id(0) == 0)
        def _():
            gn_ref[...] = jnp.zeros_like(gn_ref)

        dh = None
        for j in range(NDEV):
            for q, w_ref in enumerate((w0_ref, w1_ref, w2_ref)):
                part = _dot_nt(dz_ref[3 * j + q], w_ref[j])
                dh = part if dh is None else dh + part
        xv = x_ref[...]
        inv = _rms_inv(xv)
        xn = xv * inv
        gn_ref[0:1, :] += jnp.sum(dh * xn, axis=0, keepdims=True)
        gx_ref[...] = dx1_ref[...] + _rms_bwd(dh, xn, inv, g_ref[...])

    tile = lambda i: (i, 0)
    return pl.pallas_call(
        body, name="bwd_in", grid=(t // tm,),
        in_specs=[pl.BlockSpec((NZT, tm, CG), lambda i: (0, i, 0))]
                 + [pl.BlockSpec((NDEV, D, CG), lambda i: (0, 0, 0))] * 3
                 + [pl.BlockSpec((tm, D), tile), pl.BlockSpec((tm, D), tile), pl.BlockSpec((1, D), lambda i: (0, 0))],
        out_specs=[pl.BlockSpec((tm, D), tile), pl.BlockSpec((8, D), lambda i: (0, 0))],
        out_shape=[jax.ShapeDtypeStruct((t, D), F32), jax.ShapeDtypeStruct((8, D), F32)],
        compiler_params=_cparams(1, VMEM_BIG),
    )(dz, *w_in_pieces, dx1, x, g1)


def _slot(j):
    return j % 2, j // 2


def wgrad_cols(at, b, q, name):
    m, t = at.shape
    width = b.shape[3]

    def body(a_ref, b_ref, o_ref):
        o_ref[...] = _dot(a_ref[...], b_ref[...])

    return pl.pallas_call(
        body, name=name, grid=(NDEV,),
        in_specs=[pl.BlockSpec((m, t), lambda j: (0, 0)),
                  pl.BlockSpec((None, None, t, width), lambda j: (j, q, 0, 0))],
        out_specs=pl.BlockSpec((None, None, m, width), lambda j: (j % 2, j // 2, 0, 0)),
        out_shape=jax.ShapeDtypeStruct((2, 4, m, width), F32),
        compiler_params=_cparams(1, VMEM_BIG),
    )(at, b)


def wgrad_down(actt, dx2b):
    t = dx2b.shape[0]

    def body(a_ref, b_ref, o_ref):
        r = _dot(a_ref[...], b_ref[...])
        o_ref[0] = r[:SH_DN]
        o_ref[1] = r[SH_DN:]

    return pl.pallas_call(
        body, name="wgrad_down", grid=(NCH,),
        in_specs=[pl.BlockSpec((None, SH_UP, t), lambda k: (k, 0, 0)), pl.BlockSpec((t, D), lambda k: (0, 0))],
        out_specs=pl.BlockSpec((2, None, SH_DN, D), lambda k: (0, k, 0, 0)),
        out_shape=jax.ShapeDtypeStruct((2, 4, SH_DN, D), F32),
        compiler_params=_cparams(1, VMEM_BIG),
    )(actt, dx2b)


def wgrad_square(a, b, name, tk):
    t = a.shape[0]

    def body(a_ref, b_ref, o_ref, acc_ref):
        kt = pl.program_id(0)

        @pl.when(kt == 0)
        def _():
            acc_ref[...] = jnp.zeros_like(acc_ref)

        acc_ref[...] += _dot_tn(a_ref[...], b_ref[...].astype(BF16))

        @pl.when(kt == pl.num_programs(0) - 1)
        def _():
            for j in range(NDEV):
                cc, xy = _slot(j)
                o_ref[cc, xy] = acc_ref[j * 128:(j + 1) * 128]

    return pl.pallas_call(
        body, name=name, grid=(t // tk,),
        in_specs=[pl.BlockSpec((tk, D), lambda k: (k, 0)), pl.BlockSpec((tk, D), lambda k: (k, 0))],
        out_specs=pl.BlockSpec((2, 4, 128, D), lambda k: (0, 0, 0, 0)),
        out_shape=jax.ShapeDtypeStruct((2, 4, 128, D), F32),
        scratch_shapes=[pltpu.VMEM((D, D), F32)],
        compiler_params=_cparams(1, VMEM_BIG),
    )(a, b)


def wgrad_pool(p, dpw, tk):
    t = p.shape[1]

    def body(a_ref, b_ref, o_ref):
        @pl.when(pl.program_id(0) == 0)
        def _():
            o_ref[...] = jnp.zeros_like(o_ref)

        for g in range(NG):
            o_ref[g] += _dot_tn(a_ref[g], b_ref[g])

    return pl.pallas_call(
        body, name="wgrad_pool", grid=(t // tk,),
        in_specs=[pl.BlockSpec((NG, tk, CG), lambda k: (0, k, 0))] * 2,
        out_specs=pl.BlockSpec((NG, CG, CG), lambda k: (0, 0, 0)),
        out_shape=jax.ShapeDtypeStruct((NG, CG, CG), F32),
        compiler_params=_cparams(1, VMEM_BIG),
    )(p, dpw)


def _adamw(w, g, m, v):
    m = ADAM_B1 * m + (1.0 - ADAM_B1) * g
    v = ADAM_B2 * v + (1.0 - ADAM_B2) * (g * g)
    m_hat = m / (1.0 - ADAM_B1 ** ADAM_STEP)
    v_hat = v / (1.0 - ADAM_B2 ** ADAM_STEP)
    delta = -ADAM_LR * (m_hat / (jnp.sqrt(v_hat) + ADAM_EPS) + ADAM_WD * w)
    return delta, m, v


def _row_block(r):
    return 512 if r % 512 == 0 else r


def chip_partial(place, g, from_sibling, name):
    _, _, r, c = g.shape

    def body(place_ref, g_ref, s_ref, o_ref):
        o_ref[...] = (g_ref[...] + s_ref[...]).astype(BF16)

    return pl.pallas_call(
        body, name=name,
        grid_spec=pltpu.PrefetchScalarGridSpec(
            num_scalar_prefetch=1, grid=(3,),
            in_specs=[pl.BlockSpec((None, None, r, c), lambda k, pr: (pr[0], pr[1] ^ (k + 1), 0, 0)),
                      pl.BlockSpec((None, r, c), lambda k, pr: (pr[1] ^ (k + 1), 0, 0))],
            out_specs=pl.BlockSpec((None, r, c), lambda k, pr: (pr[1] ^ (k + 1), 0, 0))),
        out_shape=jax.ShapeDtypeStruct((4, r, c), BF16),
        compiler_params=_cparams(1, VMEM_BIG),
    )(place, g, from_sibling)


def finish_adamw(place, gs, from_sibling, from_chips, w, m, v, name, transposed=False):
    n = len(gs)
    r = gs[0].shape[2]
    widths = [g.shape[3] for g in gs]
    c = sum(widths)
    br = _row_block(r)

    def body(place_ref, *refs):
        g_refs, s_refs, c_refs = refs[:n], refs[n:2 * n], refs[2 * n:5 * n]
        w_ref, m_ref, v_ref, og_ref, od_ref, om_ref, ov_ref = refs[5 * n:]
        cols = []
        for q in range(n):
            grad = g_refs[q][...] + s_refs[q][...]
            for k in range(3):
                grad = grad + c_refs[3 * q + k][...].astype(F32)
            cols.append(grad)
        grad = cols[0] if n == 1 else jnp.concatenate(cols, axis=1)
        if transposed:
            grad = grad.T
        og_ref[...] = grad
        od_ref[...], om_ref[...], ov_ref[...] = _adamw(w_ref[...], grad, m_ref[...], v_ref[...])

    def other(k, cq):
        return pl.BlockSpec((None, br, cq), lambda i, pr: (pr[1] ^ k, i, 0))

    row = pl.BlockSpec((c, br), lambda i, pr: (0, i)) if transposed else pl.BlockSpec((br, c), lambda i, pr: (i, 0))
    out = jax.ShapeDtypeStruct((c, r) if transposed else (r, c), F32)
    in_specs = [pl.BlockSpec((None, None, br, cq), lambda i, pr: (pr[0], pr[1], i, 0)) for cq in widths]
    in_specs += [pl.BlockSpec((None, br, cq), lambda i, pr: (pr[1], i, 0)) for cq in widths]
    in_specs += [other(k, cq) for cq in widths for k in (1, 2, 3)]
    return pl.pallas_call(
        body, name=name,
        grid_spec=pltpu.PrefetchScalarGridSpec(
            num_scalar_prefetch=1, grid=(r // br,), in_specs=in_specs + [row, row, row], out_specs=[row] * 4),
        out_shape=[out] * 4,
        compiler_params=_cparams(1, VMEM_BIG),
    )(place, *gs, *from_sibling, *[fc for fc in from_chips for _ in range(3)], w, m, v)


def adamw_small(items):
    n = len(items)

    def body(*refs):
        ins, outs = refs[:4 * n], refs[4 * n:]
        for i in range(n):
            w, g, m, v = (r[...] for r in ins[4 * i:4 * i + 4])
            outs[3 * i][...], outs[3 * i + 1][...], outs[3 * i + 2][...] = _adamw(w, g, m, v)

    out = [jax.ShapeDtypeStruct(it[0].shape, F32) for it in items for _ in range(3)]
    res = pl.pallas_call(body, name="adamw_small", out_shape=out)(*[a for it in items for a in it])
    return [res[3 * i:3 * i + 3] for i in range(n)]


def kernel(x, norm_mix, w_in, pool_w, pool_scale, w_pool_proj, conv_w, w_conv_out, w_o, norm_ffn, w_up, ffn_conv_w, ffn_conv_b, w_down, norm_final, loss_target, m_norm_mix, m_w_in, m_pool_w, m_pool_scale, m_w_pool_proj, m_conv_w, m_w_conv_out, m_w_o, m_norm_ffn, m_w_up, m_ffn_conv_w, m_ffn_conv_b, m_w_down, m_norm_final, v_norm_mix, v_w_in, v_pool_w, v_pool_scale, v_w_pool_proj, v_conv_w, v_w_conv_out, v_w_o, v_norm_ffn, v_w_up, v_ffn_conv_w, v_ffn_conv_b, v_w_down, v_norm_final):
    nb, seq, _ = x.shape
    t = nb * seq
    tm_in = min(TM_IN, t)
    tm_mix = min(TM_MIX, seq)
    tm_ffn = min(TM_FFN, seq)
    tk = min(TK_WGRAD, t)
    xt = x.reshape(t, D)
    tgt = loss_target.reshape(t, D)
    xi, yi, ci = _pos()
    me = 4 * xi + 2 * yi + ci
    place = jnp.stack([ci, 2 * xi + yi]).astype(jnp.int32)

    tie = lax.optimization_barrier
    w_in_b = w_in[0].astype(BF16)
    w_in_g = [all_gather_blocks([w_in_b[:, q * CG:(q + 1) * CG]], f"all_gather_w_in_{q}", 0)[0] for q in range(3)]
    taps = (jnp.pad(conv_w[0], ((0, 5), (0, D - 128))) + jnp.pad(ffn_conv_w[0], ((3, 2), (0, D - SH_UP))))
    taps_g = _exchange_small(taps, "all_gather_taps")
    mix_shard = jnp.concatenate(
        [w_pool_proj[0], w_conv_out[0], w_o[0], pool_w[0].reshape(NG * 32, CG)], axis=1).astype(BF16)
    mix_shard, taps_g = tie((mix_shard, taps_g))
    wmix_g, = all_gather_blocks([mix_shard], "all_gather_w_mix", 0)
    ffn_shards, w_in_g[0] = tie(([w_up[0].astype(BF16), w_down[0].astype(BF16)], w_in_g[0]))
    w_up_g, = all_gather_blocks(ffn_shards[:1], "all_gather_w_up", 0)
    w_dn_g, = all_gather_blocks(ffn_shards[1:], "all_gather_w_down", 0)
    w_dn_f = w_dn_g.reshape(NCH, SH_UP, D)
    conv_w_f = taps_g[:, 0:3, :128].transpose(1, 0, 2).reshape(3, D)
    fcw_f = taps_g[:, 3:6, :SH_UP]
    fcb_f = ffn_conv_b.reshape(NDEV, 1, SH_UP)
    gfin = norm_final.reshape(1, D)

    zs, h1 = fwd_in(xt, norm_mix, w_in_g, tm_in)
    wmix_g, zs = tie((wmix_g, zs))
    wmix = wmix_g.reshape(D, MIX_COLS)
    pool_w_f = wmix_g[:, :, 3 * D:].reshape(NDEV, NG, 32, CG).transpose(1, 0, 2, 3).reshape(NG, CG, CG)
    x1, y_pool, y_conv = fwd_mix(zs, xt, pool_w_f, pool_scale, conv_w_f, wmix, tm_mix, seq)
    up, pre, act_tok, act, h2 = fwd_up(x1, norm_ffn, w_up_g, fcw_f, fcb_f, tm_ffn, seq)
    dx2, dx2b, ffn_vec = fwd_down(x1, act_tok, w_dn_f, gfin, tgt, min(TM_IN, t))

    def to_sibling(full, tag):
        return reduce_scatter_d2d(full, "reduce_scatter_d2d_" + tag, 1)

    def partials(full, from_sib, names):
        return [chip_partial(place, g, s, "chip_partial_" + nm) for g, s, nm in zip(full, from_sib, names)]

    def to_chips(parts, tag):
        return reduce_scatter_ici(parts, "reduce_scatter_ici_" + tag, 2)

    def finish(nm, gs, from_sib, from_chips, wmv, transposed=False):
        rc = (gs[0].shape[2], sum(g.shape[3] for g in gs))
        wmv2 = [a.reshape(rc).T if transposed else a.reshape(rc) for a in wmv]
        outs = finish_adamw(place, gs, from_sib, from_chips, *wmv2, "adamw_" + nm, transposed)
        return [(o.T if transposed else o).reshape(wmv[0].shape) for o in outs]

    def after(x, dep):
        return tie((x, dep))[0]

    big = {}
    d_up, dx1, g_ffn_vec, g_nffn = bwd_ffn(dx2, x1, norm_ffn, up, pre, w_up_g, fcw_f, w_dn_f, tm_ffn, seq)
    gw_up = wgrad_cols(h2, d_up.reshape(NDEV, 1, t, SH_UP), 0, "wgrad_up")
    sib_up = to_sibling([gw_up], "w_up")
    gw_dn = wgrad_down(act, after(dx2b, gw_up))
    sib_dn = to_sibling([after(gw_dn, sib_up)], "w_down")
    dx1, part_up = tie((dx1, partials([gw_up], sib_up, ["w_up"])))
    chips_up = to_chips(part_up, "w_up")
    dz, merged, p2, u, dyp, dyc, p, dpw, g_mix_vec = bwd_mix(
        dx1, zs, y_pool, y_conv, pool_w_f, pool_scale, conv_w_f, wmix, tm_mix, seq)
    merged, part_dn = tie((merged, partials([gw_dn], sib_dn, ["w_down"])))
    chips_dn = to_chips(part_dn, "w_down")
    gw_o = wgrad_square(merged, dx1, "wgrad_o", tk)
    gw_pp = wgrad_square(p2, dyp, "wgrad_pool_proj", tk)
    gw_co = wgrad_square(u, dyc, "wgrad_conv_out", tk)
    gw_pool = wgrad_pool(p, dpw, tk).reshape(NG, 4, 2, 32, CG).transpose(2, 1, 0, 3, 4).reshape(2, 4, NG * 32, CG)
    dz8 = dz.reshape(NDEV, 3, t, CG)
    gw_in, sib_in, chips_in = [None] * 3, [None] * 3, [None] * 3
    sib_a = to_sibling(after([gw_o, gw_pp], (chips_up, gw_pool, chips_dn)), "mix_a")
    sib_b = to_sibling(after([gw_co, gw_pool], sib_a), "mix_b")
    gw_in[0] = wgrad_cols(h1, dz8, 0, "wgrad_in_0")
    h1, part_a, part_b = tie((h1, partials([gw_o, gw_pp], sib_a, ["w_o", "w_pool_proj"]),
                              partials([gw_co, gw_pool], sib_b, ["w_conv_out", "pool_w"])))
    chips_a = to_chips(after(part_a, chips_dn), "mix_a")
    chips_b = to_chips(part_b, "mix_b")
    sib_in[0] = to_sibling(after([gw_in[0]], sib_b), "w_in_0")
    gw_in[1] = wgrad_cols(h1, dz8, 1, "wgrad_in_1")
    h1, part_in0, gw_in[1] = tie((h1, partials([gw_in[0]], sib_in[0], ["w_in_0"]), gw_in[1]))
    chips_in[0] = to_chips(part_in0, "w_in_0")
    sib_in[1] = to_sibling(after([gw_in[1]], sib_in[0]), "w_in_1")
    h1, big["w_down"], big["w_up"] = tie((
        h1, finish("w_down", [gw_dn], sib_dn, chips_dn, (w_down, m_w_down, v_w_down)),
        finish("w_up", [gw_up], sib_up, chips_up, (w_up, m_w_up, v_w_up), transposed=True)))
    gw_in[2] = wgrad_cols(h1, dz8, 2, "wgrad_in_2")
    sib_in[2] = to_sibling(after([gw_in[2]], (chips_a, chips_b, chips_in[0])), "w_in_2")
    sib_in[2], big["w_o"], big["w_pool_proj"], big["w_conv_out"], big["pool_w"] = tie((
        sib_in[2],
        finish("w_o", [gw_o], sib_a[:1], chips_a[:1], (w_o, m_w_o, v_w_o)),
        finish("w_pool_proj", [gw_pp], sib_a[1:], chips_a[1:], (w_pool_proj, m_w_pool_proj, v_w_pool_proj)),
        finish("w_conv_out", [gw_co], sib_b[:1], chips_b[:1], (w_conv_out, m_w_conv_out, v_w_conv_out)),
        finish("pool_w", [gw_pool], sib_b[1:], chips_b[1:], (pool_w, m_pool_w, v_pool_w))))
    dx1, part_in1, part_in2 = tie((
        dx1, partials([gw_in[1]], sib_in[1], ["w_in_1"]), partials([gw_in[2]], sib_in[2], ["w_in_2"])))
    chips_in[1] = to_chips(after(part_in1, sib_in[2]), "w_in_1")
    chips_in[2] = to_chips(part_in2, "w_in_2")
    small_g, = all_gather_blocks(
        [after(jnp.concatenate([g_mix_vec, g_nffn, ffn_vec, g_ffn_vec.reshape(8 * NDEV, D)], axis=0), sib_in[2])],
        "all_gather_small", 0)
    grad_x, g_nmix = bwd_in(dz, w_in_g, dx1, xt, norm_mix, min(TM_BWD_IN, t))
    grad_x, chips_in = tie((grad_x, chips_in))
    nmix_g, = all_gather_blocks([g_nmix], "all_gather_norm_mix", 0)
    big["w_in"] = finish("w_in", gw_in, [s[0] for s in sib_in], [c[0] for c in chips_in], (w_in, m_w_in, v_w_in))

    red, = sum_blocks([small_g], "sum_small")
    red_n, = sum_blocks([after(nmix_g, (big["w_in"], red))], "sum_norm_mix")
    g_norm_mix, g_pool_scale, g_norm_ffn = red_n[0:1], red[0:1], red[8:9]
    g_conv_w = lax.dynamic_slice(red, (1, me * 128), (3, 128))
    g_norm_final = red[16]
    loss = red[17, 0]
    g_fcb = red[24:].reshape(NDEV, 8, D)[:, 0, :SH_UP].reshape(1, FF2)
    g_fcw = lax.dynamic_slice(red, (25 + 8 * me, 0), (3, SH_UP))
    grads = {"norm_mix": g_norm_mix, "pool_scale": g_pool_scale, "norm_ffn": g_norm_ffn, "norm_final": g_norm_final,
             "ffn_conv_b": g_fcb, "conv_w": g_conv_w.reshape(1, 3, 128), "ffn_conv_w": g_fcw.reshape(1, 3, SH_UP)}
    small_wmv = {"norm_mix": (norm_mix, m_norm_mix, v_norm_mix), "pool_scale": (pool_scale, m_pool_scale, v_pool_scale),
                 "norm_ffn": (norm_ffn, m_norm_ffn, v_norm_ffn), "norm_final": (norm_final, m_norm_final, v_norm_final),
                 "ffn_conv_b": (ffn_conv_b, m_ffn_conv_b, v_ffn_conv_b), "conv_w": (conv_w, m_conv_w, v_conv_w),
                 "ffn_conv_w": (ffn_conv_w, m_ffn_conv_w, v_ffn_conv_w)}
    small_names = list(small_wmv)
    flat2 = lambda a: a.reshape(1, -1) if a.ndim == 1 else (a.transpose(1, 0, 2) if a.ndim == 3 else a)
    unflat = lambda o, like: o.transpose(1, 0, 2) if like.ndim == 3 else o.reshape(like.shape)
    small_out = adamw_small([(flat2(small_wmv[nm][0]), flat2(grads[nm]), flat2(small_wmv[nm][1]),
                              flat2(small_wmv[nm][2])) for nm in small_names])
    small = {nm: [unflat(o, small_wmv[nm][0]) for o in outs] for nm, outs in zip(small_names, small_out)}

    order = ["norm_mix", "w_in", "pool_w", "pool_scale", "w_pool_proj", "conv_w", "w_conv_out", "w_o", "norm_ffn",
             "w_up", "ffn_conv_w", "ffn_conv_b", "w_down", "norm_final"]
    out = [loss, grad_x.reshape(nb, seq, D)]
    out += [big[nm][0] if nm in big else grads[nm] for nm in order]
    for idx in range(3):
        out += [big[nm][idx + 1] if nm in big else small[nm][idx] for nm in order]
    return tuple(out)
```

```python
import jax
import jax.numpy as jnp
from jax import lax
from jax.experimental import pallas as pl
from jax.experimental.pallas import tpu as pltpu
from jax.experimental.pallas import tpu_sc as plsc

F32 = jnp.float32
BF16 = jnp.bfloat16

NDEV = 8
D = 1024
NG = 4
CG = 256
WINS = (2, 4, 8, 16)
DIN = 6 * D
SH_IN = DIN // NDEV
NZT = DIN // CG
FF2 = 5632
SH_UP = FF2 // NDEV
FF = FF2 // 2
NCH = 4
SH_DN = FF // NDEV
RMS_EPS = 1e-6
HALO = 16

ADAM_LR = 0.001
ADAM_B1 = 0.9
ADAM_B2 = 0.999
ADAM_EPS = 1e-08
ADAM_WD = 0.01
ADAM_STEP = 10

TM_IN = 512
TM_BWD_IN = 256
TM_MIX = 256
TM_FFN = 256
TK_WGRAD = 1024
MIX_POOL_PROJ, MIX_CONV_OUT, MIX_O = 0, 1, 2
MIX_COLS = 3 * D + CG
VMEM_BIG = 56 * 1024 * 1024
MESH = pl.DeviceIdType.MESH
ANY = pl.BlockSpec(memory_space=pl.ANY)


def _cparams(n_axes, vmem=None):
    return pltpu.CompilerParams(dimension_semantics=("arbitrary",) * n_axes, vmem_limit_bytes=vmem)


def _dot(a, b):
    return jnp.dot(a, b, preferred_element_type=F32)


def _dot_nt(a, b):
    return lax.dot_general(a, b, (((1,), (1,)), ((), ())), preferred_element_type=F32)


def _dot_tn(a, b):
    return lax.dot_general(a, b, (((0,), (0,)), ((), ())), preferred_element_type=F32)


def _shift_down(ext, s, lead):
    return pltpu.roll(ext, s, 0)[lead:]


def _shift_up(ext, s, tm):
    n = ext.shape[0]
    return pltpu.roll(ext, n - s, 0)[:tm]


def _rms_inv(x):
    return lax.rsqrt(jnp.mean(x * x, axis=-1, keepdims=True) + RMS_EPS)


def _rms_bwd(dh, xn, inv, g):
    dxn = dh * g
    return inv * (dxn - xn * jnp.mean(dxn * xn, axis=-1, keepdims=True))


def _pos():
    return lax.axis_index("x"), lax.axis_index("y"), lax.axis_index("c")


def _handshake(peers):
    barrier = pltpu.get_barrier_semaphore()
    for peer in peers:
        pl.semaphore_signal(barrier, inc=1, device_id=peer, device_id_type=MESH)
    pl.semaphore_wait(barrier, len(peers))


def _sequencer(body, out_type, n_sems, name, collective_id):
    return pl.kernel(
        body, out_type=out_type, mesh=plsc.ScalarSubcoreMesh(axis_name="sequencer", num_cores=1), name=name,
        scratch_types=[pltpu.SemaphoreType.DMA((n_sems,)), pltpu.SemaphoreType.DMA((n_sems,))],
        compiler_params=pltpu.CompilerParams(collective_id=collective_id))


def all_gather_blocks(shards, name, collective_id):
    n = len(shards)

    def body(*refs):
        ins, outs = refs[:n], refs[n:2 * n]
        send_sems, recv_sems = refs[2 * n:]
        x, y, c = _pos()
        me, sibling = (x, y, c), (x, y, 1 - c)
        first_chip, second_chip, diagonal = (x ^ (1 - c), y ^ c), (x ^ c, y ^ (1 - c)), (1 - x, 1 - y)
        first, second = (*first_chip, c), (*second_chip, c)
        _handshake([sibling, first, second])

        def copy(w, k, block, to, src=None):
            slot = outs[w].at[4 * block[0] + 2 * block[1] + block[2]]
            return pltpu.make_async_remote_copy(
                src_ref=slot if src is None else src, dst_ref=slot,
                send_sem=send_sems.at[8 * w + k], recv_sem=recv_sems.at[8 * w + k], device_id=to, device_id_type=MESH)

        mine, sent = [], []
        for w in range(n):
            m = pltpu.make_async_copy(ins[w], outs[w].at[4 * x + 2 * y + c], send_sems.at[8 * w + 7])
            m.start()
            mine.append(m)
            sent += [copy(w, k, me, to, src=ins[w]) for k, to in enumerate((sibling, first, second))]
        for cp in sent:
            cp.start()
        for k, chip in ((1, first_chip), (2, second_chip), (3, diagonal)):
            for w in range(n):
                copy(w, k, (*chip, c), me).wait_recv()
                onward = [copy(w, 3 + k, (*chip, c), sibling)] + ([copy(w, 3, (*chip, c), second)] if k == 1 else [])
                for cp in onward:
                    cp.start()
                sent += onward
        for w in range(n):
            copy(w, 0, sibling, me).wait_recv()
            for k, chip in ((4, second_chip), (5, first_chip), (6, diagonal)):
                copy(w, k, (*chip, 1 - c), me).wait_recv()
        for cp in sent:
            cp.wait_send()
        for m in mine:
            m.wait()

    out = [jax.ShapeDtypeStruct((NDEV,) + s.shape, s.dtype) for s in shards]
    return _sequencer(body, out, 8 * n, name, collective_id)(*shards)


def _exchange_small(v, name):
    rows = v.shape[0]

    def body(v_ref, out_ref, slots, send_sems, recv_sems, local_sem):
        x, y, c = _pos()
        me = 4 * x + 2 * y + c
        mine = pltpu.make_async_copy(v_ref, slots.at[me], local_sem)
        mine.start()
        offs = [(dx, dy, dc) for dx in (0, 1) for dy in (0, 1) for dc in (0, 1)][1:]

        def copy(k, src_slot, to):
            return pltpu.make_async_remote_copy(
                src_ref=v_ref, dst_ref=slots.at[src_slot], send_sem=send_sems.at[k], recv_sem=recv_sems.at[k],
                device_id=to, device_id_type=MESH)

        sends = []
        for k, (dx, dy, dc) in enumerate(offs):
            cp = copy(k, me, (x ^ dx, y ^ dy, c ^ dc))
            cp.start()
            sends.append(cp)
        for k, (dx, dy, dc) in enumerate(offs):
            copy(k, 4 * (x ^ dx) + 2 * (y ^ dy) + (c ^ dc), (x, y, c)).wait_recv()
        for cp in sends:
            cp.wait_send()
        mine.wait()
        out_ref[...] = slots[...]

    vmem = pl.BlockSpec(memory_space=pltpu.VMEM)
    return pl.pallas_call(
        body, name=name, out_shape=jax.ShapeDtypeStruct((NDEV, rows, D), F32), in_specs=[vmem], out_specs=vmem,
        scratch_shapes=[pltpu.VMEM((NDEV, rows, D), F32), pltpu.SemaphoreType.DMA((7,)),
                        pltpu.SemaphoreType.DMA((7,)), pltpu.SemaphoreType.DMA],
    )(v)


def sum_blocks(gathered, name):
    def body(*refs):
        for g_ref, o_ref in zip(refs[:len(gathered)], refs[len(gathered):]):
            acc = g_ref[0]
            for d in range(1, NDEV):
                acc = acc + g_ref[d]
            o_ref[...] = acc

    vmem = pl.BlockSpec(memory_space=pltpu.VMEM)
    return pl.pallas_call(
        body, name=name, out_shape=[jax.ShapeDtypeStruct(g.shape[1:], F32) for g in gathered],
        in_specs=[vmem] * len(gathered), out_specs=[vmem] * len(gathered),
    )(*gathered)


def reduce_scatter_d2d(grads, name, collective_id):
    n = len(grads)

    def body(*refs):
        ins, outs = refs[:n], refs[n:2 * n]
        send_sems, recv_sems = refs[2 * n:]
        x, y, c = _pos()
        _handshake([(x, y, 1 - c)])
        cps = []
        for w in range(n):
            cp = pltpu.make_async_remote_copy(
                src_ref=ins[w].at[1 - c], dst_ref=outs[w], send_sem=send_sems.at[w], recv_sem=recv_sems.at[w],
                device_id=(x, y, 1 - c), device_id_type=MESH)
            cp.start()
            cps.append(cp)
        for cp in cps:
            cp.wait_recv()
        for cp in cps:
            cp.wait_send()

    out = [jax.ShapeDtypeStruct(g.shape[1:], F32) for g in grads]
    return _sequencer(body, out, n, name, collective_id)(*grads)


def reduce_scatter_ici(parts, name, collective_id):
    n = len(parts)

    def body(*refs):
        ins, outs = refs[:n], refs[n:2 * n]
        send_sems, recv_sems = refs[2 * n:]
        x, y, c = _pos()
        offs = [(1, 0), (0, 1), (1, 1)]
        _handshake([(x ^ dx, y ^ dy, c) for dx, dy in offs])
        cps = []
        for w in range(n):
            for k, (dx, dy) in enumerate(offs):
                ox, oy = x ^ dx, y ^ dy
                cp = pltpu.make_async_remote_copy(
                    src_ref=ins[w].at[2 * ox + oy], dst_ref=outs[w].at[2 * x + y],
                    send_sem=send_sems.at[3 * w + k], recv_sem=recv_sems.at[3 * w + k],
                    device_id=(ox, oy, c), device_id_type=MESH)
                cp.start()
                cps.append((cp, w, k, ox, oy))
        for cp, w, k, ox, oy in cps:
            pltpu.make_async_remote_copy(
                src_ref=ins[w].at[2 * ox + oy], dst_ref=outs[w].at[2 * ox + oy],
                send_sem=send_sems.at[3 * w + k], recv_sem=recv_sems.at[3 * w + k],
                device_id=(ox, oy, c), device_id_type=MESH).wait_recv()
        for cp, *_ in cps:
            cp.wait_send()

    out = [jax.ShapeDtypeStruct(p.shape, BF16) for p in parts]
    return _sequencer(body, out, 3 * n, name, collective_id)(*parts)


def fwd_in(x, g1, w_in_pieces, tm):
    t = x.shape[0]
    tile = lambda i: (i, 0)
    w_spec = pl.BlockSpec((NDEV, D, CG), lambda i: (0, 0, 0))
    z_spec = pl.BlockSpec((NDEV, tm, CG), lambda i: (0, i, 0))
    z_shape = jax.ShapeDtypeStruct((NDEV, t, CG), BF16)

    def cost(other_bytes, transcendentals):
        return pl.CostEstimate(flops=2 * t * D * NDEV * CG, transcendentals=transcendentals,
                               bytes_accessed=other_bytes + 2 * D * NDEV * CG + 2 * t * NDEV * CG)

    def first(x_ref, g_ref, w_ref, z_ref, h_ref, ht_ref):
        xf = x_ref[...]
        h = (xf * _rms_inv(xf) * g_ref[...]).astype(BF16)
        h_ref[...] = h
        ht_ref[...] = h.T
        for j in range(NDEV):
            z_ref[j] = _dot(h, w_ref[j]).astype(BF16)

    z0, h, ht = pl.pallas_call(
        first, name="fwd_in_0", grid=(t // tm,),
        in_specs=[pl.BlockSpec((tm, D), tile), pl.BlockSpec((1, D), lambda i: (0, 0)), w_spec],
        out_specs=[z_spec, pl.BlockSpec((tm, D), tile), pl.BlockSpec((D, tm), lambda i: (0, i))],
        out_shape=[z_shape, jax.ShapeDtypeStruct((t, D), BF16), jax.ShapeDtypeStruct((D, t), BF16)],
        compiler_params=_cparams(1, VMEM_BIG), cost_estimate=cost(8 * t * D + 4 * D, t),
    )(x, g1, w_in_pieces[0])
    zs = [z0]
    for q in (1, 2):
        h, zs[-1] = lax.optimization_barrier((h, zs[-1]))

        def later(h_ref, w_ref, z_ref):
            hb = h_ref[...]
            for j in range(NDEV):
                z_ref[j] = _dot(hb, w_ref[j]).astype(BF16)

        zs.append(pl.pallas_call(
            later, name=f"fwd_in_{q}", grid=(t // tm,),
            in_specs=[pl.BlockSpec((tm, D), tile), w_spec], out_specs=z_spec, out_shape=z_shape,
            compiler_params=_cparams(1, VMEM_BIG), cost_estimate=cost(2 * t * D, 0),
        )(h, w_in_pieces[q]))
    return zs, ht


def _z_tile(z_refs, n):
    return z_refs[n % 3][n // 3]


def _pool_tile(z, zh, win, keep_hist, cnt):
    zt = z.astype(F32)
    ext = jnp.concatenate([zh.astype(F32) * keep_hist, zt], axis=0)
    s, sh = ext, 1
    while sh < win:
        s = s + pltpu.roll(s, sh, 0)
        sh *= 2
    return s[HALO:] / cnt - zt


def _conv_taps(ext, cur, w_ref, lanes, lead):
    x1 = _shift_down(ext, 1, lead)
    x2 = _shift_down(ext, 2, lead)
    out = w_ref[2:3, lanes] * cur + w_ref[1:2, lanes] * x1 + w_ref[0:1, lanes] * x2
    return out, x1, x2


def fwd_mix(zs, x, pool_w, pool_scale, conv_w, wmix, tm, seq):
    t = x.shape[0]
    tps = seq // tm
    hb = tm // HALO

    def body(z0_ref, z1_ref, z2_ref, zh0_ref, zh1_ref, zh2_ref, x_ref, pw_ref, ps_ref, wpp_ref, cw_ref, wco_ref,
             wo_ref, x1_ref, yp_ref, yc_ref):
        z_refs, zh_refs = (z0_ref, z1_ref, z2_ref), (zh0_ref, zh1_ref, zh2_ref)
        it = pl.program_id(0) % tps
        keep_hist = jnp.where(it == 0, 0.0, 1.0)
        pos = it * tm + lax.broadcasted_iota(jnp.int32, (tm, 1), 0)
        p2 = []
        for g, win in enumerate(WINS):
            cnt = jnp.minimum(pos + 1, win).astype(F32)
            p = _pool_tile(_z_tile(z_refs, g), _z_tile(zh_refs, g), win, keep_hist, cnt)
            lanes = slice(g * CG, (g + 1) * CG)
            p2.append((_dot(p.astype(BF16), pw_ref[g]) * ps_ref[:, lanes]).astype(BF16))
        y_pool = _dot(jnp.concatenate(p2, axis=1), wpp_ref[...])
        u = []
        for q in range(NG):
            lanes = slice(q * CG, (q + 1) * CG)
            cv = _z_tile(z_refs, 8 + q).astype(F32) * _z_tile(z_refs, 12 + q).astype(F32)
            cvh = _z_tile(zh_refs, 8 + q).astype(F32) * _z_tile(zh_refs, 12 + q).astype(F32) * keep_hist
            cc, _, _ = _conv_taps(jnp.concatenate([cvh, cv], axis=0), cv, cw_ref, lanes, HALO)
            u.append((_z_tile(z_refs, 4 + q).astype(F32) * cc).astype(BF16))
        y_conv = _dot(jnp.concatenate(u, axis=1), wco_ref[...])
        ypb, ycb = y_pool.astype(BF16), y_conv.astype(BF16)
        yp_ref[...] = ypb
        yc_ref[...] = ycb
        merged = []
        for q in range(NG):
            lanes = slice(q * CG, (q + 1) * CG)
            sp = jax.nn.sigmoid(_z_tile(z_refs, 16 + q).astype(F32))
            sc = jax.nn.sigmoid(_z_tile(z_refs, 20 + q).astype(F32))
            merged.append((sp * ypb[:, lanes].astype(F32) + sc * ycb[:, lanes].astype(F32)).astype(BF16))
        x1_ref[...] = x_ref[...] + _dot(jnp.concatenate(merged, axis=1), wo_ref[...])

    def hist(i):
        return jnp.maximum(i * hb - 1, 0)

    const2 = lambda i: (0, 0)
    return pl.pallas_call(
        body, name="fwd_mix", grid=(t // tm,),
        in_specs=[pl.BlockSpec((NDEV, tm, CG), lambda i: (0, i, 0))] * 3
                 + [pl.BlockSpec((NDEV, HALO, CG), lambda i: (0, hist(i), 0))] * 3
                 + [pl.BlockSpec((tm, D), lambda i: (i, 0)),
                    pl.BlockSpec((NG, CG, CG), lambda i: (0, 0, 0)), pl.BlockSpec((1, D), const2),
                    pl.BlockSpec((D, D), lambda i: (0, MIX_POOL_PROJ)), pl.BlockSpec((3, D), const2),
                    pl.BlockSpec((D, D), lambda i: (0, MIX_CONV_OUT)), pl.BlockSpec((D, D), lambda i: (0, MIX_O))],
        out_specs=[pl.BlockSpec((tm, D), lambda i: (i, 0))] * 3,
        out_shape=[jax.ShapeDtypeStruct((t, D), F32), jax.ShapeDtypeStruct((t, D), BF16),
                   jax.ShapeDtypeStruct((t, D), BF16)],
        compiler_params=_cparams(1, VMEM_BIG),
    )(*zs, *zs, x, pool_w, pool_scale, wmix, conv_w, wmix, wmix)


def fwd_up(x1, g2, w_up_g, fcw, fcb, tm, seq):
    t = x1.shape[0]
    tps = seq // tm

    def body(x1_ref, g2_ref, wup_ref, fcw_ref, fcb_ref, up_ref, pre_ref, act_ref, actt_ref, h2t_ref, hist_ref):
        i = pl.program_id(0)
        keep_hist = jnp.where(i % tps == 0, 0.0, 1.0)

        @pl.when(i == 0)
        def _():
            hist_ref[...] = jnp.zeros_like(hist_ref)

        x1v = x1_ref[...]
        h2 = (x1v * _rms_inv(x1v) * g2_ref[...]).astype(BF16)
        h2t_ref[...] = h2.T
        lanes = slice(0, SH_UP)
        for c in range(NCH):
            conv = []
            for s in range(2):
                ub = _dot(h2, wup_ref[s, c]).astype(BF16)
                up_ref[s, c] = ub
                uf = ub.astype(F32)
                ext = jnp.concatenate([hist_ref[s, c] * keep_hist, uf], axis=0)
                hist_ref[s, c] = uf[tm - 8:]
                cc, _, _ = _conv_taps(ext, uf, fcw_ref.at[s, c], lanes, 8)
                conv.append(cc + fcb_ref[s, c])
                pre_ref[s, c] = conv[s].astype(BF16)
            a = (conv[0] * jax.nn.sigmoid(conv[0]) * conv[1]).astype(BF16)
            act_ref[c] = a
            actt_ref[c] = a.T

    tile = lambda i: (i, 0)
    const2 = lambda i: (0, 0)
    whole = lambda i: (0, 0, 0, 0)
    chunks = pl.BlockSpec((2, NCH, tm, SH_UP), lambda i: (0, 0, i, 0))
    return pl.pallas_call(
        body, name="fwd_up", grid=(t // tm,),
        in_specs=[pl.BlockSpec((tm, D), tile), pl.BlockSpec((1, D), const2),
                  pl.BlockSpec((2, NCH, D, SH_UP), whole), pl.BlockSpec((2, NCH, 3, SH_UP), whole),
                  pl.BlockSpec((2, NCH, 1, SH_UP), whole)],
        out_specs=[chunks, chunks, pl.BlockSpec((NCH, tm, SH_UP), lambda i: (0, i, 0)),
                   pl.BlockSpec((NCH, SH_UP, tm), lambda i: (0, 0, i)), pl.BlockSpec((D, tm), lambda i: (0, i))],
        out_shape=[jax.ShapeDtypeStruct((2, NCH, t, SH_UP), BF16), jax.ShapeDtypeStruct((2, NCH, t, SH_UP), BF16),
                   jax.ShapeDtypeStruct((NCH, t, SH_UP), BF16), jax.ShapeDtypeStruct((NCH, SH_UP, t), BF16),
                   jax.ShapeDtypeStruct((D, t), BF16)],
        scratch_shapes=[pltpu.VMEM((2, NCH, 8, SH_UP), F32)],
        compiler_params=_cparams(1, VMEM_BIG),
    )(x1, g2, w_up_g.reshape(2, NCH, D, SH_UP), fcw.reshape(2, NCH, 3, SH_UP), fcb.reshape(2, NCH, 1, SH_UP))


def fwd_down(x1, act, w_dn, gf, tgt, tm):
    t = x1.shape[0]

    def body(x1_ref, act_ref, wdn_ref, gf_ref, tgt_ref, dx2_ref, dx2b_ref, vec_ref):
        @pl.when(pl.program_id(0) == 0)
        def _():
            vec_ref[...] = jnp.zeros_like(vec_ref)

        d = None
        for c in range(NCH):
            part = _dot(act_ref[c], wdn_ref[c])
            d = part if d is None else d + part
        x2 = x1_ref[...] + d
        inv3 = _rms_inv(x2)
        xn = x2 * inv3
        diff = xn * gf_ref[...] - tgt_ref[...]
        dy = diff * (1.0 / D)
        vec_ref[0:1, :] += jnp.sum(dy * xn, axis=0, keepdims=True)
        vec_ref[1:2, :] += 0.5 * jnp.sum(jnp.mean(diff * diff, axis=-1))
        dx2 = _rms_bwd(dy, xn, inv3, gf_ref[...])
        dx2_ref[...] = dx2
        dx2b_ref[...] = dx2.astype(BF16)

    tile = lambda i: (i, 0)
    const2 = lambda i: (0, 0)
    return pl.pallas_call(
        body, name="fwd_down", grid=(t // tm,),
        in_specs=[pl.BlockSpec((tm, D), tile), pl.BlockSpec((NCH, tm, SH_UP), lambda i: (0, i, 0)),
                  pl.BlockSpec((NCH, SH_UP, D), lambda i: (0, 0, 0)), pl.BlockSpec((1, D), const2),
                  pl.BlockSpec((tm, D), tile)],
        out_specs=[pl.BlockSpec((tm, D), tile), pl.BlockSpec((tm, D), tile), pl.BlockSpec((8, D), const2)],
        out_shape=[jax.ShapeDtypeStruct((t, D), F32), jax.ShapeDtypeStruct((t, D), BF16),
                   jax.ShapeDtypeStruct((8, D), F32)],
        compiler_params=_cparams(1, VMEM_BIG),
    )(x1, act, w_dn, gf, tgt)


def bwd_ffn(dx2, x1, g2, up, pre, w_up_g, fcw, w_dn, tm, seq):
    t = x1.shape[0]
    nt = t // tm
    tps = seq // tm

    def body(dx2_ref, x1_ref, g2_ref, up_ref, pre_ref, wup_ref, fcw_ref, wdn_ref,
             dup_ref, dx1_ref, gvec_ref, gn_ref, carry_ref):
        i = pl.program_id(0)
        it = (nt - 1 - i) % tps
        keep_next = jnp.where(it == tps - 1, 0.0, 1.0)

        @pl.when(i == 0)
        def _():
            gvec_ref[...] = jnp.zeros_like(gvec_ref)
            gn_ref[...] = jnp.zeros_like(gn_ref)
            carry_ref[...] = jnp.zeros_like(carry_ref)

        dx2v = dx2_ref[...]
        dxb = dx2v.astype(BF16)
        lanes = slice(0, SH_UP)
        dh2 = None
        for c in range(NCH):
            pre = [pre_ref[s, c].astype(F32) for s in range(2)]
            sg = jax.nn.sigmoid(pre[0])
            dact = _dot_nt(dxb, wdn_ref[c])
            dpre = [dact * pre[1] * (sg * (1.0 + pre[0] * (1.0 - sg))), dact * (pre[0] * sg)]
            for s in range(2):
                dc = dpre[s]
                ext = jnp.concatenate([dc, carry_ref[s, c] * keep_next], axis=0)
                carry_ref[s, c] = dc[:8]
                shifted = (_shift_up(ext, 2, tm), _shift_up(ext, 1, tm), dc)
                uf = up_ref[s, c].astype(F32)
                gvec_ref[s, c, 0:1, lanes] += jnp.sum(dc, axis=0, keepdims=True)
                for tap in range(3):
                    gvec_ref[s, c, tap + 1:tap + 2, lanes] += jnp.sum(shifted[tap] * uf, axis=0, keepdims=True)
                w = fcw_ref.at[s, c]
                du = w[2:3, :] * dc + w[1:2, :] * shifted[1] + w[0:1, :] * shifted[0]
                dub = du.astype(BF16)
                dup_ref[s, c] = dub
                part = _dot_nt(dub, wup_ref[s, c])
                dh2 = part if dh2 is None else dh2 + part
        x1v = x1_ref[...]
        inv2 = _rms_inv(x1v)
        xn = x1v * inv2
        gn_ref[0:1, :] += jnp.sum(dh2 * xn, axis=0, keepdims=True)
        dx1_ref[...] = dx2v + _rms_bwd(dh2, xn, inv2, g2_ref[...])

    rev = lambda i: (nt - 1 - i, 0)
    const2 = lambda i: (0, 0)
    whole = lambda i: (0, 0, 0, 0)
    chunks = pl.BlockSpec((2, NCH, tm, SH_UP), lambda i: (0, 0, nt - 1 - i, 0))
    return pl.pallas_call(
        body, name="bwd_ffn", grid=(nt,),
        in_specs=[pl.BlockSpec((tm, D), rev), pl.BlockSpec((tm, D), rev), pl.BlockSpec((1, D), const2),
                  chunks, chunks, pl.BlockSpec((2, NCH, D, SH_UP), whole), pl.BlockSpec((2, NCH, 3, SH_UP), whole),
                  pl.BlockSpec((NCH, SH_UP, D), lambda i: (0, 0, 0))],
        out_specs=[chunks, pl.BlockSpec((tm, D), rev), pl.BlockSpec((2, NCH, 8, D), whole),
                   pl.BlockSpec((8, D), const2)],
        out_shape=[jax.ShapeDtypeStruct((2, NCH, t, SH_UP), BF16), jax.ShapeDtypeStruct((t, D), F32),
                   jax.ShapeDtypeStruct((2, NCH, 8, D), F32), jax.ShapeDtypeStruct((8, D), F32)],
        scratch_shapes=[pltpu.VMEM((2, NCH, 8, SH_UP), F32)],
        compiler_params=_cparams(1, VMEM_BIG),
    )(dx2, x1, g2, up, pre, w_up_g.reshape(2, NCH, D, SH_UP), fcw.reshape(2, NCH, 3, SH_UP), w_dn)


def bwd_mix(dx1, zs, y_pool, y_conv, pool_w, pool_scale, conv_w, wmix, tm, seq):
    t = dx1.shape[0]
    nt = t // tm
    tps = seq // tm
    hb = tm // HALO

    def body(da_ref, z0_ref, z1_ref, z2_ref, zh0_ref, zh1_ref, zh2_ref, yp_ref, yc_ref, pw_ref, ps_ref, wpp_ref,
             cw_ref, wco_ref, wo_ref,
             dz_ref, mg_ref, p2_ref, u_ref, dyp_ref, dyc_ref, p_ref, dpw_ref, gvec_ref, cp_ref, cc_ref):
        z_refs, zh_refs = (z0_ref, z1_ref, z2_ref), (zh0_ref, zh1_ref, zh2_ref)
        i = pl.program_id(0)
        it = (nt - 1 - i) % tps
        keep_hist = jnp.where(it == 0, 0.0, 1.0)
        keep_next = jnp.where(it == tps - 1, 0.0, 1.0)
        pos = it * tm + lax.broadcasted_iota(jnp.int32, (tm, 1), 0)

        @pl.when(i == 0)
        def _():
            gvec_ref[...] = jnp.zeros_like(gvec_ref)
            cp_ref[...] = jnp.zeros_like(cp_ref)
            cc_ref[...] = jnp.zeros_like(cc_ref)

        dm = _dot_nt(da_ref[...].astype(BF16), wo_ref[...])
        merged, dyp, dyc = [], [], []
        for q in range(NG):
            lanes = slice(q * CG, (q + 1) * CG)
            sp = jax.nn.sigmoid(_z_tile(z_refs, 16 + q).astype(F32))
            sc = jax.nn.sigmoid(_z_tile(z_refs, 20 + q).astype(F32))
            yp = yp_ref[:, lanes].astype(F32)
            yc = yc_ref[:, lanes].astype(F32)
            dmq = dm[:, lanes]
            merged.append((sp * yp + sc * yc).astype(BF16))
            dyp.append((dmq * sp).astype(BF16))
            dyc.append((dmq * sc).astype(BF16))
            dz_ref[16 + q] = (dmq * yp * (sp * (1.0 - sp))).astype(BF16)
            dz_ref[20 + q] = (dmq * yc * (sc * (1.0 - sc))).astype(BF16)
        mg_ref[...] = jnp.concatenate(merged, axis=1)
        dypb = jnp.concatenate(dyp, axis=1)
        dycb = jnp.concatenate(dyc, axis=1)
        dyp_ref[...] = dypb
        dyc_ref[...] = dycb

        dp2 = _dot_nt(dypb, wpp_ref[...])
        p2 = []
        for g, win in enumerate(WINS):
            lanes = slice(g * CG, (g + 1) * CG)
            cnt = jnp.minimum(pos + 1, win).astype(F32)
            p = _pool_tile(_z_tile(z_refs, g), _z_tile(zh_refs, g), win, keep_hist, cnt)
            pb = p.astype(BF16)
            p_ref[g] = pb
            pw = _dot(pb, pw_ref[g])
            p2.append((pw * ps_ref[:, lanes]).astype(BF16))
            dp2g = dp2[:, lanes]
            gvec_ref[0:1, lanes] += jnp.sum(dp2g * pw, axis=0, keepdims=True)
            dpwb = (dp2g * ps_ref[:, lanes]).astype(BF16)
            dpw_ref[g] = dpwb
            dp = _dot_nt(dpwb, pw_ref[g])
            qv = dp / cnt
            ext = jnp.concatenate([qv, cp_ref[g] * keep_next], axis=0)
            cp_ref[g] = qv[:HALO]
            n = tm + HALO
            s, sh = ext, 1
            while sh < win:
                s = s + pltpu.roll(s, n - sh, 0)
                sh *= 2
            dz_ref[g] = (s[:tm] - dp).astype(BF16)
        p2_ref[...] = jnp.concatenate(p2, axis=1)

        du = _dot_nt(dycb, wco_ref[...])
        u = []
        for q in range(NG):
            lanes = slice(q * CG, (q + 1) * CG)
            zb = _z_tile(z_refs, 4 + q).astype(F32)
            zc = _z_tile(z_refs, 8 + q).astype(F32)
            zv = _z_tile(z_refs, 12 + q).astype(F32)
            cv = zc * zv
            cvh = _z_tile(zh_refs, 8 + q).astype(F32) * _z_tile(zh_refs, 12 + q).astype(F32) * keep_hist
            cc, cv1, cv2 = _conv_taps(jnp.concatenate([cvh, cv], axis=0), cv, cw_ref, lanes, HALO)
            u.append((zb * cc).astype(BF16))
            duq = du[:, lanes]
            dz_ref[4 + q] = (duq * cc).astype(BF16)
            dcc = duq * zb
            for tap, src in enumerate((cv2, cv1, cv)):
                gvec_ref[tap + 1:tap + 2, lanes] += jnp.sum(dcc * src, axis=0, keepdims=True)
            ext = jnp.concatenate([dcc, cc_ref[:, lanes] * keep_next], axis=0)
            cc_ref[:, lanes] = dcc[:8]
            dcv = (cw_ref[2:3, lanes] * dcc + cw_ref[1:2, lanes] * _shift_up(ext, 1, tm)
                   + cw_ref[0:1, lanes] * _shift_up(ext, 2, tm))
            dz_ref[8 + q] = (dcv * zv).astype(BF16)
            dz_ref[12 + q] = (dcv * zc).astype(BF16)
        u_ref[...] = jnp.concatenate(u, axis=1)

    def hist(i):
        return jnp.maximum((nt - 1 - i) * hb - 1, 0)

    rev = lambda i: (nt - 1 - i, 0)
    rev3 = lambda i: (0, nt - 1 - i, 0)
    const2 = lambda i: (0, 0)
    tok = jax.ShapeDtypeStruct((t, D), BF16)
    grp = jax.ShapeDtypeStruct((NG, t, CG), BF16)
    return pl.pallas_call(
        body, name="bwd_mix", grid=(nt,),
        in_specs=[pl.BlockSpec((tm, D), rev)] + [pl.BlockSpec((NDEV, tm, CG), rev3)] * 3
                 + [pl.BlockSpec((NDEV, HALO, CG), lambda i: (0, hist(i), 0))] * 3
                 + [pl.BlockSpec((tm, D), rev), pl.BlockSpec((tm, D), rev),
                    pl.BlockSpec((NG, CG, CG), lambda i: (0, 0, 0)), pl.BlockSpec((1, D), const2),
                    pl.BlockSpec((D, D), lambda i: (0, MIX_POOL_PROJ)), pl.BlockSpec((3, D), const2),
                    pl.BlockSpec((D, D), lambda i: (0, MIX_CONV_OUT)), pl.BlockSpec((D, D), lambda i: (0, MIX_O))],
        out_specs=[pl.BlockSpec((NZT, tm, CG), rev3)] + [pl.BlockSpec((tm, D), rev)] * 5
                  + [pl.BlockSpec((NG, tm, CG), rev3)] * 2 + [pl.BlockSpec((8, D), const2)],
        out_shape=[jax.ShapeDtypeStruct((NZT, t, CG), BF16), tok, tok, tok, tok, tok, grp, grp,
                   jax.ShapeDtypeStruct((8, D), F32)],
        scratch_shapes=[pltpu.VMEM((NG, HALO, CG), F32), pltpu.VMEM((8, D), F32)],
        compiler_params=_cparams(1, VMEM_BIG),
    )(dx1, *zs, *zs, y_pool, y_conv, pool_w, pool_scale, wmix, conv_w, wmix, wmix)


def bwd_in(dz, w_in_pieces, dx1, x, g1, tm):
    t = x.shape[0]

    def body(dz_ref, w0_ref, w1_ref, w2_ref, dx1_ref, x_ref, g_ref, gx_ref, gn_ref):
        @pl.when(pl.program_id(0) == 0)
        def _():
            gn_ref[...] = jnp.zeros_like(gn_ref)

        dh = None
        for j in range(NDEV):
            for q, w_ref in enumerate((w0_ref, w1_ref, w2_ref)):
                part = _dot_nt(dz_ref[3 * j + q], w_ref[j])
                dh = part if dh is None else dh + part
        xv = x_ref[...]
        inv = _rms_inv(xv)
        xn = xv * inv
        gn_ref[0:1, :] += jnp.sum(dh * xn, axis=0, keepdims=True)
        gx_ref[...] = dx1_ref[...] + _rms_bwd(dh, xn, inv, g_ref[...])

    tile = lambda i: (i, 0)
    return pl.pallas_call(
        body, name="bwd_in", grid=(t // tm,),
        in_specs=[pl.BlockSpec((NZT, tm, CG), lambda i: (0, i, 0))]
                 + [pl.BlockSpec((NDEV, D, CG), lambda i: (0, 0, 0))] * 3
                 + [pl.BlockSpec((tm, D), tile), pl.BlockSpec((tm, D), tile), pl.BlockSpec((1, D), lambda i: (0, 0))],
        out_specs=[pl.BlockSpec((tm, D), tile), pl.BlockSpec((8, D), lambda i: (0, 0))],
        out_shape=[jax.ShapeDtypeStruct((t, D), F32), jax.ShapeDtypeStruct((8, D), F32)],
        compiler_params=_cparams(1, VMEM_BIG),
    )(dz, *w_in_pieces, dx1, x, g1)


def _slot(j):
    return j % 2, j // 2


def wgrad_cols(at, b, q, name):
    m, t = at.shape
    width = b.shape[3]

    def body(a_ref, b_ref, o_ref):
        o_ref[...] = _dot(a_ref[...], b_ref[...])

    return pl.pallas_call(
        body, name=name, grid=(NDEV,),
        in_specs=[pl.BlockSpec((m, t), lambda j: (0, 0)),
                  pl.BlockSpec((None, None, t, width), lambda j: (j, q, 0, 0))],
        out_specs=pl.BlockSpec((None, None, m, width), lambda j: (j % 2, j // 2, 0, 0)),
        out_shape=jax.ShapeDtypeStruct((2, 4, m, width), F32),
        compiler_params=_cparams(1, VMEM_BIG),
    )(at, b)


def wgrad_down(actt, dx2b):
    t = dx2b.shape[0]

    def body(a_ref, b_ref, o_ref):
        r = _dot(a_ref[...], b_ref[...])
        o_ref[0] = r[:SH_DN]
        o_ref[1] = r[SH_DN:]

    return pl.pallas_call(
        body, name="wgrad_down", grid=(NCH,),
        in_specs=[pl.BlockSpec((None, SH_UP, t), lambda k: (k, 0, 0)), pl.BlockSpec((t, D), lambda k: (0, 0))],
        out_specs=pl.BlockSpec((2, None, SH_DN, D), lambda k: (0, k, 0, 0)),
        out_shape=jax.ShapeDtypeStruct((2, 4, SH_DN, D), F32),
        compiler_params=_cparams(1, VMEM_BIG),
    )(actt, dx2b)


def wgrad_square(a, b, name, tk):
    t = a.shape[0]

    def body(a_ref, b_ref, o_ref, acc_ref):
        kt = pl.program_id(0)

        @pl.when(kt == 0)
        def _():
            acc_ref[...] = jnp.zeros_like(acc_ref)

        acc_ref[...] += _dot_tn(a_ref[...], b_ref[...].astype(BF16))

        @pl.when(kt == pl.num_programs(0) - 1)
        def _():
            for j in range(NDEV):
                cc, xy = _slot(j)
                o_ref[cc, xy] = acc_ref[j * 128:(j + 1) * 128]

    return pl.pallas_call(
        body, name=name, grid=(t // tk,),
        in_specs=[pl.BlockSpec((tk, D), lambda k: (k, 0)), pl.BlockSpec((tk, D), lambda k: (k, 0))],
        out_specs=pl.BlockSpec((2, 4, 128, D), lambda k: (0, 0, 0, 0)),
        out_shape=jax.ShapeDtypeStruct((2, 4, 128, D), F32),
        scratch_shapes=[pltpu.VMEM((D, D), F32)],
        compiler_params=_cparams(1, VMEM_BIG),
    )(a, b)


def wgrad_pool(p, dpw, tk):
    t = p.shape[1]

    def body(a_ref, b_ref, o_ref):
        @pl.when(pl.program_id(0) == 0)
        def _():
            o_ref[...] = jnp.zeros_like(o_ref)

        for g in range(NG):
            o_ref[g] += _dot_tn(a_ref[g], b_ref[g])

    return pl.pallas_call(
        body, name="wgrad_pool", grid=(t // tk,),
        in_specs=[pl.BlockSpec((NG, tk, CG), lambda k: (0, k, 0))] * 2,
        out_specs=pl.BlockSpec((NG, CG, CG), lambda k: (0, 0, 0)),
        out_shape=jax.ShapeDtypeStruct((NG, CG, CG), F32),
        compiler_params=_cparams(1, VMEM_BIG),
    )(p, dpw)


def _adamw(w, g, m, v):
    m = ADAM_B1 * m + (1.0 - ADAM_B1) * g
    v = ADAM_B2 * v + (1.0 - ADAM_B2) * (g * g)
    m_hat = m / (1.0 - ADAM_B1 ** ADAM_STEP)
    v_hat = v / (1.0 - ADAM_B2 ** ADAM_STEP)
    delta = -ADAM_LR * (m_hat / (jnp.sqrt(v_hat) + ADAM_EPS) + ADAM_WD * w)
    return delta, m, v


def _row_block(r):
    return 512 if r % 512 == 0 else r


def chip_partial(place, g, from_sibling, name):
    _, _, r, c = g.shape

    def body(place_ref, g_ref, s_ref, o_ref):
        o_ref[...] = (g_ref[...] + s_ref[...]).astype(BF16)

    return pl.pallas_call(
        body, name=name,
        grid_spec=pltpu.PrefetchScalarGridSpec(
            num_scalar_prefetch=1, grid=(3,),
            in_specs=[pl.BlockSpec((None, None, r, c), lambda k, pr: (pr[0], pr[1] ^ (k + 1), 0, 0)),
                      pl.BlockSpec((None, r, c), lambda k, pr: (pr[1] ^ (k + 1), 0, 0))],
            out_specs=pl.BlockSpec((None, r, c), lambda k, pr: (pr[1] ^ (k + 1), 0, 0))),
        out_shape=jax.ShapeDtypeStruct((4, r, c), BF16),
        compiler_params=_cparams(1, VMEM_BIG),
    )(place, g, from_sibling)


def finish_adamw(place, gs, from_sibling, from_chips, w, m, v, name, transposed=False):
    n = len(gs)
    r = gs[0].shape[2]
    widths = [g.shape[3] for g in gs]
    c = sum(widths)
    br = _row_block(r)

    def body(place_ref, *refs):
        g_refs, s_refs, c_refs = refs[:n], refs[n:2 * n], refs[2 * n:5 * n]
        w_ref, m_ref, v_ref, og_ref, od_ref, om_ref, ov_ref = refs[5 * n:]
        cols = []
        for q in range(n):
            grad = g_refs[q][...] + s_refs[q][...]
            for k in range(3):
                grad = grad + c_refs[3 * q + k][...].astype(F32)
            cols.append(grad)
        grad = cols[0] if n == 1 else jnp.concatenate(cols, axis=1)
        if transposed:
            grad = grad.T
        og_ref[...] = grad
        od_ref[...], om_ref[...], ov_ref[...] = _adamw(w_ref[...], grad, m_ref[...], v_ref[...])

    def other(k, cq):
        return pl.BlockSpec((None, br, cq), lambda i, pr: (pr[1] ^ k, i, 0))

    row = pl.BlockSpec((c, br), lambda i, pr: (0, i)) if transposed else pl.BlockSpec((br, c), lambda i, pr: (i, 0))
    out = jax.ShapeDtypeStruct((c, r) if transposed else (r, c), F32)
    in_specs = [pl.BlockSpec((None, None, br, cq), lambda i, pr: (pr[0], pr[1], i, 0)) for cq in widths]
    in_specs += [pl.BlockSpec((None, br, cq), lambda i, pr: (pr[1], i, 0)) for cq in widths]
    in_specs += [other(k, cq) for cq in widths for k in (1, 2, 3)]
    return pl.pallas_call(
        body, name=name,
        grid_spec=pltpu.PrefetchScalarGridSpec(
            num_scalar_prefetch=1, grid=(r // br,), in_specs=in_specs + [row, row, row], out_specs=[row] * 4),
        out_shape=[out] * 4,
        compiler_params=_cparams(1, VMEM_BIG),
    )(place, *gs, *from_sibling, *[fc for fc in from_chips for _ in range(3)], w, m, v)


def adamw_small(items):
    n = len(items)

    def body(*refs):
        ins, outs = refs[:4 * n], refs[4 * n:]
        for i in range(n):
            w, g, m, v = (r[...] for r in ins[4 * i:4 * i + 4])
            outs[3 * i][...], outs[3 * i + 1][...], outs[3 * i + 2][...] = _adamw(w, g, m, v)

    out = [jax.ShapeDtypeStruct(it[0].shape, F32) for it in items for _ in range(3)]
    res = pl.pallas_call(body, name="adamw_small", out_shape=out)(*[a for it in items for a in it])
    return [res[3 * i:3 * i + 3] for i in range(n)]


def kernel(x, norm_mix, w_in, pool_w, pool_scale, w_pool_proj, conv_w, w_conv_out, w_o, norm_ffn, w_up, ffn_conv_w, ffn_conv_b, w_down, norm_final, loss_target, m_norm_mix, m_w_in, m_pool_w, m_pool_scale, m_w_pool_proj, m_conv_w, m_w_conv_out, m_w_o, m_norm_ffn, m_w_up, m_ffn_conv_w, m_ffn_conv_b, m_w_down, m_norm_final, v_norm_mix, v_w_in, v_pool_w, v_pool_scale, v_w_pool_proj, v_conv_w, v_w_conv_out, v_w_o, v_norm_ffn, v_w_up, v_ffn_conv_w, v_ffn_conv_b, v_w_down, v_norm_final):
    nb, seq, _ = x.shape
    t = nb * seq
    tm_in = min(TM_IN, t)
    tm_mix = min(TM_MIX, seq)
    tm_ffn = min(TM_FFN, seq)
    tk = min(TK_WGRAD, t)
    xt = x.reshape(t, D)
    tgt = loss_target.reshape(t, D)
    xi, yi, ci = _pos()
    me = 4 * xi + 2 * yi + ci
    place = jnp.stack([ci, 2 * xi + yi]).astype(jnp.int32)

    tie = lax.optimization_barrier
    w_in_b = w_in[0].astype(BF16)
    w_in_g = [all_gather_blocks([w_in_b[:, q * CG:(q + 1) * CG]], f"all_gather_w_in_{q}", 0)[0] for q in range(3)]
    taps = (jnp.pad(conv_w[0], ((0, 5), (0, D - 128))) + jnp.pad(ffn_conv_w[0], ((3, 2), (0, D - SH_UP))))
    taps_g = _exchange_small(taps, "all_gather_taps")
    mix_shard = jnp.concatenate(
        [w_pool_proj[0], w_conv_out[0], w_o[0], pool_w[0].reshape(NG * 32, CG)], axis=1).astype(BF16)
    conv_w_f = taps_g[:, 0:3, :128].transpose(1, 0, 2).reshape(3, D)
    fcw_f = taps_g[:, 3:6, :SH_UP]
    fcb_f = ffn_conv_b.reshape(NDEV, 1, SH_UP)
    mix_shard, conv_w_f, fcw_f, fcb_f = tie((mix_shard, conv_w_f, fcw_f, fcb_f))
    wmix_g, = all_gather_blocks([mix_shard], "all_gather_w_mix", 0)
    ffn_shards, w_in_g[0] = tie(([w_up[0].astype(BF16), w_down[0].astype(BF16)], w_in_g[0]))
    w_up_g, = all_gather_blocks(ffn_shards[:1], "all_gather_w_up", 0)
    w_dn_g, = all_gather_blocks(ffn_shards[1:], "all_gather_w_down", 0)
    w_dn_f = w_dn_g.reshape(NCH, SH_UP, D)
    gfin = norm_final.reshape(1, D)

    zs, h1 = fwd_in(xt, norm_mix, w_in_g, tm_in)
    wmix_g, zs = tie((wmix_g, zs))
    wmix = wmix_g.reshape(D, MIX_COLS)
    pool_w_f = wmix_g[:, :, 3 * D:].reshape(NDEV, NG, 32, CG).transpose(1, 0, 2, 3).reshape(NG, CG, CG)
    x1, y_pool, y_conv = fwd_mix(zs, xt, pool_w_f, pool_scale, conv_w_f, wmix, tm_mix, seq)
    up, pre, act_tok, act, h2 = fwd_up(x1, norm_ffn, w_up_g, fcw_f, fcb_f, tm_ffn, seq)
    dx2, dx2b, ffn_vec = fwd_down(x1, act_tok, w_dn_f, gfin, tgt, min(TM_IN, t))

    def to_sibling(full, tag):
        return reduce_scatter_d2d(full, "reduce_scatter_d2d_" + tag, 1)

    def partials(full, from_sib, names):
        return [chip_partial(place, g, s, "chip_partial_" + nm) for g, s, nm in zip(full, from_sib, names)]

    def to_chips(parts, tag):
        return reduce_scatter_ici(parts, "reduce_scatter_ici_" + tag, 2)

    def finish(nm, gs, from_sib, from_chips, wmv, transposed=False):
        rc = (gs[0].shape[2], sum(g.shape[3] for g in gs))
        wmv2 = [a.reshape(rc).T if transposed else a.reshape(rc) for a in wmv]
        outs = finish_adamw(place, gs, from_sib, from_chips, *wmv2, "adamw_" + nm, transposed)
        return [(o.T if transposed else o).reshape(wmv[0].shape) for o in outs]

    def after(x, dep):
        return tie((x, dep))[0]

    big = {}
    d_up, dx1, g_ffn_vec, g_nffn = bwd_ffn(dx2, x1, norm_ffn, up, pre, w_up_g, fcw_f, w_dn_f, tm_ffn, seq)
    gw_up = wgrad_cols(h2, d_up.reshape(NDEV, 1, t, SH_UP), 0, "wgrad_up")
    sib_up = to_sibling([gw_up], "w_up")
    gw_dn = wgrad_down(act, after(dx2b, gw_up))
    sib_dn = to_sibling([after(gw_dn, sib_up)], "w_down")
    dx1, part_up = tie((dx1, partials([gw_up], sib_up, ["w_up"])))
    chips_up = to_chips(part_up, "w_up")
    dz, merged, p2, u, dyp, dyc, p, dpw, g_mix_vec = bwd_mix(
        dx1, zs, y_pool, y_conv, pool_w_f, pool_scale, conv_w_f, wmix, tm_mix, seq)
    merged, part_dn = tie((merged, partials([gw_dn], sib_dn, ["w_down"])))
    chips_dn = to_chips(part_dn, "w_down")
    gw_o = wgrad_square(merged, dx1, "wgrad_o", tk)
    gw_pp = wgrad_square(p2, dyp, "wgrad_pool_proj", tk)
    gw_co = wgrad_square(u, dyc, "wgrad_conv_out", tk)
    gw_pool = wgrad_pool(p, dpw, tk).reshape(NG, 4, 2, 32, CG).transpose(2, 1, 0, 3, 4).reshape(2, 4, NG * 32, CG)
    dz8 = dz.reshape(NDEV, 3, t, CG)
    gw_in, sib_in, chips_in = [None] * 3, [None] * 3, [None] * 3
    sib_a = to_sibling(after([gw_o, gw_pp], (chips_up, gw_pool, chips_dn)), "mix_a")
    sib_b = to_sibling(after([gw_co, gw_pool], sib_a), "mix_b")
    gw_in[0] = wgrad_cols(h1, dz8, 0, "wgrad_in_0")
    h1, part_a, part_b = tie((h1, partials([gw_o, gw_pp], sib_a, ["w_o", "w_pool_proj"]),
                              partials([gw_co, gw_pool], sib_b, ["w_conv_out", "pool_w"])))
    chips_a = to_chips(after(part_a, chips_dn), "mix_a")
    chips_b = to_chips(part_b, "mix_b")
    sib_in[0] = to_sibling(after([gw_in[0]], sib_b), "w_in_0")
    gw_in[1] = wgrad_cols(h1, dz8, 1, "wgrad_in_1")
    h1, part_in0, gw_in[1] = tie((h1, partials([gw_in[0]], sib_in[0], ["w_in_0"]), gw_in[1]))
    chips_in[0] = to_chips(part_in0, "w_in_0")
    sib_in[1] = to_sibling(after([gw_in[1]], sib_in[0]), "w_in_1")
    h1, big["w_down"], big["w_up"] = tie((
        h1, finish("w_down", [gw_dn], sib_dn, chips_dn, (w_down, m_w_down, v_w_down)),
        finish("w_up", [gw_up], sib_up, chips_up, (w_up, m_w_up, v_w_up), transposed=True)))
    gw_in[2] = wgrad_cols(h1, dz8, 2, "wgrad_in_2")
    sib_in[2] = to_sibling(after([gw_in[2]], (chips_a, chips_b, chips_in[0])), "w_in_2")
    sib_in[2], big["w_o"], big["w_pool_proj"], big["w_conv_out"], big["pool_w"] = tie((
        sib_in[2],
        finish("w_o", [gw_o], sib_a[:1], chips_a[:1], (w_o, m_w_o, v_w_o)),
        finish("w_pool_proj", [gw_pp], sib_a[1:], chips_a[1:], (w_pool_proj, m_w_pool_proj, v_w_pool_proj)),
        finish("w_conv_out", [gw_co], sib_b[:1], chips_b[:1], (w_conv_out, m_w_conv_out, v_w_conv_out)),
        finish("pool_w", [gw_pool], sib_b[1:], chips_b[1:], (pool_w, m_pool_w, v_pool_w))))
    dx1, part_in1, part_in2 = tie((
        dx1, partials([gw_in[1]], sib_in[1], ["w_in_1"]), partials([gw_in[2]], sib_in[2], ["w_in_2"])))
    chips_in[1] = to_chips(after(part_in1, sib_in[2]), "w_in_1")
    chips_in[2] = to_chips(part_in2, "w_in_2")
    small_g, = all_gather_blocks(
        [after(jnp.concatenate([g_mix_vec, g_nffn, ffn_vec, g_ffn_vec.reshape(8 * NDEV, D)], axis=0), sib_in[2])],
        "all_gather_small", 0)
    grad_x, g_nmix = bwd_in(dz, w_in_g, dx1, xt, norm_mix, min(TM_BWD_IN, t))
    grad_x, chips_in = tie((grad_x, chips_in))
    nmix_g, = all_gather_blocks([g_nmix], "all_gather_norm_mix", 0)
    big["w_in"] = finish("w_in", gw_in, [s[0] for s in sib_in], [c[0] for c in chips_in], (w_in, m_w_in, v_w_in))

    red, = sum_blocks([small_g], "sum_small")
    red_n, = sum_blocks([after(nmix_g, (big["w_in"], red))], "sum_norm_mix")
    g_norm_mix, g_pool_scale, g_norm_ffn = red_n[0:1], red[0:1], red[8:9]
    g_conv_w = lax.dynamic_slice(red, (1, me * 128), (3, 128))
    g_norm_final = red[16]
    loss = red[17, 0]
    g_fcb = red[24:].reshape(NDEV, 8, D)[:, 0, :SH_UP].reshape(1, FF2)
    g_fcw = lax.dynamic_slice(red, (25 + 8 * me, 0), (3, SH_UP))
    grads = {"norm_mix": g_norm_mix, "pool_scale": g_pool_scale, "norm_ffn": g_norm_ffn, "norm_final": g_norm_final,
             "ffn_conv_b": g_fcb, "conv_w": g_conv_w.reshape(1, 3, 128), "ffn_conv_w": g_fcw.reshape(1, 3, SH_UP)}
    small_wmv = {"norm_mix": (norm_mix, m_norm_mix, v_norm_mix), "pool_scale": (pool_scale, m_pool_scale, v_pool_scale),
                 "norm_ffn": (norm_ffn, m_norm_ffn, v_norm_ffn), "norm_final": (norm_final, m_norm_final, v_norm_final),
                 "ffn_conv_b": (ffn_conv_b, m_ffn_conv_b, v_ffn_conv_b), "conv_w": (conv_w, m_conv_w, v_conv_w),
                 "ffn_conv_w": (ffn_conv_w, m_ffn_conv_w, v_ffn_conv_w)}
    small_names = list(small_wmv)
    flat2 = lambda a: a.reshape(1, -1) if a.ndim == 1 else (a.transpose(1, 0, 2) if a.ndim == 3 else a)
    unflat = lambda o, like: o.transpose(1, 0, 2) if like.ndim == 3 else o.reshape(like.shape)
    small_out = adamw_small([(flat2(small_wmv[nm][0]), flat2(grads[nm]), flat2(small_wmv[nm][1]),
                              flat2(small_wmv[nm][2])) for nm in small_names])
    small = {nm: [unflat(o, small_wmv[nm][0]) for o in outs] for nm, outs in zip(small_names, small_out)}

    order = ["norm_mix", "w_in", "pool_w", "pool_scale", "w_pool_proj", "conv_w", "w_conv_out", "w_o", "norm_ffn",
             "w_up", "ffn_conv_w", "ffn_conv_b", "w_down", "norm_final"]
    out = [loss, grad_x.reshape(nb, seq, D)]
    out += [big[nm][0] if nm in big else grads[nm] for nm in order]
    for idx in range(3):
        out += [big[nm][idx + 1] if nm in big else small[nm][idx] for nm in order]
    return tuple(out)
```

```python
import jax
import jax.numpy as jnp
from jax import lax
from jax.experimental import pallas as pl
from jax.experimental.pallas import tpu as pltpu
from jax.experimental.pallas import tpu_sc as plsc

F32 = jnp.float32
BF16 = jnp.bfloat16

NDEV = 8
D = 1024
NG = 4
CG = 256
WINS = (2, 4, 8, 16)
DIN = 6 * D
SH_IN = DIN // NDEV
NZT = DIN // CG
FF2 = 5632
SH_UP = FF2 // NDEV
FF = FF2 // 2
NCH = 4
SH_DN = FF // NDEV
RMS_EPS = 1e-6
HALO = 16

ADAM_LR = 0.001
ADAM_B1 = 0.9
ADAM_B2 = 0.999
ADAM_EPS = 1e-08
ADAM_WD = 0.01
ADAM_STEP = 10

TM_IN = 512
TM_BWD_IN = 256
TM_MIX = 256
TM_FFN = 256
TK_WGRAD = 1024
MIX_POOL_PROJ, MIX_CONV_OUT, MIX_O = 0, 1, 2
MIX_COLS = 3 * D + CG
VMEM_BIG = 56 * 1024 * 1024
MESH = pl.DeviceIdType.MESH
ANY = pl.BlockSpec(memory_space=pl.ANY)


def _cparams(n_axes, vmem=None):
    return pltpu.CompilerParams(dimension_semantics=("arbitrary",) * n_axes, vmem_limit_bytes=vmem)


def _dot(a, b):
    return jnp.dot(a, b, preferred_element_type=F32)


def _dot_nt(a, b):
    return lax.dot_general(a, b, (((1,), (1,)), ((), ())), preferred_element_type=F32)


def _dot_tn(a, b):
    return lax.dot_general(a, b, (((0,), (0,)), ((), ())), preferred_element_type=F32)


def _shift_down(ext, s, lead):
    return pltpu.roll(ext, s, 0)[lead:]


def _shift_up(ext, s, tm):
    n = ext.shape[0]
    return pltpu.roll(ext, n - s, 0)[:tm]


def _rms_inv(x):
    return lax.rsqrt(jnp.mean(x * x, axis=-1, keepdims=True) + RMS_EPS)


def _rms_bwd(dh, xn, inv, g):
    dxn = dh * g
    return inv * (dxn - xn * jnp.mean(dxn * xn, axis=-1, keepdims=True))


def _pos():
    return lax.axis_index("x"), lax.axis_index("y"), lax.axis_index("c")


def _handshake(peers):
    barrier = pltpu.get_barrier_semaphore()
    for peer in peers:
        pl.semaphore_signal(barrier, inc=1, device_id=peer, device_id_type=MESH)
    pl.semaphore_wait(barrier, len(peers))


def _sequencer(body, out_type, n_sems, name, collective_id):
    return pl.kernel(
        body, out_type=out_type, mesh=plsc.ScalarSubcoreMesh(axis_name="sequencer", num_cores=1), name=name,
        scratch_types=[pltpu.SemaphoreType.DMA((n_sems,)), pltpu.SemaphoreType.DMA((n_sems,))],
        compiler_params=pltpu.CompilerParams(collective_id=collective_id))


def all_gather_blocks(shards, name, collective_id):
    n = len(shards)

    def body(*refs):
        ins, outs = refs[:n], refs[n:2 * n]
        send_sems, recv_sems = refs[2 * n:]
        x, y, c = _pos()
        me, sibling = (x, y, c), (x, y, 1 - c)
        first_chip, second_chip, diagonal = (x ^ (1 - c), y ^ c), (x ^ c, y ^ (1 - c)), (1 - x, 1 - y)
        first, second = (*first_chip, c), (*second_chip, c)
        _handshake([sibling, first, second])

        def copy(w, k, block, to, src=None):
            slot = outs[w].at[4 * block[0] + 2 * block[1] + block[2]]
            return pltpu.make_async_remote_copy(
                src_ref=slot if src is None else src, dst_ref=slot,
                send_sem=send_sems.at[8 * w + k], recv_sem=recv_sems.at[8 * w + k], device_id=to, device_id_type=MESH)

        mine, sent = [], []
        for w in range(n):
            m = pltpu.make_async_copy(ins[w], outs[w].at[4 * x + 2 * y + c], send_sems.at[8 * w + 7])
            m.start()
            mine.append(m)
            sent += [copy(w, k, me, to, src=ins[w]) for k, to in enumerate((sibling, first, second))]
        for cp in sent:
            cp.start()
        for k, chip in ((1, first_chip), (2, second_chip), (3, diagonal)):
            for w in range(n):
                copy(w, k, (*chip, c), me).wait_recv()
                onward = [copy(w, 3 + k, (*chip, c), sibling)] + ([copy(w, 3, (*chip, c), second)] if k == 1 else [])
                for cp in onward:
                    cp.start()
                sent += onward
        for w in range(n):
            copy(w, 0, sibling, me).wait_recv()
            for k, chip in ((4, second_chip), (5, first_chip), (6, diagonal)):
                copy(w, k, (*chip, 1 - c), me).wait_recv()
        for cp in sent:
            cp.wait_send()
        for m in mine:
            m.wait()

    out = [jax.ShapeDtypeStruct((NDEV,) + s.shape, s.dtype) for s in shards]
    return _sequencer(body, out, 8 * n, name, collective_id)(*shards)


def _exchange_small(v, name):
    rows = v.shape[0]

    def body(v_ref, out_ref, slots, send_sems, recv_sems, local_sem):
        x, y, c = _pos()
        me = 4 * x + 2 * y + c
        mine = pltpu.make_async_copy(v_ref, slots.at[me], local_sem)
        mine.start()
        offs = [(dx, dy, dc) for dx in (0, 1) for dy in (0, 1) for dc in (0, 1)][1:]

        def copy(k, src_slot, to):
            return pltpu.make_async_remote_copy(
                src_ref=v_ref, dst_ref=slots.at[src_slot], send_sem=send_sems.at[k], recv_sem=recv_sems.at[k],
                device_id=to, device_id_type=MESH)

        sends = []
        for k, (dx, dy, dc) in enumerate(offs):
            cp = copy(k, me, (x ^ dx, y ^ dy, c ^ dc))
            cp.start()
            sends.append(cp)
        for k, (dx, dy, dc) in enumerate(offs):
            copy(k, 4 * (x ^ dx) + 2 * (y ^ dy) + (c ^ dc), (x, y, c)).wait_recv()
        for cp in sends:
            cp.wait_send()
        mine.wait()
        out_ref[...] = slots[...]

    vmem = pl.BlockSpec(memory_space=pltpu.VMEM)
    return pl.pallas_call(
        body, name=name, out_shape=jax.ShapeDtypeStruct((NDEV, rows, D), F32), in_specs=[vmem], out_specs=vmem,
        scratch_shapes=[pltpu.VMEM((NDEV, rows, D), F32), pltpu.SemaphoreType.DMA((7,)),
                        pltpu.SemaphoreType.DMA((7,)), pltpu.SemaphoreType.DMA],
    )(v)


def sum_blocks(gathered, name):
    def body(*refs):
        for g_ref, o_ref in zip(refs[:len(gathered)], refs[len(gathered):]):
            acc = g_ref[0]
            for d in range(1, NDEV):
                acc = acc + g_ref[d]
            o_ref[...] = acc

    vmem = pl.BlockSpec(memory_space=pltpu.VMEM)
    return pl.pallas_call(
        body, name=name, out_shape=[jax.ShapeDtypeStruct(g.shape[1:], F32) for g in gathered],
        in_specs=[vmem] * len(gathered), out_specs=[vmem] * len(gathered),
    )(*gathered)


def reduce_scatter_d2d(grads, name, collective_id):
    n = len(grads)

    def body(*refs):
        ins, outs = refs[:n], refs[n:2 * n]
        send_sems, recv_sems = refs[2 * n:]
        x, y, c = _pos()
        _handshake([(x, y, 1 - c)])
        cps = []
        for w in range(n):
            cp = pltpu.make_async_remote_copy(
                src_ref=ins[w].at[1 - c], dst_ref=outs[w], send_sem=send_sems.at[w], recv_sem=recv_sems.at[w],
                device_id=(x, y, 1 - c), device_id_type=MESH)
            cp.start()
            cps.append(cp)
        for cp in cps:
            cp.wait_recv()
        for cp in cps:
            cp.wait_send()

    out = [jax.ShapeDtypeStruct(g.shape[1:], F32) for g in grads]
    return _sequencer(body, out, n, name, collective_id)(*grads)


def reduce_scatter_ici(parts, name, collective_id):
    n = len(parts)

    def body(*refs):
        ins, outs = refs[:n], refs[n:2 * n]
        send_sems, recv_sems = refs[2 * n:]
        x, y, c = _pos()
        offs = [(1, 0), (0, 1), (1, 1)]
        _handshake([(x ^ dx, y ^ dy, c) for dx, dy in offs])
        cps = []
        for w in range(n):
            for k, (dx, dy) in enumerate(offs):
                ox, oy = x ^ dx, y ^ dy
                cp = pltpu.make_async_remote_copy(
                    src_ref=ins[w].at[2 * ox + oy], dst_ref=outs[w].at[2 * x + y],
                    send_sem=send_sems.at[3 * w + k], recv_sem=recv_sems.at[3 * w + k],
                    device_id=(ox, oy, c), device_id_type=MESH)
                cp.start()
                cps.append((cp, w, k, ox, oy))
        for cp, w, k, ox, oy in cps:
            pltpu.make_async_remote_copy(
                src_ref=ins[w].at[2 * ox + oy], dst_ref=outs[w].at[2 * ox + oy],
                send_sem=send_sems.at[3 * w + k], recv_sem=recv_sems.at[3 * w + k],
                device_id=(ox, oy, c), device_id_type=MESH).wait_recv()
        for cp, *_ in cps:
            cp.wait_send()

    out = [jax.ShapeDtypeStruct(p.shape, BF16) for p in parts]
    return _sequencer(body, out, 3 * n, name, collective_id)(*parts)


def fwd_in(x, g1, w_in_pieces, tm):
    t = x.shape[0]
    tile = lambda i: (i, 0)
    w_spec = pl.BlockSpec((NDEV, D, CG), lambda i: (0, 0, 0))
    z_spec = pl.BlockSpec((NDEV, tm, CG), lambda i: (0, i, 0))
    z_shape = jax.ShapeDtypeStruct((NDEV, t, CG), BF16)

    def cost(other_bytes, transcendentals):
        return pl.CostEstimate(flops=2 * t * D * NDEV * CG, transcendentals=transcendentals,
                               bytes_accessed=other_bytes + 2 * D * NDEV * CG + 2 * t * NDEV * CG)

    def first(x_ref, g_ref, w_ref, z_ref, h_ref, ht_ref):
        xf = x_ref[...]
        h = (xf * _rms_inv(xf) * g_ref[...]).astype(BF16)
        h_ref[...] = h
        ht_ref[...] = h.T
        for j in range(NDEV):
            z_ref[j] = _dot(h, w_ref[j]).astype(BF16)

    z0, h, ht = pl.pallas_call(
        first, name="fwd_in_0", grid=(t // tm,),
        in_specs=[pl.BlockSpec((tm, D), tile), pl.BlockSpec((1, D), lambda i: (0, 0)), w_spec],
        out_specs=[z_spec, pl.BlockSpec((tm, D), tile), pl.BlockSpec((D, tm), lambda i: (0, i))],
        out_shape=[z_shape, jax.ShapeDtypeStruct((t, D), BF16), jax.ShapeDtypeStruct((D, t), BF16)],
        compiler_params=_cparams(1, VMEM_BIG), cost_estimate=cost(8 * t * D + 4 * D, t),
    )(x, g1, w_in_pieces[0])
    zs = [z0]
    for q in (1, 2):
        h, zs[-1] = lax.optimization_barrier((h, zs[-1]))

        def later(h_ref, w_ref, z_ref):
            hb = h_ref[...]
            for j in range(NDEV):
                z_ref[j] = _dot(hb, w_ref[j]).astype(BF16)

        zs.append(pl.pallas_call(
            later, name=f"fwd_in_{q}", grid=(t // tm,),
            in_specs=[pl.BlockSpec((tm, D), tile), w_spec], out_specs=z_spec, out_shape=z_shape,
            compiler_params=_cparams(1, VMEM_BIG), cost_estimate=cost(2 * t * D, 0),
        )(h, w_in_pieces[q]))
    return zs, ht


def _z_tile(z_refs, n):
    return z_refs[n % 3][n // 3]


def _pool_tile(z, zh, win, keep_hist, cnt):
    zt = z.astype(F32)
    ext = jnp.concatenate([zh.astype(F32) * keep_hist, zt], axis=0)
    s, sh = ext, 1
    while sh < win:
        s = s + pltpu.roll(s, sh, 0)
        sh *= 2
    return s[HALO:] / cnt - zt


def _conv_taps(ext, cur, w_ref, lanes, lead):
    x1 = _shift_down(ext, 1, lead)
    x2 = _shift_down(ext, 2, lead)
    out = w_ref[2:3, lanes] * cur + w_ref[1:2, lanes] * x1 + w_ref[0:1, lanes] * x2
    return out, x1, x2


def fwd_mix(zs, x, pool_w, pool_scale, conv_w, wmix, tm, seq):
    t = x.shape[0]
    tps = seq // tm
    hb = tm // HALO

    def body(z0_ref, z1_ref, z2_ref, zh0_ref, zh1_ref, zh2_ref, x_ref, pw_ref, ps_ref, wpp_ref, cw_ref, wco_ref,
             wo_ref, x1_ref, yp_ref, yc_ref):
        z_refs, zh_refs = (z0_ref, z1_ref, z2_ref), (zh0_ref, zh1_ref, zh2_ref)
        it = pl.program_id(0) % tps
        keep_hist = jnp.where(it == 0, 0.0, 1.0)
        pos = it * tm + lax.broadcasted_iota(jnp.int32, (tm, 1), 0)
        p2 = []
        for g, win in enumerate(WINS):
            cnt = jnp.minimum(pos + 1, win).astype(F32)
            p = _pool_tile(_z_tile(z_refs, g), _z_tile(zh_refs, g), win, keep_hist, cnt)
            lanes = slice(g * CG, (g + 1) * CG)
            p2.append((_dot(p.astype(BF16), pw_ref[g]) * ps_ref[:, lanes]).astype(BF16))
        y_pool = _dot(jnp.concatenate(p2, axis=1), wpp_ref[...])
        u = []
        for q in range(NG):
            lanes = slice(q * CG, (q + 1) * CG)
            cv = _z_tile(z_refs, 8 + q).astype(F32) * _z_tile(z_refs, 12 + q).astype(F32)
            cvh = _z_tile(zh_refs, 8 + q).astype(F32) * _z_tile(zh_refs, 12 + q).astype(F32) * keep_hist
            cc, _, _ = _conv_taps(jnp.concatenate([cvh, cv], axis=0), cv, cw_ref, lanes, HALO)
            u.append((_z_tile(z_refs, 4 + q).astype(F32) * cc).astype(BF16))
        y_conv = _dot(jnp.concatenate(u, axis=1), wco_ref[...])
        ypb, ycb = y_pool.astype(BF16), y_conv.astype(BF16)
        yp_ref[...] = ypb
        yc_ref[...] = ycb
        merged = []
        for q in range(NG):
            lanes = slice(q * CG, (q + 1) * CG)
            sp = jax.nn.sigmoid(_z_tile(z_refs, 16 + q).astype(F32))
            sc = jax.nn.sigmoid(_z_tile(z_refs, 20 + q).astype(F32))
            merged.append((sp * ypb[:, lanes].astype(F32) + sc * ycb[:, lanes].astype(F32)).astype(BF16))
        x1_ref[...] = x_ref[...] + _dot(jnp.concatenate(merged, axis=1), wo_ref[...])

    def hist(i):
        return jnp.maximum(i * hb - 1, 0)

    const2 = lambda i: (0, 0)
    return pl.pallas_call(
        body, name="fwd_mix", grid=(t // tm,),
        in_specs=[pl.BlockSpec((NDEV, tm, CG), lambda i: (0, i, 0))] * 3
                 + [pl.BlockSpec((NDEV, HALO, CG), lambda i: (0, hist(i), 0))] * 3
                 + [pl.BlockSpec((tm, D), lambda i: (i, 0)),
                    pl.BlockSpec((NG, CG, CG), lambda i: (0, 0, 0)), pl.BlockSpec((1, D), const2),
                    pl.BlockSpec((D, D), lambda i: (0, MIX_POOL_PROJ)), pl.BlockSpec((3, D), const2),
                    pl.BlockSpec((D, D), lambda i: (0, MIX_CONV_OUT)), pl.BlockSpec((D, D), lambda i: (0, MIX_O))],
        out_specs=[pl.BlockSpec((tm, D), lambda i: (i, 0))] * 3,
        out_shape=[jax.ShapeDtypeStruct((t, D), F32), jax.ShapeDtypeStruct((t, D), BF16),
                   jax.ShapeDtypeStruct((t, D), BF16)],
        compiler_params=_cparams(1, VMEM_BIG),
    )(*zs, *zs, x, pool_w, pool_scale, wmix, conv_w, wmix, wmix)


def fwd_up(x1, g2, w_up_g, fcw, fcb, tm, seq):
    t = x1.shape[0]
    tps = seq // tm

    def body(x1_ref, g2_ref, wup_ref, fcw_ref, fcb_ref, up_ref, pre_ref, act_ref, actt_ref, h2t_ref, hist_ref):
        i = pl.program_id(0)
        keep_hist = jnp.where(i % tps == 0, 0.0, 1.0)

        @pl.when(i == 0)
        def _():
            hist_ref[...] = jnp.zeros_like(hist_ref)

        x1v = x1_ref[...]
        h2 = (x1v * _rms_inv(x1v) * g2_ref[...]).astype(BF16)
        h2t_ref[...] = h2.T
        lanes = slice(0, SH_UP)
        for c in range(NCH):
            conv = []
            for s in range(2):
                ub = _dot(h2, wup_ref[s, c]).astype(BF16)
                up_ref[s, c] = ub
                uf = ub.astype(F32)
                ext = jnp.concatenate([hist_ref[s, c] * keep_hist, uf], axis=0)
                hist_ref[s, c] = uf[tm - 8:]
                cc, _, _ = _conv_taps(ext, uf, fcw_ref.at[s, c], lanes, 8)
                conv.append(cc + fcb_ref[s, c])
                pre_ref[s, c] = conv[s].astype(BF16)
            a = (conv[0] * jax.nn.sigmoid(conv[0]) * conv[1]).astype(BF16)
            act_ref[c] = a
            actt_ref[c] = a.T

    tile = lambda i: (i, 0)
    const2 = lambda i: (0, 0)
    whole = lambda i: (0, 0, 0, 0)
    chunks = pl.BlockSpec((2, NCH, tm, SH_UP), lambda i: (0, 0, i, 0))
    return pl.pallas_call(
        body, name="fwd_up", grid=(t // tm,),
        in_specs=[pl.BlockSpec((tm, D), tile), pl.BlockSpec((1, D), const2),
                  pl.BlockSpec((2, NCH, D, SH_UP), whole), pl.BlockSpec((2, NCH, 3, SH_UP), whole),
                  pl.BlockSpec((2, NCH, 1, SH_UP), whole)],
        out_specs=[chunks, chunks, pl.BlockSpec((NCH, tm, SH_UP), lambda i: (0, i, 0)),
                   pl.BlockSpec((NCH, SH_UP, tm), lambda i: (0, 0, i)), pl.BlockSpec((D, tm), lambda i: (0, i))],
        out_shape=[jax.ShapeDtypeStruct((2, NCH, t, SH_UP), BF16), jax.ShapeDtypeStruct((2, NCH, t, SH_UP), BF16),
                   jax.ShapeDtypeStruct((NCH, t, SH_UP), BF16), jax.ShapeDtypeStruct((NCH, SH_UP, t), BF16),
                   jax.ShapeDtypeStruct((D, t), BF16)],
        scratch_shapes=[pltpu.VMEM((2, NCH, 8, SH_UP), F32)],
        compiler_params=_cparams(1, VMEM_BIG),
    )(x1, g2, w_up_g.reshape(2, NCH, D, SH_UP), fcw.reshape(2, NCH, 3, SH_UP), fcb.reshape(2, NCH, 1, SH_UP))


def fwd_down(x1, act, w_dn, gf, tgt, tm):
    t = x1.shape[0]

    def body(x1_ref, act_ref, wdn_ref, gf_ref, tgt_ref, dx2_ref, dx2b_ref, vec_ref):
        @pl.when(pl.program_id(0) == 0)
        def _():
            vec_ref[...] = jnp.zeros_like(vec_ref)

        d = None
        for c in range(NCH):
            part = _dot(act_ref[c], wdn_ref[c])
            d = part if d is None else d + part
        x2 = x1_ref[...] + d
        inv3 = _rms_inv(x2)
        xn = x2 * inv3
        diff = xn * gf_ref[...] - tgt_ref[...]
        dy = diff * (1.0 / D)
        vec_ref[0:1, :] += jnp.sum(dy * xn, axis=0, keepdims=True)
        vec_ref[1:2, :] += 0.5 * jnp.sum(jnp.mean(diff * diff, axis=-1))
        dx2 = _rms_bwd(dy, xn, inv3, gf_ref[...])
        dx2_ref[...] = dx2
        dx2b_ref[...] = dx2.astype(BF16)

    tile = lambda i: (i, 0)
    const2 = lambda i: (0, 0)
    return pl.pallas_call(
        body, name="fwd_down", grid=(t // tm,),
        in_specs=[pl.BlockSpec((tm, D), tile), pl.BlockSpec((NCH, tm, SH_UP), lambda i: (0, i, 0)),
                  pl.BlockSpec((NCH, SH_UP, D), lambda i: (0, 0, 0)), pl.BlockSpec((1, D), const2),
                  pl.BlockSpec((tm, D), tile)],
        out_specs=[pl.BlockSpec((tm, D), tile), pl.BlockSpec((tm, D), tile), pl.BlockSpec((8, D), const2)],
        out_shape=[jax.ShapeDtypeStruct((t, D), F32), jax.ShapeDtypeStruct((t, D), BF16),
                   jax.ShapeDtypeStruct((8, D), F32)],
        compiler_params=_cparams(1, VMEM_BIG),
    )(x1, act, w_dn, gf, tgt)


def bwd_ffn(dx2, x1, g2, up, pre, w_up_g, fcw, w_dn, tm, seq):
    t = x1.shape[0]
    nt = t // tm
    tps = seq // tm

    def body(dx2_ref, x1_ref, g2_ref, up_ref, pre_ref, wup_ref, fcw_ref, wdn_ref,
             dup_ref, dx1_ref, gvec_ref, gn_ref, carry_ref):
        i = pl.program_id(0)
        it = (nt - 1 - i) % tps
        keep_next = jnp.where(it == tps - 1, 0.0, 1.0)

        @pl.when(i == 0)
        def _():
            gvec_ref[...] = jnp.zeros_like(gvec_ref)
            gn_ref[...] = jnp.zeros_like(gn_ref)
            carry_ref[...] = jnp.zeros_like(carry_ref)

        dx2v = dx2_ref[...]
        dxb = dx2v.astype(BF16)
        lanes = slice(0, SH_UP)
        dh2 = None
        for c in range(NCH):
            pre = [pre_ref[s, c].astype(F32) for s in range(2)]
            sg = jax.nn.sigmoid(pre[0])
            dact = _dot_nt(dxb, wdn_ref[c])
            dpre = [dact * pre[1] * (sg * (1.0 + pre[0] * (1.0 - sg))), dact * (pre[0] * sg)]
            for s in range(2):
                dc = dpre[s]
                ext = jnp.concatenate([dc, carry_ref[s, c] * keep_next], axis=0)
                carry_ref[s, c] = dc[:8]
                shifted = (_shift_up(ext, 2, tm), _shift_up(ext, 1, tm), dc)
                uf = up_ref[s, c].astype(F32)
                gvec_ref[s, c, 0:1, lanes] += jnp.sum(dc, axis=0, keepdims=True)
                for tap in range(3):
                    gvec_ref[s, c, tap + 1:tap + 2, lanes] += jnp.sum(shifted[tap] * uf, axis=0, keepdims=True)
                w = fcw_ref.at[s, c]
                du = w[2:3, :] * dc + w[1:2, :] * shifted[1] + w[0:1, :] * shifted[0]
                dub = du.astype(BF16)
                dup_ref[s, c] = dub
                part = _dot_nt(dub, wup_ref[s, c])
                dh2 = part if dh2 is None else dh2 + part
        x1v = x1_ref[...]
        inv2 = _rms_inv(x1v)
        xn = x1v * inv2
        gn_ref[0:1, :] += jnp.sum(dh2 * xn, axis=0, keepdims=True)
        dx1_ref[...] = dx2v + _rms_bwd(dh2, xn, inv2, g2_ref[...])

    rev = lambda i: (nt - 1 - i, 0)
    const2 = lambda i: (0, 0)
    whole = lambda i: (0, 0, 0, 0)
    chunks = pl.BlockSpec((2, NCH, tm, SH_UP), lambda i: (0, 0, nt - 1 - i, 0))
    return pl.pallas_call(
        body, name="bwd_ffn", grid=(nt,),
        in_specs=[pl.BlockSpec((tm, D), rev), pl.BlockSpec((tm, D), rev), pl.BlockSpec((1, D), const2),
                  chunks, chunks, pl.BlockSpec((2, NCH, D, SH_UP), whole), pl.BlockSpec((2, NCH, 3, SH_UP), whole),
                  pl.BlockSpec((NCH, SH_UP, D), lambda i: (0, 0, 0))],
        out_specs=[chunks, pl.BlockSpec((tm, D), rev), pl.BlockSpec((2, NCH, 8, D), whole),
                   pl.BlockSpec((8, D), const2)],
        out_shape=[jax.ShapeDtypeStruct((2, NCH, t, SH_UP), BF16), jax.ShapeDtypeStruct((t, D), F32),
                   jax.ShapeDtypeStruct((2, NCH, 8, D), F32), jax.ShapeDtypeStruct((8, D), F32)],
        scratch_shapes=[pltpu.VMEM((2, NCH, 8, SH_UP), F32)],
        compiler_params=_cparams(1, VMEM_BIG),
    )(dx2, x1, g2, up, pre, w_up_g.reshape(2, NCH, D, SH_UP), fcw.reshape(2, NCH, 3, SH_UP), w_dn)


def bwd_mix(dx1, zs, y_pool, y_conv, pool_w, pool_scale, conv_w, wmix, tm, seq):
    t = dx1.shape[0]
    nt = t // tm
    tps = seq // tm
    hb = tm // HALO

    def body(da_ref, z0_ref, z1_ref, z2_ref, zh0_ref, zh1_ref, zh2_ref, yp_ref, yc_ref, pw_ref, ps_ref, wpp_ref,
             cw_ref, wco_ref, wo_ref,
             dz_ref, mg_ref, p2_ref, u_ref, dyp_ref, dyc_ref, p_ref, dpw_ref, gvec_ref, cp_ref, cc_ref):
        z_refs, zh_refs = (z0_ref, z1_ref, z2_ref), (zh0_ref, zh1_ref, zh2_ref)
        i = pl.program_id(0)
        it = (nt - 1 - i) % tps
        keep_hist = jnp.where(it == 0, 0.0, 1.0)
        keep_next = jnp.where(it == tps - 1, 0.0, 1.0)
        pos = it * tm + lax.broadcasted_iota(jnp.int32, (tm, 1), 0)

        @pl.when(i == 0)
        def _():
            gvec_ref[...] = jnp.zeros_like(gvec_ref)
            cp_ref[...] = jnp.zeros_like(cp_ref)
            cc_ref[...] = jnp.zeros_like(cc_ref)

        dm = _dot_nt(da_ref[...].astype(BF16), wo_ref[...])
        merged, dyp, dyc = [], [], []
        for q in range(NG):
            lanes = slice(q * CG, (q + 1) * CG)
            sp = jax.nn.sigmoid(_z_tile(z_refs, 16 + q).astype(F32))
            sc = jax.nn.sigmoid(_z_tile(z_refs, 20 + q).astype(F32))
            yp = yp_ref[:, lanes].astype(F32)
            yc = yc_ref[:, lanes].astype(F32)
            dmq = dm[:, lanes]
            merged.append((sp * yp + sc * yc).astype(BF16))
            dyp.append((dmq * sp).astype(BF16))
            dyc.append((dmq * sc).astype(BF16))
            dz_ref[16 + q] = (dmq * yp * (sp * (1.0 - sp))).astype(BF16)
            dz_ref[20 + q] = (dmq * yc * (sc * (1.0 - sc))).astype(BF16)
        mg_ref[...] = jnp.concatenate(merged, axis=1)
        dypb = jnp.concatenate(dyp, axis=1)
        dycb = jnp.concatenate(dyc, axis=1)
        dyp_ref[...] = dypb
        dyc_ref[...] = dycb

        dp2 = _dot_nt(dypb, wpp_ref[...])
        p2 = []
        for g, win in enumerate(WINS):
            lanes = slice(g * CG, (g + 1) * CG)
            cnt = jnp.minimum(pos + 1, win).astype(F32)
            p = _pool_tile(_z_tile(z_refs, g), _z_tile(zh_refs, g), win, keep_hist, cnt)
            pb = p.astype(BF16)
            p_ref[g] = pb
            pw = _dot(pb, pw_ref[g])
            p2.append((pw * ps_ref[:, lanes]).astype(BF16))
            dp2g = dp2[:, lanes]
            gvec_ref[0:1, lanes] += jnp.sum(dp2g * pw, axis=0, keepdims=True)
            dpwb = (dp2g * ps_ref[:, lanes]).astype(BF16)
            dpw_ref[g] = dpwb
            dp = _dot_nt(dpwb, pw_ref[g])
            qv = dp / cnt
            ext = jnp.concatenate([qv, cp_ref[g] * keep_next], axis=0)
            cp_ref[g] = qv[:HALO]
            n = tm + HALO
            s, sh = ext, 1
            while sh < win:
                s = s + pltpu.roll(s, n - sh, 0)
                sh *= 2
            dz_ref[g] = (s[:tm] - dp).astype(BF16)
        p2_ref[...] = jnp.concatenate(p2, axis=1)

        du = _dot_nt(dycb, wco_ref[...])
        u = []
        for q in range(NG):
            lanes = slice(q * CG, (q + 1) * CG)
            zb = _z_tile(z_refs, 4 + q).astype(F32)
            zc = _z_tile(z_refs, 8 + q).astype(F32)
            zv = _z_tile(z_refs, 12 + q).astype(F32)
            cv = zc * zv
            cvh = _z_tile(zh_refs, 8 + q).astype(F32) * _z_tile(zh_refs, 12 + q).astype(F32) * keep_hist
            cc, cv1, cv2 = _conv_taps(jnp.concatenate([cvh, cv], axis=0), cv, cw_ref, lanes, HALO)
            u.append((zb * cc).astype(BF16))
            duq = du[:, lanes]
            dz_ref[4 + q] = (duq * cc).astype(BF16)
            dcc = duq * zb
            for tap, src in enumerate((cv2, cv1, cv)):
                gvec_ref[tap + 1:tap + 2, lanes] += jnp.sum(dcc * src, axis=0, keepdims=True)
            ext = jnp.concatenate([dcc, cc_ref[:, lanes] * keep_next], axis=0)
            cc_ref[:, lanes] = dcc[:8]
            dcv = (cw_ref[2:3, lanes] * dcc + cw_ref[1:2, lanes] * _shift_up(ext, 1, tm)
                   + cw_ref[0:1, lanes] * _shift_up(ext, 2, tm))
            dz_ref[8 + q] = (dcv * zv).astype(BF16)
            dz_ref[12 + q] = (dcv * zc).astype(BF16)
        u_ref[...] = jnp.concatenate(u, axis=1)

    def hist(i):
        return jnp.maximum((nt - 1 - i) * hb - 1, 0)

    rev = lambda i: (nt - 1 - i, 0)
    rev3 = lambda i: (0, nt - 1 - i, 0)
    const2 = lambda i: (0, 0)
    tok = jax.ShapeDtypeStruct((t, D), BF16)
    grp = jax.ShapeDtypeStruct((NG, t, CG), BF16)
    return pl.pallas_call(
        body, name="bwd_mix", grid=(nt,),
        in_specs=[pl.BlockSpec((tm, D), rev)] + [pl.BlockSpec((NDEV, tm, CG), rev3)] * 3
                 + [pl.BlockSpec((NDEV, HALO, CG), lambda i: (0, hist(i), 0))] * 3
                 + [pl.BlockSpec((tm, D), rev), pl.BlockSpec((tm, D), rev),
                    pl.BlockSpec((NG, CG, CG), lambda i: (0, 0, 0)), pl.BlockSpec((1, D), const2),
                    pl.BlockSpec((D, D), lambda i: (0, MIX_POOL_PROJ)), pl.BlockSpec((3, D), const2),
                    pl.BlockSpec((D, D), lambda i: (0, MIX_CONV_OUT)), pl.BlockSpec((D, D), lambda i: (0, MIX_O))],
        out_specs=[pl.BlockSpec((NZT, tm, CG), rev3)] + [pl.BlockSpec((tm, D), rev)] * 5
                  + [pl.BlockSpec((NG, tm, CG), rev3)] * 2 + [pl.BlockSpec((8, D), const2)],
        out_shape=[jax.ShapeDtypeStruct((NZT, t, CG), BF16), tok, tok, tok, tok, tok, grp, grp,
                   jax.ShapeDtypeStruct((8, D), F32)],
        scratch_shapes=[pltpu.VMEM((NG, HALO, CG), F32), pltpu.VMEM((8, D), F32)],
        compiler_params=_cparams(1, VMEM_BIG),
    )(dx1, *zs, *zs, y_pool, y_conv, pool_w, pool_scale, wmix, conv_w, wmix, wmix)


def bwd_in(dz, w_in_pieces, dx1, x, g1, tm):
    t = x.shape[0]

    def body(dz_ref, w0_ref, w1_ref, w2_ref, dx1_ref, x_ref, g_ref, gx_ref, gn_ref):
        @pl.when(pl.program_id(0) == 0)
        def _():
            gn_ref[...] = jnp.zeros_like(gn_ref)

        dh = None
        for j in range(NDEV):
            for q, w_ref in enumerate((w0_ref, w1_ref, w2_ref)):
                part = _dot_nt(dz_ref[3 * j + q], w_ref[j])
                dh = part if dh is None else dh + part
        xv = x_ref[...]
        inv = _rms_inv(xv)
        xn = xv * inv
        gn_ref[0:1, :] += jnp.sum(dh * xn, axis=0, keepdims=True)
        gx_ref[...] = dx1_ref[...] + _rms_bwd(dh, xn, inv, g_ref[...])

    tile = lambda i: (i, 0)
    return pl.pallas_call(
        body, name="bwd_in", grid=(t // tm,),
        in_specs=[pl.BlockSpec((NZT, tm, CG), lambda i: (0, i, 0))]
                 + [pl.BlockSpec((NDEV, D, CG), lambda i: (0, 0, 0))] * 3
                 + [pl.BlockSpec((tm, D), tile), pl.BlockSpec((tm, D), tile), pl.BlockSpec((1, D), lambda i: (0, 0))],
        out_specs=[pl.BlockSpec((tm, D), tile), pl.BlockSpec((8, D), lambda i: (0, 0))],
        out_shape=[jax.ShapeDtypeStruct((t, D), F32), jax.ShapeDtypeStruct((8, D), F32)],
        compiler_params=_cparams(1, VMEM_BIG),
    )(dz, *w_in_pieces, dx1, x, g1)


def _slot(j):
    return j % 2, j // 2


def wgrad_cols(at, b, q, name):
    m, t = at.shape
    width = b.shape[3]

    def body(a_ref, b_ref, o_ref):
        o_ref[...] = _dot(a_ref[...], b_ref[...])

    return pl.pallas_call(
        body, name=name, grid=(NDEV,),
        in_specs=[pl.BlockSpec((m, t), lambda j: (0, 0)),
                  pl.BlockSpec((None, None, t, width), lambda j: (j, q, 0, 0))],
        out_specs=pl.BlockSpec((None, None, m, width), lambda j: (j % 2, j // 2, 0, 0)),
        out_shape=jax.ShapeDtypeStruct((2, 4, m, width), F32),
        compiler_params=_cparams(1, VMEM_BIG),
    )(at, b)


def wgrad_down(actt, dx2b):
    t = dx2b.shape[0]

    def body(a_ref, b_ref, o_ref):
        r = _dot(a_ref[...], b_ref[...])
        o_ref[0] = r[:SH_DN]
        o_ref[1] = r[SH_DN:]

    return pl.pallas_call(
        body, name="wgrad_down", grid=(NCH,),
        in_specs=[pl.BlockSpec((None, SH_UP, t), lambda k: (k, 0, 0)), pl.BlockSpec((t, D), lambda k: (0, 0))],
        out_specs=pl.BlockSpec((2, None, SH_DN, D), lambda k: (0, k, 0, 0)),
        out_shape=jax.ShapeDtypeStruct((2, 4, SH_DN, D), F32),
        compiler_params=_cparams(1, VMEM_BIG),
    )(actt, dx2b)


def wgrad_square(a, b, name, tk):
    t = a.shape[0]

    def body(a_ref, b_ref, o_ref, acc_ref):
        kt = pl.program_id(0)

        @pl.when(kt == 0)
        def _():
            acc_ref[...] = jnp.zeros_like(acc_ref)

        acc_ref[...] += _dot_tn(a_ref[...], b_ref[...].astype(BF16))

        @pl.when(kt == pl.num_programs(0) - 1)
        def _():
            for j in range(NDEV):
                cc, xy = _slot(j)
                o_ref[cc, xy] = acc_ref[j * 128:(j + 1) * 128]

    return pl.pallas_call(
        body, name=name, grid=(t // tk,),
        in_specs=[pl.BlockSpec((tk, D), lambda k: (k, 0)), pl.BlockSpec((tk, D), lambda k: (k, 0))],
        out_specs=pl.BlockSpec((2, 4, 128, D), lambda k: (0, 0, 0, 0)),
        out_shape=jax.ShapeDtypeStruct((2, 4, 128, D), F32),
        scratch_shapes=[pltpu.VMEM((D, D), F32)],
        compiler_params=_cparams(1, VMEM_BIG),
    )(a, b)


def wgrad_pool(p, dpw, tk):
    t = p.shape[1]

    def body(a_ref, b_ref, o_ref):
        @pl.when(pl.program_id(0) == 0)
        def _():
            o_ref[...] = jnp.zeros_like(o_ref)

        for g in range(NG):
            o_ref[g] += _dot_tn(a_ref[g], b_ref[g])

    return pl.pallas_call(
        body, name="wgrad_pool", grid=(t // tk,),
        in_specs=[pl.BlockSpec((NG, tk, CG), lambda k: (0, k, 0))] * 2,
        out_specs=pl.BlockSpec((NG, CG, CG), lambda k: (0, 0, 0)),
        out_shape=jax.ShapeDtypeStruct((NG, CG, CG), F32),
        compiler_params=_cparams(1, VMEM_BIG),
    )(p, dpw)


def _adamw(w, g, m, v):
    m = ADAM_B1 * m + (1.0 - ADAM_B1) * g
    v = ADAM_B2 * v + (1.0 - ADAM_B2) * (g * g)
    m_hat = m / (1.0 - ADAM_B1 ** ADAM_STEP)
    v_hat = v / (1.0 - ADAM_B2 ** ADAM_STEP)
    delta = -ADAM_LR * (m_hat / (jnp.sqrt(v_hat) + ADAM_EPS) + ADAM_WD * w)
    return delta, m, v


def _row_block(r):
    return 512 if r % 512 == 0 else r


def chip_partial(place, gs, from_sibling, name):
    n = len(gs)
    shapes = [g.shape[2:] for g in gs]

    def body(place_ref, *refs):
        for g_ref, s_ref, o_ref in zip(refs[:n], refs[n:2 * n], refs[2 * n:]):
            o_ref[...] = (g_ref[...] + s_ref[...]).astype(BF16)

    def slot(rc):
        return pl.BlockSpec((None,) + rc, lambda k, pr: (pr[1] ^ (k + 1), 0, 0))

    return pl.pallas_call(
        body, name=name,
        grid_spec=pltpu.PrefetchScalarGridSpec(
            num_scalar_prefetch=1, grid=(3,),
            in_specs=[pl.BlockSpec((None, None) + rc, lambda k, pr: (pr[0], pr[1] ^ (k + 1), 0, 0)) for rc in shapes]
                     + [slot(rc) for rc in shapes],
            out_specs=[slot(rc) for rc in shapes]),
        out_shape=[jax.ShapeDtypeStruct((4,) + rc, BF16) for rc in shapes],
        compiler_params=_cparams(1, VMEM_BIG),
    )(place, *gs, *from_sibling)


def finish_adamw(place, gs, from_sibling, from_chips, w, m, v, name, transposed=False):
    n = len(gs)
    r = gs[0].shape[2]
    widths = [g.shape[3] for g in gs]
    c = sum(widths)
    br = _row_block(r)

    def body(place_ref, *refs):
        g_refs, s_refs, c_refs = refs[:n], refs[n:2 * n], refs[2 * n:5 * n]
        w_ref, m_ref, v_ref, og_ref, od_ref, om_ref, ov_ref = refs[5 * n:]
        cols = []
        for q in range(n):
            grad = g_refs[q][...] + s_refs[q][...]
            for k in range(3):
                grad = grad + c_refs[3 * q + k][...].astype(F32)
            cols.append(grad)
        grad = cols[0] if n == 1 else jnp.concatenate(cols, axis=1)
        if transposed:
            grad = grad.T
        og_ref[...] = grad
        od_ref[...], om_ref[...], ov_ref[...] = _adamw(w_ref[...], grad, m_ref[...], v_ref[...])

    def other(k, cq):
        return pl.BlockSpec((None, br, cq), lambda i, pr: (pr[1] ^ k, i, 0))

    row = pl.BlockSpec((c, br), lambda i, pr: (0, i)) if transposed else pl.BlockSpec((br, c), lambda i, pr: (i, 0))
    out = jax.ShapeDtypeStruct((c, r) if transposed else (r, c), F32)
    in_specs = [pl.BlockSpec((None, None, br, cq), lambda i, pr: (pr[0], pr[1], i, 0)) for cq in widths]
    in_specs += [pl.BlockSpec((None, br, cq), lambda i, pr: (pr[1], i, 0)) for cq in widths]
    in_specs += [other(k, cq) for cq in widths for k in (1, 2, 3)]
    return pl.pallas_call(
        body, name=name,
        grid_spec=pltpu.PrefetchScalarGridSpec(
            num_scalar_prefetch=1, grid=(r // br,), in_specs=in_specs + [row, row, row], out_specs=[row] * 4),
        out_shape=[out] * 4,
        compiler_params=_cparams(1, VMEM_BIG),
    )(place, *gs, *from_sibling, *[fc for fc in from_chips for _ in range(3)], w, m, v)


def adamw_small(items):
    n = len(items)

    def body(*refs):
        ins, outs = refs[:4 * n], refs[4 * n:]
        for i in range(n):
            w, g, m, v = (r[...] for r in ins[4 * i:4 * i + 4])
            outs[3 * i][...], outs[3 * i + 1][...], outs[3 * i + 2][...] = _adamw(w, g, m, v)

    out = [jax.ShapeDtypeStruct(it[0].shape, F32) for it in items for _ in range(3)]
    res = pl.pallas_call(body, name="adamw_small", out_shape=out)(*[a for it in items for a in it])
    return [res[3 * i:3 * i + 3] for i in range(n)]


def kernel(x, norm_mix, w_in, pool_w, pool_scale, w_pool_proj, conv_w, w_conv_out, w_o, norm_ffn, w_up, ffn_conv_w, ffn_conv_b, w_down, norm_final, loss_target, m_norm_mix, m_w_in, m_pool_w, m_pool_scale, m_w_pool_proj, m_conv_w, m_w_conv_out, m_w_o, m_norm_ffn, m_w_up, m_ffn_conv_w, m_ffn_conv_b, m_w_down, m_norm_final, v_norm_mix, v_w_in, v_pool_w, v_pool_scale, v_w_pool_proj, v_conv_w, v_w_conv_out, v_w_o, v_norm_ffn, v_w_up, v_ffn_conv_w, v_ffn_conv_b, v_w_down, v_norm_final):
    nb, seq, _ = x.shape
    t = nb * seq
    tm_in = min(TM_IN, t)
    tm_mix = min(TM_MIX, seq)
    tm_ffn = min(TM_FFN, seq)
    tk = min(TK_WGRAD, t)
    xt = x.reshape(t, D)
    tgt = loss_target.reshape(t, D)
    xi, yi, ci = _pos()
    me = 4 * xi + 2 * yi + ci
    place = jnp.stack([ci, 2 * xi + yi]).astype(jnp.int32)

    tie = lax.optimization_barrier
    w_in_b = w_in[0].astype(BF16)
    w_in_g = [all_gather_blocks([w_in_b[:, q * CG:(q + 1) * CG]], f"all_gather_w_in_{q}", 0)[0] for q in range(3)]
    taps = (jnp.pad(conv_w[0], ((0, 5), (0, D - 128))) + jnp.pad(ffn_conv_w[0], ((3, 2), (0, D - SH_UP))))
    taps_g = _exchange_small(taps, "all_gather_taps")
    mix_shard = jnp.concatenate(
        [w_pool_proj[0], w_conv_out[0], w_o[0], pool_w[0].reshape(NG * 32, CG)], axis=1).astype(BF16)
    conv_w_f = taps_g[:, 0:3, :128].transpose(1, 0, 2).reshape(3, D)
    fcw_f = taps_g[:, 3:6, :SH_UP]
    fcb_f = ffn_conv_b.reshape(NDEV, 1, SH_UP)
    mix_shard, conv_w_f, fcw_f, fcb_f = tie((mix_shard, conv_w_f, fcw_f, fcb_f))
    wmix_g, = all_gather_blocks([mix_shard], "all_gather_w_mix", 0)
    ffn_shards, w_in_g[0] = tie(([w_up[0].astype(BF16), w_down[0].astype(BF16)], w_in_g[0]))
    w_up_g, = all_gather_blocks(ffn_shards[:1], "all_gather_w_up", 0)
    w_dn_g, = all_gather_blocks(ffn_shards[1:], "all_gather_w_down", 0)
    w_dn_f = w_dn_g.reshape(NCH, SH_UP, D)
    gfin = norm_final.reshape(1, D)

    zs, h1 = fwd_in(xt, norm_mix, w_in_g, tm_in)
    wmix_g, zs = tie((wmix_g, zs))
    wmix = wmix_g.reshape(D, MIX_COLS)
    pool_w_f = wmix_g[:, :, 3 * D:].reshape(NDEV, NG, 32, CG).transpose(1, 0, 2, 3).reshape(NG, CG, CG)
    x1, y_pool, y_conv = fwd_mix(zs, xt, pool_w_f, pool_scale, conv_w_f, wmix, tm_mix, seq)
    up, pre, act_tok, act, h2 = fwd_up(x1, norm_ffn, w_up_g, fcw_f, fcb_f, tm_ffn, seq)
    dx2, dx2b, ffn_vec = fwd_down(x1, act_tok, w_dn_f, gfin, tgt, min(TM_IN, t))

    def to_sibling(full, tag):
        return reduce_scatter_d2d(full, "reduce_scatter_d2d_" + tag, 1)

    def partials(full, from_sib, tag):
        return chip_partial(place, full, from_sib, "chip_partial_" + tag)

    def to_chips(parts, tag):
        return reduce_scatter_ici(parts, "reduce_scatter_ici_" + tag, 2)

    def finish(nm, gs, from_sib, from_chips, wmv, transposed=False):
        rc = (gs[0].shape[2], sum(g.shape[3] for g in gs))
        wmv2 = [a.reshape(rc).T if transposed else a.reshape(rc) for a in wmv]
        outs = finish_adamw(place, gs, from_sib, from_chips, *wmv2, "adamw_" + nm, transposed)
        return [(o.T if transposed else o).reshape(wmv[0].shape) for o in outs]

    def after(x, dep):
        return tie((x, dep))[0]

    big = {}
    d_up, dx1, g_ffn_vec, g_nffn = bwd_ffn(dx2, x1, norm_ffn, up, pre, w_up_g, fcw_f, w_dn_f, tm_ffn, seq)
    gw_up = wgrad_cols(h2, d_up.reshape(NDEV, 1, t, SH_UP), 0, "wgrad_up")
    sib_up = to_sibling([gw_up], "w_up")
    gw_dn = wgrad_down(act, after(dx2b, gw_up))
    sib_dn = to_sibling([after(gw_dn, sib_up)], "w_down")
    dx1, part_up = tie((dx1, partials([gw_up], sib_up, "w_up")))
    chips_up = to_chips(part_up, "w_up")
    dz, merged, p2, u, dyp, dyc, p, dpw, g_mix_vec = bwd_mix(
        dx1, zs, y_pool, y_conv, pool_w_f, pool_scale, conv_w_f, wmix, tm_mix, seq)
    merged, part_dn = tie((merged, partials([gw_dn], sib_dn, "w_down")))
    chips_dn = to_chips(part_dn, "w_down")
    gw_o = wgrad_square(merged, dx1, "wgrad_o", tk)
    gw_pp = wgrad_square(p2, dyp, "wgrad_pool_proj", tk)
    gw_co = wgrad_square(u, dyc, "wgrad_conv_out", tk)
    gw_pool = wgrad_pool(p, dpw, tk).reshape(NG, 4, 2, 32, CG).transpose(2, 1, 0, 3, 4).reshape(2, 4, NG * 32, CG)
    dz8 = dz.reshape(NDEV, 3, t, CG)
    gw_in, sib_in, chips_in = [None] * 3, [None] * 3, [None] * 3
    sib_a = to_sibling(after([gw_o, gw_pp], (chips_up, gw_pool, chips_dn)), "mix_a")
    sib_b = to_sibling(after([gw_co, gw_pool], sib_a), "mix_b")
    gw_in[0] = wgrad_cols(h1, dz8, 0, "wgrad_in_0")
    h1, part_a, part_b = tie((h1, partials([gw_o, gw_pp], sib_a, "mix_a"),
                              partials([gw_co, gw_pool], sib_b, "mix_b")))
    chips_a = to_chips(after(part_a, chips_dn), "mix_a")
    chips_b = to_chips(part_b, "mix_b")
    sib_in[0] = to_sibling(after([gw_in[0]], sib_b), "w_in_0")
    gw_in[1] = wgrad_cols(h1, dz8, 1, "wgrad_in_1")
    h1, part_in0, gw_in[1] = tie((h1, partials([gw_in[0]], sib_in[0], "w_in_0"), gw_in[1]))
    chips_in[0] = to_chips(part_in0, "w_in_0")
    sib_in[1] = to_sibling(after([gw_in[1]], sib_in[0]), "w_in_1")
    h1, big["w_down"], big["w_up"] = tie((
        h1, finish("w_down", [gw_dn], sib_dn, chips_dn, (w_down, m_w_down, v_w_down)),
        finish("w_up", [gw_up], sib_up, chips_up, (w_up, m_w_up, v_w_up), transposed=True)))
    gw_in[2] = wgrad_cols(h1, dz8, 2, "wgrad_in_2")
    sib_in[2] = to_sibling(after([gw_in[2]], (chips_a, chips_b, chips_in[0])), "w_in_2")
    sib_in[2], big["w_o"], big["w_pool_proj"], big["w_conv_out"], big["pool_w"] = tie((
        sib_in[2],
        finish("w_o", [gw_o], sib_a[:1], chips_a[:1], (w_o, m_w_o, v_w_o)),
        finish("w_pool_proj", [gw_pp], sib_a[1:], chips_a[1:], (w_pool_proj, m_w_pool_proj, v_w_pool_proj)),
        finish("w_conv_out", [gw_co], sib_b[:1], chips_b[:1], (w_conv_out, m_w_conv_out, v_w_conv_out)),
        finish("pool_w", [gw_pool], sib_b[1:], chips_b[1:], (pool_w, m_pool_w, v_pool_w))))
    dx1, part_in12 = tie((dx1, partials(gw_in[1:], sib_in[1] + sib_in[2], "w_in_12")))
    chips_in[1] = to_chips(part_in12[:1], "w_in_1")
    chips_in[2] = to_chips(part_in12[1:], "w_in_2")
    small_g, = all_gather_blocks(
        [after(jnp.concatenate([g_mix_vec, g_nffn, ffn_vec, g_ffn_vec.reshape(8 * NDEV, D)], axis=0), sib_in[2])],
        "all_gather_small", 0)
    grad_x, g_nmix = bwd_in(dz, w_in_g, dx1, xt, norm_mix, min(TM_BWD_IN, t))
    grad_x, chips_in = tie((grad_x, chips_in))
    nmix_g, = all_gather_blocks([g_nmix], "all_gather_norm_mix", 0)
    big["w_in"] = finish("w_in", gw_in, [s[0] for s in sib_in], [c[0] for c in chips_in], (w_in, m_w_in, v_w_in))

    red, = sum_blocks([small_g], "sum_small")
    red_n, = sum_blocks([after(nmix_g, (big["w_in"], red))], "sum_norm_mix")
    g_norm_mix, g_pool_scale, g_norm_ffn = red_n[0:1], red[0:1], red[8:9]
    g_conv_w = lax.dynamic_slice(red, (1, me * 128), (3, 128))
    g_norm_final = red[16]
    loss = red[17, 0]
    g_fcb = red[24:].reshape(NDEV, 8, D)[:, 0, :SH_UP].reshape(1, FF2)
    g_fcw = lax.dynamic_slice(red, (25 + 8 * me, 0), (3, SH_UP))
    grads = {"norm_mix": g_norm_mix, "pool_scale": g_pool_scale, "norm_ffn": g_norm_ffn, "norm_final": g_norm_final,
             "ffn_conv_b": g_fcb, "conv_w": g_conv_w.reshape(1, 3, 128), "ffn_conv_w": g_fcw.reshape(1, 3, SH_UP)}
    small_wmv = {"norm_mix": (norm_mix, m_norm_mix, v_norm_mix), "pool_scale": (pool_scale, m_pool_scale, v_pool_scale),
                 "norm_ffn": (norm_ffn, m_norm_ffn, v_norm_ffn), "norm_final": (norm_final, m_norm_final, v_norm_final),
                 "ffn_conv_b": (ffn_conv_b, m_ffn_conv_b, v_ffn_conv_b), "conv_w": (conv_w, m_conv_w, v_conv_w),
                 "ffn_conv_w": (ffn_conv_w, m_ffn_conv_w, v_ffn_conv_w)}
    small_names = list(small_wmv)
    flat2 = lambda a: a.reshape(1, -1) if a.ndim == 1 else (a.transpose(1, 0, 2) if a.ndim == 3 else a)
    unflat = lambda o, like: o.transpose(1, 0, 2) if like.ndim == 3 else o.reshape(like.shape)
    small_out = adamw_small([(flat2(small_wmv[nm][0]), flat2(grads[nm]), flat2(small_wmv[nm][1]),
                              flat2(small_wmv[nm][2])) for nm in small_names])
    small = {nm: [unflat(o, small_wmv[nm][0]) for o in outs] for nm, outs in zip(small_names, small_out)}

    order = ["norm_mix", "w_in", "pool_w", "pool_scale", "w_pool_proj", "conv_w", "w_conv_out", "w_o", "norm_ffn",
             "w_up", "ffn_conv_w", "ffn_conv_b", "w_down", "norm_final"]
    out = [loss, grad_x.reshape(nb, seq, D)]
    out += [big[nm][0] if nm in big else grads[nm] for nm in order]
    for idx in range(3):
        out += [big[nm][idx + 1] if nm in big else small[nm][idx] for nm in order]
    return tuple(out)
```

```python
import jax
import jax.numpy as jnp
from jax import lax
from jax.experimental import pallas as pl
from jax.experimental.pallas import tpu as pltpu
from jax.experimental.pallas import tpu_sc as plsc

F32 = jnp.float32
BF16 = jnp.bfloat16

NDEV = 8
D = 1024
NG = 4
CG = 256
WINS = (2, 4, 8, 16)
DIN = 6 * D
SH_IN = DIN // NDEV
NZT = DIN // CG
FF2 = 5632
SH_UP = FF2 // NDEV
FF = FF2 // 2
NCH = 4
SH_DN = FF // NDEV
RMS_EPS = 1e-6
HALO = 16

ADAM_LR = 0.001
ADAM_B1 = 0.9
ADAM_B2 = 0.999
ADAM_EPS = 1e-08
ADAM_WD = 0.01
ADAM_STEP = 10

TM_IN = 512
TM_BWD_IN = 256
TM_MIX = 256
TM_FFN = 256
TK_WGRAD = 1024
MIX_POOL_PROJ, MIX_CONV_OUT, MIX_O = 0, 1, 2
MIX_COLS = 3 * D + CG
VMEM_BIG = 56 * 1024 * 1024
MESH = pl.DeviceIdType.MESH
ANY = pl.BlockSpec(memory_space=pl.ANY)


def _cparams(n_axes, vmem=None):
    return pltpu.CompilerParams(dimension_semantics=("arbitrary",) * n_axes, vmem_limit_bytes=vmem)


def _dot(a, b):
    return jnp.dot(a, b, preferred_element_type=F32)


def _dot_nt(a, b):
    return lax.dot_general(a, b, (((1,), (1,)), ((), ())), preferred_element_type=F32)


def _dot_tn(a, b):
    return lax.dot_general(a, b, (((0,), (0,)), ((), ())), preferred_element_type=F32)


def _shift_down(ext, s, lead):
    return pltpu.roll(ext, s, 0)[lead:]


def _shift_up(ext, s, tm):
    n = ext.shape[0]
    return pltpu.roll(ext, n - s, 0)[:tm]


def _rms_inv(x):
    return lax.rsqrt(jnp.mean(x * x, axis=-1, keepdims=True) + RMS_EPS)


def _rms_bwd(dh, xn, inv, g):
    dxn = dh * g
    return inv * (dxn - xn * jnp.mean(dxn * xn, axis=-1, keepdims=True))


def _pos():
    return lax.axis_index("x"), lax.axis_index("y"), lax.axis_index("c")


def _handshake(peers):
    barrier = pltpu.get_barrier_semaphore()
    for peer in peers:
        pl.semaphore_signal(barrier, inc=1, device_id=peer, device_id_type=MESH)
    pl.semaphore_wait(barrier, len(peers))


def _sequencer(body, out_type, n_sems, name, collective_id):
    return pl.kernel(
        body, out_type=out_type, mesh=plsc.ScalarSubcoreMesh(axis_name="sequencer", num_cores=1), name=name,
        scratch_types=[pltpu.SemaphoreType.DMA((n_sems,)), pltpu.SemaphoreType.DMA((n_sems,))],
        compiler_params=pltpu.CompilerParams(collective_id=collective_id))


def all_gather_blocks(shards, name, collective_id):
    n = len(shards)

    def body(*refs):
        ins, outs = refs[:n], refs[n:2 * n]
        send_sems, recv_sems = refs[2 * n:]
        x, y, c = _pos()
        me, sibling = (x, y, c), (x, y, 1 - c)
        first_chip, second_chip, diagonal = (x ^ (1 - c), y ^ c), (x ^ c, y ^ (1 - c)), (1 - x, 1 - y)
        first, second = (*first_chip, c), (*second_chip, c)
        _handshake([sibling, first, second])

        def copy(w, k, block, to, src=None):
            slot = outs[w].at[4 * block[0] + 2 * block[1] + block[2]]
            return pltpu.make_async_remote_copy(
                src_ref=slot if src is None else src, dst_ref=slot,
                send_sem=send_sems.at[8 * w + k], recv_sem=recv_sems.at[8 * w + k], device_id=to, device_id_type=MESH)

        mine, sent = [], []
        for w in range(n):
            m = pltpu.make_async_copy(ins[w], outs[w].at[4 * x + 2 * y + c], send_sems.at[8 * w + 7])
            m.start()
            mine.append(m)
            sent += [copy(w, k, me, to, src=ins[w]) for k, to in enumerate((sibling, first, second))]
        for cp in sent:
            cp.start()
        for k, chip in ((1, first_chip), (2, second_chip), (3, diagonal)):
            for w in range(n):
                copy(w, k, (*chip, c), me).wait_recv()
                onward = [copy(w, 3 + k, (*chip, c), sibling)] + ([copy(w, 3, (*chip, c), second)] if k == 1 else [])
                for cp in onward:
                    cp.start()
                sent += onward
        for w in range(n):
            copy(w, 0, sibling, me).wait_recv()
            for k, chip in ((4, second_chip), (5, first_chip), (6, diagonal)):
                copy(w, k, (*chip, 1 - c), me).wait_recv()
        for cp in sent:
            cp.wait_send()
        for m in mine:
            m.wait()

    out = [jax.ShapeDtypeStruct((NDEV,) + s.shape, s.dtype) for s in shards]
    return _sequencer(body, out, 8 * n, name, collective_id)(*shards)


def _exchange_small(v, name):
    rows = v.shape[0]

    def body(v_ref, out_ref, slots, send_sems, recv_sems, local_sem):
        x, y, c = _pos()
        me = 4 * x + 2 * y + c
        mine = pltpu.make_async_copy(v_ref, slots.at[me], local_sem)
        mine.start()
        offs = [(dx, dy, dc) for dx in (0, 1) for dy in (0, 1) for dc in (0, 1)][1:]

        def copy(k, src_slot, to):
            return pltpu.make_async_remote_copy(
                src_ref=v_ref, dst_ref=slots.at[src_slot], send_sem=send_sems.at[k], recv_sem=recv_sems.at[k],
                device_id=to, device_id_type=MESH)

        sends = []
        for k, (dx, dy, dc) in enumerate(offs):
            cp = copy(k, me, (x ^ dx, y ^ dy, c ^ dc))
            cp.start()
            sends.append(cp)
        for k, (dx, dy, dc) in enumerate(offs):
            copy(k, 4 * (x ^ dx) + 2 * (y ^ dy) + (c ^ dc), (x, y, c)).wait_recv()
        for cp in sends:
            cp.wait_send()
        mine.wait()
        out_ref[...] = slots[...]

    vmem = pl.BlockSpec(memory_space=pltpu.VMEM)
    return pl.pallas_call(
        body, name=name, out_shape=jax.ShapeDtypeStruct((NDEV, rows, D), F32), in_specs=[vmem], out_specs=vmem,
        scratch_shapes=[pltpu.VMEM((NDEV, rows, D), F32), pltpu.SemaphoreType.DMA((7,)),
                        pltpu.SemaphoreType.DMA((7,)), pltpu.SemaphoreType.DMA],
    )(v)


def sum_blocks(gathered, name):
    def body(*refs):
        for g_ref, o_ref in zip(refs[:len(gathered)], refs[len(gathered):]):
            acc = g_ref[0]
            for d in range(1, NDEV):
                acc = acc + g_ref[d]
            o_ref[...] = acc

    vmem = pl.BlockSpec(memory_space=pltpu.VMEM)
    return pl.pallas_call(
        body, name=name, out_shape=[jax.ShapeDtypeStruct(g.shape[1:], F32) for g in gathered],
        in_specs=[vmem] * len(gathered), out_specs=[vmem] * len(gathered),
    )(*gathered)


def reduce_scatter_d2d(grads, name, collective_id):
    n = len(grads)

    def body(*refs):
        ins, outs = refs[:n], refs[n:2 * n]
        send_sems, recv_sems = refs[2 * n:]
        x, y, c = _pos()
        _handshake([(x, y, 1 - c)])
        cps = []
        for w in range(n):
            cp = pltpu.make_async_remote_copy(
                src_ref=ins[w].at[1 - c], dst_ref=outs[w], send_sem=send_sems.at[w], recv_sem=recv_sems.at[w],
                device_id=(x, y, 1 - c), device_id_type=MESH)
            cp.start()
            cps.append(cp)
        for cp in cps:
            cp.wait_recv()
        for cp in cps:
            cp.wait_send()

    out = [jax.ShapeDtypeStruct(g.shape[1:], F32) for g in grads]
    return _sequencer(body, out, n, name, collective_id)(*grads)


def reduce_scatter_ici(parts, name, collective_id):
    n = len(parts)

    def body(*refs):
        ins, outs = refs[:n], refs[n:2 * n]
        send_sems, recv_sems = refs[2 * n:]
        x, y, c = _pos()
        offs = [(1, 0), (0, 1), (1, 1)]
        _handshake([(x ^ dx, y ^ dy, c) for dx, dy in offs])
        cps = []
        for w in range(n):
            for k, (dx, dy) in enumerate(offs):
                ox, oy = x ^ dx, y ^ dy
                cp = pltpu.make_async_remote_copy(
                    src_ref=ins[w].at[2 * ox + oy], dst_ref=outs[w].at[2 * x + y],
                    send_sem=send_sems.at[3 * w + k], recv_sem=recv_sems.at[3 * w + k],
                    device_id=(ox, oy, c), device_id_type=MESH)
                cp.start()
                cps.append((cp, w, k, ox, oy))
        for cp, w, k, ox, oy in cps:
            pltpu.make_async_remote_copy(
                src_ref=ins[w].at[2 * ox + oy], dst_ref=outs[w].at[2 * ox + oy],
                send_sem=send_sems.at[3 * w + k], recv_sem=recv_sems.at[3 * w + k],
                device_id=(ox, oy, c), device_id_type=MESH).wait_recv()
        for cp, *_ in cps:
            cp.wait_send()

    out = [jax.ShapeDtypeStruct(p.shape, BF16) for p in parts]
    return _sequencer(body, out, 3 * n, name, collective_id)(*parts)


def fwd_in(x, g1, w_in_pieces, tm):
    t = x.shape[0]
    tile = lambda i: (i, 0)
    w_spec = pl.BlockSpec((NDEV, D, CG), lambda i: (0, 0, 0))
    z_spec = pl.BlockSpec((NDEV, tm, CG), lambda i: (0, i, 0))
    z_shape = jax.ShapeDtypeStruct((NDEV, t, CG), BF16)

    def cost(other_bytes, transcendentals):
        return pl.CostEstimate(flops=2 * t * D * NDEV * CG, transcendentals=transcendentals,
                               bytes_accessed=other_bytes + 2 * D * NDEV * CG + 2 * t * NDEV * CG)

    def first(x_ref, g_ref, w_ref, z_ref, h_ref, ht_ref):
        xf = x_ref[...]
        h = (xf * _rms_inv(xf) * g_ref[...]).astype(BF16)
        h_ref[...] = h
        ht_ref[...] = h.T
        for j in range(NDEV):
            z_ref[j] = _dot(h, w_ref[j]).astype(BF16)

    z0, h, ht = pl.pallas_call(
        first, name="fwd_in_0", grid=(t // tm,),
        in_specs=[pl.BlockSpec((tm, D), tile), pl.BlockSpec((1, D), lambda i: (0, 0)), w_spec],
        out_specs=[z_spec, pl.BlockSpec((tm, D), tile), pl.BlockSpec((D, tm), lambda i: (0, i))],
        out_shape=[z_shape, jax.ShapeDtypeStruct((t, D), BF16), jax.ShapeDtypeStruct((D, t), BF16)],
        compiler_params=_cparams(1, VMEM_BIG), cost_estimate=cost(8 * t * D + 4 * D, t),
    )(x, g1, w_in_pieces[0])
    zs = [z0]
    for q in (1, 2):
        h, zs[-1] = lax.optimization_barrier((h, zs[-1]))

        def later(h_ref, w_ref, z_ref):
            hb = h_ref[...]
            for j in range(NDEV):
                z_ref[j] = _dot(hb, w_ref[j]).astype(BF16)

        zs.append(pl.pallas_call(
            later, name=f"fwd_in_{q}", grid=(t // tm,),
            in_specs=[pl.BlockSpec((tm, D), tile), w_spec], out_specs=z_spec, out_shape=z_shape,
            compiler_params=_cparams(1, VMEM_BIG), cost_estimate=cost(2 * t * D, 0),
        )(h, w_in_pieces[q]))
    return zs, ht


def _z_tile(z_refs, n):
    return z_refs[n % 3][n // 3]


def _pool_tile(z, zh, win, keep_hist, cnt):
    zt = z.astype(F32)
    ext = jnp.concatenate([zh.astype(F32) * keep_hist, zt], axis=0)
    s, sh = ext, 1
    while sh < win:
        s = s + pltpu.roll(s, sh, 0)
        sh *= 2
    return s[HALO:] / cnt - zt


def _conv_taps(ext, cur, w_ref, lanes, lead):
    x1 = _shift_down(ext, 1, lead)
    x2 = _shift_down(ext, 2, lead)
    out = w_ref[2:3, lanes] * cur + w_ref[1:2, lanes] * x1 + w_ref[0:1, lanes] * x2
    return out, x1, x2


def fwd_mix(zs, x, pool_w, pool_scale, conv_w, wmix, tm, seq):
    t = x.shape[0]
    tps = seq // tm
    hb = tm // HALO

    def body(z0_ref, z1_ref, z2_ref, zh0_ref, zh1_ref, zh2_ref, x_ref, pw_ref, ps_ref, wpp_ref, cw_ref, wco_ref,
             wo_ref, x1_ref, yp_ref, yc_ref):
        z_refs, zh_refs = (z0_ref, z1_ref, z2_ref), (zh0_ref, zh1_ref, zh2_ref)
        it = pl.program_id(0) % tps
        keep_hist = jnp.where(it == 0, 0.0, 1.0)
        pos = it * tm + lax.broadcasted_iota(jnp.int32, (tm, 1), 0)
        p2 = []
        for g, win in enumerate(WINS):
            cnt = jnp.minimum(pos + 1, win).astype(F32)
            p = _pool_tile(_z_tile(z_refs, g), _z_tile(zh_refs, g), win, keep_hist, cnt)
            lanes = slice(g * CG, (g + 1) * CG)
            p2.append((_dot(p.astype(BF16), pw_ref[g]) * ps_ref[:, lanes]).astype(BF16))
        y_pool = _dot(jnp.concatenate(p2, axis=1), wpp_ref[...])
        u = []
        for q in range(NG):
            lanes = slice(q * CG, (q + 1) * CG)
            cv = _z_tile(z_refs, 8 + q).astype(F32) * _z_tile(z_refs, 12 + q).astype(F32)
            cvh = _z_tile(zh_refs, 8 + q).astype(F32) * _z_tile(zh_refs, 12 + q).astype(F32) * keep_hist
            cc, _, _ = _conv_taps(jnp.concatenate([cvh, cv], axis=0), cv, cw_ref, lanes, HALO)
            u.append((_z_tile(z_refs, 4 + q).astype(F32) * cc).astype(BF16))
        y_conv = _dot(jnp.concatenate(u, axis=1), wco_ref[...])
        ypb, ycb = y_pool.astype(BF16), y_conv.astype(BF16)
        yp_ref[...] = ypb
        yc_ref[...] = ycb
        merged = []
        for q in range(NG):
            lanes = slice(q * CG, (q + 1) * CG)
            sp = jax.nn.sigmoid(_z_tile(z_refs, 16 + q).astype(F32))
            sc = jax.nn.sigmoid(_z_tile(z_refs, 20 + q).astype(F32))
            merged.append((sp * ypb[:, lanes].astype(F32) + sc * ycb[:, lanes].astype(F32)).astype(BF16))
        x1_ref[...] = x_ref[...] + _dot(jnp.concatenate(merged, axis=1), wo_ref[...])

    def hist(i):
        return jnp.maximum(i * hb - 1, 0)

    const2 = lambda i: (0, 0)
    return pl.pallas_call(
        body, name="fwd_mix", grid=(t // tm,),
        in_specs=[pl.BlockSpec((NDEV, tm, CG), lambda i: (0, i, 0))] * 3
                 + [pl.BlockSpec((NDEV, HALO, CG), lambda i: (0, hist(i), 0))] * 3
                 + [pl.BlockSpec((tm, D), lambda i: (i, 0)),
                    pl.BlockSpec((NG, CG, CG), lambda i: (0, 0, 0)), pl.BlockSpec((1, D), const2),
                    pl.BlockSpec((D, D), lambda i: (0, MIX_POOL_PROJ)), pl.BlockSpec((3, D), const2),
                    pl.BlockSpec((D, D), lambda i: (0, MIX_CONV_OUT)), pl.BlockSpec((D, D), lambda i: (0, MIX_O))],
        out_specs=[pl.BlockSpec((tm, D), lambda i: (i, 0))] * 3,
        out_shape=[jax.ShapeDtypeStruct((t, D), F32), jax.ShapeDtypeStruct((t, D), BF16),
                   jax.ShapeDtypeStruct((t, D), BF16)],
        compiler_params=_cparams(1, VMEM_BIG),
    )(*zs, *zs, x, pool_w, pool_scale, wmix, conv_w, wmix, wmix)


def fwd_up(x1, g2, w_up_g, fcw, fcb, tm, seq):
    t = x1.shape[0]
    tps = seq // tm

    def body(x1_ref, g2_ref, wup_ref, fcw_ref, fcb_ref, up_ref, pre_ref, act_ref, actt_ref, h2t_ref, hist_ref):
        i = pl.program_id(0)
        keep_hist = jnp.where(i % tps == 0, 0.0, 1.0)

        @pl.when(i == 0)
        def _():
            hist_ref[...] = jnp.zeros_like(hist_ref)

        x1v = x1_ref[...]
        h2 = (x1v * _rms_inv(x1v) * g2_ref[...]).astype(BF16)
        h2t_ref[...] = h2.T
        lanes = slice(0, SH_UP)
        for c in range(NCH):
            conv = []
            for s in range(2):
                ub = _dot(h2, wup_ref[s, c]).astype(BF16)
                up_ref[s, c] = ub
                uf = ub.astype(F32)
                ext = jnp.concatenate([hist_ref[s, c] * keep_hist, uf], axis=0)
                hist_ref[s, c] = uf[tm - 8:]
                cc, _, _ = _conv_taps(ext, uf, fcw_ref.at[s, c], lanes, 8)
                conv.append(cc + fcb_ref[s, c])
                pre_ref[s, c] = conv[s].astype(BF16)
            a = (conv[0] * jax.nn.sigmoid(conv[0]) * conv[1]).astype(BF16)
            act_ref[c] = a
            actt_ref[c] = a.T

    tile = lambda i: (i, 0)
    const2 = lambda i: (0, 0)
    whole = lambda i: (0, 0, 0, 0)
    chunks = pl.BlockSpec((2, NCH, tm, SH_UP), lambda i: (0, 0, i, 0))
    return pl.pallas_call(
        body, name="fwd_up", grid=(t // tm,),
        in_specs=[pl.BlockSpec((tm, D), tile), pl.BlockSpec((1, D), const2),
                  pl.BlockSpec((2, NCH, D, SH_UP), whole), pl.BlockSpec((2, NCH, 3, SH_UP), whole),
                  pl.BlockSpec((2, NCH, 1, SH_UP), whole)],
        out_specs=[chunks, chunks, pl.BlockSpec((NCH, tm, SH_UP), lambda i: (0, i, 0)),
                   pl.BlockSpec((NCH, SH_UP, tm), lambda i: (0, 0, i)), pl.BlockSpec((D, tm), lambda i: (0, i))],
        out_shape=[jax.ShapeDtypeStruct((2, NCH, t, SH_UP), BF16), jax.ShapeDtypeStruct((2, NCH, t, SH_UP), BF16),
                   jax.ShapeDtypeStruct((NCH, t, SH_UP), BF16), jax.ShapeDtypeStruct((NCH, SH_UP, t), BF16),
                   jax.ShapeDtypeStruct((D, t), BF16)],
        scratch_shapes=[pltpu.VMEM((2, NCH, 8, SH_UP), F32)],
        compiler_params=_cparams(1, VMEM_BIG),
    )(x1, g2, w_up_g.reshape(2, NCH, D, SH_UP), fcw.reshape(2, NCH, 3, SH_UP), fcb.reshape(2, NCH, 1, SH_UP))


def fwd_down(x1, act, w_dn, gf, tgt, tm):
    t = x1.shape[0]

    def body(x1_ref, act_ref, wdn_ref, gf_ref, tgt_ref, dx2_ref, dx2b_ref, vec_ref):
        @pl.when(pl.program_id(0) == 0)
        def _():
            vec_ref[...] = jnp.zeros_like(vec_ref)

        d = None
        for c in range(NCH):
            part = _dot(act_ref[c], wdn_ref[c])
            d = part if d is None else d + part
        x2 = x1_ref[...] + d
        inv3 = _rms_inv(x2)
        xn = x2 * inv3
        diff = xn * gf_ref[...] - tgt_ref[...]
        dy = diff * (1.0 / D)
        vec_ref[0:1, :] += jnp.sum(dy * xn, axis=0, keepdims=True)
        vec_ref[1:2, :] += 0.5 * jnp.sum(jnp.mean(diff * diff, axis=-1))
        dx2 = _rms_bwd(dy, xn, inv3, gf_ref[...])
        dx2_ref[...] = dx2
        dx2b_ref[...] = dx2.astype(BF16)

    tile = lambda i: (i, 0)
    const2 = lambda i: (0, 0)
    return pl.pallas_call(
        body, name="fwd_down", grid=(t // tm,),
        in_specs=[pl.BlockSpec((tm, D), tile), pl.BlockSpec((NCH, tm, SH_UP), lambda i: (0, i, 0)),
                  pl.BlockSpec((NCH, SH_UP, D), lambda i: (0, 0, 0)), pl.BlockSpec((1, D), const2),
                  pl.BlockSpec((tm, D), tile)],
        out_specs=[pl.BlockSpec((tm, D), tile), pl.BlockSpec((tm, D), tile), pl.BlockSpec((8, D), const2)],
        out_shape=[jax.ShapeDtypeStruct((t, D), F32), jax.ShapeDtypeStruct((t, D), BF16),
                   jax.ShapeDtypeStruct((8, D), F32)],
        compiler_params=_cparams(1, VMEM_BIG),
    )(x1, act, w_dn, gf, tgt)


def bwd_ffn(dx2, x1, g2, up, pre, w_up_g, fcw, w_dn, tm, seq):
    t = x1.shape[0]
    nt = t // tm
    tps = seq // tm

    def body(dx2_ref, x1_ref, g2_ref, up_ref, pre_ref, wup_ref, fcw_ref, wdn_ref,
             dup_ref, dx1_ref, gvec_ref, gn_ref, carry_ref):
        i = pl.program_id(0)
        it = (nt - 1 - i) % tps
        keep_next = jnp.where(it == tps - 1, 0.0, 1.0)

        @pl.when(i == 0)
        def _():
            gvec_ref[...] = jnp.zeros_like(gvec_ref)
            gn_ref[...] = jnp.zeros_like(gn_ref)
            carry_ref[...] = jnp.zeros_like(carry_ref)

        dx2v = dx2_ref[...]
        dxb = dx2v.astype(BF16)
        lanes = slice(0, SH_UP)
        dh2 = None
        for c in range(NCH):
            pre = [pre_ref[s, c].astype(F32) for s in range(2)]
            sg = jax.nn.sigmoid(pre[0])
            dact = _dot_nt(dxb, wdn_ref[c])
            dpre = [dact * pre[1] * (sg * (1.0 + pre[0] * (1.0 - sg))), dact * (pre[0] * sg)]
            for s in range(2):
                dc = dpre[s]
                ext = jnp.concatenate([dc, carry_ref[s, c] * keep_next], axis=0)
                carry_ref[s, c] = dc[:8]
                shifted = (_shift_up(ext, 2, tm), _shift_up(ext, 1, tm), dc)
                uf = up_ref[s, c].astype(F32)
                gvec_ref[s, c, 0:1, lanes] += jnp.sum(dc, axis=0, keepdims=True)
                for tap in range(3):
                    gvec_ref[s, c, tap + 1:tap + 2, lanes] += jnp.sum(shifted[tap] * uf, axis=0, keepdims=True)
                w = fcw_ref.at[s, c]
                du = w[2:3, :] * dc + w[1:2, :] * shifted[1] + w[0:1, :] * shifted[0]
                dub = du.astype(BF16)
                dup_ref[s, c] = dub
                part = _dot_nt(dub, wup_ref[s, c])
                dh2 = part if dh2 is None else dh2 + part
        x1v = x1_ref[...]
        inv2 = _rms_inv(x1v)
        xn = x1v * inv2
        gn_ref[0:1, :] += jnp.sum(dh2 * xn, axis=0, keepdims=True)
        dx1_ref[...] = dx2v + _rms_bwd(dh2, xn, inv2, g2_ref[...])

    rev = lambda i: (nt - 1 - i, 0)
    const2 = lambda i: (0, 0)
    whole = lambda i: (0, 0, 0, 0)
    chunks = pl.BlockSpec((2, NCH, tm, SH_UP), lambda i: (0, 0, nt - 1 - i, 0))
    return pl.pallas_call(
        body, name="bwd_ffn", grid=(nt,),
        in_specs=[pl.BlockSpec((tm, D), rev), pl.BlockSpec((tm, D), rev), pl.BlockSpec((1, D), const2),
                  chunks, chunks, pl.BlockSpec((2, NCH, D, SH_UP), whole), pl.BlockSpec((2, NCH, 3, SH_UP), whole),
                  pl.BlockSpec((NCH, SH_UP, D), lambda i: (0, 0, 0))],
        out_specs=[chunks, pl.BlockSpec((tm, D), rev), pl.BlockSpec((2, NCH, 8, D), whole),
                   pl.BlockSpec((8, D), const2)],
        out_shape=[jax.ShapeDtypeStruct((2, NCH, t, SH_UP), BF16), jax.ShapeDtypeStruct((t, D), F32),
                   jax.ShapeDtypeStruct((2, NCH, 8, D), F32), jax.ShapeDtypeStruct((8, D), F32)],
        scratch_shapes=[pltpu.VMEM((2, NCH, 8, SH_UP), F32)],
        compiler_params=_cparams(1, VMEM_BIG),
    )(dx2, x1, g2, up, pre, w_up_g.reshape(2, NCH, D, SH_UP), fcw.reshape(2, NCH, 3, SH_UP), w_dn)


def bwd_mix(dx1, zs, y_pool, y_conv, pool_w, pool_scale, conv_w, wmix, tm, seq):
    t = dx1.shape[0]
    nt = t // tm
    tps = seq // tm
    hb = tm // HALO

    def body(da_ref, z0_ref, z1_ref, z2_ref, zh0_ref, zh1_ref, zh2_ref, yp_ref, yc_ref, pw_ref, ps_ref, wpp_ref,
             cw_ref, wco_ref, wo_ref,
             dz_ref, mg_ref, p2_ref, u_ref, dyp_ref, dyc_ref, p_ref, dpw_ref, gvec_ref, cp_ref, cc_ref):
        z_refs, zh_refs = (z0_ref, z1_ref, z2_ref), (zh0_ref, zh1_ref, zh2_ref)
        i = pl.program_id(0)
        it = (nt - 1 - i) % tps
        keep_hist = jnp.where(it == 0, 0.0, 1.0)
        keep_next = jnp.where(it == tps - 1, 0.0, 1.0)
        pos = it * tm + lax.broadcasted_iota(jnp.int32, (tm, 1), 0)

        @pl.when(i == 0)
        def _():
            gvec_ref[...] = jnp.zeros_like(gvec_ref)
            cp_ref[...] = jnp.zeros_like(cp_ref)
            cc_ref[...] = jnp.zeros_like(cc_ref)

        dm = _dot_nt(da_ref[...].astype(BF16), wo_ref[...])
        merged, dyp, dyc = [], [], []
        for q in range(NG):
            lanes = slice(q * CG, (q + 1) * CG)
            sp = jax.nn.sigmoid(_z_tile(z_refs, 16 + q).astype(F32))
            sc = jax.nn.sigmoid(_z_tile(z_refs, 20 + q).astype(F32))
            yp = yp_ref[:, lanes].astype(F32)
            yc = yc_ref[:, lanes].astype(F32)
            dmq = dm[:, lanes]
            merged.append((sp * yp + sc * yc).astype(BF16))
            dyp.append((dmq * sp).astype(BF16))
            dyc.append((dmq * sc).astype(BF16))
            dz_ref[16 + q] = (dmq * yp * (sp * (1.0 - sp))).astype(BF16)
            dz_ref[20 + q] = (dmq * yc * (sc * (1.0 - sc))).astype(BF16)
        mg_ref[...] = jnp.concatenate(merged, axis=1)
        dypb = jnp.concatenate(dyp, axis=1)
        dycb = jnp.concatenate(dyc, axis=1)
        dyp_ref[...] = dypb
        dyc_ref[...] = dycb

        dp2 = _dot_nt(dypb, wpp_ref[...])
        p2 = []
        for g, win in enumerate(WINS):
            lanes = slice(g * CG, (g + 1) * CG)
            cnt = jnp.minimum(pos + 1, win).astype(F32)
            p = _pool_tile(_z_tile(z_refs, g), _z_tile(zh_refs, g), win, keep_hist, cnt)
            pb = p.astype(BF16)
            p_ref[g] = pb
            pw = _dot(pb, pw_ref[g])
            p2.append((pw * ps_ref[:, lanes]).astype(BF16))
            dp2g = dp2[:, lanes]
            gvec_ref[0:1, lanes] += jnp.sum(dp2g * pw, axis=0, keepdims=True)
            dpwb = (dp2g * ps_ref[:, lanes]).astype(BF16)
            dpw_ref[g] = dpwb
            dp = _dot_nt(dpwb, pw_ref[g])
            qv = dp / cnt
            ext = jnp.concatenate([qv, cp_ref[g] * keep_next], axis=0)
            cp_ref[g] = qv[:HALO]
            n = tm + HALO
            s, sh = ext, 1
            while sh < win:
                s = s + pltpu.roll(s, n - sh, 0)
                sh *= 2
            dz_ref[g] = (s[:tm] - dp).astype(BF16)
        p2_ref[...] = jnp.concatenate(p2, axis=1)

        du = _dot_nt(dycb, wco_ref[...])
        u = []
        for q in range(NG):
            lanes = slice(q * CG, (q + 1) * CG)
            zb = _z_tile(z_refs, 4 + q).astype(F32)
            zc = _z_tile(z_refs, 8 + q).astype(F32)
            zv = _z_tile(z_refs, 12 + q).astype(F32)
            cv = zc * zv
            cvh = _z_tile(zh_refs, 8 + q).astype(F32) * _z_tile(zh_refs, 12 + q).astype(F32) * keep_hist
            cc, cv1, cv2 = _conv_taps(jnp.concatenate([cvh, cv], axis=0), cv, cw_ref, lanes, HALO)
            u.append((zb * cc).astype(BF16))
            duq = du[:, lanes]
            dz_ref[4 + q] = (duq * cc).astype(BF16)
            dcc = duq * zb
            for tap, src in enumerate((cv2, cv1, cv)):
                gvec_ref[tap + 1:tap + 2, lanes] += jnp.sum(dcc * src, axis=0, keepdims=True)
            ext = jnp.concatenate([dcc, cc_ref[:, lanes] * keep_next], axis=0)
            cc_ref[:, lanes] = dcc[:8]
            dcv = (cw_ref[2:3, lanes] * dcc + cw_ref[1:2, lanes] * _shift_up(ext, 1, tm)
                   + cw_ref[0:1, lanes] * _shift_up(ext, 2, tm))
            dz_ref[8 + q] = (dcv * zv).astype(BF16)
            dz_ref[12 + q] = (dcv * zc).astype(BF16)
        u_ref[...] = jnp.concatenate(u, axis=1)

    def hist(i):
        return jnp.maximum((nt - 1 - i) * hb - 1, 0)

    rev = lambda i: (nt - 1 - i, 0)
    rev3 = lambda i: (0, nt - 1 - i, 0)
    const2 = lambda i: (0, 0)
    tok = jax.ShapeDtypeStruct((t, D), BF16)
    grp = jax.ShapeDtypeStruct((NG, t, CG), BF16)
    return pl.pallas_call(
        body, name="bwd_mix", grid=(nt,),
        in_specs=[pl.BlockSpec((tm, D), rev)] + [pl.BlockSpec((NDEV, tm, CG), rev3)] * 3
                 + [pl.BlockSpec((NDEV, HALO, CG), lambda i: (0, hist(i), 0))] * 3
                 + [pl.BlockSpec((tm, D), rev), pl.BlockSpec((tm, D), rev),
                    pl.BlockSpec((NG, CG, CG), lambda i: (0, 0, 0)), pl.BlockSpec((1, D), const2),
                    pl.BlockSpec((D, D), lambda i: (0, MIX_POOL_PROJ)), pl.BlockSpec((3, D), const2),
                    pl.BlockSpec((D, D), lambda i: (0, MIX_CONV_OUT)), pl.BlockSpec((D, D), lambda i: (0, MIX_O))],
        out_specs=[pl.BlockSpec((NZT, tm, CG), rev3)] + [pl.BlockSpec((tm, D), rev)] * 5
                  + [pl.BlockSpec((NG, tm, CG), rev3)] * 2 + [pl.BlockSpec((8, D), const2)],
        out_shape=[jax.ShapeDtypeStruct((NZT, t, CG), BF16), tok, tok, tok, tok, tok, grp, grp,
                   jax.ShapeDtypeStruct((8, D), F32)],
        scratch_shapes=[pltpu.VMEM((NG, HALO, CG), F32), pltpu.VMEM((8, D), F32)],
        compiler_params=_cparams(1, VMEM_BIG),
    )(dx1, *zs, *zs, y_pool, y_conv, pool_w, pool_scale, wmix, conv_w, wmix, wmix)


def bwd_in(dz, w_in_pieces, dx1, x, g1, tm):
    t = x.shape[0]

    def body(dz_ref, w0_ref, w1_ref, w2_ref, dx1_ref, x_ref, g_ref, gx_ref, gn_ref):
        @pl.when(pl.program_id(0) == 0)
        def _():
            gn_ref[...] = jnp.zeros_like(gn_ref)

        dh = None
        for j in range(NDEV):
            for q, w_ref in enumerate((w0_ref, w1_ref, w2_ref)):
                part = _dot_nt(dz_ref[3 * j + q], w_ref[j])
                dh = part if dh is None else dh + part
        xv = x_ref[...]
        inv = _rms_inv(xv)
        xn = xv * inv
        gn_ref[0:1, :] += jnp.sum(dh * xn, axis=0, keepdims=True)
        gx_ref[...] = dx1_ref[...] + _rms_bwd(dh, xn, inv, g_ref[...])

    tile = lambda i: (i, 0)
    return pl.pallas_call(
        body, name="bwd_in", grid=(t // tm,),
        in_specs=[pl.BlockSpec((NZT, tm, CG), lambda i: (0, i, 0))]
                 + [pl.BlockSpec((NDEV, D, CG), lambda i: (0, 0, 0))] * 3
                 + [pl.BlockSpec((tm, D), tile), pl.BlockSpec((tm, D), tile), pl.BlockSpec((1, D), lambda i: (0, 0))],
        out_specs=[pl.BlockSpec((tm, D), tile), pl.BlockSpec((8, D), lambda i: (0, 0))],
        out_shape=[jax.ShapeDtypeStruct((t, D), F32), jax.ShapeDtypeStruct((8, D), F32)],
        compiler_params=_cparams(1, VMEM_BIG),
    )(dz, *w_in_pieces, dx1, x, g1)


def _slot(j):
    return j % 2, j // 2


def wgrad_cols(at, b, q, name):
    m, t = at.shape
    width = b.shape[3]

    def body(a_ref, b_ref, o_ref):
        o_ref[...] = _dot(a_ref[...], b_ref[...])

    return pl.pallas_call(
        body, name=name, grid=(NDEV,),
        in_specs=[pl.BlockSpec((m, t), lambda j: (0, 0)),
                  pl.BlockSpec((None, None, t, width), lambda j: (j, q, 0, 0))],
        out_specs=pl.BlockSpec((None, None, m, width), lambda j: (j % 2, j // 2, 0, 0)),
        out_shape=jax.ShapeDtypeStruct((2, 4, m, width), F32),
        compiler_params=_cparams(1, VMEM_BIG),
    )(at, b)


def wgrad_down(actt, dx2b):
    t = dx2b.shape[0]

    def body(a_ref, b_ref, o_ref):
        r = _dot(a_ref[...], b_ref[...])
        o_ref[0] = r[:SH_DN]
        o_ref[1] = r[SH_DN:]

    return pl.pallas_call(
        body, name="wgrad_down", grid=(NCH,),
        in_specs=[pl.BlockSpec((None, SH_UP, t), lambda k: (k, 0, 0)), pl.BlockSpec((t, D), lambda k: (0, 0))],
        out_specs=pl.BlockSpec((2, None, SH_DN, D), lambda k: (0, k, 0, 0)),
        out_shape=jax.ShapeDtypeStruct((2, 4, SH_DN, D), F32),
        compiler_params=_cparams(1, VMEM_BIG),
    )(actt, dx2b)


def wgrad_square(a, b, name, tk):
    t = a.shape[0]

    def body(a_ref, b_ref, o_ref, acc_ref):
        kt = pl.program_id(0)

        @pl.when(kt == 0)
        def _():
            acc_ref[...] = jnp.zeros_like(acc_ref)

        acc_ref[...] += _dot_tn(a_ref[...], b_ref[...].astype(BF16))

        @pl.when(kt == pl.num_programs(0) - 1)
        def _():
            for j in range(NDEV):
                cc, xy = _slot(j)
                o_ref[cc, xy] = acc_ref[j * 128:(j + 1) * 128]

    return pl.pallas_call(
        body, name=name, grid=(t // tk,),
        in_specs=[pl.BlockSpec((tk, D), lambda k: (k, 0)), pl.BlockSpec((tk, D), lambda k: (k, 0))],
        out_specs=pl.BlockSpec((2, 4, 128, D), lambda k: (0, 0, 0, 0)),
        out_shape=jax.ShapeDtypeStruct((2, 4, 128, D), F32),
        scratch_shapes=[pltpu.VMEM((D, D), F32)],
        compiler_params=_cparams(1, VMEM_BIG),
    )(a, b)


def wgrad_pool(p, dpw, tk):
    t = p.shape[1]

    def body(a_ref, b_ref, o_ref):
        @pl.when(pl.program_id(0) == 0)
        def _():
            o_ref[...] = jnp.zeros_like(o_ref)

        for g in range(NG):
            o_ref[g] += _dot_tn(a_ref[g], b_ref[g])

    return pl.pallas_call(
        body, name="wgrad_pool", grid=(t // tk,),
        in_specs=[pl.BlockSpec((NG, tk, CG), lambda k: (0, k, 0))] * 2,
        out_specs=pl.BlockSpec((NG, CG, CG), lambda k: (0, 0, 0)),
        out_shape=jax.ShapeDtypeStruct((NG, CG, CG), F32),
        compiler_params=_cparams(1, VMEM_BIG),
    )(p, dpw)


def _adamw(w, g, m, v):
    m = ADAM_B1 * m + (1.0 - ADAM_B1) * g
    v = ADAM_B2 * v + (1.0 - ADAM_B2) * (g * g)
    m_hat = m / (1.0 - ADAM_B1 ** ADAM_STEP)
    v_hat = v / (1.0 - ADAM_B2 ** ADAM_STEP)
    delta = -ADAM_LR * (m_hat / (jnp.sqrt(v_hat) + ADAM_EPS) + ADAM_WD * w)
    return delta, m, v


def _row_block(r):
    return 512 if r % 512 == 0 else r


def chip_partial(place, gs, from_sibling, name):
    n = len(gs)
    shapes = [g.shape[2:] for g in gs]

    def body(place_ref, *refs):
        for g_ref, s_ref, o_ref in zip(refs[:n], refs[n:2 * n], refs[2 * n:]):
            o_ref[...] = (g_ref[...] + s_ref[...]).astype(BF16)

    def slot(rc):
        return pl.BlockSpec((None,) + rc, lambda k, pr: (pr[1] ^ (k + 1), 0, 0))

    return pl.pallas_call(
        body, name=name,
        grid_spec=pltpu.PrefetchScalarGridSpec(
            num_scalar_prefetch=1, grid=(3,),
            in_specs=[pl.BlockSpec((None, None) + rc, lambda k, pr: (pr[0], pr[1] ^ (k + 1), 0, 0)) for rc in shapes]
                     + [slot(rc) for rc in shapes],
            out_specs=[slot(rc) for rc in shapes]),
        out_shape=[jax.ShapeDtypeStruct((4,) + rc, BF16) for rc in shapes],
        compiler_params=_cparams(1, VMEM_BIG),
    )(place, *gs, *from_sibling)


def finish_adamw(place, gs, from_sibling, from_chips, w, m, v, name, transposed=False):
    n = len(gs)
    r = gs[0].shape[2]
    widths = [g.shape[3] for g in gs]
    c = sum(widths)
    br = _row_block(r)

    def body(place_ref, *refs):
        g_refs, s_refs, c_refs = refs[:n], refs[n:2 * n], refs[2 * n:5 * n]
        w_ref, m_ref, v_ref, og_ref, od_ref, om_ref, ov_ref = refs[5 * n:]
        cols = []
        for q in range(n):
            grad = g_refs[q][...] + s_refs[q][...]
            for k in range(3):
                grad = grad + c_refs[3 * q + k][...].astype(F32)
            cols.append(grad)
        grad = cols[0] if n == 1 else jnp.concatenate(cols, axis=1)
        if transposed:
            grad = grad.T
        og_ref[...] = grad
        od_ref[...], om_ref[...], ov_ref[...] = _adamw(w_ref[...], grad, m_ref[...], v_ref[...])

    def other(k, cq):
        return pl.BlockSpec((None, br, cq), lambda i, pr: (pr[1] ^ k, i, 0))

    row = pl.BlockSpec((c, br), lambda i, pr: (0, i)) if transposed else pl.BlockSpec((br, c), lambda i, pr: (i, 0))
    out = jax.ShapeDtypeStruct((c, r) if transposed else (r, c), F32)
    in_specs = [pl.BlockSpec((None, None, br, cq), lambda i, pr: (pr[0], pr[1], i, 0)) for cq in widths]
    in_specs += [pl.BlockSpec((None, br, cq), lambda i, pr: (pr[1], i, 0)) for cq in widths]
    in_specs += [other(k, cq) for cq in widths for k in (1, 2, 3)]
    return pl.pallas_call(
        body, name=name,
        grid_spec=pltpu.PrefetchScalarGridSpec(
            num_scalar_prefetch=1, grid=(r // br,), in_specs=in_specs + [row, row, row], out_specs=[row] * 4),
        out_shape=[out] * 4,
        compiler_params=_cparams(1, VMEM_BIG),
    )(place, *gs, *from_sibling, *[fc for fc in from_chips for _ in range(3)], w, m, v)


def finish_adamw_group(place, items, name):
    n = len(items)

    def body(place_ref, *refs):
        ins, outs = refs[:8 * n], refs[8 * n:]
        for i in range(n):
            g_ref, s_ref, c1_ref, c2_ref, c3_ref, w_ref, m_ref, v_ref = ins[8 * i:8 * i + 8]
            grad = g_ref[...] + s_ref[...]
            for c_ref in (c1_ref, c2_ref, c3_ref):
                grad = grad + c_ref[...].astype(F32)
            og_ref, od_ref, om_ref, ov_ref = outs[4 * i:4 * i + 4]
            og_ref[...] = grad
            od_ref[...], om_ref[...], ov_ref[...] = _adamw(w_ref[...], grad, m_ref[...], v_ref[...])

    def chip_slot(rc, k):
        return pl.BlockSpec((None,) + rc, lambda i, pr: (pr[1] ^ k, 0, 0))

    in_specs, out_specs, out_shape, args = [], [], [], []
    for g, s, fc, w, m, v in items:
        rc = g.shape[2:]
        whole = pl.BlockSpec(rc, lambda i, pr: (0, 0))
        in_specs += [pl.BlockSpec((None, None) + rc, lambda i, pr: (pr[0], pr[1], 0, 0)), chip_slot(rc, 0),
                     chip_slot(rc, 1), chip_slot(rc, 2), chip_slot(rc, 3), whole, whole, whole]
        out_specs += [whole] * 4
        out_shape += [jax.ShapeDtypeStruct(rc, F32)] * 4
        args += [g, s, fc, fc, fc, w, m, v]
    res = pl.pallas_call(
        body, name=name,
        grid_spec=pltpu.PrefetchScalarGridSpec(num_scalar_prefetch=1, grid=(1,), in_specs=in_specs,
                                               out_specs=out_specs),
        out_shape=out_shape, compiler_params=_cparams(1, VMEM_BIG),
    )(place, *args)
    return [res[4 * i:4 * i + 4] for i in range(n)]


def adamw_small(items):
    n = len(items)

    def body(*refs):
        ins, outs = refs[:4 * n], refs[4 * n:]
        for i in range(n):
            w, g, m, v = (r[...] for r in ins[4 * i:4 * i + 4])
            outs[3 * i][...], outs[3 * i + 1][...], outs[3 * i + 2][...] = _adamw(w, g, m, v)

    out = [jax.ShapeDtypeStruct(it[0].shape, F32) for it in items for _ in range(3)]
    res = pl.pallas_call(body, name="adamw_small", out_shape=out)(*[a for it in items for a in it])
    return [res[3 * i:3 * i + 3] for i in range(n)]


def kernel(x, norm_mix, w_in, pool_w, pool_scale, w_pool_proj, conv_w, w_conv_out, w_o, norm_ffn, w_up, ffn_conv_w, ffn_conv_b, w_down, norm_final, loss_target, m_norm_mix, m_w_in, m_pool_w, m_pool_scale, m_w_pool_proj, m_conv_w, m_w_conv_out, m_w_o, m_norm_ffn, m_w_up, m_ffn_conv_w, m_ffn_conv_b, m_w_down, m_norm_final, v_norm_mix, v_w_in, v_pool_w, v_pool_scale, v_w_pool_proj, v_conv_w, v_w_conv_out, v_w_o, v_norm_ffn, v_w_up, v_ffn_conv_w, v_ffn_conv_b, v_w_down, v_norm_final):
    nb, seq, _ = x.shape
    t = nb * seq
    tm_in = min(TM_IN, t)
    tm_mix = min(TM_MIX, seq)
    tm_ffn = min(TM_FFN, seq)
    tk = min(TK_WGRAD, t)
    xt = x.reshape(t, D)
    tgt = loss_target.reshape(t, D)
    xi, yi, ci = _pos()
    me = 4 * xi + 2 * yi + ci
    place = jnp.stack([ci, 2 * xi + yi]).astype(jnp.int32)

    tie = lax.optimization_barrier
    w_in_b = w_in[0].astype(BF16)
    w_in_g = [all_gather_blocks([w_in_b[:, q * CG:(q + 1) * CG]], f"all_gather_w_in_{q}", 0)[0] for q in range(3)]
    taps = (jnp.pad(conv_w[0], ((0, 5), (0, D - 128))) + jnp.pad(ffn_conv_w[0], ((3, 2), (0, D - SH_UP))))
    taps_g = _exchange_small(taps, "all_gather_taps")
    mix_shard = jnp.concatenate(
        [w_pool_proj[0], w_conv_out[0], w_o[0], pool_w[0].reshape(NG * 32, CG)], axis=1).astype(BF16)
    conv_w_f = taps_g[:, 0:3, :128].transpose(1, 0, 2).reshape(3, D)
    fcw_f = taps_g[:, 3:6, :SH_UP]
    fcb_f = ffn_conv_b.reshape(NDEV, 1, SH_UP)
    mix_shard, conv_w_f, fcw_f, fcb_f = tie((mix_shard, conv_w_f, fcw_f, fcb_f))
    wmix_g, = all_gather_blocks([mix_shard], "all_gather_w_mix", 0)
    ffn_shards, w_in_g[0] = tie(([w_up[0].astype(BF16), w_down[0].astype(BF16)], w_in_g[0]))
    w_up_g, = all_gather_blocks(ffn_shards[:1], "all_gather_w_up", 0)
    w_dn_g, = all_gather_blocks(ffn_shards[1:], "all_gather_w_down", 0)
    w_dn_f = w_dn_g.reshape(NCH, SH_UP, D)
    gfin = norm_final.reshape(1, D)

    zs, h1 = fwd_in(xt, norm_mix, w_in_g, tm_in)
    wmix_g, zs = tie((wmix_g, zs))
    wmix = wmix_g.reshape(D, MIX_COLS)
    pool_w_f = wmix_g[:, :, 3 * D:].reshape(NDEV, NG, 32, CG).transpose(1, 0, 2, 3).reshape(NG, CG, CG)
    x1, y_pool, y_conv = fwd_mix(zs, xt, pool_w_f, pool_scale, conv_w_f, wmix, tm_mix, seq)
    up, pre, act_tok, act, h2 = fwd_up(x1, norm_ffn, w_up_g, fcw_f, fcb_f, tm_ffn, seq)
    dx2, dx2b, ffn_vec = fwd_down(x1, act_tok, w_dn_f, gfin, tgt, min(TM_IN, t))

    def to_sibling(full, tag):
        return reduce_scatter_d2d(full, "reduce_scatter_d2d_" + tag, 1)

    def partials(full, from_sib, tag):
        return chip_partial(place, full, from_sib, "chip_partial_" + tag)

    def to_chips(parts, tag):
        return reduce_scatter_ici(parts, "reduce_scatter_ici_" + tag, 2)

    def finish(nm, gs, from_sib, from_chips, wmv, transposed=False):
        rc = (gs[0].shape[2], sum(g.shape[3] for g in gs))
        wmv2 = [a.reshape(rc).T if transposed else a.reshape(rc) for a in wmv]
        outs = finish_adamw(place, gs, from_sib, from_chips, *wmv2, "adamw_" + nm, transposed)
        return [(o.T if transposed else o).reshape(wmv[0].shape) for o in outs]

    def after(x, dep):
        return tie((x, dep))[0]

    big = {}
    d_up, dx1, g_ffn_vec, g_nffn = bwd_ffn(dx2, x1, norm_ffn, up, pre, w_up_g, fcw_f, w_dn_f, tm_ffn, seq)
    gw_up = wgrad_cols(h2, d_up.reshape(NDEV, 1, t, SH_UP), 0, "wgrad_up")
    sib_up = to_sibling([gw_up], "w_up")
    gw_dn = wgrad_down(act, after(dx2b, gw_up))
    sib_dn = to_sibling([after(gw_dn, sib_up)], "w_down")
    dx1, part_up = tie((dx1, partials([gw_up], sib_up, "w_up")))
    chips_up = to_chips(part_up, "w_up")
    dz, merged, p2, u, dyp, dyc, p, dpw, g_mix_vec = bwd_mix(
        dx1, zs, y_pool, y_conv, pool_w_f, pool_scale, conv_w_f, wmix, tm_mix, seq)
    merged, part_dn = tie((merged, partials([gw_dn], sib_dn, "w_down")))
    chips_dn = to_chips(part_dn, "w_down")
    gw_o = wgrad_square(merged, dx1, "wgrad_o", tk)
    gw_pp = wgrad_square(p2, dyp, "wgrad_pool_proj", tk)
    gw_co = wgrad_square(u, dyc, "wgrad_conv_out", tk)
    gw_pool = wgrad_pool(p, dpw, tk).reshape(NG, 4, 2, 32, CG).transpose(2, 1, 0, 3, 4).reshape(2, 4, NG * 32, CG)
    dz8 = dz.reshape(NDEV, 3, t, CG)
    gw_in, sib_in, chips_in = [None] * 3, [None] * 3, [None] * 3
    sib_a = to_sibling(after([gw_o, gw_pp], (chips_up, gw_pool, chips_dn)), "mix_a")
    sib_b = to_sibling(after([gw_co, gw_pool], sib_a), "mix_b")
    gw_in[0] = wgrad_cols(h1, dz8, 0, "wgrad_in_0")
    h1, part_a, part_b = tie((h1, partials([gw_o, gw_pp], sib_a, "mix_a"),
                              partials([gw_co, gw_pool], sib_b, "mix_b")))
    chips_a = to_chips(after(part_a, chips_dn), "mix_a")
    chips_b = to_chips(part_b, "mix_b")
    sib_in[0] = to_sibling(after([gw_in[0]], sib_b), "w_in_0")
    gw_in[1] = wgrad_cols(h1, dz8, 1, "wgrad_in_1")
    h1, part_in0, gw_in[1] = tie((h1, partials([gw_in[0]], sib_in[0], "w_in_0"), gw_in[1]))
    chips_in[0] = to_chips(part_in0, "w_in_0")
    sib_in[1] = to_sibling(after([gw_in[1]], sib_in[0]), "w_in_1")
    h1, big["w_down"], big["w_up"] = tie((
        h1, finish("w_down", [gw_dn], sib_dn, chips_dn, (w_down, m_w_down, v_w_down)),
        finish("w_up", [gw_up], sib_up, chips_up, (w_up, m_w_up, v_w_up), transposed=True)))
    gw_in[2] = wgrad_cols(h1, dz8, 2, "wgrad_in_2")
    sib_in[2] = to_sibling(after([gw_in[2]], (chips_a, chips_b, chips_in[0])), "w_in_2")
    mix = {"w_o": (gw_o, sib_a[0], chips_a[0], w_o, m_w_o, v_w_o),
           "w_pool_proj": (gw_pp, sib_a[1], chips_a[1], w_pool_proj, m_w_pool_proj, v_w_pool_proj),
           "w_conv_out": (gw_co, sib_b[0], chips_b[0], w_conv_out, m_w_conv_out, v_w_conv_out),
           "pool_w": (gw_pool, sib_b[1], chips_b[1], pool_w, m_pool_w, v_pool_w)}
    mix_out = finish_adamw_group(
        place, [it[:3] + tuple(a.reshape(it[0].shape[2:]) for a in it[3:]) for it in mix.values()], "adamw_mixer")
    sib_in[2], mix_out = tie((sib_in[2], mix_out))
    for nm, outs in zip(mix, mix_out):
        big[nm] = [o.reshape(mix[nm][3].shape) for o in outs]
    dx1, part_in12 = tie((dx1, partials(gw_in[1:], sib_in[1] + sib_in[2], "w_in_12")))
    chips_in[1] = to_chips(part_in12[:1], "w_in_1")
    chips_in[2] = to_chips(part_in12[1:], "w_in_2")
    small_g, = all_gather_blocks(
        [after(jnp.concatenate([g_mix_vec, g_nffn, ffn_vec, g_ffn_vec.reshape(8 * NDEV, D)], axis=0), sib_in[2])],
        "all_gather_small", 0)
    grad_x, g_nmix = bwd_in(dz, w_in_g, dx1, xt, norm_mix, min(TM_BWD_IN, t))
    grad_x, chips_in = tie((grad_x, chips_in))
    nmix_g, = all_gather_blocks([g_nmix], "all_gather_norm_mix", 0)
    big["w_in"] = finish("w_in", gw_in, [s[0] for s in sib_in], [c[0] for c in chips_in], (w_in, m_w_in, v_w_in))

    red, = sum_blocks([small_g], "sum_small")
    red_n, = sum_blocks([after(nmix_g, (big["w_in"], red))], "sum_norm_mix")
    g_norm_mix, g_pool_scale, g_norm_ffn = red_n[0:1], red[0:1], red[8:9]
    g_conv_w = lax.dynamic_slice(red, (1, me * 128), (3, 128))
    g_norm_final = red[16]
    loss = red[17, 0]
    g_fcb = red[24:].reshape(NDEV, 8, D)[:, 0, :SH_UP].reshape(1, FF2)
    g_fcw = lax.dynamic_slice(red, (25 + 8 * me, 0), (3, SH_UP))
    grads = {"norm_mix": g_norm_mix, "pool_scale": g_pool_scale, "norm_ffn": g_norm_ffn, "norm_final": g_norm_final,
             "ffn_conv_b": g_fcb, "conv_w": g_conv_w.reshape(1, 3, 128), "ffn_conv_w": g_fcw.reshape(1, 3, SH_UP)}
    small_wmv = {"norm_mix": (norm_mix, m_norm_mix, v_norm_mix), "pool_scale": (pool_scale, m_pool_scale, v_pool_scale),
                 "norm_ffn": (norm_ffn, m_norm_ffn, v_norm_ffn), "norm_final": (norm_final, m_norm_final, v_norm_final),
                 "ffn_conv_b": (ffn_conv_b, m_ffn_conv_b, v_ffn_conv_b), "conv_w": (conv_w, m_conv_w, v_conv_w),
                 "ffn_conv_w": (ffn_conv_w, m_ffn_conv_w, v_ffn_conv_w)}
    small_names = list(small_wmv)
    flat2 = lambda a: a.reshape(1, -1) if a.ndim == 1 else (a.transpose(1, 0, 2) if a.ndim == 3 else a)
    unflat = lambda o, like: o.transpose(1, 0, 2) if like.ndim == 3 else o.reshape(like.shape)
    small_out = adamw_small([(flat2(small_wmv[nm][0]), flat2(grads[nm]), flat2(small_wmv[nm][1]),
                              flat2(small_wmv[nm][2])) for nm in small_names])
    small = {nm: [unflat(o, small_wmv[nm][0]) for o in outs] for nm, outs in zip(small_names, small_out)}

    order = ["norm_mix", "w_in", "pool_w", "pool_scale", "w_pool_proj", "conv_w", "w_conv_out", "w_o", "norm_ffn",
             "w_up", "ffn_conv_w", "ffn_conv_b", "w_down", "norm_final"]
    out = [loss, grad_x.reshape(nb, seq, D)]
    out += [big[nm][0] if nm in big else grads[nm] for nm in order]
    for idx in range(3):
        out += [big[nm][idx + 1] if nm in big else small[nm][idx] for nm in order]
    return tuple(out)
```

```python
import jax
import jax.numpy as jnp
from jax import lax
from jax.experimental import pallas as pl
from jax.experimental.pallas import tpu as pltpu
from jax.experimental.pallas import tpu_sc as plsc

F32 = jnp.float32
BF16 = jnp.bfloat16

NDEV = 8
D = 1024
NG = 4
CG = 256
WINS = (2, 4, 8, 16)
DIN = 6 * D
SH_IN = DIN // NDEV
NZT = DIN // CG
FF2 = 5632
SH_UP = FF2 // NDEV
FF = FF2 // 2
NCH = 4
SH_DN = FF // NDEV
RMS_EPS = 1e-6
HALO = 16

ADAM_LR = 0.001
ADAM_B1 = 0.9
ADAM_B2 = 0.999
ADAM_EPS = 1e-08
ADAM_WD = 0.01
ADAM_STEP = 10

TM_IN = 512
TM_BWD_IN = 256
TM_MIX = 256
TM_FFN = 256
TK_WGRAD = 1024
MIX_POOL_PROJ, MIX_CONV_OUT, MIX_O = 0, 1, 2
MIX_COLS = 3 * D + CG
VMEM_BIG = 56 * 1024 * 1024
MESH = pl.DeviceIdType.MESH
ANY = pl.BlockSpec(memory_space=pl.ANY)


def _cparams(n_axes, vmem=None):
    return pltpu.CompilerParams(dimension_semantics=("arbitrary",) * n_axes, vmem_limit_bytes=vmem)


def _dot(a, b):
    return jnp.dot(a, b, preferred_element_type=F32)


def _dot_nt(a, b):
    return lax.dot_general(a, b, (((1,), (1,)), ((), ())), preferred_element_type=F32)


def _dot_tn(a, b):
    return lax.dot_general(a, b, (((0,), (0,)), ((), ())), preferred_element_type=F32)


def _shift_down(ext, s, lead):
    return pltpu.roll(ext, s, 0)[lead:]


def _shift_up(ext, s, tm):
    n = ext.shape[0]
    return pltpu.roll(ext, n - s, 0)[:tm]


def _rms_inv(x):
    return lax.rsqrt(jnp.mean(x * x, axis=-1, keepdims=True) + RMS_EPS)


def _rms_bwd(dh, xn, inv, g):
    dxn = dh * g
    return inv * (dxn - xn * jnp.mean(dxn * xn, axis=-1, keepdims=True))


def _pos():
    return lax.axis_index("x"), lax.axis_index("y"), lax.axis_index("c")


def _handshake(peers):
    barrier = pltpu.get_barrier_semaphore()
    for peer in peers:
        pl.semaphore_signal(barrier, inc=1, device_id=peer, device_id_type=MESH)
    pl.semaphore_wait(barrier, len(peers))


def _sequencer(body, out_type, n_sems, name, collective_id):
    return pl.kernel(
        body, out_type=out_type, mesh=plsc.ScalarSubcoreMesh(axis_name="sequencer", num_cores=1), name=name,
        scratch_types=[pltpu.SemaphoreType.DMA((n_sems,)), pltpu.SemaphoreType.DMA((n_sems,))],
        compiler_params=pltpu.CompilerParams(collective_id=collective_id))


def all_gather_blocks(shards, name, collective_id):
    n = len(shards)

    def body(*refs):
        ins, outs = refs[:n], refs[n:2 * n]
        send_sems, recv_sems = refs[2 * n:]
        x, y, c = _pos()
        me, sibling = (x, y, c), (x, y, 1 - c)
        first_chip, second_chip, diagonal = (x ^ (1 - c), y ^ c), (x ^ c, y ^ (1 - c)), (1 - x, 1 - y)
        first, second = (*first_chip, c), (*second_chip, c)
        _handshake([sibling, first, second])

        def copy(w, k, block, to, src=None):
            slot = outs[w].at[4 * block[0] + 2 * block[1] + block[2]]
            return pltpu.make_async_remote_copy(
                src_ref=slot if src is None else src, dst_ref=slot,
                send_sem=send_sems.at[8 * w + k], recv_sem=recv_sems.at[8 * w + k], device_id=to, device_id_type=MESH)

        mine, sent = [], []
        for w in range(n):
            m = pltpu.make_async_copy(ins[w], outs[w].at[4 * x + 2 * y + c], send_sems.at[8 * w + 7])
            m.start()
            mine.append(m)
            sent += [copy(w, k, me, to, src=ins[w]) for k, to in enumerate((sibling, first, second))]
        for cp in sent:
            cp.start()
        for k, chip in ((1, first_chip), (2, second_chip), (3, diagonal)):
            for w in range(n):
                copy(w, k, (*chip, c), me).wait_recv()
                onward = [copy(w, 3 + k, (*chip, c), sibling)] + ([copy(w, 3, (*chip, c), second)] if k == 1 else [])
                for cp in onward:
                    cp.start()
                sent += onward
        for w in range(n):
            copy(w, 0, sibling, me).wait_recv()
            for k, chip in ((4, second_chip), (5, first_chip), (6, diagonal)):
                copy(w, k, (*chip, 1 - c), me).wait_recv()
        for cp in sent:
            cp.wait_send()
        for m in mine:
            m.wait()

    out = [jax.ShapeDtypeStruct((NDEV,) + s.shape, s.dtype) for s in shards]
    return _sequencer(body, out, 8 * n, name, collective_id)(*shards)


def _exchange_small(v, name):
    rows = v.shape[0]

    def body(v_ref, out_ref, slots, send_sems, recv_sems, local_sem):
        x, y, c = _pos()
        me = 4 * x + 2 * y + c
        mine = pltpu.make_async_copy(v_ref, slots.at[me], local_sem)
        mine.start()
        offs = [(dx, dy, dc) for dx in (0, 1) for dy in (0, 1) for dc in (0, 1)][1:]

        def copy(k, src_slot, to):
            return pltpu.make_async_remote_copy(
                src_ref=v_ref, dst_ref=slots.at[src_slot], send_sem=send_sems.at[k], recv_sem=recv_sems.at[k],
                device_id=to, device_id_type=MESH)

        sends = []
        for k, (dx, dy, dc) in enumerate(offs):
            cp = copy(k, me, (x ^ dx, y ^ dy, c ^ dc))
            cp.start()
            sends.append(cp)
        for k, (dx, dy, dc) in enumerate(offs):
            copy(k, 4 * (x ^ dx) + 2 * (y ^ dy) + (c ^ dc), (x, y, c)).wait_recv()
        for cp in sends:
            cp.wait_send()
        mine.wait()
        out_ref[...] = slots[...]

    vmem = pl.BlockSpec(memory_space=pltpu.VMEM)
    return pl.pallas_call(
        body, name=name, out_shape=jax.ShapeDtypeStruct((NDEV, rows, D), F32), in_specs=[vmem], out_specs=vmem,
        scratch_shapes=[pltpu.VMEM((NDEV, rows, D), F32), pltpu.SemaphoreType.DMA((7,)),
                        pltpu.SemaphoreType.DMA((7,)), pltpu.SemaphoreType.DMA],
    )(v)


def sum_blocks(gathered, name):
    def body(*refs):
        for g_ref, o_ref in zip(refs[:len(gathered)], refs[len(gathered):]):
            acc = g_ref[0]
            for d in range(1, NDEV):
                acc = acc + g_ref[d]
            o_ref[...] = acc

    vmem = pl.BlockSpec(memory_space=pltpu.VMEM)
    return pl.pallas_call(
        body, name=name, out_shape=[jax.ShapeDtypeStruct(g.shape[1:], F32) for g in gathered],
        in_specs=[vmem] * len(gathered), out_specs=[vmem] * len(gathered),
    )(*gathered)


def reduce_scatter_d2d(grads, name, collective_id):
    n = len(grads)

    def body(*refs):
        ins, outs = refs[:n], refs[n:2 * n]
        send_sems, recv_sems = refs[2 * n:]
        x, y, c = _pos()
        _handshake([(x, y, 1 - c)])
        cps = []
        for w in range(n):
            cp = pltpu.make_async_remote_copy(
                src_ref=ins[w].at[1 - c], dst_ref=outs[w], send_sem=send_sems.at[w], recv_sem=recv_sems.at[w],
                device_id=(x, y, 1 - c), device_id_type=MESH)
            cp.start()
            cps.append(cp)
        for cp in cps:
            cp.wait_recv()
        for cp in cps:
            cp.wait_send()

    out = [jax.ShapeDtypeStruct(g.shape[1:], F32) for g in grads]
    return _sequencer(body, out, n, name, collective_id)(*grads)


def reduce_scatter_ici(parts, name, collective_id):
    n = len(parts)

    def body(*refs):
        ins, outs = refs[:n], refs[n:2 * n]
        send_sems, recv_sems = refs[2 * n:]
        x, y, c = _pos()
        offs = [(1, 0), (0, 1), (1, 1)]
        _handshake([(x ^ dx, y ^ dy, c) for dx, dy in offs])
        cps = []
        for w in range(n):
            for k, (dx, dy) in enumerate(offs):
                ox, oy = x ^ dx, y ^ dy
                cp = pltpu.make_async_remote_copy(
                    src_ref=ins[w].at[2 * ox + oy], dst_ref=outs[w].at[2 * x + y],
                    send_sem=send_sems.at[3 * w + k], recv_sem=recv_sems.at[3 * w + k],
                    device_id=(ox, oy, c), device_id_type=MESH)
                cp.start()
                cps.append((cp, w, k, ox, oy))
        for cp, w, k, ox, oy in cps:
            pltpu.make_async_remote_copy(
                src_ref=ins[w].at[2 * ox + oy], dst_ref=outs[w].at[2 * ox + oy],
                send_sem=send_sems.at[3 * w + k], recv_sem=recv_sems.at[3 * w + k],
                device_id=(ox, oy, c), device_id_type=MESH).wait_recv()
        for cp, *_ in cps:
            cp.wait_send()

    out = [jax.ShapeDtypeStruct(p.shape, BF16) for p in parts]
    return _sequencer(body, out, 3 * n, name, collective_id)(*parts)


def fwd_in(x, g1, w_in_pieces, tm):
    t = x.shape[0]
    tile = lambda i: (i, 0)
    w_spec = pl.BlockSpec((NDEV, D, CG), lambda i: (0, 0, 0))
    z_spec = pl.BlockSpec((NDEV, tm, CG), lambda i: (0, i, 0))
    z_shape = jax.ShapeDtypeStruct((NDEV, t, CG), BF16)

    def cost(other_bytes, transcendentals):
        return pl.CostEstimate(flops=2 * t * D * NDEV * CG, transcendentals=transcendentals,
                               bytes_accessed=other_bytes + 2 * D * NDEV * CG + 2 * t * NDEV * CG)

    def first(x_ref, g_ref, w_ref, z_ref, h_ref, ht_ref):
        xf = x_ref[...]
        h = (xf * _rms_inv(xf) * g_ref[...]).astype(BF16)
        h_ref[...] = h
        ht_ref[...] = h.T
        for j in range(NDEV):
            z_ref[j] = _dot(h, w_ref[j]).astype(BF16)

    z0, h, ht = pl.pallas_call(
        first, name="fwd_in_0", grid=(t // tm,),
        in_specs=[pl.BlockSpec((tm, D), tile), pl.BlockSpec((1, D), lambda i: (0, 0)), w_spec],
        out_specs=[z_spec, pl.BlockSpec((tm, D), tile), pl.BlockSpec((D, tm), lambda i: (0, i))],
        out_shape=[z_shape, jax.ShapeDtypeStruct((t, D), BF16), jax.ShapeDtypeStruct((D, t), BF16)],
        compiler_params=_cparams(1, VMEM_BIG), cost_estimate=cost(8 * t * D + 4 * D, t),
    )(x, g1, w_in_pieces[0])
    zs = [z0]
    for q in (1, 2):
        h, zs[-1] = lax.optimization_barrier((h, zs[-1]))

        def later(h_ref, w_ref, z_ref):
            hb = h_ref[...]
            for j in range(NDEV):
                z_ref[j] = _dot(hb, w_ref[j]).astype(BF16)

        zs.append(pl.pallas_call(
            later, name=f"fwd_in_{q}", grid=(t // tm,),
            in_specs=[pl.BlockSpec((tm, D), tile), w_spec], out_specs=z_spec, out_shape=z_shape,
            compiler_params=_cparams(1, VMEM_BIG), cost_estimate=cost(2 * t * D, 0),
        )(h, w_in_pieces[q]))
    return zs, ht


def _z_tile(z_refs, n):
    return z_refs[n % 3][n // 3]


def _pool_tile(z, zh, win, keep_hist, cnt):
    zt = z.astype(F32)
    ext = jnp.concatenate([zh.astype(F32) * keep_hist, zt], axis=0)
    s, sh = ext, 1
    while sh < win:
        s = s + pltpu.roll(s, sh, 0)
        sh *= 2
    return s[HALO:] / cnt - zt


def _conv_taps(ext, cur, w_ref, lanes, lead):
    x1 = _shift_down(ext, 1, lead)
    x2 = _shift_down(ext, 2, lead)
    out = w_ref[2:3, lanes] * cur + w_ref[1:2, lanes] * x1 + w_ref[0:1, lanes] * x2
    return out, x1, x2


def fwd_mix(zs, x, pool_w, pool_scale, conv_w, wmix, tm, seq):
    t = x.shape[0]
    tps = seq // tm
    hb = tm // HALO

    def body(z0_ref, z1_ref, z2_ref, zh0_ref, zh1_ref, zh2_ref, x_ref, pw_ref, ps_ref, wpp_ref, cw_ref, wco_ref,
             wo_ref, x1_ref, yp_ref, yc_ref):
        z_refs, zh_refs = (z0_ref, z1_ref, z2_ref), (zh0_ref, zh1_ref, zh2_ref)
        it = pl.program_id(0) % tps
        keep_hist = jnp.where(it == 0, 0.0, 1.0)
        pos = it * tm + lax.broadcasted_iota(jnp.int32, (tm, 1), 0)
        p2 = []
        for g, win in enumerate(WINS):
            cnt = jnp.minimum(pos + 1, win).astype(F32)
            p = _pool_tile(_z_tile(z_refs, g), _z_tile(zh_refs, g), win, keep_hist, cnt)
            lanes = slice(g * CG, (g + 1) * CG)
            p2.append((_dot(p.astype(BF16), pw_ref[g]) * ps_ref[:, lanes]).astype(BF16))
        y_pool = _dot(jnp.concatenate(p2, axis=1), wpp_ref[...])
        u = []
        for q in range(NG):
            lanes = slice(q * CG, (q + 1) * CG)
            cv = _z_tile(z_refs, 8 + q).astype(F32) * _z_tile(z_refs, 12 + q).astype(F32)
            cvh = _z_tile(zh_refs, 8 + q).astype(F32) * _z_tile(zh_refs, 12 + q).astype(F32) * keep_hist
            cc, _, _ = _conv_taps(jnp.concatenate([cvh, cv], axis=0), cv, cw_ref, lanes, HALO)
            u.append((_z_tile(z_refs, 4 + q).astype(F32) * cc).astype(BF16))
        y_conv = _dot(jnp.concatenate(u, axis=1), wco_ref[...])
        ypb, ycb = y_pool.astype(BF16), y_conv.astype(BF16)
        yp_ref[...] = ypb
        yc_ref[...] = ycb
        merged = []
        for q in range(NG):
            lanes = slice(q * CG, (q + 1) * CG)
            sp = jax.nn.sigmoid(_z_tile(z_refs, 16 + q).astype(F32))
            sc = jax.nn.sigmoid(_z_tile(z_refs, 20 + q).astype(F32))
            merged.append((sp * ypb[:, lanes].astype(F32) + sc * ycb[:, lanes].astype(F32)).astype(BF16))
        x1_ref[...] = x_ref[...] + _dot(jnp.concatenate(merged, axis=1), wo_ref[...])

    def hist(i):
        return jnp.maximum(i * hb - 1, 0)

    const2 = lambda i: (0, 0)
    return pl.pallas_call(
        body, name="fwd_mix", grid=(t // tm,),
        in_specs=[pl.BlockSpec((NDEV, tm, CG), lambda i: (0, i, 0))] * 3
                 + [pl.BlockSpec((NDEV, HALO, CG), lambda i: (0, hist(i), 0))] * 3
                 + [pl.BlockSpec((tm, D), lambda i: (i, 0)),
                    pl.BlockSpec((NG, CG, CG), lambda i: (0, 0, 0)), pl.BlockSpec((1, D), const2),
                    pl.BlockSpec((D, D), lambda i: (0, MIX_POOL_PROJ)), pl.BlockSpec((3, D), const2),
                    pl.BlockSpec((D, D), lambda i: (0, MIX_CONV_OUT)), pl.BlockSpec((D, D), lambda i: (0, MIX_O))],
        out_specs=[pl.BlockSpec((tm, D), lambda i: (i, 0))] * 3,
        out_shape=[jax.ShapeDtypeStruct((t, D), F32), jax.ShapeDtypeStruct((t, D), BF16),
                   jax.ShapeDtypeStruct((t, D), BF16)],
        compiler_params=_cparams(1, VMEM_BIG),
    )(*zs, *zs, x, pool_w, pool_scale, wmix, conv_w, wmix, wmix)


def fwd_up(x1, g2, w_up_g, fcw, fcb, tm, seq):
    t = x1.shape[0]
    tps = seq // tm

    def body(x1_ref, g2_ref, wup_ref, fcw_ref, fcb_ref, up_ref, pre_ref, act_ref, actt_ref, h2t_ref, hist_ref):
        i = pl.program_id(0)
        keep_hist = jnp.where(i % tps == 0, 0.0, 1.0)

        @pl.when(i == 0)
        def _():
            hist_ref[...] = jnp.zeros_like(hist_ref)

        x1v = x1_ref[...]
        h2 = (x1v * _rms_inv(x1v) * g2_ref[...]).astype(BF16)
        h2t_ref[...] = h2.T
        lanes = slice(0, SH_UP)
        for c in range(NCH):
            conv = []
            for s in range(2):
                ub = _dot(h2, wup_ref[s, c]).astype(BF16)
                up_ref[s, c] = ub
                uf = ub.astype(F32)
                ext = jnp.concatenate([hist_ref[s, c] * keep_hist, uf], axis=0)
                hist_ref[s, c] = uf[tm - 8:]
                cc, _, _ = _conv_taps(ext, uf, fcw_ref.at[s, c], lanes, 8)
                conv.append(cc + fcb_ref[s, c])
                pre_ref[s, c] = conv[s].astype(BF16)
            a = (conv[0] * jax.nn.sigmoid(conv[0]) * conv[1]).astype(BF16)
            act_ref[c] = a
            actt_ref[c] = a.T

    tile = lambda i: (i, 0)
    const2 = lambda i: (0, 0)
    whole = lambda i: (0, 0, 0, 0)
    chunks = pl.BlockSpec((2, NCH, tm, SH_UP), lambda i: (0, 0, i, 0))
    return pl.pallas_call(
        body, name="fwd_up", grid=(t // tm,),
        in_specs=[pl.BlockSpec((tm, D), tile), pl.BlockSpec((1, D), const2),
                  pl.BlockSpec((2, NCH, D, SH_UP), whole), pl.BlockSpec((2, NCH, 3, SH_UP), whole),
                  pl.BlockSpec((2, NCH, 1, SH_UP), whole)],
        out_specs=[chunks, chunks, pl.BlockSpec((NCH, tm, SH_UP), lambda i: (0, i, 0)),
                   pl.BlockSpec((NCH, SH_UP, tm), lambda i: (0, 0, i)), pl.BlockSpec((D, tm), lambda i: (0, i))],
        out_shape=[jax.ShapeDtypeStruct((2, NCH, t, SH_UP), BF16), jax.ShapeDtypeStruct((2, NCH, t, SH_UP), BF16),
                   jax.ShapeDtypeStruct((NCH, t, SH_UP), BF16), jax.ShapeDtypeStruct((NCH, SH_UP, t), BF16),
                   jax.ShapeDtypeStruct((D, t), BF16)],
        scratch_shapes=[pltpu.VMEM((2, NCH, 8, SH_UP), F32)],
        compiler_params=_cparams(1, VMEM_BIG),
    )(x1, g2, w_up_g.reshape(2, NCH, D, SH_UP), fcw.reshape(2, NCH, 3, SH_UP), fcb.reshape(2, NCH, 1, SH_UP))


def fwd_down(x1, act, w_dn, gf, tgt, tm):
    t = x1.shape[0]

    def body(x1_ref, act_ref, wdn_ref, gf_ref, tgt_ref, dx2_ref, dx2b_ref, vec_ref):
        @pl.when(pl.program_id(0) == 0)
        def _():
            vec_ref[...] = jnp.zeros_like(vec_ref)

        d = None
        for c in range(NCH):
            part = _dot(act_ref[c], wdn_ref[c])
            d = part if d is None else d + part
        x2 = x1_ref[...] + d
        inv3 = _rms_inv(x2)
        xn = x2 * inv3
        diff = xn * gf_ref[...] - tgt_ref[...]
        dy = diff * (1.0 / D)
        vec_ref[0:1, :] += jnp.sum(dy * xn, axis=0, keepdims=True)
        vec_ref[1:2, :] += 0.5 * jnp.sum(jnp.mean(diff * diff, axis=-1))
        dx2 = _rms_bwd(dy, xn, inv3, gf_ref[...])
        dx2_ref[...] = dx2
        dx2b_ref[...] = dx2.astype(BF16)

    tile = lambda i: (i, 0)
    const2 = lambda i: (0, 0)
    return pl.pallas_call(
        body, name="fwd_down", grid=(t // tm,),
        in_specs=[pl.BlockSpec((tm, D), tile), pl.BlockSpec((NCH, tm, SH_UP), lambda i: (0, i, 0)),
                  pl.BlockSpec((NCH, SH_UP, D), lambda i: (0, 0, 0)), pl.BlockSpec((1, D), const2),
                  pl.BlockSpec((tm, D), tile)],
        out_specs=[pl.BlockSpec((tm, D), tile), pl.BlockSpec((tm, D), tile), pl.BlockSpec((8, D), const2)],
        out_shape=[jax.ShapeDtypeStruct((t, D), F32), jax.ShapeDtypeStruct((t, D), BF16),
                   jax.ShapeDtypeStruct((8, D), F32)],
        compiler_params=_cparams(1, VMEM_BIG),
    )(x1, act, w_dn, gf, tgt)


def bwd_ffn(dx2, x1, g2, up, pre, w_up_g, fcw, w_dn, tm, seq):
    t = x1.shape[0]
    nt = t // tm
    tps = seq // tm

    def body(dx2_ref, x1_ref, g2_ref, up_ref, pre_ref, wup_ref, fcw_ref, wdn_ref,
             dup_ref, dx1_ref, gvec_ref, gn_ref, carry_ref):
        i = pl.program_id(0)
        it = (nt - 1 - i) % tps
        keep_next = jnp.where(it == tps - 1, 0.0, 1.0)

        @pl.when(i == 0)
        def _():
            gvec_ref[...] = jnp.zeros_like(gvec_ref)
            gn_ref[...] = jnp.zeros_like(gn_ref)
            carry_ref[...] = jnp.zeros_like(carry_ref)

        dx2v = dx2_ref[...]
        dxb = dx2v.astype(BF16)
        lanes = slice(0, SH_UP)
        dh2 = None
        for c in range(NCH):
            pre = [pre_ref[s, c].astype(F32) for s in range(2)]
            sg = jax.nn.sigmoid(pre[0])
            dact = _dot_nt(dxb, wdn_ref[c])
            dpre = [dact * pre[1] * (sg * (1.0 + pre[0] * (1.0 - sg))), dact * (pre[0] * sg)]
            for s in range(2):
                dc = dpre[s]
                ext = jnp.concatenate([dc, carry_ref[s, c] * keep_next], axis=0)
                carry_ref[s, c] = dc[:8]
                shifted = (_shift_up(ext, 2, tm), _shift_up(ext, 1, tm), dc)
                uf = up_ref[s, c].astype(F32)
                gvec_ref[s, c, 0:1, lanes] += jnp.sum(dc, axis=0, keepdims=True)
                for tap in range(3):
                    gvec_ref[s, c, tap + 1:tap + 2, lanes] += jnp.sum(shifted[tap] * uf, axis=0, keepdims=True)
                w = fcw_ref.at[s, c]
                du = w[2:3, :] * dc + w[1:2, :] * shifted[1] + w[0:1, :] * shifted[0]
                dub = du.astype(BF16)
                dup_ref[s, c] = dub
                part = _dot_nt(dub, wup_ref[s, c])
                dh2 = part if dh2 is None else dh2 + part
        x1v = x1_ref[...]
        inv2 = _rms_inv(x1v)
        xn = x1v * inv2
        gn_ref[0:1, :] += jnp.sum(dh2 * xn, axis=0, keepdims=True)
        dx1_ref[...] = dx2v + _rms_bwd(dh2, xn, inv2, g2_ref[...])

    rev = lambda i: (nt - 1 - i, 0)
    const2 = lambda i: (0, 0)
    whole = lambda i: (0, 0, 0, 0)
    chunks = pl.BlockSpec((2, NCH, tm, SH_UP), lambda i: (0, 0, nt - 1 - i, 0))
    return pl.pallas_call(
        body, name="bwd_ffn", grid=(nt,),
        in_specs=[pl.BlockSpec((tm, D), rev), pl.BlockSpec((tm, D), rev), pl.BlockSpec((1, D), const2),
                  chunks, chunks, pl.BlockSpec((2, NCH, D, SH_UP), whole), pl.BlockSpec((2, NCH, 3, SH_UP), whole),
                  pl.BlockSpec((NCH, SH_UP, D), lambda i: (0, 0, 0))],
        out_specs=[chunks, pl.BlockSpec((tm, D), rev), pl.BlockSpec((2, NCH, 8, D), whole),
                   pl.BlockSpec((8, D), const2)],
        out_shape=[jax.ShapeDtypeStruct((2, NCH, t, SH_UP), BF16), jax.ShapeDtypeStruct((t, D), F32),
                   jax.ShapeDtypeStruct((2, NCH, 8, D), F32), jax.ShapeDtypeStruct((8, D), F32)],
        scratch_shapes=[pltpu.VMEM((2, NCH, 8, SH_UP), F32)],
        compiler_params=_cparams(1, VMEM_BIG),
    )(dx2, x1, g2, up, pre, w_up_g.reshape(2, NCH, D, SH_UP), fcw.reshape(2, NCH, 3, SH_UP), w_dn)


def bwd_mix(dx1, zs, y_pool, y_conv, pool_w, pool_scale, conv_w, wmix, tm, seq):
    t = dx1.shape[0]
    nt = t // tm
    tps = seq // tm
    hb = tm // HALO

    def body(da_ref, z0_ref, z1_ref, z2_ref, zh0_ref, zh1_ref, zh2_ref, yp_ref, yc_ref, pw_ref, ps_ref, wpp_ref,
             cw_ref, wco_ref, wo_ref,
             dz_ref, mg_ref, p2_ref, u_ref, dyp_ref, dyc_ref, gpool_ref, gvec_ref, cp_ref, cc_ref):
        z_refs, zh_refs = (z0_ref, z1_ref, z2_ref), (zh0_ref, zh1_ref, zh2_ref)
        i = pl.program_id(0)
        it = (nt - 1 - i) % tps
        keep_hist = jnp.where(it == 0, 0.0, 1.0)
        keep_next = jnp.where(it == tps - 1, 0.0, 1.0)
        pos = it * tm + lax.broadcasted_iota(jnp.int32, (tm, 1), 0)

        @pl.when(i == 0)
        def _():
            gvec_ref[...] = jnp.zeros_like(gvec_ref)
            gpool_ref[...] = jnp.zeros_like(gpool_ref)
            cp_ref[...] = jnp.zeros_like(cp_ref)
            cc_ref[...] = jnp.zeros_like(cc_ref)

        dm =_dot_nt(da_ref[...].astype(BF16), wo_ref[...])
        merged, dyp, dyc = [], [], []
        for q in range(NG):
            lanes = slice(q * CG, (q + 1) * CG)
            sp = jax.nn.sigmoid(_z_tile(z_refs, 16 + q).astype(F32))
            sc = jax.nn.sigmoid(_z_tile(z_refs, 20 + q).astype(F32))
            yp = yp_ref[:, lanes].astype(F32)
            yc = yc_ref[:, lanes].astype(F32)
            dmq = dm[:, lanes]
            merged.append((sp * yp + sc * yc).astype(BF16))
            dyp.append((dmq * sp).astype(BF16))
            dyc.append((dmq * sc).astype(BF16))
            dz_ref[16 + q] = (dmq * yp * (sp * (1.0 - sp))).astype(BF16)
            dz_ref[20 + q] = (dmq * yc * (sc * (1.0 - sc))).astype(BF16)
        mg_ref[...] = jnp.concatenate(merged, axis=1)
        dypb = jnp.concatenate(dyp, axis=1)
        dycb = jnp.concatenate(dyc, axis=1)
        dyp_ref[...] = dypb
        dyc_ref[...] = dycb

        dp2 = _dot_nt(dypb, wpp_ref[...])
        p2 = []
        for g, win in enumerate(WINS):
            lanes = slice(g * CG, (g + 1) * CG)
            cnt = jnp.minimum(pos + 1, win).astype(F32)
            p = _pool_tile(_z_tile(z_refs, g), _z_tile(zh_refs, g), win, keep_hist, cnt)
            pb = p.astype(BF16)
            pw = _dot(pb, pw_ref[g])
            p2.append((pw * ps_ref[:, lanes]).astype(BF16))
            dp2g = dp2[:, lanes]
            gvec_ref[0:1, lanes] += jnp.sum(dp2g * pw, axis=0, keepdims=True)
            dpwb = (dp2g * ps_ref[:, lanes]).astype(BF16)
            gpool_ref[g] += _dot_tn(pb, dpwb)
            dp = _dot_nt(dpwb, pw_ref[g])
            qv = dp / cnt
            ext = jnp.concatenate([qv, cp_ref[g] * keep_next], axis=0)
            cp_ref[g] = qv[:HALO]
            n = tm + HALO
            s, sh = ext, 1
            while sh < win:
                s = s + pltpu.roll(s, n - sh, 0)
                sh *= 2
            dz_ref[g] = (s[:tm] - dp).astype(BF16)
        p2_ref[...] = jnp.concatenate(p2, axis=1)

        du = _dot_nt(dycb, wco_ref[...])
        u = []
        for q in range(NG):
            lanes = slice(q * CG, (q + 1) * CG)
            zb = _z_tile(z_refs, 4 + q).astype(F32)
            zc = _z_tile(z_refs, 8 + q).astype(F32)
            zv = _z_tile(z_refs, 12 + q).astype(F32)
            cv = zc * zv
            cvh = _z_tile(zh_refs, 8 + q).astype(F32) * _z_tile(zh_refs, 12 + q).astype(F32) * keep_hist
            cc, cv1, cv2 = _conv_taps(jnp.concatenate([cvh, cv], axis=0), cv, cw_ref, lanes, HALO)
            u.append((zb * cc).astype(BF16))
            duq = du[:, lanes]
            dz_ref[4 + q] = (duq * cc).astype(BF16)
            dcc = duq * zb
            for tap, src in enumerate((cv2, cv1, cv)):
                gvec_ref[tap + 1:tap + 2, lanes] += jnp.sum(dcc * src, axis=0, keepdims=True)
            ext = jnp.concatenate([dcc, cc_ref[:, lanes] * keep_next], axis=0)
            cc_ref[:, lanes] = dcc[:8]
            dcv = (cw_ref[2:3, lanes] * dcc + cw_ref[1:2, lanes] * _shift_up(ext, 1, tm)
                   + cw_ref[0:1, lanes] * _shift_up(ext, 2, tm))
            dz_ref[8 + q] = (dcv * zv).astype(BF16)
            dz_ref[12 + q] = (dcv * zc).astype(BF16)
        u_ref[...] = jnp.concatenate(u, axis=1)

    def hist(i):
        return jnp.maximum((nt - 1 - i) * hb - 1, 0)

    rev = lambda i: (nt - 1 - i, 0)
    rev3 = lambda i: (0, nt - 1 - i, 0)
    const2 = lambda i: (0, 0)
    tok = jax.ShapeDtypeStruct((t, D), BF16)
    return pl.pallas_call(
        body, name="bwd_mix", grid=(nt,),
        in_specs=[pl.BlockSpec((tm, D), rev)] + [pl.BlockSpec((NDEV, tm, CG), rev3)] * 3
                 + [pl.BlockSpec((NDEV, HALO, CG), lambda i: (0, hist(i), 0))] * 3
                 + [pl.BlockSpec((tm, D), rev), pl.BlockSpec((tm, D), rev),
                    pl.BlockSpec((NG, CG, CG), lambda i: (0, 0, 0)), pl.BlockSpec((1, D), const2),
                    pl.BlockSpec((D, D), lambda i: (0, MIX_POOL_PROJ)), pl.BlockSpec((3, D), const2),
                    pl.BlockSpec((D, D), lambda i: (0, MIX_CONV_OUT)), pl.BlockSpec((D, D), lambda i: (0, MIX_O))],
        out_specs=[pl.BlockSpec((NZT, tm, CG), rev3)] + [pl.BlockSpec((tm, D), rev)] * 5
                  + [pl.BlockSpec((NG, CG, CG), lambda i: (0, 0, 0)), pl.BlockSpec((8, D), const2)],
        out_shape=[jax.ShapeDtypeStruct((NZT, t, CG), BF16), tok, tok, tok, tok, tok,
                   jax.ShapeDtypeStruct((NG, CG, CG), F32), jax.ShapeDtypeStruct((8, D), F32)],
        scratch_shapes=[pltpu.VMEM((NG, HALO, CG), F32), pltpu.VMEM((8, D), F32)],
        compiler_params=_cparams(1, VMEM_BIG),
    )(dx1, *zs, *zs, y_pool, y_conv, pool_w, pool_scale, wmix, conv_w, wmix, wmix)


def bwd_in(dz, w_in_pieces, dx1, x, g1, tm):
    t = x.shape[0]

    def body(dz_ref, w0_ref, w1_ref, w2_ref, dx1_ref, x_ref, g_ref, gx_ref, gn_ref):
        @pl.when(pl.program_id(0) == 0)
        def _():
            gn_ref[...] = jnp.zeros_like(gn_ref)

        dh = None
        for j in range(NDEV):
            for q, w_ref in enumerate((w0_ref, w1_ref, w2_ref)):
                part = _dot_nt(dz_ref[3 * j + q], w_ref[j])
                dh = part if dh is None else dh + part
        xv = x_ref[...]
        inv = _rms_inv(xv)
        xn = xv * inv
        gn_ref[0:1, :] += jnp.sum(dh * xn, axis=0, keepdims=True)
        gx_ref[...] = dx1_ref[...] + _rms_bwd(dh, xn, inv, g_ref[...])

    tile = lambda i: (i, 0)
    return pl.pallas_call(
        body, name="bwd_in", grid=(t // tm,),
        in_specs=[pl.BlockSpec((NZT, tm, CG), lambda i: (0, i, 0))]
                 + [pl.BlockSpec((NDEV, D, CG), lambda i: (0, 0, 0))] * 3
                 + [pl.BlockSpec((tm, D), tile), pl.BlockSpec((tm, D), tile), pl.BlockSpec((1, D), lambda i: (0, 0))],
        out_specs=[pl.BlockSpec((tm, D), tile), pl.BlockSpec((8, D), lambda i: (0, 0))],
        out_shape=[jax.ShapeDtypeStruct((t, D), F32), jax.ShapeDtypeStruct((8, D), F32)],
        compiler_params=_cparams(1, VMEM_BIG),
    )(dz, *w_in_pieces, dx1, x, g1)


def _slot(j):
    return j % 2, j // 2


def wgrad_cols(at, b, q, name):
    m, t = at.shape
    width = b.shape[3]

    def body(a_ref, b_ref, o_ref):
        o_ref[...] = _dot(a_ref[...], b_ref[...])

    return pl.pallas_call(
        body, name=name, grid=(NDEV,),
        in_specs=[pl.BlockSpec((m, t), lambda j: (0, 0)),
                  pl.BlockSpec((None, None, t, width), lambda j: (j, q, 0, 0))],
        out_specs=pl.BlockSpec((None, None, m, width), lambda j: (j % 2, j // 2, 0, 0)),
        out_shape=jax.ShapeDtypeStruct((2, 4, m, width), F32),
        compiler_params=_cparams(1, VMEM_BIG),
    )(at, b)


def wgrad_down(actt, dx2b):
    t = dx2b.shape[0]

    def body(a_ref, b_ref, o_ref):
        r = _dot(a_ref[...], b_ref[...])
        o_ref[0] = r[:SH_DN]
        o_ref[1] = r[SH_DN:]

    return pl.pallas_call(
        body, name="wgrad_down", grid=(NCH,),
        in_specs=[pl.BlockSpec((None, SH_UP, t), lambda k: (k, 0, 0)), pl.BlockSpec((t, D), lambda k: (0, 0))],
        out_specs=pl.BlockSpec((2, None, SH_DN, D), lambda k: (0, k, 0, 0)),
        out_shape=jax.ShapeDtypeStruct((2, 4, SH_DN, D), F32),
        compiler_params=_cparams(1, VMEM_BIG),
    )(actt, dx2b)


def wgrad_square(a, b, name, tk):
    t = a.shape[0]

    def body(a_ref, b_ref, o_ref, acc_ref):
        kt = pl.program_id(0)

        @pl.when(kt == 0)
        def _():
            acc_ref[...] = jnp.zeros_like(acc_ref)

        acc_ref[...] += _dot_tn(a_ref[...], b_ref[...].astype(BF16))

        @pl.when(kt == pl.num_programs(0) - 1)
        def _():
            for j in range(NDEV):
                cc, xy = _slot(j)
                o_ref[cc, xy] = acc_ref[j * 128:(j + 1) * 128]

    return pl.pallas_call(
        body, name=name, grid=(t // tk,),
        in_specs=[pl.BlockSpec((tk, D), lambda k: (k, 0)), pl.BlockSpec((tk, D), lambda k: (k, 0))],
        out_specs=pl.BlockSpec((2, 4, 128, D), lambda k: (0, 0, 0, 0)),
        out_shape=jax.ShapeDtypeStruct((2, 4, 128, D), F32),
        scratch_shapes=[pltpu.VMEM((D, D), F32)],
        compiler_params=_cparams(1, VMEM_BIG),
    )(a, b)


def _adamw(w, g, m, v):
    m = ADAM_B1 * m + (1.0 - ADAM_B1) * g
    v = ADAM_B2 * v + (1.0 - ADAM_B2) * (g * g)
    m_hat = m / (1.0 - ADAM_B1 ** ADAM_STEP)
    v_hat = v / (1.0 - ADAM_B2 ** ADAM_STEP)
    delta = -ADAM_LR * (m_hat / (jnp.sqrt(v_hat) + ADAM_EPS) + ADAM_WD * w)
    return delta, m, v


def _row_block(r):
    return 512 if r % 512 == 0 else r


def chip_partial(place, gs, from_sibling, name):
    n = len(gs)
    shapes = [g.shape[2:] for g in gs]

    def body(place_ref, *refs):
        for g_ref, s_ref, o_ref in zip(refs[:n], refs[n:2 * n], refs[2 * n:]):
            o_ref[...] = (g_ref[...] + s_ref[...]).astype(BF16)

    def slot(rc):
        return pl.BlockSpec((None,) + rc, lambda k, pr: (pr[1] ^ (k + 1), 0, 0))

    return pl.pallas_call(
        body, name=name,
        grid_spec=pltpu.PrefetchScalarGridSpec(
            num_scalar_prefetch=1, grid=(3,),
            in_specs=[pl.BlockSpec((None, None) + rc, lambda k, pr: (pr[0], pr[1] ^ (k + 1), 0, 0)) for rc in shapes]
                     + [slot(rc) for rc in shapes],
            out_specs=[slot(rc) for rc in shapes]),
        out_shape=[jax.ShapeDtypeStruct((4,) + rc, BF16) for rc in shapes],
        compiler_params=_cparams(1, VMEM_BIG),
    )(place, *gs, *from_sibling)


def finish_adamw(place, gs, from_sibling, from_chips, w, m, v, name, transposed=False):
    n = len(gs)
    r = gs[0].shape[2]
    widths = [g.shape[3] for g in gs]
    c = sum(widths)
    br = _row_block(r)

    def body(place_ref, *refs):
        g_refs, s_refs, c_refs = refs[:n], refs[n:2 * n], refs[2 * n:5 * n]
        w_ref, m_ref, v_ref, og_ref, od_ref, om_ref, ov_ref = refs[5 * n:]
        cols = []
        for q in range(n):
            grad = g_refs[q][...] + s_refs[q][...]
            for k in range(3):
                grad = grad + c_refs[3 * q + k][...].astype(F32)
            cols.append(grad)
        grad = cols[0] if n == 1 else jnp.concatenate(cols, axis=1)
        if transposed:
            grad = grad.T
        og_ref[...] = grad
        od_ref[...], om_ref[...], ov_ref[...] = _adamw(w_ref[...], grad, m_ref[...], v_ref[...])

    def other(k, cq):
        return pl.BlockSpec((None, br, cq), lambda i, pr: (pr[1] ^ k, i, 0))

    row = pl.BlockSpec((c, br), lambda i, pr: (0, i)) if transposed else pl.BlockSpec((br, c), lambda i, pr: (i, 0))
    out = jax.ShapeDtypeStruct((c, r) if transposed else (r, c), F32)
    in_specs = [pl.BlockSpec((None, None, br, cq), lambda i, pr: (pr[0], pr[1], i, 0)) for cq in widths]
    in_specs += [pl.BlockSpec((None, br, cq), lambda i, pr: (pr[1], i, 0)) for cq in widths]
    in_specs += [other(k, cq) for cq in widths for k in (1, 2, 3)]
    return pl.pallas_call(
        body, name=name,
        grid_spec=pltpu.PrefetchScalarGridSpec(
            num_scalar_prefetch=1, grid=(r // br,), in_specs=in_specs + [row, row, row], out_specs=[row] * 4),
        out_shape=[out] * 4,
        compiler_params=_cparams(1, VMEM_BIG),
    )(place, *gs, *from_sibling, *[fc for fc in from_chips for _ in range(3)], w, m, v)


def adamw_small(items):
    n = len(items)

    def body(*refs):
        ins, outs = refs[:4 * n], refs[4 * n:]
        for i in range(n):
            w, g, m, v = (r[...] for r in ins[4 * i:4 * i + 4])
            outs[3 * i][...], outs[3 * i + 1][...], outs[3 * i + 2][...] = _adamw(w, g, m, v)

    out = [jax.ShapeDtypeStruct(it[0].shape, F32) for it in items for _ in range(3)]
    res = pl.pallas_call(body, name="adamw_small", out_shape=out)(*[a for it in items for a in it])
    return [res[3 * i:3 * i + 3] for i in range(n)]


def kernel(x, norm_mix, w_in, pool_w, pool_scale, w_pool_proj, conv_w, w_conv_out, w_o, norm_ffn, w_up, ffn_conv_w, ffn_conv_b, w_down, norm_final, loss_target, m_norm_mix, m_w_in, m_pool_w, m_pool_scale, m_w_pool_proj, m_conv_w, m_w_conv_out, m_w_o, m_norm_ffn, m_w_up, m_ffn_conv_w, m_ffn_conv_b, m_w_down, m_norm_final, v_norm_mix, v_w_in, v_pool_w, v_pool_scale, v_w_pool_proj, v_conv_w, v_w_conv_out, v_w_o, v_norm_ffn, v_w_up, v_ffn_conv_w, v_ffn_conv_b, v_w_down, v_norm_final):
    nb, seq, _ = x.shape
    t = nb * seq
    tm_in = min(TM_IN, t)
    tm_mix = min(TM_MIX, seq)
    tm_ffn = min(TM_FFN, seq)
    tk = min(TK_WGRAD, t)
    xt = x.reshape(t, D)
    tgt = loss_target.reshape(t, D)
    xi, yi, ci = _pos()
    me = 4 * xi + 2 * yi + ci
    place = jnp.stack([ci, 2 * xi + yi]).astype(jnp.int32)

    tie = lax.optimization_barrier
    w_in_b = w_in[0].astype(BF16)
    w_in_g = [all_gather_blocks([w_in_b[:, q * CG:(q + 1) * CG]], f"all_gather_w_in_{q}", 0)[0] for q in range(3)]
    taps = (jnp.pad(conv_w[0], ((0, 5), (0, D - 128))) + jnp.pad(ffn_conv_w[0], ((3, 2), (0, D - SH_UP))))
    taps_g = _exchange_small(taps, "all_gather_taps")
    mix_shard = jnp.concatenate(
        [w_pool_proj[0], w_conv_out[0], w_o[0], pool_w[0].reshape(NG * 32, CG)], axis=1).astype(BF16)
    conv_w_f = taps_g[:, 0:3, :128].transpose(1, 0, 2).reshape(3, D)
    fcw_f = taps_g[:, 3:6, :SH_UP]
    fcb_f = ffn_conv_b.reshape(NDEV, 1, SH_UP)
    mix_shard, conv_w_f, fcw_f, fcb_f = tie((mix_shard, conv_w_f, fcw_f, fcb_f))
    wmix_g, = all_gather_blocks([mix_shard], "all_gather_w_mix", 0)
    ffn_shards, w_in_g[0] = tie(([w_up[0].astype(BF16), w_down[0].astype(BF16)], w_in_g[0]))
    w_up_g, = all_gather_blocks(ffn_shards[:1], "all_gather_w_up", 0)
    w_dn_g, = all_gather_blocks(ffn_shards[1:], "all_gather_w_down", 0)
    w_dn_f = w_dn_g.reshape(NCH, SH_UP, D)
    gfin = norm_final.reshape(1, D)

    zs, h1 = fwd_in(xt, norm_mix, w_in_g, tm_in)
    wmix_g, zs = tie((wmix_g, zs))
    wmix = wmix_g.reshape(D, MIX_COLS)
    pool_w_f = wmix_g[:, :, 3 * D:].reshape(NDEV, NG, 32, CG).transpose(1, 0, 2, 3).reshape(NG, CG, CG)
    x1, y_pool, y_conv = fwd_mix(zs, xt, pool_w_f, pool_scale, conv_w_f, wmix, tm_mix, seq)
    up, pre, act_tok, act, h2 = fwd_up(x1, norm_ffn, w_up_g, fcw_f, fcb_f, tm_ffn, seq)
    dx2, dx2b, ffn_vec = fwd_down(x1, act_tok, w_dn_f, gfin, tgt, min(TM_IN, t))

    def to_sibling(full, tag):
        return reduce_scatter_d2d(full, "reduce_scatter_d2d_" + tag, 1)

    def partials(full, from_sib, tag):
        return chip_partial(place, full, from_sib, "chip_partial_" + tag)

    def to_chips(parts, tag):
        return reduce_scatter_ici(parts, "reduce_scatter_ici_" + tag, 2)

    def finish(nm, gs, from_sib, from_chips, wmv, transposed=False):
        rc = (gs[0].shape[2], sum(g.shape[3] for g in gs))
        wmv2 = [a.reshape(rc).T if transposed else a.reshape(rc) for a in wmv]
        outs = finish_adamw(place, gs, from_sib, from_chips, *wmv2, "adamw_" + nm, transposed)
        return [(o.T if transposed else o).reshape(wmv[0].shape) for o in outs]

    def after(x, dep):
        return tie((x, dep))[0]

    big = {}
    d_up, dx1, g_ffn_vec, g_nffn = bwd_ffn(dx2, x1, norm_ffn, up, pre, w_up_g, fcw_f, w_dn_f, tm_ffn, seq)
    gw_up = wgrad_cols(h2, d_up.reshape(NDEV, 1, t, SH_UP), 0, "wgrad_up")
    sib_up = to_sibling([gw_up], "w_up")
    gw_dn = wgrad_down(act, after(dx2b, gw_up))
    sib_dn = to_sibling([after(gw_dn, sib_up)], "w_down")
    dx1, part_up = tie((dx1, partials([gw_up], sib_up, "w_up")))
    chips_up = to_chips(part_up, "w_up")
    dz, merged, p2, u, dyp, dyc, g_pool, g_mix_vec = bwd_mix(
        dx1, zs, y_pool, y_conv, pool_w_f, pool_scale, conv_w_f, wmix, tm_mix, seq)
    merged, part_dn = tie((merged, partials([gw_dn], sib_dn, "w_down")))
    chips_dn = to_chips(part_dn, "w_down")
    gw_o = wgrad_square(merged, dx1, "wgrad_o", tk)
    gw_pp = wgrad_square(p2, dyp, "wgrad_pool_proj", tk)
    gw_co = wgrad_square(u, dyc, "wgrad_conv_out", tk)
    gw_pool = g_pool.reshape(NG, 4, 2, 32, CG).transpose(2, 1, 0, 3, 4).reshape(2, 4, NG * 32, CG)
    dz8 = dz.reshape(NDEV, 3, t, CG)
    gw_in, sib_in, chips_in = [None] * 3, [None] * 3, [None] * 3
    sib_a = to_sibling(after([gw_o, gw_pp], (chips_up, gw_pool, chips_dn)), "mix_a")
    sib_b = to_sibling(after([gw_co, gw_pool], sib_a), "mix_b")
    gw_in[0] = wgrad_cols(h1, dz8, 0, "wgrad_in_0")
    h1, part_a, part_b = tie((h1, partials([gw_o, gw_pp], sib_a, "mix_a"),
                              partials([gw_co, gw_pool], sib_b, "mix_b")))
    chips_a = to_chips(after(part_a, chips_dn), "mix_a")
    chips_b = to_chips(part_b, "mix_b")
    sib_in[0] = to_sibling(after([gw_in[0]], sib_b), "w_in_0")
    gw_in[1] = wgrad_cols(h1, dz8, 1, "wgrad_in_1")
    h1, part_in0, gw_in[1] = tie((h1, partials([gw_in[0]], sib_in[0], "w_in_0"), gw_in[1]))
    chips_in[0] = to_chips(part_in0, "w_in_0")
    sib_in[1] = to_sibling(after([gw_in[1]], sib_in[0]), "w_in_1")
    h1, big["w_down"], big["w_up"] = tie((
        h1, finish("w_down", [gw_dn], sib_dn, chips_dn, (w_down, m_w_down, v_w_down)),
        finish("w_up", [gw_up], sib_up, chips_up, (w_up, m_w_up, v_w_up), transposed=True)))
    gw_in[2] = wgrad_cols(h1, dz8, 2, "wgrad_in_2")
    sib_in[2] = to_sibling(after([gw_in[2]], (chips_a, chips_b, chips_in[0])), "w_in_2")
    sib_in[2], big["w_o"], big["w_pool_proj"], big["w_conv_out"], big["pool_w"] = tie((
        sib_in[2],
        finish("w_o", [gw_o], sib_a[:1], chips_a[:1], (w_o, m_w_o, v_w_o)),
        finish("w_pool_proj", [gw_pp], sib_a[1:], chips_a[1:], (w_pool_proj, m_w_pool_proj, v_w_pool_proj)),
        finish("w_conv_out", [gw_co], sib_b[:1], chips_b[:1], (w_conv_out, m_w_conv_out, v_w_conv_out)),
        finish("pool_w", [gw_pool], sib_b[1:], chips_b[1:], (pool_w, m_pool_w, v_pool_w))))
    dx1, part_in12 = tie((dx1, partials(gw_in[1:], sib_in[1] + sib_in[2], "w_in_12")))
    chips_in[1] = to_chips(part_in12[:1], "w_in_1")
    chips_in[2] = to_chips(part_in12[1:], "w_in_2")
    small_g, = all_gather_blocks(
        [after(jnp.concatenate([g_mix_vec, g_nffn, ffn_vec, g_ffn_vec.reshape(8 * NDEV, D)], axis=0), sib_in[2])],
        "all_gather_small", 0)
    grad_x, g_nmix = bwd_in(dz, w_in_g, dx1, xt, norm_mix, min(TM_BWD_IN, t))
    grad_x, chips_in = tie((grad_x, chips_in))
    nmix_g, = all_gather_blocks([g_nmix], "all_gather_norm_mix", 0)
    big["w_in"] = finish("w_in", gw_in, [s[0] for s in sib_in], [c[0] for c in chips_in], (w_in, m_w_in, v_w_in))

    red, = sum_blocks([small_g], "sum_small")
    red_n, = sum_blocks([after(nmix_g, (big["w_in"], red))], "sum_norm_mix")
    g_norm_mix, g_pool_scale, g_norm_ffn = red_n[0:1], red[0:1], red[8:9]
    g_conv_w = lax.dynamic_slice(red, (1, me * 128), (3, 128))
    g_norm_final = red[16]
    loss = red[17, 0]
    g_fcb = red[24:].reshape(NDEV, 8, D)[:, 0, :SH_UP].reshape(1, FF2)
    g_fcw = lax.dynamic_slice(red, (25 + 8 * me, 0), (3, SH_UP))
    grads = {"norm_mix": g_norm_mix, "pool_scale": g_pool_scale, "norm_ffn": g_norm_ffn, "norm_final": g_norm_final,
             "ffn_conv_b": g_fcb, "conv_w": g_conv_w.reshape(1, 3, 128), "ffn_conv_w": g_fcw.reshape(1, 3, SH_UP)}
    small_wmv = {"norm_mix": (norm_mix, m_norm_mix, v_norm_mix), "pool_scale": (pool_scale, m_pool_scale, v_pool_scale),
                 "norm_ffn": (norm_ffn, m_norm_ffn, v_norm_ffn), "norm_final": (norm_final, m_norm_final, v_norm_final),
                 "ffn_conv_b": (ffn_conv_b, m_ffn_conv_b, v_ffn_conv_b), "conv_w": (conv_w, m_conv_w, v_conv_w),
                 "ffn_conv_w": (ffn_conv_w, m_ffn_conv_w, v_ffn_conv_w)}
    small_names = list(small_wmv)
    flat2 = lambda a: a.reshape(1, -1) if a.ndim == 1 else (a.transpose(1, 0, 2) if a.ndim == 3 else a)
    unflat = lambda o, like: o.transpose(1, 0, 2) if like.ndim == 3 else o.reshape(like.shape)
    small_out = adamw_small([(flat2(small_wmv[nm][0]), flat2(grads[nm]), flat2(small_wmv[nm][1]),
                              flat2(small_wmv[nm][2])) for nm in small_names])
    small = {nm: [unflat(o, small_wmv[nm][0]) for o in outs] for nm, outs in zip(small_names, small_out)}

    order = ["norm_mix", "w_in", "pool_w", "pool_scale", "w_pool_proj", "conv_w", "w_conv_out", "w_o", "norm_ffn",
             "w_up", "ffn_conv_w", "ffn_conv_b", "w_down", "norm_final"]
    out = [loss, grad_x.reshape(nb, seq, D)]
    out += [big[nm][0] if nm in big else grads[nm] for nm in order]
    for idx in range(3):
        out += [big[nm][idx + 1] if nm in big else small[nm][idx] for nm in order]
    return tuple(out)
```

```python
import jax
import jax.numpy as jnp
from jax import lax
from jax.experimental import pallas as pl
from jax.experimental.pallas import tpu as pltpu
from jax.experimental.pallas import tpu_sc as plsc

F32 = jnp.float32
BF16 = jnp.bfloat16

NDEV = 8
D = 1024
NG = 4
CG = 256
WINS = (2, 4, 8, 16)
DIN = 6 * D
SH_IN = DIN // NDEV
NZT = DIN // CG
FF2 = 5632
SH_UP = FF2 // NDEV
FF = FF2 // 2
NCH = 4
SH_DN = FF // NDEV
RMS_EPS = 1e-6
HALO = 16

ADAM_LR = 0.001
ADAM_B1 = 0.9
ADAM_B2 = 0.999
ADAM_EPS = 1e-08
ADAM_WD = 0.01
ADAM_STEP = 10

TM_IN = 512
TM_BWD_IN = 256
TM_MIX = 256
TM_FFN = 256
TK_WGRAD = 1024
MIX_POOL_PROJ, MIX_CONV_OUT, MIX_O = 0, 1, 2
MIX_COLS = 3 * D + CG
VMEM_BIG = 56 * 1024 * 1024
MESH = pl.DeviceIdType.MESH
ANY = pl.BlockSpec(memory_space=pl.ANY)


def _cparams(n_axes, vmem=None):
    return pltpu.CompilerParams(dimension_semantics=("arbitrary",) * n_axes, vmem_limit_bytes=vmem)


def _dot(a, b):
    return jnp.dot(a, b, preferred_element_type=F32)


def _dot_nt(a, b):
    return lax.dot_general(a, b, (((1,), (1,)), ((), ())), preferred_element_type=F32)


def _dot_tn(a, b):
    return lax.dot_general(a, b, (((0,), (0,)), ((), ())), preferred_element_type=F32)


def _shift_down(ext, s, lead):
    return pltpu.roll(ext, s, 0)[lead:]


def _shift_up(ext, s, tm):
    n = ext.shape[0]
    return pltpu.roll(ext, n - s, 0)[:tm]


def _rms_inv(x):
    return lax.rsqrt(jnp.mean(x * x, axis=-1, keepdims=True) + RMS_EPS)


def _rms_bwd(dh, xn, inv, g):
    dxn = dh * g
    return inv * (dxn - xn * jnp.mean(dxn * xn, axis=-1, keepdims=True))


def _pos():
    return lax.axis_index("x"), lax.axis_index("y"), lax.axis_index("c")


def _handshake(peers):
    barrier = pltpu.get_barrier_semaphore()
    for peer in peers:
        pl.semaphore_signal(barrier, inc=1, device_id=peer, device_id_type=MESH)
    pl.semaphore_wait(barrier, len(peers))


def _sequencer(body, out_type, n_sems, name, collective_id):
    return pl.kernel(
        body, out_type=out_type, mesh=plsc.ScalarSubcoreMesh(axis_name="sequencer", num_cores=1), name=name,
        scratch_types=[pltpu.SemaphoreType.DMA((n_sems,)), pltpu.SemaphoreType.DMA((n_sems,))],
        compiler_params=pltpu.CompilerParams(collective_id=collective_id))


def all_gather_blocks(shards, name, collective_id):
    n = len(shards)

    def body(*refs):
        ins, outs = refs[:n], refs[n:2 * n]
        send_sems, recv_sems = refs[2 * n:]
        x, y, c = _pos()
        me, sibling = (x, y, c), (x, y, 1 - c)
        first_chip, second_chip, diagonal = (x ^ (1 - c), y ^ c), (x ^ c, y ^ (1 - c)), (1 - x, 1 - y)
        first, second = (*first_chip, c), (*second_chip, c)
        _handshake([sibling, first, second])

        def copy(w, k, block, to, src=None):
            slot = outs[w].at[4 * block[0] + 2 * block[1] + block[2]]
            return pltpu.make_async_remote_copy(
                src_ref=slot if src is None else src, dst_ref=slot,
                send_sem=send_sems.at[8 * w + k], recv_sem=recv_sems.at[8 * w + k], device_id=to, device_id_type=MESH)

        mine, sent = [], []
        for w in range(n):
            m = pltpu.make_async_copy(ins[w], outs[w].at[4 * x + 2 * y + c], send_sems.at[8 * w + 7])
            m.start()
            mine.append(m)
            sent += [copy(w, k, me, to, src=ins[w]) for k, to in enumerate((sibling, first, second))]
        for cp in sent:
            cp.start()
        for k, chip in ((1, first_chip), (2, second_chip), (3, diagonal)):
            for w in range(n):
                copy(w, k, (*chip, c), me).wait_recv()
                onward = [copy(w, 3 + k, (*chip, c), sibling)] + ([copy(w, 3, (*chip, c), second)] if k == 1 else [])
                for cp in onward:
                    cp.start()
                sent += onward
        for w in range(n):
            copy(w, 0, sibling, me).wait_recv()
            for k, chip in ((4, second_chip), (5, first_chip), (6, diagonal)):
                copy(w, k, (*chip, 1 - c), me).wait_recv()
        for cp in sent:
            cp.wait_send()
        for m in mine:
            m.wait()

    out = [jax.ShapeDtypeStruct((NDEV,) + s.shape, s.dtype) for s in shards]
    return _sequencer(body, out, 8 * n, name, collective_id)(*shards)


def _exchange_small(v, name):
    rows = v.shape[0]

    def body(v_ref, out_ref, slots, send_sems, recv_sems, local_sem):
        x, y, c = _pos()
        me = 4 * x + 2 * y + c
        mine = pltpu.make_async_copy(v_ref, slots.at[me], local_sem)
        mine.start()
        offs = [(dx, dy, dc) for dx in (0, 1) for dy in (0, 1) for dc in (0, 1)][1:]

        def copy(k, src_slot, to):
            return pltpu.make_async_remote_copy(
                src_ref=v_ref, dst_ref=slots.at[src_slot], send_sem=send_sems.at[k], recv_sem=recv_sems.at[k],
                device_id=to, device_id_type=MESH)

        sends = []
        for k, (dx, dy, dc) in enumerate(offs):
            cp = copy(k, me, (x ^ dx, y ^ dy, c ^ dc))
            cp.start()
            sends.append(cp)
        for k, (dx, dy, dc) in enumerate(offs):
            copy(k, 4 * (x ^ dx) + 2 * (y ^ dy) + (c ^ dc), (x, y, c)).wait_recv()
        for cp in sends:
            cp.wait_send()
        mine.wait()
        out_ref[...] = slots[...]

    vmem = pl.BlockSpec(memory_space=pltpu.VMEM)
    return pl.pallas_call(
        body, name=name, out_shape=jax.ShapeDtypeStruct((NDEV, rows, D), F32), in_specs=[vmem], out_specs=vmem,
        scratch_shapes=[pltpu.VMEM((NDEV, rows, D), F32), pltpu.SemaphoreType.DMA((7,)),
                        pltpu.SemaphoreType.DMA((7,)), pltpu.SemaphoreType.DMA],
    )(v)


def sum_blocks(gathered, name):
    def body(*refs):
        for g_ref, o_ref in zip(refs[:len(gathered)], refs[len(gathered):]):
            acc = g_ref[0]
            for d in range(1, NDEV):
                acc = acc + g_ref[d]
            o_ref[...] = acc

    vmem = pl.BlockSpec(memory_space=pltpu.VMEM)
    return pl.pallas_call(
        body, name=name, out_shape=[jax.ShapeDtypeStruct(g.shape[1:], F32) for g in gathered],
        in_specs=[vmem] * len(gathered), out_specs=[vmem] * len(gathered),
    )(*gathered)


def reduce_scatter_d2d(grads, name, collective_id):
    n = len(grads)

    def body(*refs):
        ins, outs = refs[:n], refs[n:2 * n]
        send_sems, recv_sems = refs[2 * n:]
        x, y, c = _pos()
        _handshake([(x, y, 1 - c)])
        cps = []
        for w in range(n):
            cp = pltpu.make_async_remote_copy(
                src_ref=ins[w].at[1 - c], dst_ref=outs[w], send_sem=send_sems.at[w], recv_sem=recv_sems.at[w],
                device_id=(x, y, 1 - c), device_id_type=MESH)
            cp.start()
            cps.append(cp)
        for cp in cps:
            cp.wait_recv()
        for cp in cps:
            cp.wait_send()

    out = [jax.ShapeDtypeStruct(g.shape[1:], F32) for g in grads]
    return _sequencer(body, out, n, name, collective_id)(*grads)


def reduce_scatter_ici(parts, name, collective_id):
    n = len(parts)

    def body(*refs):
        ins, outs = refs[:n], refs[n:2 * n]
        send_sems, recv_sems = refs[2 * n:]
        x, y, c = _pos()
        offs = [(1, 0), (0, 1), (1, 1)]
        _handshake([(x ^ dx, y ^ dy, c) for dx, dy in offs])
        cps = []
        for w in range(n):
            for k, (dx, dy) in enumerate(offs):
                ox, oy = x ^ dx, y ^ dy
                cp = pltpu.make_async_remote_copy(
                    src_ref=ins[w].at[2 * ox + oy], dst_ref=outs[w].at[2 * x + y],
                    send_sem=send_sems.at[3 * w + k], recv_sem=recv_sems.at[3 * w + k],
                    device_id=(ox, oy, c), device_id_type=MESH)
                cp.start()
                cps.append((cp, w, k, ox, oy))
        for cp, w, k, ox, oy in cps:
            pltpu.make_async_remote_copy(
                src_ref=ins[w].at[2 * ox + oy], dst_ref=outs[w].at[2 * ox + oy],
                send_sem=send_sems.at[3 * w + k], recv_sem=recv_sems.at[3 * w + k],
                device_id=(ox, oy, c), device_id_type=MESH).wait_recv()
        for cp, *_ in cps:
            cp.wait_send()

    out = [jax.ShapeDtypeStruct(p.shape, BF16) for p in parts]
    return _sequencer(body, out, 3 * n, name, collective_id)(*parts)


def fwd_in(x, g1, w_in_pieces, tm):
    t = x.shape[0]
    tile = lambda i: (i, 0)
    w_spec = pl.BlockSpec((NDEV, D, CG), lambda i: (0, 0, 0))
    z_spec = pl.BlockSpec((NDEV, tm, CG), lambda i: (0, i, 0))
    z_shape = jax.ShapeDtypeStruct((NDEV, t, CG), BF16)

    def cost(other_bytes, transcendentals):
        return pl.CostEstimate(flops=2 * t * D * NDEV * CG, transcendentals=transcendentals,
                               bytes_accessed=other_bytes + 2 * D * NDEV * CG + 2 * t * NDEV * CG)

    def first(x_ref, g_ref, w_ref, z_ref, h_ref, ht_ref):
        xf = x_ref[...]
        h = (xf * _rms_inv(xf) * g_ref[...]).astype(BF16)
        h_ref[...] = h
        ht_ref[...] = h.T
        for j in range(NDEV):
            z_ref[j] = _dot(h, w_ref[j]).astype(BF16)

    z0, h, ht = pl.pallas_call(
        first, name="fwd_in_0", grid=(t // tm,),
        in_specs=[pl.BlockSpec((tm, D), tile), pl.BlockSpec((1, D), lambda i: (0, 0)), w_spec],
        out_specs=[z_spec, pl.BlockSpec((tm, D), tile), pl.BlockSpec((D, tm), lambda i: (0, i))],
        out_shape=[z_shape, jax.ShapeDtypeStruct((t, D), BF16), jax.ShapeDtypeStruct((D, t), BF16)],
        compiler_params=_cparams(1, VMEM_BIG), cost_estimate=cost(8 * t * D + 4 * D, t),
    )(x, g1, w_in_pieces[0])
    zs = [z0]
    for q in (1, 2):
        h, zs[-1] = lax.optimization_barrier((h, zs[-1]))

        def later(h_ref, w_ref, z_ref):
            hb = h_ref[...]
            for j in range(NDEV):
                z_ref[j] = _dot(hb, w_ref[j]).astype(BF16)

        zs.append(pl.pallas_call(
            later, name=f"fwd_in_{q}", grid=(t // tm,),
            in_specs=[pl.BlockSpec((tm, D), tile), w_spec], out_specs=z_spec, out_shape=z_shape,
            compiler_params=_cparams(1, VMEM_BIG), cost_estimate=cost(2 * t * D, 0),
        )(h, w_in_pieces[q]))
    return zs, ht


def _z_tile(z_refs, n):
    return z_refs[n % 3][n // 3]


def _pool_tile(z, zh, win, keep_hist, cnt):
    zt = z.astype(F32)
    ext = jnp.concatenate([zh.astype(F32) * keep_hist, zt], axis=0)
    s, sh = ext, 1
    while sh < win:
        s = s + pltpu.roll(s, sh, 0)
        sh *= 2
    return s[HALO:] / cnt - zt


def _conv_taps(ext, cur, w_ref, lanes, lead):
    x1 = _shift_down(ext, 1, lead)
    x2 = _shift_down(ext, 2, lead)
    out = w_ref[2:3, lanes] * cur + w_ref[1:2, lanes] * x1 + w_ref[0:1, lanes] * x2
    return out, x1, x2


def fwd_mix(zs, x, pool_w, pool_scale, conv_w, wmix, tm, seq):
    t = x.shape[0]
    tps = seq // tm
    hb = tm // HALO

    def body(z0_ref, z1_ref, z2_ref, zh0_ref, zh1_ref, zh2_ref, x_ref, pw_ref, ps_ref, wpp_ref, cw_ref, wco_ref,
             wo_ref, x1_ref, yp_ref, yc_ref):
        z_refs, zh_refs = (z0_ref, z1_ref, z2_ref), (zh0_ref, zh1_ref, zh2_ref)
        it = pl.program_id(0) % tps
        keep_hist = jnp.where(it == 0, 0.0, 1.0)
        pos = it * tm + lax.broadcasted_iota(jnp.int32, (tm, 1), 0)
        p2 = []
        for g, win in enumerate(WINS):
            cnt = jnp.minimum(pos + 1, win).astype(F32)
            p = _pool_tile(_z_tile(z_refs, g), _z_tile(zh_refs, g), win, keep_hist, cnt)
            lanes = slice(g * CG, (g + 1) * CG)
            p2.append((_dot(p.astype(BF16), pw_ref[g]) * ps_ref[:, lanes]).astype(BF16))
        y_pool = _dot(jnp.concatenate(p2, axis=1), wpp_ref[...])
        u = []
        for q in range(NG):
            lanes = slice(q * CG, (q + 1) * CG)
            cv = _z_tile(z_refs, 8 + q).astype(F32) * _z_tile(z_refs, 12 + q).astype(F32)
            cvh = _z_tile(zh_refs, 8 + q).astype(F32) * _z_tile(zh_refs, 12 + q).astype(F32) * keep_hist
            cc, _, _ = _conv_taps(jnp.concatenate([cvh, cv], axis=0), cv, cw_ref, lanes, HALO)
            u.append((_z_tile(z_refs, 4 + q).astype(F32) * cc).astype(BF16))
        y_conv = _dot(jnp.concatenate(u, axis=1), wco_ref[...])
        ypb, ycb = y_pool.astype(BF16), y_conv.astype(BF16)
        yp_ref[...] = ypb
        yc_ref[...] = ycb
        merged = []
        for q in range(NG):
            lanes = slice(q * CG, (q + 1) * CG)
            sp = jax.nn.sigmoid(_z_tile(z_refs, 16 + q).astype(F32))
            sc = jax.nn.sigmoid(_z_tile(z_refs, 20 + q).astype(F32))
            merged.append((sp * ypb[:, lanes].astype(F32) + sc * ycb[:, lanes].astype(F32)).astype(BF16))
        x1_ref[...] = x_ref[...] + _dot(jnp.concatenate(merged, axis=1), wo_ref[...])

    def hist(i):
        return jnp.maximum(i * hb - 1, 0)

    const2 = lambda i: (0, 0)
    return pl.pallas_call(
        body, name="fwd_mix", grid=(t // tm,),
        in_specs=[pl.BlockSpec((NDEV, tm, CG), lambda i: (0, i, 0))] * 3
                 + [pl.BlockSpec((NDEV, HALO, CG), lambda i: (0, hist(i), 0))] * 3
                 + [pl.BlockSpec((tm, D), lambda i: (i, 0)),
                    pl.BlockSpec((NG, CG, CG), lambda i: (0, 0, 0)), pl.BlockSpec((1, D), const2),
                    pl.BlockSpec((D, D), lambda i: (0, MIX_POOL_PROJ)), pl.BlockSpec((3, D), const2),
                    pl.BlockSpec((D, D), lambda i: (0, MIX_CONV_OUT)), pl.BlockSpec((D, D), lambda i: (0, MIX_O))],
        out_specs=[pl.BlockSpec((tm, D), lambda i: (i, 0))] * 3,
        out_shape=[jax.ShapeDtypeStruct((t, D), F32), jax.ShapeDtypeStruct((t, D), BF16),
                   jax.ShapeDtypeStruct((t, D), BF16)],
        compiler_params=_cparams(1, VMEM_BIG),
    )(*zs, *zs, x, pool_w, pool_scale, wmix, conv_w, wmix, wmix)


def fwd_up(x1, g2, w_up_g, fcw, fcb, tm, seq):
    t = x1.shape[0]
    tps = seq // tm

    def body(x1_ref, g2_ref, wup_ref, fcw_ref, fcb_ref, up_ref, pre_ref, act_ref, actt_ref, h2t_ref, hist_ref):
        i = pl.program_id(0)
        keep_hist = jnp.where(i % tps == 0, 0.0, 1.0)

        @pl.when(i == 0)
        def _():
            hist_ref[...] = jnp.zeros_like(hist_ref)

        x1v = x1_ref[...]
        h2 = (x1v * _rms_inv(x1v) * g2_ref[...]).astype(BF16)
        h2t_ref[...] = h2.T
        lanes = slice(0, SH_UP)
        def chunk(c, carry):
            conv = []
            for s in range(2):
                ub = _dot(h2, wup_ref[s, c]).astype(BF16)
                up_ref[s, c] = ub
                uf = ub.astype(F32)
                ext = jnp.concatenate([hist_ref[s, c] * keep_hist, uf], axis=0)
                hist_ref[s, c] = uf[tm - 8:]
                cc, _, _ = _conv_taps(ext, uf, fcw_ref.at[s, c], lanes, 8)
                conv.append(cc + fcb_ref[s, c])
                pre_ref[s, c] = conv[s].astype(BF16)
            a = (conv[0] * jax.nn.sigmoid(conv[0]) * conv[1]).astype(BF16)
            act_ref[c] = a
            actt_ref[c] = a.T
            return carry

        lax.fori_loop(0, NCH, chunk, 0)

    tile = lambda i: (i, 0)
    const2 = lambda i: (0, 0)
    whole = lambda i: (0, 0, 0, 0)
    chunks = pl.BlockSpec((2, NCH, tm, SH_UP), lambda i: (0, 0, i, 0))
    return pl.pallas_call(
        body, name="fwd_up", grid=(t // tm,),
        in_specs=[pl.BlockSpec((tm, D), tile), pl.BlockSpec((1, D), const2),
                  pl.BlockSpec((2, NCH, D, SH_UP), whole), pl.BlockSpec((2, NCH, 3, SH_UP), whole),
                  pl.BlockSpec((2, NCH, 1, SH_UP), whole)],
        out_specs=[chunks, chunks, pl.BlockSpec((NCH, tm, SH_UP), lambda i: (0, i, 0)),
                   pl.BlockSpec((NCH, SH_UP, tm), lambda i: (0, 0, i)), pl.BlockSpec((D, tm), lambda i: (0, i))],
        out_shape=[jax.ShapeDtypeStruct((2, NCH, t, SH_UP), BF16), jax.ShapeDtypeStruct((2, NCH, t, SH_UP), BF16),
                   jax.ShapeDtypeStruct((NCH, t, SH_UP), BF16), jax.ShapeDtypeStruct((NCH, SH_UP, t), BF16),
                   jax.ShapeDtypeStruct((D, t), BF16)],
        scratch_shapes=[pltpu.VMEM((2, NCH, 8, SH_UP), F32)],
        compiler_params=_cparams(1, VMEM_BIG),
    )(x1, g2, w_up_g.reshape(2, NCH, D, SH_UP), fcw.reshape(2, NCH, 3, SH_UP), fcb.reshape(2, NCH, 1, SH_UP))


def fwd_down(x1, act, w_dn, gf, tgt, tm):
    t = x1.shape[0]

    def body(x1_ref, act_ref, wdn_ref, gf_ref, tgt_ref, dx2_ref, dx2b_ref, vec_ref):
        @pl.when(pl.program_id(0) == 0)
        def _():
            vec_ref[...] = jnp.zeros_like(vec_ref)

        d = None
        for c in range(NCH):
            part = _dot(act_ref[c], wdn_ref[c])
            d = part if d is None else d + part
        x2 = x1_ref[...] + d
        inv3 = _rms_inv(x2)
        xn = x2 * inv3
        diff = xn * gf_ref[...] - tgt_ref[...]
        dy = diff * (1.0 / D)
        vec_ref[0:1, :] += jnp.sum(dy * xn, axis=0, keepdims=True)
        vec_ref[1:2, :] += 0.5 * jnp.sum(jnp.mean(diff * diff, axis=-1))
        dx2 = _rms_bwd(dy, xn, inv3, gf_ref[...])
        dx2_ref[...] = dx2
        dx2b_ref[...] = dx2.astype(BF16)

    tile = lambda i: (i, 0)
    const2 = lambda i: (0, 0)
    return pl.pallas_call(
        body, name="fwd_down", grid=(t // tm,),
        in_specs=[pl.BlockSpec((tm, D), tile), pl.BlockSpec((NCH, tm, SH_UP), lambda i: (0, i, 0)),
                  pl.BlockSpec((NCH, SH_UP, D), lambda i: (0, 0, 0)), pl.BlockSpec((1, D), const2),
                  pl.BlockSpec((tm, D), tile)],
        out_specs=[pl.BlockSpec((tm, D), tile), pl.BlockSpec((tm, D), tile), pl.BlockSpec((8, D), const2)],
        out_shape=[jax.ShapeDtypeStruct((t, D), F32), jax.ShapeDtypeStruct((t, D), BF16),
                   jax.ShapeDtypeStruct((8, D), F32)],
        compiler_params=_cparams(1, VMEM_BIG),
    )(x1, act, w_dn, gf, tgt)


def bwd_ffn(dx2, x1, g2, up, pre, w_up_g, fcw, w_dn, tm, seq):
    t = x1.shape[0]
    nt = t // tm
    tps = seq // tm

    def body(dx2_ref, x1_ref, g2_ref, up_ref, pre_ref, wup_ref, fcw_ref, wdn_ref,
             dup_ref, dx1_ref, gvec_ref, gn_ref, carry_ref):
        i = pl.program_id(0)
        it = (nt - 1 - i) % tps
        keep_next = jnp.where(it == tps - 1, 0.0, 1.0)

        @pl.when(i == 0)
        def _():
            gvec_ref[...] = jnp.zeros_like(gvec_ref)
            gn_ref[...] = jnp.zeros_like(gn_ref)
            carry_ref[...] = jnp.zeros_like(carry_ref)

        dx2v = dx2_ref[...]
        dxb = dx2v.astype(BF16)
        lanes = slice(0, SH_UP)
        dh2 = None
        for c in range(NCH):
            pre = [pre_ref[s, c].astype(F32) for s in range(2)]
            sg = jax.nn.sigmoid(pre[0])
            dact = _dot_nt(dxb, wdn_ref[c])
            dpre = [dact * pre[1] * (sg * (1.0 + pre[0] * (1.0 - sg))), dact * (pre[0] * sg)]
            for s in range(2):
                dc = dpre[s]
                ext = jnp.concatenate([dc, carry_ref[s, c] * keep_next], axis=0)
                carry_ref[s, c] = dc[:8]
                shifted = (_shift_up(ext, 2, tm), _shift_up(ext, 1, tm), dc)
                uf = up_ref[s, c].astype(F32)
                gvec_ref[s, c, 0:1, lanes] += jnp.sum(dc, axis=0, keepdims=True)
                for tap in range(3):
                    gvec_ref[s, c, tap + 1:tap + 2, lanes] += jnp.sum(shifted[tap] * uf, axis=0, keepdims=True)
                w = fcw_ref.at[s, c]
                du = w[2:3, :] * dc + w[1:2, :] * shifted[1] + w[0:1, :] * shifted[0]
                dub = du.astype(BF16)
                dup_ref[s, c] = dub
                part = _dot_nt(dub, wup_ref[s, c])
                dh2 = part if dh2 is None else dh2 + part
        x1v = x1_ref[...]
        inv2 = _rms_inv(x1v)
        xn = x1v * inv2
        gn_ref[0:1, :] += jnp.sum(dh2 * xn, axis=0, keepdims=True)
        dx1_ref[...] = dx2v + _rms_bwd(dh2, xn, inv2, g2_ref[...])

    rev = lambda i: (nt - 1 - i, 0)
    const2 = lambda i: (0, 0)
    whole = lambda i: (0, 0, 0, 0)
    chunks = pl.BlockSpec((2, NCH, tm, SH_UP), lambda i: (0, 0, nt - 1 - i, 0))
    return pl.pallas_call(
        body, name="bwd_ffn", grid=(nt,),
        in_specs=[pl.BlockSpec((tm, D), rev), pl.BlockSpec((tm, D), rev), pl.BlockSpec((1, D), const2),
                  chunks, chunks, pl.BlockSpec((2, NCH, D, SH_UP), whole), pl.BlockSpec((2, NCH, 3, SH_UP), whole),
                  pl.BlockSpec((NCH, SH_UP, D), lambda i: (0, 0, 0))],
        out_specs=[chunks, pl.BlockSpec((tm, D), rev), pl.BlockSpec((2, NCH, 8, D), whole),
                   pl.BlockSpec((8, D), const2)],
        out_shape=[jax.ShapeDtypeStruct((2, NCH, t, SH_UP), BF16), jax.ShapeDtypeStruct((t, D), F32),
                   jax.ShapeDtypeStruct((2, NCH, 8, D), F32), jax.ShapeDtypeStruct((8, D), F32)],
        scratch_shapes=[pltpu.VMEM((2, NCH, 8, SH_UP), F32)],
        compiler_params=_cparams(1, VMEM_BIG),
    )(dx2, x1, g2, up, pre, w_up_g.reshape(2, NCH, D, SH_UP), fcw.reshape(2, NCH, 3, SH_UP), w_dn)


def bwd_mix(dx1, zs, y_pool, y_conv, pool_w, pool_scale, conv_w, wmix, tm, seq):
    t = dx1.shape[0]
    nt = t // tm
    tps = seq // tm
    hb = tm // HALO

    def body(da_ref, z0_ref, z1_ref, z2_ref, zh0_ref, zh1_ref, zh2_ref, yp_ref, yc_ref, pw_ref, ps_ref, wpp_ref,
             cw_ref, wco_ref, wo_ref,
             dz_ref, mg_ref, p2_ref, u_ref, dyp_ref, dyc_ref, gpool_ref, gvec_ref, cp_ref, cc_ref):
        z_refs, zh_refs = (z0_ref, z1_ref, z2_ref), (zh0_ref, zh1_ref, zh2_ref)
        i = pl.program_id(0)
        it = (nt - 1 - i) % tps
        keep_hist = jnp.where(it == 0, 0.0, 1.0)
        keep_next = jnp.where(it == tps - 1, 0.0, 1.0)
        pos = it * tm + lax.broadcasted_iota(jnp.int32, (tm, 1), 0)

        @pl.when(i == 0)
        def _():
            gvec_ref[...] = jnp.zeros_like(gvec_ref)
            gpool_ref[...] = jnp.zeros_like(gpool_ref)
            cp_ref[...] = jnp.zeros_like(cp_ref)
            cc_ref[...] = jnp.zeros_like(cc_ref)

        dm =_dot_nt(da_ref[...].astype(BF16), wo_ref[...])
        merged, dyp, dyc = [], [], []
        for q in range(NG):
            lanes = slice(q * CG, (q + 1) * CG)
            sp = jax.nn.sigmoid(_z_tile(z_refs, 16 + q).astype(F32))
            sc = jax.nn.sigmoid(_z_tile(z_refs, 20 + q).astype(F32))
            yp = yp_ref[:, lanes].astype(F32)
            yc = yc_ref[:, lanes].astype(F32)
            dmq = dm[:, lanes]
            merged.append((sp * yp + sc * yc).astype(BF16))
            dyp.append((dmq * sp).astype(BF16))
            dyc.append((dmq * sc).astype(BF16))
            dz_ref[16 + q] = (dmq * yp * (sp * (1.0 - sp))).astype(BF16)
            dz_ref[20 + q] = (dmq * yc * (sc * (1.0 - sc))).astype(BF16)
        mg_ref[...] = jnp.concatenate(merged, axis=1)
        dypb = jnp.concatenate(dyp, axis=1)
        dycb = jnp.concatenate(dyc, axis=1)
        dyp_ref[...] = dypb
        dyc_ref[...] = dycb

        dp2 = _dot_nt(dypb, wpp_ref[...])
        p2 = []
        for g, win in enumerate(WINS):
            lanes = slice(g * CG, (g + 1) * CG)
            cnt = jnp.minimum(pos + 1, win).astype(F32)
            p = _pool_tile(_z_tile(z_refs, g), _z_tile(zh_refs, g), win, keep_hist, cnt)
            pb = p.astype(BF16)
            pw = _dot(pb, pw_ref[g])
            p2.append((pw * ps_ref[:, lanes]).astype(BF16))
            dp2g = dp2[:, lanes]
            gvec_ref[0:1, lanes] += jnp.sum(dp2g * pw, axis=0, keepdims=True)
            dpwb = (dp2g * ps_ref[:, lanes]).astype(BF16)
            gpool_ref[g] += _dot_tn(pb, dpwb)
            dp = _dot_nt(dpwb, pw_ref[g])
            qv = dp / cnt
            ext = jnp.concatenate([qv, cp_ref[g] * keep_next], axis=0)
            cp_ref[g] = qv[:HALO]
            n = tm + HALO
            s, sh = ext, 1
            while sh < win:
                s = s + pltpu.roll(s, n - sh, 0)
                sh *= 2
            dz_ref[g] = (s[:tm] - dp).astype(BF16)
        p2_ref[...] = jnp.concatenate(p2, axis=1)

        du = _dot_nt(dycb, wco_ref[...])
        u = []
        for q in range(NG):
            lanes = slice(q * CG, (q + 1) * CG)
            zb = _z_tile(z_refs, 4 + q).astype(F32)
            zc = _z_tile(z_refs, 8 + q).astype(F32)
            zv = _z_tile(z_refs, 12 + q).astype(F32)
            cv = zc * zv
            cvh = _z_tile(zh_refs, 8 + q).astype(F32) * _z_tile(zh_refs, 12 + q).astype(F32) * keep_hist
            cc, cv1, cv2 = _conv_taps(jnp.concatenate([cvh, cv], axis=0), cv, cw_ref, lanes, HALO)
            u.append((zb * cc).astype(BF16))
            duq = du[:, lanes]
            dz_ref[4 + q] = (duq * cc).astype(BF16)
            dcc = duq * zb
            for tap, src in enumerate((cv2, cv1, cv)):
                gvec_ref[tap + 1:tap + 2, lanes] += jnp.sum(dcc * src, axis=0, keepdims=True)
            ext = jnp.concatenate([dcc, cc_ref[:, lanes] * keep_next], axis=0)
            cc_ref[:, lanes] = dcc[:8]
            dcv = (cw_ref[2:3, lanes] * dcc + cw_ref[1:2, lanes] * _shift_up(ext, 1, tm)
                   + cw_ref[0:1, lanes] * _shift_up(ext, 2, tm))
            dz_ref[8 + q] = (dcv * zv).astype(BF16)
            dz_ref[12 + q] = (dcv * zc).astype(BF16)
        u_ref[...] = jnp.concatenate(u, axis=1)

    def hist(i):
        return jnp.maximum((nt - 1 - i) * hb - 1, 0)

    rev = lambda i: (nt - 1 - i, 0)
    rev3 = lambda i: (0, nt - 1 - i, 0)
    const2 = lambda i: (0, 0)
    tok = jax.ShapeDtypeStruct((t, D), BF16)
    return pl.pallas_call(
        body, name="bwd_mix", grid=(nt,),
        in_specs=[pl.BlockSpec((tm, D), rev)] + [pl.BlockSpec((NDEV, tm, CG), rev3)] * 3
                 + [pl.BlockSpec((NDEV, HALO, CG), lambda i: (0, hist(i), 0))] * 3
                 + [pl.BlockSpec((tm, D), rev), pl.BlockSpec((tm, D), rev),
                    pl.BlockSpec((NG, CG, CG), lambda i: (0, 0, 0)), pl.BlockSpec((1, D), const2),
                    pl.BlockSpec((D, D), lambda i: (0, MIX_POOL_PROJ)), pl.BlockSpec((3, D), const2),
                    pl.BlockSpec((D, D), lambda i: (0, MIX_CONV_OUT)), pl.BlockSpec((D, D), lambda i: (0, MIX_O))],
        out_specs=[pl.BlockSpec((NZT, tm, CG), rev3)] + [pl.BlockSpec((tm, D), rev)] * 5
                  + [pl.BlockSpec((NG, CG, CG), lambda i: (0, 0, 0)), pl.BlockSpec((8, D), const2)],
        out_shape=[jax.ShapeDtypeStruct((NZT, t, CG), BF16), tok, tok, tok, tok, tok,
                   jax.ShapeDtypeStruct((NG, CG, CG), F32), jax.ShapeDtypeStruct((8, D), F32)],
        scratch_shapes=[pltpu.VMEM((NG, HALO, CG), F32), pltpu.VMEM((8, D), F32)],
        compiler_params=_cparams(1, VMEM_BIG),
    )(dx1, *zs, *zs, y_pool, y_conv, pool_w, pool_scale, wmix, conv_w, wmix, wmix)


def bwd_in(dz, w_in_pieces, dx1, x, g1, tm):
    t = x.shape[0]

    def body(dz_ref, w0_ref, w1_ref, w2_ref, dx1_ref, x_ref, g_ref, gx_ref, gn_ref):
        @pl.when(pl.program_id(0) == 0)
        def _():
            gn_ref[...] = jnp.zeros_like(gn_ref)

        dh = None
        for j in range(NDEV):
            for q, w_ref in enumerate((w0_ref, w1_ref, w2_ref)):
                part = _dot_nt(dz_ref[3 * j + q], w_ref[j])
                dh = part if dh is None else dh + part
        xv = x_ref[...]
        inv = _rms_inv(xv)
        xn = xv * inv
        gn_ref[0:1, :] += jnp.sum(dh * xn, axis=0, keepdims=True)
        gx_ref[...] = dx1_ref[...] + _rms_bwd(dh, xn, inv, g_ref[...])

    tile = lambda i: (i, 0)
    return pl.pallas_call(
        body, name="bwd_in", grid=(t // tm,),
        in_specs=[pl.BlockSpec((NZT, tm, CG), lambda i: (0, i, 0))]
                 + [pl.BlockSpec((NDEV, D, CG), lambda i: (0, 0, 0))] * 3
                 + [pl.BlockSpec((tm, D), tile), pl.BlockSpec((tm, D), tile), pl.BlockSpec((1, D), lambda i: (0, 0))],
        out_specs=[pl.BlockSpec((tm, D), tile), pl.BlockSpec((8, D), lambda i: (0, 0))],
        out_shape=[jax.ShapeDtypeStruct((t, D), F32), jax.ShapeDtypeStruct((8, D), F32)],
        compiler_params=_cparams(1, VMEM_BIG),
    )(dz, *w_in_pieces, dx1, x, g1)


def _slot(j):
    return j % 2, j // 2


def wgrad_cols(at, b, q, name):
    m, t = at.shape
    width = b.shape[3]

    def body(a_ref, b_ref, o_ref):
        o_ref[...] = _dot(a_ref[...], b_ref[...])

    return pl.pallas_call(
        body, name=name, grid=(NDEV,),
        in_specs=[pl.BlockSpec((m, t), lambda j: (0, 0)),
                  pl.BlockSpec((None, None, t, width), lambda j: (j, q, 0, 0))],
        out_specs=pl.BlockSpec((None, None, m, width), lambda j: (j % 2, j // 2, 0, 0)),
        out_shape=jax.ShapeDtypeStruct((2, 4, m, width), F32),
        compiler_params=_cparams(1, VMEM_BIG),
    )(at, b)


def wgrad_down(actt, dx2b):
    t = dx2b.shape[0]

    def body(a_ref, b_ref, o_ref):
        r = _dot(a_ref[...], b_ref[...])
        o_ref[0] = r[:SH_DN]
        o_ref[1] = r[SH_DN:]

    return pl.pallas_call(
        body, name="wgrad_down", grid=(NCH,),
        in_specs=[pl.BlockSpec((None, SH_UP, t), lambda k: (k, 0, 0)), pl.BlockSpec((t, D), lambda k: (0, 0))],
        out_specs=pl.BlockSpec((2, None, SH_DN, D), lambda k: (0, k, 0, 0)),
        out_shape=jax.ShapeDtypeStruct((2, 4, SH_DN, D), F32),
        compiler_params=_cparams(1, VMEM_BIG),
    )(actt, dx2b)


def wgrad_square(a, b, name, tk):
    t = a.shape[0]

    def body(a_ref, b_ref, o_ref, acc_ref):
        kt = pl.program_id(0)

        @pl.when(kt == 0)
        def _():
            acc_ref[...] = jnp.zeros_like(acc_ref)

        acc_ref[...] += _dot_tn(a_ref[...], b_ref[...].astype(BF16))

        @pl.when(kt == pl.num_programs(0) - 1)
        def _():
            for j in range(NDEV):
                cc, xy = _slot(j)
                o_ref[cc, xy] = acc_ref[j * 128:(j + 1) * 128]

    return pl.pallas_call(
        body, name=name, grid=(t // tk,),
        in_specs=[pl.BlockSpec((tk, D), lambda k: (k, 0)), pl.BlockSpec((tk, D), lambda k: (k, 0))],
        out_specs=pl.BlockSpec((2, 4, 128, D), lambda k: (0, 0, 0, 0)),
        out_shape=jax.ShapeDtypeStruct((2, 4, 128, D), F32),
        scratch_shapes=[pltpu.VMEM((D, D), F32)],
        compiler_params=_cparams(1, VMEM_BIG),
    )(a, b)


def _adamw(w, g, m, v):
    m = ADAM_B1 * m + (1.0 - ADAM_B1) * g
    v = ADAM_B2 * v + (1.0 - ADAM_B2) * (g * g)
    m_hat = m / (1.0 - ADAM_B1 ** ADAM_STEP)
    v_hat = v / (1.0 - ADAM_B2 ** ADAM_STEP)
    delta = -ADAM_LR * (m_hat / (jnp.sqrt(v_hat) + ADAM_EPS) + ADAM_WD * w)
    return delta, m, v


def _row_block(r):
    return 512 if r % 512 == 0 else r


def chip_partial(place, gs, from_sibling, name):
    n = len(gs)
    shapes = [g.shape[2:] for g in gs]

    def body(place_ref, *refs):
        for g_ref, s_ref, o_ref in zip(refs[:n], refs[n:2 * n], refs[2 * n:]):
            o_ref[...] = (g_ref[...] + s_ref[...]).astype(BF16)

    def slot(rc):
        return pl.BlockSpec((None,) + rc, lambda k, pr: (pr[1] ^ (k + 1), 0, 0))

    return pl.pallas_call(
        body, name=name,
        grid_spec=pltpu.PrefetchScalarGridSpec(
            num_scalar_prefetch=1, grid=(3,),
            in_specs=[pl.BlockSpec((None, None) + rc, lambda k, pr: (pr[0], pr[1] ^ (k + 1), 0, 0)) for rc in shapes]
                     + [slot(rc) for rc in shapes],
            out_specs=[slot(rc) for rc in shapes]),
        out_shape=[jax.ShapeDtypeStruct((4,) + rc, BF16) for rc in shapes],
        compiler_params=_cparams(1, VMEM_BIG),
    )(place, *gs, *from_sibling)


def finish_adamw(place, gs, from_sibling, from_chips, w, m, v, name, transposed=False):
    n = len(gs)
    r = gs[0].shape[2]
    widths = [g.shape[3] for g in gs]
    c = sum(widths)
    br = _row_block(r)

    def body(place_ref, *refs):
        g_refs, s_refs, c_refs = refs[:n], refs[n:2 * n], refs[2 * n:5 * n]
        w_ref, m_ref, v_ref, og_ref, od_ref, om_ref, ov_ref = refs[5 * n:]
        cols = []
        for q in range(n):
            grad = g_refs[q][...] + s_refs[q][...]
            for k in range(3):
                grad = grad + c_refs[3 * q + k][...].astype(F32)
            cols.append(grad)
        grad = cols[0] if n == 1 else jnp.concatenate(cols, axis=1)
        if transposed:
            grad = grad.T
        og_ref[...] = grad
        od_ref[...], om_ref[...], ov_ref[...] = _adamw(w_ref[...], grad, m_ref[...], v_ref[...])

    def other(k, cq):
        return pl.BlockSpec((None, br, cq), lambda i, pr: (pr[1] ^ k, i, 0))

    row = pl.BlockSpec((c, br), lambda i, pr: (0, i)) if transposed else pl.BlockSpec((br, c), lambda i, pr: (i, 0))
    out = jax.ShapeDtypeStruct((c, r) if transposed else (r, c), F32)
    in_specs = [pl.BlockSpec((None, None, br, cq), lambda i, pr: (pr[0], pr[1], i, 0)) for cq in widths]
    in_specs += [pl.BlockSpec((None, br, cq), lambda i, pr: (pr[1], i, 0)) for cq in widths]
    in_specs += [other(k, cq) for cq in widths for k in (1, 2, 3)]
    return pl.pallas_call(
        body, name=name,
        grid_spec=pltpu.PrefetchScalarGridSpec(
            num_scalar_prefetch=1, grid=(r // br,), in_specs=in_specs + [row, row, row], out_specs=[row] * 4),
        out_shape=[out] * 4,
        compiler_params=_cparams(1, VMEM_BIG),
    )(place, *gs, *from_sibling, *[fc for fc in from_chips for _ in range(3)], w, m, v)


def adamw_small(items):
    n = len(items)

    def body(*refs):
        ins, outs = refs[:4 * n], refs[4 * n:]
        for i in range(n):
            w, g, m, v = (r[...] for r in ins[4 * i:4 * i + 4])
            outs[3 * i][...], outs[3 * i + 1][...], outs[3 * i + 2][...] = _adamw(w, g, m, v)

    out = [jax.ShapeDtypeStruct(it[0].shape, F32) for it in items for _ in range(3)]
    res = pl.pallas_call(body, name="adamw_small", out_shape=out)(*[a for it in items for a in it])
    return [res[3 * i:3 * i + 3] for i in range(n)]


def kernel(x, norm_mix, w_in, pool_w, pool_scale, w_pool_proj, conv_w, w_conv_out, w_o, norm_ffn, w_up, ffn_conv_w, ffn_conv_b, w_down, norm_final, loss_target, m_norm_mix, m_w_in, m_pool_w, m_pool_scale, m_w_pool_proj, m_conv_w, m_w_conv_out, m_w_o, m_norm_ffn, m_w_up, m_ffn_conv_w, m_ffn_conv_b, m_w_down, m_norm_final, v_norm_mix, v_w_in, v_pool_w, v_pool_scale, v_w_pool_proj, v_conv_w, v_w_conv_out, v_w_o, v_norm_ffn, v_w_up, v_ffn_conv_w, v_ffn_conv_b, v_w_down, v_norm_final):
    nb, seq, _ = x.shape
    t = nb * seq
    tm_in = min(TM_IN, t)
    tm_mix = min(TM_MIX, seq)
    tm_ffn = min(TM_FFN, seq)
    tk = min(TK_WGRAD, t)
    xt = x.reshape(t, D)
    tgt = loss_target.reshape(t, D)
    xi, yi, ci = _pos()
    me = 4 * xi + 2 * yi + ci
    place = jnp.stack([ci, 2 * xi + yi]).astype(jnp.int32)

    tie = lax.optimization_barrier
    w_in_b = w_in[0].astype(BF16)
    w_in_g = [all_gather_blocks([w_in_b[:, q * CG:(q + 1) * CG]], f"all_gather_w_in_{q}", 0)[0] for q in range(3)]
    taps = (jnp.pad(conv_w[0], ((0, 5), (0, D - 128))) + jnp.pad(ffn_conv_w[0], ((3, 2), (0, D - SH_UP))))
    taps_g = _exchange_small(taps, "all_gather_taps")
    mix_shard = jnp.concatenate(
        [w_pool_proj[0], w_conv_out[0], w_o[0], pool_w[0].reshape(NG * 32, CG)], axis=1).astype(BF16)
    conv_w_f = taps_g[:, 0:3, :128].transpose(1, 0, 2).reshape(3, D)
    fcw_f = taps_g[:, 3:6, :SH_UP]
    fcb_f = ffn_conv_b.reshape(NDEV, 1, SH_UP)
    mix_shard, conv_w_f, fcw_f, fcb_f = tie((mix_shard, conv_w_f, fcw_f, fcb_f))
    wmix_g, = all_gather_blocks([mix_shard], "all_gather_w_mix", 0)
    ffn_shards, w_in_g[0] = tie(([w_up[0].astype(BF16), w_down[0].astype(BF16)], w_in_g[0]))
    w_up_g, = all_gather_blocks(ffn_shards[:1], "all_gather_w_up", 0)
    w_dn_g, = all_gather_blocks(ffn_shards[1:], "all_gather_w_down", 0)
    w_dn_f = w_dn_g.reshape(NCH, SH_UP, D)
    gfin = norm_final.reshape(1, D)

    zs, h1 = fwd_in(xt, norm_mix, w_in_g, tm_in)
    wmix_g, zs = tie((wmix_g, zs))
    wmix = wmix_g.reshape(D, MIX_COLS)
    pool_w_f = wmix_g[:, :, 3 * D:].reshape(NDEV, NG, 32, CG).transpose(1, 0, 2, 3).reshape(NG, CG, CG)
    x1, y_pool, y_conv = fwd_mix(zs, xt, pool_w_f, pool_scale, conv_w_f, wmix, tm_mix, seq)
    up, pre, act_tok, act, h2 = fwd_up(x1, norm_ffn, w_up_g, fcw_f, fcb_f, tm_ffn, seq)
    dx2, dx2b, ffn_vec = fwd_down(x1, act_tok, w_dn_f, gfin, tgt, min(TM_IN, t))

    def to_sibling(full, tag):
        return reduce_scatter_d2d(full, "reduce_scatter_d2d_" + tag, 1)

    def partials(full, from_sib, tag):
        return chip_partial(place, full, from_sib, "chip_partial_" + tag)

    def to_chips(parts, tag):
        return reduce_scatter_ici(parts, "reduce_scatter_ici_" + tag, 2)

    def finish(nm, gs, from_sib, from_chips, wmv, transposed=False):
        rc = (gs[0].shape[2], sum(g.shape[3] for g in gs))
        wmv2 = [a.reshape(rc).T if transposed else a.reshape(rc) for a in wmv]
        outs = finish_adamw(place, gs, from_sib, from_chips, *wmv2, "adamw_" + nm, transposed)
        return [(o.T if transposed else o).reshape(wmv[0].shape) for o in outs]

    def after(x, dep):
        return tie((x, dep))[0]

    big = {}
    d_up, dx1, g_ffn_vec, g_nffn = bwd_ffn(dx2, x1, norm_ffn, up, pre, w_up_g, fcw_f, w_dn_f, tm_ffn, seq)
    gw_up = wgrad_cols(h2, d_up.reshape(NDEV, 1, t, SH_UP), 0, "wgrad_up")
    sib_up = to_sibling([gw_up], "w_up")
    gw_dn = wgrad_down(act, after(dx2b, gw_up))
    sib_dn = to_sibling([after(gw_dn, sib_up)], "w_down")
    dx1, part_up = tie((dx1, partials([gw_up], sib_up, "w_up")))
    chips_up = to_chips(part_up, "w_up")
    dz, merged, p2, u, dyp, dyc, g_pool, g_mix_vec = bwd_mix(
        dx1, zs, y_pool, y_conv, pool_w_f, pool_scale, conv_w_f, wmix, tm_mix, seq)
    merged, part_dn = tie((merged, partials([gw_dn], sib_dn, "w_down")))
    chips_dn = to_chips(part_dn, "w_down")
    gw_o = wgrad_square(merged, dx1, "wgrad_o", tk)
    gw_pp = wgrad_square(p2, dyp, "wgrad_pool_proj", tk)
    gw_co = wgrad_square(u, dyc, "wgrad_conv_out", tk)
    gw_pool = g_pool.reshape(NG, 4, 2, 32, CG).transpose(2, 1, 0, 3, 4).reshape(2, 4, NG * 32, CG)
    dz8 = dz.reshape(NDEV, 3, t, CG)
    gw_in, sib_in, chips_in = [None] * 3, [None] * 3, [None] * 3
    sib_a = to_sibling(after([gw_o, gw_pp], (chips_up, gw_pool, chips_dn)), "mix_a")
    sib_b = to_sibling(after([gw_co, gw_pool], sib_a), "mix_b")
    gw_in[0] = wgrad_cols(h1, dz8, 0, "wgrad_in_0")
    h1, part_a, part_b = tie((h1, partials([gw_o, gw_pp], sib_a, "mix_a"),
                              partials([gw_co, gw_pool], sib_b, "mix_b")))
    chips_a = to_chips(after(part_a, chips_dn), "mix_a")
    chips_b = to_chips(part_b, "mix_b")
    sib_in[0] = to_sibling(after([gw_in[0]], sib_b), "w_in_0")
    gw_in[1] = wgrad_cols(h1, dz8, 1, "wgrad_in_1")
    h1, part_in0, gw_in[1] = tie((h1, partials([gw_in[0]], sib_in[0], "w_in_0"), gw_in[1]))
    chips_in[0] = to_chips(part_in0, "w_in_0")
    sib_in[1] = to_sibling(after([gw_in[1]], sib_in[0]), "w_in_1")
    h1, big["w_down"], big["w_up"] = tie((
        h1, finish("w_down", [gw_dn], sib_dn, chips_dn, (w_down, m_w_down, v_w_down)),
        finish("w_up", [gw_up], sib_up, chips_up, (w_up, m_w_up, v_w_up), transposed=True)))
    gw_in[2] = wgrad_cols(h1, dz8, 2, "wgrad_in_2")
    sib_in[2] = to_sibling(after([gw_in[2]], (chips_a, chips_b, chips_in[0])), "w_in_2")
    sib_in[2], big["w_o"], big["w_pool_proj"], big["w_conv_out"], big["pool_w"] = tie((
        sib_in[2],
        finish("w_o", [gw_o], sib_a[:1], chips_a[:1], (w_o, m_w_o, v_w_o)),
        finish("w_pool_proj", [gw_pp], sib_a[1:], chips_a[1:], (w_pool_proj, m_w_pool_proj, v_w_pool_proj)),
        finish("w_conv_out", [gw_co], sib_b[:1], chips_b[:1], (w_conv_out, m_w_conv_out, v_w_conv_out)),
        finish("pool_w", [gw_pool], sib_b[1:], chips_b[1:], (pool_w, m_pool_w, v_pool_w))))
    dx1, part_in12 = tie((dx1, partials(gw_in[1:], sib_in[1] + sib_in[2], "w_in_12")))
    chips_in[1] = to_chips(part_in12[:1], "w_in_1")
    chips_in[2] = to_chips(part_in12[1:], "w_in_2")
    small_g, = all_gather_blocks(
        [after(jnp.concatenate([g_mix_vec, g_nffn, ffn_vec, g_ffn_vec.reshape(8 * NDEV, D)], axis=0), sib_in[2])],
        "all_gather_small", 0)
    grad_x, g_nmix = bwd_in(dz, w_in_g, dx1, xt, norm_mix, min(TM_BWD_IN, t))
    grad_x, chips_in = tie((grad_x, chips_in))
    nmix_g, = all_gather_blocks([g_nmix], "all_gather_norm_mix", 0)
    big["w_in"] = finish("w_in", gw_in, [s[0] for s in sib_in], [c[0] for c in chips_in], (w_in, m_w_in, v_w_in))

    red, = sum_blocks([small_g], "sum_small")
    red_n, = sum_blocks([after(nmix_g, (big["w_in"], red))], "sum_norm_mix")
    g_norm_mix, g_pool_scale, g_norm_ffn = red_n[0:1], red[0:1], red[8:9]
    g_conv_w = lax.dynamic_slice(red, (1, me * 128), (3, 128))
    g_norm_final = red[16]
    loss = red[17, 0]
    g_fcb = red[24:].reshape(NDEV, 8, D)[:, 0, :SH_UP].reshape(1, FF2)
    g_fcw = lax.dynamic_slice(red, (25 + 8 * me, 0), (3, SH_UP))
    grads = {"norm_mix": g_norm_mix, "pool_scale": g_pool_scale, "norm_ffn": g_norm_ffn, "norm_final": g_norm_final,
             "ffn_conv_b": g_fcb, "conv_w": g_conv_w.reshape(1, 3, 128), "ffn_conv_w": g_fcw.reshape(1, 3, SH_UP)}
    small_wmv = {"norm_mix": (norm_mix, m_norm_mix, v_norm_mix), "pool_scale": (pool_scale, m_pool_scale, v_pool_scale),
                 "norm_ffn": (norm_ffn, m_norm_ffn, v_norm_ffn), "norm_final": (norm_final, m_norm_final, v_norm_final),
                 "ffn_conv_b": (ffn_conv_b, m_ffn_conv_b, v_ffn_conv_b), "conv_w": (conv_w, m_conv_w, v_conv_w),
                 "ffn_conv_w": (ffn_conv_w, m_ffn_conv_w, v_ffn_conv_w)}
    small_names = list(small_wmv)
    flat2 = lambda a: a.reshape(1, -1) if a.ndim == 1 else (a.transpose(1, 0, 2) if a.ndim == 3 else a)
    unflat = lambda o, like: o.transpose(1, 0, 2) if like.ndim == 3 else o.reshape(like.shape)
    small_out = adamw_small([(flat2(small_wmv[nm][0]), flat2(grads[nm]), flat2(small_wmv[nm][1]),
                              flat2(small_wmv[nm][2])) for nm in small_names])
    small = {nm: [unflat(o, small_wmv[nm][0]) for o in outs] for nm, outs in zip(small_names, small_out)}

    order = ["norm_mix", "w_in", "pool_w", "pool_scale", "w_pool_proj", "conv_w", "w_conv_out", "w_o", "norm_ffn",
             "w_up", "ffn_conv_w", "ffn_conv_b", "w_down", "norm_final"]
    out = [loss, grad_x.reshape(nb, seq, D)]
    out += [big[nm][0] if nm in big else grads[nm] for nm in order]
    for idx in range(3):
        out += [big[nm][idx + 1] if nm in big else small[nm][idx] for nm in order]
    return tuple(out)
```

```python
import jax
import jax.numpy as jnp
from jax import lax
from jax.experimental import pallas as pl
from jax.experimental.pallas import tpu as pltpu
from jax.experimental.pallas import tpu_sc as plsc

F32 = jnp.float32
BF16 = jnp.bfloat16

NDEV = 8
D = 1024
NG = 4
CG = 256
WINS = (2, 4, 8, 16)
DIN = 6 * D
SH_IN = DIN // NDEV
NZT = DIN // CG
FF2 = 5632
SH_UP = FF2 // NDEV
FF = FF2 // 2
NCH = 4
SH_DN = FF // NDEV
RMS_EPS = 1e-6
HALO = 16

ADAM_LR = 0.001
ADAM_B1 = 0.9
ADAM_B2 = 0.999
ADAM_EPS = 1e-08
ADAM_WD = 0.01
ADAM_STEP = 10

TM_IN = 512
TM_BWD_IN = 256
TM_MIX = 256
TM_FFN = 256
TK_WGRAD = 1024
MIX_POOL_PROJ, MIX_CONV_OUT, MIX_O = 0, 1, 2
MIX_COLS = 3 * D + CG
VMEM_BIG = 56 * 1024 * 1024
MESH = pl.DeviceIdType.MESH
ANY = pl.BlockSpec(memory_space=pl.ANY)


def _cparams(n_axes, vmem=None):
    return pltpu.CompilerParams(dimension_semantics=("arbitrary",) * n_axes, vmem_limit_bytes=vmem)


def _dot(a, b):
    return jnp.dot(a, b, preferred_element_type=F32)


def _dot_nt(a, b):
    return lax.dot_general(a, b, (((1,), (1,)), ((), ())), preferred_element_type=F32)


def _dot_tn(a, b):
    return lax.dot_general(a, b, (((0,), (0,)), ((), ())), preferred_element_type=F32)


def _shift_down(ext, s, lead):
    return pltpu.roll(ext, s, 0)[lead:]


def _shift_up(ext, s, tm):
    n = ext.shape[0]
    return pltpu.roll(ext, n - s, 0)[:tm]


def _rms_inv(x):
    return lax.rsqrt(jnp.mean(x * x, axis=-1, keepdims=True) + RMS_EPS)


def _rms_bwd(dh, xn, inv, g):
    dxn = dh * g
    return inv * (dxn - xn * jnp.mean(dxn * xn, axis=-1, keepdims=True))


def _pos():
    return lax.axis_index("x"), lax.axis_index("y"), lax.axis_index("c")


def _handshake(peers):
    barrier = pltpu.get_barrier_semaphore()
    for peer in peers:
        pl.semaphore_signal(barrier, inc=1, device_id=peer, device_id_type=MESH)
    pl.semaphore_wait(barrier, len(peers))


def _sequencer(body, out_type, n_sems, name, collective_id):
    return pl.kernel(
        body, out_type=out_type, mesh=plsc.ScalarSubcoreMesh(axis_name="sequencer", num_cores=1), name=name,
        scratch_types=[pltpu.SemaphoreType.DMA((n_sems,)), pltpu.SemaphoreType.DMA((n_sems,))],
        compiler_params=pltpu.CompilerParams(collective_id=collective_id))


def all_gather_blocks(shards, name, collective_id):
    n = len(shards)

    def body(*refs):
        ins, outs = refs[:n], refs[n:2 * n]
        send_sems, recv_sems = refs[2 * n:]
        x, y, c = _pos()
        me, sibling = (x, y, c), (x, y, 1 - c)
        first_chip, second_chip, diagonal = (x ^ (1 - c), y ^ c), (x ^ c, y ^ (1 - c)), (1 - x, 1 - y)
        first, second = (*first_chip, c), (*second_chip, c)
        _handshake([sibling, first, second])

        def copy(w, k, block, to, src=None):
            slot = outs[w].at[4 * block[0] + 2 * block[1] + block[2]]
            return pltpu.make_async_remote_copy(
                src_ref=slot if src is None else src, dst_ref=slot,
                send_sem=send_sems.at[8 * w + k], recv_sem=recv_sems.at[8 * w + k], device_id=to, device_id_type=MESH)

        mine, sent = [], []
        for w in range(n):
            m = pltpu.make_async_copy(ins[w], outs[w].at[4 * x + 2 * y + c], send_sems.at[8 * w + 7])
            m.start()
            mine.append(m)
            sent += [copy(w, k, me, to, src=ins[w]) for k, to in enumerate((sibling, first, second))]
        for cp in sent:
            cp.start()
        for k, chip in ((1, first_chip), (2, second_chip), (3, diagonal)):
            for w in range(n):
                copy(w, k, (*chip, c), me).wait_recv()
                onward = [copy(w, 3 + k, (*chip, c), sibling)] + ([copy(w, 3, (*chip, c), second)] if k == 1 else [])
                for cp in onward:
                    cp.start()
                sent += onward
        for w in range(n):
            copy(w, 0, sibling, me).wait_recv()
            for k, chip in ((4, second_chip), (5, first_chip), (6, diagonal)):
                copy(w, k, (*chip, 1 - c), me).wait_recv()
        for cp in sent:
            cp.wait_send()
        for m in mine:
            m.wait()

    out = [jax.ShapeDtypeStruct((NDEV,) + s.shape, s.dtype) for s in shards]
    return _sequencer(body, out, 8 * n, name, collective_id)(*shards)


def _exchange_small(v, name):
    rows = v.shape[0]

    def body(v_ref, out_ref, slots, send_sems, recv_sems, local_sem):
        x, y, c = _pos()
        me = 4 * x + 2 * y + c
        mine = pltpu.make_async_copy(v_ref, slots.at[me], local_sem)
        mine.start()
        offs = [(dx, dy, dc) for dx in (0, 1) for dy in (0, 1) for dc in (0, 1)][1:]

        def copy(k, src_slot, to):
            return pltpu.make_async_remote_copy(
                src_ref=v_ref, dst_ref=slots.at[src_slot], send_sem=send_sems.at[k], recv_sem=recv_sems.at[k],
                device_id=to, device_id_type=MESH)

        sends = []
        for k, (dx, dy, dc) in enumerate(offs):
            cp = copy(k, me, (x ^ dx, y ^ dy, c ^ dc))
            cp.start()
            sends.append(cp)
        for k, (dx, dy, dc) in enumerate(offs):
            copy(k, 4 * (x ^ dx) + 2 * (y ^ dy) + (c ^ dc), (x, y, c)).wait_recv()
        for cp in sends:
            cp.wait_send()
        mine.wait()
        out_ref[...] = slots[...]

    vmem = pl.BlockSpec(memory_space=pltpu.VMEM)
    return pl.pallas_call(
        body, name=name, out_shape=jax.ShapeDtypeStruct((NDEV, rows, D), F32), in_specs=[vmem], out_specs=vmem,
        scratch_shapes=[pltpu.VMEM((NDEV, rows, D), F32), pltpu.SemaphoreType.DMA((7,)),
                        pltpu.SemaphoreType.DMA((7,)), pltpu.SemaphoreType.DMA],
    )(v)


def sum_blocks(gathered, name):
    def body(*refs):
        for g_ref, o_ref in zip(refs[:len(gathered)], refs[len(gathered):]):
            acc = g_ref[0]
            for d in range(1, NDEV):
                acc = acc + g_ref[d]
            o_ref[...] = acc

    vmem = pl.BlockSpec(memory_space=pltpu.VMEM)
    return pl.pallas_call(
        body, name=name, out_shape=[jax.ShapeDtypeStruct(g.shape[1:], F32) for g in gathered],
        in_specs=[vmem] * len(gathered), out_specs=[vmem] * len(gathered),
    )(*gathered)


def reduce_scatter_d2d(grads, name, collective_id):
    n = len(grads)

    def body(*refs):
        ins, outs = refs[:n], refs[n:2 * n]
        send_sems, recv_sems = refs[2 * n:]
        x, y, c = _pos()
        _handshake([(x, y, 1 - c)])
        cps = []
        for w in range(n):
            cp = pltpu.make_async_remote_copy(
                src_ref=ins[w].at[1 - c], dst_ref=outs[w], send_sem=send_sems.at[w], recv_sem=recv_sems.at[w],
                device_id=(x, y, 1 - c), device_id_type=MESH)
            cp.start()
            cps.append(cp)
        for cp in cps:
            cp.wait_recv()
        for cp in cps:
            cp.wait_send()

    out = [jax.ShapeDtypeStruct(g.shape[1:], F32) for g in grads]
    return _sequencer(body, out, n, name, collective_id)(*grads)


def reduce_scatter_ici(parts, name, collective_id):
    n = len(parts)

    def body(*refs):
        ins, outs = refs[:n], refs[n:2 * n]
        send_sems, recv_sems = refs[2 * n:]
        x, y, c = _pos()
        offs = [(1, 0), (0, 1), (1, 1)]
        _handshake([(x ^ dx, y ^ dy, c) for dx, dy in offs])
        cps = []
        for w in range(n):
            for k, (dx, dy) in enumerate(offs):
                ox, oy = x ^ dx, y ^ dy
                cp = pltpu.make_async_remote_copy(
                    src_ref=ins[w].at[2 * ox + oy], dst_ref=outs[w].at[2 * x + y],
                    send_sem=send_sems.at[3 * w + k], recv_sem=recv_sems.at[3 * w + k],
                    device_id=(ox, oy, c), device_id_type=MESH)
                cp.start()
                cps.append((cp, w, k, ox, oy))
        for cp, w, k, ox, oy in cps:
            pltpu.make_async_remote_copy(
                src_ref=ins[w].at[2 * ox + oy], dst_ref=outs[w].at[2 * ox + oy],
                send_sem=send_sems.at[3 * w + k], recv_sem=recv_sems.at[3 * w + k],
                device_id=(ox, oy, c), device_id_type=MESH).wait_recv()
        for cp, *_ in cps:
            cp.wait_send()

    out = [jax.ShapeDtypeStruct(p.shape, BF16) for p in parts]
    return _sequencer(body, out, 3 * n, name, collective_id)(*parts)


def fwd_in(x, g1, w_in_pieces, tm):
    t = x.shape[0]
    tile = lambda i: (i, 0)
    w_spec = pl.BlockSpec((NDEV, D, CG), lambda i: (0, 0, 0))
    z_spec = pl.BlockSpec((NDEV, tm, CG), lambda i: (0, i, 0))
    z_shape = jax.ShapeDtypeStruct((NDEV, t, CG), BF16)

    def cost(other_bytes, transcendentals):
        return pl.CostEstimate(flops=2 * t * D * NDEV * CG, transcendentals=transcendentals,
                               bytes_accessed=other_bytes + 2 * D * NDEV * CG + 2 * t * NDEV * CG)

    def first(x_ref, g_ref, w_ref, z_ref, h_ref, ht_ref):
        xf = x_ref[...]
        h = (xf * _rms_inv(xf) * g_ref[...]).astype(BF16)
        h_ref[...] = h
        ht_ref[...] = h.T
        for j in range(NDEV):
            z_ref[j] = _dot(h, w_ref[j]).astype(BF16)

    z0, h, ht = pl.pallas_call(
        first, name="fwd_in_0", grid=(t // tm,),
        in_specs=[pl.BlockSpec((tm, D), tile), pl.BlockSpec((1, D), lambda i: (0, 0)), w_spec],
        out_specs=[z_spec, pl.BlockSpec((tm, D), tile), pl.BlockSpec((D, tm), lambda i: (0, i))],
        out_shape=[z_shape, jax.ShapeDtypeStruct((t, D), BF16), jax.ShapeDtypeStruct((D, t), BF16)],
        compiler_params=_cparams(1, VMEM_BIG), cost_estimate=cost(8 * t * D + 4 * D, t),
    )(x, g1, w_in_pieces[0])
    zs = [z0]
    for q in (1, 2):
        h, zs[-1] = lax.optimization_barrier((h, zs[-1]))

        def later(h_ref, w_ref, z_ref):
            hb = h_ref[...]
            for j in range(NDEV):
                z_ref[j] = _dot(hb, w_ref[j]).astype(BF16)

        zs.append(pl.pallas_call(
            later, name=f"fwd_in_{q}", grid=(t // tm,),
            in_specs=[pl.BlockSpec((tm, D), tile), w_spec], out_specs=z_spec, out_shape=z_shape,
            compiler_params=_cparams(1, VMEM_BIG), cost_estimate=cost(2 * t * D, 0),
        )(h, w_in_pieces[q]))
    return zs, ht


def _z_tile(z_refs, n):
    return z_refs[n % 3][n // 3]


def _pool_tile(z, zh, win, keep_hist, cnt):
    zt = z.astype(F32)
    ext = jnp.concatenate([zh.astype(F32) * keep_hist, zt], axis=0)
    s, sh = ext, 1
    while sh < win:
        s = s + pltpu.roll(s, sh, 0)
        sh *= 2
    return s[HALO:] / cnt - zt


def _conv_taps(ext, cur, w_ref, lanes, lead):
    x1 = _shift_down(ext, 1, lead)
    x2 = _shift_down(ext, 2, lead)
    out = w_ref[2:3, lanes] * cur + w_ref[1:2, lanes] * x1 + w_ref[0:1, lanes] * x2
    return out, x1, x2


def fwd_mix(zs, x, pool_w, pool_scale, conv_w, wmix, tm, seq):
    t = x.shape[0]
    tps = seq // tm
    hb = tm // HALO

    def body(z0_ref, z1_ref, z2_ref, zh0_ref, zh1_ref, zh2_ref, x_ref, pw_ref, ps_ref, wpp_ref, cw_ref, wco_ref,
             wo_ref, x1_ref, yp_ref, yc_ref):
        z_refs, zh_refs = (z0_ref, z1_ref, z2_ref), (zh0_ref, zh1_ref, zh2_ref)
        it = pl.program_id(0) % tps
        keep_hist = jnp.where(it == 0, 0.0, 1.0)
        pos = it * tm + lax.broadcasted_iota(jnp.int32, (tm, 1), 0)
        p2 = []
        for g, win in enumerate(WINS):
            cnt = jnp.minimum(pos + 1, win).astype(F32)
            p = _pool_tile(_z_tile(z_refs, g), _z_tile(zh_refs, g), win, keep_hist, cnt)
            lanes = slice(g * CG, (g + 1) * CG)
            p2.append((_dot(p.astype(BF16), pw_ref[g]) * ps_ref[:, lanes]).astype(BF16))
        y_pool = _dot(jnp.concatenate(p2, axis=1), wpp_ref[...])
        u = []
        for q in range(NG):
            lanes = slice(q * CG, (q + 1) * CG)
            cv = _z_tile(z_refs, 8 + q).astype(F32) * _z_tile(z_refs, 12 + q).astype(F32)
            cvh = _z_tile(zh_refs, 8 + q).astype(F32) * _z_tile(zh_refs, 12 + q).astype(F32) * keep_hist
            cc, _, _ = _conv_taps(jnp.concatenate([cvh, cv], axis=0), cv, cw_ref, lanes, HALO)
            u.append((_z_tile(z_refs, 4 + q).astype(F32) * cc).astype(BF16))
        y_conv = _dot(jnp.concatenate(u, axis=1), wco_ref[...])
        ypb, ycb = y_pool.astype(BF16), y_conv.astype(BF16)
        yp_ref[...] = ypb
        yc_ref[...] = ycb
        merged = []
        for q in range(NG):
            lanes = slice(q * CG, (q + 1) * CG)
            sp = jax.nn.sigmoid(_z_tile(z_refs, 16 + q).astype(F32))
            sc = jax.nn.sigmoid(_z_tile(z_refs, 20 + q).astype(F32))
            merged.append((sp * ypb[:, lanes].astype(F32) + sc * ycb[:, lanes].astype(F32)).astype(BF16))
        x1_ref[...] = x_ref[...] + _dot(jnp.concatenate(merged, axis=1), wo_ref[...])

    def hist(i):
        return jnp.maximum(i * hb - 1, 0)

    const2 = lambda i: (0, 0)
    return pl.pallas_call(
        body, name="fwd_mix", grid=(t // tm,),
        in_specs=[pl.BlockSpec((NDEV, tm, CG), lambda i: (0, i, 0))] * 3
                 + [pl.BlockSpec((NDEV, HALO, CG), lambda i: (0, hist(i), 0))] * 3
                 + [pl.BlockSpec((tm, D), lambda i: (i, 0)),
                    pl.BlockSpec((NG, CG, CG), lambda i: (0, 0, 0)), pl.BlockSpec((1, D), const2),
                    pl.BlockSpec((D, D), lambda i: (0, MIX_POOL_PROJ)), pl.BlockSpec((3, D), const2),
                    pl.BlockSpec((D, D), lambda i: (0, MIX_CONV_OUT)), pl.BlockSpec((D, D), lambda i: (0, MIX_O))],
        out_specs=[pl.BlockSpec((tm, D), lambda i: (i, 0))] * 3,
        out_shape=[jax.ShapeDtypeStruct((t, D), F32), jax.ShapeDtypeStruct((t, D), BF16),
                   jax.ShapeDtypeStruct((t, D), BF16)],
        compiler_params=_cparams(1, VMEM_BIG),
    )(*zs, *zs, x, pool_w, pool_scale, wmix, conv_w, wmix, wmix)


def fwd_up(x1, g2, w_up_g, fcw, fcb, tm, seq):
    t = x1.shape[0]
    tps = seq // tm

    def body(x1_ref, g2_ref, wup_ref, fcw_ref, fcb_ref, up_ref, pre_ref, act_ref, actt_ref, h2t_ref, hist_ref):
        i = pl.program_id(0)
        keep_hist = jnp.where(i % tps == 0, 0.0, 1.0)

        @pl.when(i == 0)
        def _():
            hist_ref[...] = jnp.zeros_like(hist_ref)

        x1v = x1_ref[...]
        h2 = (x1v * _rms_inv(x1v) * g2_ref[...]).astype(BF16)
        h2t_ref[...] = h2.T
        lanes = slice(0, SH_UP)
        for c in range(NCH):
            conv = []
            for s in range(2):
                ub = _dot(h2, wup_ref[s, c]).astype(BF16)
                up_ref[s, c] = ub
                uf = ub.astype(F32)
                ext = jnp.concatenate([hist_ref[s, c] * keep_hist, uf], axis=0)
                hist_ref[s, c] = uf[tm - 8:]
                cc, _, _ = _conv_taps(ext, uf, fcw_ref.at[s, c], lanes, 8)
                conv.append(cc + fcb_ref[s, c])
                pre_ref[s, c] = conv[s].astype(BF16)
            a = (conv[0] * jax.nn.sigmoid(conv[0]) * conv[1]).astype(BF16)
            act_ref[c] = a
            actt_ref[c] = a.T

    tile = lambda i: (i, 0)
    const2 = lambda i: (0, 0)
    whole = lambda i: (0, 0, 0, 0)
    chunks = pl.BlockSpec((2, NCH, tm, SH_UP), lambda i: (0, 0, i, 0))
    return pl.pallas_call(
        body, name="fwd_up", grid=(t // tm,),
        in_specs=[pl.BlockSpec((tm, D), tile), pl.BlockSpec((1, D), const2),
                  pl.BlockSpec((2, NCH, D, SH_UP), whole), pl.BlockSpec((2, NCH, 3, SH_UP), whole),
                  pl.BlockSpec((2, NCH, 1, SH_UP), whole)],
        out_specs=[chunks, chunks, pl.BlockSpec((NCH, tm, SH_UP), lambda i: (0, i, 0)),
                   pl.BlockSpec((NCH, SH_UP, tm), lambda i: (0, 0, i)), pl.BlockSpec((D, tm), lambda i: (0, i))],
        out_shape=[jax.ShapeDtypeStruct((2, NCH, t, SH_UP), BF16), jax.ShapeDtypeStruct((2, NCH, t, SH_UP), BF16),
                   jax.ShapeDtypeStruct((NCH, t, SH_UP), BF16), jax.ShapeDtypeStruct((NCH, SH_UP, t), BF16),
                   jax.ShapeDtypeStruct((D, t), BF16)],
        scratch_shapes=[pltpu.VMEM((2, NCH, 8, SH_UP), F32)],
        compiler_params=_cparams(1, VMEM_BIG),
    )(x1, g2, w_up_g.reshape(2, NCH, D, SH_UP), fcw.reshape(2, NCH, 3, SH_UP), fcb.reshape(2, NCH, 1, SH_UP))


def fwd_down(x1, act, w_dn, gf, tgt, tm):
    t = x1.shape[0]

    def body(x1_ref, act_ref, wdn_ref, gf_ref, tgt_ref, dx2_ref, dx2b_ref, vec_ref):
        @pl.when(pl.program_id(0) == 0)
        def _():
            vec_ref[...] = jnp.zeros_like(vec_ref)

        half = tm // 2
        for r in range(2):
            rows = slice(r * half, (r + 1) * half)
            d = None
            for c in range(NCH):
                part = _dot(act_ref[c, rows, :], wdn_ref[c])
                d = part if d is None else d + part
            x2 = x1_ref[rows, :] + d
            inv3 = _rms_inv(x2)
            xn = x2 * inv3
            diff = xn * gf_ref[...] - tgt_ref[rows, :]
            dy = diff * (1.0 / D)
            vec_ref[0:1, :] += jnp.sum(dy * xn, axis=0, keepdims=True)
            vec_ref[1:2, :] += 0.5 * jnp.sum(jnp.mean(diff * diff, axis=-1))
            dx2 = _rms_bwd(dy, xn, inv3, gf_ref[...])
            dx2_ref[rows, :] = dx2
            dx2b_ref[rows, :] = dx2.astype(BF16)

    tile = lambda i: (i, 0)
    const2 = lambda i: (0, 0)
    return pl.pallas_call(
        body, name="fwd_down", grid=(t // tm,),
        in_specs=[pl.BlockSpec((tm, D), tile), pl.BlockSpec((NCH, tm, SH_UP), lambda i: (0, i, 0)),
                  pl.BlockSpec((NCH, SH_UP, D), lambda i: (0, 0, 0)), pl.BlockSpec((1, D), const2),
                  pl.BlockSpec((tm, D), tile)],
        out_specs=[pl.BlockSpec((tm, D), tile), pl.BlockSpec((tm, D), tile), pl.BlockSpec((8, D), const2)],
        out_shape=[jax.ShapeDtypeStruct((t, D), F32), jax.ShapeDtypeStruct((t, D), BF16),
                   jax.ShapeDtypeStruct((8, D), F32)],
        compiler_params=_cparams(1, VMEM_BIG),
    )(x1, act, w_dn, gf, tgt)


def bwd_ffn(dx2, x1, g2, up, pre, w_up_g, fcw, w_dn, tm, seq):
    t = x1.shape[0]
    nt = t // tm
    tps = seq // tm

    def body(dx2_ref, x1_ref, g2_ref, up_ref, pre_ref, wup_ref, fcw_ref, wdn_ref,
             dup_ref, dx1_ref, gvec_ref, gn_ref, carry_ref):
        i = pl.program_id(0)
        it = (nt - 1 - i) % tps
        keep_next = jnp.where(it == tps - 1, 0.0, 1.0)

        @pl.when(i == 0)
        def _():
            gvec_ref[...] = jnp.zeros_like(gvec_ref)
            gn_ref[...] = jnp.zeros_like(gn_ref)
            carry_ref[...] = jnp.zeros_like(carry_ref)

        dx2v = dx2_ref[...]
        dxb = dx2v.astype(BF16)
        lanes = slice(0, SH_UP)
        dh2 = None
        for c in range(NCH):
            pre = [pre_ref[s, c].astype(F32) for s in range(2)]
            sg = jax.nn.sigmoid(pre[0])
            dact = _dot_nt(dxb, wdn_ref[c])
            dpre = [dact * pre[1] * (sg * (1.0 + pre[0] * (1.0 - sg))), dact * (pre[0] * sg)]
            for s in range(2):
                dc = dpre[s]
                ext = jnp.concatenate([dc, carry_ref[s, c] * keep_next], axis=0)
                carry_ref[s, c] = dc[:8]
                shifted = (_shift_up(ext, 2, tm), _shift_up(ext, 1, tm), dc)
                uf = up_ref[s, c].astype(F32)
                gvec_ref[s, c, 0:1, lanes] += jnp.sum(dc, axis=0, keepdims=True)
                for tap in range(3):
                    gvec_ref[s, c, tap + 1:tap + 2, lanes] += jnp.sum(shifted[tap] * uf, axis=0, keepdims=True)
                w = fcw_ref.at[s, c]
                du = w[2:3, :] * dc + w[1:2, :] * shifted[1] + w[0:1, :] * shifted[0]
                dub = du.astype(BF16)
                dup_ref[s, c] = dub
                part = _dot_nt(dub, wup_ref[s, c])
                dh2 = part if dh2 is None else dh2 + part
        x1v = x1_ref[...]
        inv2 = _rms_inv(x1v)
        xn = x1v * inv2
        gn_ref[0:1, :] += jnp.sum(dh2 * xn, axis=0, keepdims=True)
        dx1_ref[...] = dx2v + _rms_bwd(dh2, xn, inv2, g2_ref[...])

    rev = lambda i: (nt - 1 - i, 0)
    const2 = lambda i: (0, 0)
    whole = lambda i: (0, 0, 0, 0)
    chunks = pl.BlockSpec((2, NCH, tm, SH_UP), lambda i: (0, 0, nt - 1 - i, 0))
    return pl.pallas_call(
        body, name="bwd_ffn", grid=(nt,),
        in_specs=[pl.BlockSpec((tm, D), rev), pl.BlockSpec((tm, D), rev), pl.BlockSpec((1, D), const2),
                  chunks, chunks, pl.BlockSpec((2, NCH, D, SH_UP), whole), pl.BlockSpec((2, NCH, 3, SH_UP), whole),
                  pl.BlockSpec((NCH, SH_UP, D), lambda i: (0, 0, 0))],
        out_specs=[chunks, pl.BlockSpec((tm, D), rev), pl.BlockSpec((2, NCH, 8, D), whole),
                   pl.BlockSpec((8, D), const2)],
        out_shape=[jax.ShapeDtypeStruct((2, NCH, t, SH_UP), BF16), jax.ShapeDtypeStruct((t, D), F32),
                   jax.ShapeDtypeStruct((2, NCH, 8, D), F32), jax.ShapeDtypeStruct((8, D), F32)],
        scratch_shapes=[pltpu.VMEM((2, NCH, 8, SH_UP), F32)],
        compiler_params=_cparams(1, VMEM_BIG),
    )(dx2, x1, g2, up, pre, w_up_g.reshape(2, NCH, D, SH_UP), fcw.reshape(2, NCH, 3, SH_UP), w_dn)


def bwd_mix(dx1, zs, y_pool, y_conv, pool_w, pool_scale, conv_w, wmix, tm, seq):
    t = dx1.shape[0]
    nt = t // tm
    tps = seq // tm
    hb = tm // HALO

    def body(da_ref, z0_ref, z1_ref, z2_ref, zh0_ref, zh1_ref, zh2_ref, yp_ref, yc_ref, pw_ref, ps_ref, wpp_ref,
             cw_ref, wco_ref, wo_ref,
             dz_ref, mg_ref, p2_ref, u_ref, dyp_ref, dyc_ref, gpool_ref, gvec_ref, cp_ref, cc_ref):
        z_refs, zh_refs = (z0_ref, z1_ref, z2_ref), (zh0_ref, zh1_ref, zh2_ref)
        i = pl.program_id(0)
        it = (nt - 1 - i) % tps
        keep_hist = jnp.where(it == 0, 0.0, 1.0)
        keep_next = jnp.where(it == tps - 1, 0.0, 1.0)
        pos = it * tm + lax.broadcasted_iota(jnp.int32, (tm, 1), 0)

        @pl.when(i == 0)
        def _():
            gvec_ref[...] = jnp.zeros_like(gvec_ref)
            gpool_ref[...] = jnp.zeros_like(gpool_ref)
            cp_ref[...] = jnp.zeros_like(cp_ref)
            cc_ref[...] = jnp.zeros_like(cc_ref)

        dm =_dot_nt(da_ref[...].astype(BF16), wo_ref[...])
        merged, dyp, dyc = [], [], []
        for q in range(NG):
            lanes = slice(q * CG, (q + 1) * CG)
            sp = jax.nn.sigmoid(_z_tile(z_refs, 16 + q).astype(F32))
            sc = jax.nn.sigmoid(_z_tile(z_refs, 20 + q).astype(F32))
            yp = yp_ref[:, lanes].astype(F32)
            yc = yc_ref[:, lanes].astype(F32)
            dmq = dm[:, lanes]
            merged.append((sp * yp + sc * yc).astype(BF16))
            dyp.append((dmq * sp).astype(BF16))
            dyc.append((dmq * sc).astype(BF16))
            dz_ref[16 + q] = (dmq * yp * (sp * (1.0 - sp))).astype(BF16)
            dz_ref[20 + q] = (dmq * yc * (sc * (1.0 - sc))).astype(BF16)
        mg_ref[...] = jnp.concatenate(merged, axis=1)
        dypb = jnp.concatenate(dyp, axis=1)
        dycb = jnp.concatenate(dyc, axis=1)
        dyp_ref[...] = dypb
        dyc_ref[...] = dycb

        dp2 = _dot_nt(dypb, wpp_ref[...])
        p2 = []
        for g, win in enumerate(WINS):
            lanes = slice(g * CG, (g + 1) * CG)
            cnt = jnp.minimum(pos + 1, win).astype(F32)
            p = _pool_tile(_z_tile(z_refs, g), _z_tile(zh_refs, g), win, keep_hist, cnt)
            pb = p.astype(BF16)
            pw = _dot(pb, pw_ref[g])
            p2.append((pw * ps_ref[:, lanes]).astype(BF16))
            dp2g = dp2[:, lanes]
            gvec_ref[0:1, lanes] += jnp.sum(dp2g * pw, axis=0, keepdims=True)
            dpwb = (dp2g * ps_ref[:, lanes]).astype(BF16)
            gpool_ref[g] += _dot_tn(pb, dpwb)
            dp = _dot_nt(dpwb, pw_ref[g])
            qv = dp / cnt
            ext = jnp.concatenate([qv, cp_ref[g] * keep_next], axis=0)
            cp_ref[g] = qv[:HALO]
            n = tm + HALO
            s, sh = ext, 1
            while sh < win:
                s = s + pltpu.roll(s, n - sh, 0)
                sh *= 2
            dz_ref[g] = (s[:tm] - dp).astype(BF16)
        p2_ref[...] = jnp.concatenate(p2, axis=1)

        du = _dot_nt(dycb, wco_ref[...])
        u = []
        for q in range(NG):
            lanes = slice(q * CG, (q + 1) * CG)
            zb = _z_tile(z_refs, 4 + q).astype(F32)
            zc = _z_tile(z_refs, 8 + q).astype(F32)
            zv = _z_tile(z_refs, 12 + q).astype(F32)
            cv = zc * zv
            cvh = _z_tile(zh_refs, 8 + q).astype(F32) * _z_tile(zh_refs, 12 + q).astype(F32) * keep_hist
            cc, cv1, cv2 = _conv_taps(jnp.concatenate([cvh, cv], axis=0), cv, cw_ref, lanes, HALO)
            u.append((zb * cc).astype(BF16))
            duq = du[:, lanes]
            dz_ref[4 + q] = (duq * cc).astype(BF16)
            dcc = duq * zb
            for tap, src in enumerate((cv2, cv1, cv)):
                gvec_ref[tap + 1:tap + 2, lanes] += jnp.sum(dcc * src, axis=0, keepdims=True)
            ext = jnp.concatenate([dcc, cc_ref[:, lanes] * keep_next], axis=0)
            cc_ref[:, lanes] = dcc[:8]
            dcv = (cw_ref[2:3, lanes] * dcc + cw_ref[1:2, lanes] * _shift_up(ext, 1, tm)
                   + cw_ref[0:1, lanes] * _shift_up(ext, 2, tm))
            dz_ref[8 + q] = (dcv * zv).astype(BF16)
            dz_ref[12 + q] = (dcv * zc).astype(BF16)
        u_ref[...] = jnp.concatenate(u, axis=1)

    def hist(i):
        return jnp.maximum((nt - 1 - i) * hb - 1, 0)

    rev = lambda i: (nt - 1 - i, 0)
    rev3 = lambda i: (0, nt - 1 - i, 0)
    const2 = lambda i: (0, 0)
    tok = jax.ShapeDtypeStruct((t, D), BF16)
    return pl.pallas_call(
        body, name="bwd_mix", grid=(nt,),
        in_specs=[pl.BlockSpec((tm, D), rev)] + [pl.BlockSpec((NDEV, tm, CG), rev3)] * 3
                 + [pl.BlockSpec((NDEV, HALO, CG), lambda i: (0, hist(i), 0))] * 3
                 + [pl.BlockSpec((tm, D), rev), pl.BlockSpec((tm, D), rev),
                    pl.BlockSpec((NG, CG, CG), lambda i: (0, 0, 0)), pl.BlockSpec((1, D), const2),
                    pl.BlockSpec((D, D), lambda i: (0, MIX_POOL_PROJ)), pl.BlockSpec((3, D), const2),
                    pl.BlockSpec((D, D), lambda i: (0, MIX_CONV_OUT)), pl.BlockSpec((D, D), lambda i: (0, MIX_O))],
        out_specs=[pl.BlockSpec((NZT, tm, CG), rev3)] + [pl.BlockSpec((tm, D), rev)] * 5
                  + [pl.BlockSpec((NG, CG, CG), lambda i: (0, 0, 0)), pl.BlockSpec((8, D), const2)],
        out_shape=[jax.ShapeDtypeStruct((NZT, t, CG), BF16), tok, tok, tok, tok, tok,
                   jax.ShapeDtypeStruct((NG, CG, CG), F32), jax.ShapeDtypeStruct((8, D), F32)],
        scratch_shapes=[pltpu.VMEM((NG, HALO, CG), F32), pltpu.VMEM((8, D), F32)],
        compiler_params=_cparams(1, VMEM_BIG),
    )(dx1, *zs, *zs, y_pool, y_conv, pool_w, pool_scale, wmix, conv_w, wmix, wmix)


def bwd_in(dz, w_in_pieces, dx1, x, g1, tm):
    t = x.shape[0]

    def body(dz_ref, w0_ref, w1_ref, w2_ref, dx1_ref, x_ref, g_ref, gx_ref, gn_ref):
        @pl.when(pl.program_id(0) == 0)
        def _():
            gn_ref[...] = jnp.zeros_like(gn_ref)

        dh = None
        for j in range(NDEV):
            for q, w_ref in enumerate((w0_ref, w1_ref, w2_ref)):
                part = _dot_nt(dz_ref[3 * j + q], w_ref[j])
                dh = part if dh is None else dh + part
        xv = x_ref[...]
        inv = _rms_inv(xv)
        xn = xv * inv
        gn_ref[0:1, :] += jnp.sum(dh * xn, axis=0, keepdims=True)
        gx_ref[...] = dx1_ref[...] + _rms_bwd(dh, xn, inv, g_ref[...])

    tile = lambda i: (i, 0)
    return pl.pallas_call(
        body, name="bwd_in", grid=(t // tm,),
        in_specs=[pl.BlockSpec((NZT, tm, CG), lambda i: (0, i, 0))]
                 + [pl.BlockSpec((NDEV, D, CG), lambda i: (0, 0, 0))] * 3
                 + [pl.BlockSpec((tm, D), tile), pl.BlockSpec((tm, D), tile), pl.BlockSpec((1, D), lambda i: (0, 0))],
        out_specs=[pl.BlockSpec((tm, D), tile), pl.BlockSpec((8, D), lambda i: (0, 0))],
        out_shape=[jax.ShapeDtypeStruct((t, D), F32), jax.ShapeDtypeStruct((8, D), F32)],
        compiler_params=_cparams(1, VMEM_BIG),
    )(dz, *w_in_pieces, dx1, x, g1)


def _slot(j):
    return j % 2, j // 2


def wgrad_cols(at, b, q, name):
    m, t = at.shape
    width = b.shape[3]

    def body(a_ref, b_ref, o_ref):
        o_ref[...] = _dot(a_ref[...], b_ref[...])

    return pl.pallas_call(
        body, name=name, grid=(NDEV,),
        in_specs=[pl.BlockSpec((m, t), lambda j: (0, 0)),
                  pl.BlockSpec((None, None, t, width), lambda j: (j, q, 0, 0))],
        out_specs=pl.BlockSpec((None, None, m, width), lambda j: (j % 2, j // 2, 0, 0)),
        out_shape=jax.ShapeDtypeStruct((2, 4, m, width), F32),
        compiler_params=_cparams(1, VMEM_BIG),
    )(at, b)


def wgrad_down(actt, dx2b):
    t = dx2b.shape[0]

    def body(a_ref, b_ref, o_ref):
        r = _dot(a_ref[...], b_ref[...])
        o_ref[0] = r[:SH_DN]
        o_ref[1] = r[SH_DN:]

    return pl.pallas_call(
        body, name="wgrad_down", grid=(NCH,),
        in_specs=[pl.BlockSpec((None, SH_UP, t), lambda k: (k, 0, 0)), pl.BlockSpec((t, D), lambda k: (0, 0))],
        out_specs=pl.BlockSpec((2, None, SH_DN, D), lambda k: (0, k, 0, 0)),
        out_shape=jax.ShapeDtypeStruct((2, 4, SH_DN, D), F32),
        compiler_params=_cparams(1, VMEM_BIG),
    )(actt, dx2b)


def wgrad_square(a, b, name, tk):
    t = a.shape[0]

    def body(a_ref, b_ref, o_ref, acc_ref):
        kt = pl.program_id(0)

        @pl.when(kt == 0)
        def _():
            acc_ref[...] = jnp.zeros_like(acc_ref)

        acc_ref[...] += _dot_tn(a_ref[...], b_ref[...].astype(BF16))

        @pl.when(kt == pl.num_programs(0) - 1)
        def _():
            for j in range(NDEV):
                cc, xy = _slot(j)
                o_ref[cc, xy] = acc_ref[j * 128:(j + 1) * 128]

    return pl.pallas_call(
        body, name=name, grid=(t // tk,),
        in_specs=[pl.BlockSpec((tk, D), lambda k: (k, 0)), pl.BlockSpec((tk, D), lambda k: (k, 0))],
        out_specs=pl.BlockSpec((2, 4, 128, D), lambda k: (0, 0, 0, 0)),
        out_shape=jax.ShapeDtypeStruct((2, 4, 128, D), F32),
        scratch_shapes=[pltpu.VMEM((D, D), F32)],
        compiler_params=_cparams(1, VMEM_BIG),
    )(a, b)


def _adamw(w, g, m, v):
    m = ADAM_B1 * m + (1.0 - ADAM_B1) * g
    v = ADAM_B2 * v + (1.0 - ADAM_B2) * (g * g)
    m_hat = m / (1.0 - ADAM_B1 ** ADAM_STEP)
    v_hat = v / (1.0 - ADAM_B2 ** ADAM_STEP)
    delta = -ADAM_LR * (m_hat / (jnp.sqrt(v_hat) + ADAM_EPS) + ADAM_WD * w)
    return delta, m, v


def _row_block(r):
    return 512 if r % 512 == 0 else r


def chip_partial(place, gs, from_sibling, name):
    n = len(gs)
    shapes = [g.shape[2:] for g in gs]

    def body(place_ref, *refs):
        for g_ref, s_ref, o_ref in zip(refs[:n], refs[n:2 * n], refs[2 * n:]):
            o_ref[...] = (g_ref[...] + s_ref[...]).astype(BF16)

    def slot(rc):
        return pl.BlockSpec((None,) + rc, lambda k, pr: (pr[1] ^ (k + 1), 0, 0))

    return pl.pallas_call(
        body, name=name,
        grid_spec=pltpu.PrefetchScalarGridSpec(
            num_scalar_prefetch=1, grid=(3,),
            in_specs=[pl.BlockSpec((None, None) + rc, lambda k, pr: (pr[0], pr[1] ^ (k + 1), 0, 0)) for rc in shapes]
                     + [slot(rc) for rc in shapes],
            out_specs=[slot(rc) for rc in shapes]),
        out_shape=[jax.ShapeDtypeStruct((4,) + rc, BF16) for rc in shapes],
        compiler_params=_cparams(1, VMEM_BIG),
    )(place, *gs, *from_sibling)


def finish_adamw(place, gs, from_sibling, from_chips, w, m, v, name, transposed=False):
    n = len(gs)
    r = gs[0].shape[2]
    widths = [g.shape[3] for g in gs]
    c = sum(widths)
    br = _row_block(r)

    def body(place_ref, *refs):
        g_refs, s_refs, c_refs = refs[:n], refs[n:2 * n], refs[2 * n:5 * n]
        w_ref, m_ref, v_ref, og_ref, od_ref, om_ref, ov_ref = refs[5 * n:]
        cols = []
        for q in range(n):
            grad = g_refs[q][...] + s_refs[q][...]
            for k in range(3):
                grad = grad + c_refs[3 * q + k][...].astype(F32)
            cols.append(grad)
        grad = cols[0] if n == 1 else jnp.concatenate(cols, axis=1)
        if transposed:
            grad = grad.T
        og_ref[...] = grad
        od_ref[...], om_ref[...], ov_ref[...] = _adamw(w_ref[...], grad, m_ref[...], v_ref[...])

    def other(k, cq):
        return pl.BlockSpec((None, br, cq), lambda i, pr: (pr[1] ^ k, i, 0))

    row = pl.BlockSpec((c, br), lambda i, pr: (0, i)) if transposed else pl.BlockSpec((br, c), lambda i, pr: (i, 0))
    out = jax.ShapeDtypeStruct((c, r) if transposed else (r, c), F32)
    in_specs = [pl.BlockSpec((None, None, br, cq), lambda i, pr: (pr[0], pr[1], i, 0)) for cq in widths]
    in_specs += [pl.BlockSpec((None, br, cq), lambda i, pr: (pr[1], i, 0)) for cq in widths]
    in_specs += [other(k, cq) for cq in widths for k in (1, 2, 3)]
    return pl.pallas_call(
        body, name=name,
        grid_spec=pltpu.PrefetchScalarGridSpec(
            num_scalar_prefetch=1, grid=(r // br,), in_specs=in_specs + [row, row, row], out_specs=[row] * 4),
        out_shape=[out] * 4,
        compiler_params=_cparams(1, VMEM_BIG),
    )(place, *gs, *from_sibling, *[fc for fc in from_chips for _ in range(3)], w, m, v)


def adamw_small(items):
    n = len(items)

    def body(*refs):
        ins, outs = refs[:4 * n], refs[4 * n:]
        for i in range(n):
            w, g, m, v = (r[...] for r in ins[4 * i:4 * i + 4])
            outs[3 * i][...], outs[3 * i + 1][...], outs[3 * i + 2][...] = _adamw(w, g, m, v)

    out = [jax.ShapeDtypeStruct(it[0].shape, F32) for it in items for _ in range(3)]
    res = pl.pallas_call(body, name="adamw_small", out_shape=out)(*[a for it in items for a in it])
    return [res[3 * i:3 * i + 3] for i in range(n)]


def kernel(x, norm_mix, w_in, pool_w, pool_scale, w_pool_proj, conv_w, w_conv_out, w_o, norm_ffn, w_up, ffn_conv_w, ffn_conv_b, w_down, norm_final, loss_target, m_norm_mix, m_w_in, m_pool_w, m_pool_scale, m_w_pool_proj, m_conv_w, m_w_conv_out, m_w_o, m_norm_ffn, m_w_up, m_ffn_conv_w, m_ffn_conv_b, m_w_down, m_norm_final, v_norm_mix, v_w_in, v_pool_w, v_pool_scale, v_w_pool_proj, v_conv_w, v_w_conv_out, v_w_o, v_norm_ffn, v_w_up, v_ffn_conv_w, v_ffn_conv_b, v_w_down, v_norm_final):
    nb, seq, _ = x.shape
    t = nb * seq
    tm_in = min(TM_IN, t)
    tm_mix = min(TM_MIX, seq)
    tm_ffn = min(TM_FFN, seq)
    tk = min(TK_WGRAD, t)
    xt = x.reshape(t, D)
    tgt = loss_target.reshape(t, D)
    xi, yi, ci = _pos()
    me = 4 * xi + 2 * yi + ci
    place = jnp.stack([ci, 2 * xi + yi]).astype(jnp.int32)

    tie = lax.optimization_barrier
    w_in_b = w_in[0].astype(BF16)
    w_in_g = [all_gather_blocks([w_in_b[:, q * CG:(q + 1) * CG]], f"all_gather_w_in_{q}", 0)[0] for q in range(3)]
    taps = (jnp.pad(conv_w[0], ((0, 5), (0, D - 128))) + jnp.pad(ffn_conv_w[0], ((3, 2), (0, D - SH_UP))))
    taps_g = _exchange_small(taps, "all_gather_taps")
    mix_shard = jnp.concatenate(
        [w_pool_proj[0], w_conv_out[0], w_o[0], pool_w[0].reshape(NG * 32, CG)], axis=1).astype(BF16)
    conv_w_f = taps_g[:, 0:3, :128].transpose(1, 0, 2).reshape(3, D)
    fcw_f = taps_g[:, 3:6, :SH_UP]
    fcb_f = ffn_conv_b.reshape(NDEV, 1, SH_UP)
    mix_shard, conv_w_f, fcw_f, fcb_f = tie((mix_shard, conv_w_f, fcw_f, fcb_f))
    wmix_g, = all_gather_blocks([mix_shard], "all_gather_w_mix", 0)
    ffn_shards, w_in_g[0] = tie(([w_up[0].astype(BF16), w_down[0].astype(BF16)], w_in_g[0]))
    w_up_g, = all_gather_blocks(ffn_shards[:1], "all_gather_w_up", 0)
    w_dn_g, = all_gather_blocks(ffn_shards[1:], "all_gather_w_down", 0)
    w_dn_f = w_dn_g.reshape(NCH, SH_UP, D)
    gfin = norm_final.reshape(1, D)

    zs, h1 = fwd_in(xt, norm_mix, w_in_g, tm_in)
    wmix_g, zs = tie((wmix_g, zs))
    wmix = wmix_g.reshape(D, MIX_COLS)
    pool_w_f = wmix_g[:, :, 3 * D:].reshape(NDEV, NG, 32, CG).transpose(1, 0, 2, 3).reshape(NG, CG, CG)
    x1, y_pool, y_conv = fwd_mix(zs, xt, pool_w_f, pool_scale, conv_w_f, wmix, tm_mix, seq)
    up, pre, act_tok, act, h2 = fwd_up(x1, norm_ffn, w_up_g, fcw_f, fcb_f, tm_ffn, seq)
    dx2, dx2b, ffn_vec = fwd_down(x1, act_tok, w_dn_f, gfin, tgt, min(TM_IN, t))

    def to_sibling(full, tag):
        return reduce_scatter_d2d(full, "reduce_scatter_d2d_" + tag, 1)

    def partials(full, from_sib, tag):
        return chip_partial(place, full, from_sib, "chip_partial_" + tag)

    def to_chips(parts, tag):
        return reduce_scatter_ici(parts, "reduce_scatter_ici_" + tag, 2)

    def finish(nm, gs, from_sib, from_chips, wmv, transposed=False):
        rc = (gs[0].shape[2], sum(g.shape[3] for g in gs))
        wmv2 = [a.reshape(rc).T if transposed else a.reshape(rc) for a in wmv]
        outs = finish_adamw(place, gs, from_sib, from_chips, *wmv2, "adamw_" + nm, transposed)
        return [(o.T if transposed else o).reshape(wmv[0].shape) for o in outs]

    def after(x, dep):
        return tie((x, dep))[0]

    big = {}
    d_up, dx1, g_ffn_vec, g_nffn = bwd_ffn(dx2, x1, norm_ffn, up, pre, w_up_g, fcw_f, w_dn_f, tm_ffn, seq)
    gw_up = wgrad_cols(h2, d_up.reshape(NDEV, 1, t, SH_UP), 0, "wgrad_up")
    sib_up = to_sibling([gw_up], "w_up")
    gw_dn = wgrad_down(act, after(dx2b, gw_up))
    sib_dn = to_sibling([after(gw_dn, sib_up)], "w_down")
    dx1, part_up = tie((dx1, partials([gw_up], sib_up, "w_up")))
    chips_up = to_chips(part_up, "w_up")
    dz, merged, p2, u, dyp, dyc, g_pool, g_mix_vec = bwd_mix(
        dx1, zs, y_pool, y_conv, pool_w_f, pool_scale, conv_w_f, wmix, tm_mix, seq)
    merged, part_dn = tie((merged, partials([gw_dn], sib_dn, "w_down")))
    chips_dn = to_chips(part_dn, "w_down")
    gw_o = wgrad_square(merged, dx1, "wgrad_o", tk)
    gw_pp = wgrad_square(p2, dyp, "wgrad_pool_proj", tk)
    gw_co = wgrad_square(u, dyc, "wgrad_conv_out", tk)
    gw_pool = g_pool.reshape(NG, 4, 2, 32, CG).transpose(2, 1, 0, 3, 4).reshape(2, 4, NG * 32, CG)
    dz8 = dz.reshape(NDEV, 3, t, CG)
    gw_in, sib_in, chips_in = [None] * 3, [None] * 3, [None] * 3
    sib_a = to_sibling(after([gw_o, gw_pp], (chips_up, gw_pool, chips_dn)), "mix_a")
    sib_b = to_sibling(after([gw_co, gw_pool], sib_a), "mix_b")
    gw_in[0] = wgrad_cols(h1, dz8, 0, "wgrad_in_0")
    h1, part_a, part_b = tie((h1, partials([gw_o, gw_pp], sib_a, "mix_a"),
                              partials([gw_co, gw_pool], sib_b, "mix_b")))
    chips_a = to_chips(after(part_a, chips_dn), "mix_a")
    chips_b = to_chips(part_b, "mix_b")
    sib_in[0] = to_sibling(after([gw_in[0]], sib_b), "w_in_0")
    gw_in[1] = wgrad_cols(h1, dz8, 1, "wgrad_in_1")
    h1, part_in0, gw_in[1] = tie((h1, partials([gw_in[0]], sib_in[0], "w_in_0"), gw_in[1]))
    chips_in[0] = to_chips(part_in0, "w_in_0")
    sib_in[1] = to_sibling(after([gw_in[1]], sib_in[0]), "w_in_1")
    h1, big["w_down"], big["w_up"] = tie((
        h1, finish("w_down", [gw_dn], sib_dn, chips_dn, (w_down, m_w_down, v_w_down)),
        finish("w_up", [gw_up], sib_up, chips_up, (w_up, m_w_up, v_w_up), transposed=True)))
    gw_in[2] = wgrad_cols(h1, dz8, 2, "wgrad_in_2")
    sib_in[2] = to_sibling(after([gw_in[2]], (chips_a, chips_b, chips_in[0])), "w_in_2")
    sib_in[2], big["w_o"], big["w_pool_proj"], big["w_conv_out"], big["pool_w"] = tie((
        sib_in[2],
        finish("w_o", [gw_o], sib_a[:1], chips_a[:1], (w_o, m_w_o, v_w_o)),
        finish("w_pool_proj", [gw_pp], sib_a[1:], chips_a[1:], (w_pool_proj, m_w_pool_proj, v_w_pool_proj)),
        finish("w_conv_out", [gw_co], sib_b[:1], chips_b[:1], (w_conv_out, m_w_conv_out, v_w_conv_out)),
        finish("pool_w", [gw_pool], sib_b[1:], chips_b[1:], (pool_w, m_pool_w, v_pool_w))))
    dx1, part_in12 = tie((dx1, partials(gw_in[1:], sib_in[1] + sib_in[2], "w_in_12")))
    chips_in[1] = to_chips(part_in12[:1], "w_in_1")
    chips_in[2] = to_chips(part_in12[1:], "w_in_2")
    small_g, = all_gather_blocks(
        [after(jnp.concatenate([g_mix_vec, g_nffn, ffn_vec, g_ffn_vec.reshape(8 * NDEV, D)], axis=0), sib_in[2])],
        "all_gather_small", 0)
    grad_x, g_nmix = bwd_in(dz, w_in_g, dx1, xt, norm_mix, min(TM_BWD_IN, t))
    grad_x, chips_in = tie((grad_x, chips_in))
    nmix_g, = all_gather_blocks([g_nmix], "all_gather_norm_mix", 0)
    big["w_in"] = finish("w_in", gw_in, [s[0] for s in sib_in], [c[0] for c in chips_in], (w_in, m_w_in, v_w_in))

    red, = sum_blocks([small_g], "sum_small")
    red_n, = sum_blocks([after(nmix_g, (big["w_in"], red))], "sum_norm_mix")
    g_norm_mix, g_pool_scale, g_norm_ffn = red_n[0:1], red[0:1], red[8:9]
    g_conv_w = lax.dynamic_slice(red, (1, me * 128), (3, 128))
    g_norm_final = red[16]
    loss = red[17, 0]
    g_fcb = red[24:].reshape(NDEV, 8, D)[:, 0, :SH_UP].reshape(1, FF2)
    g_fcw = lax.dynamic_slice(red, (25 + 8 * me, 0), (3, SH_UP))
    grads = {"norm_mix": g_norm_mix, "pool_scale": g_pool_scale, "norm_ffn": g_norm_ffn, "norm_final": g_norm_final,
             "ffn_conv_b": g_fcb, "conv_w": g_conv_w.reshape(1, 3, 128), "ffn_conv_w": g_fcw.reshape(1, 3, SH_UP)}
    small_wmv = {"norm_mix": (norm_mix, m_norm_mix, v_norm_mix), "pool_scale": (pool_scale, m_pool_scale, v_pool_scale),
                 "norm_ffn": (norm_ffn, m_norm_ffn, v_norm_ffn), "norm_final": (norm_final, m_norm_final, v_norm_final),
                 "ffn_conv_b": (ffn_conv_b, m_ffn_conv_b, v_ffn_conv_b), "conv_w": (conv_w, m_conv_w, v_conv_w),
                 "ffn_conv_w": (ffn_conv_w, m_ffn_conv_w, v_ffn_conv_w)}
    small_names = list(small_wmv)
    flat2 = lambda a: a.reshape(1, -1) if a.ndim == 1 else (a.transpose(1, 0, 2) if a.ndim == 3 else a)
    unflat = lambda o, like: o.transpose(1, 0, 2) if like.ndim == 3 else o.reshape(like.shape)
    small_out = adamw_small([(flat2(small_wmv[nm][0]), flat2(grads[nm]), flat2(small_wmv[nm][1]),
                              flat2(small_wmv[nm][2])) for nm in small_names])
    small = {nm: [unflat(o, small_wmv[nm][0]) for o in outs] for nm, outs in zip(small_names, small_out)}

    order = ["norm_mix", "w_in", "pool_w", "pool_scale", "w_pool_proj", "conv_w", "w_conv_out", "w_o", "norm_ffn",
             "w_up", "ffn_conv_w", "ffn_conv_b", "w_down", "norm_final"]
    out = [loss, grad_x.reshape(nb, seq, D)]
    out += [big[nm][0] if nm in big else grads[nm] for nm in order]
    for idx in range(3):
        out += [big[nm][idx + 1] if nm in big else small[nm][idx] for nm in order]
    return tuple(out)
```
